```python
import jax, jax.numpy as jnp
from jax import lax
import numpy as np

D_MODEL = 4096
BATCH = 8
SEQ = 4096
DEPTH = 1

D_MIX = D_MODEL
RWKV_W = D_MIX // 2
CONV_W = D_MIX - RWKV_W
HEAD_SIZE = 64
N_HEADS = RWKV_W // HEAD_SIZE
LORA_W = 96
LORA_A = 96
CONV_K = 31
NORM_EPS = 1e-6
LN_EPS = 1e-5
GN_EPS = 1e-5 * HEAD_SIZE
SHIFT_COLS = 3 * RWKV_W + LORA_W + LORA_A
IN_COLS = SHIFT_COLS + RWKV_W + 2 * CONV_W + CONV_W

kernel_name = "hybrid_rwkv7_conformer_parallel"


def rms_norm(x, g):
    xf = x.astype(jnp.float32)
    y = xf * lax.rsqrt(jnp.mean(xf * xf, axis=-1, keepdims=True) + NORM_EPS)
    return (y * g.astype(jnp.float32)).astype(x.dtype)


def layer_norm(x, g, b):
    xf = x.astype(jnp.float32)
    mu = jnp.mean(xf, axis=-1, keepdims=True)
    var = jnp.mean(jnp.square(xf - mu), axis=-1, keepdims=True)
    y = (xf - mu) * lax.rsqrt(var + LN_EPS)
    return (y * g.astype(jnp.float32) + b.astype(jnp.float32)).astype(x.dtype)


def rwkv7_recurrence(r, w, k, v, kk, a):
    B, T, H, N = r.shape
    tm = lambda z: jnp.moveaxis(z, 1, 0)
    seq = (tm(r), tm(w), tm(k), tm(v), tm(kk), tm(kk * a))

    def step(S, inp):
        r_t, w_t, k_t, v_t, kk_t, b_t = inp
        sa = jnp.einsum('bhij,bhj->bhi', S, -kk_t)
        S = S * w_t[:, :, None, :] + sa[..., None] * b_t[:, :, None, :] + v_t[..., None] * k_t[:, :, None, :]
        y_t = jnp.einsum('bhij,bhj->bhi', S, r_t)
        return S, y_t

    S0 = jnp.zeros((B, H, N, N), jnp.float32)
    _, ys = lax.scan(step, S0, seq)
    return jnp.moveaxis(ys, 0, 1)


def _fwd_setup_inputs(seed: int = 0) -> dict:
    key = jax.random.key(seed)
    ks = jax.random.split(key, 24)
    f32 = jnp.float32
    nrm = lambda k, s, sc: jax.random.normal(k, s, f32) * sc
    return {
        "x": nrm(ks[0], (BATCH, SEQ, D_MODEL), 1.0),
        "norm_pre_g": 1.0 + nrm(ks[1], (D_MODEL,), 0.02),
        "w_in": nrm(ks[2], (D_MODEL, IN_COLS), D_MODEL ** -0.5),
        "mu_shift": jax.random.uniform(ks[3], (SHIFT_COLS,), f32, 0.0, 1.0),
        "w0": jax.random.uniform(ks[4], (RWKV_W,), f32, -5.5, -0.5),
        "w_lora_up": nrm(ks[5], (LORA_W, RWKV_W), 0.5 * LORA_W ** -0.5),
        "a0": nrm(ks[6], (RWKV_W,), 0.1),
        "a_lora_up": nrm(ks[7], (LORA_A, RWKV_W), 0.5 * LORA_A ** -0.5),
        "k_k": 0.85 + nrm(ks[8], (RWKV_W,), 0.02),
        "k_a": 1.0 + nrm(ks[9], (RWKV_W,), 0.02),
        "r_k": nrm(ks[10], (N_HEADS, HEAD_SIZE), 0.1),
        "lnx_g": 1.0 + nrm(ks[11], (RWKV_W,), 0.02),
        "lnx_b": nrm(ks[12], (RWKV_W,), 0.01),
        "conv_w": nrm(ks[13], (CONV_K, CONV_W), CONV_K ** -0.5),
        "conv_b": nrm(ks[14], (CONV_W,), 0.01),
        "cln_g": 1.0 + nrm(ks[15], (CONV_W,), 0.02),
        "cln_b": nrm(ks[16], (CONV_W,), 0.01),
        "w_pw2": nrm(ks[17], (CONV_W, CONV_W), CONV_W ** -0.5),
        "b_pw2": nrm(ks[18], (CONV_W,), 0.01),
        "w_out": nrm(ks[19], (D_MIX, D_MODEL), D_MIX ** -0.5),
        "norm_post_g": 1.0 + nrm(ks[20], (D_MODEL,), 0.02),
    }


def hybrid_layer(x, norm_pre_g, w_in, mu_shift, w0, w_lora_up, a0, a_lora_up, k_k, k_a, r_k,
                 lnx_g, lnx_b, conv_w, conv_b, cln_g, cln_b, w_pw2, b_pw2, w_out, norm_post_g):
    B, T, _ = x.shape
    f32 = jnp.float32
    h = rms_norm(x, norm_pre_g)
    proj = jnp.einsum('btd,dc->btc', h, w_in)
    c0 = SHIFT_COLS
    c1 = c0 + RWKV_W
    c2 = c1 + CONV_W
    c3 = c2 + CONV_W
    rwkv_in, g_rwkv, glu_v, glu_g, g_conv = jnp.split(proj, [c0, c1, c2, c3], axis=-1)

    prev = jnp.pad(rwkv_in, ((0, 0), (1, 0), (0, 0)))[:, :-1]
    xs = rwkv_in + (prev - rwkv_in) * mu_shift
    r, k, v, w_low, a_low = jnp.split(xs, [RWKV_W, 2 * RWKV_W, 3 * RWKV_W, 3 * RWKV_W + LORA_W], axis=-1)
    w_log = -jax.nn.softplus(-(w0 + jnp.tanh(w_low) @ w_lora_up).astype(f32)) - 0.5
    decay = jnp.exp(-jnp.exp(w_log))
    a = jax.nn.sigmoid((a0 + a_low @ a_lora_up).astype(f32))
    hs = lambda z: z.astype(f32).reshape(B, T, N_HEADS, HEAD_SIZE)
    r_h, k_f, v_h, a_h, w_h = hs(r), hs(k), hs(v), hs(a), hs(decay)
    k_k_h = k_k.astype(f32).reshape(N_HEADS, HEAD_SIZE)
    k_a_h = k_a.astype(f32).reshape(N_HEADS, HEAD_SIZE)
    kk = k_f * k_k_h
    kk = kk / jnp.maximum(jnp.linalg.norm(kk, axis=-1, keepdims=True), 1e-12)
    k_h = k_f * (1.0 + (a_h - 1.0) * k_a_h)
    y = rwkv7_recurrence(r_h, w_h, k_h, v_h, kk, a_h)
    mu = jnp.mean(y, axis=-1, keepdims=True)
    var = jnp.mean(jnp.square(y - mu), axis=-1, keepdims=True)
    y = ((y - mu) * lax.rsqrt(var + GN_EPS)).reshape(B, T, RWKV_W)
    y = y * lnx_g.astype(f32) + lnx_b.astype(f32)
    bonus = jnp.sum(r_h * k_h * r_k.astype(f32), axis=-1, keepdims=True) * v_h
    y = (y + bonus.reshape(B, T, RWKV_W)).astype(x.dtype)
    y_rwkv = y * jax.nn.silu(g_rwkv)

    u = glu_v * jax.nn.sigmoid(glu_g)
    u_pad = jnp.pad(u, ((0, 0), (CONV_K - 1, 0), (0, 0)))
    c = lax.conv_general_dilated(u_pad, conv_w[:, None, :].astype(u.dtype), window_strides=(1,),
                                 padding='VALID', dimension_numbers=('NWC', 'WIO', 'NWC'),
                                 feature_group_count=CONV_W) + conv_b
    c = jax.nn.silu(layer_norm(c, cln_g, cln_b))
    c = jnp.einsum('btc,ce->bte', c, w_pw2) + b_pw2
    y_conv = c * jax.nn.silu(g_conv)

    mix = jnp.concatenate([y_rwkv, y_conv], axis=-1)
    out = jnp.einsum('btc,cd->btd', mix, w_out)
    return x + rms_norm(out, norm_post_g)


def _fwd_reference(x, norm_pre_g, w_in, mu_shift, w0, w_lora_up, a0, a_lora_up, k_k, k_a, r_k,
              lnx_g, lnx_b, conv_w, conv_b, cln_g, cln_b, w_pw2, b_pw2, w_out, norm_post_g):
    for _ in range(DEPTH):
        x = hybrid_layer(x, norm_pre_g, w_in, mu_shift, w0, w_lora_up, a0, a_lora_up, k_k, k_a, r_k,
                         lnx_g, lnx_b, conv_w, conv_b, cln_g, cln_b, w_pw2, b_pw2, w_out, norm_post_g)
    return x


import jax as _jax
import jax.numpy as _jnp

TWIN_FORMAT = 'train_step'
FWD_PARAMS = ['x', 'norm_pre_g', 'w_in', 'mu_shift', 'w0', 'w_lora_up', 'a0', 'a_lora_up', 'k_k', 'k_a', 'r_k', 'lnx_g', 'lnx_b', 'conv_w', 'conv_b', 'cln_g', 'cln_b', 'w_pw2', 'b_pw2', 'w_out', 'norm_post_g']
TWIN_WEIGHTS = ['norm_pre_g', 'w_in', 'mu_shift', 'w0', 'w_lora_up', 'a0', 'a_lora_up', 'k_k', 'k_a', 'r_k', 'lnx_g', 'lnx_b', 'conv_w', 'conv_b', 'cln_g', 'cln_b', 'w_pw2', 'b_pw2', 'w_out', 'norm_post_g']
TWIN_DIFF_INPUT = 'x'
TWIN_INPUTS = ['x', 'norm_pre_g', 'w_in', 'mu_shift', 'w0', 'w_lora_up', 'a0', 'a_lora_up', 'k_k', 'k_a', 'r_k', 'lnx_g', 'lnx_b', 'conv_w', 'conv_b', 'cln_g', 'cln_b', 'w_pw2', 'b_pw2', 'w_out', 'norm_post_g', 'loss_target', 'm_norm_pre_g', 'm_w_in', 'm_mu_shift', 'm_w0', 'm_w_lora_up', 'm_a0', 'm_a_lora_up', 'm_k_k', 'm_k_a', 'm_r_k', 'm_lnx_g', 'm_lnx_b', 'm_conv_w', 'm_conv_b', 'm_cln_g', 'm_cln_b', 'm_w_pw2', 'm_b_pw2', 'm_w_out', 'm_norm_post_g', 'v_norm_pre_g', 'v_w_in', 'v_mu_shift', 'v_w0', 'v_w_lora_up', 'v_a0', 'v_a_lora_up', 'v_k_k', 'v_k_a', 'v_r_k', 'v_lnx_g', 'v_lnx_b', 'v_conv_w', 'v_conv_b', 'v_cln_g', 'v_cln_b', 'v_w_pw2', 'v_b_pw2', 'v_w_out', 'v_norm_post_g']
TWIN_OUTPUTS = ['loss', 'grad_x', 'grad_norm_pre_g', 'grad_w_in', 'grad_mu_shift', 'grad_w0', 'grad_w_lora_up', 'grad_a0', 'grad_a_lora_up', 'grad_k_k', 'grad_k_a', 'grad_r_k', 'grad_lnx_g', 'grad_lnx_b', 'grad_conv_w', 'grad_conv_b', 'grad_cln_g', 'grad_cln_b', 'grad_w_pw2', 'grad_b_pw2', 'grad_w_out', 'grad_norm_post_g', 'delta_norm_pre_g', 'delta_w_in', 'delta_mu_shift', 'delta_w0', 'delta_w_lora_up', 'delta_a0', 'delta_a_lora_up', 'delta_k_k', 'delta_k_a', 'delta_r_k', 'delta_lnx_g', 'delta_lnx_b', 'delta_conv_w', 'delta_conv_b', 'delta_cln_g', 'delta_cln_b', 'delta_w_pw2', 'delta_b_pw2', 'delta_w_out', 'delta_norm_post_g', 'new_m_norm_pre_g', 'new_m_w_in', 'new_m_mu_shift', 'new_m_w0', 'new_m_w_lora_up', 'new_m_a0', 'new_m_a_lora_up', 'new_m_k_k', 'new_m_k_a', 'new_m_r_k', 'new_m_lnx_g', 'new_m_lnx_b', 'new_m_conv_w', 'new_m_conv_b', 'new_m_cln_g', 'new_m_cln_b', 'new_m_w_pw2', 'new_m_b_pw2', 'new_m_w_out', 'new_m_norm_post_g', 'new_v_norm_pre_g', 'new_v_w_in', 'new_v_mu_shift', 'new_v_w0', 'new_v_w_lora_up', 'new_v_a0', 'new_v_a_lora_up', 'new_v_k_k', 'new_v_k_a', 'new_v_r_k', 'new_v_lnx_g', 'new_v_lnx_b', 'new_v_conv_w', 'new_v_conv_b', 'new_v_cln_g', 'new_v_cln_b', 'new_v_w_pw2', 'new_v_b_pw2', 'new_v_w_out', 'new_v_norm_post_g']
TWIN_LEAF_KINDS = {'loss': 'loss', 'grad_x': 'grad_x', 'grad_norm_pre_g': 'grad_w', 'grad_w_in': 'grad_w', 'grad_mu_shift': 'grad_w', 'grad_w0': 'grad_w', 'grad_w_lora_up': 'grad_w', 'grad_a0': 'grad_w', 'grad_a_lora_up': 'grad_w', 'grad_k_k': 'grad_w', 'grad_k_a': 'grad_w', 'grad_r_k': 'grad_w', 'grad_lnx_g': 'grad_w', 'grad_lnx_b': 'grad_w', 'grad_conv_w': 'grad_w', 'grad_conv_b': 'grad_w', 'grad_cln_g': 'grad_w', 'grad_cln_b': 'grad_w', 'grad_w_pw2': 'grad_w', 'grad_b_pw2': 'grad_w', 'grad_w_out': 'grad_w', 'grad_norm_post_g': 'grad_w', 'delta_norm_pre_g': 'delta_w', 'delta_w_in': 'delta_w', 'delta_mu_shift': 'delta_w', 'delta_w0': 'delta_w', 'delta_w_lora_up': 'delta_w', 'delta_a0': 'delta_w', 'delta_a_lora_up': 'delta_w', 'delta_k_k': 'delta_w', 'delta_k_a': 'delta_w', 'delta_r_k': 'delta_w', 'delta_lnx_g': 'delta_w', 'delta_lnx_b': 'delta_w', 'delta_conv_w': 'delta_w', 'delta_conv_b': 'delta_w', 'delta_cln_g': 'delta_w', 'delta_cln_b': 'delta_w', 'delta_w_pw2': 'delta_w', 'delta_b_pw2': 'delta_w', 'delta_w_out': 'delta_w', 'delta_norm_post_g': 'delta_w', 'new_m_norm_pre_g': 'new_m', 'new_m_w_in': 'new_m', 'new_m_mu_shift': 'new_m', 'new_m_w0': 'new_m', 'new_m_w_lora_up': 'new_m', 'new_m_a0': 'new_m', 'new_m_a_lora_up': 'new_m', 'new_m_k_k': 'new_m', 'new_m_k_a': 'new_m', 'new_m_r_k': 'new_m', 'new_m_lnx_g': 'new_m', 'new_m_lnx_b': 'new_m', 'new_m_conv_w': 'new_m', 'new_m_conv_b': 'new_m', 'new_m_cln_g': 'new_m', 'new_m_cln_b': 'new_m', 'new_m_w_pw2': 'new_m', 'new_m_b_pw2': 'new_m', 'new_m_w_out': 'new_m', 'new_m_norm_post_g': 'new_m', 'new_v_norm_pre_g': 'new_v', 'new_v_w_in': 'new_v', 'new_v_mu_shift': 'new_v', 'new_v_w0': 'new_v', 'new_v_w_lora_up': 'new_v', 'new_v_a0': 'new_v', 'new_v_a_lora_up': 'new_v', 'new_v_k_k': 'new_v', 'new_v_k_a': 'new_v', 'new_v_r_k': 'new_v', 'new_v_lnx_g': 'new_v', 'new_v_lnx_b': 'new_v', 'new_v_conv_w': 'new_v', 'new_v_conv_b': 'new_v', 'new_v_cln_g': 'new_v', 'new_v_cln_b': 'new_v', 'new_v_w_pw2': 'new_v', 'new_v_b_pw2': 'new_v', 'new_v_w_out': 'new_v', 'new_v_norm_post_g': 'new_v'}


def _forward(args):
    return _fwd_reference(*[args[k] for k in FWD_PARAMS])


def _output_shape():
    out = _jax.eval_shape(lambda: _forward(_fwd_setup_inputs(0)))
    return out.shape, out.dtype

N_MICROBATCH = 1
ADAM_LR = 0.001
ADAM_B1 = 0.9
ADAM_B2 = 0.999
ADAM_EPS = 1e-08
ADAM_WD = 0.01
ADAM_STEP = 10
PER_EXAMPLE_BATCH_AXIS = {'x': 0, 'loss_target': 0}
SHARED_INPUTS = []
_WEIGHT_DTYPES = {'norm_pre_g': _jnp.float32, 'w_in': _jnp.float32, 'mu_shift': _jnp.float32, 'w0': _jnp.float32, 'w_lora_up': _jnp.float32, 'a0': _jnp.float32, 'a_lora_up': _jnp.float32, 'k_k': _jnp.float32, 'k_a': _jnp.float32, 'r_k': _jnp.float32, 'lnx_g': _jnp.float32, 'lnx_b': _jnp.float32, 'conv_w': _jnp.float32, 'conv_b': _jnp.float32, 'cln_g': _jnp.float32, 'cln_b': _jnp.float32, 'w_pw2': _jnp.float32, 'b_pw2': _jnp.float32, 'w_out': _jnp.float32, 'norm_post_g': _jnp.float32}
MOMENT_SCALE = {'norm_pre_g': 1.285139e-01, 'w_in': 6.595452e-02, 'mu_shift': 1.296027e-01, 'w0': 3.696491e-02, 'w_lora_up': 4.135118e-03, 'a0': 3.245707e-02, 'a_lora_up': 3.094285e-02, 'k_k': 9.548143e-02, 'k_a': 8.454193e-02, 'r_k': 1.724771e-01, 'lnx_g': 7.564692e-02, 'lnx_b': 1.036197e-01, 'conv_w': 4.645714e-02, 'conv_b': 1.003683e-01, 'cln_g': 6.000654e-02, 'cln_b': 6.010850e-02, 'w_pw2': 4.760842e-02, 'b_pw2': 1.129250e-01, 'w_out': 6.335486e-02, 'norm_post_g': 7.986358e+00}


def _to_microbatches(a, axis):
    t = _jnp.moveaxis(a, axis, 0)
    t = t.reshape((N_MICROBATCH, t.shape[0] // N_MICROBATCH) + t.shape[1:])
    return _jnp.moveaxis(t, 1, axis + 1)


def setup_inputs(seed: int = 0) -> dict:
    inp = _fwd_setup_inputs(seed)
    key = _jax.random.fold_in(_jax.random.key(seed), 7919)
    shape, _ = _output_shape()
    out = dict(inp)
    out["loss_target"] = _jax.random.normal(_jax.random.fold_in(key, 0), shape, _jnp.float32)
    for i, name in enumerate(TWIN_WEIGHTS):
        w = inp[name].astype(_jnp.float32)
        if MOMENT_SCALE is None:
            s = _jnp.sqrt(_jnp.mean(_jnp.square(w)) + 1e-30)
        else:
            s = MOMENT_SCALE[name]
        km, kv = _jax.random.split(_jax.random.fold_in(key, i + 1))
        out[name] = w
        out["m_" + name] = s * _jax.random.normal(km, w.shape, _jnp.float32)
        out["v_" + name] = (s * s) * _jax.random.uniform(kv, w.shape, _jnp.float32, 0.5, 1.5)
    if N_MICROBATCH > 1:
        for name, axis in PER_EXAMPLE_BATCH_AXIS.items():
            out[name] = _to_microbatches(out[name], axis)
    return {'x': out['x'], 'norm_pre_g': out['norm_pre_g'], 'w_in': out['w_in'], 'mu_shift': out['mu_shift'], 'w0': out['w0'], 'w_lora_up': out['w_lora_up'], 'a0': out['a0'], 'a_lora_up': out['a_lora_up'], 'k_k': out['k_k'], 'k_a': out['k_a'], 'r_k': out['r_k'], 'lnx_g': out['lnx_g'], 'lnx_b': out['lnx_b'], 'conv_w': out['conv_w'], 'conv_b': out['conv_b'], 'cln_g': out['cln_g'], 'cln_b': out['cln_b'], 'w_pw2': out['w_pw2'], 'b_pw2': out['b_pw2'], 'w_out': out['w_out'], 'norm_post_g': out['norm_post_g'], 'loss_target': out['loss_target'], 'm_norm_pre_g': out['m_norm_pre_g'], 'm_w_in': out['m_w_in'], 'm_mu_shift': out['m_mu_shift'], 'm_w0': out['m_w0'], 'm_w_lora_up': out['m_w_lora_up'], 'm_a0': out['m_a0'], 'm_a_lora_up': out['m_a_lora_up'], 'm_k_k': out['m_k_k'], 'm_k_a': out['m_k_a'], 'm_r_k': out['m_r_k'], 'm_lnx_g': out['m_lnx_g'], 'm_lnx_b': out['m_lnx_b'], 'm_conv_w': out['m_conv_w'], 'm_conv_b': out['m_conv_b'], 'm_cln_g': out['m_cln_g'], 'm_cln_b': out['m_cln_b'], 'm_w_pw2': out['m_w_pw2'], 'm_b_pw2': out['m_b_pw2'], 'm_w_out': out['m_w_out'], 'm_norm_post_g': out['m_norm_post_g'], 'v_norm_pre_g': out['v_norm_pre_g'], 'v_w_in': out['v_w_in'], 'v_mu_shift': out['v_mu_shift'], 'v_w0': out['v_w0'], 'v_w_lora_up': out['v_w_lora_up'], 'v_a0': out['v_a0'], 'v_a_lora_up': out['v_a_lora_up'], 'v_k_k': out['v_k_k'], 'v_k_a': out['v_k_a'], 'v_r_k': out['v_r_k'], 'v_lnx_g': out['v_lnx_g'], 'v_lnx_b': out['v_lnx_b'], 'v_conv_w': out['v_conv_w'], 'v_conv_b': out['v_conv_b'], 'v_cln_g': out['v_cln_g'], 'v_cln_b': out['v_cln_b'], 'v_w_pw2': out['v_w_pw2'], 'v_b_pw2': out['v_b_pw2'], 'v_w_out': out['v_w_out'], 'v_norm_post_g': out['v_norm_post_g']}


def _loss(weights, diff, rest, loss_target):
    with _jax.named_scope("forward"):
        args = {**rest, TWIN_DIFF_INPUT: diff, **{k: w.astype(_WEIGHT_DTYPES[k]) for k, w in weights.items()}}
        y = _forward(args)
    with _jax.named_scope("loss_head"):
        err = _jnp.square(y.astype(_jnp.float32) - loss_target)
        return 0.5 * _jnp.sum(_jnp.mean(err, axis=-1)) if err.ndim else 0.5 * err


def _adamw(w, g, m, v):
    m = ADAM_B1 * m + (1.0 - ADAM_B1) * g
    v = ADAM_B2 * v + (1.0 - ADAM_B2) * _jnp.square(g)
    m_hat = m / (1.0 - ADAM_B1 ** ADAM_STEP)
    v_hat = v / (1.0 - ADAM_B2 ** ADAM_STEP)
    delta = -ADAM_LR * (m_hat / (_jnp.sqrt(v_hat) + ADAM_EPS) + ADAM_WD * w)
    return delta, m, v


def reference(x, norm_pre_g, w_in, mu_shift, w0, w_lora_up, a0, a_lora_up, k_k, k_a, r_k, lnx_g, lnx_b, conv_w, conv_b, cln_g, cln_b, w_pw2, b_pw2, w_out, norm_post_g, loss_target, m_norm_pre_g, m_w_in, m_mu_shift, m_w0, m_w_lora_up, m_a0, m_a_lora_up, m_k_k, m_k_a, m_r_k, m_lnx_g, m_lnx_b, m_conv_w, m_conv_b, m_cln_g, m_cln_b, m_w_pw2, m_b_pw2, m_w_out, m_norm_post_g, v_norm_pre_g, v_w_in, v_mu_shift, v_w0, v_w_lora_up, v_a0, v_a_lora_up, v_k_k, v_k_a, v_r_k, v_lnx_g, v_lnx_b, v_conv_w, v_conv_b, v_cln_g, v_cln_b, v_w_pw2, v_b_pw2, v_w_out, v_norm_post_g):
    given = dict(x=x, norm_pre_g=norm_pre_g, w_in=w_in, mu_shift=mu_shift, w0=w0, w_lora_up=w_lora_up, a0=a0, a_lora_up=a_lora_up, k_k=k_k, k_a=k_a, r_k=r_k, lnx_g=lnx_g, lnx_b=lnx_b, conv_w=conv_w, conv_b=conv_b, cln_g=cln_g, cln_b=cln_b, w_pw2=w_pw2, b_pw2=b_pw2, w_out=w_out, norm_post_g=norm_post_g, loss_target=loss_target, m_norm_pre_g=m_norm_pre_g, m_w_in=m_w_in, m_mu_shift=m_mu_shift, m_w0=m_w0, m_w_lora_up=m_w_lora_up, m_a0=m_a0, m_a_lora_up=m_a_lora_up, m_k_k=m_k_k, m_k_a=m_k_a, m_r_k=m_r_k, m_lnx_g=m_lnx_g, m_lnx_b=m_lnx_b, m_conv_w=m_conv_w, m_conv_b=m_conv_b, m_cln_g=m_cln_g, m_cln_b=m_cln_b, m_w_pw2=m_w_pw2, m_b_pw2=m_b_pw2, m_w_out=m_w_out, m_norm_post_g=m_norm_post_g, v_norm_pre_g=v_norm_pre_g, v_w_in=v_w_in, v_mu_shift=v_mu_shift, v_w0=v_w0, v_w_lora_up=v_w_lora_up, v_a0=v_a0, v_a_lora_up=v_a_lora_up, v_k_k=v_k_k, v_k_a=v_k_a, v_r_k=v_r_k, v_lnx_g=v_lnx_g, v_lnx_b=v_lnx_b, v_conv_w=v_conv_w, v_conv_b=v_conv_b, v_cln_g=v_cln_g, v_cln_b=v_cln_b, v_w_pw2=v_w_pw2, v_b_pw2=v_b_pw2, v_w_out=v_w_out, v_norm_post_g=v_norm_post_g)
    weights = {n: given[n] for n in TWIN_WEIGHTS}
    shared = {n: given[n] for n in SHARED_INPUTS}
    per_example = {n: given[n] for n in ['x']}
    grad_fn = _jax.value_and_grad(_loss, argnums=(0, 1))

    def one_microbatch(ex, loss_target):
        ex = dict(ex)
        diff = ex.pop(TWIN_DIFF_INPUT)
        return grad_fn(weights, diff, {**shared, **ex}, loss_target)

    if N_MICROBATCH == 1:
        loss, (grad_w, grad_x) = one_microbatch(per_example, given["loss_target"])
    else:
        def body(carry, xs):
            loss_sum, grad_sum = carry
            l_k, (gw_k, gx_k) = one_microbatch(xs[0], xs[1])
            with _jax.named_scope("update"):
                return (loss_sum + l_k, _jax.tree.map(_jnp.add, grad_sum, gw_k)), gx_k

        init = (_jnp.zeros((), _jnp.float32), _jax.tree.map(_jnp.zeros_like, weights))
        (loss, grad_w), grad_x = _jax.lax.scan(body, init, (per_example, given["loss_target"]))
    with _jax.named_scope("update"):
        delta_w, new_m, new_v = {}, {}, {}
        for n in TWIN_WEIGHTS:
            delta_w[n], new_m[n], new_v[n] = _adamw(weights[n], grad_w[n], given["m_" + n], given["v_" + n])
    return (loss, grad_x, *[grad_w[n] for n in TWIN_WEIGHTS], *[delta_w[n] for n in TWIN_WEIGHTS],
            *[new_m[n] for n in TWIN_WEIGHTS], *[new_v[n] for n in TWIN_WEIGHTS])
```

```python
import functools

import jax
import jax.numpy as jnp
from jax import lax
from jax.experimental import pallas as pl
from jax.experimental.pallas import tpu as pltpu

f32 = jnp.float32
bf16 = jnp.bfloat16
MESH = pl.DeviceIdType.MESH
HI = lax.Precision.HIGHEST

NORM_EPS = 1e-6
LN_EPS = 1e-5
ADAM_LR, ADAM_B1, ADAM_B2, ADAM_EPS, ADAM_WD, ADAM_STEP = 0.001, 0.9, 0.999, 1e-08, 0.01, 10

LANES = 128
SUBLANES = 8
LORA_PAD = 256
CONV_HALO = 32
WKV_CHUNK = 64
WKV_HEADS = 4
VMEM_LIMIT = 56 * 1024 * 1024


def _cparams(n_axes):
    return pltpu.CompilerParams(dimension_semantics=("arbitrary",) * n_axes, vmem_limit_bytes=VMEM_LIMIT)


def _tile(dim, target):
    best = None
    t = LANES
    while t <= min(dim, target):
        if dim % t == 0:
            best = t
        t += LANES
    return best if best is not None else dim


def _mm(a, b, prec=None):
    return lax.dot_general(a, b, (((1,), (0,)), ((), ())), precision=prec, preferred_element_type=f32)


def _mm_nt(a, b, prec=None):
    return lax.dot_general(a, b, (((1,), (1,)), ((), ())), precision=prec, preferred_element_type=f32)


def _mm_tn(a, b, prec=None):
    return lax.dot_general(a, b, (((0,), (0,)), ((), ())), precision=prec, preferred_element_type=f32)


@jax.custom_vjp
def _bmm(a, b):
    return _mm(a.astype(bf16), b.astype(bf16))


def _bmm_fwd(a, b):
    return _bmm(a, b), (a, b)


def _bmm_bwd(res, dc):
    a, b = res
    dcb = dc.astype(bf16)
    return _mm_nt(dcb, b.astype(bf16)), _mm_tn(a.astype(bf16), dcb)


_bmm.defvjp(_bmm_fwd, _bmm_bwd)


def _matmul(name, a, b, mode, out_dtype, tm_t=1024, tn_t=768, tk_t=512):
    if mode == "nn":
        (M, K), (_, N) = a.shape, b.shape
    elif mode == "nt":
        (M, K), (N, _) = a.shape, b.shape
    else:
        (K, M), (_, N) = a.shape, b.shape
    tm, tn, tk = _tile(M, tm_t), _tile(N, tn_t), _tile(K, tk_t)
    nk = K // tk
    dot = {"nn": _mm, "nt": _mm_nt, "tn": _mm_tn}[mode]

    def body(a_ref, b_ref, o_ref, acc_ref):
        k = pl.program_id(2)

        @pl.when(k == 0)
        def _():
            acc_ref[...] = jnp.zeros_like(acc_ref)

        acc_ref[...] += dot(a_ref[...], b_ref[...])

        @pl.when(k == nk - 1)
        def _():
            o_ref[...] = acc_ref[...].astype(o_ref.dtype)

    a_spec = {"nn": pl.BlockSpec((tm, tk), lambda i, j, k: (i, k)),
              "nt": pl.BlockSpec((tm, tk), lambda i, j, k: (i, k)),
              "tn": pl.BlockSpec((tk, tm), lambda i, j, k: (k, i))}[mode]
    b_spec = {"nn": pl.BlockSpec((tk, tn), lambda i, j, k: (k, j)),
              "nt": pl.BlockSpec((tn, tk), lambda i, j, k: (j, k)),
              "tn": pl.BlockSpec((tk, tn), lambda i, j, k: (k, j))}[mode]
    return pl.pallas_call(
        body, name=name, grid=(M // tm, N // tn, nk),
        in_specs=[a_spec, b_spec],
        out_specs=pl.BlockSpec((tm, tn), lambda i, j, k: (i, j)),
        out_shape=jax.ShapeDtypeStruct((M, N), out_dtype),
        scratch_shapes=[pltpu.VMEM((tm, tn), f32)],
        compiler_params=_cparams(3),
    )(a, b)


def _row_spec(op, tm, ncol):
    arr, off, width, tiled = op
    if tiled:
        cw = width // ncol
        return pl.BlockSpec((tm, cw), lambda j, i: (i, off // cw + j))
    return pl.BlockSpec((tm, width), lambda j, i: (i, off // width))


def _param_spec(p, ncol):
    arr, tiled = p
    rows, width = arr.shape
    if tiled:
        return pl.BlockSpec((rows, width // ncol), lambda j, i: (0, j))
    return pl.BlockSpec((rows, width), lambda j, i: (0, 0))


def _row_fwd(name, fn, params, rows, outs, T, tm, ncol=1):
    npar, nrow = len(params), len(rows)

    def body(*refs):
        pv = [r[...] for r in refs[:npar]]
        rv = [r[...].astype(f32) for r in refs[npar:npar + nrow]]
        res = fn(*pv, *rv)
        for o_ref, val in zip(refs[npar + nrow:], res):
            o_ref[...] = val.astype(o_ref.dtype)

    return pl.pallas_call(
        body, name=name, grid=(ncol, T // tm),
        in_specs=[_param_spec(p, ncol) for p in params] + [_row_spec(r, tm, ncol) for r in rows],
        out_specs=[pl.BlockSpec((tm, w // ncol), lambda j, i: (i, j)) for w, _ in outs],
        out_shape=[jax.ShapeDtypeStruct((T, w), dt) for w, dt in outs],
        compiler_params=_cparams(2),
    )(*[p[0] for p in params], *[r[0] for r in rows])


def _row_bwd(name, fn, params, rows, cots, row_grads, T, tm, ncol=1):
    npar, nrow, ncot = len(params), len(rows), len(cots)
    want = [k for k, dt in enumerate(row_grads) if dt is not None]

    def body(*refs):
        pv = [r[...] for r in refs[:npar]]
        rv = [r[...].astype(f32) for r in refs[npar:npar + nrow]]
        cv = tuple(r[...].astype(f32) for r in refs[npar + nrow:npar + nrow + ncot])
        out_refs = refs[npar + nrow + ncot:]
        _, vjp = jax.vjp(fn, *pv, *rv)
        grads = vjp(cv)
        for o_ref, k in zip(out_refs[:len(want)], want):
            o_ref[...] = grads[npar + k].astype(o_ref.dtype)
        j, i = pl.program_id(0), pl.program_id(1)
        for o_ref, p, g in zip(out_refs[len(want):], params, grads[:npar]):
            first = (i == 0) if p[1] else jnp.logical_and(i == 0, j == 0)

            @pl.when(first)
            def _():
                o_ref[...] = jnp.zeros_like(o_ref)

            o_ref[...] += g

    def grad_spec(op):
        arr, off, width, tiled = op
        if tiled:
            return pl.BlockSpec((tm, width // ncol), lambda j, i: (i, j)), (T, width)
        return pl.BlockSpec((tm, width), lambda j, i: (i, j)), (T, width * ncol)

    gspecs = [grad_spec(rows[k]) for k in want]
    return pl.pallas_call(
        body, name=name, grid=(ncol, T // tm),
        in_specs=[_param_spec(p, ncol) for p in params] + [_row_spec(r, tm, ncol) for r in rows]
        + [_row_spec(c, tm, ncol) for c in cots],
        out_specs=[s for s, _ in gspecs] + [_param_spec(p, ncol) for p in params],
        out_shape=[jax.ShapeDtypeStruct(shp, row_grads[k]) for (_, shp), k in zip(gspecs, want)]
        + [jax.ShapeDtypeStruct(p[0].shape, f32) for p in params],
        compiler_params=_cparams(2),
    )(*[p[0] for p in params], *[r[0] for r in rows], *[c[0] for c in cots])


def _seg_sum(x, head):
    li = lax.broadcasted_iota(jnp.int32, (LANES, LANES), 0) // head
    lj = lax.broadcasted_iota(jnp.int32, (LANES, LANES), 1) // head
    q = (li == lj).astype(f32)
    parts = [_mm(x[:, s:s + LANES], q, HI) for s in range(0, x.shape[1], LANES)]
    return parts[0] if len(parts) == 1 else jnp.concatenate(parts, axis=1)


def _sigmoid(z):
    return 1.0 / (1.0 + jnp.exp(-z))


def _silu(z):
    return z * _sigmoid(z)


def _rms(g, x):
    return x * lax.rsqrt(jnp.mean(x * x, axis=-1, keepdims=True) + NORM_EPS) * g


def _fn_rms_pre(g, x):
    return (_rms(g, x),)


def _fn_lora(w0, wup, a0, aup, xl):
    qw = w0 + _bmm(jnp.tanh(xl), wup)
    qa = a0 + _bmm(xl, aup)
    return qw, qa


def _fn_rwkv_pre(head, k_k, k_a, xk, qw, qa):
    w_log = -(jnp.maximum(-qw, 0.0) + jnp.log(1.0 + jnp.exp(-jnp.abs(qw)))) - 0.5
    lw = -jnp.exp(w_log)
    a_sig = _sigmoid(qa)
    kk = xk * k_k
    kk = kk / jnp.maximum(jnp.sqrt(_seg_sum(kk * kk, head)), 1e-12)
    k_h = xk * (1.0 + (a_sig - 1.0) * k_a)
    return lw, k_h, -kk, kk * a_sig


def _fn_rwkv_post(head, lnx_g, lnx_b, r_k, y, r, k_h, v, g):
    inv = 1.0 / head
    mu = _seg_sum(y, head) * inv
    d = y - mu
    var = _seg_sum(d * d, head) * inv
    yn = d * lax.rsqrt(var + 1e-5 * head) * lnx_g + lnx_b
    bonus = _seg_sum(r * k_h * r_k, head) * v
    return ((yn + bonus) * _silu(g),)


def _fn_conv_ln(cln_g, cln_b, c):
    mu = jnp.mean(c, axis=-1, keepdims=True)
    d = c - mu
    var = jnp.mean(d * d, axis=-1, keepdims=True)
    return (_silu(d * lax.rsqrt(var + LN_EPS) * cln_g + cln_b),)


def _fn_conv_post(b_pw2, c2, g):
    return ((c2 + b_pw2) * _silu(g),)


def _post(out, x, tgt, g, T, D, tm):
    def body(g_ref, o_ref, x_ref, t_ref, dout_ref, gx_ref, loss_ref, dg_ref):
        i = pl.program_id(0)
        o, vjp = jax.vjp(_rms, g_ref[...], o_ref[...])
        err = x_ref[...] + o - t_ref[...]
        d_y = err * (1.0 / D)
        dg, d_out = vjp(d_y)
        dout_ref[...] = d_out.astype(dout_ref.dtype)
        gx_ref[...] = d_y

        @pl.when(i == 0)
        def _():
            loss_ref[...] = jnp.zeros_like(loss_ref)
            dg_ref[...] = jnp.zeros_like(dg_ref)

        loss_ref[...] += jnp.sum(err * err, keepdims=True) * (0.5 / D)
        dg_ref[...] += dg

    row = pl.BlockSpec((tm, D), lambda i: (i, 0))
    vec = pl.BlockSpec((1, D), lambda i: (0, 0))
    return pl.pallas_call(
        body, name="post_loss", grid=(T // tm,),
        in_specs=[vec, row, row, row],
        out_specs=[row, row, pl.BlockSpec((1, 1), lambda i: (0, 0)), vec],
        out_shape=[jax.ShapeDtypeStruct((T, D), bf16), jax.ShapeDtypeStruct((T, D), f32),
                   jax.ShapeDtypeStruct((1, 1), f32), jax.ShapeDtypeStruct((1, D), f32)],
        compiler_params=_cparams(1),
    )(g, out, x, tgt)


def _rms_pre_bwd(x, g, dh, gx_res, T, D, tm):
    def body(g_ref, x_ref, dh_ref, res_ref, dx_ref, dg_ref):
        i = pl.program_id(0)
        _, vjp = jax.vjp(_rms, g_ref[...], x_ref[...])
        dg, dx = vjp(dh_ref[...].astype(f32))
        dx_ref[...] = dx + res_ref[...]

        @pl.when(i == 0)
        def _():
            dg_ref[...] = jnp.zeros_like(dg_ref)

        dg_ref[...] += dg

    row = pl.BlockSpec((tm, D), lambda i: (i, 0))
    vec = pl.BlockSpec((1, D), lambda i: (0, 0))
    return pl.pallas_call(
        body, name="rms_pre_bwd", grid=(T // tm,),
        in_specs=[vec, row, row, row], out_specs=[row, vec],
        out_shape=[jax.ShapeDtypeStruct((T, D), f32), jax.ShapeDtypeStruct((1, D), f32)],
        compiler_params=_cparams(1),
    )(g, x, dh, gx_res)


def _prev_rows(cur, halo_ref, first):
    top = jnp.where(first, 0.0, halo_ref[SUBLANES - 1:SUBLANES, :])
    rolled = pltpu.roll(cur, 1, 0)
    rid = lax.broadcasted_iota(jnp.int32, cur.shape, 0)
    return jnp.where(rid == 0, top, rolled)


def _shift_fwd(name, proj, off, width, mu, T, tm):
    cw = _tile(width, 512)
    ncol, cb = width // cw, off // cw
    hb = tm // SUBLANES

    def body(mu_ref, cur_ref, halo_ref, o_ref):
        i = pl.program_id(1)
        cur = cur_ref[...]
        prev = _prev_rows(cur, halo_ref, i == 0)
        o_ref[...] = cur + (prev - cur) * mu_ref[...]

    return pl.pallas_call(
        body, name=name, grid=(ncol, T // tm),
        in_specs=[pl.BlockSpec((1, cw), lambda j, i: (0, j)),
                  pl.BlockSpec((tm, cw), lambda j, i: (i, cb + j)),
                  pl.BlockSpec((SUBLANES, cw), lambda j, i: (jnp.maximum(i * hb - 1, 0), cb + j))],
        out_specs=pl.BlockSpec((tm, cw), lambda j, i: (i, j)),
        out_shape=jax.ShapeDtypeStruct((T, width), f32),
        compiler_params=_cparams(2),
    )(mu, proj, proj)


def _shift_bwd(name, proj, off, width, mu, dxs, T, tm):
    cw = _tile(width, 512)
    ncol, cb = width // cw, off // cw
    hb, nt = tm // SUBLANES, T // tm

    def body(mu_ref, cur_ref, halo_ref, d_ref, dnext_ref, o_ref, dmu_ref):
        i = pl.program_id(1)
        cur = cur_ref[...]
        prev = _prev_rows(cur, halo_ref, i == 0)
        d = d_ref[...]
        bottom = jnp.where(i == nt - 1, 0.0, dnext_ref[0:1, :])
        rid = lax.broadcasted_iota(jnp.int32, d.shape, 0)
        d_next = jnp.where(rid == tm - 1, bottom, pltpu.roll(d, tm - 1, 0))
        mu_v = mu_ref[...]
        o_ref[...] = (d * (1.0 - mu_v) + d_next * mu_v).astype(o_ref.dtype)

        @pl.when(i == 0)
        def _():
            dmu_ref[...] = jnp.zeros_like(dmu_ref)

        dmu_ref[...] += jnp.sum(d * (prev - cur), axis=0, keepdims=True)

    return pl.pallas_call(
        body, name=name, grid=(ncol, nt),
        in_specs=[pl.BlockSpec((1, cw), lambda j, i: (0, j)),
                  pl.BlockSpec((tm, cw), lambda j, i: (i, cb + j)),
                  pl.BlockSpec((SUBLANES, cw), lambda j, i: (jnp.maximum(i * hb - 1, 0), cb + j)),
                  pl.BlockSpec((tm, cw), lambda j, i: (i, j)),
                  pl.BlockSpec((SUBLANES, cw), lambda j, i: (jnp.minimum((i + 1) * hb, nt * hb - 1), j))],
        out_specs=[pl.BlockSpec((tm, cw), lambda j, i: (i, j)), pl.BlockSpec((1, cw), lambda j, i: (0, j))],
        out_shape=[jax.ShapeDtypeStruct((T, width), bf16), jax.ShapeDtypeStruct((1, width), f32)],
        compiler_params=_cparams(2),
    )(mu, proj, proj, dxs, dxs)


def _conv_fwd(proj, off_v, off_g, width, conv_w, conv_b, ktaps, T, tm):
    cw = _tile(width, 512)
    ncol = width // cw
    hb = tm // CONV_HALO
    lead = CONV_HALO - (ktaps - 1)

    def body(w_ref, b_ref, v_ref, g_ref, hv_ref, hg_ref, o_ref, u_ref):
        i = pl.program_id(1)
        halo = hv_ref[...] * _sigmoid(hg_ref[...])
        u_ref[0:CONV_HALO, :] = jnp.where(i == 0, 0.0, halo)
        u_ref[CONV_HALO:, :] = v_ref[...] * _sigmoid(g_ref[...])
        acc = jnp.zeros((tm, cw), f32) + b_ref[...]
        for j in range(ktaps):
            acc = acc + u_ref[pl.ds(lead + j, tm), :] * w_ref[j:j + 1, :]
        o_ref[...] = acc

    def tile(off):
        return pl.BlockSpec((tm, cw), lambda j, i: (i, off // cw + j))

    def halo(off):
        return pl.BlockSpec((CONV_HALO, cw), lambda j, i: (jnp.maximum(i * hb - 1, 0), off // cw + j))

    return pl.pallas_call(
        body, name="conv_fwd", grid=(ncol, T // tm),
        in_specs=[pl.BlockSpec((CONV_HALO, cw), lambda j, i: (0, j)), pl.BlockSpec((1, cw), lambda j, i: (0, j)),
                  tile(off_v), tile(off_g), halo(off_v), halo(off_g)],
        out_specs=pl.BlockSpec((tm, cw), lambda j, i: (i, j)),
        out_shape=jax.ShapeDtypeStruct((T, width), f32),
        scratch_shapes=[pltpu.VMEM((CONV_HALO + tm, cw), f32)],
        compiler_params=_cparams(2),
    )(conv_w, conv_b, proj, proj, proj, proj)


def _conv_bwd(proj, off_v, off_g, width, conv_w, dc, ktaps, T, tm):
    cw = _tile(width, 512)
    ncol = width // cw
    hb, nt = tm // CONV_HALO, T // tm
    lead = CONV_HALO - (ktaps - 1)

    def body(w_ref, v_ref, g_ref, hv_ref, hg_ref, dc_ref, dcn_ref, dv_ref, dg_ref, dw_ref, db_ref, u_ref, d_ref):
        i = pl.program_id(1)
        halo = hv_ref[...] * _sigmoid(hg_ref[...])
        u_ref[0:CONV_HALO, :] = jnp.where(i == 0, 0.0, halo)
        sig = _sigmoid(g_ref[...])
        gv = v_ref[...]
        u_ref[CONV_HALO:, :] = gv * sig
        dcur = dc_ref[...]
        d_ref[0:tm, :] = dcur
        d_ref[tm:, :] = jnp.where(i == nt - 1, 0.0, dcn_ref[...])

        @pl.when(i == 0)
        def _():
            dw_ref[...] = jnp.zeros_like(dw_ref)
            db_ref[...] = jnp.zeros_like(db_ref)

        du = jnp.zeros((tm, cw), f32)
        for j in range(ktaps):
            du = du + d_ref[pl.ds(ktaps - 1 - j, tm), :] * w_ref[j:j + 1, :]
            dw_ref[j:j + 1, :] += jnp.sum(u_ref[pl.ds(lead + j, tm), :] * dcur, axis=0, keepdims=True)
        db_ref[...] += jnp.sum(dcur, axis=0, keepdims=True)
        dv_ref[...] = (du * sig).astype(dv_ref.dtype)
        dg_ref[...] = (du * gv * sig * (1.0 - sig)).astype(dg_ref.dtype)

    def tile(off):
        return pl.BlockSpec((tm, cw), lambda j, i: (i, off // cw + j))

    def halo(off):
        return pl.BlockSpec((CONV_HALO, cw), lambda j, i: (jnp.maximum(i * hb - 1, 0), off // cw + j))

    return pl.pallas_call(
        body, name="conv_bwd", grid=(ncol, nt),
        in_specs=[pl.BlockSpec((CONV_HALO, cw), lambda j, i: (0, j)),
                  tile(off_v), tile(off_g), halo(off_v), halo(off_g),
                  pl.BlockSpec((tm, cw), lambda j, i: (i, j)),
                  pl.BlockSpec((CONV_HALO, cw), lambda j, i: (jnp.minimum((i + 1) * hb, nt * hb - 1), j))],
        out_specs=[pl.BlockSpec((tm, cw), lambda j, i: (i, j)), pl.BlockSpec((tm, cw), lambda j, i: (i, j)),
                   pl.BlockSpec((CONV_HALO, cw), lambda j, i: (0, j)), pl.BlockSpec((1, cw), lambda j, i: (0, j))],
        out_shape=[jax.ShapeDtypeStruct((T, width), bf16), jax.ShapeDtypeStruct((T, width), bf16),
                   jax.ShapeDtypeStruct((CONV_HALO, width), f32), jax.ShapeDtypeStruct((1, width), f32)],
        scratch_shapes=[pltpu.VMEM((CONV_HALO + tm, cw), f32), pltpu.VMEM((tm + CONV_HALO, cw), f32)],
        compiler_params=_cparams(2),
    )(conv_w, proj, proj, proj, proj, dc, dc)


def _wkv_chunk(r, lw, k, v, a, b, S0):
    C = r.shape[0]
    row = lax.broadcasted_iota(jnp.int32, (C, C), 0)
    col = lax.broadcasted_iota(jnp.int32, (C, C), 1)
    incl = row >= col
    strict = row > col
    G = _mm(incl.astype(f32), lw, HI)
    g_end = jnp.sum(lw, axis=0, keepdims=True)
    e_g, e_ng = jnp.exp(G), jnp.exp(-G)
    At = a * jnp.exp(G - lw)
    Rt = r * e_g
    Kt = k * e_ng
    Bt = b * e_ng
    to_end = jnp.exp(g_end - G)
    zero = jnp.zeros((C, C), f32)
    L = jnp.where(strict, _mm_nt(At, Bt, HI), zero)
    M = jnp.where(strict, _mm_nt(At, Kt, HI), zero)
    Pb = jnp.where(incl, _mm_nt(Rt, Bt, HI), zero)
    Pk = jnp.where(incl, _mm_nt(Rt, Kt, HI), zero)
    X = _mm_nt(At, S0, HI) + _mm(M, v, HI)
    Lp = L
    n = 1
    while n < C:
        X = X + _mm(Lp, X, HI)
        n *= 2
        if n < C:
            Lp = _mm(Lp, Lp, HI)
    y = _mm_nt(Rt, S0, HI) + _mm(Pb, X, HI) + _mm(Pk, v, HI)
    S1 = S0 * jnp.exp(g_end) + _mm_tn(X, b * to_end, HI) + _mm_tn(v, k * to_end, HI)
    return y, S1


def _wkv_fwd(r, lw, k, v, a, b, head, T, RW):
    C = min(WKV_CHUNK, T)
    nh = RW // head
    hb = min(WKV_HEADS, nh)
    bw = hb * head
    nc = T // C

    def body(r_ref, lw_ref, k_ref, v_ref, a_ref, b_ref, y_ref, st_ref, s_ref):
        @pl.when(pl.program_id(1) == 0)
        def _():
            s_ref[...] = jnp.zeros_like(s_ref)

        for h in range(hb):
            sl = slice(h * head, (h + 1) * head)
            S0 = s_ref[h]
            st_ref[0, h] = S0
            y, S1 = _wkv_chunk(r_ref[:, sl], lw_ref[:, sl], k_ref[:, sl], v_ref[:, sl], a_ref[:, sl], b_ref[:, sl], S0)
            y_ref[:, sl] = y
            s_ref[h] = S1

    blk = pl.BlockSpec((C, bw), lambda g, c: (c, g))
    return pl.pallas_call(
        body, name="wkv_fwd", grid=(nh // hb, nc),
        in_specs=[blk] * 6,
        out_specs=[blk, pl.BlockSpec((1, hb, head, head), lambda g, c: (c, g, 0, 0))],
        out_shape=[jax.ShapeDtypeStruct((T, RW), f32), jax.ShapeDtypeStruct((nc, nh, head, head), f32)],
        scratch_shapes=[pltpu.VMEM((hb, head, head), f32)],
        compiler_params=_cparams(2),
    )(r, lw, k, v, a, b)


def _wkv_bwd(r, lw, k, v, a, b, states, dy, dr_x, dk_x, dv_x, head, T, RW):
    C = min(WKV_CHUNK, T)
    nh = RW // head
    hb = min(WKV_HEADS, nh)
    bw = hb * head
    nc = T // C

    def body(r_ref, lw_ref, k_ref, v_ref, a_ref, b_ref, st_ref, dy_ref, drx_ref, dkx_ref, dvx_ref,
             dr_ref, dlw_ref, dk_ref, dv_ref, da_ref, db_ref, ds_ref):
        @pl.when(pl.program_id(1) == 0)
        def _():
            ds_ref[...] = jnp.zeros_like(ds_ref)

        for h in range(hb):
            sl = slice(h * head, (h + 1) * head)
            _, vjp = jax.vjp(_wkv_chunk, r_ref[:, sl], lw_ref[:, sl], k_ref[:, sl], v_ref[:, sl], a_ref[:, sl],
                             b_ref[:, sl], st_ref[0, h])
            dr, dlw, dk, dv, da, db, ds0 = vjp((dy_ref[:, sl], ds_ref[h]))
            dr_ref[:, sl] = dr + drx_ref[:, sl]
            dlw_ref[:, sl] = dlw
            dk_ref[:, sl] = dk + dkx_ref[:, sl]
            dv_ref[:, sl] = dv + dvx_ref[:, sl]
            da_ref[:, sl] = da
            db_ref[:, sl] = db
            ds_ref[h] = ds0

    blk = pl.BlockSpec((C, bw), lambda g, c: (nc - 1 - c, g))
    return pl.pallas_call(
        body, name="wkv_bwd", grid=(nh // hb, nc),
        in_specs=[blk] * 6 + [pl.BlockSpec((1, hb, head, head), lambda g, c: (nc - 1 - c, g, 0, 0))] + [blk] * 4,
        out_specs=[blk] * 6,
        out_shape=[jax.ShapeDtypeStruct((T, RW), f32)] * 6,
        scratch_shapes=[pltpu.VMEM((hb, head, head), f32)],
        compiler_params=_cparams(2),
    )(r, lw, k, v, a, b, states, dy, dr_x, dk_x, dv_x)


def _rows_tile(R, row_bytes, budget):
    best = None
    t = SUBLANES
    while t <= R:
        if R % t == 0 and t * row_bytes <= budget:
            best = t
        t += SUBLANES
    return best if best is not None else R


def _sum8(name, parts):
    _, R, W = parts.shape
    tr = _rows_tile(R, 8 * W * 4, 4 << 20)

    def body(p_ref, o_ref):
        acc = p_ref[0]
        for d in range(1, 8):
            acc = acc + p_ref[d]
        o_ref[...] = acc

    return pl.pallas_call(
        body, name=name, grid=(R // tr,),
        in_specs=[pl.BlockSpec((8, tr, W), lambda i: (0, i, 0))],
        out_specs=pl.BlockSpec((tr, W), lambda i: (i, 0)),
        out_shape=jax.ShapeDtypeStruct((R, W), f32),
        compiler_params=_cparams(1),
    )(parts)


def _adamw(name, w, g, m, v):
    R, W = w.shape
    tr = _rows_tile(R, W * 4, 1 << 20)

    def body(w_ref, g_ref, m_ref, v_ref, d_ref, nm_ref, nv_ref):
        g_v = g_ref[...]
        nm = ADAM_B1 * m_ref[...] + (1.0 - ADAM_B1) * g_v
        nv = ADAM_B2 * v_ref[...] + (1.0 - ADAM_B2) * (g_v * g_v)
        m_hat = nm / (1.0 - ADAM_B1 ** ADAM_STEP)
        v_hat = nv / (1.0 - ADAM_B2 ** ADAM_STEP)
        d_ref[...] = -ADAM_LR * (m_hat / (jnp.sqrt(v_hat) + ADAM_EPS) + ADAM_WD * w_ref[...])
        nm_ref[...] = nm
        nv_ref[...] = nv

    blk = pl.BlockSpec((tr, W), lambda i: (i, 0))
    return pl.pallas_call(
        body, name=name, grid=(R // tr,),
        in_specs=[blk] * 4, out_specs=[blk] * 3,
        out_shape=[jax.ShapeDtypeStruct((R, W), f32)] * 3,
        compiler_params=_cparams(1),
    )(w, g, m, v)


ANY = pl.BlockSpec(memory_space=pl.ANY)


def _place():
    return lax.axis_index("x"), lax.axis_index("y"), lax.axis_index("c")


def _gather_chips(arrays):
    n = len(arrays)

    def body(*refs):
        ins, outs = refs[:n], refs[n:2 * n]
        send_sems, recv_sems, local_sems = refs[2 * n:]
        x, y, c = _place()
        mine = 2 * x + y
        chips = [(1 - x, y), (x, 1 - y), (1 - x, 1 - y)]
        local = [pltpu.make_async_copy(ins[a], outs[a].at[mine], local_sems.at[a]) for a in range(n)]
        for cp in local:
            cp.start()
        sends = []
        for a in range(n):
            for j, (px, py) in enumerate(chips):
                cp = pltpu.make_async_remote_copy(
                    src_ref=ins[a], dst_ref=outs[a].at[mine], send_sem=send_sems.at[3 * a + j],
                    recv_sem=recv_sems.at[3 * a + j], device_id=(px, py, c), device_id_type=MESH)
                cp.start()
                sends.append(cp)
        for a in range(n):
            for j, (px, py) in enumerate(chips):
                pltpu.make_async_remote_copy(
                    src_ref=ins[a], dst_ref=outs[a].at[2 * px + py], send_sem=send_sems.at[3 * a + j],
                    recv_sem=recv_sems.at[3 * a + j], device_id=(px, py, c), device_id_type=MESH).wait_recv()
        for cp in sends:
            cp.wait_send()
        for cp in local:
            cp.wait()

    return pl.pallas_call(
        body, name="gather_weights",
        in_specs=[ANY] * n, out_specs=[ANY] * n,
        out_shape=[jax.ShapeDtypeStruct((4,) + a.shape, a.dtype) for a in arrays],
        scratch_shapes=[pltpu.SemaphoreType.DMA((3 * n,)), pltpu.SemaphoreType.DMA((3 * n,)),
                        pltpu.SemaphoreType.DMA((n,))],
    )(*arrays)


def _exchange_pieces(pieces, whole):
    n, m = len(pieces), len(whole)
    tot = n + m

    def body(*refs):
        ins, outs = refs[:tot], refs[tot:2 * tot]
        send_sems, recv_sems, local_sems = refs[2 * tot:]
        x, y, c = _place()
        mine = 4 * x + 2 * y + c
        peers = [(x ^ (k >> 2), y ^ ((k >> 1) & 1), c ^ (k & 1)) for k in range(1, 8)]

        def src(a, idx):
            return ins[a].at[idx] if a < n else ins[a]

        local = [pltpu.make_async_copy(src(a, mine), outs[a].at[mine], local_sems.at[a]) for a in range(tot)]
        for cp in local:
            cp.start()
        sends = []
        for a in range(tot):
            for j, (px, py, pc) in enumerate(peers):
                cp = pltpu.make_async_remote_copy(
                    src_ref=src(a, 4 * px + 2 * py + pc), dst_ref=outs[a].at[mine], send_sem=send_sems.at[7 * a + j],
                    recv_sem=recv_sems.at[7 * a + j], device_id=(px, py, pc), device_id_type=MESH)
                cp.start()
                sends.append(cp)
        for a in range(tot):
            for j, (px, py, pc) in enumerate(peers):
                pltpu.make_async_remote_copy(
                    src_ref=src(a, mine), dst_ref=outs[a].at[4 * px + 2 * py + pc], send_sem=send_sems.at[7 * a + j],
                    recv_sem=recv_sems.at[7 * a + j], device_id=(px, py, pc), device_id_type=MESH).wait_recv()
        for cp in sends:
            cp.wait_send()
        for cp in local:
            cp.wait()

    shapes = [jax.ShapeDtypeStruct(a.shape, a.dtype) for a in pieces]
    shapes += [jax.ShapeDtypeStruct((8,) + a.shape, a.dtype) for a in whole]
    return pl.pallas_call(
        body, name="exchange_grads",
        in_specs=[ANY] * tot, out_specs=[ANY] * tot, out_shape=shapes,
        scratch_shapes=[pltpu.SemaphoreType.DMA((7 * tot,)), pltpu.SemaphoreType.DMA((7 * tot,)),
                        pltpu.SemaphoreType.DMA((tot,))],
    )(*pieces, *whole)


def _share_sibling(arrays):
    n = len(arrays)

    def body(*refs):
        ins, outs = refs[:n], refs[n:2 * n]
        send_sems, recv_sems, local_sems = refs[2 * n:]
        x, y, c = _place()
        sib = (x, y, 1 - c)
        local = [pltpu.make_async_copy(ins[a], outs[a].at[c], local_sems.at[a]) for a in range(n)]
        for cp in local:
            cp.start()
        sends = []
        for a in range(n):
            cp = pltpu.make_async_remote_copy(
                src_ref=ins[a], dst_ref=outs[a].at[c], send_sem=send_sems.at[a], recv_sem=recv_sems.at[a],
                device_id=sib, device_id_type=MESH)
            cp.start()
            sends.append(cp)
        for a in range(n):
            pltpu.make_async_remote_copy(
                src_ref=ins[a], dst_ref=outs[a].at[1 - c], send_sem=send_sems.at[a], recv_sem=recv_sems.at[a],
                device_id=sib, device_id_type=MESH).wait_recv()
        for cp in sends:
            cp.wait_send()
        for cp in local:
            cp.wait()

    return pl.pallas_call(
        body, name="share_sibling",
        in_specs=[ANY] * n, out_specs=[ANY] * n,
        out_shape=[jax.ShapeDtypeStruct((2,) + a.shape, a.dtype) for a in arrays],
        scratch_shapes=[pltpu.SemaphoreType.DMA((n,)), pltpu.SemaphoreType.DMA((n,)), pltpu.SemaphoreType.DMA((n,))],
    )(*arrays)


def kernel(x, norm_pre_g, w_in, mu_shift, w0, w_lora_up, a0, a_lora_up, k_k, k_a, r_k, lnx_g, lnx_b, conv_w, conv_b, cln_g, cln_b, w_pw2, b_pw2, w_out, norm_post_g, loss_target, m_norm_pre_g, m_w_in, m_mu_shift, m_w0, m_w_lora_up, m_a0, m_a_lora_up, m_k_k, m_k_a, m_r_k, m_lnx_g, m_lnx_b, m_conv_w, m_conv_b, m_cln_g, m_cln_b, m_w_pw2, m_b_pw2, m_w_out, m_norm_post_g, v_norm_pre_g, v_w_in, v_mu_shift, v_w0, v_w_lora_up, v_a0, v_a_lora_up, v_k_k, v_k_a, v_r_k, v_lnx_g, v_lnx_b, v_conv_w, v_conv_b, v_cln_g, v_cln_b, v_w_pw2, v_b_pw2, v_w_out, v_norm_post_g):
    _, T, D = x.shape
    RW = w0.shape[0]
    CW = conv_b.shape[0]
    head = r_k.shape[1]
    lora = w_lora_up.shape[0]
    ktaps = conv_w.shape[0]
    assert RW == CW and 2 * lora <= LORA_PAD and ktaps - 1 <= CONV_HALO
    n_in = 3 * RW + 2 * lora + RW + 3 * CW
    shard = n_in // 4
    PW = 7 * RW + LORA_PAD
    off_l = 7 * RW
    tm = min(256, T // 2)
    tm_wide = min(128, T // 2)
    row = lambda vec: vec.reshape(1, -1)
    x2, tgt2 = x[0], loss_target[0]

    g_win, g_wup, g_aup, g_cw, g_pw2, g_wout = _gather_chips(
        [w_in.astype(bf16), w_lora_up, a_lora_up, conv_w, w_pw2.astype(bf16), w_out.astype(bf16)])
    cat_cols = lambda g: jnp.concatenate([g[s] for s in range(4)], axis=1)
    win_full = cat_cols(g_win)
    lo = 3 * RW
    wp = jnp.concatenate([win_full[:, :lo], win_full[:, lo + 2 * lora:], win_full[:, lo:lo + 2 * lora],
                          jnp.zeros((D, LORA_PAD - 2 * lora), bf16)], axis=1)
    wup_full, aup_full, cw_full = cat_cols(g_wup), cat_cols(g_aup), cat_cols(g_cw)
    zl = lambda n: jnp.zeros((n, RW), f32)
    wup_p = jnp.concatenate([wup_full, zl(LORA_PAD - lora)], axis=0)
    aup_p = jnp.concatenate([zl(lora), aup_full, zl(LORA_PAD - 2 * lora)], axis=0)
    cw_p = jnp.concatenate([cw_full, jnp.zeros((CONV_HALO - ktaps, CW), f32)], axis=0)
    pw2_full = g_pw2.reshape(CW, CW)
    wout_full = g_wout.reshape(RW + CW, D)
    mu_r, mu_k, mu_v = (row(mu_shift[s * RW:(s + 1) * RW]) for s in range(3))
    mu_l = row(jnp.concatenate([mu_shift[3 * RW:], jnp.zeros((LORA_PAD - 2 * lora,), f32)]))

    npg = row(norm_pre_g)
    (h,) = _row_fwd("rms_pre", _fn_rms_pre, [(npg, False)], [(x2, 0, D, False)], [(D, bf16)], T, tm)
    proj = _matmul("proj", h, wp, "nn", f32)
    xs_r = _shift_fwd("shift_r", proj, 0, RW, mu_r, T, tm)
    xs_k = _shift_fwd("shift_k", proj, RW, RW, mu_k, T, tm)
    xs_v = _shift_fwd("shift_v", proj, 2 * RW, RW, mu_v, T, tm)
    xs_l = _shift_fwd("shift_l", proj, off_l, LORA_PAD, mu_l, T, tm)
    lora_params = [(row(w0), False), (wup_p, False), (row(a0), False), (aup_p, False)]
    qw, qa = _row_fwd("lora_up", _fn_lora, lora_params, [(xs_l, 0, LORA_PAD, False)], [(RW, f32), (RW, f32)], T, tm)
    ncol = RW // _tile(RW, 512)
    fn_pre = functools.partial(_fn_rwkv_pre, head)
    pre_params = [(row(k_k), True), (row(k_a), True)]
    pre_rows = [(xs_k, 0, RW, True), (qw, 0, RW, True), (qa, 0, RW, True)]
    lw, k_h, a_rec, b_rec = _row_fwd("rwkv_pre", fn_pre, pre_params, pre_rows, [(RW, f32)] * 4, T, tm, ncol)
    y_wkv, states = _wkv_fwd(xs_r, lw, k_h, xs_v, a_rec, b_rec, head, T, RW)
    fn_post = functools.partial(_fn_rwkv_post, head)
    post_params = [(row(lnx_g), True), (row(lnx_b), True), (r_k.reshape(1, RW), True)]
    post_rows = [(y_wkv, 0, RW, True), (xs_r, 0, RW, True), (k_h, 0, RW, True), (xs_v, 0, RW, True),
                 (proj, 3 * RW, RW, True)]
    (y_rwkv,) = _row_fwd("rwkv_post", fn_post, post_params, post_rows, [(RW, bf16)], T, tm, ncol)

    c_pre = _conv_fwd(proj, 4 * RW, 5 * RW, CW, cw_p, row(conv_b), ktaps, T, tm)
    ln_params = [(row(cln_g), False), (row(cln_b), False)]
    (c_act,) = _row_fwd("conv_ln", _fn_conv_ln, ln_params, [(c_pre, 0, CW, False)], [(CW, bf16)], T, tm)
    c2 = _matmul("pw2", c_act, pw2_full, "nn", f32)
    cpost_params = [(row(b_pw2), True)]
    cpost_rows = [(c2, 0, CW, True), (proj, 6 * RW, CW, True)]
    (y_conv,) = _row_fwd("conv_post", _fn_conv_post, cpost_params, cpost_rows, [(CW, bf16)], T, tm, ncol)

    mix = jnp.concatenate([y_rwkv, y_conv], axis=1)
    out = _matmul("out_proj", mix, wout_full, "nn", f32)
    d_out, gx_res, loss_part, g_npost = _post(out, x2, tgt2, row(norm_post_g), T, D, tm_wide)

    g_wout_full = _matmul("d_w_out", mix, d_out, "tn", f32)
    d_mix = _matmul("d_mix", d_out, wout_full, "nt", f32)

    d_c2, d_gconv, g_bpw2 = _row_bwd("conv_post_bwd", _fn_conv_post, cpost_params, cpost_rows,
                                      [(d_mix, RW, CW, True)], [bf16, bf16], T, tm, ncol)
    g_pw2_full = _matmul("d_w_pw2", c_act, d_c2, "tn", f32)
    d_cact = _matmul("d_c_act", d_c2, pw2_full, "nt", f32)
    d_cpre, g_clng, g_clnb = _row_bwd("conv_ln_bwd", _fn_conv_ln, ln_params, [(c_pre, 0, CW, False)],
                                      [(d_cact, 0, CW, False)], [f32], T, tm)
    d_gluv, d_glug, g_cw_p, g_cb = _conv_bwd(proj, 4 * RW, 5 * RW, CW, cw_p, d_cpre, ktaps, T, tm)

    d_y, dr_x, dk_x, dv_x, d_grwkv, g_lnxg, g_lnxb, g_rk = _row_bwd(
        "rwkv_post_bwd", fn_post, post_params, post_rows, [(d_mix, 0, RW, True)], [f32, f32, f32, f32, bf16], T, tm, ncol)
    d_xr, d_lw, d_kh, d_xv, d_a, d_b = _wkv_bwd(xs_r, lw, k_h, xs_v, a_rec, b_rec, states, d_y, dr_x, dk_x, dv_x,
                                                head, T, RW)
    pre_cots = [(d_lw, 0, RW, True), (d_kh, 0, RW, True), (d_a, 0, RW, True), (d_b, 0, RW, True)]
    d_xk, d_qw, d_qa, g_kk, g_ka = _row_bwd("rwkv_pre_bwd", fn_pre, pre_params, pre_rows, pre_cots, [f32, f32, f32],
                                            T, tm, ncol)
    d_xl, g_w0, g_wup_p, g_a0, g_aup_p = _row_bwd("lora_up_bwd", _fn_lora, lora_params, [(xs_l, 0, LORA_PAD, False)],
                                                  [(d_qw, 0, RW, False), (d_qa, 0, RW, False)], [f32], T, tm)
    dp_r, g_mur = _shift_bwd("shift_r_bwd", proj, 0, RW, mu_r, d_xr, T, tm)
    dp_k, g_muk = _shift_bwd("shift_k_bwd", proj, RW, RW, mu_k, d_xk, T, tm)
    dp_v, g_muv = _shift_bwd("shift_v_bwd", proj, 2 * RW, RW, mu_v, d_xv, T, tm)
    dp_l, g_mul = _shift_bwd("shift_l_bwd", proj, off_l, LORA_PAD, mu_l, d_xl, T, tm)
    d_proj = jnp.concatenate([dp_r, dp_k, dp_v, d_grwkv, d_gluv, d_glug, d_gconv, dp_l], axis=1)

    g_wp = _matmul("d_w_in", h, d_proj, "tn", f32)
    d_h = _matmul("d_h", d_proj, wp, "nt", bf16)
    grad_x2, g_npre = _rms_pre_bwd(x2, npg, d_h, gx_res, T, D, tm_wide)

    g_win_full = jnp.concatenate([g_wp[:, :lo], g_wp[:, off_l:off_l + 2 * lora], g_wp[:, lo:off_l]], axis=1)
    win_pieces = jnp.stack([g_win_full[:, s * shard:(s + 1) * shard] for s in range(4)]).reshape(8, D // 2, shard)
    wout_pieces = g_wout_full.reshape(8, (RW + CW) // 8, D)
    pw2_pieces = g_pw2_full.reshape(8, CW // 8, CW)
    g_mu = jnp.concatenate([g_mur[0], g_muk[0], g_muv[0], g_mul[0, :2 * lora]])
    pad_rows = lambda a, n: jnp.concatenate([a, jnp.zeros((n - a.shape[0], a.shape[1]), f32)], axis=0)
    n_mu = -(-mu_shift.shape[0] // RW)
    small_vecs = [g_npre.reshape(D // RW, RW), pad_rows(jnp.pad(g_mu, (0, n_mu * RW - g_mu.shape[0])).reshape(n_mu, RW), n_mu),
                  g_w0, g_a0, g_kk, g_ka, g_rk, g_lnxg, g_lnxb, g_cb, g_clng, g_clnb, g_bpw2,
                  g_npost.reshape(D // RW, RW), g_wup_p[:lora], g_aup_p[lora:2 * lora], g_cw_p[:ktaps]]
    n_small = sum(a.shape[0] for a in small_vecs)
    n_small_pad = -(-n_small // SUBLANES) * SUBLANES
    small = pad_rows(jnp.concatenate(small_vecs, axis=0), n_small_pad)

    r_win, r_wout, r_pw2, r_small = _exchange_pieces([win_pieces, wout_pieces, pw2_pieces], [small])
    s_win = _sum8("sum_w_in", r_win)
    s_wout = _sum8("sum_w_out", r_wout)
    s_pw2 = _sum8("sum_w_pw2", r_pw2)
    s_small = _sum8("sum_small", r_small)
    sh_win, sh_wout, sh_pw2 = _share_sibling([s_win, s_wout, s_pw2])
    grad_w_in = sh_win.reshape(D, shard)
    grad_w_out = sh_wout.reshape((RW + CW) // 4, D)
    grad_w_pw2 = sh_pw2.reshape(CW // 4, CW)

    chip = 2 * lax.axis_index("x") + lax.axis_index("y")
    pos = [0]

    def take(nrows):
        a = s_small[pos[0]:pos[0] + nrows]
        pos[0] += nrows
        return a

    my_cols = lambda a, w: lax.dynamic_slice_in_dim(a, chip * w, w, axis=1)
    grads = {}
    grads["norm_pre_g"] = take(D // RW).reshape(D)
    grads["mu_shift"] = take(n_mu).reshape(-1)[:mu_shift.shape[0]]
    for nm in ["w0", "a0", "k_k", "k_a"]:
        grads[nm] = take(1).reshape(RW)
    grads["r_k"] = take(1).reshape(r_k.shape)
    for nm in ["lnx_g", "lnx_b", "conv_b", "cln_g", "cln_b", "b_pw2"]:
        grads[nm] = take(1).reshape(RW)
    grads["norm_post_g"] = take(D // RW).reshape(D)
    grads["w_lora_up"] = my_cols(take(lora), RW // 4)
    grads["a_lora_up"] = my_cols(take(lora), RW // 4)
    grads["conv_w"] = my_cols(take(ktaps), CW // 4)
    grads["w_in"], grads["w_out"], grads["w_pw2"] = grad_w_in, grad_w_out, grad_w_pw2

    weights = dict(norm_pre_g=norm_pre_g, w_in=w_in, mu_shift=mu_shift, w0=w0, w_lora_up=w_lora_up, a0=a0,
                   a_lora_up=a_lora_up, k_k=k_k, k_a=k_a, r_k=r_k, lnx_g=lnx_g, lnx_b=lnx_b, conv_w=conv_w,
                   conv_b=conv_b, cln_g=cln_g, cln_b=cln_b, w_pw2=w_pw2, b_pw2=b_pw2, w_out=w_out,
                   norm_post_g=norm_post_g)
    ms = dict(norm_pre_g=m_norm_pre_g, w_in=m_w_in, mu_shift=m_mu_shift, w0=m_w0, w_lora_up=m_w_lora_up, a0=m_a0,
              a_lora_up=m_a_lora_up, k_k=m_k_k, k_a=m_k_a, r_k=m_r_k, lnx_g=m_lnx_g, lnx_b=m_lnx_b, conv_w=m_conv_w,
              conv_b=m_conv_b, cln_g=m_cln_g, cln_b=m_cln_b, w_pw2=m_w_pw2, b_pw2=m_b_pw2, w_out=m_w_out,
              norm_post_g=m_norm_post_g)
    vs = dict(norm_pre_g=v_norm_pre_g, w_in=v_w_in, mu_shift=v_mu_shift, w0=v_w0, w_lora_up=v_w_lora_up, a0=v_a0,
              a_lora_up=v_a_lora_up, k_k=v_k_k, k_a=v_k_a, r_k=v_r_k, lnx_g=v_lnx_g, lnx_b=v_lnx_b, conv_w=v_conv_w,
              conv_b=v_conv_b, cln_g=v_cln_g, cln_b=v_cln_b, w_pw2=v_w_pw2, b_pw2=v_b_pw2, w_out=v_w_out,
              norm_post_g=v_norm_post_g)
    names = list(weights)
    big = ["w_in", "w_out", "w_pw2"]
    deltas, new_m, new_v = {}, {}, {}
    for nm in big:
        deltas[nm], new_m[nm], new_v[nm] = _adamw("adamw_" + nm, weights[nm], grads[nm], ms[nm], vs[nm])
    rest = [nm for nm in names if nm not in big]
    sizes = [weights[nm].size for nm in rest]
    total = sum(sizes)
    width = 4 * LANES
    rows_p = -(-total // (width * SUBLANES)) * SUBLANES

    def pack(d):
        flat = jnp.concatenate([d[nm].reshape(-1) for nm in rest])
        return jnp.pad(flat, (0, rows_p * width - total)).reshape(rows_p, width)

    p_d, p_m, p_v = _adamw("adamw_small", pack(weights), pack(grads), pack(ms), pack(vs))
    o = 0
    for nm, sz in zip(rest, sizes):
        shp = weights[nm].shape
        deltas[nm] = p_d.reshape(-1)[o:o + sz].reshape(shp)
        new_m[nm] = p_m.reshape(-1)[o:o + sz].reshape(shp)
        new_v[nm] = p_v.reshape(-1)[o:o + sz].reshape(shp)
        o += sz

    loss = lax.psum(loss_part[0, 0], ("x", "y", "c"))
    grad_x = grad_x2[None]
    return (loss, grad_x, *[grads[nm] for nm in names], *[deltas[nm] for nm in names],
            *[new_m[nm] for nm in names], *[new_v[nm] for nm in names])
```

```python
import functools

import jax
import jax.numpy as jnp
from jax import lax
from jax.experimental import pallas as pl
from jax.experimental.pallas import tpu as pltpu

f32 = jnp.float32
bf16 = jnp.bfloat16
MESH = pl.DeviceIdType.MESH
HI = lax.Precision.HIGHEST

NORM_EPS = 1e-6
LN_EPS = 1e-5
ADAM_LR, ADAM_B1, ADAM_B2, ADAM_EPS, ADAM_WD, ADAM_STEP = 0.001, 0.9, 0.999, 1e-08, 0.01, 10

LANES = 128
SUBLANES = 8
LORA_PAD = 256
CONV_HALO = 32
WKV_CHUNK = 64
WKV_HEADS = 4
WKV_STATE_HEADS = 8
WKV_PREC = lax.Precision.HIGH
SHARE_CHUNKS = 8
VMEM_LIMIT = 56 * 1024 * 1024


def _cparams(n_axes):
    return pltpu.CompilerParams(dimension_semantics=("arbitrary",) * n_axes, vmem_limit_bytes=VMEM_LIMIT)


def _tile(dim, target):
    best = None
    t = LANES
    while t <= min(dim, target):
        if dim % t == 0:
            best = t
        t += LANES
    return best if best is not None else dim


def _mm(a, b, prec=None):
    return lax.dot_general(a, b, (((1,), (0,)), ((), ())), precision=prec, preferred_element_type=f32)


def _mm_nt(a, b, prec=None):
    return lax.dot_general(a, b, (((1,), (1,)), ((), ())), precision=prec, preferred_element_type=f32)


def _mm_tn(a, b, prec=None):
    return lax.dot_general(a, b, (((0,), (0,)), ((), ())), precision=prec, preferred_element_type=f32)


@jax.custom_vjp
def _bmm(a, b):
    return _mm(a.astype(bf16), b.astype(bf16))


def _bmm_fwd(a, b):
    return _bmm(a, b), (a, b)


def _bmm_bwd(res, dc):
    a, b = res
    dcb = dc.astype(bf16)
    return _mm_nt(dcb, b.astype(bf16)), _mm_tn(a.astype(bf16), dcb)


_bmm.defvjp(_bmm_fwd, _bmm_bwd)


def _matmul(name, a, b, mode, out_dtype, tm_t=1024, tn_t=768, tk_t=1024):
    if mode == "nn":
        (M, K), (_, N) = a.shape, b.shape
    elif mode == "nt":
        (M, K), (N, _) = a.shape, b.shape
    else:
        (K, M), (_, N) = a.shape, b.shape
    tm, tn, tk = _tile(M, tm_t), _tile(N, tn_t), _tile(K, tk_t)
    nk = K // tk
    dot = {"nn": _mm, "nt": _mm_nt, "tn": _mm_tn}[mode]

    def body(a_ref, b_ref, o_ref, acc_ref):
        k = pl.program_id(2)

        @pl.when(k == 0)
        def _():
            acc_ref[...] = jnp.zeros_like(acc_ref)

        acc_ref[...] += dot(a_ref[...], b_ref[...])

        @pl.when(k == nk - 1)
        def _():
            o_ref[...] = acc_ref[...].astype(o_ref.dtype)

    a_spec = {"nn": pl.BlockSpec((tm, tk), lambda i, j, k: (i, k)),
              "nt": pl.BlockSpec((tm, tk), lambda i, j, k: (i, k)),
              "tn": pl.BlockSpec((tk, tm), lambda i, j, k: (k, i))}[mode]
    b_spec = {"nn": pl.BlockSpec((tk, tn), lambda i, j, k: (k, j)),
              "nt": pl.BlockSpec((tn, tk), lambda i, j, k: (j, k)),
              "tn": pl.BlockSpec((tk, tn), lambda i, j, k: (k, j))}[mode]
    return pl.pallas_call(
        body, name=name, grid=(M // tm, N // tn, nk),
        in_specs=[a_spec, b_spec],
        out_specs=pl.BlockSpec((tm, tn), lambda i, j, k: (i, j)),
        out_shape=jax.ShapeDtypeStruct((M, N), out_dtype),
        scratch_shapes=[pltpu.VMEM((tm, tn), f32)],
        compiler_params=_cparams(3),
    )(a, b)


def _row_spec(op, tm, ncol):
    arr, off, width, tiled = op
    if tiled:
        cw = width // ncol
        return pl.BlockSpec((tm, cw), lambda j, i: (i, off // cw + j))
    return pl.BlockSpec((tm, width), lambda j, i: (i, off // width))


def _param_spec(p, ncol):
    arr, tiled = p
    rows, width = arr.shape
    if tiled:
        return pl.BlockSpec((rows, width // ncol), lambda j, i: (0, j))
    return pl.BlockSpec((rows, width), lambda j, i: (0, 0))


def _row_fwd(name, fn, params, rows, outs, T, tm, ncol=1):
    npar, nrow = len(params), len(rows)

    def body(*refs):
        pv = [r[...] for r in refs[:npar]]
        rv = [r[...].astype(f32) for r in refs[npar:npar + nrow]]
        res = fn(*pv, *rv)
        for o_ref, val in zip(refs[npar + nrow:], res):
            o_ref[...] = val.astype(o_ref.dtype)

    return pl.pallas_call(
        body, name=name, grid=(ncol, T // tm),
        in_specs=[_param_spec(p, ncol) for p in params] + [_row_spec(r, tm, ncol) for r in rows],
        out_specs=[pl.BlockSpec((tm, w // ncol), lambda j, i: (i, j)) for w, _ in outs],
        out_shape=[jax.ShapeDtypeStruct((T, w), dt) for w, dt in outs],
        compiler_params=_cparams(2),
    )(*[p[0] for p in params], *[r[0] for r in rows])


def _row_bwd(name, fn, params, rows, cots, row_grads, T, tm, ncol=1):
    npar, nrow, ncot = len(params), len(rows), len(cots)
    want = [k for k, dt in enumerate(row_grads) if dt is not None]

    def body(*refs):
        pv = [r[...] for r in refs[:npar]]
        rv = [r[...].astype(f32) for r in refs[npar:npar + nrow]]
        cv = tuple(r[...].astype(f32) for r in refs[npar + nrow:npar + nrow + ncot])
        out_refs = refs[npar + nrow + ncot:]
        _, vjp = jax.vjp(fn, *pv, *rv)
        grads = vjp(cv)
        for o_ref, k in zip(out_refs[:len(want)], want):
            o_ref[...] = grads[npar + k].astype(o_ref.dtype)
        j, i = pl.program_id(0), pl.program_id(1)
        for o_ref, p, g in zip(out_refs[len(want):], params, grads[:npar]):
            first = (i == 0) if p[1] else jnp.logical_and(i == 0, j == 0)

            @pl.when(first)
            def _():
                o_ref[...] = jnp.zeros_like(o_ref)

            o_ref[...] += g

    def grad_spec(op):
        arr, off, width, tiled = op
        if tiled:
            return pl.BlockSpec((tm, width // ncol), lambda j, i: (i, j)), (T, width)
        return pl.BlockSpec((tm, width), lambda j, i: (i, j)), (T, width * ncol)

    gspecs = [grad_spec(rows[k]) for k in want]
    return pl.pallas_call(
        body, name=name, grid=(ncol, T // tm),
        in_specs=[_param_spec(p, ncol) for p in params] + [_row_spec(r, tm, ncol) for r in rows]
        + [_row_spec(c, tm, ncol) for c in cots],
        out_specs=[s for s, _ in gspecs] + [_param_spec(p, ncol) for p in params],
        out_shape=[jax.ShapeDtypeStruct(shp, row_grads[k]) for (_, shp), k in zip(gspecs, want)]
        + [jax.ShapeDtypeStruct(p[0].shape, f32) for p in params],
        compiler_params=_cparams(2),
    )(*[p[0] for p in params], *[r[0] for r in rows], *[c[0] for c in cots])


def _seg_sum(x, head):
    li = lax.broadcasted_iota(jnp.int32, (LANES, LANES), 0) // head
    lj = lax.broadcasted_iota(jnp.int32, (LANES, LANES), 1) // head
    q = (li == lj).astype(f32)
    parts = [_mm(x[:, s:s + LANES], q, HI) for s in range(0, x.shape[1], LANES)]
    return parts[0] if len(parts) == 1 else jnp.concatenate(parts, axis=1)


def _sigmoid(z):
    return 1.0 / (1.0 + jnp.exp(-z))


def _silu(z):
    return z * _sigmoid(z)


def _rms(g, x):
    return x * lax.rsqrt(jnp.mean(x * x, axis=-1, keepdims=True) + NORM_EPS) * g


def _fn_rms_pre(g, x):
    return (_rms(g, x),)


def _fn_lora(w0, wup, a0, aup, xl):
    qw = w0 + _bmm(jnp.tanh(xl), wup)
    qa = a0 + _bmm(xl, aup)
    return qw, qa


def _fn_rwkv_pre(head, k_k, k_a, xk, qw, qa):
    w_log = -(jnp.maximum(-qw, 0.0) + jnp.log(1.0 + jnp.exp(-jnp.abs(qw)))) - 0.5
    lw = -jnp.exp(w_log)
    a_sig = _sigmoid(qa)
    kk = xk * k_k
    kk = kk / jnp.maximum(jnp.sqrt(_seg_sum(kk * kk, head)), 1e-12)
    k_h = xk * (1.0 + (a_sig - 1.0) * k_a)
    return lw, k_h, -kk, kk * a_sig


def _fn_rwkv_post(head, lnx_g, lnx_b, r_k, y, r, k_h, v, g):
    inv = 1.0 / head
    mu = _seg_sum(y, head) * inv
    d = y - mu
    var = _seg_sum(d * d, head) * inv
    yn = d * lax.rsqrt(var + 1e-5 * head) * lnx_g + lnx_b
    bonus = _seg_sum(r * k_h * r_k, head) * v
    return ((yn + bonus) * _silu(g),)


def _fn_conv_ln(cln_g, cln_b, c):
    mu = jnp.mean(c, axis=-1, keepdims=True)
    d = c - mu
    var = jnp.mean(d * d, axis=-1, keepdims=True)
    return (_silu(d * lax.rsqrt(var + LN_EPS) * cln_g + cln_b),)


def _fn_conv_post(b_pw2, c2, g):
    return ((c2 + b_pw2) * _silu(g),)


def _post(out, x, tgt, g, T, D, tm):
    def body(g_ref, o_ref, x_ref, t_ref, dout_ref, gx_ref, loss_ref, dg_ref):
        i = pl.program_id(0)
        o, vjp = jax.vjp(_rms, g_ref[...], o_ref[...])
        err = x_ref[...] + o - t_ref[...]
        d_y = err * (1.0 / D)
        dg, d_out = vjp(d_y)
        dout_ref[...] = d_out.astype(dout_ref.dtype)
        gx_ref[...] = d_y

        @pl.when(i == 0)
        def _():
            loss_ref[...] = jnp.zeros_like(loss_ref)
            dg_ref[...] = jnp.zeros_like(dg_ref)

        loss_ref[...] += jnp.sum(err * err, keepdims=True) * (0.5 / D)
        dg_ref[...] += dg

    row = pl.BlockSpec((tm, D), lambda i: (i, 0))
    vec = pl.BlockSpec((1, D), lambda i: (0, 0))
    return pl.pallas_call(
        body, name="post_loss", grid=(T // tm,),
        in_specs=[vec, row, row, row],
        out_specs=[row, row, pl.BlockSpec((1, 1), lambda i: (0, 0)), vec],
        out_shape=[jax.ShapeDtypeStruct((T, D), bf16), jax.ShapeDtypeStruct((T, D), f32),
                   jax.ShapeDtypeStruct((1, 1), f32), jax.ShapeDtypeStruct((1, D), f32)],
        compiler_params=_cparams(1),
    )(g, out, x, tgt)


def _rms_pre_bwd(x, g, dh, gx_res, T, D, tm):
    def body(g_ref, x_ref, dh_ref, res_ref, dx_ref, dg_ref):
        i = pl.program_id(0)
        _, vjp = jax.vjp(_rms, g_ref[...], x_ref[...])
        dg, dx = vjp(dh_ref[...].astype(f32))
        dx_ref[...] = dx + res_ref[...]

        @pl.when(i == 0)
        def _():
            dg_ref[...] = jnp.zeros_like(dg_ref)

        dg_ref[...] += dg

    row = pl.BlockSpec((tm, D), lambda i: (i, 0))
    vec = pl.BlockSpec((1, D), lambda i: (0, 0))
    return pl.pallas_call(
        body, name="rms_pre_bwd", grid=(T // tm,),
        in_specs=[vec, row, row, row], out_specs=[row, vec],
        out_shape=[jax.ShapeDtypeStruct((T, D), f32), jax.ShapeDtypeStruct((1, D), f32)],
        compiler_params=_cparams(1),
    )(g, x, dh, gx_res)


def _prev_rows(cur, halo_ref, first):
    top = jnp.where(first, 0.0, halo_ref[SUBLANES - 1:SUBLANES, :])
    rolled = pltpu.roll(cur, 1, 0)
    rid = lax.broadcasted_iota(jnp.int32, cur.shape, 0)
    return jnp.where(rid == 0, top, rolled)


def _shift_fwd(name, proj, off, width, mu, T, tm):
    cw = _tile(width, 512)
    ncol, cb = width // cw, off // cw
    hb = tm // SUBLANES

    def body(mu_ref, cur_ref, halo_ref, o_ref):
        i = pl.program_id(1)
        cur = cur_ref[...]
        prev = _prev_rows(cur, halo_ref, i == 0)
        o_ref[...] = cur + (prev - cur) * mu_ref[...]

    return pl.pallas_call(
        body, name=name, grid=(ncol, T // tm),
        in_specs=[pl.BlockSpec((1, cw), lambda j, i: (0, j)),
                  pl.BlockSpec((tm, cw), lambda j, i: (i, cb + j)),
                  pl.BlockSpec((SUBLANES, cw), lambda j, i: (jnp.maximum(i * hb - 1, 0), cb + j))],
        out_specs=pl.BlockSpec((tm, cw), lambda j, i: (i, j)),
        out_shape=jax.ShapeDtypeStruct((T, width), f32),
        compiler_params=_cparams(2),
    )(mu, proj, proj)


def _shift_bwd(name, proj, off, width, mu, dxs, T, tm):
    cw = _tile(width, 512)
    ncol, cb = width // cw, off // cw
    hb, nt = tm // SUBLANES, T // tm

    def body(mu_ref, cur_ref, halo_ref, d_ref, dnext_ref, o_ref, dmu_ref):
        i = pl.program_id(1)
        cur = cur_ref[...]
        prev = _prev_rows(cur, halo_ref, i == 0)
        d = d_ref[...]
        bottom = jnp.where(i == nt - 1, 0.0, dnext_ref[0:1, :])
        rid = lax.broadcasted_iota(jnp.int32, d.shape, 0)
        d_next = jnp.where(rid == tm - 1, bottom, pltpu.roll(d, tm - 1, 0))
        mu_v = mu_ref[...]
        o_ref[...] = (d * (1.0 - mu_v) + d_next * mu_v).astype(o_ref.dtype)

        @pl.when(i == 0)
        def _():
            dmu_ref[...] = jnp.zeros_like(dmu_ref)

        dmu_ref[...] += jnp.sum(d * (prev - cur), axis=0, keepdims=True)

    return pl.pallas_call(
        body, name=name, grid=(ncol, nt),
        in_specs=[pl.BlockSpec((1, cw), lambda j, i: (0, j)),
                  pl.BlockSpec((tm, cw), lambda j, i: (i, cb + j)),
                  pl.BlockSpec((SUBLANES, cw), lambda j, i: (jnp.maximum(i * hb - 1, 0), cb + j)),
                  pl.BlockSpec((tm, cw), lambda j, i: (i, j)),
                  pl.BlockSpec((SUBLANES, cw), lambda j, i: (jnp.minimum((i + 1) * hb, nt * hb - 1), j))],
        out_specs=[pl.BlockSpec((tm, cw), lambda j, i: (i, j)), pl.BlockSpec((1, cw), lambda j, i: (0, j))],
        out_shape=[jax.ShapeDtypeStruct((T, width), bf16), jax.ShapeDtypeStruct((1, width), f32)],
        compiler_params=_cparams(2),
    )(mu, proj, proj, dxs, dxs)


def _conv_fwd(proj, off_v, off_g, width, conv_w, conv_b, ktaps, T, tm):
    cw = _tile(width, 512)
    ncol = width // cw
    hb = tm // CONV_HALO
    lead = CONV_HALO - (ktaps - 1)

    def body(w_ref, b_ref, v_ref, g_ref, hv_ref, hg_ref, o_ref, u_ref):
        i = pl.program_id(1)
        halo = hv_ref[...] * _sigmoid(hg_ref[...])
        u_ref[0:CONV_HALO, :] = jnp.where(i == 0, 0.0, halo)
        u_ref[CONV_HALO:, :] = v_ref[...] * _sigmoid(g_ref[...])
        acc = jnp.zeros((tm, cw), f32) + b_ref[...]
        for j in range(ktaps):
            acc = acc + u_ref[pl.ds(lead + j, tm), :] * w_ref[j:j + 1, :]
        o_ref[...] = acc

    def tile(off):
        return pl.BlockSpec((tm, cw), lambda j, i: (i, off // cw + j))

    def halo(off):
        return pl.BlockSpec((CONV_HALO, cw), lambda j, i: (jnp.maximum(i * hb - 1, 0), off // cw + j))

    return pl.pallas_call(
        body, name="conv_fwd", grid=(ncol, T // tm),
        in_specs=[pl.BlockSpec((CONV_HALO, cw), lambda j, i: (0, j)), pl.BlockSpec((1, cw), lambda j, i: (0, j)),
                  tile(off_v), tile(off_g), halo(off_v), halo(off_g)],
        out_specs=pl.BlockSpec((tm, cw), lambda j, i: (i, j)),
        out_shape=jax.ShapeDtypeStruct((T, width), f32),
        scratch_shapes=[pltpu.VMEM((CONV_HALO + tm, cw), f32)],
        compiler_params=_cparams(2),
    )(conv_w, conv_b, proj, proj, proj, proj)


def _conv_bwd(proj, off_v, off_g, width, conv_w, dc, ktaps, T, tm):
    cw = _tile(width, 512)
    ncol = width // cw
    hb, nt = tm // CONV_HALO, T // tm
    lead = CONV_HALO - (ktaps - 1)

    def body(w_ref, v_ref, g_ref, hv_ref, hg_ref, dc_ref, dcn_ref, dv_ref, dg_ref, dw_ref, db_ref, u_ref, d_ref):
        i = pl.program_id(1)
        halo = hv_ref[...] * _sigmoid(hg_ref[...])
        u_ref[0:CONV_HALO, :] = jnp.where(i == 0, 0.0, halo)
        sig = _sigmoid(g_ref[...])
        gv = v_ref[...]
        u_ref[CONV_HALO:, :] = gv * sig
        dcur = dc_ref[...]
        d_ref[0:tm, :] = dcur
        d_ref[tm:, :] = jnp.where(i == nt - 1, 0.0, dcn_ref[...])

        @pl.when(i == 0)
        def _():
            dw_ref[...] = jnp.zeros_like(dw_ref)
            db_ref[...] = jnp.zeros_like(db_ref)

        du = jnp.zeros((tm, cw), f32)
        for j in range(ktaps):
            du = du + d_ref[pl.ds(ktaps - 1 - j, tm), :] * w_ref[j:j + 1, :]
            dw_ref[j:j + 1, :] += jnp.sum(u_ref[pl.ds(lead + j, tm), :] * dcur, axis=0, keepdims=True)
        db_ref[...] += jnp.sum(dcur, axis=0, keepdims=True)
        dv_ref[...] = (du * sig).astype(dv_ref.dtype)
        dg_ref[...] = (du * gv * sig * (1.0 - sig)).astype(dg_ref.dtype)

    def tile(off):
        return pl.BlockSpec((tm, cw), lambda j, i: (i, off // cw + j))

    def halo(off):
        return pl.BlockSpec((CONV_HALO, cw), lambda j, i: (jnp.maximum(i * hb - 1, 0), off // cw + j))

    return pl.pallas_call(
        body, name="conv_bwd", grid=(ncol, nt),
        in_specs=[pl.BlockSpec((CONV_HALO, cw), lambda j, i: (0, j)),
                  tile(off_v), tile(off_g), halo(off_v), halo(off_g),
                  pl.BlockSpec((tm, cw), lambda j, i: (i, j)),
                  pl.BlockSpec((CONV_HALO, cw), lambda j, i: (jnp.minimum((i + 1) * hb, nt * hb - 1), j))],
        out_specs=[pl.BlockSpec((tm, cw), lambda j, i: (i, j)), pl.BlockSpec((tm, cw), lambda j, i: (i, j)),
                   pl.BlockSpec((CONV_HALO, cw), lambda j, i: (0, j)), pl.BlockSpec((1, cw), lambda j, i: (0, j))],
        out_shape=[jax.ShapeDtypeStruct((T, width), bf16), jax.ShapeDtypeStruct((T, width), bf16),
                   jax.ShapeDtypeStruct((CONV_HALO, width), f32), jax.ShapeDtypeStruct((1, width), f32)],
        scratch_shapes=[pltpu.VMEM((CONV_HALO + tm, cw), f32), pltpu.VMEM((tm + CONV_HALO, cw), f32)],
        compiler_params=_cparams(2),
    )(conv_w, proj, proj, proj, proj, dc, dc)


def _each(f, *lists):
    return [f(*xs) for xs in zip(*lists)]


def _wkv_local(r, lw, k, v, a, b):
    C = r[0].shape[0]
    P = WKV_PREC
    row = lax.broadcasted_iota(jnp.int32, (C, C), 0)
    col = lax.broadcasted_iota(jnp.int32, (C, C), 1)
    incl, strict = row >= col, row > col
    tri = incl.astype(f32)
    zero = jnp.zeros((C, C), f32)
    G = _each(lambda x: _mm(tri, x, HI), lw)
    to_end = _each(lambda x, g: jnp.exp(jnp.sum(x, axis=0, keepdims=True) - g), lw, G)
    e_g = _each(jnp.exp, G)
    e_ng = _each(lambda g: jnp.exp(-g), G)
    At = _each(lambda x, g, w: x * jnp.exp(g - w), a, G, lw)
    Rt = _each(jnp.multiply, r, e_g)
    Kt = _each(jnp.multiply, k, e_ng)
    Bt = _each(jnp.multiply, b, e_ng)
    L = _each(lambda x, y: jnp.where(strict, _mm_nt(x, y, P), zero), At, Bt)
    M = _each(lambda x, y: jnp.where(strict, _mm_nt(x, y, P), zero), At, Kt)
    Pb = _each(lambda x, y: jnp.where(incl, _mm_nt(x, y, P), zero), Rt, Bt)
    Pk = _each(lambda x, y: jnp.where(incl, _mm_nt(x, y, P), zero), Rt, Kt)
    W = At
    U = _each(lambda m, x: _mm(m, x, P), M, v)
    Lp = L
    n = 1
    while n < C:
        W = _each(lambda x, l: x + _mm(l, x, P), W, Lp)
        U = _each(lambda x, l: x + _mm(l, x, P), U, Lp)
        n *= 2
        if n < C:
            Lp = _each(lambda l: _mm(l, l, P), Lp)
    Y0 = _each(lambda m, x: _mm(m, x, P), Pk, v)
    Bend = _each(jnp.multiply, b, to_end)
    Z = _each(lambda x, y, e: _mm_tn(x, y * e, P), v, k, to_end)
    return W, U, Rt, Pb, Y0, Bend, Z


def _wkv_state(S0, W, U, Rt, Pb, Bend, lw, Y0, Z):
    P = WKV_PREC
    X = _each(lambda w, s, u: _mm_nt(w, s, P) + u, W, S0, U)
    yS = _each(lambda x, s: _mm_nt(x, s, P), Rt, S0)
    y = _each(lambda p, x, a, c: _mm(p, x, P) + a + c, Pb, X, yS, Y0)
    S1 = _each(lambda s, w, x, e, z: s * jnp.exp(jnp.sum(w, axis=0, keepdims=True)) + _mm_tn(x, e, P) + z,
               S0, lw, X, Bend, Z)
    return y, S1


def _wkv_dims(head, T, RW, heads_per_step):
    C = min(WKV_CHUNK, T)
    nh = RW // head
    hb = min(heads_per_step, nh)
    return C, nh, hb, hb * head, T // C


def _heads(ref, hb, head):
    return [ref[:, h * head:(h + 1) * head] for h in range(hb)]


def _put_heads(ref, vals, head):
    for h, val in enumerate(vals):
        ref[:, h * head:(h + 1) * head] = val


def _wkv_local_fwd(r, lw, k, v, a, b, head, T, RW):
    C, nh, hb, bw, nc = _wkv_dims(head, T, RW, WKV_HEADS)

    def body(*refs):
        ins, outs = refs[:6], refs[6:]
        res = _wkv_local(*[_heads(x, hb, head) for x in ins])
        for o_ref, vals in zip(outs[:6], res[:6]):
            _put_heads(o_ref, vals, head)
        for h in range(hb):
            outs[6][0, h] = res[6][h]

    blk = pl.BlockSpec((C, bw), lambda g, c: (c, g))
    sq = pl.BlockSpec((1, hb, head, head), lambda g, c: (c, g, 0, 0))
    return pl.pallas_call(
        body, name="wkv_local", grid=(nh // hb, nc),
        in_specs=[blk] * 6, out_specs=[blk] * 6 + [sq],
        out_shape=[jax.ShapeDtypeStruct((T, RW), f32)] * 6 + [jax.ShapeDtypeStruct((nc, nh, head, head), f32)],
        compiler_params=_cparams(2),
    )(r, lw, k, v, a, b)


def _wkv_state_fwd(W, U, Rt, Pb, Bend, lw, Y0, Z, head, T, RW):
    C, nh, hb, bw, nc = _wkv_dims(head, T, RW, WKV_STATE_HEADS)

    def body(w_ref, u_ref, rt_ref, pb_ref, be_ref, lw_ref, y0_ref, z_ref, y_ref, st_ref, s_ref):
        @pl.when(pl.program_id(1) == 0)
        def _():
            s_ref[...] = jnp.zeros_like(s_ref)

        S0 = [s_ref[h] for h in range(hb)]
        for h in range(hb):
            st_ref[0, h] = S0[h]
        rows = [_heads(x, hb, head) for x in (w_ref, u_ref, rt_ref, pb_ref, be_ref, lw_ref, y0_ref)]
        y, S1 = _wkv_state(S0, *rows, [z_ref[0, h] for h in range(hb)])
        _put_heads(y_ref, y, head)
        for h in range(hb):
            s_ref[h] = S1[h]

    blk = pl.BlockSpec((C, bw), lambda g, c: (c, g))
    sq = pl.BlockSpec((1, hb, head, head), lambda g, c: (c, g, 0, 0))
    return pl.pallas_call(
        body, name="wkv_state", grid=(nh // hb, nc),
        in_specs=[blk] * 7 + [sq], out_specs=[blk, sq],
        out_shape=[jax.ShapeDtypeStruct((T, RW), f32), jax.ShapeDtypeStruct((nc, nh, head, head), f32)],
        scratch_shapes=[pltpu.VMEM((hb, head, head), f32)],
        compiler_params=_cparams(2),
    )(W, U, Rt, Pb, Bend, lw, Y0, Z)


def _wkv_state_bwd(W, U, Rt, Pb, Bend, lw, Y0, Z, states, dy, head, T, RW):
    C, nh, hb, bw, nc = _wkv_dims(head, T, RW, WKV_STATE_HEADS)

    def body(w_ref, u_ref, rt_ref, pb_ref, be_ref, lw_ref, y0_ref, z_ref, st_ref, dy_ref,
             dw_ref, du_ref, drt_ref, dpb_ref, dbe_ref, dlw_ref, dz_ref, ds_ref):
        @pl.when(pl.program_id(1) == 0)
        def _():
            ds_ref[...] = jnp.zeros_like(ds_ref)

        dS1 = [ds_ref[h] for h in range(hb)]
        for h in range(hb):
            dz_ref[0, h] = dS1[h]
        rows = [_heads(x, hb, head) for x in (w_ref, u_ref, rt_ref, pb_ref, be_ref, lw_ref)]
        Y0 = _heads(y0_ref, hb, head)
        Zs = [z_ref[0, h] for h in range(hb)]
        _, vjp = jax.vjp(lambda s0, *rw: _wkv_state(s0, *rw, Y0, Zs), [st_ref[0, h] for h in range(hb)], *rows)
        grads = vjp((_heads(dy_ref, hb, head), dS1))
        for o_ref, vals in zip((dw_ref, du_ref, drt_ref, dpb_ref, dbe_ref, dlw_ref), grads[1:]):
            _put_heads(o_ref, vals, head)
        for h in range(hb):
            ds_ref[h] = grads[0][h]

    blk = pl.BlockSpec((C, bw), lambda g, c: (nc - 1 - c, g))
    sq = pl.BlockSpec((1, hb, head, head), lambda g, c: (nc - 1 - c, g, 0, 0))
    return pl.pallas_call(
        body, name="wkv_state_bwd", grid=(nh // hb, nc),
        in_specs=[blk] * 7 + [sq, sq, blk], out_specs=[blk] * 6 + [sq],
        out_shape=[jax.ShapeDtypeStruct((T, RW), f32)] * 6 + [jax.ShapeDtypeStruct((nc, nh, head, head), f32)],
        scratch_shapes=[pltpu.VMEM((hb, head, head), f32)],
        compiler_params=_cparams(2),
    )(W, U, Rt, Pb, Bend, lw, Y0, Z, states, dy)


def _wkv_local_bwd(r, lw, k, v, a, b, cots, d_lw_x, dr_x, dk_x, dv_x, head, T, RW):
    C, nh, hb, bw, nc = _wkv_dims(head, T, RW, WKV_HEADS)

    def body(*refs):
        ins, cot_refs, add_refs, outs = refs[:6], refs[6:13], refs[13:17], refs[17:]
        _, vjp = jax.vjp(_wkv_local, *[_heads(x, hb, head) for x in ins])
        cts = [_heads(x, hb, head) for x in cot_refs[:6]] + [[cot_refs[6][0, h] for h in range(hb)]]
        dr, dlw, dk, dv, da, db = vjp(tuple(cts))
        dlw_x, drx, dkx, dvx = [_heads(x, hb, head) for x in add_refs]
        _put_heads(outs[0], _each(jnp.add, dr, drx), head)
        _put_heads(outs[1], _each(jnp.add, dlw, dlw_x), head)
        _put_heads(outs[2], _each(jnp.add, dk, dkx), head)
        _put_heads(outs[3], _each(jnp.add, dv, dvx), head)
        _put_heads(outs[4], da, head)
        _put_heads(outs[5], db, head)

    blk = pl.BlockSpec((C, bw), lambda g, c: (c, g))
    sq = pl.BlockSpec((1, hb, head, head), lambda g, c: (c, g, 0, 0))
    return pl.pallas_call(
        body, name="wkv_local_bwd", grid=(nh // hb, nc),
        in_specs=[blk] * 12 + [sq] + [blk] * 4, out_specs=[blk] * 6,
        out_shape=[jax.ShapeDtypeStruct((T, RW), f32)] * 6,
        compiler_params=_cparams(2),
    )(r, lw, k, v, a, b, *cots, d_lw_x, dr_x, dk_x, dv_x)


def _rows_tile(R, row_bytes, budget, mult=SUBLANES):
    best = None
    t = mult
    while t <= R:
        if R % t == 0 and t * row_bytes <= budget:
            best = t
        t += mult
    return best if best is not None else R


def _sum8(name, parts):
    _, R, W = parts.shape
    tr = _rows_tile(R, 8 * W * 4, 4 << 20, 2 * SUBLANES)

    def body(p_ref, o_ref):
        acc = p_ref[0].astype(f32)
        for d in range(1, 8):
            acc = acc + p_ref[d].astype(f32)
        o_ref[...] = acc

    return pl.pallas_call(
        body, name=name, grid=(R // tr,),
        in_specs=[pl.BlockSpec((8, tr, W), lambda i: (0, i, 0))],
        out_specs=pl.BlockSpec((tr, W), lambda i: (i, 0)),
        out_shape=jax.ShapeDtypeStruct((R, W), f32),
        compiler_params=_cparams(1),
    )(parts)


def _adamw(name, w, g, m, v):
    R, W = w.shape
    tr = _rows_tile(R, W * 4, 1 << 20)

    def body(w_ref, g_ref, m_ref, v_ref, d_ref, nm_ref, nv_ref):
        g_v = g_ref[...]
        nm = ADAM_B1 * m_ref[...] + (1.0 - ADAM_B1) * g_v
        nv = ADAM_B2 * v_ref[...] + (1.0 - ADAM_B2) * (g_v * g_v)
        m_hat = nm / (1.0 - ADAM_B1 ** ADAM_STEP)
        v_hat = nv / (1.0 - ADAM_B2 ** ADAM_STEP)
        d_ref[...] = -ADAM_LR * (m_hat / (jnp.sqrt(v_hat) + ADAM_EPS) + ADAM_WD * w_ref[...])
        nm_ref[...] = nm
        nv_ref[...] = nv

    blk = pl.BlockSpec((tr, W), lambda i: (i, 0))
    return pl.pallas_call(
        body, name=name, grid=(R // tr,),
        in_specs=[blk] * 4, out_specs=[blk] * 3,
        out_shape=[jax.ShapeDtypeStruct((R, W), f32)] * 3,
        compiler_params=_cparams(1),
    )(w, g, m, v)


ANY = pl.BlockSpec(memory_space=pl.ANY)


def _place():
    return lax.axis_index("x"), lax.axis_index("y"), lax.axis_index("c")


def _gather_chips(arrays):
    n = len(arrays)

    def body(*refs):
        ins, outs = refs[:n], refs[n:2 * n]
        send_sems, recv_sems, local_sems = refs[2 * n:]
        x, y, c = _place()
        mine = 2 * x + y
        chips = [(1 - x, y), (x, 1 - y), (1 - x, 1 - y)]
        local = [pltpu.make_async_copy(ins[a], outs[a].at[mine], local_sems.at[a]) for a in range(n)]
        for cp in local:
            cp.start()
        sends = []
        for a in range(n):
            for j, (px, py) in enumerate(chips):
                cp = pltpu.make_async_remote_copy(
                    src_ref=ins[a], dst_ref=outs[a].at[mine], send_sem=send_sems.at[3 * a + j],
                    recv_sem=recv_sems.at[3 * a + j], device_id=(px, py, c), device_id_type=MESH)
                cp.start()
                sends.append(cp)
        for a in range(n):
            for j, (px, py) in enumerate(chips):
                pltpu.make_async_remote_copy(
                    src_ref=ins[a], dst_ref=outs[a].at[2 * px + py], send_sem=send_sems.at[3 * a + j],
                    recv_sem=recv_sems.at[3 * a + j], device_id=(px, py, c), device_id_type=MESH).wait_recv()
        for cp in sends:
            cp.wait_send()
        for cp in local:
            cp.wait()

    return pl.pallas_call(
        body, name="gather_weights",
        in_specs=[ANY] * n, out_specs=[ANY] * n,
        out_shape=[jax.ShapeDtypeStruct((4,) + a.shape, a.dtype) for a in arrays],
        scratch_shapes=[pltpu.SemaphoreType.DMA((3 * n,)), pltpu.SemaphoreType.DMA((3 * n,)),
                        pltpu.SemaphoreType.DMA((n,))],
    )(*arrays)


def _exchange_pieces(pieces, whole):
    n, m = len(pieces), len(whole)
    tot = n + m

    def body(*refs):
        ins, outs = refs[:tot], refs[tot:2 * tot]
        send_sems, recv_sems, local_sems = refs[2 * tot:]
        x, y, c = _place()
        mine = 4 * x + 2 * y + c
        peers = [(x ^ (k >> 2), y ^ ((k >> 1) & 1), c ^ (k & 1)) for k in range(1, 8)]

        def src(a, idx):
            return ins[a].at[idx] if a < n else ins[a]

        local = [pltpu.make_async_copy(src(a, mine), outs[a].at[mine], local_sems.at[a]) for a in range(tot)]
        for cp in local:
            cp.start()
        sends = []
        for a in range(tot):
            for j, (px, py, pc) in enumerate(peers):
                cp = pltpu.make_async_remote_copy(
                    src_ref=src(a, 4 * px + 2 * py + pc), dst_ref=outs[a].at[mine], send_sem=send_sems.at[7 * a + j],
                    recv_sem=recv_sems.at[7 * a + j], device_id=(px, py, pc), device_id_type=MESH)
                cp.start()
                sends.append(cp)
        for a in range(tot):
            for j, (px, py, pc) in enumerate(peers):
                pltpu.make_async_remote_copy(
                    src_ref=src(a, mine), dst_ref=outs[a].at[4 * px + 2 * py + pc], send_sem=send_sems.at[7 * a + j],
                    recv_sem=recv_sems.at[7 * a + j], device_id=(px, py, pc), device_id_type=MESH).wait_recv()
        for cp in sends:
            cp.wait_send()
        for cp in local:
            cp.wait()

    shapes = [jax.ShapeDtypeStruct(a.shape, a.dtype) for a in pieces]
    shapes += [jax.ShapeDtypeStruct((8,) + a.shape, a.dtype) for a in whole]
    return pl.pallas_call(
        body, name="exchange_grads",
        in_specs=[ANY] * tot, out_specs=[ANY] * tot, out_shape=shapes,
        scratch_shapes=[pltpu.SemaphoreType.DMA((7 * tot,)), pltpu.SemaphoreType.DMA((7 * tot,)),
                        pltpu.SemaphoreType.DMA((tot,))],
    )(*pieces, *whole)


def _share_sibling(arrays):
    n = len(arrays)
    parts = []
    for a, arr in enumerate(arrays):
        k = SHARE_CHUNKS if arr.shape[0] % (SHARE_CHUNKS * SUBLANES) == 0 else 1
        step = arr.shape[0] // k
        parts += [(a, q * step, step) for q in range(k)]
    npart = len(parts)

    def body(*refs):
        ins, outs = refs[:n], refs[n:2 * n]
        send_sems, recv_sems, local_sems = refs[2 * n:]
        x, y, c = _place()
        sib = (x, y, 1 - c)
        local = [pltpu.make_async_copy(ins[a], outs[a].at[c], local_sems.at[a]) for a in range(n)]
        for cp in local:
            cp.start()

        def copy(p, slot):
            a, r0, nr = parts[p]
            return pltpu.make_async_remote_copy(
                src_ref=ins[a].at[pl.ds(r0, nr)], dst_ref=outs[a].at[slot, pl.ds(r0, nr)], send_sem=send_sems.at[p],
                recv_sem=recv_sems.at[p], device_id=sib, device_id_type=MESH)

        sends = [copy(p, c) for p in range(npart)]
        for cp in sends:
            cp.start()
        for p in range(npart):
            copy(p, 1 - c).wait_recv()
        for cp in sends:
            cp.wait_send()
        for cp in local:
            cp.wait()

    return pl.pallas_call(
        body, name="share_sibling",
        in_specs=[ANY] * n, out_specs=[ANY] * n,
        out_shape=[jax.ShapeDtypeStruct((2,) + a.shape, a.dtype) for a in arrays],
        scratch_shapes=[pltpu.SemaphoreType.DMA((npart,)), pltpu.SemaphoreType.DMA((npart,)),
                        pltpu.SemaphoreType.DMA((n,))],
    )(*arrays)


def kernel(x, norm_pre_g, w_in, mu_shift, w0, w_lora_up, a0, a_lora_up, k_k, k_a, r_k, lnx_g, lnx_b, conv_w, conv_b, cln_g, cln_b, w_pw2, b_pw2, w_out, norm_post_g, loss_target, m_norm_pre_g, m_w_in, m_mu_shift, m_w0, m_w_lora_up, m_a0, m_a_lora_up, m_k_k, m_k_a, m_r_k, m_lnx_g, m_lnx_b, m_conv_w, m_conv_b, m_cln_g, m_cln_b, m_w_pw2, m_b_pw2, m_w_out, m_norm_post_g, v_norm_pre_g, v_w_in, v_mu_shift, v_w0, v_w_lora_up, v_a0, v_a_lora_up, v_k_k, v_k_a, v_r_k, v_lnx_g, v_lnx_b, v_conv_w, v_conv_b, v_cln_g, v_cln_b, v_w_pw2, v_b_pw2, v_w_out, v_norm_post_g):
    _, T, D = x.shape
    RW = w0.shape[0]
    CW = conv_b.shape[0]
    head = r_k.shape[1]
    lora = w_lora_up.shape[0]
    ktaps = conv_w.shape[0]
    assert RW == CW and 2 * lora <= LORA_PAD and ktaps - 1 <= CONV_HALO
    n_in = 3 * RW + 2 * lora + RW + 3 * CW
    shard = n_in // 4
    PW = 7 * RW + LORA_PAD
    off_l = 7 * RW
    tm = min(256, T // 2)
    tm_wide = min(128, T // 2)
    row = lambda vec: vec.reshape(1, -1)
    x2, tgt2 = x[0], loss_target[0]

    g_win, g_wup, g_aup, g_cw, g_pw2, g_wout = _gather_chips(
        [w_in.astype(bf16), w_lora_up, a_lora_up, conv_w, w_pw2.astype(bf16), w_out.astype(bf16)])
    cat_cols = lambda g: jnp.concatenate([g[s] for s in range(4)], axis=1)
    win_full = cat_cols(g_win)
    lo = 3 * RW
    wp = jnp.concatenate([win_full[:, :lo], win_full[:, lo + 2 * lora:], win_full[:, lo:lo + 2 * lora],
                          jnp.zeros((D, LORA_PAD - 2 * lora), bf16)], axis=1)
    wup_full, aup_full, cw_full = cat_cols(g_wup), cat_cols(g_aup), cat_cols(g_cw)
    zl = lambda n: jnp.zeros((n, RW), f32)
    wup_p = jnp.concatenate([wup_full, zl(LORA_PAD - lora)], axis=0)
    aup_p = jnp.concatenate([zl(lora), aup_full, zl(LORA_PAD - 2 * lora)], axis=0)
    cw_p = jnp.concatenate([cw_full, jnp.zeros((CONV_HALO - ktaps, CW), f32)], axis=0)
    pw2_full = g_pw2.reshape(CW, CW)
    wout_full = g_wout.reshape(RW + CW, D)
    mu_r, mu_k, mu_v = (row(mu_shift[s * RW:(s + 1) * RW]) for s in range(3))
    mu_l = row(jnp.concatenate([mu_shift[3 * RW:], jnp.zeros((LORA_PAD - 2 * lora,), f32)]))

    npg = row(norm_pre_g)
    (h,) = _row_fwd("rms_pre", _fn_rms_pre, [(npg, False)], [(x2, 0, D, False)], [(D, bf16)], T, tm)
    proj = _matmul("proj", h, wp, "nn", f32)
    xs_r = _shift_fwd("shift_r", proj, 0, RW, mu_r, T, tm)
    xs_k = _shift_fwd("shift_k", proj, RW, RW, mu_k, T, tm)
    xs_v = _shift_fwd("shift_v", proj, 2 * RW, RW, mu_v, T, tm)
    xs_l = _shift_fwd("shift_l", proj, off_l, LORA_PAD, mu_l, T, tm)
    lora_params = [(row(w0), False), (wup_p, False), (row(a0), False), (aup_p, False)]
    qw, qa = _row_fwd("lora_up", _fn_lora, lora_params, [(xs_l, 0, LORA_PAD, False)], [(RW, f32), (RW, f32)], T, tm)
    ncol = RW // _tile(RW, 512)
    fn_pre = functools.partial(_fn_rwkv_pre, head)
    pre_params = [(row(k_k), True), (row(k_a), True)]
    pre_rows = [(xs_k, 0, RW, True), (qw, 0, RW, True), (qa, 0, RW, True)]
    lw, k_h, a_rec, b_rec = _row_fwd("rwkv_pre", fn_pre, pre_params, pre_rows, [(RW, f32)] * 4, T, tm, ncol)
    wkv_in = (xs_r, lw, k_h, xs_v, a_rec, b_rec)
    c_w, c_u, c_rt, c_pb, c_y0, c_bend, c_z = _wkv_local_fwd(*wkv_in, head, T, RW)
    wkv_loc = (c_w, c_u, c_rt, c_pb, c_bend, lw, c_y0, c_z)
    y_wkv, states = _wkv_state_fwd(*wkv_loc, head, T, RW)
    fn_post = functools.partial(_fn_rwkv_post, head)
    post_params = [(row(lnx_g), True), (row(lnx_b), True), (r_k.reshape(1, RW), True)]
    post_rows = [(y_wkv, 0, RW, True), (xs_r, 0, RW, True), (k_h, 0, RW, True), (xs_v, 0, RW, True),
                 (proj, 3 * RW, RW, True)]
    (y_rwkv,) = _row_fwd("rwkv_post", fn_post, post_params, post_rows, [(RW, bf16)], T, tm, ncol)

    c_pre = _conv_fwd(proj, 4 * RW, 5 * RW, CW, cw_p, row(conv_b), ktaps, T, tm)
    ln_params = [(row(cln_g), False), (row(cln_b), False)]
    (c_act,) = _row_fwd("conv_ln", _fn_conv_ln, ln_params, [(c_pre, 0, CW, False)], [(CW, bf16)], T, tm)
    c2 = _matmul("pw2", c_act, pw2_full, "nn", f32)
    cpost_params = [(row(b_pw2), True)]
    cpost_rows = [(c2, 0, CW, True), (proj, 6 * RW, CW, True)]
    (y_conv,) = _row_fwd("conv_post", _fn_conv_post, cpost_params, cpost_rows, [(CW, bf16)], T, tm, ncol)

    mix = jnp.concatenate([y_rwkv, y_conv], axis=1)
    out = _matmul("out_proj", mix, wout_full, "nn", f32)
    d_out, gx_res, loss_part, g_npost = _post(out, x2, tgt2, row(norm_post_g), T, D, tm_wide)

    g_wout_full = _matmul("d_w_out", mix, d_out, "tn", f32)
    d_mix = _matmul("d_mix", d_out, wout_full, "nt", f32)

    d_c2, d_gconv, g_bpw2 = _row_bwd("conv_post_bwd", _fn_conv_post, cpost_params, cpost_rows,
                                      [(d_mix, RW, CW, True)], [bf16, bf16], T, tm, ncol)
    g_pw2_full = _matmul("d_w_pw2", c_act, d_c2, "tn", f32)
    d_cact = _matmul("d_c_act", d_c2, pw2_full, "nt", f32)
    d_cpre, g_clng, g_clnb = _row_bwd("conv_ln_bwd", _fn_conv_ln, ln_params, [(c_pre, 0, CW, False)],
                                      [(d_cact, 0, CW, False)], [f32], T, tm)
    d_gluv, d_glug, g_cw_p, g_cb = _conv_bwd(proj, 4 * RW, 5 * RW, CW, cw_p, d_cpre, ktaps, T, tm)

    d_y, dr_x, dk_x, dv_x, d_grwkv, g_lnxg, g_lnxb, g_rk = _row_bwd(
        "rwkv_post_bwd", fn_post, post_params, post_rows, [(d_mix, 0, RW, True)], [f32, f32, f32, f32, bf16], T, tm, ncol)
    d_cw, d_cu, d_crt, d_cpb, d_cbend, d_lw_dec, d_cz = _wkv_state_bwd(*wkv_loc, states, d_y, head, T, RW)
    d_xr, d_lw, d_kh, d_xv, d_a, d_b = _wkv_local_bwd(
        *wkv_in, (d_cw, d_cu, d_crt, d_cpb, d_y, d_cbend, d_cz), d_lw_dec, dr_x, dk_x, dv_x, head, T, RW)
    pre_cots = [(d_lw, 0, RW, True), (d_kh, 0, RW, True), (d_a, 0, RW, True), (d_b, 0, RW, True)]
    d_xk, d_qw, d_qa, g_kk, g_ka = _row_bwd("rwkv_pre_bwd", fn_pre, pre_params, pre_rows, pre_cots, [f32, f32, f32],
                                            T, tm, ncol)
    d_xl, g_w0, g_wup_p, g_a0, g_aup_p = _row_bwd("lora_up_bwd", _fn_lora, lora_params, [(xs_l, 0, LORA_PAD, False)],
                                                  [(d_qw, 0, RW, False), (d_qa, 0, RW, False)], [f32], T, tm)
    dp_r, g_mur = _shift_bwd("shift_r_bwd", proj, 0, RW, mu_r, d_xr, T, tm)
    dp_k, g_muk = _shift_bwd("shift_k_bwd", proj, RW, RW, mu_k, d_xk, T, tm)
    dp_v, g_muv = _shift_bwd("shift_v_bwd", proj, 2 * RW, RW, mu_v, d_xv, T, tm)
    dp_l, g_mul = _shift_bwd("shift_l_bwd", proj, off_l, LORA_PAD, mu_l, d_xl, T, tm)
    d_proj = jnp.concatenate([dp_r, dp_k, dp_v, d_grwkv, d_gluv, d_glug, d_gconv, dp_l], axis=1)

    g_wp = _matmul("d_w_in", h, d_proj, "tn", f32)
    d_h = _matmul("d_h", d_proj, wp, "nt", bf16, tk_t=2560)
    grad_x2, g_npre = _rms_pre_bwd(x2, npg, d_h, gx_res, T, D, tm_wide)

    g_win_full = jnp.concatenate([g_wp[:, :lo], g_wp[:, off_l:off_l + 2 * lora], g_wp[:, lo:off_l]], axis=1)
    win_pieces = jnp.stack([g_win_full[:, s * shard:(s + 1) * shard].astype(bf16) for s in range(4)])
    win_pieces = win_pieces.reshape(8, D // 2, shard)
    wout_pieces = g_wout_full.astype(bf16).reshape(8, (RW + CW) // 8, D)
    pw2_pieces = g_pw2_full.astype(bf16).reshape(8, CW // 8, CW)
    g_mu = jnp.concatenate([g_mur[0], g_muk[0], g_muv[0], g_mul[0, :2 * lora]])
    pad_rows = lambda a, n: jnp.concatenate([a, jnp.zeros((n - a.shape[0], a.shape[1]), f32)], axis=0)
    n_mu = -(-mu_shift.shape[0] // RW)
    small_vecs = [g_npre.reshape(D // RW, RW), pad_rows(jnp.pad(g_mu, (0, n_mu * RW - g_mu.shape[0])).reshape(n_mu, RW), n_mu),
                  g_w0, g_a0, g_kk, g_ka, g_rk, g_lnxg, g_lnxb, g_cb, g_clng, g_clnb, g_bpw2,
                  g_npost.reshape(D // RW, RW), g_wup_p[:lora], g_aup_p[lora:2 * lora], g_cw_p[:ktaps]]
    n_small = sum(a.shape[0] for a in small_vecs)
    n_small_pad = -(-n_small // (2 * SUBLANES)) * (2 * SUBLANES)
    small = pad_rows(jnp.concatenate(small_vecs, axis=0), n_small_pad)

    r_win, r_wout, r_pw2, r_small = _exchange_pieces([win_pieces, wout_pieces, pw2_pieces], [small])
    s_win = _sum8("sum_w_in", r_win)
    s_wout = _sum8("sum_w_out", r_wout)
    s_pw2 = _sum8("sum_w_pw2", r_pw2)
    s_small = _sum8("sum_small", r_small)
    sh_win, sh_wout, sh_pw2 = _share_sibling([s_win, s_wout, s_pw2])
    grad_w_in = sh_win.reshape(D, shard)
    grad_w_out = sh_wout.reshape((RW + CW) // 4, D)
    grad_w_pw2 = sh_pw2.reshape(CW // 4, CW)

    chip = 2 * lax.axis_index("x") + lax.axis_index("y")
    pos = [0]

    def take(nrows):
        a = s_small[pos[0]:pos[0] + nrows]
        pos[0] += nrows
        return a

    my_cols = lambda a, w: lax.dynamic_slice_in_dim(a, chip * w, w, axis=1)
    grads = {}
    grads["norm_pre_g"] = take(D // RW).reshape(D)
    grads["mu_shift"] = take(n_mu).reshape(-1)[:mu_shift.shape[0]]
    for nm in ["w0", "a0", "k_k", "k_a"]:
        grads[nm] = take(1).reshape(RW)
    grads["r_k"] = take(1).reshape(r_k.shape)
    for nm in ["lnx_g", "lnx_b", "conv_b", "cln_g", "cln_b", "b_pw2"]:
        grads[nm] = take(1).reshape(RW)
    grads["norm_post_g"] = take(D // RW).reshape(D)
    grads["w_lora_up"] = my_cols(take(lora), RW // 4)
    grads["a_lora_up"] = my_cols(take(lora), RW // 4)
    grads["conv_w"] = my_cols(take(ktaps), CW // 4)
    grads["w_in"], grads["w_out"], grads["w_pw2"] = grad_w_in, grad_w_out, grad_w_pw2

    weights = dict(norm_pre_g=norm_pre_g, w_in=w_in, mu_shift=mu_shift, w0=w0, w_lora_up=w_lora_up, a0=a0,
                   a_lora_up=a_lora_up, k_k=k_k, k_a=k_a, r_k=r_k, lnx_g=lnx_g, lnx_b=lnx_b, conv_w=conv_w,
                   conv_b=conv_b, cln_g=cln_g, cln_b=cln_b, w_pw2=w_pw2, b_pw2=b_pw2, w_out=w_out,
                   norm_post_g=norm_post_g)
    ms = dict(norm_pre_g=m_norm_pre_g, w_in=m_w_in, mu_shift=m_mu_shift, w0=m_w0, w_lora_up=m_w_lora_up, a0=m_a0,
              a_lora_up=m_a_lora_up, k_k=m_k_k, k_a=m_k_a, r_k=m_r_k, lnx_g=m_lnx_g, lnx_b=m_lnx_b, conv_w=m_conv_w,
              conv_b=m_conv_b, cln_g=m_cln_g, cln_b=m_cln_b, w_pw2=m_w_pw2, b_pw2=m_b_pw2, w_out=m_w_out,
              norm_post_g=m_norm_post_g)
    vs = dict(norm_pre_g=v_norm_pre_g, w_in=v_w_in, mu_shift=v_mu_shift, w0=v_w0, w_lora_up=v_w_lora_up, a0=v_a0,
              a_lora_up=v_a_lora_up, k_k=v_k_k, k_a=v_k_a, r_k=v_r_k, lnx_g=v_lnx_g, lnx_b=v_lnx_b, conv_w=v_conv_w,
              conv_b=v_conv_b, cln_g=v_cln_g, cln_b=v_cln_b, w_pw2=v_w_pw2, b_pw2=v_b_pw2, w_out=v_w_out,
              norm_post_g=v_norm_post_g)
    names = list(weights)
    big = ["w_in", "w_out", "w_pw2"]
    deltas, new_m, new_v = {}, {}, {}
    for nm in big:
        deltas[nm], new_m[nm], new_v[nm] = _adamw("adamw_" + nm, weights[nm], grads[nm], ms[nm], vs[nm])
    rest = [nm for nm in names if nm not in big]
    sizes = [weights[nm].size for nm in rest]
    total = sum(sizes)
    width = 4 * LANES
    rows_p = -(-total // (width * SUBLANES)) * SUBLANES

    def pack(d):
        flat = jnp.concatenate([d[nm].reshape(-1) for nm in rest])
        return jnp.pad(flat, (0, rows_p * width - total)).reshape(rows_p, width)

    p_d, p_m, p_v = _adamw("adamw_small", pack(weights), pack(grads), pack(ms), pack(vs))
    o = 0
    for nm, sz in zip(rest, sizes):
        shp = weights[nm].shape
        deltas[nm] = p_d.reshape(-1)[o:o + sz].reshape(shp)
        new_m[nm] = p_m.reshape(-1)[o:o + sz].reshape(shp)
        new_v[nm] = p_v.reshape(-1)[o:o + sz].reshape(shp)
        o += sz

    loss = lax.psum(loss_part[0, 0], ("x", "y", "c"))
    grad_x = grad_x2[None]
    return (loss, grad_x, *[grads[nm] for nm in names], *[deltas[nm] for nm in names],
            *[new_m[nm] for nm in names], *[new_v[nm] for nm in names])
```

```python
import functools

import jax
import jax.numpy as jnp
from jax import lax
from jax.experimental import pallas as pl
from jax.experimental.pallas import tpu as pltpu

f32 = jnp.float32
bf16 = jnp.bfloat16
MESH = pl.DeviceIdType.MESH
HI = lax.Precision.HIGHEST

NORM_EPS = 1e-6
LN_EPS = 1e-5
ADAM_LR, ADAM_B1, ADAM_B2, ADAM_EPS, ADAM_WD, ADAM_STEP = 0.001, 0.9, 0.999, 1e-08, 0.01, 10

LANES = 128
SUBLANES = 8
LORA_PAD = 256
CONV_HALO = 32
WKV_CHUNK = 64
WKV_HEADS = 4
WKV_STATE_HEADS = 8
WKV_PREC = lax.Precision.HIGH
SHARE_CHUNKS = 8
VMEM_LIMIT = 56 * 1024 * 1024


def _cparams(n_axes):
    return pltpu.CompilerParams(dimension_semantics=("arbitrary",) * n_axes, vmem_limit_bytes=VMEM_LIMIT)


def _tile(dim, target):
    best = None
    t = LANES
    while t <= min(dim, target):
        if dim % t == 0:
            best = t
        t += LANES
    return best if best is not None else dim


def _mm(a, b, prec=None):
    return lax.dot_general(a, b, (((1,), (0,)), ((), ())), precision=prec, preferred_element_type=f32)


def _mm_nt(a, b, prec=None):
    return lax.dot_general(a, b, (((1,), (1,)), ((), ())), precision=prec, preferred_element_type=f32)


def _mm_tn(a, b, prec=None):
    return lax.dot_general(a, b, (((0,), (0,)), ((), ())), precision=prec, preferred_element_type=f32)


@jax.custom_vjp
def _bmm(a, b):
    return _mm(a.astype(bf16), b.astype(bf16))


def _bmm_fwd(a, b):
    return _bmm(a, b), (a, b)


def _bmm_bwd(res, dc):
    a, b = res
    dcb = dc.astype(bf16)
    return _mm_nt(dcb, b.astype(bf16)), _mm_tn(a.astype(bf16), dcb)


_bmm.defvjp(_bmm_fwd, _bmm_bwd)


def _matmul(name, a, b, mode, out_dtype, tm_t=1024, tn_t=768, tk_t=1024):
    if mode == "nn":
        (M, K), (_, N) = a.shape, b.shape
    elif mode == "nt":
        (M, K), (N, _) = a.shape, b.shape
    else:
        (K, M), (_, N) = a.shape, b.shape
    tm, tn, tk = _tile(M, tm_t), _tile(N, tn_t), _tile(K, tk_t)
    nk = K // tk
    dot = {"nn": _mm, "nt": _mm_nt, "tn": _mm_tn}[mode]

    def body(a_ref, b_ref, o_ref, acc_ref):
        k = pl.program_id(2)

        @pl.when(k == 0)
        def _():
            acc_ref[...] = jnp.zeros_like(acc_ref)

        acc_ref[...] += dot(a_ref[...], b_ref[...])

        @pl.when(k == nk - 1)
        def _():
            o_ref[...] = acc_ref[...].astype(o_ref.dtype)

    a_spec = {"nn": pl.BlockSpec((tm, tk), lambda i, j, k: (i, k)),
              "nt": pl.BlockSpec((tm, tk), lambda i, j, k: (i, k)),
              "tn": pl.BlockSpec((tk, tm), lambda i, j, k: (k, i))}[mode]
    b_spec = {"nn": pl.BlockSpec((tk, tn), lambda i, j, k: (k, j)),
              "nt": pl.BlockSpec((tn, tk), lambda i, j, k: (j, k)),
              "tn": pl.BlockSpec((tk, tn), lambda i, j, k: (k, j))}[mode]
    return pl.pallas_call(
        body, name=name, grid=(M // tm, N // tn, nk),
        in_specs=[a_spec, b_spec],
        out_specs=pl.BlockSpec((tm, tn), lambda i, j, k: (i, j)),
        out_shape=jax.ShapeDtypeStruct((M, N), out_dtype),
        scratch_shapes=[pltpu.VMEM((tm, tn), f32)],
        compiler_params=_cparams(3),
    )(a, b)


def _row_spec(op, tm, ncol):
    arr, off, width, tiled = op
    if tiled:
        cw = width // ncol
        return pl.BlockSpec((tm, cw), lambda j, i: (i, off // cw + j))
    return pl.BlockSpec((tm, width), lambda j, i: (i, off // width))


def _param_spec(p, ncol):
    arr, tiled = p
    rows, width = arr.shape
    if tiled:
        return pl.BlockSpec((rows, width // ncol), lambda j, i: (0, j))
    return pl.BlockSpec((rows, width), lambda j, i: (0, 0))


def _row_fwd(name, fn, params, rows, outs, T, tm, ncol=1):
    npar, nrow = len(params), len(rows)

    def body(*refs):
        pv = [r[...] for r in refs[:npar]]
        rv = [r[...].astype(f32) for r in refs[npar:npar + nrow]]
        res = fn(*pv, *rv)
        for o_ref, val in zip(refs[npar + nrow:], res):
            o_ref[...] = val.astype(o_ref.dtype)

    return pl.pallas_call(
        body, name=name, grid=(ncol, T // tm),
        in_specs=[_param_spec(p, ncol) for p in params] + [_row_spec(r, tm, ncol) for r in rows],
        out_specs=[pl.BlockSpec((tm, w // ncol), lambda j, i: (i, j)) for w, _ in outs],
        out_shape=[jax.ShapeDtypeStruct((T, w), dt) for w, dt in outs],
        compiler_params=_cparams(2),
    )(*[p[0] for p in params], *[r[0] for r in rows])


def _row_bwd(name, fn, params, rows, cots, row_grads, T, tm, ncol=1):
    npar, nrow, ncot = len(params), len(rows), len(cots)
    want = [k for k, dt in enumerate(row_grads) if dt is not None]

    def body(*refs):
        pv = [r[...] for r in refs[:npar]]
        rv = [r[...].astype(f32) for r in refs[npar:npar + nrow]]
        cv = tuple(r[...].astype(f32) for r in refs[npar + nrow:npar + nrow + ncot])
        out_refs = refs[npar + nrow + ncot:]
        _, vjp = jax.vjp(fn, *pv, *rv)
        grads = vjp(cv)
        for o_ref, k in zip(out_refs[:len(want)], want):
            o_ref[...] = grads[npar + k].astype(o_ref.dtype)
        j, i = pl.program_id(0), pl.program_id(1)
        for o_ref, p, g in zip(out_refs[len(want):], params, grads[:npar]):
            first = (i == 0) if p[1] else jnp.logical_and(i == 0, j == 0)

            @pl.when(first)
            def _():
                o_ref[...] = jnp.zeros_like(o_ref)

            o_ref[...] += g

    def grad_spec(op):
        arr, off, width, tiled = op
        if tiled:
            return pl.BlockSpec((tm, width // ncol), lambda j, i: (i, j)), (T, width)
        return pl.BlockSpec((tm, width), lambda j, i: (i, j)), (T, width * ncol)

    gspecs = [grad_spec(rows[k]) for k in want]
    return pl.pallas_call(
        body, name=name, grid=(ncol, T // tm),
        in_specs=[_param_spec(p, ncol) for p in params] + [_row_spec(r, tm, ncol) for r in rows]
        + [_row_spec(c, tm, ncol) for c in cots],
        out_specs=[s for s, _ in gspecs] + [_param_spec(p, ncol) for p in params],
        out_shape=[jax.ShapeDtypeStruct(shp, row_grads[k]) for (_, shp), k in zip(gspecs, want)]
        + [jax.ShapeDtypeStruct(p[0].shape, f32) for p in params],
        compiler_params=_cparams(2),
    )(*[p[0] for p in params], *[r[0] for r in rows], *[c[0] for c in cots])


def _seg_sum(x, head):
    li = lax.broadcasted_iota(jnp.int32, (LANES, LANES), 0) // head
    lj = lax.broadcasted_iota(jnp.int32, (LANES, LANES), 1) // head
    q = (li == lj).astype(f32)
    parts = [_mm(x[:, s:s + LANES], q, HI) for s in range(0, x.shape[1], LANES)]
    return parts[0] if len(parts) == 1 else jnp.concatenate(parts, axis=1)


def _sigmoid(z):
    return 1.0 / (1.0 + jnp.exp(-z))


def _silu(z):
    return z * _sigmoid(z)


def _rms(g, x):
    return x * lax.rsqrt(jnp.mean(x * x, axis=-1, keepdims=True) + NORM_EPS) * g


def _fn_rms_pre(g, x):
    return (_rms(g, x),)


def _fn_lora(w0, wup, a0, aup, xl):
    qw = w0 + _bmm(jnp.tanh(xl), wup)
    qa = a0 + _bmm(xl, aup)
    return qw, qa


def _fn_rwkv_pre(head, k_k, k_a, xk, qw, qa):
    w_log = -(jnp.maximum(-qw, 0.0) + jnp.log(1.0 + jnp.exp(-jnp.abs(qw)))) - 0.5
    lw = -jnp.exp(w_log)
    a_sig = _sigmoid(qa)
    kk = xk * k_k
    kk = kk / jnp.maximum(jnp.sqrt(_seg_sum(kk * kk, head)), 1e-12)
    k_h = xk * (1.0 + (a_sig - 1.0) * k_a)
    return lw, k_h, -kk, kk * a_sig


def _fn_rwkv_post(head, lnx_g, lnx_b, r_k, y, r, k_h, v, g):
    inv = 1.0 / head
    mu = _seg_sum(y, head) * inv
    d = y - mu
    var = _seg_sum(d * d, head) * inv
    yn = d * lax.rsqrt(var + 1e-5 * head) * lnx_g + lnx_b
    bonus = _seg_sum(r * k_h * r_k, head) * v
    return ((yn + bonus) * _silu(g),)


def _fn_conv_ln(cln_g, cln_b, c):
    mu = jnp.mean(c, axis=-1, keepdims=True)
    d = c - mu
    var = jnp.mean(d * d, axis=-1, keepdims=True)
    return (_silu(d * lax.rsqrt(var + LN_EPS) * cln_g + cln_b),)


def _fn_conv_post(b_pw2, c2, g):
    return ((c2 + b_pw2) * _silu(g),)


def _post(out, x, tgt, g, T, D, tm):
    def body(g_ref, o_ref, x_ref, t_ref, dout_ref, gx_ref, loss_ref, dg_ref):
        i = pl.program_id(0)
        o, vjp = jax.vjp(_rms, g_ref[...], o_ref[...])
        err = x_ref[...] + o - t_ref[...]
        d_y = err * (1.0 / D)
        dg, d_out = vjp(d_y)
        dout_ref[...] = d_out.astype(dout_ref.dtype)
        gx_ref[...] = d_y

        @pl.when(i == 0)
        def _():
            loss_ref[...] = jnp.zeros_like(loss_ref)
            dg_ref[...] = jnp.zeros_like(dg_ref)

        loss_ref[...] += jnp.sum(err * err, keepdims=True) * (0.5 / D)
        dg_ref[...] += dg

    row = pl.BlockSpec((tm, D), lambda i: (i, 0))
    vec = pl.BlockSpec((1, D), lambda i: (0, 0))
    return pl.pallas_call(
        body, name="post_loss", grid=(T // tm,),
        in_specs=[vec, row, row, row],
        out_specs=[row, row, pl.BlockSpec((1, 1), lambda i: (0, 0)), vec],
        out_shape=[jax.ShapeDtypeStruct((T, D), bf16), jax.ShapeDtypeStruct((T, D), f32),
                   jax.ShapeDtypeStruct((1, 1), f32), jax.ShapeDtypeStruct((1, D), f32)],
        compiler_params=_cparams(1),
    )(g, out, x, tgt)


def _rms_pre_bwd(x, g, dh, gx_res, T, D, tm):
    def body(g_ref, x_ref, dh_ref, res_ref, dx_ref, dg_ref):
        i = pl.program_id(0)
        _, vjp = jax.vjp(_rms, g_ref[...], x_ref[...])
        dg, dx = vjp(dh_ref[...].astype(f32))
        dx_ref[...] = dx + res_ref[...]

        @pl.when(i == 0)
        def _():
            dg_ref[...] = jnp.zeros_like(dg_ref)

        dg_ref[...] += dg

    row = pl.BlockSpec((tm, D), lambda i: (i, 0))
    vec = pl.BlockSpec((1, D), lambda i: (0, 0))
    return pl.pallas_call(
        body, name="rms_pre_bwd", grid=(T // tm,),
        in_specs=[vec, row, row, row], out_specs=[row, vec],
        out_shape=[jax.ShapeDtypeStruct((T, D), f32), jax.ShapeDtypeStruct((1, D), f32)],
        compiler_params=_cparams(1),
    )(g, x, dh, gx_res)


def _prev_rows(cur, halo_ref, first):
    top = jnp.where(first, 0.0, halo_ref[SUBLANES - 1:SUBLANES, :])
    rolled = pltpu.roll(cur, 1, 0)
    rid = lax.broadcasted_iota(jnp.int32, cur.shape, 0)
    return jnp.where(rid == 0, top, rolled)


def _shift_fwd(name, proj, off, width, mu, T, tm):
    cw = _tile(width, 512)
    ncol, cb = width // cw, off // cw
    hb = tm // SUBLANES

    def body(mu_ref, cur_ref, halo_ref, o_ref):
        i = pl.program_id(1)
        cur = cur_ref[...]
        prev = _prev_rows(cur, halo_ref, i == 0)
        o_ref[...] = cur + (prev - cur) * mu_ref[...]

    return pl.pallas_call(
        body, name=name, grid=(ncol, T // tm),
        in_specs=[pl.BlockSpec((1, cw), lambda j, i: (0, j)),
                  pl.BlockSpec((tm, cw), lambda j, i: (i, cb + j)),
                  pl.BlockSpec((SUBLANES, cw), lambda j, i: (jnp.maximum(i * hb - 1, 0), cb + j))],
        out_specs=pl.BlockSpec((tm, cw), lambda j, i: (i, j)),
        out_shape=jax.ShapeDtypeStruct((T, width), f32),
        compiler_params=_cparams(2),
    )(mu, proj, proj)


def _shift_bwd(name, proj, off, width, mu, dxs, T, tm):
    cw = _tile(width, 512)
    ncol, cb = width // cw, off // cw
    hb, nt = tm // SUBLANES, T // tm

    def body(mu_ref, cur_ref, halo_ref, d_ref, dnext_ref, o_ref, dmu_ref):
        i = pl.program_id(1)
        cur = cur_ref[...]
        prev = _prev_rows(cur, halo_ref, i == 0)
        d = d_ref[...]
        bottom = jnp.where(i == nt - 1, 0.0, dnext_ref[0:1, :])
        rid = lax.broadcasted_iota(jnp.int32, d.shape, 0)
        d_next = jnp.where(rid == tm - 1, bottom, pltpu.roll(d, tm - 1, 0))
        mu_v = mu_ref[...]
        o_ref[...] = (d * (1.0 - mu_v) + d_next * mu_v).astype(o_ref.dtype)

        @pl.when(i == 0)
        def _():
            dmu_ref[...] = jnp.zeros_like(dmu_ref)

        dmu_ref[...] += jnp.sum(d * (prev - cur), axis=0, keepdims=True)

    return pl.pallas_call(
        body, name=name, grid=(ncol, nt),
        in_specs=[pl.BlockSpec((1, cw), lambda j, i: (0, j)),
                  pl.BlockSpec((tm, cw), lambda j, i: (i, cb + j)),
                  pl.BlockSpec((SUBLANES, cw), lambda j, i: (jnp.maximum(i * hb - 1, 0), cb + j)),
                  pl.BlockSpec((tm, cw), lambda j, i: (i, j)),
                  pl.BlockSpec((SUBLANES, cw), lambda j, i: (jnp.minimum((i + 1) * hb, nt * hb - 1), j))],
        out_specs=[pl.BlockSpec((tm, cw), lambda j, i: (i, j)), pl.BlockSpec((1, cw), lambda j, i: (0, j))],
        out_shape=[jax.ShapeDtypeStruct((T, width), bf16), jax.ShapeDtypeStruct((1, width), f32)],
        compiler_params=_cparams(2),
    )(mu, proj, proj, dxs, dxs)


def _conv_fwd(proj, off_v, off_g, width, conv_w, conv_b, ktaps, T, tm):
    cw = _tile(width, 512)
    ncol = width // cw
    hb = tm // CONV_HALO
    lead = CONV_HALO - (ktaps - 1)

    def body(w_ref, b_ref, v_ref, g_ref, hv_ref, hg_ref, o_ref, u_ref):
        i = pl.program_id(1)
        halo = hv_ref[...] * _sigmoid(hg_ref[...])
        u_ref[0:CONV_HALO, :] = jnp.where(i == 0, 0.0, halo)
        u_ref[CONV_HALO:, :] = v_ref[...] * _sigmoid(g_ref[...])
        acc = jnp.zeros((tm, cw), f32) + b_ref[...]
        for j in range(ktaps):
            acc = acc + u_ref[pl.ds(lead + j, tm), :] * w_ref[j:j + 1, :]
        o_ref[...] = acc

    def tile(off):
        return pl.BlockSpec((tm, cw), lambda j, i: (i, off // cw + j))

    def halo(off):
        return pl.BlockSpec((CONV_HALO, cw), lambda j, i: (jnp.maximum(i * hb - 1, 0), off // cw + j))

    return pl.pallas_call(
        body, name="conv_fwd", grid=(ncol, T // tm),
        in_specs=[pl.BlockSpec((CONV_HALO, cw), lambda j, i: (0, j)), pl.BlockSpec((1, cw), lambda j, i: (0, j)),
                  tile(off_v), tile(off_g), halo(off_v), halo(off_g)],
        out_specs=pl.BlockSpec((tm, cw), lambda j, i: (i, j)),
        out_shape=jax.ShapeDtypeStruct((T, width), f32),
        scratch_shapes=[pltpu.VMEM((CONV_HALO + tm, cw), f32)],
        compiler_params=_cparams(2),
    )(conv_w, conv_b, proj, proj, proj, proj)


def _conv_bwd(proj, off_v, off_g, width, conv_w, dc, ktaps, T, tm):
    cw = _tile(width, 512)
    ncol = width // cw
    hb, nt = tm // CONV_HALO, T // tm
    lead = CONV_HALO - (ktaps - 1)

    def body(w_ref, v_ref, g_ref, hv_ref, hg_ref, dc_ref, dcn_ref, dv_ref, dg_ref, dw_ref, db_ref, u_ref, d_ref):
        i = pl.program_id(1)
        halo = hv_ref[...] * _sigmoid(hg_ref[...])
        u_ref[0:CONV_HALO, :] = jnp.where(i == 0, 0.0, halo)
        sig = _sigmoid(g_ref[...])
        gv = v_ref[...]
        u_ref[CONV_HALO:, :] = gv * sig
        dcur = dc_ref[...]
        d_ref[0:tm, :] = dcur
        d_ref[tm:, :] = jnp.where(i == nt - 1, 0.0, dcn_ref[...])

        @pl.when(i == 0)
        def _():
            dw_ref[...] = jnp.zeros_like(dw_ref)
            db_ref[...] = jnp.zeros_like(db_ref)

        du = jnp.zeros((tm, cw), f32)
        for j in range(ktaps):
            du = du + d_ref[pl.ds(ktaps - 1 - j, tm), :] * w_ref[j:j + 1, :]
            dw_ref[j:j + 1, :] += jnp.sum(u_ref[pl.ds(lead + j, tm), :] * dcur, axis=0, keepdims=True)
        db_ref[...] += jnp.sum(dcur, axis=0, keepdims=True)
        dv_ref[...] = (du * sig).astype(dv_ref.dtype)
        dg_ref[...] = (du * gv * sig * (1.0 - sig)).astype(dg_ref.dtype)

    def tile(off):
        return pl.BlockSpec((tm, cw), lambda j, i: (i, off // cw + j))

    def halo(off):
        return pl.BlockSpec((CONV_HALO, cw), lambda j, i: (jnp.maximum(i * hb - 1, 0), off // cw + j))

    return pl.pallas_call(
        body, name="conv_bwd", grid=(ncol, nt),
        in_specs=[pl.BlockSpec((CONV_HALO, cw), lambda j, i: (0, j)),
                  tile(off_v), tile(off_g), halo(off_v), halo(off_g),
                  pl.BlockSpec((tm, cw), lambda j, i: (i, j)),
                  pl.BlockSpec((CONV_HALO, cw), lambda j, i: (jnp.minimum((i + 1) * hb, nt * hb - 1), j))],
        out_specs=[pl.BlockSpec((tm, cw), lambda j, i: (i, j)), pl.BlockSpec((tm, cw), lambda j, i: (i, j)),
                   pl.BlockSpec((CONV_HALO, cw), lambda j, i: (0, j)), pl.BlockSpec((1, cw), lambda j, i: (0, j))],
        out_shape=[jax.ShapeDtypeStruct((T, width), bf16), jax.ShapeDtypeStruct((T, width), bf16),
                   jax.ShapeDtypeStruct((CONV_HALO, width), f32), jax.ShapeDtypeStruct((1, width), f32)],
        scratch_shapes=[pltpu.VMEM((CONV_HALO + tm, cw), f32), pltpu.VMEM((tm + CONV_HALO, cw), f32)],
        compiler_params=_cparams(2),
    )(conv_w, proj, proj, proj, proj, dc, dc)


def _each(f, *lists):
    return [f(*xs) for xs in zip(*lists)]


def _wkv_local(r, lw, k, v, a, b):
    C = r[0].shape[0]
    P = WKV_PREC
    row = lax.broadcasted_iota(jnp.int32, (C, C), 0)
    col = lax.broadcasted_iota(jnp.int32, (C, C), 1)
    incl, strict = row >= col, row > col
    tri = incl.astype(f32)
    zero = jnp.zeros((C, C), f32)
    G = _each(lambda x: _mm(tri, x, HI), lw)
    to_end = _each(lambda x, g: jnp.exp(jnp.sum(x, axis=0, keepdims=True) - g), lw, G)
    e_g = _each(jnp.exp, G)
    e_ng = _each(lambda g: jnp.exp(-g), G)
    At = _each(lambda x, g, w: x * jnp.exp(g - w), a, G, lw)
    Rt = _each(jnp.multiply, r, e_g)
    Kt = _each(jnp.multiply, k, e_ng)
    Bt = _each(jnp.multiply, b, e_ng)
    L = _each(lambda x, y: jnp.where(strict, _mm_nt(x, y, P), zero), At, Bt)
    M = _each(lambda x, y: jnp.where(strict, _mm_nt(x, y, P), zero), At, Kt)
    Pb = _each(lambda x, y: jnp.where(incl, _mm_nt(x, y, P), zero), Rt, Bt)
    Pk = _each(lambda x, y: jnp.where(incl, _mm_nt(x, y, P), zero), Rt, Kt)
    W = At
    U = _each(_bmm, M, v)
    Lp = L
    n = 1
    while n < C:
        W = _each(lambda x, l: x + _bmm(l, x), W, Lp)
        U = _each(lambda x, l: x + _bmm(l, x), U, Lp)
        n *= 2
        if n < C:
            Lp = _each(lambda l: _bmm(l, l), Lp)
    Y0 = _each(_bmm, Pk, v)
    Bend = _each(jnp.multiply, b, to_end)
    Z = _each(lambda x, y, e: _mm_tn(x, y * e, P), v, k, to_end)
    return W, U, Rt, Pb, Y0, Bend, Z


def _wkv_state(S0, W, U, Rt, Pb, Bend, lw, Y0, Z):
    P = WKV_PREC
    X = _each(lambda w, s, u: _mm_nt(w, s, P) + u, W, S0, U)
    yS = _each(lambda x, s: _mm_nt(x, s, P), Rt, S0)
    y = _each(lambda p, x, a, c: _mm(p, x, P) + a + c, Pb, X, yS, Y0)
    S1 = _each(lambda s, w, x, e, z: s * jnp.exp(jnp.sum(w, axis=0, keepdims=True)) + _mm_tn(x, e, P) + z,
               S0, lw, X, Bend, Z)
    return y, S1


def _wkv_dims(head, T, RW, heads_per_step):
    C = min(WKV_CHUNK, T)
    nh = RW // head
    hb = min(heads_per_step, nh)
    return C, nh, hb, hb * head, T // C


def _heads(ref, hb, head):
    return [ref[:, h * head:(h + 1) * head] for h in range(hb)]


def _put_heads(ref, vals, head):
    for h, val in enumerate(vals):
        ref[:, h * head:(h + 1) * head] = val


def _wkv_local_fwd(r, lw, k, v, a, b, head, T, RW):
    C, nh, hb, bw, nc = _wkv_dims(head, T, RW, WKV_HEADS)

    def body(*refs):
        ins, outs = refs[:6], refs[6:]
        res = _wkv_local(*[_heads(x, hb, head) for x in ins])
        for o_ref, vals in zip(outs[:6], res[:6]):
            _put_heads(o_ref, vals, head)
        for h in range(hb):
            outs[6][0, h] = res[6][h]

    blk = pl.BlockSpec((C, bw), lambda g, c: (c, g))
    sq = pl.BlockSpec((1, hb, head, head), lambda g, c: (c, g, 0, 0))
    return pl.pallas_call(
        body, name="wkv_local", grid=(nh // hb, nc),
        in_specs=[blk] * 6, out_specs=[blk] * 6 + [sq],
        out_shape=[jax.ShapeDtypeStruct((T, RW), f32)] * 6 + [jax.ShapeDtypeStruct((nc, nh, head, head), f32)],
        compiler_params=_cparams(2),
    )(r, lw, k, v, a, b)


def _wkv_state_fwd(W, U, Rt, Pb, Bend, lw, Y0, Z, head, T, RW):
    C, nh, hb, bw, nc = _wkv_dims(head, T, RW, WKV_STATE_HEADS)

    def body(w_ref, u_ref, rt_ref, pb_ref, be_ref, lw_ref, y0_ref, z_ref, y_ref, st_ref, s_ref):
        @pl.when(pl.program_id(1) == 0)
        def _():
            s_ref[...] = jnp.zeros_like(s_ref)

        S0 = [s_ref[h] for h in range(hb)]
        for h in range(hb):
            st_ref[0, h] = S0[h]
        rows = [_heads(x, hb, head) for x in (w_ref, u_ref, rt_ref, pb_ref, be_ref, lw_ref, y0_ref)]
        y, S1 = _wkv_state(S0, *rows, [z_ref[0, h] for h in range(hb)])
        _put_heads(y_ref, y, head)
        for h in range(hb):
            s_ref[h] = S1[h]

    blk = pl.BlockSpec((C, bw), lambda g, c: (c, g))
    sq = pl.BlockSpec((1, hb, head, head), lambda g, c: (c, g, 0, 0))
    return pl.pallas_call(
        body, name="wkv_state", grid=(nh // hb, nc),
        in_specs=[blk] * 7 + [sq], out_specs=[blk, sq],
        out_shape=[jax.ShapeDtypeStruct((T, RW), f32), jax.ShapeDtypeStruct((nc, nh, head, head), f32)],
        scratch_shapes=[pltpu.VMEM((hb, head, head), f32)],
        compiler_params=_cparams(2),
    )(W, U, Rt, Pb, Bend, lw, Y0, Z)


def _wkv_state_bwd(W, U, Rt, Pb, Bend, lw, Y0, Z, states, dy, head, T, RW):
    C, nh, hb, bw, nc = _wkv_dims(head, T, RW, WKV_STATE_HEADS)

    def body(w_ref, u_ref, rt_ref, pb_ref, be_ref, lw_ref, y0_ref, z_ref, st_ref, dy_ref,
             dw_ref, du_ref, drt_ref, dpb_ref, dbe_ref, dlw_ref, dz_ref, ds_ref):
        @pl.when(pl.program_id(1) == 0)
        def _():
            ds_ref[...] = jnp.zeros_like(ds_ref)

        dS1 = [ds_ref[h] for h in range(hb)]
        for h in range(hb):
            dz_ref[0, h] = dS1[h]
        rows = [_heads(x, hb, head) for x in (w_ref, u_ref, rt_ref, pb_ref, be_ref, lw_ref)]
        Y0 = _heads(y0_ref, hb, head)
        Zs = [z_ref[0, h] for h in range(hb)]
        _, vjp = jax.vjp(lambda s0, *rw: _wkv_state(s0, *rw, Y0, Zs), [st_ref[0, h] for h in range(hb)], *rows)
        grads = vjp((_heads(dy_ref, hb, head), dS1))
        for o_ref, vals in zip((dw_ref, du_ref, drt_ref, dpb_ref, dbe_ref, dlw_ref), grads[1:]):
            _put_heads(o_ref, vals, head)
        for h in range(hb):
            ds_ref[h] = grads[0][h]

    blk = pl.BlockSpec((C, bw), lambda g, c: (nc - 1 - c, g))
    sq = pl.BlockSpec((1, hb, head, head), lambda g, c: (nc - 1 - c, g, 0, 0))
    return pl.pallas_call(
        body, name="wkv_state_bwd", grid=(nh // hb, nc),
        in_specs=[blk] * 7 + [sq, sq, blk], out_specs=[blk] * 6 + [sq],
        out_shape=[jax.ShapeDtypeStruct((T, RW), f32)] * 6 + [jax.ShapeDtypeStruct((nc, nh, head, head), f32)],
        scratch_shapes=[pltpu.VMEM((hb, head, head), f32)],
        compiler_params=_cparams(2),
    )(W, U, Rt, Pb, Bend, lw, Y0, Z, states, dy)


def _wkv_local_bwd(r, lw, k, v, a, b, cots, d_lw_x, dr_x, dk_x, dv_x, head, T, RW):
    C, nh, hb, bw, nc = _wkv_dims(head, T, RW, WKV_HEADS)

    def body(*refs):
        ins, cot_refs, add_refs, outs = refs[:6], refs[6:13], refs[13:17], refs[17:]
        _, vjp = jax.vjp(_wkv_local, *[_heads(x, hb, head) for x in ins])
        cts = [_heads(x, hb, head) for x in cot_refs[:6]] + [[cot_refs[6][0, h] for h in range(hb)]]
        dr, dlw, dk, dv, da, db = vjp(tuple(cts))
        dlw_x, drx, dkx, dvx = [_heads(x, hb, head) for x in add_refs]
        _put_heads(outs[0], _each(jnp.add, dr, drx), head)
        _put_heads(outs[1], _each(jnp.add, dlw, dlw_x), head)
        _put_heads(outs[2], _each(jnp.add, dk, dkx), head)
        _put_heads(outs[3], _each(jnp.add, dv, dvx), head)
        _put_heads(outs[4], da, head)
        _put_heads(outs[5], db, head)

    blk = pl.BlockSpec((C, bw), lambda g, c: (c, g))
    sq = pl.BlockSpec((1, hb, head, head), lambda g, c: (c, g, 0, 0))
    return pl.pallas_call(
        body, name="wkv_local_bwd", grid=(nh // hb, nc),
        in_specs=[blk] * 12 + [sq] + [blk] * 4, out_specs=[blk] * 6,
        out_shape=[jax.ShapeDtypeStruct((T, RW), f32)] * 6,
        compiler_params=_cparams(2),
    )(r, lw, k, v, a, b, *cots, d_lw_x, dr_x, dk_x, dv_x)


def _rows_tile(R, row_bytes, budget, mult=SUBLANES):
    best = None
    t = mult
    while t <= R:
        if R % t == 0 and t * row_bytes <= budget:
            best = t
        t += mult
    return best if best is not None else R


def _sum8(name, parts):
    _, R, W = parts.shape
    tr = _rows_tile(R, 8 * W * 4, 4 << 20, 2 * SUBLANES)

    def body(p_ref, o_ref):
        acc = p_ref[0].astype(f32)
        for d in range(1, 8):
            acc = acc + p_ref[d].astype(f32)
        o_ref[...] = acc

    return pl.pallas_call(
        body, name=name, grid=(R // tr,),
        in_specs=[pl.BlockSpec((8, tr, W), lambda i: (0, i, 0))],
        out_specs=pl.BlockSpec((tr, W), lambda i: (i, 0)),
        out_shape=jax.ShapeDtypeStruct((R, W), f32),
        compiler_params=_cparams(1),
    )(parts)


def _adamw(name, w, g, m, v):
    R, W = w.shape
    tr = _rows_tile(R, W * 4, 1 << 20)

    def body(w_ref, g_ref, m_ref, v_ref, d_ref, nm_ref, nv_ref):
        g_v = g_ref[...]
        nm = ADAM_B1 * m_ref[...] + (1.0 - ADAM_B1) * g_v
        nv = ADAM_B2 * v_ref[...] + (1.0 - ADAM_B2) * (g_v * g_v)
        m_hat = nm / (1.0 - ADAM_B1 ** ADAM_STEP)
        v_hat = nv / (1.0 - ADAM_B2 ** ADAM_STEP)
        d_ref[...] = -ADAM_LR * (m_hat / (jnp.sqrt(v_hat) + ADAM_EPS) + ADAM_WD * w_ref[...])
        nm_ref[...] = nm
        nv_ref[...] = nv

    blk = pl.BlockSpec((tr, W), lambda i: (i, 0))
    return pl.pallas_call(
        body, name=name, grid=(R // tr,),
        in_specs=[blk] * 4, out_specs=[blk] * 3,
        out_shape=[jax.ShapeDtypeStruct((R, W), f32)] * 3,
        compiler_params=_cparams(1),
    )(w, g, m, v)


ANY = pl.BlockSpec(memory_space=pl.ANY)


def _place():
    return lax.axis_index("x"), lax.axis_index("y"), lax.axis_index("c")


def _gather_chips(arrays):
    n = len(arrays)
    parts = []
    for a, arr in enumerate(arrays):
        k = SHARE_CHUNKS // 2 if arr.shape[1] % (SHARE_CHUNKS * SUBLANES) == 0 else 1
        step = arr.shape[1] // k
        parts += [(a, h, q * step, step) for h in range(2) for q in range(k)]

    def body(*refs):
        ins, outs = refs[:n], refs[n:2 * n]
        send_sems, recv_sems, local_sems = refs[2 * n:]
        x, y, c = _place()
        mine = 2 * x + y
        sib = (x, y, 1 - c)
        chips = [(1 - x, y), (x, 1 - y), (1 - x, 1 - y)]
        local = [pltpu.make_async_copy(ins[a].at[h, pl.ds(r0, nr)], outs[a].at[mine, h, pl.ds(r0, nr)], local_sems.at[p])
                 for p, (a, h, r0, nr) in enumerate(parts)]
        for cp in local:
            cp.start()

        def over_ici(a, j, slot):
            px, py = chips[j]
            return pltpu.make_async_remote_copy(
                src_ref=ins[a].at[c], dst_ref=outs[a].at[slot, c], send_sem=send_sems.at[3 * a + j],
                recv_sem=recv_sems.at[3 * a + j], device_id=(px, py, c), device_id_type=MESH)

        def over_d2d(a, j, half):
            px, py = chips[j]
            slot = 2 * px + py
            return pltpu.make_async_remote_copy(
                src_ref=outs[a].at[slot, half], dst_ref=outs[a].at[slot, half], send_sem=send_sems.at[3 * (n + a) + j],
                recv_sem=recv_sems.at[3 * (n + a) + j], device_id=sib, device_id_type=MESH)

        sends = [over_ici(a, j, mine) for a in range(n) for j in range(3)]
        for cp in sends:
            cp.start()
        passed = []
        for a in range(n):
            for j, (px, py) in enumerate(chips):
                over_ici(a, j, 2 * px + py).wait_recv()
                cp = over_d2d(a, j, c)
                cp.start()
                passed.append(cp)
        for a in range(n):
            for j in range(3):
                over_d2d(a, j, 1 - c).wait_recv()
        for cp in sends + passed:
            cp.wait_send()
        for cp in local:
            cp.wait()

    return pl.pallas_call(
        body, name="gather_weights",
        in_specs=[ANY] * n, out_specs=[ANY] * n,
        out_shape=[jax.ShapeDtypeStruct((4,) + a.shape, a.dtype) for a in arrays],
        scratch_shapes=[pltpu.SemaphoreType.DMA((6 * n,)), pltpu.SemaphoreType.DMA((6 * n,)),
                        pltpu.SemaphoreType.DMA((len(parts),))],
    )(*arrays)


def _exchange_pieces(pieces, whole):
    n, m = len(pieces), len(whole)
    tot = n + m

    def body(*refs):
        ins, outs = refs[:tot], refs[tot:2 * tot]
        send_sems, recv_sems, local_sems = refs[2 * tot:]
        x, y, c = _place()
        mine = 4 * x + 2 * y + c
        peers = [(x ^ (k >> 2), y ^ ((k >> 1) & 1), c ^ (k & 1)) for k in range(1, 8)]

        def src(a, idx):
            return ins[a].at[idx] if a < n else ins[a]

        local = [pltpu.make_async_copy(src(a, mine), outs[a].at[mine], local_sems.at[a]) for a in range(tot)]
        for cp in local:
            cp.start()
        sends = []
        for a in range(tot):
            for j, (px, py, pc) in enumerate(peers):
                cp = pltpu.make_async_remote_copy(
                    src_ref=src(a, 4 * px + 2 * py + pc), dst_ref=outs[a].at[mine], send_sem=send_sems.at[7 * a + j],
                    recv_sem=recv_sems.at[7 * a + j], device_id=(px, py, pc), device_id_type=MESH)
                cp.start()
                sends.append(cp)
        for a in range(tot):
            for j, (px, py, pc) in enumerate(peers):
                pltpu.make_async_remote_copy(
                    src_ref=src(a, mine), dst_ref=outs[a].at[4 * px + 2 * py + pc], send_sem=send_sems.at[7 * a + j],
                    recv_sem=recv_sems.at[7 * a + j], device_id=(px, py, pc), device_id_type=MESH).wait_recv()
        for cp in sends:
            cp.wait_send()
        for cp in local:
            cp.wait()

    shapes = [jax.ShapeDtypeStruct(a.shape, a.dtype) for a in pieces]
    shapes += [jax.ShapeDtypeStruct((8,) + a.shape, a.dtype) for a in whole]
    return pl.pallas_call(
        body, name="exchange_grads",
        in_specs=[ANY] * tot, out_specs=[ANY] * tot, out_shape=shapes,
        scratch_shapes=[pltpu.SemaphoreType.DMA((7 * tot,)), pltpu.SemaphoreType.DMA((7 * tot,)),
                        pltpu.SemaphoreType.DMA((tot,))],
    )(*pieces, *whole)


def _share_sibling(arrays):
    n = len(arrays)
    parts = []
    for a, arr in enumerate(arrays):
        k = SHARE_CHUNKS if arr.shape[0] % (SHARE_CHUNKS * SUBLANES) == 0 else 1
        step = arr.shape[0] // k
        parts += [(a, q * step, step) for q in range(k)]
    npart = len(parts)

    def body(*refs):
        ins, outs = refs[:n], refs[n:2 * n]
        send_sems, recv_sems = refs[2 * n:]
        x, y, c = _place()

        def copy(p):
            a, r0, nr = parts[p]
            return pltpu.make_async_remote_copy(
                src_ref=ins[a].at[pl.ds(r0, nr)], dst_ref=outs[a].at[pl.ds(r0, nr)], send_sem=send_sems.at[p],
                recv_sem=recv_sems.at[p], device_id=(x, y, 1 - c), device_id_type=MESH)

        copies = [copy(p) for p in range(npart)]
        for cp in copies:
            cp.start()
        for cp in copies:
            cp.wait_recv()
        for cp in copies:
            cp.wait_send()

    return pl.pallas_call(
        body, name="share_sibling",
        in_specs=[ANY] * n, out_specs=[ANY] * n,
        out_shape=[jax.ShapeDtypeStruct(a.shape, a.dtype) for a in arrays],
        scratch_shapes=[pltpu.SemaphoreType.DMA((npart,)), pltpu.SemaphoreType.DMA((npart,))],
    )(*arrays)


def kernel(x, norm_pre_g, w_in, mu_shift, w0, w_lora_up, a0, a_lora_up, k_k, k_a, r_k, lnx_g, lnx_b, conv_w, conv_b, cln_g, cln_b, w_pw2, b_pw2, w_out, norm_post_g, loss_target, m_norm_pre_g, m_w_in, m_mu_shift, m_w0, m_w_lora_up, m_a0, m_a_lora_up, m_k_k, m_k_a, m_r_k, m_lnx_g, m_lnx_b, m_conv_w, m_conv_b, m_cln_g, m_cln_b, m_w_pw2, m_b_pw2, m_w_out, m_norm_post_g, v_norm_pre_g, v_w_in, v_mu_shift, v_w0, v_w_lora_up, v_a0, v_a_lora_up, v_k_k, v_k_a, v_r_k, v_lnx_g, v_lnx_b, v_conv_w, v_conv_b, v_cln_g, v_cln_b, v_w_pw2, v_b_pw2, v_w_out, v_norm_post_g):
    _, T, D = x.shape
    RW = w0.shape[0]
    CW = conv_b.shape[0]
    head = r_k.shape[1]
    lora = w_lora_up.shape[0]
    ktaps = conv_w.shape[0]
    assert RW == CW and 2 * lora <= LORA_PAD and ktaps - 1 <= CONV_HALO
    n_in = 3 * RW + 2 * lora + RW + 3 * CW
    shard = n_in // 4
    PW = 7 * RW + LORA_PAD
    off_l = 7 * RW
    tm = min(256, T // 2)
    tm_wide = min(128, T // 2)
    row = lambda vec: vec.reshape(1, -1)
    x2, tgt2 = x[0], loss_target[0]

    halves = lambda a: a.reshape(2, a.shape[0] // 2, a.shape[1])
    conv_w_p = jnp.concatenate([conv_w, jnp.zeros((CONV_HALO - ktaps, CW // 4), f32)], axis=0)
    g_wup, g_aup, g_cw, g_pw2, g_wout, g_win = _gather_chips(
        [halves(a) for a in (w_lora_up, a_lora_up, conv_w_p, w_pw2.astype(bf16), w_out.astype(bf16), w_in.astype(bf16))])
    cat_cols = lambda g: jnp.concatenate([g[s].reshape(-1, g.shape[-1]) for s in range(4)], axis=1)
    win_full = cat_cols(g_win)
    lo = 3 * RW
    wp = jnp.concatenate([win_full[:, :lo], win_full[:, lo + 2 * lora:], win_full[:, lo:lo + 2 * lora],
                          jnp.zeros((D, LORA_PAD - 2 * lora), bf16)], axis=1)
    wup_full, aup_full, cw_p = cat_cols(g_wup), cat_cols(g_aup), cat_cols(g_cw)
    zl = lambda n: jnp.zeros((n, RW), f32)
    wup_p = jnp.concatenate([wup_full, zl(LORA_PAD - lora)], axis=0)
    aup_p = jnp.concatenate([zl(lora), aup_full, zl(LORA_PAD - 2 * lora)], axis=0)
    pw2_full = g_pw2.reshape(CW, CW)
    wout_full = g_wout.reshape(RW + CW, D)
    mu_r, mu_k, mu_v = (row(mu_shift[s * RW:(s + 1) * RW]) for s in range(3))
    mu_l = row(jnp.concatenate([mu_shift[3 * RW:], jnp.zeros((LORA_PAD - 2 * lora,), f32)]))

    npg = row(norm_pre_g)
    (h,) = _row_fwd("rms_pre", _fn_rms_pre, [(npg, False)], [(x2, 0, D, False)], [(D, bf16)], T, tm)
    proj = _matmul("proj", h, wp, "nn", f32)
    xs_r = _shift_fwd("shift_r", proj, 0, RW, mu_r, T, tm)
    xs_k = _shift_fwd("shift_k", proj, RW, RW, mu_k, T, tm)
    xs_v = _shift_fwd("shift_v", proj, 2 * RW, RW, mu_v, T, tm)
    xs_l = _shift_fwd("shift_l", proj, off_l, LORA_PAD, mu_l, T, tm)
    lora_params = [(row(w0), False), (wup_p, False), (row(a0), False), (aup_p, False)]
    qw, qa = _row_fwd("lora_up", _fn_lora, lora_params, [(xs_l, 0, LORA_PAD, False)], [(RW, f32), (RW, f32)], T, tm)
    ncol = RW // _tile(RW, 512)
    fn_pre = functools.partial(_fn_rwkv_pre, head)
    pre_params = [(row(k_k), True), (row(k_a), True)]
    pre_rows = [(xs_k, 0, RW, True), (qw, 0, RW, True), (qa, 0, RW, True)]
    lw, k_h, a_rec, b_rec = _row_fwd("rwkv_pre", fn_pre, pre_params, pre_rows, [(RW, f32)] * 4, T, tm, ncol)
    wkv_in = (xs_r, lw, k_h, xs_v, a_rec, b_rec)
    c_w, c_u, c_rt, c_pb, c_y0, c_bend, c_z = _wkv_local_fwd(*wkv_in, head, T, RW)
    wkv_loc = (c_w, c_u, c_rt, c_pb, c_bend, lw, c_y0, c_z)
    y_wkv, states = _wkv_state_fwd(*wkv_loc, head, T, RW)
    fn_post = functools.partial(_fn_rwkv_post, head)
    post_params = [(row(lnx_g), True), (row(lnx_b), True), (r_k.reshape(1, RW), True)]
    post_rows = [(y_wkv, 0, RW, True), (xs_r, 0, RW, True), (k_h, 0, RW, True), (xs_v, 0, RW, True),
                 (proj, 3 * RW, RW, True)]
    (y_rwkv,) = _row_fwd("rwkv_post", fn_post, post_params, post_rows, [(RW, bf16)], T, tm, ncol)

    c_pre = _conv_fwd(proj, 4 * RW, 5 * RW, CW, cw_p, row(conv_b), ktaps, T, tm)
    ln_params = [(row(cln_g), False), (row(cln_b), False)]
    (c_act,) = _row_fwd("conv_ln", _fn_conv_ln, ln_params, [(c_pre, 0, CW, False)], [(CW, bf16)], T, tm)
    c2 = _matmul("pw2", c_act, pw2_full, "nn", f32)
    cpost_params = [(row(b_pw2), True)]
    cpost_rows = [(c2, 0, CW, True), (proj, 6 * RW, CW, True)]
    (y_conv,) = _row_fwd("conv_post", _fn_conv_post, cpost_params, cpost_rows, [(CW, bf16)], T, tm, ncol)

    mix = jnp.concatenate([y_rwkv, y_conv], axis=1)
    out = _matmul("out_proj", mix, wout_full, "nn", f32)
    d_out, gx_res, loss_part, g_npost = _post(out, x2, tgt2, row(norm_post_g), T, D, tm_wide)

    g_wout_full = _matmul("d_w_out", mix, d_out, "tn", bf16)
    d_mix = _matmul("d_mix", d_out, wout_full, "nt", f32)

    d_c2, d_gconv, g_bpw2 = _row_bwd("conv_post_bwd", _fn_conv_post, cpost_params, cpost_rows,
                                      [(d_mix, RW, CW, True)], [bf16, bf16], T, tm, ncol)
    g_pw2_full = _matmul("d_w_pw2", c_act, d_c2, "tn", bf16)
    d_cact = _matmul("d_c_act", d_c2, pw2_full, "nt", f32)
    d_cpre, g_clng, g_clnb = _row_bwd("conv_ln_bwd", _fn_conv_ln, ln_params, [(c_pre, 0, CW, False)],
                                      [(d_cact, 0, CW, False)], [f32], T, tm)
    d_gluv, d_glug, g_cw_p, g_cb = _conv_bwd(proj, 4 * RW, 5 * RW, CW, cw_p, d_cpre, ktaps, T, tm)

    d_y, dr_x, dk_x, dv_x, d_grwkv, g_lnxg, g_lnxb, g_rk = _row_bwd(
        "rwkv_post_bwd", fn_post, post_params, post_rows, [(d_mix, 0, RW, True)], [f32, f32, f32, f32, bf16], T, tm, ncol)
    d_cw, d_cu, d_crt, d_cpb, d_cbend, d_lw_dec, d_cz = _wkv_state_bwd(*wkv_loc, states, d_y, head, T, RW)
    d_xr, d_lw, d_kh, d_xv, d_a, d_b = _wkv_local_bwd(
        *wkv_in, (d_cw, d_cu, d_crt, d_cpb, d_y, d_cbend, d_cz), d_lw_dec, dr_x, dk_x, dv_x, head, T, RW)
    pre_cots = [(d_lw, 0, RW, True), (d_kh, 0, RW, True), (d_a, 0, RW, True), (d_b, 0, RW, True)]
    d_xk, d_qw, d_qa, g_kk, g_ka = _row_bwd("rwkv_pre_bwd", fn_pre, pre_params, pre_rows, pre_cots, [f32, f32, f32],
                                            T, tm, ncol)
    d_xl, g_w0, g_wup_p, g_a0, g_aup_p = _row_bwd("lora_up_bwd", _fn_lora, lora_params, [(xs_l, 0, LORA_PAD, False)],
                                                  [(d_qw, 0, RW, False), (d_qa, 0, RW, False)], [f32], T, tm)
    dp_r, g_mur = _shift_bwd("shift_r_bwd", proj, 0, RW, mu_r, d_xr, T, tm)
    dp_k, g_muk = _shift_bwd("shift_k_bwd", proj, RW, RW, mu_k, d_xk, T, tm)
    dp_v, g_muv = _shift_bwd("shift_v_bwd", proj, 2 * RW, RW, mu_v, d_xv, T, tm)
    dp_l, g_mul = _shift_bwd("shift_l_bwd", proj, off_l, LORA_PAD, mu_l, d_xl, T, tm)
    d_proj = jnp.concatenate([dp_r, dp_k, dp_v, d_grwkv, d_gluv, d_glug, d_gconv, dp_l], axis=1)

    g_wp = _matmul("d_w_in", h, d_proj, "tn", bf16)
    d_h = _matmul("d_h", d_proj, wp, "nt", bf16, tk_t=2560)
    grad_x2, g_npre = _rms_pre_bwd(x2, npg, d_h, gx_res, T, D, tm_wide)

    g_win_full = jnp.concatenate([g_wp[:, :lo], g_wp[:, off_l:off_l + 2 * lora], g_wp[:, lo:off_l]], axis=1)
    win_pieces = jnp.stack([g_win_full[:, s * shard:(s + 1) * shard].astype(bf16) for s in range(4)])
    win_pieces = win_pieces.reshape(8, D // 2, shard)
    wout_pieces = g_wout_full.astype(bf16).reshape(8, (RW + CW) // 8, D)
    pw2_pieces = g_pw2_full.astype(bf16).reshape(8, CW // 8, CW)
    g_mu = jnp.concatenate([g_mur[0], g_muk[0], g_muv[0], g_mul[0, :2 * lora]])
    pad_rows = lambda a, n: jnp.concatenate([a, jnp.zeros((n - a.shape[0], a.shape[1]), f32)], axis=0)
    n_mu = -(-mu_shift.shape[0] // RW)
    small_vecs = [g_npre.reshape(D // RW, RW), pad_rows(jnp.pad(g_mu, (0, n_mu * RW - g_mu.shape[0])).reshape(n_mu, RW), n_mu),
                  g_w0, g_a0, g_kk, g_ka, g_rk, g_lnxg, g_lnxb, g_cb, g_clng, g_clnb, g_bpw2,
                  g_npost.reshape(D // RW, RW), g_wup_p[:lora], g_aup_p[lora:2 * lora], g_cw_p[:ktaps]]
    n_small = sum(a.shape[0] for a in small_vecs)
    n_small_pad = -(-n_small // (2 * SUBLANES)) * (2 * SUBLANES)
    small = pad_rows(jnp.concatenate(small_vecs, axis=0), n_small_pad)

    r_win, r_wout, r_pw2, r_small = _exchange_pieces([win_pieces, wout_pieces, pw2_pieces], [small])
    s_win = _sum8("sum_w_in", r_win)
    s_wout = _sum8("sum_w_out", r_wout)
    s_pw2 = _sum8("sum_w_pw2", r_pw2)
    s_small = _sum8("sum_small", r_small)
    o_win, o_wout, o_pw2 = _share_sibling([s_win, s_wout, s_pw2])
    south = lax.axis_index("c") == 0

    def both_halves(own, other):
        return jnp.concatenate([jnp.where(south, own, other), jnp.where(south, other, own)], axis=0)

    grad_w_in = both_halves(s_win, o_win)
    grad_w_out = both_halves(s_wout, o_wout)
    grad_w_pw2 = both_halves(s_pw2, o_pw2)

    chip = 2 * lax.axis_index("x") + lax.axis_index("y")
    pos = [0]

    def take(nrows):
        a = s_small[pos[0]:pos[0] + nrows]
        pos[0] += nrows
        return a

    my_cols = lambda a, w: lax.dynamic_slice_in_dim(a, chip * w, w, axis=1)
    grads = {}
    grads["norm_pre_g"] = take(D // RW).reshape(D)
    grads["mu_shift"] = take(n_mu).reshape(-1)[:mu_shift.shape[0]]
    for nm in ["w0", "a0", "k_k", "k_a"]:
        grads[nm] = take(1).reshape(RW)
    grads["r_k"] = take(1).reshape(r_k.shape)
    for nm in ["lnx_g", "lnx_b", "conv_b", "cln_g", "cln_b", "b_pw2"]:
        grads[nm] = take(1).reshape(RW)
    grads["norm_post_g"] = take(D // RW).reshape(D)
    grads["w_lora_up"] = my_cols(take(lora), RW // 4)
    grads["a_lora_up"] = my_cols(take(lora), RW // 4)
    grads["conv_w"] = my_cols(take(ktaps), CW // 4)
    grads["w_in"], grads["w_out"], grads["w_pw2"] = grad_w_in, grad_w_out, grad_w_pw2

    weights = dict(norm_pre_g=norm_pre_g, w_in=w_in, mu_shift=mu_shift, w0=w0, w_lora_up=w_lora_up, a0=a0,
                   a_lora_up=a_lora_up, k_k=k_k, k_a=k_a, r_k=r_k, lnx_g=lnx_g, lnx_b=lnx_b, conv_w=conv_w,
                   conv_b=conv_b, cln_g=cln_g, cln_b=cln_b, w_pw2=w_pw2, b_pw2=b_pw2, w_out=w_out,
                   norm_post_g=norm_post_g)
    ms = dict(norm_pre_g=m_norm_pre_g, w_in=m_w_in, mu_shift=m_mu_shift, w0=m_w0, w_lora_up=m_w_lora_up, a0=m_a0,
              a_lora_up=m_a_lora_up, k_k=m_k_k, k_a=m_k_a, r_k=m_r_k, lnx_g=m_lnx_g, lnx_b=m_lnx_b, conv_w=m_conv_w,
              conv_b=m_conv_b, cln_g=m_cln_g, cln_b=m_cln_b, w_pw2=m_w_pw2, b_pw2=m_b_pw2, w_out=m_w_out,
              norm_post_g=m_norm_post_g)
    vs = dict(norm_pre_g=v_norm_pre_g, w_in=v_w_in, mu_shift=v_mu_shift, w0=v_w0, w_lora_up=v_w_lora_up, a0=v_a0,
              a_lora_up=v_a_lora_up, k_k=v_k_k, k_a=v_k_a, r_k=v_r_k, lnx_g=v_lnx_g, lnx_b=v_lnx_b, conv_w=v_conv_w,
              conv_b=v_conv_b, cln_g=v_cln_g, cln_b=v_cln_b, w_pw2=v_w_pw2, b_pw2=v_b_pw2, w_out=v_w_out,
              norm_post_g=v_norm_post_g)
    names = list(weights)
    big = ["w_in", "w_out", "w_pw2"]
    deltas, new_m, new_v = {}, {}, {}
    for nm in big:
        deltas[nm], new_m[nm], new_v[nm] = _adamw("adamw_" + nm, weights[nm], grads[nm], ms[nm], vs[nm])
    rest = [nm for nm in names if nm not in big]
    sizes = [weights[nm].size for nm in rest]
    total = sum(sizes)
    width = 4 * LANES
    rows_p = -(-total // (width * SUBLANES)) * SUBLANES

    def pack(d):
        flat = jnp.concatenate([d[nm].reshape(-1) for nm in rest])
        return jnp.pad(flat, (0, rows_p * width - total)).reshape(rows_p, width)

    p_d, p_m, p_v = _adamw("adamw_small", pack(weights), pack(grads), pack(ms), pack(vs))
    o = 0
    for nm, sz in zip(rest, sizes):
        shp = weights[nm].shape
        deltas[nm] = p_d.reshape(-1)[o:o + sz].reshape(shp)
        new_m[nm] = p_m.reshape(-1)[o:o + sz].reshape(shp)
        new_v[nm] = p_v.reshape(-1)[o:o + sz].reshape(shp)
        o += sz

    loss = lax.psum(loss_part[0, 0], ("x", "y", "c"))
    grad_x = grad_x2[None]
    return (loss, grad_x, *[grads[nm] for nm in names], *[deltas[nm] for nm in names],
            *[new_m[nm] for nm in names], *[new_v[nm] for nm in names])
```

```python
import functools

import jax
import jax.numpy as jnp
from jax import lax
from jax.experimental import pallas as pl
from jax.experimental.pallas import tpu as pltpu

f32 = jnp.float32
bf16 = jnp.bfloat16
MESH = pl.DeviceIdType.MESH
HI = lax.Precision.HIGHEST

NORM_EPS = 1e-6
LN_EPS = 1e-5
ADAM_LR, ADAM_B1, ADAM_B2, ADAM_EPS, ADAM_WD, ADAM_STEP = 0.001, 0.9, 0.999, 1e-08, 0.01, 10

LANES = 128
SUBLANES = 8
LORA_PAD = 256
CONV_HALO = 32
WKV_CHUNK = 64
WKV_HEADS = 16
WKV_STATE_HEADS = 32
WKV_PREC = lax.Precision.HIGH
SHARE_CHUNKS = 8
VMEM_LIMIT = 56 * 1024 * 1024


def _cparams(n_axes):
    return pltpu.CompilerParams(dimension_semantics=("arbitrary",) * n_axes, vmem_limit_bytes=VMEM_LIMIT)


def _tile(dim, target):
    best = None
    t = LANES
    while t <= min(dim, target):
        if dim % t == 0:
            best = t
        t += LANES
    return best if best is not None else dim


def _mm(a, b, prec=None):
    return lax.dot_general(a, b, (((1,), (0,)), ((), ())), precision=prec, preferred_element_type=f32)


def _mm_nt(a, b, prec=None):
    return lax.dot_general(a, b, (((1,), (1,)), ((), ())), precision=prec, preferred_element_type=f32)


def _mm_tn(a, b, prec=None):
    return lax.dot_general(a, b, (((0,), (0,)), ((), ())), precision=prec, preferred_element_type=f32)


@jax.custom_vjp
def _bmm(a, b):
    return _mm(a.astype(bf16), b.astype(bf16))


def _bmm_fwd(a, b):
    return _bmm(a, b), (a, b)


def _bmm_bwd(res, dc):
    a, b = res
    dcb = dc.astype(bf16)
    return _mm_nt(dcb, b.astype(bf16)), _mm_tn(a.astype(bf16), dcb)


_bmm.defvjp(_bmm_fwd, _bmm_bwd)


def _matmul(name, a, b, mode, out_dtype, tm_t=1024, tn_t=768, tk_t=1024):
    if mode == "nn":
        (M, K), (_, N) = a.shape, b.shape
    elif mode == "nt":
        (M, K), (N, _) = a.shape, b.shape
    else:
        (K, M), (_, N) = a.shape, b.shape
    tm, tn, tk = _tile(M, tm_t), _tile(N, tn_t), _tile(K, tk_t)
    nk = K // tk
    dot = {"nn": _mm, "nt": _mm_nt, "tn": _mm_tn}[mode]

    def body(a_ref, b_ref, o_ref, acc_ref):
        k = pl.program_id(2)

        @pl.when(k == 0)
        def _():
            acc_ref[...] = jnp.zeros_like(acc_ref)

        acc_ref[...] += dot(a_ref[...], b_ref[...])

        @pl.when(k == nk - 1)
        def _():
            o_ref[...] = acc_ref[...].astype(o_ref.dtype)

    a_spec = {"nn": pl.BlockSpec((tm, tk), lambda i, j, k: (i, k)),
              "nt": pl.BlockSpec((tm, tk), lambda i, j, k: (i, k)),
              "tn": pl.BlockSpec((tk, tm), lambda i, j, k: (k, i))}[mode]
    b_spec = {"nn": pl.BlockSpec((tk, tn), lambda i, j, k: (k, j)),
              "nt": pl.BlockSpec((tn, tk), lambda i, j, k: (j, k)),
              "tn": pl.BlockSpec((tk, tn), lambda i, j, k: (k, j))}[mode]
    return pl.pallas_call(
        body, name=name, grid=(M // tm, N // tn, nk),
        in_specs=[a_spec, b_spec],
        out_specs=pl.BlockSpec((tm, tn), lambda i, j, k: (i, j)),
        out_shape=jax.ShapeDtypeStruct((M, N), out_dtype),
        scratch_shapes=[pltpu.VMEM((tm, tn), f32)],
        compiler_params=_cparams(3),
    )(a, b)


def _row_spec(op, tm, ncol):
    arr, off, width, tiled = op
    if tiled:
        cw = width // ncol
        return pl.BlockSpec((tm, cw), lambda j, i: (i, off // cw + j))
    return pl.BlockSpec((tm, width), lambda j, i: (i, off // width))


def _param_spec(p, ncol):
    arr, tiled = p
    rows, width = arr.shape
    if tiled:
        return pl.BlockSpec((rows, width // ncol), lambda j, i: (0, j))
    return pl.BlockSpec((rows, width), lambda j, i: (0, 0))


def _row_fwd(name, fn, params, rows, outs, T, tm, ncol=1):
    npar, nrow = len(params), len(rows)

    def body(*refs):
        pv = [r[...] for r in refs[:npar]]
        rv = [r[...].astype(f32) for r in refs[npar:npar + nrow]]
        res = fn(*pv, *rv)
        for o_ref, val in zip(refs[npar + nrow:], res):
            o_ref[...] = val.astype(o_ref.dtype)

    return pl.pallas_call(
        body, name=name, grid=(ncol, T // tm),
        in_specs=[_param_spec(p, ncol) for p in params] + [_row_spec(r, tm, ncol) for r in rows],
        out_specs=[pl.BlockSpec((tm, w // ncol), lambda j, i: (i, j)) for w, _ in outs],
        out_shape=[jax.ShapeDtypeStruct((T, w), dt) for w, dt in outs],
        compiler_params=_cparams(2),
    )(*[p[0] for p in params], *[r[0] for r in rows])


def _row_bwd(name, fn, params, rows, cots, row_grads, T, tm, ncol=1):
    npar, nrow, ncot = len(params), len(rows), len(cots)
    want = [k for k, dt in enumerate(row_grads) if dt is not None]

    def body(*refs):
        pv = [r[...] for r in refs[:npar]]
        rv = [r[...].astype(f32) for r in refs[npar:npar + nrow]]
        cv = tuple(r[...].astype(f32) for r in refs[npar + nrow:npar + nrow + ncot])
        out_refs = refs[npar + nrow + ncot:]
        _, vjp = jax.vjp(fn, *pv, *rv)
        grads = vjp(cv)
        for o_ref, k in zip(out_refs[:len(want)], want):
            o_ref[...] = grads[npar + k].astype(o_ref.dtype)
        j, i = pl.program_id(0), pl.program_id(1)
        for o_ref, p, g in zip(out_refs[len(want):], params, grads[:npar]):
            first = (i == 0) if p[1] else jnp.logical_and(i == 0, j == 0)

            @pl.when(first)
            def _():
                o_ref[...] = jnp.zeros_like(o_ref)

            o_ref[...] += g

    def grad_spec(op):
        arr, off, width, tiled = op
        if tiled:
            return pl.BlockSpec((tm, width // ncol), lambda j, i: (i, j)), (T, width)
        return pl.BlockSpec((tm, width), lambda j, i: (i, j)), (T, width * ncol)

    gspecs = [grad_spec(rows[k]) for k in want]
    return pl.pallas_call(
        body, name=name, grid=(ncol, T // tm),
        in_specs=[_param_spec(p, ncol) for p in params] + [_row_spec(r, tm, ncol) for r in rows]
        + [_row_spec(c, tm, ncol) for c in cots],
        out_specs=[s for s, _ in gspecs] + [_param_spec(p, ncol) for p in params],
        out_shape=[jax.ShapeDtypeStruct(shp, row_grads[k]) for (_, shp), k in zip(gspecs, want)]
        + [jax.ShapeDtypeStruct(p[0].shape, f32) for p in params],
        compiler_params=_cparams(2),
    )(*[p[0] for p in params], *[r[0] for r in rows], *[c[0] for c in cots])


def _seg_sum(x, head):
    li = lax.broadcasted_iota(jnp.int32, (LANES, LANES), 0) // head
    lj = lax.broadcasted_iota(jnp.int32, (LANES, LANES), 1) // head
    q = (li == lj).astype(f32)
    parts = [_mm(x[:, s:s + LANES], q, HI) for s in range(0, x.shape[1], LANES)]
    return parts[0] if len(parts) == 1 else jnp.concatenate(parts, axis=1)


def _sigmoid(z):
    return 1.0 / (1.0 + jnp.exp(-z))


def _silu(z):
    return z * _sigmoid(z)


def _rms(g, x):
    return x * lax.rsqrt(jnp.mean(x * x, axis=-1, keepdims=True) + NORM_EPS) * g


def _fn_rms_pre(g, x):
    return (_rms(g, x),)


def _fn_lora(w0, wup, a0, aup, xl):
    qw = w0 + _bmm(jnp.tanh(xl), wup)
    qa = a0 + _bmm(xl, aup)
    return qw, qa


def _fn_rwkv_pre(head, k_k, k_a, xk, qw, qa):
    w_log = -(jnp.maximum(-qw, 0.0) + jnp.log(1.0 + jnp.exp(-jnp.abs(qw)))) - 0.5
    lw = -jnp.exp(w_log)
    a_sig = _sigmoid(qa)
    kk = xk * k_k
    kk = kk / jnp.maximum(jnp.sqrt(_seg_sum(kk * kk, head)), 1e-12)
    k_h = xk * (1.0 + (a_sig - 1.0) * k_a)
    return lw, k_h, -kk, kk * a_sig


def _fn_rwkv_post(head, lnx_g, lnx_b, r_k, y, r, k_h, v, g):
    inv = 1.0 / head
    mu = _seg_sum(y, head) * inv
    d = y - mu
    var = _seg_sum(d * d, head) * inv
    yn = d * lax.rsqrt(var + 1e-5 * head) * lnx_g + lnx_b
    bonus = _seg_sum(r * k_h * r_k, head) * v
    return ((yn + bonus) * _silu(g),)


def _fn_conv_ln(cln_g, cln_b, c):
    mu = jnp.mean(c, axis=-1, keepdims=True)
    d = c - mu
    var = jnp.mean(d * d, axis=-1, keepdims=True)
    return (_silu(d * lax.rsqrt(var + LN_EPS) * cln_g + cln_b),)


def _fn_conv_post(b_pw2, c2, g):
    return ((c2 + b_pw2) * _silu(g),)


def _post(out, x, tgt, g, T, D, tm):
    def body(g_ref, o_ref, x_ref, t_ref, dout_ref, gx_ref, loss_ref, dg_ref):
        i = pl.program_id(0)
        o, vjp = jax.vjp(_rms, g_ref[...], o_ref[...])
        err = x_ref[...] + o - t_ref[...]
        d_y = err * (1.0 / D)
        dg, d_out = vjp(d_y)
        dout_ref[...] = d_out.astype(dout_ref.dtype)
        gx_ref[...] = d_y

        @pl.when(i == 0)
        def _():
            loss_ref[...] = jnp.zeros_like(loss_ref)
            dg_ref[...] = jnp.zeros_like(dg_ref)

        loss_ref[...] += jnp.sum(err * err, keepdims=True) * (0.5 / D)
        dg_ref[...] += dg

    row = pl.BlockSpec((tm, D), lambda i: (i, 0))
    vec = pl.BlockSpec((1, D), lambda i: (0, 0))
    return pl.pallas_call(
        body, name="post_loss", grid=(T // tm,),
        in_specs=[vec, row, row, row],
        out_specs=[row, row, pl.BlockSpec((1, 1), lambda i: (0, 0)), vec],
        out_shape=[jax.ShapeDtypeStruct((T, D), bf16), jax.ShapeDtypeStruct((T, D), f32),
                   jax.ShapeDtypeStruct((1, 1), f32), jax.ShapeDtypeStruct((1, D), f32)],
        compiler_params=_cparams(1),
    )(g, out, x, tgt)


def _rms_pre_bwd(x, g, dh, gx_res, T, D, tm):
    def body(g_ref, x_ref, dh_ref, res_ref, dx_ref, dg_ref):
        i = pl.program_id(0)
        _, vjp = jax.vjp(_rms, g_ref[...], x_ref[...])
        dg, dx = vjp(dh_ref[...].astype(f32))
        dx_ref[...] = dx + res_ref[...]

        @pl.when(i == 0)
        def _():
            dg_ref[...] = jnp.zeros_like(dg_ref)

        dg_ref[...] += dg

    row = pl.BlockSpec((tm, D), lambda i: (i, 0))
    vec = pl.BlockSpec((1, D), lambda i: (0, 0))
    return pl.pallas_call(
        body, name="rms_pre_bwd", grid=(T // tm,),
        in_specs=[vec, row, row, row], out_specs=[row, vec],
        out_shape=[jax.ShapeDtypeStruct((T, D), f32), jax.ShapeDtypeStruct((1, D), f32)],
        compiler_params=_cparams(1),
    )(g, x, dh, gx_res)


def _prev_rows(cur, halo_ref, first):
    top = jnp.where(first, 0.0, halo_ref[SUBLANES - 1:SUBLANES, :])
    rolled = pltpu.roll(cur, 1, 0)
    rid = lax.broadcasted_iota(jnp.int32, cur.shape, 0)
    return jnp.where(rid == 0, top, rolled)


def _shift_fwd(name, proj, off, width, mu, T, tm):
    cw = _tile(width, 512)
    ncol, cb = width // cw, off // cw
    hb = tm // SUBLANES

    def body(mu_ref, cur_ref, halo_ref, o_ref):
        i = pl.program_id(1)
        cur = cur_ref[...]
        prev = _prev_rows(cur, halo_ref, i == 0)
        o_ref[...] = cur + (prev - cur) * mu_ref[...]

    return pl.pallas_call(
        body, name=name, grid=(ncol, T // tm),
        in_specs=[pl.BlockSpec((1, cw), lambda j, i: (0, j)),
                  pl.BlockSpec((tm, cw), lambda j, i: (i, cb + j)),
                  pl.BlockSpec((SUBLANES, cw), lambda j, i: (jnp.maximum(i * hb - 1, 0), cb + j))],
        out_specs=pl.BlockSpec((tm, cw), lambda j, i: (i, j)),
        out_shape=jax.ShapeDtypeStruct((T, width), f32),
        compiler_params=_cparams(2),
    )(mu, proj, proj)


def _shift_bwd(name, proj, off, width, mu, dxs, T, tm):
    cw = _tile(width, 512)
    ncol, cb = width // cw, off // cw
    hb, nt = tm // SUBLANES, T // tm

    def body(mu_ref, cur_ref, halo_ref, d_ref, dnext_ref, o_ref, dmu_ref):
        i = pl.program_id(1)
        cur = cur_ref[...]
        prev = _prev_rows(cur, halo_ref, i == 0)
        d = d_ref[...]
        bottom = jnp.where(i == nt - 1, 0.0, dnext_ref[0:1, :])
        rid = lax.broadcasted_iota(jnp.int32, d.shape, 0)
        d_next = jnp.where(rid == tm - 1, bottom, pltpu.roll(d, tm - 1, 0))
        mu_v = mu_ref[...]
        o_ref[...] = (d * (1.0 - mu_v) + d_next * mu_v).astype(o_ref.dtype)

        @pl.when(i == 0)
        def _():
            dmu_ref[...] = jnp.zeros_like(dmu_ref)

        dmu_ref[...] += jnp.sum(d * (prev - cur), axis=0, keepdims=True)

    return pl.pallas_call(
        body, name=name, grid=(ncol, nt),
        in_specs=[pl.BlockSpec((1, cw), lambda j, i: (0, j)),
                  pl.BlockSpec((tm, cw), lambda j, i: (i, cb + j)),
                  pl.BlockSpec((SUBLANES, cw), lambda j, i: (jnp.maximum(i * hb - 1, 0), cb + j)),
                  pl.BlockSpec((tm, cw), lambda j, i: (i, j)),
                  pl.BlockSpec((SUBLANES, cw), lambda j, i: (jnp.minimum((i + 1) * hb, nt * hb - 1), j))],
        out_specs=[pl.BlockSpec((tm, cw), lambda j, i: (i, j)), pl.BlockSpec((1, cw), lambda j, i: (0, j))],
        out_shape=[jax.ShapeDtypeStruct((T, width), bf16), jax.ShapeDtypeStruct((1, width), f32)],
        compiler_params=_cparams(2),
    )(mu, proj, proj, dxs, dxs)


def _conv_fwd(proj, off_v, off_g, width, conv_w, conv_b, ktaps, T, tm):
    cw = _tile(width, 512)
    ncol = width // cw
    hb = tm // CONV_HALO
    lead = CONV_HALO - (ktaps - 1)

    def body(w_ref, b_ref, v_ref, g_ref, hv_ref, hg_ref, o_ref, u_ref):
        i = pl.program_id(1)
        halo = hv_ref[...] * _sigmoid(hg_ref[...])
        u_ref[0:CONV_HALO, :] = jnp.where(i == 0, 0.0, halo)
        u_ref[CONV_HALO:, :] = v_ref[...] * _sigmoid(g_ref[...])
        acc = jnp.zeros((tm, cw), f32) + b_ref[...]
        for j in range(ktaps):
            acc = acc + u_ref[pl.ds(lead + j, tm), :] * w_ref[j:j + 1, :]
        o_ref[...] = acc

    def tile(off):
        return pl.BlockSpec((tm, cw), lambda j, i: (i, off // cw + j))

    def halo(off):
        return pl.BlockSpec((CONV_HALO, cw), lambda j, i: (jnp.maximum(i * hb - 1, 0), off // cw + j))

    return pl.pallas_call(
        body, name="conv_fwd", grid=(ncol, T // tm),
        in_specs=[pl.BlockSpec((CONV_HALO, cw), lambda j, i: (0, j)), pl.BlockSpec((1, cw), lambda j, i: (0, j)),
                  tile(off_v), tile(off_g), halo(off_v), halo(off_g)],
        out_specs=pl.BlockSpec((tm, cw), lambda j, i: (i, j)),
        out_shape=jax.ShapeDtypeStruct((T, width), f32),
        scratch_shapes=[pltpu.VMEM((CONV_HALO + tm, cw), f32)],
        compiler_params=_cparams(2),
    )(conv_w, conv_b, proj, proj, proj, proj)


def _conv_bwd(proj, off_v, off_g, width, conv_w, dc, ktaps, T, tm):
    cw = _tile(width, 512)
    ncol = width // cw
    hb, nt = tm // CONV_HALO, T // tm
    lead = CONV_HALO - (ktaps - 1)

    def body(w_ref, v_ref, g_ref, hv_ref, hg_ref, dc_ref, dcn_ref, dv_ref, dg_ref, dw_ref, db_ref, u_ref, d_ref):
        i = pl.program_id(1)
        halo = hv_ref[...] * _sigmoid(hg_ref[...])
        u_ref[0:CONV_HALO, :] = jnp.where(i == 0, 0.0, halo)
        sig = _sigmoid(g_ref[...])
        gv = v_ref[...]
        u_ref[CONV_HALO:, :] = gv * sig
        dcur = dc_ref[...]
        d_ref[0:tm, :] = dcur
        d_ref[tm:, :] = jnp.where(i == nt - 1, 0.0, dcn_ref[...])

        @pl.when(i == 0)
        def _():
            dw_ref[...] = jnp.zeros_like(dw_ref)
            db_ref[...] = jnp.zeros_like(db_ref)

        du = jnp.zeros((tm, cw), f32)
        for j in range(ktaps):
            du = du + d_ref[pl.ds(ktaps - 1 - j, tm), :] * w_ref[j:j + 1, :]
            dw_ref[j:j + 1, :] += jnp.sum(u_ref[pl.ds(lead + j, tm), :] * dcur, axis=0, keepdims=True)
        db_ref[...] += jnp.sum(dcur, axis=0, keepdims=True)
        dv_ref[...] = (du * sig).astype(dv_ref.dtype)
        dg_ref[...] = (du * gv * sig * (1.0 - sig)).astype(dg_ref.dtype)

    def tile(off):
        return pl.BlockSpec((tm, cw), lambda j, i: (i, off // cw + j))

    def halo(off):
        return pl.BlockSpec((CONV_HALO, cw), lambda j, i: (jnp.maximum(i * hb - 1, 0), off // cw + j))

    return pl.pallas_call(
        body, name="conv_bwd", grid=(ncol, nt),
        in_specs=[pl.BlockSpec((CONV_HALO, cw), lambda j, i: (0, j)),
                  tile(off_v), tile(off_g), halo(off_v), halo(off_g),
                  pl.BlockSpec((tm, cw), lambda j, i: (i, j)),
                  pl.BlockSpec((CONV_HALO, cw), lambda j, i: (jnp.minimum((i + 1) * hb, nt * hb - 1), j))],
        out_specs=[pl.BlockSpec((tm, cw), lambda j, i: (i, j)), pl.BlockSpec((tm, cw), lambda j, i: (i, j)),
                   pl.BlockSpec((CONV_HALO, cw), lambda j, i: (0, j)), pl.BlockSpec((1, cw), lambda j, i: (0, j))],
        out_shape=[jax.ShapeDtypeStruct((T, width), bf16), jax.ShapeDtypeStruct((T, width), bf16),
                   jax.ShapeDtypeStruct((CONV_HALO, width), f32), jax.ShapeDtypeStruct((1, width), f32)],
        scratch_shapes=[pltpu.VMEM((CONV_HALO + tm, cw), f32), pltpu.VMEM((tm + CONV_HALO, cw), f32)],
        compiler_params=_cparams(2),
    )(conv_w, proj, proj, proj, proj, dc, dc)


def _each(f, *lists):
    return [f(*xs) for xs in zip(*lists)]


def _wkv_local(r, lw, k, v, a, b):
    C = r[0].shape[0]
    P = WKV_PREC
    row = lax.broadcasted_iota(jnp.int32, (C, C), 0)
    col = lax.broadcasted_iota(jnp.int32, (C, C), 1)
    incl, strict = row >= col, row > col
    tri = incl.astype(f32)
    zero = jnp.zeros((C, C), f32)
    G = _each(lambda x: _mm(tri, x, P), lw)
    to_end = _each(lambda x, g: jnp.exp(jnp.sum(x, axis=0, keepdims=True) - g), lw, G)
    e_g = _each(jnp.exp, G)
    e_ng = _each(lambda g: jnp.exp(-g), G)
    At = _each(lambda x, g, w: x * jnp.exp(g - w), a, G, lw)
    Rt = _each(jnp.multiply, r, e_g)
    Kt = _each(jnp.multiply, k, e_ng)
    Bt = _each(jnp.multiply, b, e_ng)
    sc = _each(lambda at, rt, bt, kt: _mm_nt(jnp.concatenate([at, rt], axis=0), jnp.concatenate([bt, kt], axis=0), P),
               At, Rt, Bt, Kt)
    L = _each(lambda s: jnp.where(strict, s[:C, :C], zero), sc)
    M = _each(lambda s: jnp.where(strict, s[:C, C:], zero), sc)
    Pb = _each(lambda s: jnp.where(incl, s[C:, :C], zero), sc)
    Pk = _each(lambda s: jnp.where(incl, s[C:, C:], zero), sc)
    MPk = _each(lambda m, p, x: _bmm(jnp.concatenate([m, p], axis=0), x), M, Pk, v)
    WU = _each(lambda at, mp: jnp.concatenate([at, mp[:C]], axis=1), At, MPk)
    Lp = L
    n = 1
    while n < C:
        n *= 2
        if n < C:
            step = _each(lambda l, x: _bmm(l, jnp.concatenate([x, l], axis=1)), Lp, WU)
            WU = _each(lambda x, s: x + s[:, :x.shape[1]], WU, step)
            Lp = _each(lambda x, s: s[:, x.shape[1]:], WU, step)
        else:
            WU = _each(lambda x, l: x + _bmm(l, x), WU, Lp)
    N = r[0].shape[1]
    W = _each(lambda x: x[:, :N], WU)
    U = _each(lambda x: x[:, N:], WU)
    Y0 = _each(lambda mp: mp[C:], MPk)
    Bend = _each(jnp.multiply, b, to_end)
    Z = _each(lambda x, y, e: _mm_tn(x, y * e, P), v, k, to_end)
    return W, U, Rt, Pb, Y0, Bend, Z


def _wkv_state(S0, W, U, Rt, Pb, Bend, lw, Y0, Z):
    P = WKV_PREC
    C = W[0].shape[0]
    WR = _each(lambda w, rt, s: _mm_nt(jnp.concatenate([w, rt], axis=0), s, P), W, Rt, S0)
    X = _each(lambda wr, u: wr[:C] + u, WR, U)
    y = _each(lambda p, x, wr, c: _mm(p, x, P) + wr[C:] + c, Pb, X, WR, Y0)
    S1 = _each(lambda s, w, x, e, z: s * jnp.exp(jnp.sum(w, axis=0, keepdims=True)) + _mm_tn(x, e, P) + z,
               S0, lw, X, Bend, Z)
    return y, S1


def _wkv_dims(head, T, RW, heads_per_step):
    C = min(WKV_CHUNK, T)
    nh = RW // head
    hb = min(heads_per_step, nh)
    return C, nh, hb, hb * head, T // C


def _heads(ref, hb, head):
    return [ref[:, h * head:(h + 1) * head] for h in range(hb)]


def _put_heads(ref, vals, head):
    for h, val in enumerate(vals):
        ref[:, h * head:(h + 1) * head] = val


def _wkv_local_fwd(r, lw, k, v, a, b, head, T, RW):
    C, nh, hb, bw, nc = _wkv_dims(head, T, RW, WKV_HEADS)

    def body(*refs):
        ins, outs = refs[:6], refs[6:]
        res = _wkv_local(*[_heads(x, hb, head) for x in ins])
        for o_ref, vals in zip(outs[:6], res[:6]):
            _put_heads(o_ref, vals, head)
        for h in range(hb):
            outs[6][0, h] = res[6][h]

    blk = pl.BlockSpec((C, bw), lambda g, c: (c, g))
    sq = pl.BlockSpec((1, hb, head, head), lambda g, c: (c, g, 0, 0))
    return pl.pallas_call(
        body, name="wkv_local", grid=(nh // hb, nc),
        in_specs=[blk] * 6, out_specs=[blk] * 6 + [sq],
        out_shape=[jax.ShapeDtypeStruct((T, RW), f32)] * 6 + [jax.ShapeDtypeStruct((nc, nh, head, head), f32)],
        compiler_params=_cparams(2),
    )(r, lw, k, v, a, b)


def _wkv_state_fwd(W, U, Rt, Pb, Bend, lw, Y0, Z, head, T, RW):
    C, nh, hb, bw, nc = _wkv_dims(head, T, RW, WKV_STATE_HEADS)

    def body(w_ref, u_ref, rt_ref, pb_ref, be_ref, lw_ref, y0_ref, z_ref, y_ref, st_ref, s_ref):
        @pl.when(pl.program_id(1) == 0)
        def _():
            s_ref[...] = jnp.zeros_like(s_ref)

        S0 = [s_ref[h] for h in range(hb)]
        for h in range(hb):
            st_ref[0, h] = S0[h]
        rows = [_heads(x, hb, head) for x in (w_ref, u_ref, rt_ref, pb_ref, be_ref, lw_ref, y0_ref)]
        y, S1 = _wkv_state(S0, *rows, [z_ref[0, h] for h in range(hb)])
        _put_heads(y_ref, y, head)
        for h in range(hb):
            s_ref[h] = S1[h]

    blk = pl.BlockSpec((C, bw), lambda g, c: (c, g))
    sq = pl.BlockSpec((1, hb, head, head), lambda g, c: (c, g, 0, 0))
    return pl.pallas_call(
        body, name="wkv_state", grid=(nh // hb, nc),
        in_specs=[blk] * 7 + [sq], out_specs=[blk, sq],
        out_shape=[jax.ShapeDtypeStruct((T, RW), f32), jax.ShapeDtypeStruct((nc, nh, head, head), f32)],
        scratch_shapes=[pltpu.VMEM((hb, head, head), f32)],
        compiler_params=_cparams(2),
    )(W, U, Rt, Pb, Bend, lw, Y0, Z)


def _wkv_state_bwd(W, U, Rt, Pb, Bend, lw, Y0, Z, states, dy, head, T, RW):
    C, nh, hb, bw, nc = _wkv_dims(head, T, RW, WKV_STATE_HEADS)

    def body(w_ref, u_ref, rt_ref, pb_ref, be_ref, lw_ref, y0_ref, z_ref, st_ref, dy_ref,
             dw_ref, du_ref, drt_ref, dpb_ref, dbe_ref, dlw_ref, dz_ref, ds_ref):
        @pl.when(pl.program_id(1) == 0)
        def _():
            ds_ref[...] = jnp.zeros_like(ds_ref)

        dS1 = [ds_ref[h] for h in range(hb)]
        for h in range(hb):
            dz_ref[0, h] = dS1[h]
        rows = [_heads(x, hb, head) for x in (w_ref, u_ref, rt_ref, pb_ref, be_ref, lw_ref)]
        Y0 = _heads(y0_ref, hb, head)
        Zs = [z_ref[0, h] for h in range(hb)]
        _, vjp = jax.vjp(lambda s0, *rw: _wkv_state(s0, *rw, Y0, Zs), [st_ref[0, h] for h in range(hb)], *rows)
        grads = vjp((_heads(dy_ref, hb, head), dS1))
        for o_ref, vals in zip((dw_ref, du_ref, drt_ref, dpb_ref, dbe_ref, dlw_ref), grads[1:]):
            _put_heads(o_ref, vals, head)
        for h in range(hb):
            ds_ref[h] = grads[0][h]

    blk = pl.BlockSpec((C, bw), lambda g, c: (nc - 1 - c, g))
    sq = pl.BlockSpec((1, hb, head, head), lambda g, c: (nc - 1 - c, g, 0, 0))
    return pl.pallas_call(
        body, name="wkv_state_bwd", grid=(nh // hb, nc),
        in_specs=[blk] * 7 + [sq, sq, blk], out_specs=[blk] * 6 + [sq],
        out_shape=[jax.ShapeDtypeStruct((T, RW), f32)] * 6 + [jax.ShapeDtypeStruct((nc, nh, head, head), f32)],
        scratch_shapes=[pltpu.VMEM((hb, head, head), f32)],
        compiler_params=_cparams(2),
    )(W, U, Rt, Pb, Bend, lw, Y0, Z, states, dy)


def _wkv_local_bwd(r, lw, k, v, a, b, cots, d_lw_x, dr_x, dk_x, dv_x, head, T, RW):
    C, nh, hb, bw, nc = _wkv_dims(head, T, RW, WKV_HEADS)

    def body(*refs):
        ins, cot_refs, add_refs, outs = refs[:6], refs[6:13], refs[13:17], refs[17:]
        _, vjp = jax.vjp(_wkv_local, *[_heads(x, hb, head) for x in ins])
        cts = [_heads(x, hb, head) for x in cot_refs[:6]] + [[cot_refs[6][0, h] for h in range(hb)]]
        dr, dlw, dk, dv, da, db = vjp(tuple(cts))
        dlw_x, drx, dkx, dvx = [_heads(x, hb, head) for x in add_refs]
        _put_heads(outs[0], _each(jnp.add, dr, drx), head)
        _put_heads(outs[1], _each(jnp.add, dlw, dlw_x), head)
        _put_heads(outs[2], _each(jnp.add, dk, dkx), head)
        _put_heads(outs[3], _each(jnp.add, dv, dvx), head)
        _put_heads(outs[4], da, head)
        _put_heads(outs[5], db, head)

    blk = pl.BlockSpec((C, bw), lambda g, c: (c, g))
    sq = pl.BlockSpec((1, hb, head, head), lambda g, c: (c, g, 0, 0))
    return pl.pallas_call(
        body, name="wkv_local_bwd", grid=(nh // hb, nc),
        in_specs=[blk] * 12 + [sq] + [blk] * 4, out_specs=[blk] * 6,
        out_shape=[jax.ShapeDtypeStruct((T, RW), f32)] * 6,
        compiler_params=_cparams(2),
    )(r, lw, k, v, a, b, *cots, d_lw_x, dr_x, dk_x, dv_x)


def _rows_tile(R, row_bytes, budget, mult=SUBLANES):
    best = None
    t = mult
    while t <= R:
        if R % t == 0 and t * row_bytes <= budget:
            best = t
        t += mult
    return best if best is not None else R


def _sum_slots(name, parts):
    S, R, W = parts.shape
    tr = _rows_tile(R, S * W * 4, 4 << 20, 2 * SUBLANES)

    def body(p_ref, o_ref):
        acc = p_ref[0].astype(f32)
        for d in range(1, S):
            acc = acc + p_ref[d].astype(f32)
        o_ref[...] = acc

    return pl.pallas_call(
        body, name=name, grid=(R // tr,),
        in_specs=[pl.BlockSpec((S, tr, W), lambda i: (0, i, 0))],
        out_specs=pl.BlockSpec((tr, W), lambda i: (i, 0)),
        out_shape=jax.ShapeDtypeStruct((R, W), f32),
        compiler_params=_cparams(1),
    )(parts)


def _add_pair(name, a, b):
    R, W = a.shape
    tr = _rows_tile(R, W * 4, 2 << 20, 2 * SUBLANES)

    def body(a_ref, b_ref, o_ref):
        o_ref[...] = (a_ref[...].astype(f32) + b_ref[...].astype(f32)).astype(o_ref.dtype)

    blk = pl.BlockSpec((tr, W), lambda i: (i, 0))
    return pl.pallas_call(
        body, name=name, grid=(R // tr,), in_specs=[blk, blk], out_specs=blk,
        out_shape=jax.ShapeDtypeStruct((R, W), a.dtype), compiler_params=_cparams(1),
    )(a, b)


def _adamw(name, w, g, m, v):
    R, W = w.shape
    tr = _rows_tile(R, W * 4, 1 << 20)

    def body(w_ref, g_ref, m_ref, v_ref, d_ref, nm_ref, nv_ref):
        g_v = g_ref[...]
        nm = ADAM_B1 * m_ref[...] + (1.0 - ADAM_B1) * g_v
        nv = ADAM_B2 * v_ref[...] + (1.0 - ADAM_B2) * (g_v * g_v)
        m_hat = nm / (1.0 - ADAM_B1 ** ADAM_STEP)
        v_hat = nv / (1.0 - ADAM_B2 ** ADAM_STEP)
        d_ref[...] = -ADAM_LR * (m_hat / (jnp.sqrt(v_hat) + ADAM_EPS) + ADAM_WD * w_ref[...])
        nm_ref[...] = nm
        nv_ref[...] = nv

    blk = pl.BlockSpec((tr, W), lambda i: (i, 0))
    return pl.pallas_call(
        body, name=name, grid=(R // tr,),
        in_specs=[blk] * 4, out_specs=[blk] * 3,
        out_shape=[jax.ShapeDtypeStruct((R, W), f32)] * 3,
        compiler_params=_cparams(1),
    )(w, g, m, v)


ANY = pl.BlockSpec(memory_space=pl.ANY)


def _place():
    return lax.axis_index("x"), lax.axis_index("y"), lax.axis_index("c")


def _gather_chips(arrays):
    n = len(arrays)
    parts = []
    for a, arr in enumerate(arrays):
        k = SHARE_CHUNKS // 2 if arr.shape[1] % (SHARE_CHUNKS * SUBLANES) == 0 else 1
        step = arr.shape[1] // k
        parts += [(a, h, q * step, step) for h in range(2) for q in range(k)]

    def body(*refs):
        ins, outs = refs[:n], refs[n:2 * n]
        send_sems, recv_sems, local_sems = refs[2 * n:]
        x, y, c = _place()
        mine = 2 * x + y
        sib = (x, y, 1 - c)
        chips = [(1 - x, y), (x, 1 - y), (1 - x, 1 - y)]
        local = [pltpu.make_async_copy(ins[a].at[h, pl.ds(r0, nr)], outs[a].at[mine, h, pl.ds(r0, nr)], local_sems.at[p])
                 for p, (a, h, r0, nr) in enumerate(parts)]
        for cp in local:
            cp.start()

        def over_ici(a, j, slot):
            px, py = chips[j]
            return pltpu.make_async_remote_copy(
                src_ref=ins[a].at[c], dst_ref=outs[a].at[slot, c], send_sem=send_sems.at[3 * a + j],
                recv_sem=recv_sems.at[3 * a + j], device_id=(px, py, c), device_id_type=MESH)

        def over_d2d(a, j, half):
            px, py = chips[j]
            slot = 2 * px + py
            return pltpu.make_async_remote_copy(
                src_ref=outs[a].at[slot, half], dst_ref=outs[a].at[slot, half], send_sem=send_sems.at[3 * (n + a) + j],
                recv_sem=recv_sems.at[3 * (n + a) + j], device_id=sib, device_id_type=MESH)

        sends = [over_ici(a, j, mine) for a in range(n) for j in range(3)]
        for cp in sends:
            cp.start()
        passed = []
        for a in range(n):
            for j, (px, py) in enumerate(chips):
                over_ici(a, j, 2 * px + py).wait_recv()
                cp = over_d2d(a, j, c)
                cp.start()
                passed.append(cp)
        for a in range(n):
            for j in range(3):
                over_d2d(a, j, 1 - c).wait_recv()
        for cp in sends + passed:
            cp.wait_send()
        for cp in local:
            cp.wait()

    return pl.pallas_call(
        body, name="gather_weights",
        in_specs=[ANY] * n, out_specs=[ANY] * n,
        out_shape=[jax.ShapeDtypeStruct((4,) + a.shape, a.dtype) for a in arrays],
        scratch_shapes=[pltpu.SemaphoreType.DMA((6 * n,)), pltpu.SemaphoreType.DMA((6 * n,)),
                        pltpu.SemaphoreType.DMA((len(parts),))],
    )(*arrays)


def _exchange_chips(pieces, whole):
    n, m = len(pieces), len(whole)
    parts = []
    for a, arr in enumerate(pieces):
        k = SHARE_CHUNKS // 2 if arr.shape[1] % (SHARE_CHUNKS * SUBLANES) == 0 else 1
        step = arr.shape[1] // k
        parts += [(a, q * step, step) for q in range(k)]

    def body(*refs):
        ins, outs = refs[:n + m], refs[n + m:2 * (n + m)]
        send_sems, recv_sems, local_sems = refs[2 * (n + m):]
        x, y, c = _place()
        chip, dev = 2 * x + y, 4 * x + 2 * y + c
        chips = [(1 - x, y), (x, 1 - y), (1 - x, 1 - y)]
        peers = [(x ^ (k >> 2), y ^ ((k >> 1) & 1), c ^ (k & 1)) for k in range(1, 8)]
        local = [pltpu.make_async_copy(ins[a].at[chip, pl.ds(r0, nr)], outs[a].at[chip, pl.ds(r0, nr)], local_sems.at[p])
                 for p, (a, r0, nr) in enumerate(parts)]
        local += [pltpu.make_async_copy(ins[n + b], outs[n + b].at[dev], local_sems.at[len(parts) + b]) for b in range(m)]
        for cp in local:
            cp.start()

        def piece(a, j, slot_from):
            px, py = chips[j]
            return pltpu.make_async_remote_copy(
                src_ref=ins[a].at[2 * px + py], dst_ref=outs[a].at[slot_from], send_sem=send_sems.at[3 * a + j],
                recv_sem=recv_sems.at[3 * a + j], device_id=(px, py, c), device_id_type=MESH)

        def everyone(b, j, slot_from):
            px, py, pc = peers[j]
            return pltpu.make_async_remote_copy(
                src_ref=ins[n + b], dst_ref=outs[n + b].at[slot_from], send_sem=send_sems.at[3 * n + 7 * b + j],
                recv_sem=recv_sems.at[3 * n + 7 * b + j], device_id=(px, py, pc), device_id_type=MESH)

        sends = [everyone(b, j, dev) for b in range(m) for j in range(7)]
        sends += [piece(a, j, chip) for a in range(n) for j in range(3)]
        for cp in sends:
            cp.start()
        for b in range(m):
            for j, (px, py, pc) in enumerate(peers):
                everyone(b, j, 4 * px + 2 * py + pc).wait_recv()
        for a in range(n):
            for j, (px, py) in enumerate(chips):
                piece(a, j, 2 * px + py).wait_recv()
        for cp in sends:
            cp.wait_send()
        for cp in local:
            cp.wait()

    shapes = [jax.ShapeDtypeStruct(a.shape, a.dtype) for a in pieces]
    shapes += [jax.ShapeDtypeStruct((8,) + a.shape, a.dtype) for a in whole]
    nsem = 3 * n + 7 * m
    return pl.pallas_call(
        body, name="exchange_grads",
        in_specs=[ANY] * (n + m), out_specs=[ANY] * (n + m), out_shape=shapes,
        scratch_shapes=[pltpu.SemaphoreType.DMA((nsem,)), pltpu.SemaphoreType.DMA((nsem,)),
                        pltpu.SemaphoreType.DMA((len(parts) + m,))],
    )(*pieces, *whole)


def _share_sibling(name, arrays):
    n = len(arrays)
    parts = []
    for a, arr in enumerate(arrays):
        k = SHARE_CHUNKS if arr.shape[0] % (SHARE_CHUNKS * SUBLANES) == 0 else 1
        k = arr.shape[0] if arr.ndim == 3 else k
        step = arr.shape[0] // k
        parts += [(a, q * step, step) for q in range(k)]
    npart = len(parts)

    def body(*refs):
        ins, outs = refs[:n], refs[n:2 * n]
        send_sems, recv_sems = refs[2 * n:]
        x, y, c = _place()

        def copy(p):
            a, r0, nr = parts[p]
            return pltpu.make_async_remote_copy(
                src_ref=ins[a].at[pl.ds(r0, nr)], dst_ref=outs[a].at[pl.ds(r0, nr)], send_sem=send_sems.at[p],
                recv_sem=recv_sems.at[p], device_id=(x, y, 1 - c), device_id_type=MESH)

        copies = [copy(p) for p in range(npart)]
        for cp in copies:
            cp.start()
        for cp in copies:
            cp.wait_recv()
        for cp in copies:
            cp.wait_send()

    return pl.pallas_call(
        body, name=name,
        in_specs=[ANY] * n, out_specs=[ANY] * n,
        out_shape=[jax.ShapeDtypeStruct(a.shape, a.dtype) for a in arrays],
        scratch_shapes=[pltpu.SemaphoreType.DMA((npart,)), pltpu.SemaphoreType.DMA((npart,))],
    )(*arrays)


def kernel(x, norm_pre_g, w_in, mu_shift, w0, w_lora_up, a0, a_lora_up, k_k, k_a, r_k, lnx_g, lnx_b, conv_w, conv_b, cln_g, cln_b, w_pw2, b_pw2, w_out, norm_post_g, loss_target, m_norm_pre_g, m_w_in, m_mu_shift, m_w0, m_w_lora_up, m_a0, m_a_lora_up, m_k_k, m_k_a, m_r_k, m_lnx_g, m_lnx_b, m_conv_w, m_conv_b, m_cln_g, m_cln_b, m_w_pw2, m_b_pw2, m_w_out, m_norm_post_g, v_norm_pre_g, v_w_in, v_mu_shift, v_w0, v_w_lora_up, v_a0, v_a_lora_up, v_k_k, v_k_a, v_r_k, v_lnx_g, v_lnx_b, v_conv_w, v_conv_b, v_cln_g, v_cln_b, v_w_pw2, v_b_pw2, v_w_out, v_norm_post_g):
    _, T, D = x.shape
    RW = w0.shape[0]
    CW = conv_b.shape[0]
    head = r_k.shape[1]
    lora = w_lora_up.shape[0]
    ktaps = conv_w.shape[0]
    assert RW == CW and 2 * lora <= LORA_PAD and ktaps - 1 <= CONV_HALO
    n_in = 3 * RW + 2 * lora + RW + 3 * CW
    shard = n_in // 4
    PW = 7 * RW + LORA_PAD
    off_l = 7 * RW
    tm = min(256, T // 2)
    tm_wide = min(128, T // 2)
    row = lambda vec: vec.reshape(1, -1)
    x2, tgt2 = x[0], loss_target[0]

    halves = lambda a: a.reshape(2, a.shape[0] // 2, a.shape[1])
    conv_w_p = jnp.concatenate([conv_w, jnp.zeros((CONV_HALO - ktaps, CW // 4), f32)], axis=0)
    g_wup, g_aup, g_cw, g_pw2, g_wout, g_win = _gather_chips(
        [halves(a) for a in (w_lora_up, a_lora_up, conv_w_p, w_pw2.astype(bf16), w_out.astype(bf16), w_in.astype(bf16))])
    cat_cols = lambda g: jnp.concatenate([g[s].reshape(-1, g.shape[-1]) for s in range(4)], axis=1)
    win_full = cat_cols(g_win)
    lo = 3 * RW
    wp = jnp.concatenate([win_full[:, :lo], win_full[:, lo + 2 * lora:], win_full[:, lo:lo + 2 * lora],
                          jnp.zeros((D, LORA_PAD - 2 * lora), bf16)], axis=1)
    wup_full, aup_full, cw_p = cat_cols(g_wup), cat_cols(g_aup), cat_cols(g_cw)
    zl = lambda n: jnp.zeros((n, RW), f32)
    wup_p = jnp.concatenate([wup_full, zl(LORA_PAD - lora)], axis=0)
    aup_p = jnp.concatenate([zl(lora), aup_full, zl(LORA_PAD - 2 * lora)], axis=0)
    pw2_full = g_pw2.reshape(CW, CW)
    wout_full = g_wout.reshape(RW + CW, D)
    mu_r, mu_k, mu_v = (row(mu_shift[s * RW:(s + 1) * RW]) for s in range(3))
    mu_l = row(jnp.concatenate([mu_shift[3 * RW:], jnp.zeros((LORA_PAD - 2 * lora,), f32)]))

    npg = row(norm_pre_g)
    (h,) = _row_fwd("rms_pre", _fn_rms_pre, [(npg, False)], [(x2, 0, D, False)], [(D, bf16)], T, tm)
    proj = _matmul("proj", h, wp, "nn", f32)
    xs_r = _shift_fwd("shift_r", proj, 0, RW, mu_r, T, tm)
    xs_k = _shift_fwd("shift_k", proj, RW, RW, mu_k, T, tm)
    xs_v = _shift_fwd("shift_v", proj, 2 * RW, RW, mu_v, T, tm)
    xs_l = _shift_fwd("shift_l", proj, off_l, LORA_PAD, mu_l, T, tm)
    lora_params = [(row(w0), False), (wup_p, False), (row(a0), False), (aup_p, False)]
    qw, qa = _row_fwd("lora_up", _fn_lora, lora_params, [(xs_l, 0, LORA_PAD, False)], [(RW, f32), (RW, f32)], T, tm)
    ncol = RW // _tile(RW, 512)
    fn_pre = functools.partial(_fn_rwkv_pre, head)
    pre_params = [(row(k_k), True), (row(k_a), True)]
    pre_rows = [(xs_k, 0, RW, True), (qw, 0, RW, True), (qa, 0, RW, True)]
    lw, k_h, a_rec, b_rec = _row_fwd("rwkv_pre", fn_pre, pre_params, pre_rows, [(RW, f32)] * 4, T, tm, ncol)
    wkv_in = (xs_r, lw, k_h, xs_v, a_rec, b_rec)
    c_w, c_u, c_rt, c_pb, c_y0, c_bend, c_z = _wkv_local_fwd(*wkv_in, head, T, RW)
    wkv_loc = (c_w, c_u, c_rt, c_pb, c_bend, lw, c_y0, c_z)
    y_wkv, states = _wkv_state_fwd(*wkv_loc, head, T, RW)
    fn_post = functools.partial(_fn_rwkv_post, head)
    post_params = [(row(lnx_g), True), (row(lnx_b), True), (r_k.reshape(1, RW), True)]
    post_rows = [(y_wkv, 0, RW, True), (xs_r, 0, RW, True), (k_h, 0, RW, True), (xs_v, 0, RW, True),
                 (proj, 3 * RW, RW, True)]
    (y_rwkv,) = _row_fwd("rwkv_post", fn_post, post_params, post_rows, [(RW, bf16)], T, tm, ncol)

    c_pre = _conv_fwd(proj, 4 * RW, 5 * RW, CW, cw_p, row(conv_b), ktaps, T, tm)
    ln_params = [(row(cln_g), False), (row(cln_b), False)]
    (c_act,) = _row_fwd("conv_ln", _fn_conv_ln, ln_params, [(c_pre, 0, CW, False)], [(CW, bf16)], T, tm)
    c2 = _matmul("pw2", c_act, pw2_full, "nn", f32)
    cpost_params = [(row(b_pw2), True)]
    cpost_rows = [(c2, 0, CW, True), (proj, 6 * RW, CW, True)]
    (y_conv,) = _row_fwd("conv_post", _fn_conv_post, cpost_params, cpost_rows, [(CW, bf16)], T, tm, ncol)

    mix = jnp.concatenate([y_rwkv, y_conv], axis=1)
    out = _matmul("out_proj", mix, wout_full, "nn", f32)
    d_out, gx_res, loss_part, g_npost = _post(out, x2, tgt2, row(norm_post_g), T, D, tm_wide)

    g_wout_full = _matmul("d_w_out", mix, d_out, "tn", bf16)
    d_mix = _matmul("d_mix", d_out, wout_full, "nt", f32)

    d_c2, d_gconv, g_bpw2 = _row_bwd("conv_post_bwd", _fn_conv_post, cpost_params, cpost_rows,
                                      [(d_mix, RW, CW, True)], [bf16, bf16], T, tm, ncol)
    g_pw2_full = _matmul("d_w_pw2", c_act, d_c2, "tn", bf16)
    d_cact = _matmul("d_c_act", d_c2, pw2_full, "nt", f32)
    d_cpre, g_clng, g_clnb = _row_bwd("conv_ln_bwd", _fn_conv_ln, ln_params, [(c_pre, 0, CW, False)],
                                      [(d_cact, 0, CW, False)], [f32], T, tm)
    d_gluv, d_glug, g_cw_p, g_cb = _conv_bwd(proj, 4 * RW, 5 * RW, CW, cw_p, d_cpre, ktaps, T, tm)

    d_y, dr_x, dk_x, dv_x, d_grwkv, g_lnxg, g_lnxb, g_rk = _row_bwd(
        "rwkv_post_bwd", fn_post, post_params, post_rows, [(d_mix, 0, RW, True)], [f32, f32, f32, f32, bf16], T, tm, ncol)
    d_cw, d_cu, d_crt, d_cpb, d_cbend, d_lw_dec, d_cz = _wkv_state_bwd(*wkv_loc, states, d_y, head, T, RW)
    d_xr, d_lw, d_kh, d_xv, d_a, d_b = _wkv_local_bwd(
        *wkv_in, (d_cw, d_cu, d_crt, d_cpb, d_y, d_cbend, d_cz), d_lw_dec, dr_x, dk_x, dv_x, head, T, RW)
    pre_cots = [(d_lw, 0, RW, True), (d_kh, 0, RW, True), (d_a, 0, RW, True), (d_b, 0, RW, True)]
    d_xk, d_qw, d_qa, g_kk, g_ka = _row_bwd("rwkv_pre_bwd", fn_pre, pre_params, pre_rows, pre_cots, [f32, f32, f32],
                                            T, tm, ncol)
    d_xl, g_w0, g_wup_p, g_a0, g_aup_p = _row_bwd("lora_up_bwd", _fn_lora, lora_params, [(xs_l, 0, LORA_PAD, False)],
                                                  [(d_qw, 0, RW, False), (d_qa, 0, RW, False)], [f32], T, tm)
    dp_r, g_mur = _shift_bwd("shift_r_bwd", proj, 0, RW, mu_r, d_xr, T, tm)
    dp_k, g_muk = _shift_bwd("shift_k_bwd", proj, RW, RW, mu_k, d_xk, T, tm)
    dp_v, g_muv = _shift_bwd("shift_v_bwd", proj, 2 * RW, RW, mu_v, d_xv, T, tm)
    dp_l, g_mul = _shift_bwd("shift_l_bwd", proj, off_l, LORA_PAD, mu_l, d_xl, T, tm)
    d_proj = jnp.concatenate([dp_r, dp_k, dp_v, d_grwkv, d_gluv, d_glug, d_gconv, dp_l], axis=1)

    g_wp = _matmul("d_w_in", h, d_proj, "tn", bf16)
    d_h = _matmul("d_h", d_proj, wp, "nt", bf16, tk_t=2560)
    grad_x2, g_npre = _rms_pre_bwd(x2, npg, d_h, gx_res, T, D, tm_wide)

    g_win_full = jnp.concatenate([g_wp[:, :lo], g_wp[:, off_l:off_l + 2 * lora], g_wp[:, lo:off_l]], axis=1)
    core = lax.axis_index("c")

    def row_halves(a):
        a = a.reshape(4, 2, a.shape[1] // 2, a.shape[2])
        return a[:, 0], a[:, 1]

    def keep_give(a):
        h0, h1 = row_halves(a)
        return jnp.where(core == 0, h0, h1), jnp.where(core == 0, h1, h0)

    win_keep, win_give = keep_give(jnp.stack([g_win_full[:, s * shard:(s + 1) * shard] for s in range(4)]))
    wout_keep, wout_give = keep_give(g_wout_full.reshape(4, (RW + CW) // 4, D))
    pw2_keep, pw2_give = keep_give(g_pw2_full.reshape(4, CW // 4, CW))
    g_mu = jnp.concatenate([g_mur[0], g_muk[0], g_muv[0], g_mul[0, :2 * lora]])
    pad_rows = lambda a, n: jnp.concatenate([a, jnp.zeros((n - a.shape[0], a.shape[1]), f32)], axis=0)
    n_mu = -(-mu_shift.shape[0] // RW)
    small_vecs = [g_npre.reshape(D // RW, RW), pad_rows(jnp.pad(g_mu, (0, n_mu * RW - g_mu.shape[0])).reshape(n_mu, RW), n_mu),
                  g_w0, g_a0, g_kk, g_ka, g_rk, g_lnxg, g_lnxb, g_cb, g_clng, g_clnb, g_bpw2,
                  g_npost.reshape(D // RW, RW), g_wup_p[:lora], g_aup_p[lora:2 * lora], g_cw_p[:ktaps]]
    n_small = sum(a.shape[0] for a in small_vecs)
    n_small_pad = -(-n_small // (2 * SUBLANES)) * (2 * SUBLANES)
    small = pad_rows(jnp.concatenate(small_vecs, axis=0), n_small_pad)

    win_got, wout_got, pw2_got = _share_sibling("pair_exchange", [win_give, wout_give, pw2_give])
    flat = lambda a: a.reshape(-1, a.shape[-1])
    chip_sum = lambda nm, keep, got: _add_pair(nm, flat(keep), flat(got)).reshape(keep.shape)
    q_win = chip_sum("chip_sum_w_in", win_keep, win_got)
    q_wout = chip_sum("chip_sum_w_out", wout_keep, wout_got)
    q_pw2 = chip_sum("chip_sum_w_pw2", pw2_keep, pw2_got)
    r_win, r_wout, r_pw2, r_small = _exchange_chips([q_win, q_wout, q_pw2], [small])
    s_win = _sum_slots("sum_w_in", r_win)
    s_wout = _sum_slots("sum_w_out", r_wout)
    s_pw2 = _sum_slots("sum_w_pw2", r_pw2)
    s_small = _sum_slots("sum_small", r_small)
    o_win, o_wout, o_pw2 = _share_sibling("share_sibling", [s_win, s_wout, s_pw2])
    south = lax.axis_index("c") == 0

    def both_halves(own, other):
        return jnp.concatenate([jnp.where(south, own, other), jnp.where(south, other, own)], axis=0)

    grad_w_in = both_halves(s_win, o_win)
    grad_w_out = both_halves(s_wout, o_wout)
    grad_w_pw2 = both_halves(s_pw2, o_pw2)

    chip = 2 * lax.axis_index("x") + lax.axis_index("y")
    pos = [0]

    def take(nrows):
        a = s_small[pos[0]:pos[0] + nrows]
        pos[0] += nrows
        return a

    my_cols = lambda a, w: lax.dynamic_slice_in_dim(a, chip * w, w, axis=1)
    grads = {}
    grads["norm_pre_g"] = take(D // RW).reshape(D)
    grads["mu_shift"] = take(n_mu).reshape(-1)[:mu_shift.shape[0]]
    for nm in ["w0", "a0", "k_k", "k_a"]:
        grads[nm] = take(1).reshape(RW)
    grads["r_k"] = take(1).reshape(r_k.shape)
    for nm in ["lnx_g", "lnx_b", "conv_b", "cln_g", "cln_b", "b_pw2"]:
        grads[nm] = take(1).reshape(RW)
    grads["norm_post_g"] = take(D // RW).reshape(D)
    grads["w_lora_up"] = my_cols(take(lora), RW // 4)
    grads["a_lora_up"] = my_cols(take(lora), RW // 4)
    grads["conv_w"] = my_cols(take(ktaps), CW // 4)
    grads["w_in"], grads["w_out"], grads["w_pw2"] = grad_w_in, grad_w_out, grad_w_pw2

    weights = dict(norm_pre_g=norm_pre_g, w_in=w_in, mu_shift=mu_shift, w0=w0, w_lora_up=w_lora_up, a0=a0,
                   a_lora_up=a_lora_up, k_k=k_k, k_a=k_a, r_k=r_k, lnx_g=lnx_g, lnx_b=lnx_b, conv_w=conv_w,
                   conv_b=conv_b, cln_g=cln_g, cln_b=cln_b, w_pw2=w_pw2, b_pw2=b_pw2, w_out=w_out,
                   norm_post_g=norm_post_g)
    ms = dict(norm_pre_g=m_norm_pre_g, w_in=m_w_in, mu_shift=m_mu_shift, w0=m_w0, w_lora_up=m_w_lora_up, a0=m_a0,
              a_lora_up=m_a_lora_up, k_k=m_k_k, k_a=m_k_a, r_k=m_r_k, lnx_g=m_lnx_g, lnx_b=m_lnx_b, conv_w=m_conv_w,
              conv_b=m_conv_b, cln_g=m_cln_g, cln_b=m_cln_b, w_pw2=m_w_pw2, b_pw2=m_b_pw2, w_out=m_w_out,
              norm_post_g=m_norm_post_g)
    vs = dict(norm_pre_g=v_norm_pre_g, w_in=v_w_in, mu_shift=v_mu_shift, w0=v_w0, w_lora_up=v_w_lora_up, a0=v_a0,
              a_lora_up=v_a_lora_up, k_k=v_k_k, k_a=v_k_a, r_k=v_r_k, lnx_g=v_lnx_g, lnx_b=v_lnx_b, conv_w=v_conv_w,
              conv_b=v_conv_b, cln_g=v_cln_g, cln_b=v_cln_b, w_pw2=v_w_pw2, b_pw2=v_b_pw2, w_out=v_w_out,
              norm_post_g=v_norm_post_g)
    names = list(weights)
    big = ["w_in", "w_out", "w_pw2"]
    deltas, new_m, new_v = {}, {}, {}
    for nm in big:
        deltas[nm], new_m[nm], new_v[nm] = _adamw("adamw_" + nm, weights[nm], grads[nm], ms[nm], vs[nm])
    rest = [nm for nm in names if nm not in big]
    sizes = [weights[nm].size for nm in rest]
    total = sum(sizes)
    width = 4 * LANES
    rows_p = -(-total // (width * SUBLANES)) * SUBLANES

    def pack(d):
        flat = jnp.concatenate([d[nm].reshape(-1) for nm in rest])
        return jnp.pad(flat, (0, rows_p * width - total)).reshape(rows_p, width)

    p_d, p_m, p_v = _adamw("adamw_small", pack(weights), pack(grads), pack(ms), pack(vs))
    o = 0
    for nm, sz in zip(rest, sizes):
        shp = weights[nm].shape
        deltas[nm] = p_d.reshape(-1)[o:o + sz].reshape(shp)
        new_m[nm] = p_m.reshape(-1)[o:o + sz].reshape(shp)
        new_v[nm] = p_v.reshape(-1)[o:o + sz].reshape(shp)
        o += sz

    loss = lax.psum(loss_part[0, 0], ("x", "y", "c"))
    grad_x = grad_x2[None]
    return (loss, grad_x, *[grads[nm] for nm in names], *[deltas[nm] for nm in names],
            *[new_m[nm] for nm in names], *[new_v[nm] for nm in names])
```

```python
import functools

import jax
import jax.numpy as jnp
from jax import lax
from jax.experimental import pallas as pl
from jax.experimental.pallas import tpu as pltpu

f32 = jnp.float32
bf16 = jnp.bfloat16
MESH = pl.DeviceIdType.MESH

NORM_EPS = 1e-6
LN_EPS = 1e-5
ADAM_LR, ADAM_B1, ADAM_B2, ADAM_EPS, ADAM_WD, ADAM_STEP = 0.001, 0.9, 0.999, 1e-08, 0.01, 10

LANES = 128
SUBLANES = 8
LORA_PAD = 256
CONV_HALO = 32
WKV_CHUNK = 64
WKV_HEADS = 16
WKV_STATE_HEADS = 32
WKV_PREC = lax.Precision.HIGH
SHARE_CHUNKS = 8
VMEM_LIMIT = 56 * 1024 * 1024


def _cparams(n_axes):
    return pltpu.CompilerParams(dimension_semantics=("arbitrary",) * n_axes, vmem_limit_bytes=VMEM_LIMIT)


def _tile(dim, target):
    best = None
    t = LANES
    while t <= min(dim, target):
        if dim % t == 0:
            best = t
        t += LANES
    return best if best is not None else dim


def _mm(a, b, prec=None):
    return lax.dot_general(a, b, (((1,), (0,)), ((), ())), precision=prec, preferred_element_type=f32)


def _mm_nt(a, b, prec=None):
    return lax.dot_general(a, b, (((1,), (1,)), ((), ())), precision=prec, preferred_element_type=f32)


def _mm_tn(a, b, prec=None):
    return lax.dot_general(a, b, (((0,), (0,)), ((), ())), precision=prec, preferred_element_type=f32)


@jax.custom_vjp
def _bmm(a, b):
    return _mm(a.astype(bf16), b.astype(bf16))


def _bmm_fwd(a, b):
    return _bmm(a, b), (a, b)


def _bmm_bwd(res, dc):
    a, b = res
    dcb = dc.astype(bf16)
    return _mm_nt(dcb, b.astype(bf16)), _mm_tn(a.astype(bf16), dcb)


_bmm.defvjp(_bmm_fwd, _bmm_bwd)


def _matmul(name, a, b, mode, out_dtype, tm_t=1024, tn_t=768, tk_t=1024):
    if mode == "nn":
        (M, K), (_, N) = a.shape, b.shape
    elif mode == "nt":
        (M, K), (N, _) = a.shape, b.shape
    else:
        (K, M), (_, N) = a.shape, b.shape
    tm, tn, tk = _tile(M, tm_t), _tile(N, tn_t), _tile(K, tk_t)
    nk = K // tk
    dot = {"nn": _mm, "nt": _mm_nt, "tn": _mm_tn}[mode]

    def body(a_ref, b_ref, o_ref, acc_ref):
        k = pl.program_id(2)

        @pl.when(k == 0)
        def _():
            acc_ref[...] = jnp.zeros_like(acc_ref)

        acc_ref[...] += dot(a_ref[...], b_ref[...])

        @pl.when(k == nk - 1)
        def _():
            o_ref[...] = acc_ref[...].astype(o_ref.dtype)

    a_spec = {"nn": pl.BlockSpec((tm, tk), lambda i, j, k: (i, k)),
              "nt": pl.BlockSpec((tm, tk), lambda i, j, k: (i, k)),
              "tn": pl.BlockSpec((tk, tm), lambda i, j, k: (k, i))}[mode]
    b_spec = {"nn": pl.BlockSpec((tk, tn), lambda i, j, k: (k, j)),
              "nt": pl.BlockSpec((tn, tk), lambda i, j, k: (j, k)),
              "tn": pl.BlockSpec((tk, tn), lambda i, j, k: (k, j))}[mode]
    return pl.pallas_call(
        body, name=name, grid=(M // tm, N // tn, nk),
        in_specs=[a_spec, b_spec],
        out_specs=pl.BlockSpec((tm, tn), lambda i, j, k: (i, j)),
        out_shape=jax.ShapeDtypeStruct((M, N), out_dtype),
        scratch_shapes=[pltpu.VMEM((tm, tn), f32)],
        compiler_params=_cparams(3),
    )(a, b)


def _row_spec(op, tm, ncol):
    arr, off, width, tiled = op
    if tiled:
        cw = width // ncol
        return pl.BlockSpec((tm, cw), lambda j, i: (i, off // cw + j))
    return pl.BlockSpec((tm, width), lambda j, i: (i, off // width))


def _param_spec(p, ncol):
    arr, tiled = p
    rows, width = arr.shape
    if tiled:
        return pl.BlockSpec((rows, width // ncol), lambda j, i: (0, j))
    return pl.BlockSpec((rows, width), lambda j, i: (0, 0))


def _row_fwd(name, fn, params, rows, outs, T, tm, ncol=1):
    npar, nrow = len(params), len(rows)

    def body(*refs):
        pv = [r[...] for r in refs[:npar]]
        rv = [r[...].astype(f32) for r in refs[npar:npar + nrow]]
        res = fn(*pv, *rv)
        for o_ref, val in zip(refs[npar + nrow:], res):
            o_ref[...] = val.astype(o_ref.dtype)

    return pl.pallas_call(
        body, name=name, grid=(ncol, T // tm),
        in_specs=[_param_spec(p, ncol) for p in params] + [_row_spec(r, tm, ncol) for r in rows],
        out_specs=[pl.BlockSpec((tm, w // ncol), lambda j, i: (i, j)) for w, _ in outs],
        out_shape=[jax.ShapeDtypeStruct((T, w), dt) for w, dt in outs],
        compiler_params=_cparams(2),
    )(*[p[0] for p in params], *[r[0] for r in rows])


def _row_bwd(name, fn, params, rows, cots, row_grads, T, tm, ncol=1):
    npar, nrow, ncot = len(params), len(rows), len(cots)
    want = [k for k, dt in enumerate(row_grads) if dt is not None]

    def body(*refs):
        pv = [r[...] for r in refs[:npar]]
        rv = [r[...].astype(f32) for r in refs[npar:npar + nrow]]
        cv = tuple(r[...].astype(f32) for r in refs[npar + nrow:npar + nrow + ncot])
        out_refs = refs[npar + nrow + ncot:]
        _, vjp = jax.vjp(fn, *pv, *rv)
        grads = vjp(cv)
        for o_ref, k in zip(out_refs[:len(want)], want):
            o_ref[...] = grads[npar + k].astype(o_ref.dtype)
        j, i = pl.program_id(0), pl.program_id(1)
        for o_ref, p, g in zip(out_refs[len(want):], params, grads[:npar]):
            first = (i == 0) if p[1] else jnp.logical_and(i == 0, j == 0)

            @pl.when(first)
            def _():
                o_ref[...] = jnp.zeros_like(o_ref)

            o_ref[...] += g

    def grad_spec(op):
        arr, off, width, tiled = op
        if tiled:
            return pl.BlockSpec((tm, width // ncol), lambda j, i: (i, j)), (T, width)
        return pl.BlockSpec((tm, width), lambda j, i: (i, j)), (T, width * ncol)

    gspecs = [grad_spec(rows[k]) for k in want]
    return pl.pallas_call(
        body, name=name, grid=(ncol, T // tm),
        in_specs=[_param_spec(p, ncol) for p in params] + [_row_spec(r, tm, ncol) for r in rows]
        + [_row_spec(c, tm, ncol) for c in cots],
        out_specs=[s for s, _ in gspecs] + [_param_spec(p, ncol) for p in params],
        out_shape=[jax.ShapeDtypeStruct(shp, row_grads[k]) for (_, shp), k in zip(gspecs, want)]
        + [jax.ShapeDtypeStruct(p[0].shape, f32) for p in params],
        compiler_params=_cparams(2),
    )(*[p[0] for p in params], *[r[0] for r in rows], *[c[0] for c in cots])


def _seg_sum(x, head):
    li = lax.broadcasted_iota(jnp.int32, (LANES, LANES), 0) // head
    lj = lax.broadcasted_iota(jnp.int32, (LANES, LANES), 1) // head
    q = (li == lj).astype(f32)
    parts = [_mm(x[:, s:s + LANES], q, lax.Precision.HIGH) for s in range(0, x.shape[1], LANES)]
    return parts[0] if len(parts) == 1 else jnp.concatenate(parts, axis=1)


def _sigmoid(z):
    return 1.0 / (1.0 + jnp.exp(-z))


def _silu(z):
    return z * _sigmoid(z)


def _rms(g, x):
    return x * lax.rsqrt(jnp.mean(x * x, axis=-1, keepdims=True) + NORM_EPS) * g


def _fn_rms_pre(g, x):
    return (_rms(g, x),)


def _fn_lora(w0, wup, a0, aup, xl):
    qw = w0 + _bmm(jnp.tanh(xl), wup)
    qa = a0 + _bmm(xl, aup)
    return qw, qa


def _fn_rwkv_pre(head, k_k, k_a, xk, qw, qa):
    w_log = -(jnp.maximum(-qw, 0.0) + jnp.log(1.0 + jnp.exp(-jnp.abs(qw)))) - 0.5
    lw = -jnp.exp(w_log)
    a_sig = _sigmoid(qa)
    kk = xk * k_k
    kk = kk / jnp.maximum(jnp.sqrt(_seg_sum(kk * kk, head)), 1e-12)
    k_h = xk * (1.0 + (a_sig - 1.0) * k_a)
    return lw, k_h, -kk, kk * a_sig


def _fn_rwkv_post(head, lnx_g, lnx_b, r_k, y, r, k_h, v, g):
    inv = 1.0 / head
    mu = _seg_sum(y, head) * inv
    d = y - mu
    var = _seg_sum(d * d, head) * inv
    yn = d * lax.rsqrt(var + 1e-5 * head) * lnx_g + lnx_b
    bonus = _seg_sum(r * k_h * r_k, head) * v
    return ((yn + bonus) * _silu(g),)


def _fn_conv_ln(cln_g, cln_b, c):
    mu = jnp.mean(c, axis=-1, keepdims=True)
    d = c - mu
    var = jnp.mean(d * d, axis=-1, keepdims=True)
    return (_silu(d * lax.rsqrt(var + LN_EPS) * cln_g + cln_b),)


def _fn_conv_post(b_pw2, c2, g):
    return ((c2 + b_pw2) * _silu(g),)


def _post(out, x, tgt, g, T, D, tm):
    def body(g_ref, o_ref, x_ref, t_ref, dout_ref, gx_ref, loss_ref, dg_ref):
        i = pl.program_id(0)
        o, vjp = jax.vjp(_rms, g_ref[...], o_ref[...])
        err = x_ref[...] + o - t_ref[...]
        d_y = err * (1.0 / D)
        dg, d_out = vjp(d_y)
        dout_ref[...] = d_out.astype(dout_ref.dtype)
        gx_ref[...] = d_y

        @pl.when(i == 0)
        def _():
            loss_ref[...] = jnp.zeros_like(loss_ref)
            dg_ref[...] = jnp.zeros_like(dg_ref)

        loss_ref[...] += jnp.sum(err * err, keepdims=True) * (0.5 / D)
        dg_ref[...] += dg

    row = pl.BlockSpec((tm, D), lambda i: (i, 0))
    vec = pl.BlockSpec((1, D), lambda i: (0, 0))
    return pl.pallas_call(
        body, name="post_loss", grid=(T // tm,),
        in_specs=[vec, row, row, row],
        out_specs=[row, row, pl.BlockSpec((1, 1), lambda i: (0, 0)), vec],
        out_shape=[jax.ShapeDtypeStruct((T, D), bf16), jax.ShapeDtypeStruct((T, D), f32),
                   jax.ShapeDtypeStruct((1, 1), f32), jax.ShapeDtypeStruct((1, D), f32)],
        compiler_params=_cparams(1),
    )(g, out, x, tgt)


def _rms_pre_bwd(x, g, dh, gx_res, T, D, tm):
    def body(g_ref, x_ref, dh_ref, res_ref, dx_ref, dg_ref):
        i = pl.program_id(0)
        _, vjp = jax.vjp(_rms, g_ref[...], x_ref[...])
        dg, dx = vjp(dh_ref[...].astype(f32))
        dx_ref[...] = dx + res_ref[...]

        @pl.when(i == 0)
        def _():
            dg_ref[...] = jnp.zeros_like(dg_ref)

        dg_ref[...] += dg

    row = pl.BlockSpec((tm, D), lambda i: (i, 0))
    vec = pl.BlockSpec((1, D), lambda i: (0, 0))
    return pl.pallas_call(
        body, name="rms_pre_bwd", grid=(T // tm,),
        in_specs=[vec, row, row, row], out_specs=[row, vec],
        out_shape=[jax.ShapeDtypeStruct((T, D), f32), jax.ShapeDtypeStruct((1, D), f32)],
        compiler_params=_cparams(1),
    )(g, x, dh, gx_res)


def _prev_rows(cur, halo_ref, first):
    top = jnp.where(first, 0.0, halo_ref[SUBLANES - 1:SUBLANES, :])
    rolled = pltpu.roll(cur, 1, 0)
    rid = lax.broadcasted_iota(jnp.int32, cur.shape, 0)
    return jnp.where(rid == 0, top, rolled)


def _shift_fwd(name, proj, off, width, mu, T, tm):
    cw = _tile(width, 512)
    ncol, cb = width // cw, off // cw
    hb = tm // SUBLANES

    def body(mu_ref, cur_ref, halo_ref, o_ref):
        i = pl.program_id(1)
        cur = cur_ref[...]
        prev = _prev_rows(cur, halo_ref, i == 0)
        o_ref[...] = cur + (prev - cur) * mu_ref[...]

    return pl.pallas_call(
        body, name=name, grid=(ncol, T // tm),
        in_specs=[pl.BlockSpec((1, cw), lambda j, i: (0, j)),
                  pl.BlockSpec((tm, cw), lambda j, i: (i, cb + j)),
                  pl.BlockSpec((SUBLANES, cw), lambda j, i: (jnp.maximum(i * hb - 1, 0), cb + j))],
        out_specs=pl.BlockSpec((tm, cw), lambda j, i: (i, j)),
        out_shape=jax.ShapeDtypeStruct((T, width), f32),
        compiler_params=_cparams(2),
    )(mu, proj, proj)


def _shift_bwd(name, proj, off, width, mu, dxs, T, tm):
    cw = _tile(width, 512)
    ncol, cb = width // cw, off // cw
    hb, nt = tm // SUBLANES, T // tm

    def body(mu_ref, cur_ref, halo_ref, d_ref, dnext_ref, o_ref, dmu_ref):
        i = pl.program_id(1)
        cur = cur_ref[...]
        prev = _prev_rows(cur, halo_ref, i == 0)
        d = d_ref[...]
        bottom = jnp.where(i == nt - 1, 0.0, dnext_ref[0:1, :])
        rid = lax.broadcasted_iota(jnp.int32, d.shape, 0)
        d_next = jnp.where(rid == tm - 1, bottom, pltpu.roll(d, tm - 1, 0))
        mu_v = mu_ref[...]
        o_ref[...] = (d * (1.0 - mu_v) + d_next * mu_v).astype(o_ref.dtype)

        @pl.when(i == 0)
        def _():
            dmu_ref[...] = jnp.zeros_like(dmu_ref)

        dmu_ref[...] += jnp.sum(d * (prev - cur), axis=0, keepdims=True)

    return pl.pallas_call(
        body, name=name, grid=(ncol, nt),
        in_specs=[pl.BlockSpec((1, cw), lambda j, i: (0, j)),
                  pl.BlockSpec((tm, cw), lambda j, i: (i, cb + j)),
                  pl.BlockSpec((SUBLANES, cw), lambda j, i: (jnp.maximum(i * hb - 1, 0), cb + j)),
                  pl.BlockSpec((tm, cw), lambda j, i: (i, j)),
                  pl.BlockSpec((SUBLANES, cw), lambda j, i: (jnp.minimum((i + 1) * hb, nt * hb - 1), j))],
        out_specs=[pl.BlockSpec((tm, cw), lambda j, i: (i, j)), pl.BlockSpec((1, cw), lambda j, i: (0, j))],
        out_shape=[jax.ShapeDtypeStruct((T, width), bf16), jax.ShapeDtypeStruct((1, width), f32)],
        compiler_params=_cparams(2),
    )(mu, proj, proj, dxs, dxs)


def _rolled_copies(dst_ref, ext):
    n = ext.shape[0]
    dst_ref[0] = ext
    for r in range(1, SUBLANES):
        dst_ref[r] = pltpu.roll(ext, n - r, 0)


def _window(rolled_ref, start, rows):
    q, r = divmod(start, SUBLANES)
    return rolled_ref[r, pl.ds(SUBLANES * q, rows), :]


def _conv_fwd(proj, off_v, off_g, width, conv_w, conv_b, ktaps, T, tm):
    cw = _tile(width, 512)
    ncol = width // cw
    hb = tm // CONV_HALO
    lead = CONV_HALO - (ktaps - 1)

    def body(w_ref, b_ref, v_ref, g_ref, hv_ref, hg_ref, o_ref, u_ref):
        i = pl.program_id(1)
        halo = hv_ref[...] * _sigmoid(hg_ref[...])
        _rolled_copies(u_ref, jnp.concatenate([jnp.where(i == 0, 0.0, halo), v_ref[...] * _sigmoid(g_ref[...])], axis=0))
        acc = jnp.zeros((tm, cw), f32) + b_ref[...]
        for j in range(ktaps):
            acc = acc + _window(u_ref, lead + j, tm) * w_ref[j:j + 1, :]
        o_ref[...] = acc

    def tile(off):
        return pl.BlockSpec((tm, cw), lambda j, i: (i, off // cw + j))

    def halo(off):
        return pl.BlockSpec((CONV_HALO, cw), lambda j, i: (jnp.maximum(i * hb - 1, 0), off // cw + j))

    return pl.pallas_call(
        body, name="conv_fwd", grid=(ncol, T // tm),
        in_specs=[pl.BlockSpec((CONV_HALO, cw), lambda j, i: (0, j)), pl.BlockSpec((1, cw), lambda j, i: (0, j)),
                  tile(off_v), tile(off_g), halo(off_v), halo(off_g)],
        out_specs=pl.BlockSpec((tm, cw), lambda j, i: (i, j)),
        out_shape=jax.ShapeDtypeStruct((T, width), f32),
        scratch_shapes=[pltpu.VMEM((SUBLANES, CONV_HALO + tm, cw), f32)],
        compiler_params=_cparams(2),
    )(conv_w, conv_b, proj, proj, proj, proj)


def _conv_bwd(proj, off_v, off_g, width, conv_w, dc, ktaps, T, tm):
    cw = _tile(width, 512)
    ncol = width // cw
    hb, nt = tm // CONV_HALO, T // tm
    lead = CONV_HALO - (ktaps - 1)

    def body(w_ref, v_ref, g_ref, hv_ref, hg_ref, dc_ref, dcn_ref, dv_ref, dg_ref, dw_ref, db_ref, u_ref, d_ref):
        i = pl.program_id(1)
        halo = hv_ref[...] * _sigmoid(hg_ref[...])
        sig = _sigmoid(g_ref[...])
        gv = v_ref[...]
        _rolled_copies(u_ref, jnp.concatenate([jnp.where(i == 0, 0.0, halo), gv * sig], axis=0))
        dcur = dc_ref[...]
        _rolled_copies(d_ref, jnp.concatenate([dcur, jnp.where(i == nt - 1, 0.0, dcn_ref[...])], axis=0))

        @pl.when(i == 0)
        def _():
            dw_ref[...] = jnp.zeros_like(dw_ref)
            db_ref[...] = jnp.zeros_like(db_ref)

        du = jnp.zeros((tm, cw), f32)
        for j in range(ktaps):
            du = du + _window(d_ref, ktaps - 1 - j, tm) * w_ref[j:j + 1, :]
            dw_ref[j:j + 1, :] += jnp.sum(_window(u_ref, lead + j, tm) * dcur, axis=0, keepdims=True)
        db_ref[...] += jnp.sum(dcur, axis=0, keepdims=True)
        dv_ref[...] = (du * sig).astype(dv_ref.dtype)
        dg_ref[...] = (du * gv * sig * (1.0 - sig)).astype(dg_ref.dtype)

    def tile(off):
        return pl.BlockSpec((tm, cw), lambda j, i: (i, off // cw + j))

    def halo(off):
        return pl.BlockSpec((CONV_HALO, cw), lambda j, i: (jnp.maximum(i * hb - 1, 0), off // cw + j))

    return pl.pallas_call(
        body, name="conv_bwd", grid=(ncol, nt),
        in_specs=[pl.BlockSpec((CONV_HALO, cw), lambda j, i: (0, j)),
                  tile(off_v), tile(off_g), halo(off_v), halo(off_g),
                  pl.BlockSpec((tm, cw), lambda j, i: (i, j)),
                  pl.BlockSpec((CONV_HALO, cw), lambda j, i: (jnp.minimum((i + 1) * hb, nt * hb - 1), j))],
        out_specs=[pl.BlockSpec((tm, cw), lambda j, i: (i, j)), pl.BlockSpec((tm, cw), lambda j, i: (i, j)),
                   pl.BlockSpec((CONV_HALO, cw), lambda j, i: (0, j)), pl.BlockSpec((1, cw), lambda j, i: (0, j))],
        out_shape=[jax.ShapeDtypeStruct((T, width), bf16), jax.ShapeDtypeStruct((T, width), bf16),
                   jax.ShapeDtypeStruct((CONV_HALO, width), f32), jax.ShapeDtypeStruct((1, width), f32)],
        scratch_shapes=[pltpu.VMEM((SUBLANES, CONV_HALO + tm, cw), f32), pltpu.VMEM((SUBLANES, tm + CONV_HALO, cw), f32)],
        compiler_params=_cparams(2),
    )(conv_w, proj, proj, proj, proj, dc, dc)


def _each(f, *lists):
    return [f(*xs) for xs in zip(*lists)]


def _wkv_local(r, lw, k, v, a, b):
    C = r[0].shape[0]
    P = WKV_PREC
    row = lax.broadcasted_iota(jnp.int32, (C, C), 0)
    col = lax.broadcasted_iota(jnp.int32, (C, C), 1)
    incl, strict = row >= col, row > col
    tri = incl.astype(f32)
    zero = jnp.zeros((C, C), f32)
    G = _each(lambda x: _mm(tri, x, P), lw)
    to_end = _each(lambda x, g: jnp.exp(jnp.sum(x, axis=0, keepdims=True) - g), lw, G)
    e_g = _each(jnp.exp, G)
    e_ng = _each(lambda g: jnp.exp(-g), G)
    At = _each(lambda x, g, w: x * jnp.exp(g - w), a, G, lw)
    Rt = _each(jnp.multiply, r, e_g)
    Kt = _each(jnp.multiply, k, e_ng)
    Bt = _each(jnp.multiply, b, e_ng)
    sc = _each(lambda at, rt, bt, kt: _mm_nt(jnp.concatenate([at, rt], axis=0), jnp.concatenate([bt, kt], axis=0), P),
               At, Rt, Bt, Kt)
    L = _each(lambda s: jnp.where(strict, s[:C, :C], zero), sc)
    M = _each(lambda s: jnp.where(strict, s[:C, C:], zero), sc)
    Pb = _each(lambda s: jnp.where(incl, s[C:, :C], zero), sc)
    Pk = _each(lambda s: jnp.where(incl, s[C:, C:], zero), sc)
    MPk = _each(lambda m, p, x: _bmm(jnp.concatenate([m, p], axis=0), x), M, Pk, v)
    WU = _each(lambda at, mp: jnp.concatenate([at, mp[:C]], axis=1), At, MPk)
    Lp = L
    n = 1
    while n < C:
        n *= 2
        if n < C:
            step = _each(lambda l, x: _bmm(l, jnp.concatenate([x, l], axis=1)), Lp, WU)
            WU = _each(lambda x, s: x + s[:, :x.shape[1]], WU, step)
            Lp = _each(lambda x, s: s[:, x.shape[1]:], WU, step)
        else:
            WU = _each(lambda x, l: x + _bmm(l, x), WU, Lp)
    N = r[0].shape[1]
    W = _each(lambda x: x[:, :N], WU)
    U = _each(lambda x: x[:, N:], WU)
    Y0 = _each(lambda mp: mp[C:], MPk)
    Bend = _each(jnp.multiply, b, to_end)
    Z = _each(lambda x, y, e: _mm_tn(x, y * e, P), v, k, to_end)
    return W, U, Rt, Pb, Y0, Bend, Z


def _wkv_state(S0, W, U, Rt, Pb, Bend, lw, Y0, Z):
    P = WKV_PREC
    C = W[0].shape[0]
    WR = _each(lambda w, rt, s: _mm_nt(jnp.concatenate([w, rt], axis=0), s, P), W, Rt, S0)
    X = _each(lambda wr, u: wr[:C] + u, WR, U)
    y = _each(lambda p, x, wr, c: _mm(p, x, P) + wr[C:] + c, Pb, X, WR, Y0)
    S1 = _each(lambda s, w, x, e, z: s * jnp.exp(jnp.sum(w, axis=0, keepdims=True)) + _mm_tn(x, e, P) + z,
               S0, lw, X, Bend, Z)
    return y, S1


def _wkv_dims(head, T, RW, heads_per_step):
    C = min(WKV_CHUNK, T)
    nh = RW // head
    hb = min(heads_per_step, nh)
    return C, nh, hb, hb * head, T // C


def _heads(ref, hb, head):
    return [ref[:, h * head:(h + 1) * head] for h in range(hb)]


def _put_heads(ref, vals, head):
    for h, val in enumerate(vals):
        ref[:, h * head:(h + 1) * head] = val


def _wkv_local_fwd(r, lw, k, v, a, b, head, T, RW):
    C, nh, hb, bw, nc = _wkv_dims(head, T, RW, WKV_HEADS)

    def body(*refs):
        ins, outs = refs[:6], refs[6:]
        res = _wkv_local(*[_heads(x, hb, head) for x in ins])
        for o_ref, vals in zip(outs[:6], res[:6]):
            _put_heads(o_ref, vals, head)
        for h in range(hb):
            outs[6][0, h] = res[6][h]

    blk = pl.BlockSpec((C, bw), lambda g, c: (c, g))
    sq = pl.BlockSpec((1, hb, head, head), lambda g, c: (c, g, 0, 0))
    return pl.pallas_call(
        body, name="wkv_local", grid=(nh // hb, nc),
        in_specs=[blk] * 6, out_specs=[blk] * 6 + [sq],
        out_shape=[jax.ShapeDtypeStruct((T, RW), f32)] * 6 + [jax.ShapeDtypeStruct((nc, nh, head, head), f32)],
        compiler_params=_cparams(2),
    )(r, lw, k, v, a, b)


def _wkv_state_fwd(W, U, Rt, Pb, Bend, lw, Y0, Z, head, T, RW):
    C, nh, hb, bw, nc = _wkv_dims(head, T, RW, WKV_STATE_HEADS)

    def body(w_ref, u_ref, rt_ref, pb_ref, be_ref, lw_ref, y0_ref, z_ref, y_ref, st_ref, s_ref):
        @pl.when(pl.program_id(1) == 0)
        def _():
            s_ref[...] = jnp.zeros_like(s_ref)

        S0 = [s_ref[h] for h in range(hb)]
        for h in range(hb):
            st_ref[0, h] = S0[h]
        rows = [_heads(x, hb, head) for x in (w_ref, u_ref, rt_ref, pb_ref, be_ref, lw_ref, y0_ref)]
        y, S1 = _wkv_state(S0, *rows, [z_ref[0, h] for h in range(hb)])
        _put_heads(y_ref, y, head)
        for h in range(hb):
            s_ref[h] = S1[h]

    blk = pl.BlockSpec((C, bw), lambda g, c: (c, g))
    sq = pl.BlockSpec((1, hb, head, head), lambda g, c: (c, g, 0, 0))
    return pl.pallas_call(
        body, name="wkv_state", grid=(nh // hb, nc),
        in_specs=[blk] * 7 + [sq], out_specs=[blk, sq],
        out_shape=[jax.ShapeDtypeStruct((T, RW), f32), jax.ShapeDtypeStruct((nc, nh, head, head), f32)],
        scratch_shapes=[pltpu.VMEM((hb, head, head), f32)],
        compiler_params=_cparams(2),
    )(W, U, Rt, Pb, Bend, lw, Y0, Z)


def _wkv_state_bwd(W, U, Rt, Pb, Bend, lw, Y0, Z, states, dy, head, T, RW):
    C, nh, hb, bw, nc = _wkv_dims(head, T, RW, WKV_STATE_HEADS)

    def body(w_ref, u_ref, rt_ref, pb_ref, be_ref, lw_ref, y0_ref, z_ref, st_ref, dy_ref,
             dw_ref, du_ref, drt_ref, dpb_ref, dbe_ref, dlw_ref, dz_ref, ds_ref):
        @pl.when(pl.program_id(1) == 0)
        def _():
            ds_ref[...] = jnp.zeros_like(ds_ref)

        dS1 = [ds_ref[h] for h in range(hb)]
        for h in range(hb):
            dz_ref[0, h] = dS1[h]
        rows = [_heads(x, hb, head) for x in (w_ref, u_ref, rt_ref, pb_ref, be_ref, lw_ref)]
        Y0 = _heads(y0_ref, hb, head)
        Zs = [z_ref[0, h] for h in range(hb)]
        _, vjp = jax.vjp(lambda s0, *rw: _wkv_state(s0, *rw, Y0, Zs), [st_ref[0, h] for h in range(hb)], *rows)
        grads = vjp((_heads(dy_ref, hb, head), dS1))
        for o_ref, vals in zip((dw_ref, du_ref, drt_ref, dpb_ref, dbe_ref, dlw_ref), grads[1:]):
            _put_heads(o_ref, vals, head)
        for h in range(hb):
            ds_ref[h] = grads[0][h]

    blk = pl.BlockSpec((C, bw), lambda g, c: (nc - 1 - c, g))
    sq = pl.BlockSpec((1, hb, head, head), lambda g, c: (nc - 1 - c, g, 0, 0))
    return pl.pallas_call(
        body, name="wkv_state_bwd", grid=(nh // hb, nc),
        in_specs=[blk] * 7 + [sq, sq, blk], out_specs=[blk] * 6 + [sq],
        out_shape=[jax.ShapeDtypeStruct((T, RW), f32)] * 6 + [jax.ShapeDtypeStruct((nc, nh, head, head), f32)],
        scratch_shapes=[pltpu.VMEM((hb, head, head), f32)],
        compiler_params=_cparams(2),
    )(W, U, Rt, Pb, Bend, lw, Y0, Z, states, dy)


def _wkv_local_bwd(r, lw, k, v, a, b, cots, d_lw_x, dr_x, dk_x, dv_x, head, T, RW):
    C, nh, hb, bw, nc = _wkv_dims(head, T, RW, WKV_HEADS)

    def body(*refs):
        ins, cot_refs, add_refs, outs = refs[:6], refs[6:13], refs[13:17], refs[17:]
        _, vjp = jax.vjp(_wkv_local, *[_heads(x, hb, head) for x in ins])
        cts = [_heads(x, hb, head) for x in cot_refs[:6]] + [[cot_refs[6][0, h] for h in range(hb)]]
        dr, dlw, dk, dv, da, db = vjp(tuple(cts))
        dlw_x, drx, dkx, dvx = [_heads(x, hb, head) for x in add_refs]
        _put_heads(outs[0], _each(jnp.add, dr, drx), head)
        _put_heads(outs[1], _each(jnp.add, dlw, dlw_x), head)
        _put_heads(outs[2], _each(jnp.add, dk, dkx), head)
        _put_heads(outs[3], _each(jnp.add, dv, dvx), head)
        _put_heads(outs[4], da, head)
        _put_heads(outs[5], db, head)

    blk = pl.BlockSpec((C, bw), lambda g, c: (c, g))
    sq = pl.BlockSpec((1, hb, head, head), lambda g, c: (c, g, 0, 0))
    return pl.pallas_call(
        body, name="wkv_local_bwd", grid=(nh // hb, nc),
        in_specs=[blk] * 12 + [sq] + [blk] * 4, out_specs=[blk] * 6,
        out_shape=[jax.ShapeDtypeStruct((T, RW), f32)] * 6,
        compiler_params=_cparams(2),
    )(r, lw, k, v, a, b, *cots, d_lw_x, dr_x, dk_x, dv_x)


def _rows_tile(R, row_bytes, budget, mult=SUBLANES):
    best = None
    t = mult
    while t <= R:
        if R % t == 0 and t * row_bytes <= budget:
            best = t
        t += mult
    return best if best is not None else R


def _sum_slots(name, parts):
    S, R, W = parts.shape
    budget = 4 << 20
    tr = _rows_tile(R, S * W * 4, budget, 2 * SUBLANES)
    cw = W if tr * S * W * 4 <= 2 * budget else _tile(W, max(LANES, 2 * budget // (S * tr * 4)))

    def body(p_ref, o_ref):
        acc = p_ref[0].astype(f32)
        for d in range(1, S):
            acc = acc + p_ref[d].astype(f32)
        o_ref[...] = acc

    return pl.pallas_call(
        body, name=name, grid=(R // tr, W // cw),
        in_specs=[pl.BlockSpec((S, tr, cw), lambda i, j: (0, i, j))],
        out_specs=pl.BlockSpec((tr, cw), lambda i, j: (i, j)),
        out_shape=jax.ShapeDtypeStruct((R, W), f32),
        compiler_params=_cparams(2),
    )(parts)


def _add_pair(name, a, b):
    R, W = a.shape
    tr = _rows_tile(R, W * 4, 2 << 20, 2 * SUBLANES)

    def body(a_ref, b_ref, o_ref):
        o_ref[...] = (a_ref[...].astype(f32) + b_ref[...].astype(f32)).astype(o_ref.dtype)

    blk = pl.BlockSpec((tr, W), lambda i: (i, 0))
    return pl.pallas_call(
        body, name=name, grid=(R // tr,), in_specs=[blk, blk], out_specs=blk,
        out_shape=jax.ShapeDtypeStruct((R, W), a.dtype), compiler_params=_cparams(1),
    )(a, b)


def _adamw(name, w, g, m, v):
    R, W = w.shape
    tr = _rows_tile(R, W * 4, 1 << 20)

    def body(w_ref, g_ref, m_ref, v_ref, d_ref, nm_ref, nv_ref):
        g_v = g_ref[...]
        nm = ADAM_B1 * m_ref[...] + (1.0 - ADAM_B1) * g_v
        nv = ADAM_B2 * v_ref[...] + (1.0 - ADAM_B2) * (g_v * g_v)
        m_hat = nm / (1.0 - ADAM_B1 ** ADAM_STEP)
        v_hat = nv / (1.0 - ADAM_B2 ** ADAM_STEP)
        d_ref[...] = -ADAM_LR * (m_hat / (jnp.sqrt(v_hat) + ADAM_EPS) + ADAM_WD * w_ref[...])
        nm_ref[...] = nm
        nv_ref[...] = nv

    blk = pl.BlockSpec((tr, W), lambda i: (i, 0))
    return pl.pallas_call(
        body, name=name, grid=(R // tr,),
        in_specs=[blk] * 4, out_specs=[blk] * 3,
        out_shape=[jax.ShapeDtypeStruct((R, W), f32)] * 3,
        compiler_params=_cparams(1),
    )(w, g, m, v)


ANY = pl.BlockSpec(memory_space=pl.ANY)


def _place():
    return lax.axis_index("x"), lax.axis_index("y"), lax.axis_index("c")


def _copy_chunks(rows, cols):
    k = SHARE_CHUNKS // 2
    if rows % (k * 2 * SUBLANES) == 0:
        return [(pl.ds(q * (rows // k), rows // k), pl.ds(0, cols)) for q in range(k)]
    if cols % (k * LANES) == 0:
        return [(pl.ds(0, rows), pl.ds(q * (cols // k), cols // k)) for q in range(k)]
    return [(pl.ds(0, rows), pl.ds(0, cols))]


def _gather_chips(arrays):
    n = len(arrays)
    parts = [(a, h, blk) for a, arr in enumerate(arrays) for h in range(2) for blk in _copy_chunks(*arr.shape[1:])]

    def body(*refs):
        ins, outs = refs[:n], refs[n:2 * n]
        send_sems, recv_sems, local_sems = refs[2 * n:]
        x, y, c = _place()
        mine = 2 * x + y
        sib = (x, y, 1 - c)
        chips = [(1 - x, y), (x, 1 - y), (1 - x, 1 - y)]
        local = [pltpu.make_async_copy(ins[a].at[(h, *blk)], outs[a].at[(mine, h, *blk)], local_sems.at[p])
                 for p, (a, h, blk) in enumerate(parts)]
        for cp in local:
            cp.start()

        def over_ici(a, j, slot):
            px, py = chips[j]
            return pltpu.make_async_remote_copy(
                src_ref=ins[a].at[c], dst_ref=outs[a].at[slot, c], send_sem=send_sems.at[3 * a + j],
                recv_sem=recv_sems.at[3 * a + j], device_id=(px, py, c), device_id_type=MESH)

        def over_d2d(a, j, half):
            px, py = chips[j]
            slot = 2 * px + py
            return pltpu.make_async_remote_copy(
                src_ref=outs[a].at[slot, half], dst_ref=outs[a].at[slot, half], send_sem=send_sems.at[3 * (n + a) + j],
                recv_sem=recv_sems.at[3 * (n + a) + j], device_id=sib, device_id_type=MESH)

        sends = [over_ici(a, j, mine) for a in range(n) for j in range(3)]
        for cp in sends:
            cp.start()
        passed = []
        for a in range(n):
            for j, (px, py) in enumerate(chips):
                over_ici(a, j, 2 * px + py).wait_recv()
                cp = over_d2d(a, j, c)
                cp.start()
                passed.append(cp)
        for a in range(n):
            for j in range(3):
                over_d2d(a, j, 1 - c).wait_recv()
        for cp in sends + passed:
            cp.wait_send()
        for cp in local:
            cp.wait()

    return pl.pallas_call(
        body, name="gather_weights",
        in_specs=[ANY] * n, out_specs=[ANY] * n,
        out_shape=[jax.ShapeDtypeStruct((4,) + a.shape, a.dtype) for a in arrays],
        scratch_shapes=[pltpu.SemaphoreType.DMA((6 * n,)), pltpu.SemaphoreType.DMA((6 * n,)),
                        pltpu.SemaphoreType.DMA((len(parts),))],
    )(*arrays)


def _exchange_chips(pieces, whole):
    n, m = len(pieces), len(whole)
    parts = [(a, blk) for a, arr in enumerate(pieces) for blk in _copy_chunks(*arr.shape[1:])]

    def body(*refs):
        ins, outs = refs[:n + m], refs[n + m:2 * (n + m)]
        send_sems, recv_sems, local_sems = refs[2 * (n + m):]
        x, y, c = _place()
        chip, dev = 2 * x + y, 4 * x + 2 * y + c
        chips = [(1 - x, y), (x, 1 - y), (1 - x, 1 - y)]
        peers = [(x ^ (k >> 2), y ^ ((k >> 1) & 1), c ^ (k & 1)) for k in range(1, 8)]
        local = [pltpu.make_async_copy(ins[a].at[(chip, *blk)], outs[a].at[(chip, *blk)], local_sems.at[p])
                 for p, (a, blk) in enumerate(parts)]
        local += [pltpu.make_async_copy(ins[n + b], outs[n + b].at[dev], local_sems.at[len(parts) + b]) for b in range(m)]
        for cp in local:
            cp.start()

        def piece(a, j, slot_from):
            px, py = chips[j]
            return pltpu.make_async_remote_copy(
                src_ref=ins[a].at[2 * px + py], dst_ref=outs[a].at[slot_from], send_sem=send_sems.at[3 * a + j],
                recv_sem=recv_sems.at[3 * a + j], device_id=(px, py, c), device_id_type=MESH)

        def everyone(b, j, slot_from):
            px, py, pc = peers[j]
            return pltpu.make_async_remote_copy(
                src_ref=ins[n + b], dst_ref=outs[n + b].at[slot_from], send_sem=send_sems.at[3 * n + 7 * b + j],
                recv_sem=recv_sems.at[3 * n + 7 * b + j], device_id=(px, py, pc), device_id_type=MESH)

        sends = [everyone(b, j, dev) for b in range(m) for j in range(7)]
        sends += [piece(a, j, chip) for a in range(n) for j in range(3)]
        for cp in sends:
            cp.start()
        for b in range(m):
            for j, (px, py, pc) in enumerate(peers):
                everyone(b, j, 4 * px + 2 * py + pc).wait_recv()
        for a in range(n):
            for j, (px, py) in enumerate(chips):
                piece(a, j, 2 * px + py).wait_recv()
        for cp in sends:
            cp.wait_send()
        for cp in local:
            cp.wait()

    shapes = [jax.ShapeDtypeStruct(a.shape, a.dtype) for a in pieces]
    shapes += [jax.ShapeDtypeStruct((8,) + a.shape, a.dtype) for a in whole]
    nsem = 3 * n + 7 * m
    return pl.pallas_call(
        body, name="exchange_grads",
        in_specs=[ANY] * (n + m), out_specs=[ANY] * (n + m), out_shape=shapes,
        scratch_shapes=[pltpu.SemaphoreType.DMA((nsem,)), pltpu.SemaphoreType.DMA((nsem,)),
                        pltpu.SemaphoreType.DMA((len(parts) + m,))],
    )(*pieces, *whole)


def _share_sibling(name, arrays):
    n = len(arrays)
    parts = []
    for a, arr in enumerate(arrays):
        k = SHARE_CHUNKS if arr.shape[0] % (SHARE_CHUNKS * SUBLANES) == 0 else 1
        k = arr.shape[0] if arr.ndim == 3 else k
        step = arr.shape[0] // k
        parts += [(a, q * step, step) for q in range(k)]
    npart = len(parts)

    def body(*refs):
        ins, outs = refs[:n], refs[n:2 * n]
        send_sems, recv_sems = refs[2 * n:]
        x, y, c = _place()

        def copy(p):
            a, r0, nr = parts[p]
            return pltpu.make_async_remote_copy(
                src_ref=ins[a].at[pl.ds(r0, nr)], dst_ref=outs[a].at[pl.ds(r0, nr)], send_sem=send_sems.at[p],
                recv_sem=recv_sems.at[p], device_id=(x, y, 1 - c), device_id_type=MESH)

        copies = [copy(p) for p in range(npart)]
        for cp in copies:
            cp.start()
        for cp in copies:
            cp.wait_recv()
        for cp in copies:
            cp.wait_send()

    return pl.pallas_call(
        body, name=name,
        in_specs=[ANY] * n, out_specs=[ANY] * n,
        out_shape=[jax.ShapeDtypeStruct(a.shape, a.dtype) for a in arrays],
        scratch_shapes=[pltpu.SemaphoreType.DMA((npart,)), pltpu.SemaphoreType.DMA((npart,))],
    )(*arrays)


def _place_blocks(blocks, axis):
    shape = list(blocks[0].shape)
    shape[axis] = sum(b.shape[axis] for b in blocks)
    buf = lax.empty(tuple(shape), blocks[0].dtype)
    at = 0
    for b in blocks:
        buf = lax.dynamic_update_slice_in_dim(buf, b, at, axis)
        at += b.shape[axis]
    return buf


def kernel(x, norm_pre_g, w_in, mu_shift, w0, w_lora_up, a0, a_lora_up, k_k, k_a, r_k, lnx_g, lnx_b, conv_w, conv_b, cln_g, cln_b, w_pw2, b_pw2, w_out, norm_post_g, loss_target, m_norm_pre_g, m_w_in, m_mu_shift, m_w0, m_w_lora_up, m_a0, m_a_lora_up, m_k_k, m_k_a, m_r_k, m_lnx_g, m_lnx_b, m_conv_w, m_conv_b, m_cln_g, m_cln_b, m_w_pw2, m_b_pw2, m_w_out, m_norm_post_g, v_norm_pre_g, v_w_in, v_mu_shift, v_w0, v_w_lora_up, v_a0, v_a_lora_up, v_k_k, v_k_a, v_r_k, v_lnx_g, v_lnx_b, v_conv_w, v_conv_b, v_cln_g, v_cln_b, v_w_pw2, v_b_pw2, v_w_out, v_norm_post_g):
    _, T, D = x.shape
    RW = w0.shape[0]
    CW = conv_b.shape[0]
    head = r_k.shape[1]
    lora = w_lora_up.shape[0]
    ktaps = conv_w.shape[0]
    assert RW == CW and 2 * lora <= LORA_PAD and ktaps - 1 <= CONV_HALO
    n_in = 3 * RW + 2 * lora + RW + 3 * CW
    shard = n_in // 4
    PW = 7 * RW + LORA_PAD
    off_l = 7 * RW
    tm = min(256, T // 2)
    tm_wide = min(128, T // 2)
    row = lambda vec: vec.reshape(1, -1)
    x2, tgt2 = x[0], loss_target[0]

    halves = lambda a: a.reshape(2, a.shape[0] // 2, a.shape[1])
    conv_w_p = jnp.concatenate([conv_w, jnp.zeros((CONV_HALO - ktaps, CW // 4), f32)], axis=0)
    w_in_t, m_w_in_t, v_w_in_t = w_in.T, m_w_in.T, v_w_in.T
    g_wup, g_aup, g_cw, g_pw2, g_wout, g_win = _gather_chips(
        [halves(a) for a in (w_lora_up, a_lora_up, conv_w_p, w_pw2.astype(bf16), w_out.astype(bf16), w_in_t.astype(bf16))])
    cat_cols = lambda g: jnp.concatenate([g[s].reshape(-1, g.shape[-1]) for s in range(4)], axis=1)
    win_t = g_win.reshape(n_in, D)
    lo = 3 * RW
    wp_t = _place_blocks([win_t[:lo], win_t[lo + 2 * lora:], win_t[lo:lo + 2 * lora],
                          jnp.zeros((LORA_PAD - 2 * lora, D), bf16)], axis=0)
    wup_full, aup_full, cw_p = cat_cols(g_wup), cat_cols(g_aup), cat_cols(g_cw)
    zl = lambda n: jnp.zeros((n, RW), f32)
    wup_p = jnp.concatenate([wup_full, zl(LORA_PAD - lora)], axis=0)
    aup_p = jnp.concatenate([zl(lora), aup_full, zl(LORA_PAD - 2 * lora)], axis=0)
    pw2_full = g_pw2.reshape(CW, CW)
    wout_full = g_wout.reshape(RW + CW, D)
    mu_r, mu_k, mu_v = (row(mu_shift[s * RW:(s + 1) * RW]) for s in range(3))
    mu_l = row(jnp.concatenate([mu_shift[3 * RW:], jnp.zeros((LORA_PAD - 2 * lora,), f32)]))

    npg = row(norm_pre_g)
    (h,) = _row_fwd("rms_pre", _fn_rms_pre, [(npg, False)], [(x2, 0, D, False)], [(D, bf16)], T, tm)
    proj = _matmul("proj", h, wp_t, "nt", f32)
    xs_r = _shift_fwd("shift_r", proj, 0, RW, mu_r, T, tm)
    xs_k = _shift_fwd("shift_k", proj, RW, RW, mu_k, T, tm)
    xs_v = _shift_fwd("shift_v", proj, 2 * RW, RW, mu_v, T, tm)
    xs_l = _shift_fwd("shift_l", proj, off_l, LORA_PAD, mu_l, T, tm)
    lora_params = [(row(w0), False), (wup_p, False), (row(a0), False), (aup_p, False)]
    qw, qa = _row_fwd("lora_up", _fn_lora, lora_params, [(xs_l, 0, LORA_PAD, False)], [(RW, f32), (RW, f32)], T, tm)
    ncol = RW // _tile(RW, 512)
    fn_pre = functools.partial(_fn_rwkv_pre, head)
    pre_params = [(row(k_k), True), (row(k_a), True)]
    pre_rows = [(xs_k, 0, RW, True), (qw, 0, RW, True), (qa, 0, RW, True)]
    lw, k_h, a_rec, b_rec = _row_fwd("rwkv_pre", fn_pre, pre_params, pre_rows, [(RW, f32)] * 4, T, tm, ncol)
    wkv_in = (xs_r, lw, k_h, xs_v, a_rec, b_rec)
    c_w, c_u, c_rt, c_pb, c_y0, c_bend, c_z = _wkv_local_fwd(*wkv_in, head, T, RW)
    wkv_loc = (c_w, c_u, c_rt, c_pb, c_bend, lw, c_y0, c_z)
    y_wkv, states = _wkv_state_fwd(*wkv_loc, head, T, RW)
    fn_post = functools.partial(_fn_rwkv_post, head)
    post_params = [(row(lnx_g), True), (row(lnx_b), True), (r_k.reshape(1, RW), True)]
    post_rows = [(y_wkv, 0, RW, True), (xs_r, 0, RW, True), (k_h, 0, RW, True), (xs_v, 0, RW, True),
                 (proj, 3 * RW, RW, True)]
    (y_rwkv,) = _row_fwd("rwkv_post", fn_post, post_params, post_rows, [(RW, bf16)], T, tm, ncol)

    c_pre = _conv_fwd(proj, 4 * RW, 5 * RW, CW, cw_p, row(conv_b), ktaps, T, tm)
    ln_params = [(row(cln_g), False), (row(cln_b), False)]
    (c_act,) = _row_fwd("conv_ln", _fn_conv_ln, ln_params, [(c_pre, 0, CW, False)], [(CW, bf16)], T, tm)
    c2 = _matmul("pw2", c_act, pw2_full, "nn", f32)
    cpost_params = [(row(b_pw2), True)]
    cpost_rows = [(c2, 0, CW, True), (proj, 6 * RW, CW, True)]
    (y_conv,) = _row_fwd("conv_post", _fn_conv_post, cpost_params, cpost_rows, [(CW, bf16)], T, tm, ncol)

    mix = jnp.concatenate([y_rwkv, y_conv], axis=1)
    out = _matmul("out_proj", mix, wout_full, "nn", f32)
    d_out, gx_res, loss_part, g_npost = _post(out, x2, tgt2, row(norm_post_g), T, D, tm_wide)

    g_wout_full = _matmul("d_w_out", mix, d_out, "tn", bf16)
    d_mix = _matmul("d_mix", d_out, wout_full, "nt", f32)

    d_c2, d_gconv, g_bpw2 = _row_bwd("conv_post_bwd", _fn_conv_post, cpost_params, cpost_rows,
                                      [(d_mix, RW, CW, True)], [bf16, bf16], T, tm, ncol)
    g_pw2_full = _matmul("d_w_pw2", c_act, d_c2, "tn", bf16)
    d_cact = _matmul("d_c_act", d_c2, pw2_full, "nt", f32)
    d_cpre, g_clng, g_clnb = _row_bwd("conv_ln_bwd", _fn_conv_ln, ln_params, [(c_pre, 0, CW, False)],
                                      [(d_cact, 0, CW, False)], [f32], T, tm)
    d_gluv, d_glug, g_cw_p, g_cb = _conv_bwd(proj, 4 * RW, 5 * RW, CW, cw_p, d_cpre, ktaps, T, tm)

    d_y, dr_x, dk_x, dv_x, d_grwkv, g_lnxg, g_lnxb, g_rk = _row_bwd(
        "rwkv_post_bwd", fn_post, post_params, post_rows, [(d_mix, 0, RW, True)], [f32, f32, f32, f32, bf16], T, tm, ncol)
    d_cw, d_cu, d_crt, d_cpb, d_cbend, d_lw_dec, d_cz = _wkv_state_bwd(*wkv_loc, states, d_y, head, T, RW)
    d_xr, d_lw, d_kh, d_xv, d_a, d_b = _wkv_local_bwd(
        *wkv_in, (d_cw, d_cu, d_crt, d_cpb, d_y, d_cbend, d_cz), d_lw_dec, dr_x, dk_x, dv_x, head, T, RW)
    pre_cots = [(d_lw, 0, RW, True), (d_kh, 0, RW, True), (d_a, 0, RW, True), (d_b, 0, RW, True)]
    d_xk, d_qw, d_qa, g_kk, g_ka = _row_bwd("rwkv_pre_bwd", fn_pre, pre_params, pre_rows, pre_cots, [f32, f32, f32],
                                            T, tm, ncol)
    d_xl, g_w0, g_wup_p, g_a0, g_aup_p = _row_bwd("lora_up_bwd", _fn_lora, lora_params, [(xs_l, 0, LORA_PAD, False)],
                                                  [(d_qw, 0, RW, False), (d_qa, 0, RW, False)], [f32], T, tm)
    dp_r, g_mur = _shift_bwd("shift_r_bwd", proj, 0, RW, mu_r, d_xr, T, tm)
    dp_k, g_muk = _shift_bwd("shift_k_bwd", proj, RW, RW, mu_k, d_xk, T, tm)
    dp_v, g_muv = _shift_bwd("shift_v_bwd", proj, 2 * RW, RW, mu_v, d_xv, T, tm)
    dp_l, g_mul = _shift_bwd("shift_l_bwd", proj, off_l, LORA_PAD, mu_l, d_xl, T, tm)
    d_proj = _place_blocks([dp_r, dp_k, dp_v, d_grwkv, d_gluv, d_glug, d_gconv, dp_l], axis=1)

    g_wp_t = _matmul("d_w_in", d_proj, h, "tn", bf16, tn_t=1024)
    d_h = _matmul("d_h", d_proj, wp_t, "nn", bf16, tn_t=1024, tk_t=2560)
    grad_x2, g_npre = _rms_pre_bwd(x2, npg, d_h, gx_res, T, D, tm_wide)

    g_win_t = _place_blocks([g_wp_t[:lo], g_wp_t[off_l:off_l + 2 * lora], g_wp_t[lo:off_l]], axis=0)
    core = lax.axis_index("c")

    def row_halves(a):
        a = a.reshape(4, 2, a.shape[1] // 2, a.shape[2])
        return a[:, 0], a[:, 1]

    def keep_give(a):
        h0, h1 = row_halves(a)
        return jnp.where(core == 0, h0, h1), jnp.where(core == 0, h1, h0)

    win_keep, win_give = keep_give(g_win_t.reshape(4, shard, D))
    wout_keep, wout_give = keep_give(g_wout_full.reshape(4, (RW + CW) // 4, D))
    pw2_keep, pw2_give = keep_give(g_pw2_full.reshape(4, CW // 4, CW))
    g_mu = jnp.concatenate([g_mur[0], g_muk[0], g_muv[0], g_mul[0, :2 * lora]])
    pad_rows = lambda a, n: jnp.concatenate([a, jnp.zeros((n - a.shape[0], a.shape[1]), f32)], axis=0)
    n_mu = -(-mu_shift.shape[0] // RW)
    small_vecs = [g_npre.reshape(D // RW, RW), pad_rows(jnp.pad(g_mu, (0, n_mu * RW - g_mu.shape[0])).reshape(n_mu, RW), n_mu),
                  g_w0, g_a0, g_kk, g_ka, g_rk, g_lnxg, g_lnxb, g_cb, g_clng, g_clnb, g_bpw2,
                  g_npost.reshape(D // RW, RW), g_wup_p[:lora], g_aup_p[lora:2 * lora], g_cw_p[:ktaps]]
    n_small = sum(a.shape[0] for a in small_vecs)
    n_small_pad = -(-n_small // (2 * SUBLANES)) * (2 * SUBLANES)
    small = pad_rows(jnp.concatenate(small_vecs, axis=0), n_small_pad)

    win_got, wout_got, pw2_got = _share_sibling("pair_exchange", [win_give, wout_give, pw2_give])
    flat = lambda a: a.reshape(-1, a.shape[-1])
    chip_sum = lambda nm, keep, got: _add_pair(nm, flat(keep), flat(got)).reshape(keep.shape)
    q_win = chip_sum("chip_sum_w_in", win_keep, win_got)
    q_wout = chip_sum("chip_sum_w_out", wout_keep, wout_got)
    q_pw2 = chip_sum("chip_sum_w_pw2", pw2_keep, pw2_got)
    r_win, r_wout, r_pw2, r_small = _exchange_chips([q_win, q_wout, q_pw2], [small])
    s_win = _sum_slots("sum_w_in", r_win)
    s_wout = _sum_slots("sum_w_out", r_wout)
    s_pw2 = _sum_slots("sum_w_pw2", r_pw2)
    s_small = _sum_slots("sum_small", r_small)
    o_win, o_wout, o_pw2 = _share_sibling("share_sibling", [s_win, s_wout, s_pw2])
    south = lax.axis_index("c") == 0

    def both_halves(own, other):
        return jnp.concatenate([jnp.where(south, own, other), jnp.where(south, other, own)], axis=0)

    grad_w_in = both_halves(s_win, o_win)
    grad_w_out = both_halves(s_wout, o_wout)
    grad_w_pw2 = both_halves(s_pw2, o_pw2)

    chip = 2 * lax.axis_index("x") + lax.axis_index("y")
    pos = [0]

    def take(nrows):
        a = s_small[pos[0]:pos[0] + nrows]
        pos[0] += nrows
        return a

    my_cols = lambda a, w: lax.dynamic_slice_in_dim(a, chip * w, w, axis=1)
    grads = {}
    grads["norm_pre_g"] = take(D // RW).reshape(D)
    grads["mu_shift"] = take(n_mu).reshape(-1)[:mu_shift.shape[0]]
    for nm in ["w0", "a0", "k_k", "k_a"]:
        grads[nm] = take(1).reshape(RW)
    grads["r_k"] = take(1).reshape(r_k.shape)
    for nm in ["lnx_g", "lnx_b", "conv_b", "cln_g", "cln_b", "b_pw2"]:
        grads[nm] = take(1).reshape(RW)
    grads["norm_post_g"] = take(D // RW).reshape(D)
    grads["w_lora_up"] = my_cols(take(lora), RW // 4)
    grads["a_lora_up"] = my_cols(take(lora), RW // 4)
    grads["conv_w"] = my_cols(take(ktaps), CW // 4)
    grads["w_in"], grads["w_out"], grads["w_pw2"] = grad_w_in, grad_w_out, grad_w_pw2

    weights = dict(norm_pre_g=norm_pre_g, w_in=w_in, mu_shift=mu_shift, w0=w0, w_lora_up=w_lora_up, a0=a0,
                   a_lora_up=a_lora_up, k_k=k_k, k_a=k_a, r_k=r_k, lnx_g=lnx_g, lnx_b=lnx_b, conv_w=conv_w,
                   conv_b=conv_b, cln_g=cln_g, cln_b=cln_b, w_pw2=w_pw2, b_pw2=b_pw2, w_out=w_out,
                   norm_post_g=norm_post_g)
    ms = dict(norm_pre_g=m_norm_pre_g, w_in=m_w_in, mu_shift=m_mu_shift, w0=m_w0, w_lora_up=m_w_lora_up, a0=m_a0,
              a_lora_up=m_a_lora_up, k_k=m_k_k, k_a=m_k_a, r_k=m_r_k, lnx_g=m_lnx_g, lnx_b=m_lnx_b, conv_w=m_conv_w,
              conv_b=m_conv_b, cln_g=m_cln_g, cln_b=m_cln_b, w_pw2=m_w_pw2, b_pw2=m_b_pw2, w_out=m_w_out,
              norm_post_g=m_norm_post_g)
    vs = dict(norm_pre_g=v_norm_pre_g, w_in=v_w_in, mu_shift=v_mu_shift, w0=v_w0, w_lora_up=v_w_lora_up, a0=v_a0,
              a_lora_up=v_a_lora_up, k_k=v_k_k, k_a=v_k_a, r_k=v_r_k, lnx_g=v_lnx_g, lnx_b=v_lnx_b, conv_w=v_conv_w,
              conv_b=v_conv_b, cln_g=v_cln_g, cln_b=v_cln_b, w_pw2=v_w_pw2, b_pw2=v_b_pw2, w_out=v_w_out,
              norm_post_g=v_norm_post_g)
    names = list(weights)
    big = ["w_in", "w_out", "w_pw2"]
    deltas, new_m, new_v = {}, {}, {}
    d_t, m_t, v_t = _adamw("adamw_w_in", w_in_t, grad_w_in, m_w_in_t, v_w_in_t)
    grads["w_in"], deltas["w_in"], new_m["w_in"], new_v["w_in"] = grad_w_in.T, d_t.T, m_t.T, v_t.T
    for nm in big[1:]:
        deltas[nm], new_m[nm], new_v[nm] = _adamw("adamw_" + nm, weights[nm], grads[nm], ms[nm], vs[nm])
    rest = [nm for nm in names if nm not in big]
    sizes = [weights[nm].size for nm in rest]
    total = sum(sizes)
    width = 4 * LANES
    rows_p = -(-total // (width * SUBLANES)) * SUBLANES

    def pack(d):
        flat = jnp.concatenate([d[nm].reshape(-1) for nm in rest])
        return jnp.pad(flat, (0, rows_p * width - total)).reshape(rows_p, width)

    p_d, p_m, p_v = _adamw("adamw_small", pack(weights), pack(grads), pack(ms), pack(vs))
    o = 0
    for nm, sz in zip(rest, sizes):
        shp = weights[nm].shape
        deltas[nm] = p_d.reshape(-1)[o:o + sz].reshape(shp)
        new_m[nm] = p_m.reshape(-1)[o:o + sz].reshape(shp)
        new_v[nm] = p_v.reshape(-1)[o:o + sz].reshape(shp)
        o += sz

    loss = lax.psum(loss_part[0, 0], ("x", "y", "c"))
    grad_x = grad_x2[None]
    return (loss, grad_x, *[grads[nm] for nm in names], *[deltas[nm] for nm in names],
            *[new_m[nm] for nm in names], *[new_v[nm] for nm in names])
```

```python
import functools

import jax
import jax.numpy as jnp
from jax import lax
from jax.experimental import pallas as pl
from jax.experimental.pallas import tpu as pltpu

f32 = jnp.float32
bf16 = jnp.bfloat16
MESH = pl.DeviceIdType.MESH

NORM_EPS = 1e-6
LN_EPS = 1e-5
ADAM_LR, ADAM_B1, ADAM_B2, ADAM_EPS, ADAM_WD, ADAM_STEP = 0.001, 0.9, 0.999, 1e-08, 0.01, 10

LANES = 128
SUBLANES = 8
LORA_PAD = 256
CONV_HALO = 32
WKV_CHUNK = 64
WKV_HEADS = 16
WKV_STATE_HEADS = 32
WKV_PREC = lax.Precision.HIGH
SHARE_CHUNKS = 8
VMEM_LIMIT = 56 * 1024 * 1024


def _cparams(n_axes):
    return pltpu.CompilerParams(dimension_semantics=("arbitrary",) * n_axes, vmem_limit_bytes=VMEM_LIMIT)


def _tile(dim, target):
    best = None
    t = LANES
    while t <= min(dim, target):
        if dim % t == 0:
            best = t
        t += LANES
    return best if best is not None else dim


def _mm(a, b, prec=None):
    return lax.dot_general(a, b, (((1,), (0,)), ((), ())), precision=prec, preferred_element_type=f32)


def _mm_nt(a, b, prec=None):
    return lax.dot_general(a, b, (((1,), (1,)), ((), ())), precision=prec, preferred_element_type=f32)


def _mm_tn(a, b, prec=None):
    return lax.dot_general(a, b, (((0,), (0,)), ((), ())), precision=prec, preferred_element_type=f32)


@jax.custom_vjp
def _bmm(a, b):
    return _mm(a.astype(bf16), b.astype(bf16))


def _bmm_fwd(a, b):
    return _bmm(a, b), (a, b)


def _bmm_bwd(res, dc):
    a, b = res
    dcb = dc.astype(bf16)
    return _mm_nt(dcb, b.astype(bf16)), _mm_tn(a.astype(bf16), dcb)


_bmm.defvjp(_bmm_fwd, _bmm_bwd)


def _matmul(name, a, b, mode, out_dtype, tm_t=1024, tn_t=1024, tk_t=4096, comm=None):
    if mode == "nn":
        (M, K), (_, N) = a.shape, b.shape
    elif mode == "nt":
        (M, K), (N, _) = a.shape, b.shape
    else:
        (K, M), (_, N) = a.shape, b.shape
    tm, tn, tk = _tile(M, tm_t), _tile(N, tn_t), _tile(K, tk_t)
    ni, nj, nk = M // tm, N // tn, K // tk
    dot = {"nn": _mm, "nt": _mm_nt, "tn": _mm_tn}[mode]
    nc = len(comm.operands) if comm else 0

    def body(*refs):
        a_ref, b_ref = refs[:2]
        o_ref = refs[2 + nc]
        scratch = refs[3 + 2 * nc:]
        i, j, k = pl.program_id(0), pl.program_id(1), pl.program_id(2)
        if comm:
            comm_refs = (refs[2:2 + nc], refs[3 + nc:3 + 2 * nc], scratch[:len(comm.scratch)])

            @pl.when(jnp.logical_and(jnp.logical_and(i == 0, j == 0), k == 0))
            def _():
                comm.start(*comm_refs)

        if nk == 1:
            o_ref[...] = dot(a_ref[...], b_ref[...]).astype(o_ref.dtype)
        else:
            acc_ref = scratch[-1]

            @pl.when(k == 0)
            def _():
                acc_ref[...] = jnp.zeros_like(acc_ref)

            acc_ref[...] += dot(a_ref[...], b_ref[...])

            @pl.when(k == nk - 1)
            def _():
                o_ref[...] = acc_ref[...].astype(o_ref.dtype)

        if comm:
            @pl.when(jnp.logical_and(jnp.logical_and(i == ni - 1, j == nj - 1), k == nk - 1))
            def _():
                comm.wait(*comm_refs)

    a_spec = {"nn": pl.BlockSpec((tm, tk), lambda i, j, k: (i, k)),
              "nt": pl.BlockSpec((tm, tk), lambda i, j, k: (i, k)),
              "tn": pl.BlockSpec((tk, tm), lambda i, j, k: (k, i))}[mode]
    b_spec = {"nn": pl.BlockSpec((tk, tn), lambda i, j, k: (k, j)),
              "nt": pl.BlockSpec((tn, tk), lambda i, j, k: (j, k)),
              "tn": pl.BlockSpec((tk, tn), lambda i, j, k: (k, j))}[mode]
    res = pl.pallas_call(
        body, name=name, grid=(ni, nj, nk),
        in_specs=[a_spec, b_spec] + [ANY] * nc,
        out_specs=[pl.BlockSpec((tm, tn), lambda i, j, k: (i, j))] + [ANY] * nc,
        out_shape=[jax.ShapeDtypeStruct((M, N), out_dtype)] + (list(comm.out_shape) if comm else []),
        scratch_shapes=(list(comm.scratch) if comm else []) + ([pltpu.VMEM((tm, tn), f32)] if nk > 1 else []),
        compiler_params=_cparams(3),
    )(a, b, *(comm.operands if comm else []))
    return (res[0], res[1:]) if comm else res[0]


def _row_spec(op, tm, ncol):
    arr, off, width, tiled = op
    if tiled:
        cw = width // ncol
        return pl.BlockSpec((tm, cw), lambda j, i: (i, off // cw + j))
    return pl.BlockSpec((tm, width), lambda j, i: (i, off // width))


def _param_spec(p, ncol):
    arr, tiled = p
    rows, width = arr.shape
    if tiled:
        return pl.BlockSpec((rows, width // ncol), lambda j, i: (0, j))
    return pl.BlockSpec((rows, width), lambda j, i: (0, 0))


def _row_fwd(name, fn, params, rows, outs, T, tm, ncol=1):
    npar, nrow = len(params), len(rows)

    def body(*refs):
        pv = [r[...] for r in refs[:npar]]
        rv = [r[...].astype(f32) for r in refs[npar:npar + nrow]]
        res = fn(*pv, *rv)
        for o_ref, val in zip(refs[npar + nrow:], res):
            o_ref[...] = val.astype(o_ref.dtype)

    return pl.pallas_call(
        body, name=name, grid=(ncol, T // tm),
        in_specs=[_param_spec(p, ncol) for p in params] + [_row_spec(r, tm, ncol) for r in rows],
        out_specs=[pl.BlockSpec((tm, w // ncol), lambda j, i: (i, j)) for w, _ in outs],
        out_shape=[jax.ShapeDtypeStruct((T, w), dt) for w, dt in outs],
        compiler_params=_cparams(2),
    )(*[p[0] for p in params], *[r[0] for r in rows])


def _row_bwd(name, fn, params, rows, cots, row_grads, T, tm, ncol=1):
    npar, nrow, ncot = len(params), len(rows), len(cots)
    want = [k for k, dt in enumerate(row_grads) if dt is not None]

    def body(*refs):
        pv = [r[...] for r in refs[:npar]]
        rv = [r[...].astype(f32) for r in refs[npar:npar + nrow]]
        cv = tuple(r[...].astype(f32) for r in refs[npar + nrow:npar + nrow + ncot])
        out_refs = refs[npar + nrow + ncot:]
        _, vjp = jax.vjp(fn, *pv, *rv)
        grads = vjp(cv)
        for o_ref, k in zip(out_refs[:len(want)], want):
            o_ref[...] = grads[npar + k].astype(o_ref.dtype)
        j, i = pl.program_id(0), pl.program_id(1)
        for o_ref, p, g in zip(out_refs[len(want):], params, grads[:npar]):
            first = (i == 0) if p[1] else jnp.logical_and(i == 0, j == 0)

            @pl.when(first)
            def _():
                o_ref[...] = jnp.zeros_like(o_ref)

            o_ref[...] += g

    def grad_spec(op):
        arr, off, width, tiled = op
        if tiled:
            return pl.BlockSpec((tm, width // ncol), lambda j, i: (i, j)), (T, width)
        return pl.BlockSpec((tm, width), lambda j, i: (i, j)), (T, width * ncol)

    gspecs = [grad_spec(rows[k]) for k in want]
    return pl.pallas_call(
        body, name=name, grid=(ncol, T // tm),
        in_specs=[_param_spec(p, ncol) for p in params] + [_row_spec(r, tm, ncol) for r in rows]
        + [_row_spec(c, tm, ncol) for c in cots],
        out_specs=[s for s, _ in gspecs] + [_param_spec(p, ncol) for p in params],
        out_shape=[jax.ShapeDtypeStruct(shp, row_grads[k]) for (_, shp), k in zip(gspecs, want)]
        + [jax.ShapeDtypeStruct(p[0].shape, f32) for p in params],
        compiler_params=_cparams(2),
    )(*[p[0] for p in params], *[r[0] for r in rows], *[c[0] for c in cots])


def _seg_sum(x, head):
    li = lax.broadcasted_iota(jnp.int32, (LANES, LANES), 0) // head
    lj = lax.broadcasted_iota(jnp.int32, (LANES, LANES), 1) // head
    q = (li == lj).astype(f32)
    parts = [_mm(x[:, s:s + LANES], q, lax.Precision.HIGH) for s in range(0, x.shape[1], LANES)]
    return parts[0] if len(parts) == 1 else jnp.concatenate(parts, axis=1)


def _sigmoid(z):
    return 1.0 / (1.0 + jnp.exp(-z))


def _silu(z):
    return z * _sigmoid(z)


def _rms(g, x):
    return x * lax.rsqrt(jnp.mean(x * x, axis=-1, keepdims=True) + NORM_EPS) * g


def _fn_rms_pre(g, x):
    return (_rms(g, x),)


def _fn_lora(w0, wup, a0, aup, xl):
    qw = w0 + _bmm(jnp.tanh(xl), wup)
    qa = a0 + _bmm(xl, aup)
    return qw, qa


def _fn_rwkv_pre(head, k_k, k_a, xk, qw, qa):
    w_log = -(jnp.maximum(-qw, 0.0) + jnp.log(1.0 + jnp.exp(-jnp.abs(qw)))) - 0.5
    lw = -jnp.exp(w_log)
    a_sig = _sigmoid(qa)
    kk = xk * k_k
    kk = kk / jnp.maximum(jnp.sqrt(_seg_sum(kk * kk, head)), 1e-12)
    k_h = xk * (1.0 + (a_sig - 1.0) * k_a)
    return lw, k_h, -kk, kk * a_sig


def _fn_rwkv_post(head, lnx_g, lnx_b, r_k, y, r, k_h, v, g):
    inv = 1.0 / head
    mu = _seg_sum(y, head) * inv
    d = y - mu
    var = _seg_sum(d * d, head) * inv
    yn = d * lax.rsqrt(var + 1e-5 * head) * lnx_g + lnx_b
    bonus = _seg_sum(r * k_h * r_k, head) * v
    return ((yn + bonus) * _silu(g),)


def _fn_conv_ln(cln_g, cln_b, c):
    mu = jnp.mean(c, axis=-1, keepdims=True)
    d = c - mu
    var = jnp.mean(d * d, axis=-1, keepdims=True)
    return (_silu(d * lax.rsqrt(var + LN_EPS) * cln_g + cln_b),)


def _fn_conv_post(b_pw2, c2, g):
    return ((c2 + b_pw2) * _silu(g),)


def _post(out, x, tgt, g, T, D, tm):
    def body(g_ref, o_ref, x_ref, t_ref, dout_ref, gx_ref, loss_ref, dg_ref):
        i = pl.program_id(0)
        o, vjp = jax.vjp(_rms, g_ref[...], o_ref[...])
        err = x_ref[...] + o - t_ref[...]
        d_y = err * (1.0 / D)
        dg, d_out = vjp(d_y)
        dout_ref[...] = d_out.astype(dout_ref.dtype)
        gx_ref[...] = d_y

        @pl.when(i == 0)
        def _():
            loss_ref[...] = jnp.zeros_like(loss_ref)
            dg_ref[...] = jnp.zeros_like(dg_ref)

        loss_ref[...] += jnp.sum(err * err, keepdims=True) * (0.5 / D)
        dg_ref[...] += dg

    row = pl.BlockSpec((tm, D), lambda i: (i, 0))
    vec = pl.BlockSpec((1, D), lambda i: (0, 0))
    return pl.pallas_call(
        body, name="post_loss", grid=(T // tm,),
        in_specs=[vec, row, row, row],
        out_specs=[row, row, pl.BlockSpec((1, 1), lambda i: (0, 0)), vec],
        out_shape=[jax.ShapeDtypeStruct((T, D), bf16), jax.ShapeDtypeStruct((T, D), f32),
                   jax.ShapeDtypeStruct((1, 1), f32), jax.ShapeDtypeStruct((1, D), f32)],
        compiler_params=_cparams(1),
    )(g, out, x, tgt)


def _rms_pre_bwd(x, g, dh, gx_res, T, D, tm):
    def body(g_ref, x_ref, dh_ref, res_ref, dx_ref, dg_ref):
        i = pl.program_id(0)
        _, vjp = jax.vjp(_rms, g_ref[...], x_ref[...])
        dg, dx = vjp(dh_ref[...].astype(f32))
        dx_ref[...] = dx + res_ref[...]

        @pl.when(i == 0)
        def _():
            dg_ref[...] = jnp.zeros_like(dg_ref)

        dg_ref[...] += dg

    row = pl.BlockSpec((tm, D), lambda i: (i, 0))
    vec = pl.BlockSpec((1, D), lambda i: (0, 0))
    return pl.pallas_call(
        body, name="rms_pre_bwd", grid=(T // tm,),
        in_specs=[vec, row, row, row], out_specs=[row, vec],
        out_shape=[jax.ShapeDtypeStruct((T, D), f32), jax.ShapeDtypeStruct((1, D), f32)],
        compiler_params=_cparams(1),
    )(g, x, dh, gx_res)


def _prev_rows(cur, halo_ref, first):
    top = jnp.where(first, 0.0, halo_ref[SUBLANES - 1:SUBLANES, :])
    rolled = pltpu.roll(cur, 1, 0)
    rid = lax.broadcasted_iota(jnp.int32, cur.shape, 0)
    return jnp.where(rid == 0, top, rolled)


def _shift_fwd(name, proj, off, width, mu, T, tm):
    cw = _tile(width, 512)
    ncol, cb = width // cw, off // cw
    hb = tm // SUBLANES

    def body(mu_ref, cur_ref, halo_ref, o_ref):
        i = pl.program_id(1)
        cur = cur_ref[...]
        prev = _prev_rows(cur, halo_ref, i == 0)
        o_ref[...] = cur + (prev - cur) * mu_ref[...]

    return pl.pallas_call(
        body, name=name, grid=(ncol, T // tm),
        in_specs=[pl.BlockSpec((1, cw), lambda j, i: (0, j)),
                  pl.BlockSpec((tm, cw), lambda j, i: (i, cb + j)),
                  pl.BlockSpec((SUBLANES, cw), lambda j, i: (jnp.maximum(i * hb - 1, 0), cb + j))],
        out_specs=pl.BlockSpec((tm, cw), lambda j, i: (i, j)),
        out_shape=jax.ShapeDtypeStruct((T, width), f32),
        compiler_params=_cparams(2),
    )(mu, proj, proj)


def _shift_bwd(name, proj, off, width, mu, dxs, T, tm):
    cw = _tile(width, 512)
    ncol, cb = width // cw, off // cw
    hb, nt = tm // SUBLANES, T // tm

    def body(mu_ref, cur_ref, halo_ref, d_ref, dnext_ref, o_ref, dmu_ref):
        i = pl.program_id(1)
        cur = cur_ref[...]
        prev = _prev_rows(cur, halo_ref, i == 0)
        d = d_ref[...]
        bottom = jnp.where(i == nt - 1, 0.0, dnext_ref[0:1, :])
        rid = lax.broadcasted_iota(jnp.int32, d.shape, 0)
        d_next = jnp.where(rid == tm - 1, bottom, pltpu.roll(d, tm - 1, 0))
        mu_v = mu_ref[...]
        o_ref[...] = (d * (1.0 - mu_v) + d_next * mu_v).astype(o_ref.dtype)

        @pl.when(i == 0)
        def _():
            dmu_ref[...] = jnp.zeros_like(dmu_ref)

        dmu_ref[...] += jnp.sum(d * (prev - cur), axis=0, keepdims=True)

    return pl.pallas_call(
        body, name=name, grid=(ncol, nt),
        in_specs=[pl.BlockSpec((1, cw), lambda j, i: (0, j)),
                  pl.BlockSpec((tm, cw), lambda j, i: (i, cb + j)),
                  pl.BlockSpec((SUBLANES, cw), lambda j, i: (jnp.maximum(i * hb - 1, 0), cb + j)),
                  pl.BlockSpec((tm, cw), lambda j, i: (i, j)),
                  pl.BlockSpec((SUBLANES, cw), lambda j, i: (jnp.minimum((i + 1) * hb, nt * hb - 1), j))],
        out_specs=[pl.BlockSpec((tm, cw), lambda j, i: (i, j)), pl.BlockSpec((1, cw), lambda j, i: (0, j))],
        out_shape=[jax.ShapeDtypeStruct((T, width), bf16), jax.ShapeDtypeStruct((1, width), f32)],
        compiler_params=_cparams(2),
    )(mu, proj, proj, dxs, dxs)


def _rolled_copies(dst_ref, ext):
    n = ext.shape[0]
    dst_ref[0] = ext
    for r in range(1, SUBLANES):
        dst_ref[r] = pltpu.roll(ext, n - r, 0)


def _window(rolled_ref, start, rows):
    q, r = divmod(start, SUBLANES)
    return rolled_ref[r, pl.ds(SUBLANES * q, rows), :]


def _conv_fwd(proj, off_v, off_g, width, conv_w, conv_b, ktaps, T, tm):
    cw = _tile(width, 512)
    ncol = width // cw
    hb = tm // CONV_HALO
    lead = CONV_HALO - (ktaps - 1)

    def body(w_ref, b_ref, v_ref, g_ref, hv_ref, hg_ref, o_ref, u_ref):
        i = pl.program_id(1)
        halo = hv_ref[...] * _sigmoid(hg_ref[...])
        _rolled_copies(u_ref, jnp.concatenate([jnp.where(i == 0, 0.0, halo), v_ref[...] * _sigmoid(g_ref[...])], axis=0))
        acc = jnp.zeros((tm, cw), f32) + b_ref[...]
        for j in range(ktaps):
            acc = acc + _window(u_ref, lead + j, tm) * w_ref[j:j + 1, :]
        o_ref[...] = acc

    def tile(off):
        return pl.BlockSpec((tm, cw), lambda j, i: (i, off // cw + j))

    def halo(off):
        return pl.BlockSpec((CONV_HALO, cw), lambda j, i: (jnp.maximum(i * hb - 1, 0), off // cw + j))

    return pl.pallas_call(
        body, name="conv_fwd", grid=(ncol, T // tm),
        in_specs=[pl.BlockSpec((CONV_HALO, cw), lambda j, i: (0, j)), pl.BlockSpec((1, cw), lambda j, i: (0, j)),
                  tile(off_v), tile(off_g), halo(off_v), halo(off_g)],
        out_specs=pl.BlockSpec((tm, cw), lambda j, i: (i, j)),
        out_shape=jax.ShapeDtypeStruct((T, width), f32),
        scratch_shapes=[pltpu.VMEM((SUBLANES, CONV_HALO + tm, cw), f32)],
        compiler_params=_cparams(2),
    )(conv_w, conv_b, proj, proj, proj, proj)


def _conv_bwd(proj, off_v, off_g, width, conv_w, dc, ktaps, T, tm):
    cw = _tile(width, 512)
    ncol = width // cw
    hb, nt = tm // CONV_HALO, T // tm
    lead = CONV_HALO - (ktaps - 1)

    def body(w_ref, v_ref, g_ref, hv_ref, hg_ref, dc_ref, dcn_ref, dv_ref, dg_ref, dw_ref, db_ref, u_ref, d_ref):
        i = pl.program_id(1)
        halo = hv_ref[...] * _sigmoid(hg_ref[...])
        sig = _sigmoid(g_ref[...])
        gv = v_ref[...]
        _rolled_copies(u_ref, jnp.concatenate([jnp.where(i == 0, 0.0, halo), gv * sig], axis=0))
        dcur = dc_ref[...]
        _rolled_copies(d_ref, jnp.concatenate([dcur, jnp.where(i == nt - 1, 0.0, dcn_ref[...])], axis=0))

        @pl.when(i == 0)
        def _():
            dw_ref[...] = jnp.zeros_like(dw_ref)
            db_ref[...] = jnp.zeros_like(db_ref)

        du = jnp.zeros((tm, cw), f32)
        for j in range(ktaps):
            du = du + _window(d_ref, ktaps - 1 - j, tm) * w_ref[j:j + 1, :]
            dw_ref[j:j + 1, :] += jnp.sum(_window(u_ref, lead + j, tm) * dcur, axis=0, keepdims=True)
        db_ref[...] += jnp.sum(dcur, axis=0, keepdims=True)
        dv_ref[...] = (du * sig).astype(dv_ref.dtype)
        dg_ref[...] = (du * gv * sig * (1.0 - sig)).astype(dg_ref.dtype)

    def tile(off):
        return pl.BlockSpec((tm, cw), lambda j, i: (i, off // cw + j))

    def halo(off):
        return pl.BlockSpec((CONV_HALO, cw), lambda j, i: (jnp.maximum(i * hb - 1, 0), off // cw + j))

    return pl.pallas_call(
        body, name="conv_bwd", grid=(ncol, nt),
        in_specs=[pl.BlockSpec((CONV_HALO, cw), lambda j, i: (0, j)),
                  tile(off_v), tile(off_g), halo(off_v), halo(off_g),
                  pl.BlockSpec((tm, cw), lambda j, i: (i, j)),
                  pl.BlockSpec((CONV_HALO, cw), lambda j, i: (jnp.minimum((i + 1) * hb, nt * hb - 1), j))],
        out_specs=[pl.BlockSpec((tm, cw), lambda j, i: (i, j)), pl.BlockSpec((tm, cw), lambda j, i: (i, j)),
                   pl.BlockSpec((CONV_HALO, cw), lambda j, i: (0, j)), pl.BlockSpec((1, cw), lambda j, i: (0, j))],
        out_shape=[jax.ShapeDtypeStruct((T, width), bf16), jax.ShapeDtypeStruct((T, width), bf16),
                   jax.ShapeDtypeStruct((CONV_HALO, width), f32), jax.ShapeDtypeStruct((1, width), f32)],
        scratch_shapes=[pltpu.VMEM((SUBLANES, CONV_HALO + tm, cw), f32), pltpu.VMEM((SUBLANES, tm + CONV_HALO, cw), f32)],
        compiler_params=_cparams(2),
    )(conv_w, proj, proj, proj, proj, dc, dc)


def _each(f, *lists):
    return [f(*xs) for xs in zip(*lists)]


def _wkv_local(r, lw, k, v, a, b):
    C = r[0].shape[0]
    P = WKV_PREC
    row = lax.broadcasted_iota(jnp.int32, (C, C), 0)
    col = lax.broadcasted_iota(jnp.int32, (C, C), 1)
    incl, strict = row >= col, row > col
    tri = incl.astype(f32)
    zero = jnp.zeros((C, C), f32)
    G = _each(lambda x: _mm(tri, x, P), lw)
    to_end = _each(lambda x, g: jnp.exp(jnp.sum(x, axis=0, keepdims=True) - g), lw, G)
    e_g = _each(jnp.exp, G)
    e_ng = _each(lambda g: jnp.exp(-g), G)
    At = _each(lambda x, g, w: x * jnp.exp(g - w), a, G, lw)
    Rt = _each(jnp.multiply, r, e_g)
    Kt = _each(jnp.multiply, k, e_ng)
    Bt = _each(jnp.multiply, b, e_ng)
    sc = _each(lambda at, rt, bt, kt: _mm_nt(jnp.concatenate([at, rt], axis=0), jnp.concatenate([bt, kt], axis=0), P),
               At, Rt, Bt, Kt)
    L = _each(lambda s: jnp.where(strict, s[:C, :C], zero), sc)
    M = _each(lambda s: jnp.where(strict, s[:C, C:], zero), sc)
    Pb = _each(lambda s: jnp.where(incl, s[C:, :C], zero), sc)
    Pk = _each(lambda s: jnp.where(incl, s[C:, C:], zero), sc)
    MPk = _each(lambda m, p, x: _bmm(jnp.concatenate([m, p], axis=0), x), M, Pk, v)
    WU = _each(lambda at, mp: jnp.concatenate([at, mp[:C]], axis=1), At, MPk)
    Lp = L
    n = 1
    while n < C:
        n *= 2
        if n < C:
            step = _each(lambda l, x: _bmm(l, jnp.concatenate([x, l], axis=1)), Lp, WU)
            WU = _each(lambda x, s: x + s[:, :x.shape[1]], WU, step)
            Lp = _each(lambda x, s: s[:, x.shape[1]:], WU, step)
        else:
            WU = _each(lambda x, l: x + _bmm(l, x), WU, Lp)
    N = r[0].shape[1]
    W = _each(lambda x: x[:, :N], WU)
    U = _each(lambda x: x[:, N:], WU)
    Y0 = _each(lambda mp: mp[C:], MPk)
    Bend = _each(jnp.multiply, b, to_end)
    Z = _each(lambda x, y, e: _mm_tn(x, y * e, P), v, k, to_end)
    return W, U, Rt, Pb, Y0, Bend, Z


def _wkv_state(S0, W, U, Rt, Pb, Bend, lw, Y0, Z):
    P = WKV_PREC
    C = W[0].shape[0]
    WR = _each(lambda w, rt, s: _mm_nt(jnp.concatenate([w, rt], axis=0), s, P), W, Rt, S0)
    X = _each(lambda wr, u: wr[:C] + u, WR, U)
    y = _each(lambda p, x, wr, c: _mm(p, x, P) + wr[C:] + c, Pb, X, WR, Y0)
    S1 = _each(lambda s, w, x, e, z: s * jnp.exp(jnp.sum(w, axis=0, keepdims=True)) + _mm_tn(x, e, P) + z,
               S0, lw, X, Bend, Z)
    return y, S1


def _wkv_dims(head, T, RW, heads_per_step):
    C = min(WKV_CHUNK, T)
    nh = RW // head
    hb = min(heads_per_step, nh)
    return C, nh, hb, hb * head, T // C


def _heads(ref, hb, head):
    return [ref[:, h * head:(h + 1) * head] for h in range(hb)]


def _put_heads(ref, vals, head):
    for h, val in enumerate(vals):
        ref[:, h * head:(h + 1) * head] = val


def _wkv_local_fwd(r, lw, k, v, a, b, head, T, RW):
    C, nh, hb, bw, nc = _wkv_dims(head, T, RW, WKV_HEADS)

    def body(*refs):
        ins, outs = refs[:6], refs[6:]
        res = _wkv_local(*[_heads(x, hb, head) for x in ins])
        for o_ref, vals in zip(outs[:6], res[:6]):
            _put_heads(o_ref, vals, head)
        for h in range(hb):
            outs[6][0, h] = res[6][h]

    blk = pl.BlockSpec((C, bw), lambda g, c: (c, g))
    sq = pl.BlockSpec((1, hb, head, head), lambda g, c: (c, g, 0, 0))
    return pl.pallas_call(
        body, name="wkv_local", grid=(nh // hb, nc),
        in_specs=[blk] * 6, out_specs=[blk] * 6 + [sq],
        out_shape=[jax.ShapeDtypeStruct((T, RW), f32)] * 6 + [jax.ShapeDtypeStruct((nc, nh, head, head), f32)],
        compiler_params=_cparams(2),
    )(r, lw, k, v, a, b)


def _wkv_state_fwd(W, U, Rt, Pb, Bend, lw, Y0, Z, head, T, RW):
    C, nh, hb, bw, nc = _wkv_dims(head, T, RW, WKV_STATE_HEADS)

    def body(w_ref, u_ref, rt_ref, pb_ref, be_ref, lw_ref, y0_ref, z_ref, y_ref, st_ref, s_ref):
        @pl.when(pl.program_id(1) == 0)
        def _():
            s_ref[...] = jnp.zeros_like(s_ref)

        S0 = [s_ref[h] for h in range(hb)]
        for h in range(hb):
            st_ref[0, h] = S0[h]
        rows = [_heads(x, hb, head) for x in (w_ref, u_ref, rt_ref, pb_ref, be_ref, lw_ref, y0_ref)]
        y, S1 = _wkv_state(S0, *rows, [z_ref[0, h] for h in range(hb)])
        _put_heads(y_ref, y, head)
        for h in range(hb):
            s_ref[h] = S1[h]

    blk = pl.BlockSpec((C, bw), lambda g, c: (c, g))
    sq = pl.BlockSpec((1, hb, head, head), lambda g, c: (c, g, 0, 0))
    return pl.pallas_call(
        body, name="wkv_state", grid=(nh // hb, nc),
        in_specs=[blk] * 7 + [sq], out_specs=[blk, sq],
        out_shape=[jax.ShapeDtypeStruct((T, RW), f32), jax.ShapeDtypeStruct((nc, nh, head, head), f32)],
        scratch_shapes=[pltpu.VMEM((hb, head, head), f32)],
        compiler_params=_cparams(2),
    )(W, U, Rt, Pb, Bend, lw, Y0, Z)


def _wkv_state_bwd(W, U, Rt, Pb, Bend, lw, Y0, Z, states, dy, head, T, RW):
    C, nh, hb, bw, nc = _wkv_dims(head, T, RW, WKV_STATE_HEADS)

    def body(w_ref, u_ref, rt_ref, pb_ref, be_ref, lw_ref, y0_ref, z_ref, st_ref, dy_ref,
             dw_ref, du_ref, drt_ref, dpb_ref, dbe_ref, dlw_ref, dz_ref, ds_ref):
        @pl.when(pl.program_id(1) == 0)
        def _():
            ds_ref[...] = jnp.zeros_like(ds_ref)

        dS1 = [ds_ref[h] for h in range(hb)]
        for h in range(hb):
            dz_ref[0, h] = dS1[h]
        rows = [_heads(x, hb, head) for x in (w_ref, u_ref, rt_ref, pb_ref, be_ref, lw_ref)]
        Y0 = _heads(y0_ref, hb, head)
        Zs = [z_ref[0, h] for h in range(hb)]
        _, vjp = jax.vjp(lambda s0, *rw: _wkv_state(s0, *rw, Y0, Zs), [st_ref[0, h] for h in range(hb)], *rows)
        grads = vjp((_heads(dy_ref, hb, head), dS1))
        for o_ref, vals in zip((dw_ref, du_ref, drt_ref, dpb_ref, dbe_ref, dlw_ref), grads[1:]):
            _put_heads(o_ref, vals, head)
        for h in range(hb):
            ds_ref[h] = grads[0][h]

    blk = pl.BlockSpec((C, bw), lambda g, c: (nc - 1 - c, g))
    sq = pl.BlockSpec((1, hb, head, head), lambda g, c: (nc - 1 - c, g, 0, 0))
    return pl.pallas_call(
        body, name="wkv_state_bwd", grid=(nh // hb, nc),
        in_specs=[blk] * 7 + [sq, sq, blk], out_specs=[blk] * 6 + [sq],
        out_shape=[jax.ShapeDtypeStruct((T, RW), f32)] * 6 + [jax.ShapeDtypeStruct((nc, nh, head, head), f32)],
        scratch_shapes=[pltpu.VMEM((hb, head, head), f32)],
        compiler_params=_cparams(2),
    )(W, U, Rt, Pb, Bend, lw, Y0, Z, states, dy)


def _wkv_local_bwd(r, lw, k, v, a, b, cots, d_lw_x, dr_x, dk_x, dv_x, head, T, RW):
    C, nh, hb, bw, nc = _wkv_dims(head, T, RW, WKV_HEADS)

    def body(*refs):
        ins, cot_refs, add_refs, outs = refs[:6], refs[6:13], refs[13:17], refs[17:]
        _, vjp = jax.vjp(_wkv_local, *[_heads(x, hb, head) for x in ins])
        cts = [_heads(x, hb, head) for x in cot_refs[:6]] + [[cot_refs[6][0, h] for h in range(hb)]]
        dr, dlw, dk, dv, da, db = vjp(tuple(cts))
        dlw_x, drx, dkx, dvx = [_heads(x, hb, head) for x in add_refs]
        _put_heads(outs[0], _each(jnp.add, dr, drx), head)
        _put_heads(outs[1], _each(jnp.add, dlw, dlw_x), head)
        _put_heads(outs[2], _each(jnp.add, dk, dkx), head)
        _put_heads(outs[3], _each(jnp.add, dv, dvx), head)
        _put_heads(outs[4], da, head)
        _put_heads(outs[5], db, head)

    blk = pl.BlockSpec((C, bw), lambda g, c: (c, g))
    sq = pl.BlockSpec((1, hb, head, head), lambda g, c: (c, g, 0, 0))
    return pl.pallas_call(
        body, name="wkv_local_bwd", grid=(nh // hb, nc),
        in_specs=[blk] * 12 + [sq] + [blk] * 4, out_specs=[blk] * 6,
        out_shape=[jax.ShapeDtypeStruct((T, RW), f32)] * 6,
        compiler_params=_cparams(2),
    )(r, lw, k, v, a, b, *cots, d_lw_x, dr_x, dk_x, dv_x)


def _rows_tile(R, row_bytes, budget, mult=SUBLANES):
    best = None
    t = mult
    while t <= R:
        if R % t == 0 and t * row_bytes <= budget:
            best = t
        t += mult
    return best if best is not None else R


def _sum_slots(name, parts):
    S, R, W = parts.shape
    budget = 4 << 20
    tr = _rows_tile(R, S * W * 4, budget, 2 * SUBLANES)
    cw = W if tr * S * W * 4 <= 2 * budget else _tile(W, max(LANES, 2 * budget // (S * tr * 4)))

    def body(p_ref, o_ref):
        acc = p_ref[0].astype(f32)
        for d in range(1, S):
            acc = acc + p_ref[d].astype(f32)
        o_ref[...] = acc

    return pl.pallas_call(
        body, name=name, grid=(R // tr, W // cw),
        in_specs=[pl.BlockSpec((S, tr, cw), lambda i, j: (0, i, j))],
        out_specs=pl.BlockSpec((tr, cw), lambda i, j: (i, j)),
        out_shape=jax.ShapeDtypeStruct((R, W), f32),
        compiler_params=_cparams(2),
    )(parts)


def _add_pair(name, a, b):
    R, W = a.shape
    tr = _rows_tile(R, W * 4, 2 << 20, 2 * SUBLANES)

    def body(a_ref, b_ref, o_ref):
        o_ref[...] = (a_ref[...].astype(f32) + b_ref[...].astype(f32)).astype(o_ref.dtype)

    blk = pl.BlockSpec((tr, W), lambda i: (i, 0))
    return pl.pallas_call(
        body, name=name, grid=(R // tr,), in_specs=[blk, blk], out_specs=blk,
        out_shape=jax.ShapeDtypeStruct((R, W), a.dtype), compiler_params=_cparams(1),
    )(a, b)


def _adamw(name, w, g, m, v):
    R, W = w.shape
    tr = _rows_tile(R, W * 4, 1 << 20)

    def body(w_ref, g_ref, m_ref, v_ref, d_ref, nm_ref, nv_ref):
        g_v = g_ref[...]
        nm = ADAM_B1 * m_ref[...] + (1.0 - ADAM_B1) * g_v
        nv = ADAM_B2 * v_ref[...] + (1.0 - ADAM_B2) * (g_v * g_v)
        m_hat = nm / (1.0 - ADAM_B1 ** ADAM_STEP)
        v_hat = nv / (1.0 - ADAM_B2 ** ADAM_STEP)
        d_ref[...] = -ADAM_LR * (m_hat / (jnp.sqrt(v_hat) + ADAM_EPS) + ADAM_WD * w_ref[...])
        nm_ref[...] = nm
        nv_ref[...] = nv

    blk = pl.BlockSpec((tr, W), lambda i: (i, 0))
    return pl.pallas_call(
        body, name=name, grid=(R // tr,),
        in_specs=[blk] * 4, out_specs=[blk] * 3,
        out_shape=[jax.ShapeDtypeStruct((R, W), f32)] * 3,
        compiler_params=_cparams(1),
    )(w, g, m, v)


ANY = pl.BlockSpec(memory_space=pl.ANY)


def _place():
    return lax.axis_index("x"), lax.axis_index("y"), lax.axis_index("c")


class _Comm:
    def __init__(self, operands, out_shape, scratch, start, wait):
        self.operands, self.out_shape, self.scratch, self.start, self.wait = operands, out_shape, scratch, start, wait


def _run_comm(name, comm):
    n = len(comm.operands)

    def body(*refs):
        parts = (refs[:n], refs[n:2 * n], refs[2 * n:])
        comm.start(*parts)
        comm.wait(*parts)

    return pl.pallas_call(
        body, name=name, in_specs=[ANY] * n, out_specs=[ANY] * n, out_shape=comm.out_shape,
        scratch_shapes=comm.scratch,
    )(*comm.operands)


def _copy_chunks(rows, cols):
    k = SHARE_CHUNKS // 2
    if rows % (k * 2 * SUBLANES) == 0:
        return [(pl.ds(q * (rows // k), rows // k), pl.ds(0, cols)) for q in range(k)]
    if cols % (k * LANES) == 0:
        return [(pl.ds(0, rows), pl.ds(q * (cols // k), cols // k)) for q in range(k)]
    return [(pl.ds(0, rows), pl.ds(0, cols))]


def _gather_chips(arrays):
    n = len(arrays)
    parts = [(a, h, blk) for a, arr in enumerate(arrays) for h in range(2) for blk in _copy_chunks(*arr.shape[1:])]

    def copies(ins, outs, sems):
        send_sems, recv_sems, local_sems = sems
        x, y, c = _place()
        mine = 2 * x + y
        sib = (x, y, 1 - c)
        chips = [(1 - x, y), (x, 1 - y), (1 - x, 1 - y)]
        local = [pltpu.make_async_copy(ins[a].at[(h, *blk)], outs[a].at[(mine, h, *blk)], local_sems.at[p])
                 for p, (a, h, blk) in enumerate(parts)]

        def over_ici(a, j, slot):
            px, py = chips[j]
            return pltpu.make_async_remote_copy(
                src_ref=ins[a].at[c], dst_ref=outs[a].at[slot, c], send_sem=send_sems.at[3 * a + j],
                recv_sem=recv_sems.at[3 * a + j], device_id=(px, py, c), device_id_type=MESH)

        def over_d2d(a, j, half):
            px, py = chips[j]
            slot = 2 * px + py
            return pltpu.make_async_remote_copy(
                src_ref=outs[a].at[slot, half], dst_ref=outs[a].at[slot, half], send_sem=send_sems.at[3 * (n + a) + j],
                recv_sem=recv_sems.at[3 * (n + a) + j], device_id=sib, device_id_type=MESH)

        pairs = [(a, j) for a in range(n) for j in range(3)]
        sends = [over_ici(a, j, mine) for a, j in pairs]
        landing = [over_ici(a, j, 2 * chips[j][0] + chips[j][1]) for a, j in pairs]
        passed = [over_d2d(a, j, c) for a, j in pairs]
        from_sib = [over_d2d(a, j, 1 - c) for a, j in pairs]
        return local, sends, landing, passed, from_sib

    def start(ins, outs, sems):
        local, sends, _, _, _ = copies(ins, outs, sems)
        for cp in local + sends:
            cp.start()

    def wait(ins, outs, sems):
        local, sends, landing, passed, from_sib = copies(ins, outs, sems)
        for got, on in zip(landing, passed):
            got.wait_recv()
            on.start()
        for cp in from_sib:
            cp.wait_recv()
        for cp in sends + passed:
            cp.wait_send()
        for cp in local:
            cp.wait()

    return _Comm(arrays, [jax.ShapeDtypeStruct((4,) + a.shape, a.dtype) for a in arrays],
                 [pltpu.SemaphoreType.DMA((6 * n,)), pltpu.SemaphoreType.DMA((6 * n,)),
                  pltpu.SemaphoreType.DMA((len(parts),))], start, wait)


def _exchange_chips(pieces, whole):
    n, m = len(pieces), len(whole)
    parts = [(a, blk) for a, arr in enumerate(pieces) for blk in _copy_chunks(*arr.shape[1:])]

    def copies(ins, outs, sems):
        send_sems, recv_sems, local_sems = sems
        x, y, c = _place()
        chip, dev = 2 * x + y, 4 * x + 2 * y + c
        chips = [(1 - x, y), (x, 1 - y), (1 - x, 1 - y)]
        peers = [(x ^ (k >> 2), y ^ ((k >> 1) & 1), c ^ (k & 1)) for k in range(1, 8)]
        local = [pltpu.make_async_copy(ins[a].at[(chip, *blk)], outs[a].at[(chip, *blk)], local_sems.at[p])
                 for p, (a, blk) in enumerate(parts)]
        local += [pltpu.make_async_copy(ins[n + b], outs[n + b].at[dev], local_sems.at[len(parts) + b]) for b in range(m)]

        def piece(a, j, slot_from):
            px, py = chips[j]
            return pltpu.make_async_remote_copy(
                src_ref=ins[a].at[2 * px + py], dst_ref=outs[a].at[slot_from], send_sem=send_sems.at[3 * a + j],
                recv_sem=recv_sems.at[3 * a + j], device_id=(px, py, c), device_id_type=MESH)

        def everyone(b, j, slot_from):
            px, py, pc = peers[j]
            return pltpu.make_async_remote_copy(
                src_ref=ins[n + b], dst_ref=outs[n + b].at[slot_from], send_sem=send_sems.at[3 * n + 7 * b + j],
                recv_sem=recv_sems.at[3 * n + 7 * b + j], device_id=(px, py, pc), device_id_type=MESH)

        sends = [everyone(b, j, dev) for b in range(m) for j in range(7)]
        sends += [piece(a, j, chip) for a in range(n) for j in range(3)]
        landing = [everyone(b, j, 4 * px + 2 * py + pc) for b in range(m) for j, (px, py, pc) in enumerate(peers)]
        landing += [piece(a, j, 2 * px + py) for a in range(n) for j, (px, py) in enumerate(chips)]
        return local, sends, landing

    def start(ins, outs, sems):
        local, sends, _ = copies(ins, outs, sems)
        for cp in local + sends:
            cp.start()

    def wait(ins, outs, sems):
        local, sends, landing = copies(ins, outs, sems)
        for cp in landing:
            cp.wait_recv()
        for cp in sends:
            cp.wait_send()
        for cp in local:
            cp.wait()

    shapes = [jax.ShapeDtypeStruct(a.shape, a.dtype) for a in pieces]
    shapes += [jax.ShapeDtypeStruct((8,) + a.shape, a.dtype) for a in whole]
    nsem = 3 * n + 7 * m
    return _Comm(list(pieces) + list(whole), shapes,
                 [pltpu.SemaphoreType.DMA((nsem,)), pltpu.SemaphoreType.DMA((nsem,)),
                  pltpu.SemaphoreType.DMA((len(parts) + m,))], start, wait)


def _share_sibling(name, arrays):
    n = len(arrays)
    parts = []
    for a, arr in enumerate(arrays):
        k = SHARE_CHUNKS if arr.shape[0] % (SHARE_CHUNKS * SUBLANES) == 0 else 1
        k = arr.shape[0] if arr.ndim == 3 else k
        step = arr.shape[0] // k
        parts += [(a, q * step, step) for q in range(k)]
    npart = len(parts)

    def body(*refs):
        ins, outs = refs[:n], refs[n:2 * n]
        send_sems, recv_sems = refs[2 * n:]
        x, y, c = _place()

        def copy(p):
            a, r0, nr = parts[p]
            return pltpu.make_async_remote_copy(
                src_ref=ins[a].at[pl.ds(r0, nr)], dst_ref=outs[a].at[pl.ds(r0, nr)], send_sem=send_sems.at[p],
                recv_sem=recv_sems.at[p], device_id=(x, y, 1 - c), device_id_type=MESH)

        copies = [copy(p) for p in range(npart)]
        for cp in copies:
            cp.start()
        for cp in copies:
            cp.wait_recv()
        for cp in copies:
            cp.wait_send()

    return pl.pallas_call(
        body, name=name,
        in_specs=[ANY] * n, out_specs=[ANY] * n,
        out_shape=[jax.ShapeDtypeStruct(a.shape, a.dtype) for a in arrays],
        scratch_shapes=[pltpu.SemaphoreType.DMA((npart,)), pltpu.SemaphoreType.DMA((npart,))],
    )(*arrays)


def _place_blocks(blocks, axis):
    shape = list(blocks[0].shape)
    shape[axis] = sum(b.shape[axis] for b in blocks)
    buf = lax.empty(tuple(shape), blocks[0].dtype)
    at = 0
    for b in blocks:
        buf = lax.dynamic_update_slice_in_dim(buf, b, at, axis)
        at += b.shape[axis]
    return buf


def kernel(x, norm_pre_g, w_in, mu_shift, w0, w_lora_up, a0, a_lora_up, k_k, k_a, r_k, lnx_g, lnx_b, conv_w, conv_b, cln_g, cln_b, w_pw2, b_pw2, w_out, norm_post_g, loss_target, m_norm_pre_g, m_w_in, m_mu_shift, m_w0, m_w_lora_up, m_a0, m_a_lora_up, m_k_k, m_k_a, m_r_k, m_lnx_g, m_lnx_b, m_conv_w, m_conv_b, m_cln_g, m_cln_b, m_w_pw2, m_b_pw2, m_w_out, m_norm_post_g, v_norm_pre_g, v_w_in, v_mu_shift, v_w0, v_w_lora_up, v_a0, v_a_lora_up, v_k_k, v_k_a, v_r_k, v_lnx_g, v_lnx_b, v_conv_w, v_conv_b, v_cln_g, v_cln_b, v_w_pw2, v_b_pw2, v_w_out, v_norm_post_g):
    _, T, D = x.shape
    RW = w0.shape[0]
    CW = conv_b.shape[0]
    head = r_k.shape[1]
    lora = w_lora_up.shape[0]
    ktaps = conv_w.shape[0]
    assert RW == CW and 2 * lora <= LORA_PAD and ktaps - 1 <= CONV_HALO
    n_in = 3 * RW + 2 * lora + RW + 3 * CW
    shard = n_in // 4
    PW = 7 * RW + LORA_PAD
    off_l = 7 * RW
    tm = min(256, T // 2)
    tm_wide = min(128, T // 2)
    row = lambda vec: vec.reshape(1, -1)
    x2, tgt2 = x[0], loss_target[0]

    halves = lambda a: a.reshape(2, a.shape[0] // 2, a.shape[1])
    conv_w_p = jnp.concatenate([conv_w, jnp.zeros((CONV_HALO - ktaps, CW // 4), f32)], axis=0)
    w_in_t, m_w_in_t, v_w_in_t = w_in.T, m_w_in.T, v_w_in.T
    (g_win,) = _run_comm("gather_w_in", _gather_chips([halves(w_in_t.astype(bf16))]))
    win_t = g_win.reshape(n_in, D)
    lo = 3 * RW
    wp_t = _place_blocks([win_t[:lo], win_t[lo + 2 * lora:], win_t[lo:lo + 2 * lora],
                          jnp.zeros((LORA_PAD - 2 * lora, D), bf16)], axis=0)
    npg = row(norm_pre_g)
    (h,) = _row_fwd("rms_pre", _fn_rms_pre, [(npg, False)], [(x2, 0, D, False)], [(D, bf16)], T, tm)
    others = [halves(a) for a in (w_lora_up, a_lora_up, conv_w_p, w_pw2.astype(bf16), w_out.astype(bf16))]
    proj, (g_wup, g_aup, g_cw, g_pw2, g_wout) = _matmul("proj", h, wp_t, "nt", f32, comm=_gather_chips(others))
    cat_cols = lambda g: jnp.concatenate([g[s].reshape(-1, g.shape[-1]) for s in range(4)], axis=1)
    wup_full, aup_full, cw_p = cat_cols(g_wup), cat_cols(g_aup), cat_cols(g_cw)
    zl = lambda n: jnp.zeros((n, RW), f32)
    wup_p = jnp.concatenate([wup_full, zl(LORA_PAD - lora)], axis=0)
    aup_p = jnp.concatenate([zl(lora), aup_full, zl(LORA_PAD - 2 * lora)], axis=0)
    pw2_full = g_pw2.reshape(CW, CW)
    wout_full = g_wout.reshape(RW + CW, D)
    mu_r, mu_k, mu_v = (row(mu_shift[s * RW:(s + 1) * RW]) for s in range(3))
    mu_l = row(jnp.concatenate([mu_shift[3 * RW:], jnp.zeros((LORA_PAD - 2 * lora,), f32)]))

    xs_r = _shift_fwd("shift_r", proj, 0, RW, mu_r, T, tm)
    xs_k = _shift_fwd("shift_k", proj, RW, RW, mu_k, T, tm)
    xs_v = _shift_fwd("shift_v", proj, 2 * RW, RW, mu_v, T, tm)
    xs_l = _shift_fwd("shift_l", proj, off_l, LORA_PAD, mu_l, T, tm)
    lora_params = [(row(w0), False), (wup_p, False), (row(a0), False), (aup_p, False)]
    qw, qa = _row_fwd("lora_up", _fn_lora, lora_params, [(xs_l, 0, LORA_PAD, False)], [(RW, f32), (RW, f32)], T, tm)
    ncol = RW // _tile(RW, 512)
    fn_pre = functools.partial(_fn_rwkv_pre, head)
    pre_params = [(row(k_k), True), (row(k_a), True)]
    pre_rows = [(xs_k, 0, RW, True), (qw, 0, RW, True), (qa, 0, RW, True)]
    lw, k_h, a_rec, b_rec = _row_fwd("rwkv_pre", fn_pre, pre_params, pre_rows, [(RW, f32)] * 4, T, tm, ncol)
    wkv_in = (xs_r, lw, k_h, xs_v, a_rec, b_rec)
    c_w, c_u, c_rt, c_pb, c_y0, c_bend, c_z = _wkv_local_fwd(*wkv_in, head, T, RW)
    wkv_loc = (c_w, c_u, c_rt, c_pb, c_bend, lw, c_y0, c_z)
    y_wkv, states = _wkv_state_fwd(*wkv_loc, head, T, RW)
    fn_post = functools.partial(_fn_rwkv_post, head)
    post_params = [(row(lnx_g), True), (row(lnx_b), True), (r_k.reshape(1, RW), True)]
    post_rows = [(y_wkv, 0, RW, True), (xs_r, 0, RW, True), (k_h, 0, RW, True), (xs_v, 0, RW, True),
                 (proj, 3 * RW, RW, True)]
    (y_rwkv,) = _row_fwd("rwkv_post", fn_post, post_params, post_rows, [(RW, bf16)], T, tm, ncol)

    c_pre = _conv_fwd(proj, 4 * RW, 5 * RW, CW, cw_p, row(conv_b), ktaps, T, tm)
    ln_params = [(row(cln_g), False), (row(cln_b), False)]
    (c_act,) = _row_fwd("conv_ln", _fn_conv_ln, ln_params, [(c_pre, 0, CW, False)], [(CW, bf16)], T, tm)
    c2 = _matmul("pw2", c_act, pw2_full, "nn", f32)
    cpost_params = [(row(b_pw2), True)]
    cpost_rows = [(c2, 0, CW, True), (proj, 6 * RW, CW, True)]
    (y_conv,) = _row_fwd("conv_post", _fn_conv_post, cpost_params, cpost_rows, [(CW, bf16)], T, tm, ncol)

    mix = jnp.concatenate([y_rwkv, y_conv], axis=1)
    out = _matmul("out_proj", mix, wout_full, "nn", f32)
    d_out, gx_res, loss_part, g_npost = _post(out, x2, tgt2, row(norm_post_g), T, D, tm_wide)

    g_wout_full = _matmul("d_w_out", mix, d_out, "tn", bf16)
    d_mix = _matmul("d_mix", d_out, wout_full, "nt", f32)

    d_c2, d_gconv, g_bpw2 = _row_bwd("conv_post_bwd", _fn_conv_post, cpost_params, cpost_rows,
                                      [(d_mix, RW, CW, True)], [bf16, bf16], T, tm, ncol)
    g_pw2_full = _matmul("d_w_pw2", c_act, d_c2, "tn", bf16)
    d_cact = _matmul("d_c_act", d_c2, pw2_full, "nt", f32)
    d_cpre, g_clng, g_clnb = _row_bwd("conv_ln_bwd", _fn_conv_ln, ln_params, [(c_pre, 0, CW, False)],
                                      [(d_cact, 0, CW, False)], [f32], T, tm)
    d_gluv, d_glug, g_cw_p, g_cb = _conv_bwd(proj, 4 * RW, 5 * RW, CW, cw_p, d_cpre, ktaps, T, tm)

    d_y, dr_x, dk_x, dv_x, d_grwkv, g_lnxg, g_lnxb, g_rk = _row_bwd(
        "rwkv_post_bwd", fn_post, post_params, post_rows, [(d_mix, 0, RW, True)], [f32, f32, f32, f32, bf16], T, tm, ncol)
    d_cw, d_cu, d_crt, d_cpb, d_cbend, d_lw_dec, d_cz = _wkv_state_bwd(*wkv_loc, states, d_y, head, T, RW)
    d_xr, d_lw, d_kh, d_xv, d_a, d_b = _wkv_local_bwd(
        *wkv_in, (d_cw, d_cu, d_crt, d_cpb, d_y, d_cbend, d_cz), d_lw_dec, dr_x, dk_x, dv_x, head, T, RW)
    pre_cots = [(d_lw, 0, RW, True), (d_kh, 0, RW, True), (d_a, 0, RW, True), (d_b, 0, RW, True)]
    d_xk, d_qw, d_qa, g_kk, g_ka = _row_bwd("rwkv_pre_bwd", fn_pre, pre_params, pre_rows, pre_cots, [f32, f32, f32],
                                            T, tm, ncol)
    d_xl, g_w0, g_wup_p, g_a0, g_aup_p = _row_bwd("lora_up_bwd", _fn_lora, lora_params, [(xs_l, 0, LORA_PAD, False)],
                                                  [(d_qw, 0, RW, False), (d_qa, 0, RW, False)], [f32], T, tm)
    dp_r, g_mur = _shift_bwd("shift_r_bwd", proj, 0, RW, mu_r, d_xr, T, tm)
    dp_k, g_muk = _shift_bwd("shift_k_bwd", proj, RW, RW, mu_k, d_xk, T, tm)
    dp_v, g_muv = _shift_bwd("shift_v_bwd", proj, 2 * RW, RW, mu_v, d_xv, T, tm)
    dp_l, g_mul = _shift_bwd("shift_l_bwd", proj, off_l, LORA_PAD, mu_l, d_xl, T, tm)
    d_proj = _place_blocks([dp_r, dp_k, dp_v, d_grwkv, d_gluv, d_glug, d_gconv, dp_l], axis=1)

    g_wp_t = _matmul("d_w_in", d_proj, h, "tn", bf16)

    g_win_t = _place_blocks([g_wp_t[:lo], g_wp_t[off_l:off_l + 2 * lora], g_wp_t[lo:off_l]], axis=0)
    core = lax.axis_index("c")

    def row_halves(a):
        a = a.reshape(4, 2, a.shape[1] // 2, a.shape[2])
        return a[:, 0], a[:, 1]

    def keep_give(a):
        h0, h1 = row_halves(a)
        return jnp.where(core == 0, h0, h1), jnp.where(core == 0, h1, h0)

    win_keep, win_give = keep_give(g_win_t.reshape(4, shard, D))
    wout_keep, wout_give = keep_give(g_wout_full.reshape(4, (RW + CW) // 4, D))
    pw2_keep, pw2_give = keep_give(g_pw2_full.reshape(4, CW // 4, CW))
    g_mu = jnp.concatenate([g_mur[0], g_muk[0], g_muv[0], g_mul[0, :2 * lora]])
    pad_rows = lambda a, n: jnp.concatenate([a, jnp.zeros((n - a.shape[0], a.shape[1]), f32)], axis=0)
    n_mu = -(-mu_shift.shape[0] // RW)
    small_vecs = [pad_rows(jnp.pad(g_mu, (0, n_mu * RW - g_mu.shape[0])).reshape(n_mu, RW), n_mu),
                  g_w0, g_a0, g_kk, g_ka, g_rk, g_lnxg, g_lnxb, g_cb, g_clng, g_clnb, g_bpw2,
                  g_npost.reshape(D // RW, RW), g_wup_p[:lora], g_aup_p[lora:2 * lora], g_cw_p[:ktaps]]
    n_small = sum(a.shape[0] for a in small_vecs)
    n_small_pad = -(-n_small // (2 * SUBLANES)) * (2 * SUBLANES)
    small = pad_rows(jnp.concatenate(small_vecs, axis=0), n_small_pad)

    win_got, wout_got, pw2_got = _share_sibling("pair_exchange", [win_give, wout_give, pw2_give])
    flat = lambda a: a.reshape(-1, a.shape[-1])
    chip_sum = lambda nm, keep, got: _add_pair(nm, flat(keep), flat(got)).reshape(keep.shape)
    q_win = chip_sum("chip_sum_w_in", win_keep, win_got)
    q_wout = chip_sum("chip_sum_w_out", wout_keep, wout_got)
    q_pw2 = chip_sum("chip_sum_w_pw2", pw2_keep, pw2_got)
    d_h, (r_win, r_wout, r_pw2, r_small) = _matmul("d_h", d_proj, wp_t, "nn", bf16, tk_t=2560,
                                                   comm=_exchange_chips([q_win, q_wout, q_pw2], [small]))
    grad_x2, g_npre = _rms_pre_bwd(x2, npg, d_h, gx_res, T, D, tm_wide)
    (r_npre,) = _run_comm("exchange_norm_pre", _exchange_chips([], [g_npre.reshape(D // RW, RW)]))
    s_npre = _sum_slots("sum_norm_pre", r_npre)
    s_win = _sum_slots("sum_w_in", r_win)
    s_wout = _sum_slots("sum_w_out", r_wout)
    s_pw2 = _sum_slots("sum_w_pw2", r_pw2)
    s_small = _sum_slots("sum_small", r_small)
    o_win, o_wout, o_pw2 = _share_sibling("share_sibling", [s_win, s_wout, s_pw2])
    south = lax.axis_index("c") == 0

    def both_halves(own, other):
        return jnp.concatenate([jnp.where(south, own, other), jnp.where(south, other, own)], axis=0)

    grad_w_in = both_halves(s_win, o_win)
    grad_w_out = both_halves(s_wout, o_wout)
    grad_w_pw2 = both_halves(s_pw2, o_pw2)

    chip = 2 * lax.axis_index("x") + lax.axis_index("y")
    pos = [0]

    def take(nrows):
        a = s_small[pos[0]:pos[0] + nrows]
        pos[0] += nrows
        return a

    my_cols = lambda a, w: lax.dynamic_slice_in_dim(a, chip * w, w, axis=1)
    grads = {}
    grads["norm_pre_g"] = s_npre.reshape(D)
    grads["mu_shift"] = take(n_mu).reshape(-1)[:mu_shift.shape[0]]
    for nm in ["w0", "a0", "k_k", "k_a"]:
        grads[nm] = take(1).reshape(RW)
    grads["r_k"] = take(1).reshape(r_k.shape)
    for nm in ["lnx_g", "lnx_b", "conv_b", "cln_g", "cln_b", "b_pw2"]:
        grads[nm] = take(1).reshape(RW)
    grads["norm_post_g"] = take(D // RW).reshape(D)
    grads["w_lora_up"] = my_cols(take(lora), RW // 4)
    grads["a_lora_up"] = my_cols(take(lora), RW // 4)
    grads["conv_w"] = my_cols(take(ktaps), CW // 4)
    grads["w_in"], grads["w_out"], grads["w_pw2"] = grad_w_in, grad_w_out, grad_w_pw2

    weights = dict(norm_pre_g=norm_pre_g, w_in=w_in, mu_shift=mu_shift, w0=w0, w_lora_up=w_lora_up, a0=a0,
                   a_lora_up=a_lora_up, k_k=k_k, k_a=k_a, r_k=r_k, lnx_g=lnx_g, lnx_b=lnx_b, conv_w=conv_w,
                   conv_b=conv_b, cln_g=cln_g, cln_b=cln_b, w_pw2=w_pw2, b_pw2=b_pw2, w_out=w_out,
                   norm_post_g=norm_post_g)
    ms = dict(norm_pre_g=m_norm_pre_g, w_in=m_w_in, mu_shift=m_mu_shift, w0=m_w0, w_lora_up=m_w_lora_up, a0=m_a0,
              a_lora_up=m_a_lora_up, k_k=m_k_k, k_a=m_k_a, r_k=m_r_k, lnx_g=m_lnx_g, lnx_b=m_lnx_b, conv_w=m_conv_w,
              conv_b=m_conv_b, cln_g=m_cln_g, cln_b=m_cln_b, w_pw2=m_w_pw2, b_pw2=m_b_pw2, w_out=m_w_out,
              norm_post_g=m_norm_post_g)
    vs = dict(norm_pre_g=v_norm_pre_g, w_in=v_w_in, mu_shift=v_mu_shift, w0=v_w0, w_lora_up=v_w_lora_up, a0=v_a0,
              a_lora_up=v_a_lora_up, k_k=v_k_k, k_a=v_k_a, r_k=v_r_k, lnx_g=v_lnx_g, lnx_b=v_lnx_b, conv_w=v_conv_w,
              conv_b=v_conv_b, cln_g=v_cln_g, cln_b=v_cln_b, w_pw2=v_w_pw2, b_pw2=v_b_pw2, w_out=v_w_out,
              norm_post_g=v_norm_post_g)
    names = list(weights)
    big = ["w_in", "w_out", "w_pw2"]
    deltas, new_m, new_v = {}, {}, {}
    d_t, m_t, v_t = _adamw("adamw_w_in", w_in_t, grad_w_in, m_w_in_t, v_w_in_t)
    grads["w_in"], deltas["w_in"], new_m["w_in"], new_v["w_in"] = grad_w_in.T, d_t.T, m_t.T, v_t.T
    for nm in big[1:]:
        deltas[nm], new_m[nm], new_v[nm] = _adamw("adamw_" + nm, weights[nm], grads[nm], ms[nm], vs[nm])
    rest = [nm for nm in names if nm not in big]
    sizes = [weights[nm].size for nm in rest]
    total = sum(sizes)
    width = 4 * LANES
    rows_p = -(-total // (width * SUBLANES)) * SUBLANES

    def pack(d):
        flat = jnp.concatenate([d[nm].reshape(-1) for nm in rest])
        return jnp.pad(flat, (0, rows_p * width - total)).reshape(rows_p, width)

    p_d, p_m, p_v = _adamw("adamw_small", pack(weights), pack(grads), pack(ms), pack(vs))
    o = 0
    for nm, sz in zip(rest, sizes):
        shp = weights[nm].shape
        deltas[nm] = p_d.reshape(-1)[o:o + sz].reshape(shp)
        new_m[nm] = p_m.reshape(-1)[o:o + sz].reshape(shp)
        new_v[nm] = p_v.reshape(-1)[o:o + sz].reshape(shp)
        o += sz

    loss = lax.psum(loss_part[0, 0], ("x", "y", "c"))
    grad_x = grad_x2[None]
    return (loss, grad_x, *[grads[nm] for nm in names], *[deltas[nm] for nm in names],
            *[new_m[nm] for nm in names], *[new_v[nm] for nm in names])
```

```python
import functools

import jax
import jax.numpy as jnp
from jax import lax
from jax.experimental import pallas as pl
from jax.experimental.pallas import tpu as pltpu

f32 = jnp.float32
bf16 = jnp.bfloat16
MESH = pl.DeviceIdType.MESH

NORM_EPS = 1e-6
LN_EPS = 1e-5
ADAM_LR, ADAM_B1, ADAM_B2, ADAM_EPS, ADAM_WD, ADAM_STEP = 0.001, 0.9, 0.999, 1e-08, 0.01, 10

LANES = 128
SUBLANES = 8
LORA_PAD = 256
CONV_HALO = 32
WKV_CHUNK = 64
WKV_HEADS = 16
WKV_STATE_HEADS = 32
WKV_PREC = lax.Precision.HIGH
SHARE_CHUNKS = 8
VMEM_LIMIT = 56 * 1024 * 1024


def _cparams(n_axes):
    return pltpu.CompilerParams(dimension_semantics=("arbitrary",) * n_axes, vmem_limit_bytes=VMEM_LIMIT)


def _tile(dim, target):
    best = None
    t = LANES
    while t <= min(dim, target):
        if dim % t == 0:
            best = t
        t += LANES
    return best if best is not None else dim


def _mm(a, b, prec=None):
    return lax.dot_general(a, b, (((1,), (0,)), ((), ())), precision=prec, preferred_element_type=f32)


def _mm_nt(a, b, prec=None):
    return lax.dot_general(a, b, (((1,), (1,)), ((), ())), precision=prec, preferred_element_type=f32)


def _mm_tn(a, b, prec=None):
    return lax.dot_general(a, b, (((0,), (0,)), ((), ())), precision=prec, preferred_element_type=f32)


@jax.custom_vjp
def _bmm(a, b):
    return _mm(a.astype(bf16), b.astype(bf16))


def _bmm_fwd(a, b):
    return _bmm(a, b), (a, b)


def _bmm_bwd(res, dc):
    a, b = res
    dcb = dc.astype(bf16)
    return _mm_nt(dcb, b.astype(bf16)), _mm_tn(a.astype(bf16), dcb)


_bmm.defvjp(_bmm_fwd, _bmm_bwd)


def _matmul(name, a, b, mode, out_dtype, tm_t=1024, tn_t=1024, tk_t=4096, comm=None):
    if mode == "nn":
        (M, K), (_, N) = a.shape, b.shape
    elif mode == "nt":
        (M, K), (N, _) = a.shape, b.shape
    else:
        (K, M), (_, N) = a.shape, b.shape
    tm, tn, tk = _tile(M, tm_t), _tile(N, tn_t), _tile(K, tk_t)
    ni, nj, nk = M // tm, N // tn, K // tk
    dot = {"nn": _mm, "nt": _mm_nt, "tn": _mm_tn}[mode]
    nc = len(comm.operands) if comm else 0

    def body(*refs):
        a_ref, b_ref = refs[:2]
        o_ref = refs[2 + nc]
        scratch = refs[3 + 2 * nc:]
        i, j, k = pl.program_id(0), pl.program_id(1), pl.program_id(2)
        if comm:
            comm_refs = (refs[2:2 + nc], refs[3 + nc:3 + 2 * nc], scratch[:len(comm.scratch)])

            @pl.when(jnp.logical_and(jnp.logical_and(i == 0, j == 0), k == 0))
            def _():
                comm.start(*comm_refs)

        if nk == 1:
            o_ref[...] = dot(a_ref[...], b_ref[...]).astype(o_ref.dtype)
        else:
            acc_ref = scratch[-1]

            @pl.when(k == 0)
            def _():
                acc_ref[...] = jnp.zeros_like(acc_ref)

            acc_ref[...] += dot(a_ref[...], b_ref[...])

            @pl.when(k == nk - 1)
            def _():
                o_ref[...] = acc_ref[...].astype(o_ref.dtype)

        if comm:
            @pl.when(jnp.logical_and(jnp.logical_and(i == ni - 1, j == nj - 1), k == nk - 1))
            def _():
                comm.wait(*comm_refs)

    a_spec = {"nn": pl.BlockSpec((tm, tk), lambda i, j, k: (i, k)),
              "nt": pl.BlockSpec((tm, tk), lambda i, j, k: (i, k)),
              "tn": pl.BlockSpec((tk, tm), lambda i, j, k: (k, i))}[mode]
    b_spec = {"nn": pl.BlockSpec((tk, tn), lambda i, j, k: (k, j)),
              "nt": pl.BlockSpec((tn, tk), lambda i, j, k: (j, k)),
              "tn": pl.BlockSpec((tk, tn), lambda i, j, k: (k, j))}[mode]
    res = pl.pallas_call(
        body, name=name, grid=(ni, nj, nk),
        in_specs=[a_spec, b_spec] + [ANY] * nc,
        out_specs=[pl.BlockSpec((tm, tn), lambda i, j, k: (i, j))] + [ANY] * nc,
        out_shape=[jax.ShapeDtypeStruct((M, N), out_dtype)] + (list(comm.out_shape) if comm else []),
        scratch_shapes=(list(comm.scratch) if comm else []) + ([pltpu.VMEM((tm, tn), f32)] if nk > 1 else []),
        compiler_params=_cparams(3),
    )(a, b, *(comm.operands if comm else []))
    return (res[0], res[1:]) if comm else res[0]


def _row_spec(op, tm, ncol):
    arr, off, width, tiled = op
    if tiled:
        cw = width // ncol
        return pl.BlockSpec((tm, cw), lambda j, i: (i, off // cw + j))
    return pl.BlockSpec((tm, width), lambda j, i: (i, off // width))


def _param_spec(p, ncol):
    arr, tiled = p
    rows, width = arr.shape
    if tiled:
        return pl.BlockSpec((rows, width // ncol), lambda j, i: (0, j))
    return pl.BlockSpec((rows, width), lambda j, i: (0, 0))


def _row_fwd(name, fn, params, rows, outs, T, tm, ncol=1):
    npar, nrow = len(params), len(rows)

    def body(*refs):
        pv = [r[...] for r in refs[:npar]]
        rv = [r[...].astype(f32) for r in refs[npar:npar + nrow]]
        res = fn(*pv, *rv)
        for o_ref, val in zip(refs[npar + nrow:], res):
            o_ref[...] = val.astype(o_ref.dtype)

    return pl.pallas_call(
        body, name=name, grid=(ncol, T // tm),
        in_specs=[_param_spec(p, ncol) for p in params] + [_row_spec(r, tm, ncol) for r in rows],
        out_specs=[pl.BlockSpec((tm, w // ncol), lambda j, i: (i, j)) for w, _ in outs],
        out_shape=[jax.ShapeDtypeStruct((T, w), dt) for w, dt in outs],
        compiler_params=_cparams(2),
    )(*[p[0] for p in params], *[r[0] for r in rows])


def _row_bwd(name, fn, params, rows, cots, row_grads, T, tm, ncol=1):
    npar, nrow, ncot = len(params), len(rows), len(cots)
    want = [k for k, dt in enumerate(row_grads) if dt is not None]

    def body(*refs):
        pv = [r[...] for r in refs[:npar]]
        rv = [r[...].astype(f32) for r in refs[npar:npar + nrow]]
        cv = tuple(r[...].astype(f32) for r in refs[npar + nrow:npar + nrow + ncot])
        out_refs = refs[npar + nrow + ncot:]
        _, vjp = jax.vjp(fn, *pv, *rv)
        grads = vjp(cv)
        for o_ref, k in zip(out_refs[:len(want)], want):
            o_ref[...] = grads[npar + k].astype(o_ref.dtype)
        j, i = pl.program_id(0), pl.program_id(1)
        for o_ref, p, g in zip(out_refs[len(want):], params, grads[:npar]):
            first = (i == 0) if p[1] else jnp.logical_and(i == 0, j == 0)

            @pl.when(first)
            def _():
                o_ref[...] = jnp.zeros_like(o_ref)

            o_ref[...] += g

    def grad_spec(op):
        arr, off, width, tiled = op
        if tiled:
            return pl.BlockSpec((tm, width // ncol), lambda j, i: (i, j)), (T, width)
        return pl.BlockSpec((tm, width), lambda j, i: (i, j)), (T, width * ncol)

    gspecs = [grad_spec(rows[k]) for k in want]
    return pl.pallas_call(
        body, name=name, grid=(ncol, T // tm),
        in_specs=[_param_spec(p, ncol) for p in params] + [_row_spec(r, tm, ncol) for r in rows]
        + [_row_spec(c, tm, ncol) for c in cots],
        out_specs=[s for s, _ in gspecs] + [_param_spec(p, ncol) for p in params],
        out_shape=[jax.ShapeDtypeStruct(shp, row_grads[k]) for (_, shp), k in zip(gspecs, want)]
        + [jax.ShapeDtypeStruct(p[0].shape, f32) for p in params],
        compiler_params=_cparams(2),
    )(*[p[0] for p in params], *[r[0] for r in rows], *[c[0] for c in cots])


def _seg_sum(x, head):
    li = lax.broadcasted_iota(jnp.int32, (LANES, LANES), 0) // head
    lj = lax.broadcasted_iota(jnp.int32, (LANES, LANES), 1) // head
    q = (li == lj).astype(f32)
    parts = [_mm(x[:, s:s + LANES], q, lax.Precision.HIGH) for s in range(0, x.shape[1], LANES)]
    return parts[0] if len(parts) == 1 else jnp.concatenate(parts, axis=1)


def _sigmoid(z):
    return 1.0 / (1.0 + jnp.exp(-z))


def _silu(z):
    return z * _sigmoid(z)


def _rms(g, x):
    return x * lax.rsqrt(jnp.mean(x * x, axis=-1, keepdims=True) + NORM_EPS) * g


def _fn_rms_pre(g, x):
    return (_rms(g, x),)


def _fn_lora(w0, wup, a0, aup, xl):
    qw = w0 + _bmm(jnp.tanh(xl), wup)
    qa = a0 + _bmm(xl, aup)
    return qw, qa


def _fn_rwkv_pre(head, k_k, k_a, xk, qw, qa):
    w_log = -(jnp.maximum(-qw, 0.0) + jnp.log(1.0 + jnp.exp(-jnp.abs(qw)))) - 0.5
    lw = -jnp.exp(w_log)
    a_sig = _sigmoid(qa)
    kk = xk * k_k
    kk = kk / jnp.maximum(jnp.sqrt(_seg_sum(kk * kk, head)), 1e-12)
    k_h = xk * (1.0 + (a_sig - 1.0) * k_a)
    return lw, k_h, -kk, kk * a_sig


def _fn_rwkv_post(head, lnx_g, lnx_b, r_k, y, r, k_h, v, g):
    inv = 1.0 / head
    mu = _seg_sum(y, head) * inv
    d = y - mu
    var = _seg_sum(d * d, head) * inv
    yn = d * lax.rsqrt(var + 1e-5 * head) * lnx_g + lnx_b
    bonus = _seg_sum(r * k_h * r_k, head) * v
    return ((yn + bonus) * _silu(g),)


def _fn_conv_ln(cln_g, cln_b, c):
    mu = jnp.mean(c, axis=-1, keepdims=True)
    d = c - mu
    var = jnp.mean(d * d, axis=-1, keepdims=True)
    return (_silu(d * lax.rsqrt(var + LN_EPS) * cln_g + cln_b),)


def _fn_conv_post(b_pw2, c2, g):
    return ((c2 + b_pw2) * _silu(g),)


def _post(out, x, tgt, g, T, D, tm):
    def body(g_ref, o_ref, x_ref, t_ref, dout_ref, gx_ref, loss_ref, dg_ref):
        i = pl.program_id(0)
        o, vjp = jax.vjp(_rms, g_ref[...], o_ref[...])
        err = x_ref[...] + o - t_ref[...]
        d_y = err * (1.0 / D)
        dg, d_out = vjp(d_y)
        dout_ref[...] = d_out.astype(dout_ref.dtype)
        gx_ref[...] = d_y

        @pl.when(i == 0)
        def _():
            loss_ref[...] = jnp.zeros_like(loss_ref)
            dg_ref[...] = jnp.zeros_like(dg_ref)

        loss_ref[...] += jnp.sum(err * err, keepdims=True) * (0.5 / D)
        dg_ref[...] += dg

    row = pl.BlockSpec((tm, D), lambda i: (i, 0))
    vec = pl.BlockSpec((1, D), lambda i: (0, 0))
    return pl.pallas_call(
        body, name="post_loss", grid=(T // tm,),
        in_specs=[vec, row, row, row],
        out_specs=[row, row, pl.BlockSpec((1, 1), lambda i: (0, 0)), vec],
        out_shape=[jax.ShapeDtypeStruct((T, D), bf16), jax.ShapeDtypeStruct((T, D), f32),
                   jax.ShapeDtypeStruct((1, 1), f32), jax.ShapeDtypeStruct((1, D), f32)],
        compiler_params=_cparams(1),
    )(g, out, x, tgt)


def _rms_pre_bwd(x, g, dh, gx_res, T, D, tm):
    def body(g_ref, x_ref, dh_ref, res_ref, dx_ref, dg_ref):
        i = pl.program_id(0)
        _, vjp = jax.vjp(_rms, g_ref[...], x_ref[...])
        dg, dx = vjp(dh_ref[...].astype(f32))
        dx_ref[...] = dx + res_ref[...]

        @pl.when(i == 0)
        def _():
            dg_ref[...] = jnp.zeros_like(dg_ref)

        dg_ref[...] += dg

    row = pl.BlockSpec((tm, D), lambda i: (i, 0))
    vec = pl.BlockSpec((1, D), lambda i: (0, 0))
    return pl.pallas_call(
        body, name="rms_pre_bwd", grid=(T // tm,),
        in_specs=[vec, row, row, row], out_specs=[row, vec],
        out_shape=[jax.ShapeDtypeStruct((T, D), f32), jax.ShapeDtypeStruct((1, D), f32)],
        compiler_params=_cparams(1),
    )(g, x, dh, gx_res)


def _prev_rows(cur, halo_ref, first):
    top = jnp.where(first, 0.0, halo_ref[SUBLANES - 1:SUBLANES, :])
    rolled = pltpu.roll(cur, 1, 0)
    rid = lax.broadcasted_iota(jnp.int32, cur.shape, 0)
    return jnp.where(rid == 0, top, rolled)


def _shift_fwd(name, proj, off, width, mu, T, tm):
    cw = _tile(width, 512)
    ncol, cb = width // cw, off // cw
    hb = tm // SUBLANES

    def body(mu_ref, cur_ref, halo_ref, o_ref):
        i = pl.program_id(1)
        cur = cur_ref[...]
        prev = _prev_rows(cur, halo_ref, i == 0)
        o_ref[...] = cur + (prev - cur) * mu_ref[...]

    return pl.pallas_call(
        body, name=name, grid=(ncol, T // tm),
        in_specs=[pl.BlockSpec((1, cw), lambda j, i: (0, j)),
                  pl.BlockSpec((tm, cw), lambda j, i: (i, cb + j)),
                  pl.BlockSpec((SUBLANES, cw), lambda j, i: (jnp.maximum(i * hb - 1, 0), cb + j))],
        out_specs=pl.BlockSpec((tm, cw), lambda j, i: (i, j)),
        out_shape=jax.ShapeDtypeStruct((T, width), f32),
        compiler_params=_cparams(2),
    )(mu, proj, proj)


def _shift_bwd(name, proj, off, width, mu, dxs, T, tm):
    cw = _tile(width, 512)
    ncol, cb = width // cw, off // cw
    hb, nt = tm // SUBLANES, T // tm

    def body(mu_ref, cur_ref, halo_ref, d_ref, dnext_ref, o_ref, dmu_ref):
        i = pl.program_id(1)
        cur = cur_ref[...]
        prev = _prev_rows(cur, halo_ref, i == 0)
        d = d_ref[...]
        bottom = jnp.where(i == nt - 1, 0.0, dnext_ref[0:1, :])
        rid = lax.broadcasted_iota(jnp.int32, d.shape, 0)
        d_next = jnp.where(rid == tm - 1, bottom, pltpu.roll(d, tm - 1, 0))
        mu_v = mu_ref[...]
        o_ref[...] = (d * (1.0 - mu_v) + d_next * mu_v).astype(o_ref.dtype)

        @pl.when(i == 0)
        def _():
            dmu_ref[...] = jnp.zeros_like(dmu_ref)

        dmu_ref[...] += jnp.sum(d * (prev - cur), axis=0, keepdims=True)

    return pl.pallas_call(
        body, name=name, grid=(ncol, nt),
        in_specs=[pl.BlockSpec((1, cw), lambda j, i: (0, j)),
                  pl.BlockSpec((tm, cw), lambda j, i: (i, cb + j)),
                  pl.BlockSpec((SUBLANES, cw), lambda j, i: (jnp.maximum(i * hb - 1, 0), cb + j)),
                  pl.BlockSpec((tm, cw), lambda j, i: (i, j)),
                  pl.BlockSpec((SUBLANES, cw), lambda j, i: (jnp.minimum((i + 1) * hb, nt * hb - 1), j))],
        out_specs=[pl.BlockSpec((tm, cw), lambda j, i: (i, j)), pl.BlockSpec((1, cw), lambda j, i: (0, j))],
        out_shape=[jax.ShapeDtypeStruct((T, width), bf16), jax.ShapeDtypeStruct((1, width), f32)],
        compiler_params=_cparams(2),
    )(mu, proj, proj, dxs, dxs)


def _rolled_copies(dst_ref, ext):
    n = ext.shape[0]
    dst_ref[0] = ext
    for r in range(1, SUBLANES):
        dst_ref[r] = pltpu.roll(ext, n - r, 0)


def _window(rolled_ref, start, rows):
    q, r = divmod(start, SUBLANES)
    return rolled_ref[r, pl.ds(SUBLANES * q, rows), :]


def _conv_fwd(proj, off_v, off_g, width, conv_w, conv_b, ktaps, T, tm):
    cw = _tile(width, 512)
    ncol = width // cw
    hb = tm // CONV_HALO
    lead = CONV_HALO - (ktaps - 1)

    def body(w_ref, b_ref, v_ref, g_ref, hv_ref, hg_ref, o_ref, u_ref):
        i = pl.program_id(1)
        halo = hv_ref[...] * _sigmoid(hg_ref[...])
        _rolled_copies(u_ref, jnp.concatenate([jnp.where(i == 0, 0.0, halo), v_ref[...] * _sigmoid(g_ref[...])], axis=0))
        acc = jnp.zeros((tm, cw), f32) + b_ref[...]
        for j in range(ktaps):
            acc = acc + _window(u_ref, lead + j, tm) * w_ref[j:j + 1, :]
        o_ref[...] = acc

    def tile(off):
        return pl.BlockSpec((tm, cw), lambda j, i: (i, off // cw + j))

    def halo(off):
        return pl.BlockSpec((CONV_HALO, cw), lambda j, i: (jnp.maximum(i * hb - 1, 0), off // cw + j))

    return pl.pallas_call(
        body, name="conv_fwd", grid=(ncol, T // tm),
        in_specs=[pl.BlockSpec((CONV_HALO, cw), lambda j, i: (0, j)), pl.BlockSpec((1, cw), lambda j, i: (0, j)),
                  tile(off_v), tile(off_g), halo(off_v), halo(off_g)],
        out_specs=pl.BlockSpec((tm, cw), lambda j, i: (i, j)),
        out_shape=jax.ShapeDtypeStruct((T, width), f32),
        scratch_shapes=[pltpu.VMEM((SUBLANES, CONV_HALO + tm, cw), f32)],
        compiler_params=_cparams(2),
    )(conv_w, conv_b, proj, proj, proj, proj)


def _conv_bwd(proj, off_v, off_g, width, conv_w, dc, ktaps, T, tm):
    cw = _tile(width, 512)
    ncol = width // cw
    hb, nt = tm // CONV_HALO, T // tm
    lead = CONV_HALO - (ktaps - 1)

    def body(w_ref, v_ref, g_ref, hv_ref, hg_ref, dc_ref, dcn_ref, dv_ref, dg_ref, dw_ref, db_ref, u_ref, d_ref):
        i = pl.program_id(1)
        halo = hv_ref[...] * _sigmoid(hg_ref[...])
        sig = _sigmoid(g_ref[...])
        gv = v_ref[...]
        _rolled_copies(u_ref, jnp.concatenate([jnp.where(i == 0, 0.0, halo), gv * sig], axis=0))
        dcur = dc_ref[...]
        _rolled_copies(d_ref, jnp.concatenate([dcur, jnp.where(i == nt - 1, 0.0, dcn_ref[...])], axis=0))

        @pl.when(i == 0)
        def _():
            dw_ref[...] = jnp.zeros_like(dw_ref)
            db_ref[...] = jnp.zeros_like(db_ref)

        du = jnp.zeros((tm, cw), f32)
        for j in range(ktaps):
            du = du + _window(d_ref, ktaps - 1 - j, tm) * w_ref[j:j + 1, :]
            dw_ref[j:j + 1, :] += jnp.sum(_window(u_ref, lead + j, tm) * dcur, axis=0, keepdims=True)
        db_ref[...] += jnp.sum(dcur, axis=0, keepdims=True)
        dv_ref[...] = (du * sig).astype(dv_ref.dtype)
        dg_ref[...] = (du * gv * sig * (1.0 - sig)).astype(dg_ref.dtype)

    def tile(off):
        return pl.BlockSpec((tm, cw), lambda j, i: (i, off // cw + j))

    def halo(off):
        return pl.BlockSpec((CONV_HALO, cw), lambda j, i: (jnp.maximum(i * hb - 1, 0), off // cw + j))

    return pl.pallas_call(
        body, name="conv_bwd", grid=(ncol, nt),
        in_specs=[pl.BlockSpec((CONV_HALO, cw), lambda j, i: (0, j)),
                  tile(off_v), tile(off_g), halo(off_v), halo(off_g),
                  pl.BlockSpec((tm, cw), lambda j, i: (i, j)),
                  pl.BlockSpec((CONV_HALO, cw), lambda j, i: (jnp.minimum((i + 1) * hb, nt * hb - 1), j))],
        out_specs=[pl.BlockSpec((tm, cw), lambda j, i: (i, j)), pl.BlockSpec((tm, cw), lambda j, i: (i, j)),
                   pl.BlockSpec((CONV_HALO, cw), lambda j, i: (0, j)), pl.BlockSpec((1, cw), lambda j, i: (0, j))],
        out_shape=[jax.ShapeDtypeStruct((T, width), bf16), jax.ShapeDtypeStruct((T, width), bf16),
                   jax.ShapeDtypeStruct((CONV_HALO, width), f32), jax.ShapeDtypeStruct((1, width), f32)],
        scratch_shapes=[pltpu.VMEM((SUBLANES, CONV_HALO + tm, cw), f32), pltpu.VMEM((SUBLANES, tm + CONV_HALO, cw), f32)],
        compiler_params=_cparams(2),
    )(conv_w, proj, proj, proj, proj, dc, dc)


def _each(f, *lists):
    return [f(*xs) for xs in zip(*lists)]


def _wkv_local(r, lw, k, v, a, b):
    C = r[0].shape[0]
    P = WKV_PREC
    row = lax.broadcasted_iota(jnp.int32, (C, C), 0)
    col = lax.broadcasted_iota(jnp.int32, (C, C), 1)
    incl, strict = row >= col, row > col
    tri = incl.astype(f32)
    zero = jnp.zeros((C, C), f32)
    G = _each(lambda x: _mm(tri, x, P), lw)
    to_end = _each(lambda x, g: jnp.exp(jnp.sum(x, axis=0, keepdims=True) - g), lw, G)
    e_g = _each(jnp.exp, G)
    e_ng = _each(lambda g: jnp.exp(-g), G)
    At = _each(lambda x, g, w: x * jnp.exp(g - w), a, G, lw)
    Rt = _each(jnp.multiply, r, e_g)
    Kt = _each(jnp.multiply, k, e_ng)
    Bt = _each(jnp.multiply, b, e_ng)
    sc = _each(lambda at, rt, bt, kt: _mm_nt(jnp.concatenate([at, rt], axis=0), jnp.concatenate([bt, kt], axis=0), P),
               At, Rt, Bt, Kt)
    L = _each(lambda s: jnp.where(strict, s[:C, :C], zero), sc)
    M = _each(lambda s: jnp.where(strict, s[:C, C:], zero), sc)
    Pb = _each(lambda s: jnp.where(incl, s[C:, :C], zero), sc)
    Pk = _each(lambda s: jnp.where(incl, s[C:, C:], zero), sc)
    MPk = _each(lambda m, p, x: _bmm(jnp.concatenate([m, p], axis=0), x), M, Pk, v)
    WU = _each(lambda at, mp: jnp.concatenate([at, mp[:C]], axis=1), At, MPk)
    Lp = L
    n = 1
    while n < C:
        n *= 2
        if n < C:
            step = _each(lambda l, x: _bmm(l, jnp.concatenate([x, l], axis=1)), Lp, WU)
            WU = _each(lambda x, s: x + s[:, :x.shape[1]], WU, step)
            Lp = _each(lambda x, s: s[:, x.shape[1]:], WU, step)
        else:
            WU = _each(lambda x, l: x + _bmm(l, x), WU, Lp)
    N = r[0].shape[1]
    W = _each(lambda x: x[:, :N], WU)
    U = _each(lambda x: x[:, N:], WU)
    Y0 = _each(lambda mp: mp[C:], MPk)
    Bend = _each(jnp.multiply, b, to_end)
    Z = _each(lambda x, y, e: _mm_tn(x, y * e, P), v, k, to_end)
    return W, U, Rt, Pb, Y0, Bend, Z


def _wkv_state(S0, W, U, Rt, Pb, Bend, lw, Y0, Z):
    P = WKV_PREC
    C = W[0].shape[0]
    WR = _each(lambda w, rt, s: _mm_nt(jnp.concatenate([w, rt], axis=0), s, P), W, Rt, S0)
    X = _each(lambda wr, u: wr[:C] + u, WR, U)
    y = _each(lambda p, x, wr, c: _mm(p, x, P) + wr[C:] + c, Pb, X, WR, Y0)
    S1 = _each(lambda s, w, x, e, z: s * jnp.exp(jnp.sum(w, axis=0, keepdims=True)) + _mm_tn(x, e, P) + z,
               S0, lw, X, Bend, Z)
    return y, S1


def _wkv_dims(head, T, RW, heads_per_step):
    C = min(WKV_CHUNK, T)
    nh = RW // head
    hb = min(heads_per_step, nh)
    return C, nh, hb, hb * head, T // C


def _heads(ref, hb, head):
    return [ref[:, h * head:(h + 1) * head] for h in range(hb)]


def _put_heads(ref, vals, head):
    for h, val in enumerate(vals):
        ref[:, h * head:(h + 1) * head] = val


def _wkv_local_fwd(r, lw, k, v, a, b, head, T, RW):
    C, nh, hb, bw, nc = _wkv_dims(head, T, RW, WKV_HEADS)

    def body(*refs):
        ins, outs = refs[:6], refs[6:]
        res = _wkv_local(*[_heads(x, hb, head) for x in ins])
        for o_ref, vals in zip(outs[:6], res[:6]):
            _put_heads(o_ref, vals, head)
        for h in range(hb):
            outs[6][0, h] = res[6][h]

    blk = pl.BlockSpec((C, bw), lambda g, c: (c, g))
    sq = pl.BlockSpec((1, hb, head, head), lambda g, c: (c, g, 0, 0))
    return pl.pallas_call(
        body, name="wkv_local", grid=(nh // hb, nc),
        in_specs=[blk] * 6, out_specs=[blk] * 6 + [sq],
        out_shape=[jax.ShapeDtypeStruct((T, RW), f32)] * 6 + [jax.ShapeDtypeStruct((nc, nh, head, head), f32)],
        compiler_params=_cparams(2),
    )(r, lw, k, v, a, b)


def _wkv_state_fwd(W, U, Rt, Pb, Bend, lw, Y0, Z, head, T, RW):
    C, nh, hb, bw, nc = _wkv_dims(head, T, RW, WKV_STATE_HEADS)

    def body(w_ref, u_ref, rt_ref, pb_ref, be_ref, lw_ref, y0_ref, z_ref, y_ref, st_ref, s_ref):
        @pl.when(pl.program_id(1) == 0)
        def _():
            s_ref[...] = jnp.zeros_like(s_ref)

        S0 = [s_ref[h] for h in range(hb)]
        for h in range(hb):
            st_ref[0, h] = S0[h]
        rows = [_heads(x, hb, head) for x in (w_ref, u_ref, rt_ref, pb_ref, be_ref, lw_ref, y0_ref)]
        y, S1 = _wkv_state(S0, *rows, [z_ref[0, h] for h in range(hb)])
        _put_heads(y_ref, y, head)
        for h in range(hb):
            s_ref[h] = S1[h]

    blk = pl.BlockSpec((C, bw), lambda g, c: (c, g))
    sq = pl.BlockSpec((1, hb, head, head), lambda g, c: (c, g, 0, 0))
    return pl.pallas_call(
        body, name="wkv_state", grid=(nh // hb, nc),
        in_specs=[blk] * 7 + [sq], out_specs=[blk, sq],
        out_shape=[jax.ShapeDtypeStruct((T, RW), f32), jax.ShapeDtypeStruct((nc, nh, head, head), f32)],
        scratch_shapes=[pltpu.VMEM((hb, head, head), f32)],
        compiler_params=_cparams(2),
    )(W, U, Rt, Pb, Bend, lw, Y0, Z)


def _wkv_state_bwd(W, U, Rt, Pb, Bend, lw, Y0, Z, states, dy, head, T, RW):
    C, nh, hb, bw, nc = _wkv_dims(head, T, RW, WKV_STATE_HEADS)

    def body(w_ref, u_ref, rt_ref, pb_ref, be_ref, lw_ref, y0_ref, z_ref, st_ref, dy_ref,
             dw_ref, du_ref, drt_ref, dpb_ref, dbe_ref, dlw_ref, dz_ref, ds_ref):
        @pl.when(pl.program_id(1) == 0)
        def _():
            ds_ref[...] = jnp.zeros_like(ds_ref)

        dS1 = [ds_ref[h] for h in range(hb)]
        for h in range(hb):
            dz_ref[0, h] = dS1[h]
        rows = [_heads(x, hb, head) for x in (w_ref, u_ref, rt_ref, pb_ref, be_ref, lw_ref)]
        Y0 = _heads(y0_ref, hb, head)
        Zs = [z_ref[0, h] for h in range(hb)]
        _, vjp = jax.vjp(lambda s0, *rw: _wkv_state(s0, *rw, Y0, Zs), [st_ref[0, h] for h in range(hb)], *rows)
        grads = vjp((_heads(dy_ref, hb, head), dS1))
        for o_ref, vals in zip((dw_ref, du_ref, drt_ref, dpb_ref, dbe_ref, dlw_ref), grads[1:]):
            _put_heads(o_ref, vals, head)
        for h in range(hb):
            ds_ref[h] = grads[0][h]

    blk = pl.BlockSpec((C, bw), lambda g, c: (nc - 1 - c, g))
    sq = pl.BlockSpec((1, hb, head, head), lambda g, c: (nc - 1 - c, g, 0, 0))
    return pl.pallas_call(
        body, name="wkv_state_bwd", grid=(nh // hb, nc),
        in_specs=[blk] * 7 + [sq, sq, blk], out_specs=[blk] * 6 + [sq],
        out_shape=[jax.ShapeDtypeStruct((T, RW), f32)] * 6 + [jax.ShapeDtypeStruct((nc, nh, head, head), f32)],
        scratch_shapes=[pltpu.VMEM((hb, head, head), f32)],
        compiler_params=_cparams(2),
    )(W, U, Rt, Pb, Bend, lw, Y0, Z, states, dy)


def _wkv_local_bwd(r, lw, k, v, a, b, cots, d_lw_x, dr_x, dk_x, dv_x, head, T, RW):
    C, nh, hb, bw, nc = _wkv_dims(head, T, RW, WKV_HEADS)

    def body(*refs):
        ins, cot_refs, add_refs, outs = refs[:6], refs[6:13], refs[13:17], refs[17:]
        _, vjp = jax.vjp(_wkv_local, *[_heads(x, hb, head) for x in ins])
        cts = [_heads(x, hb, head) for x in cot_refs[:6]] + [[cot_refs[6][0, h] for h in range(hb)]]
        dr, dlw, dk, dv, da, db = vjp(tuple(cts))
        dlw_x, drx, dkx, dvx = [_heads(x, hb, head) for x in add_refs]
        _put_heads(outs[0], _each(jnp.add, dr, drx), head)
        _put_heads(outs[1], _each(jnp.add, dlw, dlw_x), head)
        _put_heads(outs[2], _each(jnp.add, dk, dkx), head)
        _put_heads(outs[3], _each(jnp.add, dv, dvx), head)
        _put_heads(outs[4], da, head)
        _put_heads(outs[5], db, head)

    blk = pl.BlockSpec((C, bw), lambda g, c: (c, g))
    sq = pl.BlockSpec((1, hb, head, head), lambda g, c: (c, g, 0, 0))
    return pl.pallas_call(
        body, name="wkv_local_bwd", grid=(nh // hb, nc),
        in_specs=[blk] * 12 + [sq] + [blk] * 4, out_specs=[blk] * 6,
        out_shape=[jax.ShapeDtypeStruct((T, RW), f32)] * 6,
        compiler_params=_cparams(2),
    )(r, lw, k, v, a, b, *cots, d_lw_x, dr_x, dk_x, dv_x)


def _rows_tile(R, row_bytes, budget, mult=SUBLANES):
    best = None
    t = mult
    while t <= R:
        if R % t == 0 and t * row_bytes <= budget:
            best = t
        t += mult
    return best if best is not None else R


def _sum_slots(name, parts):
    S, R, W = parts.shape
    budget = 4 << 20
    tr = _rows_tile(R, S * W * 4, budget, 2 * SUBLANES)
    cw = W if tr * S * W * 4 <= 2 * budget else _tile(W, max(LANES, 2 * budget // (S * tr * 4)))

    def body(p_ref, o_ref):
        acc = p_ref[0].astype(f32)
        for d in range(1, S):
            acc = acc + p_ref[d].astype(f32)
        o_ref[...] = acc

    return pl.pallas_call(
        body, name=name, grid=(R // tr, W // cw),
        in_specs=[pl.BlockSpec((S, tr, cw), lambda i, j: (0, i, j))],
        out_specs=pl.BlockSpec((tr, cw), lambda i, j: (i, j)),
        out_shape=jax.ShapeDtypeStruct((R, W), f32),
        compiler_params=_cparams(2),
    )(parts)


def _add_pair(name, a, b):
    R, W = a.shape
    tr = _rows_tile(R, W * 4, 2 << 20, 2 * SUBLANES)

    def body(a_ref, b_ref, o_ref):
        o_ref[...] = (a_ref[...].astype(f32) + b_ref[...].astype(f32)).astype(o_ref.dtype)

    blk = pl.BlockSpec((tr, W), lambda i: (i, 0))
    return pl.pallas_call(
        body, name=name, grid=(R // tr,), in_specs=[blk, blk], out_specs=blk,
        out_shape=jax.ShapeDtypeStruct((R, W), a.dtype), compiler_params=_cparams(1),
    )(a, b)


def _adamw(name, w, g, m, v):
    R, W = w.shape
    tr = _rows_tile(R, W * 4, 1 << 20)

    def body(w_ref, g_ref, m_ref, v_ref, d_ref, nm_ref, nv_ref):
        g_v = g_ref[...]
        nm = ADAM_B1 * m_ref[...] + (1.0 - ADAM_B1) * g_v
        nv = ADAM_B2 * v_ref[...] + (1.0 - ADAM_B2) * (g_v * g_v)
        m_hat = nm / (1.0 - ADAM_B1 ** ADAM_STEP)
        v_hat = nv / (1.0 - ADAM_B2 ** ADAM_STEP)
        d_ref[...] = -ADAM_LR * (m_hat / (jnp.sqrt(v_hat) + ADAM_EPS) + ADAM_WD * w_ref[...])
        nm_ref[...] = nm
        nv_ref[...] = nv

    blk = pl.BlockSpec((tr, W), lambda i: (i, 0))
    return pl.pallas_call(
        body, name=name, grid=(R // tr,),
        in_specs=[blk] * 4, out_specs=[blk] * 3,
        out_shape=[jax.ShapeDtypeStruct((R, W), f32)] * 3,
        compiler_params=_cparams(1),
    )(w, g, m, v)


ANY = pl.BlockSpec(memory_space=pl.ANY)


def _place():
    return lax.axis_index("x"), lax.axis_index("y"), lax.axis_index("c")


class _Comm:
    def __init__(self, operands, out_shape, scratch, start, wait):
        self.operands, self.out_shape, self.scratch, self.start, self.wait = operands, out_shape, scratch, start, wait


def _run_comm(name, comm):
    n = len(comm.operands)

    def body(*refs):
        parts = (refs[:n], refs[n:2 * n], refs[2 * n:])
        comm.start(*parts)
        comm.wait(*parts)

    return pl.pallas_call(
        body, name=name, in_specs=[ANY] * n, out_specs=[ANY] * n, out_shape=comm.out_shape,
        scratch_shapes=comm.scratch,
    )(*comm.operands)


def _copy_chunks(rows, cols):
    k = SHARE_CHUNKS // 2
    if rows % (k * 2 * SUBLANES) == 0:
        return [(pl.ds(q * (rows // k), rows // k), pl.ds(0, cols)) for q in range(k)]
    if cols % (k * LANES) == 0:
        return [(pl.ds(0, rows), pl.ds(q * (cols // k), cols // k)) for q in range(k)]
    return [(pl.ds(0, rows), pl.ds(0, cols))]


def _gather_chips(arrays):
    n = len(arrays)
    parts = [(a, h, blk) for a, arr in enumerate(arrays) for h in range(2) for blk in _copy_chunks(*arr.shape[1:])]

    def copies(ins, outs, sems):
        send_sems, recv_sems, local_sems = sems
        x, y, c = _place()
        mine = 2 * x + y
        sib = (x, y, 1 - c)
        chips = [(1 - x, y), (x, 1 - y), (1 - x, 1 - y)]

        def local():
            return [pltpu.make_async_copy(ins[a].at[(h, *blk)], outs[a].at[(mine, h, *blk)], local_sems.at[p])
                    for p, (a, h, blk) in enumerate(parts)]

        def over_ici(a, j, slot):
            px, py = chips[j]
            return pltpu.make_async_remote_copy(
                src_ref=ins[a].at[c], dst_ref=outs[a].at[slot, c], send_sem=send_sems.at[3 * a + j],
                recv_sem=recv_sems.at[3 * a + j], device_id=(px, py, c), device_id_type=MESH)

        def over_d2d(a, j, half):
            px, py = chips[j]
            slot = 2 * px + py
            return pltpu.make_async_remote_copy(
                src_ref=outs[a].at[slot, half], dst_ref=outs[a].at[slot, half], send_sem=send_sems.at[3 * (n + a) + j],
                recv_sem=recv_sems.at[3 * (n + a) + j], device_id=sib, device_id_type=MESH)

        pairs = [(a, j) for a in range(n) for j in range(3)]
        return dict(local=local,
                    sends=lambda: [over_ici(a, j, mine) for a, j in pairs],
                    landing=lambda: [over_ici(a, j, 2 * chips[j][0] + chips[j][1]) for a, j in pairs],
                    passed=lambda: [over_d2d(a, j, c) for a, j in pairs],
                    from_sib=lambda: [over_d2d(a, j, 1 - c) for a, j in pairs])

    def start(ins, outs, sems):
        cps = copies(ins, outs, sems)
        for cp in cps["local"]() + cps["sends"]():
            cp.start()

    def wait(ins, outs, sems):
        cps = copies(ins, outs, sems)
        passed = cps["passed"]()
        for got, on in zip(cps["landing"](), passed):
            got.wait_recv()
            on.start()
        for cp in cps["from_sib"]():
            cp.wait_recv()
        for cp in cps["sends"]() + passed:
            cp.wait_send()
        for cp in cps["local"]():
            cp.wait()

    return _Comm(arrays, [jax.ShapeDtypeStruct((4,) + a.shape, a.dtype) for a in arrays],
                 [pltpu.SemaphoreType.DMA((6 * n,)), pltpu.SemaphoreType.DMA((6 * n,)),
                  pltpu.SemaphoreType.DMA((len(parts),))], start, wait)


def _exchange_chips(pieces, whole):
    n, m = len(pieces), len(whole)
    parts = [(a, blk) for a, arr in enumerate(pieces) for blk in _copy_chunks(*arr.shape[1:])]

    def copies(ins, outs, sems):
        send_sems, recv_sems, local_sems = sems
        x, y, c = _place()
        chip, dev = 2 * x + y, 4 * x + 2 * y + c
        chips = [(1 - x, y), (x, 1 - y), (1 - x, 1 - y)]
        peers = [(x ^ (k >> 2), y ^ ((k >> 1) & 1), c ^ (k & 1)) for k in range(1, 8)]
        def local():
            cps = [pltpu.make_async_copy(ins[a].at[(chip, *blk)], outs[a].at[(chip, *blk)], local_sems.at[p])
                   for p, (a, blk) in enumerate(parts)]
            return cps + [pltpu.make_async_copy(ins[n + b], outs[n + b].at[dev], local_sems.at[len(parts) + b])
                          for b in range(m)]

        def piece(a, j, slot_from):
            px, py = chips[j]
            return pltpu.make_async_remote_copy(
                src_ref=ins[a].at[2 * px + py], dst_ref=outs[a].at[slot_from], send_sem=send_sems.at[3 * a + j],
                recv_sem=recv_sems.at[3 * a + j], device_id=(px, py, c), device_id_type=MESH)

        def everyone(b, j, slot_from):
            px, py, pc = peers[j]
            return pltpu.make_async_remote_copy(
                src_ref=ins[n + b], dst_ref=outs[n + b].at[slot_from], send_sem=send_sems.at[3 * n + 7 * b + j],
                recv_sem=recv_sems.at[3 * n + 7 * b + j], device_id=(px, py, pc), device_id_type=MESH)

        def sends():
            return ([everyone(b, j, dev) for b in range(m) for j in range(7)]
                    + [piece(a, j, chip) for a in range(n) for j in range(3)])

        def landing():
            return ([everyone(b, j, 4 * px + 2 * py + pc) for b in range(m) for j, (px, py, pc) in enumerate(peers)]
                    + [piece(a, j, 2 * px + py) for a in range(n) for j, (px, py) in enumerate(chips)])

        return local, sends, landing

    def start(ins, outs, sems):
        local, sends, _ = copies(ins, outs, sems)
        for cp in local() + sends():
            cp.start()

    def wait(ins, outs, sems):
        local, sends, landing = copies(ins, outs, sems)
        for cp in landing():
            cp.wait_recv()
        for cp in sends():
            cp.wait_send()
        for cp in local():
            cp.wait()

    shapes = [jax.ShapeDtypeStruct(a.shape, a.dtype) for a in pieces]
    shapes += [jax.ShapeDtypeStruct((8,) + a.shape, a.dtype) for a in whole]
    nsem = 3 * n + 7 * m
    return _Comm(list(pieces) + list(whole), shapes,
                 [pltpu.SemaphoreType.DMA((nsem,)), pltpu.SemaphoreType.DMA((nsem,)),
                  pltpu.SemaphoreType.DMA((len(parts) + m,))], start, wait)


def _share_sibling(name, arrays):
    n = len(arrays)
    parts = []
    for a, arr in enumerate(arrays):
        k = SHARE_CHUNKS if arr.shape[0] % (SHARE_CHUNKS * SUBLANES) == 0 else 1
        k = arr.shape[0] if arr.ndim == 3 else k
        step = arr.shape[0] // k
        parts += [(a, q * step, step) for q in range(k)]
    npart = len(parts)

    def body(*refs):
        ins, outs = refs[:n], refs[n:2 * n]
        send_sems, recv_sems = refs[2 * n:]
        x, y, c = _place()

        def copy(p):
            a, r0, nr = parts[p]
            return pltpu.make_async_remote_copy(
                src_ref=ins[a].at[pl.ds(r0, nr)], dst_ref=outs[a].at[pl.ds(r0, nr)], send_sem=send_sems.at[p],
                recv_sem=recv_sems.at[p], device_id=(x, y, 1 - c), device_id_type=MESH)

        copies = [copy(p) for p in range(npart)]
        for cp in copies:
            cp.start()
        for cp in copies:
            cp.wait_recv()
        for cp in copies:
            cp.wait_send()

    return pl.pallas_call(
        body, name=name,
        in_specs=[ANY] * n, out_specs=[ANY] * n,
        out_shape=[jax.ShapeDtypeStruct(a.shape, a.dtype) for a in arrays],
        scratch_shapes=[pltpu.SemaphoreType.DMA((npart,)), pltpu.SemaphoreType.DMA((npart,))],
    )(*arrays)


def _place_blocks(blocks, axis):
    shape = list(blocks[0].shape)
    shape[axis] = sum(b.shape[axis] for b in blocks)
    buf = lax.empty(tuple(shape), blocks[0].dtype)
    at = 0
    for b in blocks:
        buf = lax.dynamic_update_slice_in_dim(buf, b, at, axis)
        at += b.shape[axis]
    return buf


def kernel(x, norm_pre_g, w_in, mu_shift, w0, w_lora_up, a0, a_lora_up, k_k, k_a, r_k, lnx_g, lnx_b, conv_w, conv_b, cln_g, cln_b, w_pw2, b_pw2, w_out, norm_post_g, loss_target, m_norm_pre_g, m_w_in, m_mu_shift, m_w0, m_w_lora_up, m_a0, m_a_lora_up, m_k_k, m_k_a, m_r_k, m_lnx_g, m_lnx_b, m_conv_w, m_conv_b, m_cln_g, m_cln_b, m_w_pw2, m_b_pw2, m_w_out, m_norm_post_g, v_norm_pre_g, v_w_in, v_mu_shift, v_w0, v_w_lora_up, v_a0, v_a_lora_up, v_k_k, v_k_a, v_r_k, v_lnx_g, v_lnx_b, v_conv_w, v_conv_b, v_cln_g, v_cln_b, v_w_pw2, v_b_pw2, v_w_out, v_norm_post_g):
    _, T, D = x.shape
    RW = w0.shape[0]
    CW = conv_b.shape[0]
    head = r_k.shape[1]
    lora = w_lora_up.shape[0]
    ktaps = conv_w.shape[0]
    assert RW == CW and 2 * lora <= LORA_PAD and ktaps - 1 <= CONV_HALO
    n_in = 3 * RW + 2 * lora + RW + 3 * CW
    shard = n_in // 4
    PW = 7 * RW + LORA_PAD
    off_l = 7 * RW
    tm = min(256, T // 2)
    tm_wide = min(128, T // 2)
    row = lambda vec: vec.reshape(1, -1)
    x2, tgt2 = x[0], loss_target[0]

    halves = lambda a: a.reshape(2, a.shape[0] // 2, a.shape[1])
    conv_w_p = jnp.concatenate([conv_w, jnp.zeros((CONV_HALO - ktaps, CW // 4), f32)], axis=0)
    w_in_t, m_w_in_t, v_w_in_t = w_in.T, m_w_in.T, v_w_in.T
    (g_win,) = _run_comm("gather_w_in", _gather_chips([halves(w_in_t.astype(bf16))]))
    win_t = g_win.reshape(n_in, D)
    lo = 3 * RW
    wp_t = _place_blocks([win_t[:lo], win_t[lo + 2 * lora:], win_t[lo:lo + 2 * lora],
                          jnp.zeros((LORA_PAD - 2 * lora, D), bf16)], axis=0)
    npg = row(norm_pre_g)
    (h,) = _row_fwd("rms_pre", _fn_rms_pre, [(npg, False)], [(x2, 0, D, False)], [(D, bf16)], T, tm)
    others = [halves(a) for a in (w_lora_up, a_lora_up, conv_w_p, w_pw2.astype(bf16), w_out.astype(bf16))]
    proj, (g_wup, g_aup, g_cw, g_pw2, g_wout) = _matmul("proj", h, wp_t, "nt", f32, comm=_gather_chips(others))
    cat_cols = lambda g: jnp.concatenate([g[s].reshape(-1, g.shape[-1]) for s in range(4)], axis=1)
    wup_full, aup_full, cw_p = cat_cols(g_wup), cat_cols(g_aup), cat_cols(g_cw)
    zl = lambda n: jnp.zeros((n, RW), f32)
    wup_p = jnp.concatenate([wup_full, zl(LORA_PAD - lora)], axis=0)
    aup_p = jnp.concatenate([zl(lora), aup_full, zl(LORA_PAD - 2 * lora)], axis=0)
    pw2_full = g_pw2.reshape(CW, CW)
    wout_full = g_wout.reshape(RW + CW, D)
    mu_r, mu_k, mu_v = (row(mu_shift[s * RW:(s + 1) * RW]) for s in range(3))
    mu_l = row(jnp.concatenate([mu_shift[3 * RW:], jnp.zeros((LORA_PAD - 2 * lora,), f32)]))

    xs_r = _shift_fwd("shift_r", proj, 0, RW, mu_r, T, tm)
    xs_k = _shift_fwd("shift_k", proj, RW, RW, mu_k, T, tm)
    xs_v = _shift_fwd("shift_v", proj, 2 * RW, RW, mu_v, T, tm)
    xs_l = _shift_fwd("shift_l", proj, off_l, LORA_PAD, mu_l, T, tm)
    lora_params = [(row(w0), False), (wup_p, False), (row(a0), False), (aup_p, False)]
    qw, qa = _row_fwd("lora_up", _fn_lora, lora_params, [(xs_l, 0, LORA_PAD, False)], [(RW, f32), (RW, f32)], T, tm)
    ncol = RW // _tile(RW, 512)
    fn_pre = functools.partial(_fn_rwkv_pre, head)
    pre_params = [(row(k_k), True), (row(k_a), True)]
    pre_rows = [(xs_k, 0, RW, True), (qw, 0, RW, True), (qa, 0, RW, True)]
    lw, k_h, a_rec, b_rec = _row_fwd("rwkv_pre", fn_pre, pre_params, pre_rows, [(RW, f32)] * 4, T, tm, ncol)
    wkv_in = (xs_r, lw, k_h, xs_v, a_rec, b_rec)
    c_w, c_u, c_rt, c_pb, c_y0, c_bend, c_z = _wkv_local_fwd(*wkv_in, head, T, RW)
    wkv_loc = (c_w, c_u, c_rt, c_pb, c_bend, lw, c_y0, c_z)
    y_wkv, states = _wkv_state_fwd(*wkv_loc, head, T, RW)
    fn_post = functools.partial(_fn_rwkv_post, head)
    post_params = [(row(lnx_g), True), (row(lnx_b), True), (r_k.reshape(1, RW), True)]
    post_rows = [(y_wkv, 0, RW, True), (xs_r, 0, RW, True), (k_h, 0, RW, True), (xs_v, 0, RW, True),
                 (proj, 3 * RW, RW, True)]
    (y_rwkv,) = _row_fwd("rwkv_post", fn_post, post_params, post_rows, [(RW, bf16)], T, tm, ncol)

    c_pre = _conv_fwd(proj, 4 * RW, 5 * RW, CW, cw_p, row(conv_b), ktaps, T, tm)
    ln_params = [(row(cln_g), False), (row(cln_b), False)]
    (c_act,) = _row_fwd("conv_ln", _fn_conv_ln, ln_params, [(c_pre, 0, CW, False)], [(CW, bf16)], T, tm)
    c2 = _matmul("pw2", c_act, pw2_full, "nn", f32)
    cpost_params = [(row(b_pw2), True)]
    cpost_rows = [(c2, 0, CW, True), (proj, 6 * RW, CW, True)]
    (y_conv,) = _row_fwd("conv_post", _fn_conv_post, cpost_params, cpost_rows, [(CW, bf16)], T, tm, ncol)

    mix = jnp.concatenate([y_rwkv, y_conv], axis=1)
    out = _matmul("out_proj", mix, wout_full, "nn", f32)
    d_out, gx_res, loss_part, g_npost = _post(out, x2, tgt2, row(norm_post_g), T, D, tm_wide)

    g_wout_full = _matmul("d_w_out", mix, d_out, "tn", bf16)
    d_mix = _matmul("d_mix", d_out, wout_full, "nt", f32)

    d_c2, d_gconv, g_bpw2 = _row_bwd("conv_post_bwd", _fn_conv_post, cpost_params, cpost_rows,
                                      [(d_mix, RW, CW, True)], [bf16, bf16], T, tm, ncol)
    g_pw2_full = _matmul("d_w_pw2", c_act, d_c2, "tn", bf16)
    d_cact = _matmul("d_c_act", d_c2, pw2_full, "nt", f32)
    d_cpre, g_clng, g_clnb = _row_bwd("conv_ln_bwd", _fn_conv_ln, ln_params, [(c_pre, 0, CW, False)],
                                      [(d_cact, 0, CW, False)], [f32], T, tm)
    d_gluv, d_glug, g_cw_p, g_cb = _conv_bwd(proj, 4 * RW, 5 * RW, CW, cw_p, d_cpre, ktaps, T, tm)

    d_y, dr_x, dk_x, dv_x, d_grwkv, g_lnxg, g_lnxb, g_rk = _row_bwd(
        "rwkv_post_bwd", fn_post, post_params, post_rows, [(d_mix, 0, RW, True)], [f32, f32, f32, f32, bf16], T, tm, ncol)
    d_cw, d_cu, d_crt, d_cpb, d_cbend, d_lw_dec, d_cz = _wkv_state_bwd(*wkv_loc, states, d_y, head, T, RW)
    d_xr, d_lw, d_kh, d_xv, d_a, d_b = _wkv_local_bwd(
        *wkv_in, (d_cw, d_cu, d_crt, d_cpb, d_y, d_cbend, d_cz), d_lw_dec, dr_x, dk_x, dv_x, head, T, RW)
    pre_cots = [(d_lw, 0, RW, True), (d_kh, 0, RW, True), (d_a, 0, RW, True), (d_b, 0, RW, True)]
    d_xk, d_qw, d_qa, g_kk, g_ka = _row_bwd("rwkv_pre_bwd", fn_pre, pre_params, pre_rows, pre_cots, [f32, f32, f32],
                                            T, tm, ncol)
    d_xl, g_w0, g_wup_p, g_a0, g_aup_p = _row_bwd("lora_up_bwd", _fn_lora, lora_params, [(xs_l, 0, LORA_PAD, False)],
                                                  [(d_qw, 0, RW, False), (d_qa, 0, RW, False)], [f32], T, tm)
    dp_r, g_mur = _shift_bwd("shift_r_bwd", proj, 0, RW, mu_r, d_xr, T, tm)
    dp_k, g_muk = _shift_bwd("shift_k_bwd", proj, RW, RW, mu_k, d_xk, T, tm)
    dp_v, g_muv = _shift_bwd("shift_v_bwd", proj, 2 * RW, RW, mu_v, d_xv, T, tm)
    dp_l, g_mul = _shift_bwd("shift_l_bwd", proj, off_l, LORA_PAD, mu_l, d_xl, T, tm)
    d_proj = _place_blocks([dp_r, dp_k, dp_v, d_grwkv, d_gluv, d_glug, d_gconv, dp_l], axis=1)

    south = lax.axis_index("c") == 0
    flat = lambda a: a.reshape(-1, a.shape[-1])

    def keep_give(a):
        a = a.reshape(4, 2, a.shape[1] // 2, a.shape[2])
        return jnp.where(south, a[:, 0], a[:, 1]), jnp.where(south, a[:, 1], a[:, 0])

    def chip_sums(tag, shards):
        kept, given = zip(*[keep_give(a) for a in shards])
        got = _share_sibling("pair_exchange_" + tag, list(given))
        return [_add_pair("chip_sum_%s_%d" % (tag, i), flat(k), flat(g)).reshape(k.shape)
                for i, (k, g) in enumerate(zip(kept, got))]

    def both_halves(own, other):
        return jnp.concatenate([jnp.where(south, own, other), jnp.where(south, other, own)], axis=0)

    def all_chips(tag, slots):
        sums = [_sum_slots("sum_%s_%d" % (tag, i), r) for i, r in enumerate(slots)]
        others = _share_sibling("share_" + tag, sums)
        return [both_halves(s, o) for s, o in zip(sums, others)]

    q_early = chip_sums("early", [g_wout_full.reshape(4, (RW + CW) // 4, D), g_pw2_full.reshape(4, CW // 4, CW)])
    g_wp_t, r_early = _matmul("d_w_in", d_proj, h, "tn", bf16, comm=_exchange_chips(q_early, []))
    g_win_t = _place_blocks([g_wp_t[:lo], g_wp_t[off_l:off_l + 2 * lora], g_wp_t[lo:off_l]], axis=0)
    col_shards = lambda a: a.reshape(a.shape[0], 4, a.shape[1] // 4).transpose(1, 0, 2)
    q_late = chip_sums("late", [g_win_t.reshape(4, shard, D), col_shards(g_wup_p[:lora]),
                                col_shards(g_aup_p[lora:2 * lora]), col_shards(g_cw_p)])
    g_mu = jnp.concatenate([g_mur[0], g_muk[0], g_muv[0], g_mul[0, :2 * lora]])
    pad_rows = lambda a, n: jnp.concatenate([a, jnp.zeros((n - a.shape[0], a.shape[1]), f32)], axis=0)
    n_mu = -(-mu_shift.shape[0] // RW)
    small_vecs = [pad_rows(jnp.pad(g_mu, (0, n_mu * RW - g_mu.shape[0])).reshape(n_mu, RW), n_mu),
                  g_w0, g_a0, g_kk, g_ka, g_rk, g_lnxg, g_lnxb, g_cb, g_clng, g_clnb, g_bpw2,
                  g_npost.reshape(D // RW, RW)]
    n_small = sum(a.shape[0] for a in small_vecs)
    n_small_pad = -(-n_small // (2 * SUBLANES)) * (2 * SUBLANES)
    small = pad_rows(jnp.concatenate(small_vecs, axis=0), n_small_pad)

    d_h, r_late = _matmul("d_h", d_proj, wp_t, "nn", bf16, tk_t=2560, comm=_exchange_chips(q_late, [small]))
    grad_x2, g_npre = _rms_pre_bwd(x2, npg, d_h, gx_res, T, D, tm_wide)
    (r_npre,) = _run_comm("exchange_norm_pre", _exchange_chips([], [g_npre.reshape(D // RW, RW)]))
    s_npre = _sum_slots("sum_norm_pre", r_npre)
    s_small = _sum_slots("sum_small", r_late[4])
    grad_w_out, grad_w_pw2 = all_chips("early", r_early)
    grad_w_in, grad_wup, grad_aup, grad_cw = all_chips("late", r_late[:4])

    pos = [0]

    def take(nrows):
        a = s_small[pos[0]:pos[0] + nrows]
        pos[0] += nrows
        return a

    grads = {}
    grads["norm_pre_g"] = s_npre.reshape(D)
    grads["mu_shift"] = take(n_mu).reshape(-1)[:mu_shift.shape[0]]
    for nm in ["w0", "a0", "k_k", "k_a"]:
        grads[nm] = take(1).reshape(RW)
    grads["r_k"] = take(1).reshape(r_k.shape)
    for nm in ["lnx_g", "lnx_b", "conv_b", "cln_g", "cln_b", "b_pw2"]:
        grads[nm] = take(1).reshape(RW)
    grads["norm_post_g"] = take(D // RW).reshape(D)
    grads["w_lora_up"], grads["a_lora_up"], grads["conv_w"] = grad_wup, grad_aup, grad_cw[:ktaps]
    grads["w_in"], grads["w_out"], grads["w_pw2"] = grad_w_in, grad_w_out, grad_w_pw2

    weights = dict(norm_pre_g=norm_pre_g, w_in=w_in, mu_shift=mu_shift, w0=w0, w_lora_up=w_lora_up, a0=a0,
                   a_lora_up=a_lora_up, k_k=k_k, k_a=k_a, r_k=r_k, lnx_g=lnx_g, lnx_b=lnx_b, conv_w=conv_w,
                   conv_b=conv_b, cln_g=cln_g, cln_b=cln_b, w_pw2=w_pw2, b_pw2=b_pw2, w_out=w_out,
                   norm_post_g=norm_post_g)
    ms = dict(norm_pre_g=m_norm_pre_g, w_in=m_w_in, mu_shift=m_mu_shift, w0=m_w0, w_lora_up=m_w_lora_up, a0=m_a0,
              a_lora_up=m_a_lora_up, k_k=m_k_k, k_a=m_k_a, r_k=m_r_k, lnx_g=m_lnx_g, lnx_b=m_lnx_b, conv_w=m_conv_w,
              conv_b=m_conv_b, cln_g=m_cln_g, cln_b=m_cln_b, w_pw2=m_w_pw2, b_pw2=m_b_pw2, w_out=m_w_out,
              norm_post_g=m_norm_post_g)
    vs = dict(norm_pre_g=v_norm_pre_g, w_in=v_w_in, mu_shift=v_mu_shift, w0=v_w0, w_lora_up=v_w_lora_up, a0=v_a0,
              a_lora_up=v_a_lora_up, k_k=v_k_k, k_a=v_k_a, r_k=v_r_k, lnx_g=v_lnx_g, lnx_b=v_lnx_b, conv_w=v_conv_w,
              conv_b=v_conv_b, cln_g=v_cln_g, cln_b=v_cln_b, w_pw2=v_w_pw2, b_pw2=v_b_pw2, w_out=v_w_out,
              norm_post_g=v_norm_post_g)
    names = list(weights)
    big = ["w_in", "w_out", "w_pw2"]
    deltas, new_m, new_v = {}, {}, {}
    d_t, m_t, v_t = _adamw("adamw_w_in", w_in_t, grad_w_in, m_w_in_t, v_w_in_t)
    grads["w_in"], deltas["w_in"], new_m["w_in"], new_v["w_in"] = grad_w_in.T, d_t.T, m_t.T, v_t.T
    for nm in big[1:]:
        deltas[nm], new_m[nm], new_v[nm] = _adamw("adamw_" + nm, weights[nm], grads[nm], ms[nm], vs[nm])
    rest = [nm for nm in names if nm not in big]
    sizes = [weights[nm].size for nm in rest]
    total = sum(sizes)
    width = 4 * LANES
    rows_p = -(-total // (width * SUBLANES)) * SUBLANES

    def pack(d):
        flat = jnp.concatenate([d[nm].reshape(-1) for nm in rest])
        return jnp.pad(flat, (0, rows_p * width - total)).reshape(rows_p, width)

    p_d, p_m, p_v = _adamw("adamw_small", pack(weights), pack(grads), pack(ms), pack(vs))
    o = 0
    for nm, sz in zip(rest, sizes):
        shp = weights[nm].shape
        deltas[nm] = p_d.reshape(-1)[o:o + sz].reshape(shp)
        new_m[nm] = p_m.reshape(-1)[o:o + sz].reshape(shp)
        new_v[nm] = p_v.reshape(-1)[o:o + sz].reshape(shp)
        o += sz

    loss = lax.psum(loss_part[0, 0], ("x", "y", "c"))
    grad_x = grad_x2[None]
    return (loss, grad_x, *[grads[nm] for nm in names], *[deltas[nm] for nm in names],
            *[new_m[nm] for nm in names], *[new_v[nm] for nm in names])
```

```python
import functools

import jax
import jax.numpy as jnp
from jax import lax
from jax.experimental import pallas as pl
from jax.experimental.pallas import tpu as pltpu

f32 = jnp.float32
bf16 = jnp.bfloat16
MESH = pl.DeviceIdType.MESH

NORM_EPS = 1e-6
LN_EPS = 1e-5
ADAM_LR, ADAM_B1, ADAM_B2, ADAM_EPS, ADAM_WD, ADAM_STEP = 0.001, 0.9, 0.999, 1e-08, 0.01, 10

LANES = 128
SUBLANES = 8
LORA_PAD = 256
CONV_HALO = 32
WKV_CHUNK = 64
WKV_HEADS = 16
WKV_STATE_HEADS = 32
WKV_PREC = lax.Precision.HIGH
SHARE_CHUNKS = 8
VMEM_LIMIT = 56 * 1024 * 1024


def _cparams(n_axes):
    return pltpu.CompilerParams(dimension_semantics=("arbitrary",) * n_axes, vmem_limit_bytes=VMEM_LIMIT)


def _tile(dim, target):
    best = None
    t = LANES
    while t <= min(dim, target):
        if dim % t == 0:
            best = t
        t += LANES
    return best if best is not None else dim


def _mm(a, b, prec=None):
    return lax.dot_general(a, b, (((1,), (0,)), ((), ())), precision=prec, preferred_element_type=f32)


def _mm_nt(a, b, prec=None):
    return lax.dot_general(a, b, (((1,), (1,)), ((), ())), precision=prec, preferred_element_type=f32)


def _mm_tn(a, b, prec=None):
    return lax.dot_general(a, b, (((0,), (0,)), ((), ())), precision=prec, preferred_element_type=f32)


@jax.custom_vjp
def _bmm(a, b):
    return _mm(a.astype(bf16), b.astype(bf16))


def _bmm_fwd(a, b):
    return _bmm(a, b), (a, b)


def _bmm_bwd(res, dc):
    a, b = res
    dcb = dc.astype(bf16)
    return _mm_nt(dcb, b.astype(bf16)), _mm_tn(a.astype(bf16), dcb)


_bmm.defvjp(_bmm_fwd, _bmm_bwd)


def _matmul(name, a, b, mode, out_dtype, tm_t=1024, tn_t=1024, tk_t=4096, comm=None):
    if mode == "nn":
        (M, K), (_, N) = a.shape, b.shape
    elif mode == "nt":
        (M, K), (N, _) = a.shape, b.shape
    else:
        (K, M), (_, N) = a.shape, b.shape
    tm, tn, tk = _tile(M, tm_t), _tile(N, tn_t), _tile(K, tk_t)
    ni, nj, nk = M // tm, N // tn, K // tk
    dot = {"nn": _mm, "nt": _mm_nt, "tn": _mm_tn}[mode]
    nc = len(comm.operands) if comm else 0

    def body(*refs):
        a_ref, b_ref = refs[:2]
        o_ref = refs[2 + nc]
        scratch = refs[3 + 2 * nc:]
        i, j, k = pl.program_id(0), pl.program_id(1), pl.program_id(2)
        if comm:
            comm_refs = (refs[2:2 + nc], refs[3 + nc:3 + 2 * nc], scratch[:len(comm.scratch)])

            @pl.when(jnp.logical_and(jnp.logical_and(i == 0, j == 0), k == 0))
            def _():
                comm.start(*comm_refs)

        if nk == 1:
            o_ref[...] = dot(a_ref[...], b_ref[...]).astype(o_ref.dtype)
        else:
            acc_ref = scratch[-1]

            @pl.when(k == 0)
            def _():
                acc_ref[...] = jnp.zeros_like(acc_ref)

            acc_ref[...] += dot(a_ref[...], b_ref[...])

            @pl.when(k == nk - 1)
            def _():
                o_ref[...] = acc_ref[...].astype(o_ref.dtype)

        if comm:
            @pl.when(jnp.logical_and(jnp.logical_and(i == ni - 1, j == nj - 1), k == nk - 1))
            def _():
                comm.wait(*comm_refs)

    a_spec = {"nn": pl.BlockSpec((tm, tk), lambda i, j, k: (i, k)),
              "nt": pl.BlockSpec((tm, tk), lambda i, j, k: (i, k)),
              "tn": pl.BlockSpec((tk, tm), lambda i, j, k: (k, i))}[mode]
    b_spec = {"nn": pl.BlockSpec((tk, tn), lambda i, j, k: (k, j)),
              "nt": pl.BlockSpec((tn, tk), lambda i, j, k: (j, k)),
              "tn": pl.BlockSpec((tk, tn), lambda i, j, k: (k, j))}[mode]
    res = pl.pallas_call(
        body, name=name, grid=(ni, nj, nk),
        in_specs=[a_spec, b_spec] + [ANY] * nc,
        out_specs=[pl.BlockSpec((tm, tn), lambda i, j, k: (i, j))] + [ANY] * nc,
        out_shape=[jax.ShapeDtypeStruct((M, N), out_dtype)] + (list(comm.out_shape) if comm else []),
        scratch_shapes=(list(comm.scratch) if comm else []) + ([pltpu.VMEM((tm, tn), f32)] if nk > 1 else []),
        compiler_params=_cparams(3),
    )(a, b, *(comm.operands if comm else []))
    return (res[0], res[1:]) if comm else res[0]


def _row_spec(op, tm, ncol):
    arr, off, width, tiled = op
    if tiled:
        cw = width // ncol
        return pl.BlockSpec((tm, cw), lambda j, i: (i, off // cw + j))
    return pl.BlockSpec((tm, width), lambda j, i: (i, off // width))


def _param_spec(p, ncol):
    arr, tiled = p
    rows, width = arr.shape
    if tiled:
        return pl.BlockSpec((rows, width // ncol), lambda j, i: (0, j))
    return pl.BlockSpec((rows, width), lambda j, i: (0, 0))


def _row_fwd(name, fn, params, rows, outs, T, tm, ncol=1):
    npar, nrow = len(params), len(rows)

    def body(*refs):
        pv = [r[...] for r in refs[:npar]]
        rv = [r[...].astype(f32) for r in refs[npar:npar + nrow]]
        res = fn(*pv, *rv)
        for o_ref, val in zip(refs[npar + nrow:], res):
            o_ref[...] = val.astype(o_ref.dtype)

    return pl.pallas_call(
        body, name=name, grid=(ncol, T // tm),
        in_specs=[_param_spec(p, ncol) for p in params] + [_row_spec(r, tm, ncol) for r in rows],
        out_specs=[pl.BlockSpec((tm, w // ncol), lambda j, i: (i, j)) for w, _ in outs],
        out_shape=[jax.ShapeDtypeStruct((T, w), dt) for w, dt in outs],
        compiler_params=_cparams(2),
    )(*[p[0] for p in params], *[r[0] for r in rows])


def _row_bwd(name, fn, params, rows, cots, row_grads, T, tm, ncol=1):
    npar, nrow, ncot = len(params), len(rows), len(cots)
    want = [k for k, dt in enumerate(row_grads) if dt is not None]

    def body(*refs):
        pv = [r[...] for r in refs[:npar]]
        rv = [r[...].astype(f32) for r in refs[npar:npar + nrow]]
        cv = tuple(r[...].astype(f32) for r in refs[npar + nrow:npar + nrow + ncot])
        out_refs = refs[npar + nrow + ncot:]
        _, vjp = jax.vjp(fn, *pv, *rv)
        grads = vjp(cv)
        for o_ref, k in zip(out_refs[:len(want)], want):
            o_ref[...] = grads[npar + k].astype(o_ref.dtype)
        j, i = pl.program_id(0), pl.program_id(1)
        for o_ref, p, g in zip(out_refs[len(want):], params, grads[:npar]):
            first = (i == 0) if p[1] else jnp.logical_and(i == 0, j == 0)

            @pl.when(first)
            def _():
                o_ref[...] = jnp.zeros_like(o_ref)

            o_ref[...] += g

    def grad_spec(op):
        arr, off, width, tiled = op
        if tiled:
            return pl.BlockSpec((tm, width // ncol), lambda j, i: (i, j)), (T, width)
        return pl.BlockSpec((tm, width), lambda j, i: (i, j)), (T, width * ncol)

    gspecs = [grad_spec(rows[k]) for k in want]
    return pl.pallas_call(
        body, name=name, grid=(ncol, T // tm),
        in_specs=[_param_spec(p, ncol) for p in params] + [_row_spec(r, tm, ncol) for r in rows]
        + [_row_spec(c, tm, ncol) for c in cots],
        out_specs=[s for s, _ in gspecs] + [_param_spec(p, ncol) for p in params],
        out_shape=[jax.ShapeDtypeStruct(shp, row_grads[k]) for (_, shp), k in zip(gspecs, want)]
        + [jax.ShapeDtypeStruct(p[0].shape, f32) for p in params],
        compiler_params=_cparams(2),
    )(*[p[0] for p in params], *[r[0] for r in rows], *[c[0] for c in cots])


def _seg_sum(x, head):
    li = lax.broadcasted_iota(jnp.int32, (LANES, LANES), 0) // head
    lj = lax.broadcasted_iota(jnp.int32, (LANES, LANES), 1) // head
    q = (li == lj).astype(f32)
    parts = [_mm(x[:, s:s + LANES], q, lax.Precision.HIGH) for s in range(0, x.shape[1], LANES)]
    return parts[0] if len(parts) == 1 else jnp.concatenate(parts, axis=1)


def _sigmoid(z):
    return 1.0 / (1.0 + jnp.exp(-z))


def _silu(z):
    return z * _sigmoid(z)


def _rms(g, x):
    return x * lax.rsqrt(jnp.mean(x * x, axis=-1, keepdims=True) + NORM_EPS) * g


def _fn_rms_pre(g, x):
    return (_rms(g, x),)


def _fn_lora(w0, wup, a0, aup, xl):
    qw = w0 + _bmm(jnp.tanh(xl), wup)
    qa = a0 + _bmm(xl, aup)
    return qw, qa


def _fn_rwkv_pre(head, k_k, k_a, xk, qw, qa):
    w_log = -(jnp.maximum(-qw, 0.0) + jnp.log(1.0 + jnp.exp(-jnp.abs(qw)))) - 0.5
    lw = -jnp.exp(w_log)
    a_sig = _sigmoid(qa)
    kk = xk * k_k
    kk = kk / jnp.maximum(jnp.sqrt(_seg_sum(kk * kk, head)), 1e-12)
    k_h = xk * (1.0 + (a_sig - 1.0) * k_a)
    return lw, k_h, -kk, kk * a_sig


def _fn_rwkv_post(head, lnx_g, lnx_b, r_k, y, r, k_h, v, g):
    inv = 1.0 / head
    mu = _seg_sum(y, head) * inv
    d = y - mu
    var = _seg_sum(d * d, head) * inv
    yn = d * lax.rsqrt(var + 1e-5 * head) * lnx_g + lnx_b
    bonus = _seg_sum(r * k_h * r_k, head) * v
    return ((yn + bonus) * _silu(g),)


def _fn_conv_ln(cln_g, cln_b, c):
    mu = jnp.mean(c, axis=-1, keepdims=True)
    d = c - mu
    var = jnp.mean(d * d, axis=-1, keepdims=True)
    return (_silu(d * lax.rsqrt(var + LN_EPS) * cln_g + cln_b),)


def _fn_conv_post(b_pw2, c2, g):
    return ((c2 + b_pw2) * _silu(g),)


def _post(out, x, tgt, g, T, D, tm):
    def body(g_ref, o_ref, x_ref, t_ref, dout_ref, gx_ref, loss_ref, dg_ref):
        i = pl.program_id(0)
        o, vjp = jax.vjp(_rms, g_ref[...], o_ref[...])
        err = x_ref[...] + o - t_ref[...]
        d_y = err * (1.0 / D)
        dg, d_out = vjp(d_y)
        dout_ref[...] = d_out.astype(dout_ref.dtype)
        gx_ref[...] = d_y

        @pl.when(i == 0)
        def _():
            loss_ref[...] = jnp.zeros_like(loss_ref)
            dg_ref[...] = jnp.zeros_like(dg_ref)

        loss_ref[...] += jnp.sum(err * err, keepdims=True) * (0.5 / D)
        dg_ref[...] += dg

    row = pl.BlockSpec((tm, D), lambda i: (i, 0))
    vec = pl.BlockSpec((1, D), lambda i: (0, 0))
    return pl.pallas_call(
        body, name="post_loss", grid=(T // tm,),
        in_specs=[vec, row, row, row],
        out_specs=[row, row, pl.BlockSpec((1, 1), lambda i: (0, 0)), vec],
        out_shape=[jax.ShapeDtypeStruct((T, D), bf16), jax.ShapeDtypeStruct((T, D), f32),
                   jax.ShapeDtypeStruct((1, 1), f32), jax.ShapeDtypeStruct((1, D), f32)],
        compiler_params=_cparams(1),
    )(g, out, x, tgt)


def _rms_pre_bwd(x, g, dh, gx_res, T, D, tm):
    def body(g_ref, x_ref, dh_ref, res_ref, dx_ref, dg_ref):
        i = pl.program_id(0)
        _, vjp = jax.vjp(_rms, g_ref[...], x_ref[...])
        dg, dx = vjp(dh_ref[...].astype(f32))
        dx_ref[...] = dx + res_ref[...]

        @pl.when(i == 0)
        def _():
            dg_ref[...] = jnp.zeros_like(dg_ref)

        dg_ref[...] += dg

    row = pl.BlockSpec((tm, D), lambda i: (i, 0))
    vec = pl.BlockSpec((1, D), lambda i: (0, 0))
    return pl.pallas_call(
        body, name="rms_pre_bwd", grid=(T // tm,),
        in_specs=[vec, row, row, row], out_specs=[row, vec],
        out_shape=[jax.ShapeDtypeStruct((T, D), f32), jax.ShapeDtypeStruct((1, D), f32)],
        compiler_params=_cparams(1),
    )(g, x, dh, gx_res)


def _prev_rows(cur, halo_ref, first):
    top = jnp.where(first, 0.0, halo_ref[SUBLANES - 1:SUBLANES, :])
    rolled = pltpu.roll(cur, 1, 0)
    rid = lax.broadcasted_iota(jnp.int32, cur.shape, 0)
    return jnp.where(rid == 0, top, rolled)


def _shift_fwd(name, proj, off, width, mu, T, tm):
    cw = _tile(width, 512)
    ncol, cb = width // cw, off // cw
    hb = tm // SUBLANES

    def body(mu_ref, cur_ref, halo_ref, o_ref):
        i = pl.program_id(1)
        cur = cur_ref[...]
        prev = _prev_rows(cur, halo_ref, i == 0)
        o_ref[...] = cur + (prev - cur) * mu_ref[...]

    return pl.pallas_call(
        body, name=name, grid=(ncol, T // tm),
        in_specs=[pl.BlockSpec((1, cw), lambda j, i: (0, j)),
                  pl.BlockSpec((tm, cw), lambda j, i: (i, cb + j)),
                  pl.BlockSpec((SUBLANES, cw), lambda j, i: (jnp.maximum(i * hb - 1, 0), cb + j))],
        out_specs=pl.BlockSpec((tm, cw), lambda j, i: (i, j)),
        out_shape=jax.ShapeDtypeStruct((T, width), f32),
        compiler_params=_cparams(2),
    )(mu, proj, proj)


def _shift_bwd(name, proj, off, width, mu, dxs, T, tm):
    cw = _tile(width, 512)
    ncol, cb = width // cw, off // cw
    hb, nt = tm // SUBLANES, T // tm

    def body(mu_ref, cur_ref, halo_ref, d_ref, dnext_ref, o_ref, dmu_ref):
        i = pl.program_id(1)
        cur = cur_ref[...]
        prev = _prev_rows(cur, halo_ref, i == 0)
        d = d_ref[...]
        bottom = jnp.where(i == nt - 1, 0.0, dnext_ref[0:1, :])
        rid = lax.broadcasted_iota(jnp.int32, d.shape, 0)
        d_next = jnp.where(rid == tm - 1, bottom, pltpu.roll(d, tm - 1, 0))
        mu_v = mu_ref[...]
        o_ref[...] = (d * (1.0 - mu_v) + d_next * mu_v).astype(o_ref.dtype)

        @pl.when(i == 0)
        def _():
            dmu_ref[...] = jnp.zeros_like(dmu_ref)

        dmu_ref[...] += jnp.sum(d * (prev - cur), axis=0, keepdims=True)

    return pl.pallas_call(
        body, name=name, grid=(ncol, nt),
        in_specs=[pl.BlockSpec((1, cw), lambda j, i: (0, j)),
                  pl.BlockSpec((tm, cw), lambda j, i: (i, cb + j)),
                  pl.BlockSpec((SUBLANES, cw), lambda j, i: (jnp.maximum(i * hb - 1, 0), cb + j)),
                  pl.BlockSpec((tm, cw), lambda j, i: (i, j)),
                  pl.BlockSpec((SUBLANES, cw), lambda j, i: (jnp.minimum((i + 1) * hb, nt * hb - 1), j))],
        out_specs=[pl.BlockSpec((tm, cw), lambda j, i: (i, j)), pl.BlockSpec((1, cw), lambda j, i: (0, j))],
        out_shape=[jax.ShapeDtypeStruct((T, width), bf16), jax.ShapeDtypeStruct((1, width), f32)],
        compiler_params=_cparams(2),
    )(mu, proj, proj, dxs, dxs)


def _rolled_copies(dst_ref, ext):
    n = ext.shape[0]
    dst_ref[0] = ext
    for r in range(1, SUBLANES):
        dst_ref[r] = pltpu.roll(ext, n - r, 0)


def _window(rolled_ref, start, rows):
    q, r = divmod(start, SUBLANES)
    return rolled_ref[r, pl.ds(SUBLANES * q, rows), :]


def _conv_fwd(proj, off_v, off_g, width, conv_w, conv_b, ktaps, T, tm):
    cw = _tile(width, 512)
    ncol = width // cw
    hb = tm // CONV_HALO
    lead = CONV_HALO - (ktaps - 1)

    def body(w_ref, b_ref, v_ref, g_ref, hv_ref, hg_ref, o_ref, u_ref):
        i = pl.program_id(1)
        halo = hv_ref[...] * _sigmoid(hg_ref[...])
        _rolled_copies(u_ref, jnp.concatenate([jnp.where(i == 0, 0.0, halo), v_ref[...] * _sigmoid(g_ref[...])], axis=0))
        acc = jnp.zeros((tm, cw), f32) + b_ref[...]
        for j in range(ktaps):
            acc = acc + _window(u_ref, lead + j, tm) * w_ref[j:j + 1, :]
        o_ref[...] = acc

    def tile(off):
        return pl.BlockSpec((tm, cw), lambda j, i: (i, off // cw + j))

    def halo(off):
        return pl.BlockSpec((CONV_HALO, cw), lambda j, i: (jnp.maximum(i * hb - 1, 0), off // cw + j))

    return pl.pallas_call(
        body, name="conv_fwd", grid=(ncol, T // tm),
        in_specs=[pl.BlockSpec((CONV_HALO, cw), lambda j, i: (0, j)), pl.BlockSpec((1, cw), lambda j, i: (0, j)),
                  tile(off_v), tile(off_g), halo(off_v), halo(off_g)],
        out_specs=pl.BlockSpec((tm, cw), lambda j, i: (i, j)),
        out_shape=jax.ShapeDtypeStruct((T, width), f32),
        scratch_shapes=[pltpu.VMEM((SUBLANES, CONV_HALO + tm, cw), f32)],
        compiler_params=_cparams(2),
    )(conv_w, conv_b, proj, proj, proj, proj)


def _conv_bwd(proj, off_v, off_g, width, conv_w, dc, ktaps, T, tm):
    cw = _tile(width, 512)
    ncol = width // cw
    hb, nt = tm // CONV_HALO, T // tm
    lead = CONV_HALO - (ktaps - 1)

    def body(w_ref, v_ref, g_ref, hv_ref, hg_ref, dc_ref, dcn_ref, dv_ref, dg_ref, dw_ref, db_ref, u_ref, d_ref):
        i = pl.program_id(1)
        halo = hv_ref[...] * _sigmoid(hg_ref[...])
        sig = _sigmoid(g_ref[...])
        gv = v_ref[...]
        _rolled_copies(u_ref, jnp.concatenate([jnp.where(i == 0, 0.0, halo), gv * sig], axis=0))
        dcur = dc_ref[...]
        _rolled_copies(d_ref, jnp.concatenate([dcur, jnp.where(i == nt - 1, 0.0, dcn_ref[...])], axis=0))

        @pl.when(i == 0)
        def _():
            dw_ref[...] = jnp.zeros_like(dw_ref)
            db_ref[...] = jnp.zeros_like(db_ref)

        du = jnp.zeros((tm, cw), f32)
        for j in range(ktaps):
            du = du + _window(d_ref, ktaps - 1 - j, tm) * w_ref[j:j + 1, :]
            dw_ref[j:j + 1, :] += jnp.sum(_window(u_ref, lead + j, tm) * dcur, axis=0, keepdims=True)
        db_ref[...] += jnp.sum(dcur, axis=0, keepdims=True)
        dv_ref[...] = (du * sig).astype(dv_ref.dtype)
        dg_ref[...] = (du * gv * sig * (1.0 - sig)).astype(dg_ref.dtype)

    def tile(off):
        return pl.BlockSpec((tm, cw), lambda j, i: (i, off // cw + j))

    def halo(off):
        return pl.BlockSpec((CONV_HALO, cw), lambda j, i: (jnp.maximum(i * hb - 1, 0), off // cw + j))

    return pl.pallas_call(
        body, name="conv_bwd", grid=(ncol, nt),
        in_specs=[pl.BlockSpec((CONV_HALO, cw), lambda j, i: (0, j)),
                  tile(off_v), tile(off_g), halo(off_v), halo(off_g),
                  pl.BlockSpec((tm, cw), lambda j, i: (i, j)),
                  pl.BlockSpec((CONV_HALO, cw), lambda j, i: (jnp.minimum((i + 1) * hb, nt * hb - 1), j))],
        out_specs=[pl.BlockSpec((tm, cw), lambda j, i: (i, j)), pl.BlockSpec((tm, cw), lambda j, i: (i, j)),
                   pl.BlockSpec((CONV_HALO, cw), lambda j, i: (0, j)), pl.BlockSpec((1, cw), lambda j, i: (0, j))],
        out_shape=[jax.ShapeDtypeStruct((T, width), bf16), jax.ShapeDtypeStruct((T, width), bf16),
                   jax.ShapeDtypeStruct((CONV_HALO, width), f32), jax.ShapeDtypeStruct((1, width), f32)],
        scratch_shapes=[pltpu.VMEM((SUBLANES, CONV_HALO + tm, cw), f32), pltpu.VMEM((SUBLANES, tm + CONV_HALO, cw), f32)],
        compiler_params=_cparams(2),
    )(conv_w, proj, proj, proj, proj, dc, dc)


def _each(f, *lists):
    return [f(*xs) for xs in zip(*lists)]


def _wkv_local(r, lw, k, v, a, b):
    C = r[0].shape[0]
    P = WKV_PREC
    row = lax.broadcasted_iota(jnp.int32, (C, C), 0)
    col = lax.broadcasted_iota(jnp.int32, (C, C), 1)
    incl, strict = row >= col, row > col
    tri = incl.astype(f32)
    zero = jnp.zeros((C, C), f32)
    G = _each(lambda x: _mm(tri, x, P), lw)
    to_end = _each(lambda x, g: jnp.exp(jnp.sum(x, axis=0, keepdims=True) - g), lw, G)
    e_g = _each(jnp.exp, G)
    e_ng = _each(lambda g: jnp.exp(-g), G)
    At = _each(lambda x, g, w: x * jnp.exp(g - w), a, G, lw)
    Rt = _each(jnp.multiply, r, e_g)
    Kt = _each(jnp.multiply, k, e_ng)
    Bt = _each(jnp.multiply, b, e_ng)
    sc = _each(lambda at, rt, bt, kt: _mm_nt(jnp.concatenate([at, rt], axis=0), jnp.concatenate([bt, kt], axis=0), P),
               At, Rt, Bt, Kt)
    L = _each(lambda s: jnp.where(strict, s[:C, :C], zero), sc)
    M = _each(lambda s: jnp.where(strict, s[:C, C:], zero), sc)
    Pb = _each(lambda s: jnp.where(incl, s[C:, :C], zero), sc)
    Pk = _each(lambda s: jnp.where(incl, s[C:, C:], zero), sc)
    MPk = _each(lambda m, p, x: _bmm(jnp.concatenate([m, p], axis=0), x), M, Pk, v)
    WU = _each(lambda at, mp: jnp.concatenate([at, mp[:C]], axis=1), At, MPk)
    Lp = L
    n = 1
    while n < C:
        n *= 2
        if n < C:
            step = _each(lambda l, x: _bmm(l, jnp.concatenate([x, l], axis=1)), Lp, WU)
            WU = _each(lambda x, s: x + s[:, :x.shape[1]], WU, step)
            Lp = _each(lambda x, s: s[:, x.shape[1]:], WU, step)
        else:
            WU = _each(lambda x, l: x + _bmm(l, x), WU, Lp)
    N = r[0].shape[1]
    W = _each(lambda x: x[:, :N], WU)
    U = _each(lambda x: x[:, N:], WU)
    Y0 = _each(lambda mp: mp[C:], MPk)
    Bend = _each(jnp.multiply, b, to_end)
    Z = _each(lambda x, y, e: _mm_tn(x, y * e, P), v, k, to_end)
    return W, U, Rt, Pb, Y0, Bend, Z


def _wkv_state(S0, W, U, Rt, Pb, Bend, lw, Y0, Z):
    P = WKV_PREC
    C = W[0].shape[0]
    WR = _each(lambda w, rt, s: _mm_nt(jnp.concatenate([w, rt], axis=0), s, P), W, Rt, S0)
    X = _each(lambda wr, u: wr[:C] + u, WR, U)
    y = _each(lambda p, x, wr, c: _mm(p, x, P) + wr[C:] + c, Pb, X, WR, Y0)
    S1 = _each(lambda s, w, x, e, z: s * jnp.exp(jnp.sum(w, axis=0, keepdims=True)) + _mm_tn(x, e, P) + z,
               S0, lw, X, Bend, Z)
    return y, S1


def _wkv_dims(head, T, RW, heads_per_step):
    C = min(WKV_CHUNK, T)
    nh = RW // head
    hb = min(heads_per_step, nh)
    return C, nh, hb, hb * head, T // C


def _heads(ref, hb, head):
    return [ref[:, h * head:(h + 1) * head] for h in range(hb)]


def _put_heads(ref, vals, head):
    for h, val in enumerate(vals):
        ref[:, h * head:(h + 1) * head] = val


def _wkv_local_fwd(r, lw, k, v, a, b, head, T, RW):
    C, nh, hb, bw, nc = _wkv_dims(head, T, RW, WKV_HEADS)

    def body(*refs):
        ins, outs = refs[:6], refs[6:]
        res = _wkv_local(*[_heads(x, hb, head) for x in ins])
        for o_ref, vals in zip(outs[:6], res[:6]):
            _put_heads(o_ref, vals, head)
        for h in range(hb):
            outs[6][0, h] = res[6][h]

    blk = pl.BlockSpec((C, bw), lambda g, c: (c, g))
    sq = pl.BlockSpec((1, hb, head, head), lambda g, c: (c, g, 0, 0))
    return pl.pallas_call(
        body, name="wkv_local", grid=(nh // hb, nc),
        in_specs=[blk] * 6, out_specs=[blk] * 6 + [sq],
        out_shape=[jax.ShapeDtypeStruct((T, RW), f32)] * 6 + [jax.ShapeDtypeStruct((nc, nh, head, head), f32)],
        compiler_params=_cparams(2),
    )(r, lw, k, v, a, b)


def _wkv_state_fwd(W, U, Rt, Pb, Bend, lw, Y0, Z, head, T, RW):
    C, nh, hb, bw, nc = _wkv_dims(head, T, RW, WKV_STATE_HEADS)

    def body(w_ref, u_ref, rt_ref, pb_ref, be_ref, lw_ref, y0_ref, z_ref, y_ref, st_ref, s_ref):
        @pl.when(pl.program_id(1) == 0)
        def _():
            s_ref[...] = jnp.zeros_like(s_ref)

        S0 = [s_ref[h] for h in range(hb)]
        for h in range(hb):
            st_ref[0, h] = S0[h]
        rows = [_heads(x, hb, head) for x in (w_ref, u_ref, rt_ref, pb_ref, be_ref, lw_ref, y0_ref)]
        y, S1 = _wkv_state(S0, *rows, [z_ref[0, h] for h in range(hb)])
        _put_heads(y_ref, y, head)
        for h in range(hb):
            s_ref[h] = S1[h]

    blk = pl.BlockSpec((C, bw), lambda g, c: (c, g))
    sq = pl.BlockSpec((1, hb, head, head), lambda g, c: (c, g, 0, 0))
    return pl.pallas_call(
        body, name="wkv_state", grid=(nh // hb, nc),
        in_specs=[blk] * 7 + [sq], out_specs=[blk, sq],
        out_shape=[jax.ShapeDtypeStruct((T, RW), f32), jax.ShapeDtypeStruct((nc, nh, head, head), f32)],
        scratch_shapes=[pltpu.VMEM((hb, head, head), f32)],
        compiler_params=_cparams(2),
    )(W, U, Rt, Pb, Bend, lw, Y0, Z)


def _wkv_state_bwd(W, U, Rt, Pb, Bend, lw, Y0, Z, states, dy, head, T, RW):
    C, nh, hb, bw, nc = _wkv_dims(head, T, RW, WKV_STATE_HEADS)

    def body(w_ref, u_ref, rt_ref, pb_ref, be_ref, lw_ref, y0_ref, z_ref, st_ref, dy_ref,
             dw_ref, du_ref, drt_ref, dpb_ref, dbe_ref, dlw_ref, dz_ref, ds_ref):
        @pl.when(pl.program_id(1) == 0)
        def _():
            ds_ref[...] = jnp.zeros_like(ds_ref)

        dS1 = [ds_ref[h] for h in range(hb)]
        for h in range(hb):
            dz_ref[0, h] = dS1[h]
        rows = [_heads(x, hb, head) for x in (w_ref, u_ref, rt_ref, pb_ref, be_ref, lw_ref)]
        Y0 = _heads(y0_ref, hb, head)
        Zs = [z_ref[0, h] for h in range(hb)]
        _, vjp = jax.vjp(lambda s0, *rw: _wkv_state(s0, *rw, Y0, Zs), [st_ref[0, h] for h in range(hb)], *rows)
        grads = vjp((_heads(dy_ref, hb, head), dS1))
        for o_ref, vals in zip((dw_ref, du_ref, drt_ref, dpb_ref, dbe_ref, dlw_ref), grads[1:]):
            _put_heads(o_ref, vals, head)
        for h in range(hb):
            ds_ref[h] = grads[0][h]

    blk = pl.BlockSpec((C, bw), lambda g, c: (nc - 1 - c, g))
    sq = pl.BlockSpec((1, hb, head, head), lambda g, c: (nc - 1 - c, g, 0, 0))
    return pl.pallas_call(
        body, name="wkv_state_bwd", grid=(nh // hb, nc),
        in_specs=[blk] * 7 + [sq, sq, blk], out_specs=[blk] * 6 + [sq],
        out_shape=[jax.ShapeDtypeStruct((T, RW), f32)] * 6 + [jax.ShapeDtypeStruct((nc, nh, head, head), f32)],
        scratch_shapes=[pltpu.VMEM((hb, head, head), f32)],
        compiler_params=_cparams(2),
    )(W, U, Rt, Pb, Bend, lw, Y0, Z, states, dy)


def _wkv_local_bwd(r, lw, k, v, a, b, cots, d_lw_x, dr_x, dk_x, dv_x, head, T, RW):
    C, nh, hb, bw, nc = _wkv_dims(head, T, RW, WKV_HEADS)

    def body(*refs):
        ins, cot_refs, add_refs, outs = refs[:6], refs[6:13], refs[13:17], refs[17:]
        _, vjp = jax.vjp(_wkv_local, *[_heads(x, hb, head) for x in ins])
        cts = [_heads(x, hb, head) for x in cot_refs[:6]] + [[cot_refs[6][0, h] for h in range(hb)]]
        dr, dlw, dk, dv, da, db = vjp(tuple(cts))
        dlw_x, drx, dkx, dvx = [_heads(x, hb, head) for x in add_refs]
        _put_heads(outs[0], _each(jnp.add, dr, drx), head)
        _put_heads(outs[1], _each(jnp.add, dlw, dlw_x), head)
        _put_heads(outs[2], _each(jnp.add, dk, dkx), head)
        _put_heads(outs[3], _each(jnp.add, dv, dvx), head)
        _put_heads(outs[4], da, head)
        _put_heads(outs[5], db, head)

    blk = pl.BlockSpec((C, bw), lambda g, c: (c, g))
    sq = pl.BlockSpec((1, hb, head, head), lambda g, c: (c, g, 0, 0))
    return pl.pallas_call(
        body, name="wkv_local_bwd", grid=(nh // hb, nc),
        in_specs=[blk] * 12 + [sq] + [blk] * 4, out_specs=[blk] * 6,
        out_shape=[jax.ShapeDtypeStruct((T, RW), f32)] * 6,
        compiler_params=_cparams(2),
    )(r, lw, k, v, a, b, *cots, d_lw_x, dr_x, dk_x, dv_x)


def _rows_tile(R, row_bytes, budget, mult=SUBLANES):
    best = None
    t = mult
    while t <= R:
        if R % t == 0 and t * row_bytes <= budget:
            best = t
        t += mult
    return best if best is not None else R


def _sum_slots(name, parts):
    S, R, W = parts.shape
    budget = 4 << 20
    tr = _rows_tile(R, S * W * 4, budget, 2 * SUBLANES)
    cw = W if tr * S * W * 4 <= 2 * budget else _tile(W, max(LANES, 2 * budget // (S * tr * 4)))

    def body(p_ref, o_ref):
        acc = p_ref[0].astype(f32)
        for d in range(1, S):
            acc = acc + p_ref[d].astype(f32)
        o_ref[...] = acc

    return pl.pallas_call(
        body, name=name, grid=(R // tr, W // cw),
        in_specs=[pl.BlockSpec((S, tr, cw), lambda i, j: (0, i, j))],
        out_specs=pl.BlockSpec((tr, cw), lambda i, j: (i, j)),
        out_shape=jax.ShapeDtypeStruct((R, W), f32),
        compiler_params=_cparams(2),
    )(parts)


def _tile2d(R, W, budget, mult):
    tr = _rows_tile(R, LANES * 4, budget, mult)
    cw = _tile(W, max(LANES, budget // (tr * 4))) if W % LANES == 0 else W
    return tr, cw


def _add_pair(name, a, b):
    R, W = a.shape
    tr, cw = _tile2d(R, W, 2 << 20, 2 * SUBLANES)

    def body(a_ref, b_ref, o_ref):
        o_ref[...] = (a_ref[...].astype(f32) + b_ref[...].astype(f32)).astype(o_ref.dtype)

    blk = pl.BlockSpec((tr, cw), lambda i, j: (i, j))
    return pl.pallas_call(
        body, name=name, grid=(R // tr, W // cw), in_specs=[blk, blk], out_specs=blk,
        out_shape=jax.ShapeDtypeStruct((R, W), a.dtype), compiler_params=_cparams(2),
    )(a, b)


def _adamw(name, w, g, m, v):
    R, W = w.shape
    tr, cw = _tile2d(R, W, 2 << 20, SUBLANES)

    def body(w_ref, g_ref, m_ref, v_ref, d_ref, nm_ref, nv_ref):
        g_v = g_ref[...]
        nm = ADAM_B1 * m_ref[...] + (1.0 - ADAM_B1) * g_v
        nv = ADAM_B2 * v_ref[...] + (1.0 - ADAM_B2) * (g_v * g_v)
        m_hat = nm / (1.0 - ADAM_B1 ** ADAM_STEP)
        v_hat = nv / (1.0 - ADAM_B2 ** ADAM_STEP)
        d_ref[...] = -ADAM_LR * (m_hat / (jnp.sqrt(v_hat) + ADAM_EPS) + ADAM_WD * w_ref[...])
        nm_ref[...] = nm
        nv_ref[...] = nv

    blk = pl.BlockSpec((tr, cw), lambda i, j: (i, j))
    return pl.pallas_call(
        body, name=name, grid=(R // tr, W // cw),
        in_specs=[blk] * 4, out_specs=[blk] * 3,
        out_shape=[jax.ShapeDtypeStruct((R, W), f32)] * 3,
        compiler_params=_cparams(2),
    )(w, g, m, v)


ANY = pl.BlockSpec(memory_space=pl.ANY)


def _place():
    return lax.axis_index("x"), lax.axis_index("y"), lax.axis_index("c")


class _Comm:
    def __init__(self, operands, out_shape, scratch, start, wait):
        self.operands, self.out_shape, self.scratch, self.start, self.wait = operands, out_shape, scratch, start, wait


def _run_comm(name, comm):
    n = len(comm.operands)

    def body(*refs):
        parts = (refs[:n], refs[n:2 * n], refs[2 * n:])
        comm.start(*parts)
        comm.wait(*parts)

    return pl.pallas_call(
        body, name=name, in_specs=[ANY] * n, out_specs=[ANY] * n, out_shape=comm.out_shape,
        scratch_shapes=comm.scratch,
    )(*comm.operands)


def _copy_chunks(rows, cols):
    k = SHARE_CHUNKS // 2
    if rows % (k * 2 * SUBLANES) == 0:
        return [(pl.ds(q * (rows // k), rows // k), pl.ds(0, cols)) for q in range(k)]
    if cols % (k * LANES) == 0:
        return [(pl.ds(0, rows), pl.ds(q * (cols // k), cols // k)) for q in range(k)]
    return [(pl.ds(0, rows), pl.ds(0, cols))]


def _gather_chips(arrays):
    n = len(arrays)
    parts = [(a, h, blk) for a, arr in enumerate(arrays) for h in range(2) for blk in _copy_chunks(*arr.shape[1:])]

    def copies(ins, outs, sems):
        send_sems, recv_sems, local_sems = sems
        x, y, c = _place()
        mine = 2 * x + y
        sib = (x, y, 1 - c)
        chips = [(1 - x, y), (x, 1 - y), (1 - x, 1 - y)]

        def local():
            return [pltpu.make_async_copy(ins[a].at[(h, *blk)], outs[a].at[(mine, h, *blk)], local_sems.at[p])
                    for p, (a, h, blk) in enumerate(parts)]

        def over_ici(a, j, slot):
            px, py = chips[j]
            return pltpu.make_async_remote_copy(
                src_ref=ins[a].at[c], dst_ref=outs[a].at[slot, c], send_sem=send_sems.at[3 * a + j],
                recv_sem=recv_sems.at[3 * a + j], device_id=(px, py, c), device_id_type=MESH)

        def over_d2d(a, j, half):
            px, py = chips[j]
            slot = 2 * px + py
            return pltpu.make_async_remote_copy(
                src_ref=outs[a].at[slot, half], dst_ref=outs[a].at[slot, half], send_sem=send_sems.at[3 * (n + a) + j],
                recv_sem=recv_sems.at[3 * (n + a) + j], device_id=sib, device_id_type=MESH)

        pairs = [(a, j) for a in range(n) for j in range(3)]
        return dict(local=local,
                    sends=lambda: [over_ici(a, j, mine) for a, j in pairs],
                    landing=lambda: [over_ici(a, j, 2 * chips[j][0] + chips[j][1]) for a, j in pairs],
                    passed=lambda: [over_d2d(a, j, c) for a, j in pairs],
                    from_sib=lambda: [over_d2d(a, j, 1 - c) for a, j in pairs])

    def start(ins, outs, sems):
        cps = copies(ins, outs, sems)
        for cp in cps["local"]() + cps["sends"]():
            cp.start()

    def wait(ins, outs, sems):
        cps = copies(ins, outs, sems)
        passed = cps["passed"]()
        for got, on in zip(cps["landing"](), passed):
            got.wait_recv()
            on.start()
        for cp in cps["from_sib"]():
            cp.wait_recv()
        for cp in cps["sends"]() + passed:
            cp.wait_send()
        for cp in cps["local"]():
            cp.wait()

    return _Comm(arrays, [jax.ShapeDtypeStruct((4,) + a.shape, a.dtype) for a in arrays],
                 [pltpu.SemaphoreType.DMA((6 * n,)), pltpu.SemaphoreType.DMA((6 * n,)),
                  pltpu.SemaphoreType.DMA((len(parts),))], start, wait)


def _exchange_chips(pieces, whole):
    n, m = len(pieces), len(whole)
    parts = [(a, blk) for a, arr in enumerate(pieces) for blk in _copy_chunks(*arr.shape[1:])]

    def copies(ins, outs, sems):
        send_sems, recv_sems, local_sems = sems
        x, y, c = _place()
        chip, dev = 2 * x + y, 4 * x + 2 * y + c
        chips = [(1 - x, y), (x, 1 - y), (1 - x, 1 - y)]
        peers = [(x ^ (k >> 2), y ^ ((k >> 1) & 1), c ^ (k & 1)) for k in range(1, 8)]
        def local():
            cps = [pltpu.make_async_copy(ins[a].at[(chip, *blk)], outs[a].at[(chip, *blk)], local_sems.at[p])
                   for p, (a, blk) in enumerate(parts)]
            return cps + [pltpu.make_async_copy(ins[n + b], outs[n + b].at[dev], local_sems.at[len(parts) + b])
                          for b in range(m)]

        def piece(a, j, slot_from):
            px, py = chips[j]
            return pltpu.make_async_remote_copy(
                src_ref=ins[a].at[2 * px + py], dst_ref=outs[a].at[slot_from], send_sem=send_sems.at[3 * a + j],
                recv_sem=recv_sems.at[3 * a + j], device_id=(px, py, c), device_id_type=MESH)

        def everyone(b, j, slot_from):
            px, py, pc = peers[j]
            return pltpu.make_async_remote_copy(
                src_ref=ins[n + b], dst_ref=outs[n + b].at[slot_from], send_sem=send_sems.at[3 * n + 7 * b + j],
                recv_sem=recv_sems.at[3 * n + 7 * b + j], device_id=(px, py, pc), device_id_type=MESH)

        def sends():
            return ([everyone(b, j, dev) for b in range(m) for j in range(7)]
                    + [piece(a, j, chip) for a in range(n) for j in range(3)])

        def landing():
            return ([everyone(b, j, 4 * px + 2 * py + pc) for b in range(m) for j, (px, py, pc) in enumerate(peers)]
                    + [piece(a, j, 2 * px + py) for a in range(n) for j, (px, py) in enumerate(chips)])

        return local, sends, landing

    def start(ins, outs, sems):
        local, sends, _ = copies(ins, outs, sems)
        for cp in local() + sends():
            cp.start()

    def wait(ins, outs, sems):
        local, sends, landing = copies(ins, outs, sems)
        for cp in landing():
            cp.wait_recv()
        for cp in sends():
            cp.wait_send()
        for cp in local():
            cp.wait()

    shapes = [jax.ShapeDtypeStruct(a.shape, a.dtype) for a in pieces]
    shapes += [jax.ShapeDtypeStruct((8,) + a.shape, a.dtype) for a in whole]
    nsem = 3 * n + 7 * m
    return _Comm(list(pieces) + list(whole), shapes,
                 [pltpu.SemaphoreType.DMA((nsem,)), pltpu.SemaphoreType.DMA((nsem,)),
                  pltpu.SemaphoreType.DMA((len(parts) + m,))], start, wait)


def _share_sibling(name, arrays):
    n = len(arrays)
    parts = []
    for a, arr in enumerate(arrays):
        k = SHARE_CHUNKS if arr.shape[0] % (SHARE_CHUNKS * SUBLANES) == 0 else 1
        k = arr.shape[0] if arr.ndim == 3 else k
        step = arr.shape[0] // k
        parts += [(a, q * step, step) for q in range(k)]
    npart = len(parts)

    def body(*refs):
        ins, outs = refs[:n], refs[n:2 * n]
        send_sems, recv_sems = refs[2 * n:]
        x, y, c = _place()

        def copy(p):
            a, r0, nr = parts[p]
            return pltpu.make_async_remote_copy(
                src_ref=ins[a].at[pl.ds(r0, nr)], dst_ref=outs[a].at[pl.ds(r0, nr)], send_sem=send_sems.at[p],
                recv_sem=recv_sems.at[p], device_id=(x, y, 1 - c), device_id_type=MESH)

        copies = [copy(p) for p in range(npart)]
        for cp in copies:
            cp.start()
        for cp in copies:
            cp.wait_recv()
        for cp in copies:
            cp.wait_send()

    return pl.pallas_call(
        body, name=name,
        in_specs=[ANY] * n, out_specs=[ANY] * n,
        out_shape=[jax.ShapeDtypeStruct(a.shape, a.dtype) for a in arrays],
        scratch_shapes=[pltpu.SemaphoreType.DMA((npart,)), pltpu.SemaphoreType.DMA((npart,))],
    )(*arrays)


def _place_blocks(blocks, axis):
    shape = list(blocks[0].shape)
    shape[axis] = sum(b.shape[axis] for b in blocks)
    buf = lax.empty(tuple(shape), blocks[0].dtype)
    at = 0
    for b in blocks:
        buf = lax.dynamic_update_slice_in_dim(buf, b, at, axis)
        at += b.shape[axis]
    return buf


def kernel(x, norm_pre_g, w_in, mu_shift, w0, w_lora_up, a0, a_lora_up, k_k, k_a, r_k, lnx_g, lnx_b, conv_w, conv_b, cln_g, cln_b, w_pw2, b_pw2, w_out, norm_post_g, loss_target, m_norm_pre_g, m_w_in, m_mu_shift, m_w0, m_w_lora_up, m_a0, m_a_lora_up, m_k_k, m_k_a, m_r_k, m_lnx_g, m_lnx_b, m_conv_w, m_conv_b, m_cln_g, m_cln_b, m_w_pw2, m_b_pw2, m_w_out, m_norm_post_g, v_norm_pre_g, v_w_in, v_mu_shift, v_w0, v_w_lora_up, v_a0, v_a_lora_up, v_k_k, v_k_a, v_r_k, v_lnx_g, v_lnx_b, v_conv_w, v_conv_b, v_cln_g, v_cln_b, v_w_pw2, v_b_pw2, v_w_out, v_norm_post_g):
    _, T, D = x.shape
    RW = w0.shape[0]
    CW = conv_b.shape[0]
    head = r_k.shape[1]
    lora = w_lora_up.shape[0]
    ktaps = conv_w.shape[0]
    assert RW == CW and 2 * lora <= LORA_PAD and ktaps - 1 <= CONV_HALO
    n_in = 3 * RW + 2 * lora + RW + 3 * CW
    shard = n_in // 4
    PW = 7 * RW + LORA_PAD
    off_l = 7 * RW
    tm = min(256, T // 2)
    tm_wide = min(128, T // 2)
    tm_halo = min(512, T // 2)
    row = lambda vec: vec.reshape(1, -1)
    x2, tgt2 = x[0], loss_target[0]

    halves = lambda a: a.reshape(2, a.shape[0] // 2, a.shape[1])
    conv_w_p = jnp.concatenate([conv_w, jnp.zeros((CONV_HALO - ktaps, CW // 4), f32)], axis=0)
    w_in_t, m_w_in_t, v_w_in_t = w_in.T, m_w_in.T, v_w_in.T
    (g_win,) = _run_comm("gather_w_in", _gather_chips([halves(w_in_t.astype(bf16))]))
    win_t = g_win.reshape(n_in, D)
    lo = 3 * RW
    wp_t = _place_blocks([win_t[:lo], win_t[lo + 2 * lora:], win_t[lo:lo + 2 * lora],
                          jnp.zeros((LORA_PAD - 2 * lora, D), bf16)], axis=0)
    npg = row(norm_pre_g)
    (h,) = _row_fwd("rms_pre", _fn_rms_pre, [(npg, False)], [(x2, 0, D, False)], [(D, bf16)], T, tm)
    others = [halves(a) for a in (w_lora_up, a_lora_up, conv_w_p, w_pw2.astype(bf16), w_out.astype(bf16))]
    proj, (g_wup, g_aup, g_cw, g_pw2, g_wout) = _matmul("proj", h, wp_t, "nt", f32, comm=_gather_chips(others))
    cat_cols = lambda g: jnp.concatenate([g[s].reshape(-1, g.shape[-1]) for s in range(4)], axis=1)
    wup_full, aup_full, cw_p = cat_cols(g_wup), cat_cols(g_aup), cat_cols(g_cw)
    zl = lambda n: jnp.zeros((n, RW), f32)
    wup_p = jnp.concatenate([wup_full, zl(LORA_PAD - lora)], axis=0)
    aup_p = jnp.concatenate([zl(lora), aup_full, zl(LORA_PAD - 2 * lora)], axis=0)
    pw2_full = g_pw2.reshape(CW, CW)
    wout_full = g_wout.reshape(RW + CW, D)
    mu_r, mu_k, mu_v = (row(mu_shift[s * RW:(s + 1) * RW]) for s in range(3))
    mu_l = row(jnp.concatenate([mu_shift[3 * RW:], jnp.zeros((LORA_PAD - 2 * lora,), f32)]))

    xs_r = _shift_fwd("shift_r", proj, 0, RW, mu_r, T, tm_halo)
    xs_k = _shift_fwd("shift_k", proj, RW, RW, mu_k, T, tm_halo)
    xs_v = _shift_fwd("shift_v", proj, 2 * RW, RW, mu_v, T, tm_halo)
    xs_l = _shift_fwd("shift_l", proj, off_l, LORA_PAD, mu_l, T, tm_halo)
    lora_params = [(row(w0), False), (wup_p, False), (row(a0), False), (aup_p, False)]
    qw, qa = _row_fwd("lora_up", _fn_lora, lora_params, [(xs_l, 0, LORA_PAD, False)], [(RW, f32), (RW, f32)], T, tm)
    ncol = RW // _tile(RW, 512)
    fn_pre = functools.partial(_fn_rwkv_pre, head)
    pre_params = [(row(k_k), True), (row(k_a), True)]
    pre_rows = [(xs_k, 0, RW, True), (qw, 0, RW, True), (qa, 0, RW, True)]
    lw, k_h, a_rec, b_rec = _row_fwd("rwkv_pre", fn_pre, pre_params, pre_rows, [(RW, f32)] * 4, T, tm, ncol)
    wkv_in = (xs_r, lw, k_h, xs_v, a_rec, b_rec)
    c_w, c_u, c_rt, c_pb, c_y0, c_bend, c_z = _wkv_local_fwd(*wkv_in, head, T, RW)
    wkv_loc = (c_w, c_u, c_rt, c_pb, c_bend, lw, c_y0, c_z)
    y_wkv, states = _wkv_state_fwd(*wkv_loc, head, T, RW)
    fn_post = functools.partial(_fn_rwkv_post, head)
    post_params = [(row(lnx_g), True), (row(lnx_b), True), (r_k.reshape(1, RW), True)]
    post_rows = [(y_wkv, 0, RW, True), (xs_r, 0, RW, True), (k_h, 0, RW, True), (xs_v, 0, RW, True),
                 (proj, 3 * RW, RW, True)]
    (y_rwkv,) = _row_fwd("rwkv_post", fn_post, post_params, post_rows, [(RW, bf16)], T, tm, ncol)

    c_pre = _conv_fwd(proj, 4 * RW, 5 * RW, CW, cw_p, row(conv_b), ktaps, T, tm_halo)
    ln_params = [(row(cln_g), False), (row(cln_b), False)]
    (c_act,) = _row_fwd("conv_ln", _fn_conv_ln, ln_params, [(c_pre, 0, CW, False)], [(CW, bf16)], T, tm)
    c2 = _matmul("pw2", c_act, pw2_full, "nn", f32)
    cpost_params = [(row(b_pw2), True)]
    cpost_rows = [(c2, 0, CW, True), (proj, 6 * RW, CW, True)]
    (y_conv,) = _row_fwd("conv_post", _fn_conv_post, cpost_params, cpost_rows, [(CW, bf16)], T, tm, ncol)

    mix = jnp.concatenate([y_rwkv, y_conv], axis=1)
    out = _matmul("out_proj", mix, wout_full, "nn", f32)
    d_out, gx_res, loss_part, g_npost = _post(out, x2, tgt2, row(norm_post_g), T, D, tm_wide)

    g_wout_full = _matmul("d_w_out", mix, d_out, "tn", bf16)
    d_mix = _matmul("d_mix", d_out, wout_full, "nt", f32)

    d_c2, d_gconv, g_bpw2 = _row_bwd("conv_post_bwd", _fn_conv_post, cpost_params, cpost_rows,
                                      [(d_mix, RW, CW, True)], [bf16, bf16], T, tm, ncol)
    g_pw2_full = _matmul("d_w_pw2", c_act, d_c2, "tn", bf16)
    d_cact = _matmul("d_c_act", d_c2, pw2_full, "nt", f32)
    d_cpre, g_clng, g_clnb = _row_bwd("conv_ln_bwd", _fn_conv_ln, ln_params, [(c_pre, 0, CW, False)],
                                      [(d_cact, 0, CW, False)], [f32], T, tm)
    d_gluv, d_glug, g_cw_p, g_cb = _conv_bwd(proj, 4 * RW, 5 * RW, CW, cw_p, d_cpre, ktaps, T, tm_halo)

    d_y, dr_x, dk_x, dv_x, d_grwkv, g_lnxg, g_lnxb, g_rk = _row_bwd(
        "rwkv_post_bwd", fn_post, post_params, post_rows, [(d_mix, 0, RW, True)], [f32, f32, f32, f32, bf16], T, tm, ncol)
    d_cw, d_cu, d_crt, d_cpb, d_cbend, d_lw_dec, d_cz = _wkv_state_bwd(*wkv_loc, states, d_y, head, T, RW)
    d_xr, d_lw, d_kh, d_xv, d_a, d_b = _wkv_local_bwd(
        *wkv_in, (d_cw, d_cu, d_crt, d_cpb, d_y, d_cbend, d_cz), d_lw_dec, dr_x, dk_x, dv_x, head, T, RW)
    pre_cots = [(d_lw, 0, RW, True), (d_kh, 0, RW, True), (d_a, 0, RW, True), (d_b, 0, RW, True)]
    d_xk, d_qw, d_qa, g_kk, g_ka = _row_bwd("rwkv_pre_bwd", fn_pre, pre_params, pre_rows, pre_cots, [f32, f32, f32],
                                            T, tm, ncol)
    d_xl, g_w0, g_wup_p, g_a0, g_aup_p = _row_bwd("lora_up_bwd", _fn_lora, lora_params, [(xs_l, 0, LORA_PAD, False)],
                                                  [(d_qw, 0, RW, False), (d_qa, 0, RW, False)], [f32], T, tm)
    dp_r, g_mur = _shift_bwd("shift_r_bwd", proj, 0, RW, mu_r, d_xr, T, tm_halo)
    dp_k, g_muk = _shift_bwd("shift_k_bwd", proj, RW, RW, mu_k, d_xk, T, tm_halo)
    dp_v, g_muv = _shift_bwd("shift_v_bwd", proj, 2 * RW, RW, mu_v, d_xv, T, tm_halo)
    dp_l, g_mul = _shift_bwd("shift_l_bwd", proj, off_l, LORA_PAD, mu_l, d_xl, T, tm_halo)
    d_proj = _place_blocks([dp_r, dp_k, dp_v, d_grwkv, d_gluv, d_glug, d_gconv, dp_l], axis=1)

    south = lax.axis_index("c") == 0
    flat = lambda a: a.reshape(-1, a.shape[-1])

    def keep_give(a):
        a = a.reshape(4, 2, a.shape[1] // 2, a.shape[2])
        return jnp.where(south, a[:, 0], a[:, 1]), jnp.where(south, a[:, 1], a[:, 0])

    def chip_sums(tag, shards):
        kept, given = zip(*[keep_give(a) for a in shards])
        got = _share_sibling("pair_exchange_" + tag, list(given))
        return [_add_pair("chip_sum_%s_%d" % (tag, i), flat(k), flat(g)).reshape(k.shape)
                for i, (k, g) in enumerate(zip(kept, got))]

    def both_halves(own, other):
        return jnp.concatenate([jnp.where(south, own, other), jnp.where(south, other, own)], axis=0)

    def all_chips(tag, slots):
        sums = [_sum_slots("sum_%s_%d" % (tag, i), r) for i, r in enumerate(slots)]
        others = _share_sibling("share_" + tag, sums)
        return [both_halves(s, o) for s, o in zip(sums, others)]

    q_early = chip_sums("early", [g_wout_full.reshape(4, (RW + CW) // 4, D), g_pw2_full.reshape(4, CW // 4, CW)])
    g_wp_t, r_early = _matmul("d_w_in", d_proj, h, "tn", bf16, comm=_exchange_chips(q_early, []))
    g_win_t = _place_blocks([g_wp_t[:lo], g_wp_t[off_l:off_l + 2 * lora], g_wp_t[lo:off_l]], axis=0)
    col_shards = lambda a: a.reshape(a.shape[0], 4, a.shape[1] // 4).transpose(1, 0, 2)
    q_late = chip_sums("late", [g_win_t.reshape(4, shard, D), col_shards(g_wup_p[:lora]),
                                col_shards(g_aup_p[lora:2 * lora]), col_shards(g_cw_p)])
    g_mu = jnp.concatenate([g_mur[0], g_muk[0], g_muv[0], g_mul[0, :2 * lora]])
    pad_rows = lambda a, n: jnp.concatenate([a, jnp.zeros((n - a.shape[0], a.shape[1]), f32)], axis=0)
    n_mu = -(-mu_shift.shape[0] // RW)
    small_vecs = [pad_rows(jnp.pad(g_mu, (0, n_mu * RW - g_mu.shape[0])).reshape(n_mu, RW), n_mu),
                  g_w0, g_a0, g_kk, g_ka, g_rk, g_lnxg, g_lnxb, g_cb, g_clng, g_clnb, g_bpw2,
                  g_npost.reshape(D // RW, RW)]
    n_small = sum(a.shape[0] for a in small_vecs)
    n_small_pad = -(-n_small // (2 * SUBLANES)) * (2 * SUBLANES)
    small = pad_rows(jnp.concatenate(small_vecs, axis=0), n_small_pad)

    d_h, r_late = _matmul("d_h", d_proj, wp_t, "nn", bf16, tk_t=2560, comm=_exchange_chips(q_late, [small]))
    grad_x2, g_npre = _rms_pre_bwd(x2, npg, d_h, gx_res, T, D, tm_wide)
    (r_npre,) = _run_comm("exchange_norm_pre", _exchange_chips([], [g_npre.reshape(D // RW, RW)]))
    s_npre = _sum_slots("sum_norm_pre", r_npre)
    s_small = _sum_slots("sum_small", r_late[4])
    grad_w_out, grad_w_pw2 = all_chips("early", r_early)
    grad_w_in, grad_wup, grad_aup, grad_cw = all_chips("late", r_late[:4])

    pos = [0]

    def take(nrows):
        a = s_small[pos[0]:pos[0] + nrows]
        pos[0] += nrows
        return a

    grads = {}
    grads["norm_pre_g"] = s_npre.reshape(D)
    grads["mu_shift"] = take(n_mu).reshape(-1)[:mu_shift.shape[0]]
    for nm in ["w0", "a0", "k_k", "k_a"]:
        grads[nm] = take(1).reshape(RW)
    grads["r_k"] = take(1).reshape(r_k.shape)
    for nm in ["lnx_g", "lnx_b", "conv_b", "cln_g", "cln_b", "b_pw2"]:
        grads[nm] = take(1).reshape(RW)
    grads["norm_post_g"] = take(D // RW).reshape(D)
    grads["w_lora_up"], grads["a_lora_up"], grads["conv_w"] = grad_wup, grad_aup, grad_cw[:ktaps]
    grads["w_in"], grads["w_out"], grads["w_pw2"] = grad_w_in, grad_w_out, grad_w_pw2

    weights = dict(norm_pre_g=norm_pre_g, w_in=w_in, mu_shift=mu_shift, w0=w0, w_lora_up=w_lora_up, a0=a0,
                   a_lora_up=a_lora_up, k_k=k_k, k_a=k_a, r_k=r_k, lnx_g=lnx_g, lnx_b=lnx_b, conv_w=conv_w,
                   conv_b=conv_b, cln_g=cln_g, cln_b=cln_b, w_pw2=w_pw2, b_pw2=b_pw2, w_out=w_out,
                   norm_post_g=norm_post_g)
    ms = dict(norm_pre_g=m_norm_pre_g, w_in=m_w_in, mu_shift=m_mu_shift, w0=m_w0, w_lora_up=m_w_lora_up, a0=m_a0,
              a_lora_up=m_a_lora_up, k_k=m_k_k, k_a=m_k_a, r_k=m_r_k, lnx_g=m_lnx_g, lnx_b=m_lnx_b, conv_w=m_conv_w,
              conv_b=m_conv_b, cln_g=m_cln_g, cln_b=m_cln_b, w_pw2=m_w_pw2, b_pw2=m_b_pw2, w_out=m_w_out,
              norm_post_g=m_norm_post_g)
    vs = dict(norm_pre_g=v_norm_pre_g, w_in=v_w_in, mu_shift=v_mu_shift, w0=v_w0, w_lora_up=v_w_lora_up, a0=v_a0,
              a_lora_up=v_a_lora_up, k_k=v_k_k, k_a=v_k_a, r_k=v_r_k, lnx_g=v_lnx_g, lnx_b=v_lnx_b, conv_w=v_conv_w,
              conv_b=v_conv_b, cln_g=v_cln_g, cln_b=v_cln_b, w_pw2=v_w_pw2, b_pw2=v_b_pw2, w_out=v_w_out,
              norm_post_g=v_norm_post_g)
    names = list(weights)
    big = ["w_in", "w_out", "w_pw2"]
    deltas, new_m, new_v = {}, {}, {}
    d_t, m_t, v_t = _adamw("adamw_w_in", w_in_t, grad_w_in, m_w_in_t, v_w_in_t)
    grads["w_in"], deltas["w_in"], new_m["w_in"], new_v["w_in"] = grad_w_in.T, d_t.T, m_t.T, v_t.T
    for nm in big[1:]:
        deltas[nm], new_m[nm], new_v[nm] = _adamw("adamw_" + nm, weights[nm], grads[nm], ms[nm], vs[nm])
    rest = [nm for nm in names if nm not in big]
    sizes = [weights[nm].size for nm in rest]
    total = sum(sizes)
    width = 4 * LANES
    rows_p = -(-total // (width * SUBLANES)) * SUBLANES

    def pack(d):
        flat = jnp.concatenate([d[nm].reshape(-1) for nm in rest])
        return jnp.pad(flat, (0, rows_p * width - total)).reshape(rows_p, width)

    p_d, p_m, p_v = _adamw("adamw_small", pack(weights), pack(grads), pack(ms), pack(vs))
    o = 0
    for nm, sz in zip(rest, sizes):
        shp = weights[nm].shape
        deltas[nm] = p_d.reshape(-1)[o:o + sz].reshape(shp)
        new_m[nm] = p_m.reshape(-1)[o:o + sz].reshape(shp)
        new_v[nm] = p_v.reshape(-1)[o:o + sz].reshape(shp)
        o += sz

    loss = lax.psum(loss_part[0, 0], ("x", "y", "c"))
    grad_x = grad_x2[None]
    return (loss, grad_x, *[grads[nm] for nm in names], *[deltas[nm] for nm in names],
            *[new_m[nm] for nm in names], *[new_v[nm] for nm in names])
```

```python
import functools

import jax
import jax.numpy as jnp
from jax import lax
from jax.experimental import pallas as pl
from jax.experimental.pallas import tpu as pltpu

f32 = jnp.float32
bf16 = jnp.bfloat16
MESH = pl.DeviceIdType.MESH

NORM_EPS = 1e-6
LN_EPS = 1e-5
ADAM_LR, ADAM_B1, ADAM_B2, ADAM_EPS, ADAM_WD, ADAM_STEP = 0.001, 0.9, 0.999, 1e-08, 0.01, 10

LANES = 128
SUBLANES = 8
LORA_PAD = 256
CONV_HALO = 32
WKV_CHUNK = 64
WKV_HEADS = 16
WKV_STATE_HEADS = 32
WKV_PREC = lax.Precision.HIGH
SHARE_CHUNKS = 8
VMEM_LIMIT = 56 * 1024 * 1024


def _cparams(n_axes):
    return pltpu.CompilerParams(dimension_semantics=("arbitrary",) * n_axes, vmem_limit_bytes=VMEM_LIMIT)


def _tile(dim, target):
    best = None
    t = LANES
    while t <= min(dim, target):
        if dim % t == 0:
            best = t
        t += LANES
    return best if best is not None else dim


def _mm(a, b, prec=None):
    return lax.dot_general(a, b, (((1,), (0,)), ((), ())), precision=prec, preferred_element_type=f32)


def _mm_nt(a, b, prec=None):
    return lax.dot_general(a, b, (((1,), (1,)), ((), ())), precision=prec, preferred_element_type=f32)


def _mm_tn(a, b, prec=None):
    return lax.dot_general(a, b, (((0,), (0,)), ((), ())), precision=prec, preferred_element_type=f32)


@jax.custom_vjp
def _bmm(a, b):
    return _mm(a.astype(bf16), b.astype(bf16))


def _bmm_fwd(a, b):
    return _bmm(a, b), (a, b)


def _bmm_bwd(res, dc):
    a, b = res
    dcb = dc.astype(bf16)
    return _mm_nt(dcb, b.astype(bf16)), _mm_tn(a.astype(bf16), dcb)


_bmm.defvjp(_bmm_fwd, _bmm_bwd)


def _matmul(name, a, b, mode, out_dtype, tm_t=1024, tn_t=1024, tk_t=4096, comm=None):
    if mode == "nn":
        (M, K), (_, N) = a.shape, b.shape
    elif mode == "nt":
        (M, K), (N, _) = a.shape, b.shape
    else:
        (K, M), (_, N) = a.shape, b.shape
    tm, tn, tk = _tile(M, tm_t), _tile(N, tn_t), _tile(K, tk_t)
    ni, nj, nk = M // tm, N // tn, K // tk
    dot = {"nn": _mm, "nt": _mm_nt, "tn": _mm_tn}[mode]
    nc = len(comm.operands) if comm else 0

    def body(*refs):
        a_ref, b_ref = refs[:2]
        o_ref = refs[2 + nc]
        scratch = refs[3 + 2 * nc:]
        i, j, k = pl.program_id(0), pl.program_id(1), pl.program_id(2)
        if comm:
            comm_refs = (refs[2:2 + nc], refs[3 + nc:3 + 2 * nc], scratch[:len(comm.scratch)])

            @pl.when(jnp.logical_and(jnp.logical_and(i == 0, j == 0), k == 0))
            def _():
                comm.start(*comm_refs)

        if nk == 1:
            o_ref[...] = dot(a_ref[...], b_ref[...]).astype(o_ref.dtype)
        else:
            acc_ref = scratch[-1]

            @pl.when(k == 0)
            def _():
                acc_ref[...] = jnp.zeros_like(acc_ref)

            acc_ref[...] += dot(a_ref[...], b_ref[...])

            @pl.when(k == nk - 1)
            def _():
                o_ref[...] = acc_ref[...].astype(o_ref.dtype)

        if comm:
            @pl.when(jnp.logical_and(jnp.logical_and(i == ni - 1, j == nj - 1), k == nk - 1))
            def _():
                comm.wait(*comm_refs)

    a_spec = {"nn": pl.BlockSpec((tm, tk), lambda i, j, k: (i, k)),
              "nt": pl.BlockSpec((tm, tk), lambda i, j, k: (i, k)),
              "tn": pl.BlockSpec((tk, tm), lambda i, j, k: (k, i))}[mode]
    b_spec = {"nn": pl.BlockSpec((tk, tn), lambda i, j, k: (k, j)),
              "nt": pl.BlockSpec((tn, tk), lambda i, j, k: (j, k)),
              "tn": pl.BlockSpec((tk, tn), lambda i, j, k: (k, j))}[mode]
    res = pl.pallas_call(
        body, name=name, grid=(ni, nj, nk),
        in_specs=[a_spec, b_spec] + [ANY] * nc,
        out_specs=[pl.BlockSpec((tm, tn), lambda i, j, k: (i, j))] + [ANY] * nc,
        out_shape=[jax.ShapeDtypeStruct((M, N), out_dtype)] + (list(comm.out_shape) if comm else []),
        scratch_shapes=(list(comm.scratch) if comm else []) + ([pltpu.VMEM((tm, tn), f32)] if nk > 1 else []),
        compiler_params=_cparams(3),
    )(a, b, *(comm.operands if comm else []))
    return (res[0], res[1:]) if comm else res[0]


def _row_spec(op, tm, ncol):
    arr, off, width, tiled = op
    if tiled:
        cw = width // ncol
        return pl.BlockSpec((tm, cw), lambda j, i: (i, off // cw + j))
    return pl.BlockSpec((tm, width), lambda j, i: (i, off // width))


def _param_spec(p, ncol):
    arr, tiled = p
    rows, width = arr.shape
    if tiled:
        return pl.BlockSpec((rows, width // ncol), lambda j, i: (0, j))
    return pl.BlockSpec((rows, width), lambda j, i: (0, 0))


def _row_fwd(name, fn, params, rows, outs, T, tm, ncol=1):
    npar, nrow = len(params), len(rows)

    def body(*refs):
        pv = [r[...] for r in refs[:npar]]
        rv = [r[...].astype(f32) for r in refs[npar:npar + nrow]]
        res = fn(*pv, *rv)
        for o_ref, val in zip(refs[npar + nrow:], res):
            o_ref[...] = val.astype(o_ref.dtype)

    return pl.pallas_call(
        body, name=name, grid=(ncol, T // tm),
        in_specs=[_param_spec(p, ncol) for p in params] + [_row_spec(r, tm, ncol) for r in rows],
        out_specs=[pl.BlockSpec((tm, w // ncol), lambda j, i: (i, j)) for w, _ in outs],
        out_shape=[jax.ShapeDtypeStruct((T, w), dt) for w, dt in outs],
        compiler_params=_cparams(2),
    )(*[p[0] for p in params], *[r[0] for r in rows])


def _row_bwd(name, fn, params, rows, cots, row_grads, T, tm, ncol=1):
    npar, nrow, ncot = len(params), len(rows), len(cots)
    want = [k for k, dt in enumerate(row_grads) if dt is not None]

    def body(*refs):
        pv = [r[...] for r in refs[:npar]]
        rv = [r[...].astype(f32) for r in refs[npar:npar + nrow]]
        cv = tuple(r[...].astype(f32) for r in refs[npar + nrow:npar + nrow + ncot])
        out_refs = refs[npar + nrow + ncot:]
        _, vjp = jax.vjp(fn, *pv, *rv)
        grads = vjp(cv)
        for o_ref, k in zip(out_refs[:len(want)], want):
            o_ref[...] = grads[npar + k].astype(o_ref.dtype)
        j, i = pl.program_id(0), pl.program_id(1)
        for o_ref, p, g in zip(out_refs[len(want):], params, grads[:npar]):
            first = (i == 0) if p[1] else jnp.logical_and(i == 0, j == 0)

            @pl.when(first)
            def _():
                o_ref[...] = jnp.zeros_like(o_ref)

            o_ref[...] += g

    def grad_spec(op):
        arr, off, width, tiled = op
        if tiled:
            return pl.BlockSpec((tm, width // ncol), lambda j, i: (i, j)), (T, width)
        return pl.BlockSpec((tm, width), lambda j, i: (i, j)), (T, width * ncol)

    gspecs = [grad_spec(rows[k]) for k in want]
    return pl.pallas_call(
        body, name=name, grid=(ncol, T // tm),
        in_specs=[_param_spec(p, ncol) for p in params] + [_row_spec(r, tm, ncol) for r in rows]
        + [_row_spec(c, tm, ncol) for c in cots],
        out_specs=[s for s, _ in gspecs] + [_param_spec(p, ncol) for p in params],
        out_shape=[jax.ShapeDtypeStruct(shp, row_grads[k]) for (_, shp), k in zip(gspecs, want)]
        + [jax.ShapeDtypeStruct(p[0].shape, f32) for p in params],
        compiler_params=_cparams(2),
    )(*[p[0] for p in params], *[r[0] for r in rows], *[c[0] for c in cots])


def _seg_sum(x, head):
    li = lax.broadcasted_iota(jnp.int32, (LANES, LANES), 0) // head
    lj = lax.broadcasted_iota(jnp.int32, (LANES, LANES), 1) // head
    q = (li == lj).astype(f32)
    parts = [_mm(x[:, s:s + LANES], q, lax.Precision.HIGH) for s in range(0, x.shape[1], LANES)]
    return parts[0] if len(parts) == 1 else jnp.concatenate(parts, axis=1)


def _sigmoid(z):
    return 1.0 / (1.0 + jnp.exp(-z))


def _silu(z):
    return z * _sigmoid(z)


def _rms(g, x):
    return x * lax.rsqrt(jnp.mean(x * x, axis=-1, keepdims=True) + NORM_EPS) * g


def _fn_rms_pre(g, x):
    return (_rms(g, x),)


def _fn_lora(w0, wup, a0, aup, xl):
    qw = w0 + _bmm(jnp.tanh(xl), wup)
    qa = a0 + _bmm(xl, aup)
    return qw, qa


def _fn_rwkv_pre(head, k_k, k_a, xk, qw, qa):
    w_log = -(jnp.maximum(-qw, 0.0) + jnp.log(1.0 + jnp.exp(-jnp.abs(qw)))) - 0.5
    lw = -jnp.exp(w_log)
    a_sig = _sigmoid(qa)
    kk = xk * k_k
    kk = kk / jnp.maximum(jnp.sqrt(_seg_sum(kk * kk, head)), 1e-12)
    k_h = xk * (1.0 + (a_sig - 1.0) * k_a)
    return lw, k_h, -kk, kk * a_sig


def _fn_rwkv_post(head, lnx_g, lnx_b, r_k, y, r, k_h, v, g):
    inv = 1.0 / head
    mu = _seg_sum(y, head) * inv
    d = y - mu
    var = _seg_sum(d * d, head) * inv
    yn = d * lax.rsqrt(var + 1e-5 * head) * lnx_g + lnx_b
    bonus = _seg_sum(r * k_h * r_k, head) * v
    return ((yn + bonus) * _silu(g),)


def _fn_conv_ln(cln_g, cln_b, c):
    mu = jnp.mean(c, axis=-1, keepdims=True)
    d = c - mu
    var = jnp.mean(d * d, axis=-1, keepdims=True)
    return (_silu(d * lax.rsqrt(var + LN_EPS) * cln_g + cln_b),)


def _fn_conv_post(b_pw2, c2, g):
    return ((c2 + b_pw2) * _silu(g),)


def _post(out, x, tgt, g, T, D, tm):
    def body(g_ref, o_ref, x_ref, t_ref, dout_ref, gx_ref, loss_ref, dg_ref):
        i = pl.program_id(0)
        o, vjp = jax.vjp(_rms, g_ref[...], o_ref[...])
        err = x_ref[...] + o - t_ref[...]
        d_y = err * (1.0 / D)
        dg, d_out = vjp(d_y)
        dout_ref[...] = d_out.astype(dout_ref.dtype)
        gx_ref[...] = d_y

        @pl.when(i == 0)
        def _():
            loss_ref[...] = jnp.zeros_like(loss_ref)
            dg_ref[...] = jnp.zeros_like(dg_ref)

        loss_ref[...] += jnp.sum(err * err, keepdims=True) * (0.5 / D)
        dg_ref[...] += dg

    row = pl.BlockSpec((tm, D), lambda i: (i, 0))
    vec = pl.BlockSpec((1, D), lambda i: (0, 0))
    return pl.pallas_call(
        body, name="post_loss", grid=(T // tm,),
        in_specs=[vec, row, row, row],
        out_specs=[row, row, pl.BlockSpec((1, 1), lambda i: (0, 0)), vec],
        out_shape=[jax.ShapeDtypeStruct((T, D), bf16), jax.ShapeDtypeStruct((T, D), f32),
                   jax.ShapeDtypeStruct((1, 1), f32), jax.ShapeDtypeStruct((1, D), f32)],
        compiler_params=_cparams(1),
    )(g, out, x, tgt)


def _rms_pre_bwd(x, g, dh, gx_res, T, D, tm):
    def body(g_ref, x_ref, dh_ref, res_ref, dx_ref, dg_ref):
        i = pl.program_id(0)
        _, vjp = jax.vjp(_rms, g_ref[...], x_ref[...])
        dg, dx = vjp(dh_ref[...].astype(f32))
        dx_ref[...] = dx + res_ref[...]

        @pl.when(i == 0)
        def _():
            dg_ref[...] = jnp.zeros_like(dg_ref)

        dg_ref[...] += dg

    row = pl.BlockSpec((tm, D), lambda i: (i, 0))
    vec = pl.BlockSpec((1, D), lambda i: (0, 0))
    return pl.pallas_call(
        body, name="rms_pre_bwd", grid=(T // tm,),
        in_specs=[vec, row, row, row], out_specs=[row, vec],
        out_shape=[jax.ShapeDtypeStruct((T, D), f32), jax.ShapeDtypeStruct((1, D), f32)],
        compiler_params=_cparams(1),
    )(g, x, dh, gx_res)


def _prev_rows(cur, halo_ref, first):
    top = jnp.where(first, 0.0, halo_ref[SUBLANES - 1:SUBLANES, :])
    rolled = pltpu.roll(cur, 1, 0)
    rid = lax.broadcasted_iota(jnp.int32, cur.shape, 0)
    return jnp.where(rid == 0, top, rolled)


def _shift_fwd(name, proj, off, width, mu, T, tm):
    cw = _tile(width, 512)
    ncol, cb = width // cw, off // cw
    hb = tm // SUBLANES

    def body(mu_ref, cur_ref, halo_ref, o_ref):
        i = pl.program_id(1)
        cur = cur_ref[...]
        prev = _prev_rows(cur, halo_ref, i == 0)
        o_ref[...] = cur + (prev - cur) * mu_ref[...]

    return pl.pallas_call(
        body, name=name, grid=(ncol, T // tm),
        in_specs=[pl.BlockSpec((1, cw), lambda j, i: (0, j)),
                  pl.BlockSpec((tm, cw), lambda j, i: (i, cb + j)),
                  pl.BlockSpec((SUBLANES, cw), lambda j, i: (jnp.maximum(i * hb - 1, 0), cb + j))],
        out_specs=pl.BlockSpec((tm, cw), lambda j, i: (i, j)),
        out_shape=jax.ShapeDtypeStruct((T, width), f32),
        compiler_params=_cparams(2),
    )(mu, proj, proj)


def _shift_bwd(name, proj, off, width, mu, dxs, T, tm):
    cw = _tile(width, 512)
    ncol, cb = width // cw, off // cw
    hb, nt = tm // SUBLANES, T // tm

    def body(mu_ref, cur_ref, halo_ref, d_ref, dnext_ref, o_ref, dmu_ref):
        i = pl.program_id(1)
        cur = cur_ref[...]
        prev = _prev_rows(cur, halo_ref, i == 0)
        d = d_ref[...]
        bottom = jnp.where(i == nt - 1, 0.0, dnext_ref[0:1, :])
        rid = lax.broadcasted_iota(jnp.int32, d.shape, 0)
        d_next = jnp.where(rid == tm - 1, bottom, pltpu.roll(d, tm - 1, 0))
        mu_v = mu_ref[...]
        o_ref[...] = (d * (1.0 - mu_v) + d_next * mu_v).astype(o_ref.dtype)

        @pl.when(i == 0)
        def _():
            dmu_ref[...] = jnp.zeros_like(dmu_ref)

        dmu_ref[...] += jnp.sum(d * (prev - cur), axis=0, keepdims=True)

    return pl.pallas_call(
        body, name=name, grid=(ncol, nt),
        in_specs=[pl.BlockSpec((1, cw), lambda j, i: (0, j)),
                  pl.BlockSpec((tm, cw), lambda j, i: (i, cb + j)),
                  pl.BlockSpec((SUBLANES, cw), lambda j, i: (jnp.maximum(i * hb - 1, 0), cb + j)),
                  pl.BlockSpec((tm, cw), lambda j, i: (i, j)),
                  pl.BlockSpec((SUBLANES, cw), lambda j, i: (jnp.minimum((i + 1) * hb, nt * hb - 1), j))],
        out_specs=[pl.BlockSpec((tm, cw), lambda j, i: (i, j)), pl.BlockSpec((1, cw), lambda j, i: (0, j))],
        out_shape=[jax.ShapeDtypeStruct((T, width), bf16), jax.ShapeDtypeStruct((1, width), f32)],
        compiler_params=_cparams(2),
    )(mu, proj, proj, dxs, dxs)


def _rolled_copies(dst_ref, ext):
    n = ext.shape[0]
    dst_ref[0] = ext
    for r in range(1, SUBLANES):
        dst_ref[r] = pltpu.roll(ext, n - r, 0)


def _window(rolled_ref, start, rows):
    q, r = divmod(start, SUBLANES)
    return rolled_ref[r, pl.ds(SUBLANES * q, rows), :]


def _conv_fwd(proj, off_v, off_g, width, conv_w, conv_b, ktaps, T, tm):
    cw = _tile(width, 512)
    ncol = width // cw
    hb = tm // CONV_HALO
    lead = CONV_HALO - (ktaps - 1)

    def body(w_ref, b_ref, v_ref, g_ref, hv_ref, hg_ref, o_ref, u_ref):
        i = pl.program_id(1)
        halo = hv_ref[...] * _sigmoid(hg_ref[...])
        _rolled_copies(u_ref, jnp.concatenate([jnp.where(i == 0, 0.0, halo), v_ref[...] * _sigmoid(g_ref[...])], axis=0))
        acc = jnp.zeros((tm, cw), f32) + b_ref[...]
        for j in range(ktaps):
            acc = acc + _window(u_ref, lead + j, tm) * w_ref[j:j + 1, :]
        o_ref[...] = acc

    def tile(off):
        return pl.BlockSpec((tm, cw), lambda j, i: (i, off // cw + j))

    def halo(off):
        return pl.BlockSpec((CONV_HALO, cw), lambda j, i: (jnp.maximum(i * hb - 1, 0), off // cw + j))

    return pl.pallas_call(
        body, name="conv_fwd", grid=(ncol, T // tm),
        in_specs=[pl.BlockSpec((CONV_HALO, cw), lambda j, i: (0, j)), pl.BlockSpec((1, cw), lambda j, i: (0, j)),
                  tile(off_v), tile(off_g), halo(off_v), halo(off_g)],
        out_specs=pl.BlockSpec((tm, cw), lambda j, i: (i, j)),
        out_shape=jax.ShapeDtypeStruct((T, width), f32),
        scratch_shapes=[pltpu.VMEM((SUBLANES, CONV_HALO + tm, cw), f32)],
        compiler_params=_cparams(2),
    )(conv_w, conv_b, proj, proj, proj, proj)


def _conv_bwd(proj, off_v, off_g, width, conv_w, dc, ktaps, T, tm):
    cw = _tile(width, 512)
    ncol = width // cw
    hb, nt = tm // CONV_HALO, T // tm
    lead = CONV_HALO - (ktaps - 1)

    def body(w_ref, v_ref, g_ref, hv_ref, hg_ref, dc_ref, dcn_ref, dv_ref, dg_ref, dw_ref, db_ref, u_ref, d_ref):
        i = pl.program_id(1)
        halo = hv_ref[...] * _sigmoid(hg_ref[...])
        sig = _sigmoid(g_ref[...])
        gv = v_ref[...]
        _rolled_copies(u_ref, jnp.concatenate([jnp.where(i == 0, 0.0, halo), gv * sig], axis=0))
        dcur = dc_ref[...]
        _rolled_copies(d_ref, jnp.concatenate([dcur, jnp.where(i == nt - 1, 0.0, dcn_ref[...])], axis=0))

        @pl.when(i == 0)
        def _():
            dw_ref[...] = jnp.zeros_like(dw_ref)
            db_ref[...] = jnp.zeros_like(db_ref)

        du = jnp.zeros((tm, cw), f32)
        for j in range(ktaps):
            du = du + _window(d_ref, ktaps - 1 - j, tm) * w_ref[j:j + 1, :]
            dw_ref[j:j + 1, :] += jnp.sum(_window(u_ref, lead + j, tm) * dcur, axis=0, keepdims=True)
        db_ref[...] += jnp.sum(dcur, axis=0, keepdims=True)
        dv_ref[...] = (du * sig).astype(dv_ref.dtype)
        dg_ref[...] = (du * gv * sig * (1.0 - sig)).astype(dg_ref.dtype)

    def tile(off):
        return pl.BlockSpec((tm, cw), lambda j, i: (i, off // cw + j))

    def halo(off):
        return pl.BlockSpec((CONV_HALO, cw), lambda j, i: (jnp.maximum(i * hb - 1, 0), off // cw + j))

    return pl.pallas_call(
        body, name="conv_bwd", grid=(ncol, nt),
        in_specs=[pl.BlockSpec((CONV_HALO, cw), lambda j, i: (0, j)),
                  tile(off_v), tile(off_g), halo(off_v), halo(off_g),
                  pl.BlockSpec((tm, cw), lambda j, i: (i, j)),
                  pl.BlockSpec((CONV_HALO, cw), lambda j, i: (jnp.minimum((i + 1) * hb, nt * hb - 1), j))],
        out_specs=[pl.BlockSpec((tm, cw), lambda j, i: (i, j)), pl.BlockSpec((tm, cw), lambda j, i: (i, j)),
                   pl.BlockSpec((CONV_HALO, cw), lambda j, i: (0, j)), pl.BlockSpec((1, cw), lambda j, i: (0, j))],
        out_shape=[jax.ShapeDtypeStruct((T, width), bf16), jax.ShapeDtypeStruct((T, width), bf16),
                   jax.ShapeDtypeStruct((CONV_HALO, width), f32), jax.ShapeDtypeStruct((1, width), f32)],
        scratch_shapes=[pltpu.VMEM((SUBLANES, CONV_HALO + tm, cw), f32), pltpu.VMEM((SUBLANES, tm + CONV_HALO, cw), f32)],
        compiler_params=_cparams(2),
    )(conv_w, proj, proj, proj, proj, dc, dc)


def _each(f, *lists):
    return [f(*xs) for xs in zip(*lists)]


def _wkv_local(r, lw, k, v, a, b):
    C = r[0].shape[0]
    P = WKV_PREC
    row = lax.broadcasted_iota(jnp.int32, (C, C), 0)
    col = lax.broadcasted_iota(jnp.int32, (C, C), 1)
    incl, strict = row >= col, row > col
    tri = incl.astype(f32)
    zero = jnp.zeros((C, C), f32)
    G = _each(lambda x: _mm(tri, x, P), lw)
    to_end = _each(lambda x, g: jnp.exp(jnp.sum(x, axis=0, keepdims=True) - g), lw, G)
    e_g = _each(jnp.exp, G)
    e_ng = _each(lambda g: jnp.exp(-g), G)
    At = _each(lambda x, g, w: x * jnp.exp(g - w), a, G, lw)
    Rt = _each(jnp.multiply, r, e_g)
    Kt = _each(jnp.multiply, k, e_ng)
    Bt = _each(jnp.multiply, b, e_ng)
    sc = _each(lambda at, rt, bt, kt: _mm_nt(jnp.concatenate([at, rt], axis=0), jnp.concatenate([bt, kt], axis=0), P),
               At, Rt, Bt, Kt)
    L = _each(lambda s: jnp.where(strict, s[:C, :C], zero), sc)
    M = _each(lambda s: jnp.where(strict, s[:C, C:], zero), sc)
    Pb = _each(lambda s: jnp.where(incl, s[C:, :C], zero), sc)
    Pk = _each(lambda s: jnp.where(incl, s[C:, C:], zero), sc)
    MPk = _each(lambda m, p, x: _bmm(jnp.concatenate([m, p], axis=0), x), M, Pk, v)
    WU = _each(lambda at, mp: jnp.concatenate([at, mp[:C]], axis=1), At, MPk)
    Lp = L
    n = 1
    while n < C:
        n *= 2
        if n < C:
            step = _each(lambda l, x: _bmm(l, jnp.concatenate([x, l], axis=1)), Lp, WU)
            WU = _each(lambda x, s: x + s[:, :x.shape[1]], WU, step)
            Lp = _each(lambda x, s: s[:, x.shape[1]:], WU, step)
        else:
            WU = _each(lambda x, l: x + _bmm(l, x), WU, Lp)
    N = r[0].shape[1]
    W = _each(lambda x: x[:, :N], WU)
    U = _each(lambda x: x[:, N:], WU)
    Y0 = _each(lambda mp: mp[C:], MPk)
    Bend = _each(jnp.multiply, b, to_end)
    Z = _each(lambda x, y, e: _mm_tn(x, y * e, P), v, k, to_end)
    return W, U, Rt, Pb, Y0, Bend, Z


def _wkv_state(S0, W, U, Rt, Pb, Bend, lw, Y0, Z):
    P = WKV_PREC
    C = W[0].shape[0]
    WR = _each(lambda w, rt, s: _mm_nt(jnp.concatenate([w, rt], axis=0), s, P), W, Rt, S0)
    X = _each(lambda wr, u: wr[:C] + u, WR, U)
    y = _each(lambda p, x, wr, c: _mm(p, x, P) + wr[C:] + c, Pb, X, WR, Y0)
    S1 = _each(lambda s, w, x, e, z: s * jnp.exp(jnp.sum(w, axis=0, keepdims=True)) + _mm_tn(x, e, P) + z,
               S0, lw, X, Bend, Z)
    return y, S1


def _wkv_dims(head, T, RW, heads_per_step):
    C = min(WKV_CHUNK, T)
    nh = RW // head
    hb = min(heads_per_step, nh)
    return C, nh, hb, hb * head, T // C


def _heads(ref, hb, head):
    return [ref[:, h * head:(h + 1) * head] for h in range(hb)]


def _put_heads(ref, vals, head):
    for h, val in enumerate(vals):
        ref[:, h * head:(h + 1) * head] = val


def _wkv_local_fwd(r, lw, k, v, a, b, head, T, RW):
    C, nh, hb, bw, nc = _wkv_dims(head, T, RW, WKV_HEADS)

    def body(*refs):
        ins, outs = refs[:6], refs[6:]
        res = _wkv_local(*[_heads(x, hb, head) for x in ins])
        for o_ref, vals in zip(outs[:6], res[:6]):
            _put_heads(o_ref, vals, head)
        for h in range(hb):
            outs[6][0, h] = res[6][h]

    blk = pl.BlockSpec((C, bw), lambda g, c: (c, g))
    sq = pl.BlockSpec((1, hb, head, head), lambda g, c: (c, g, 0, 0))
    return pl.pallas_call(
        body, name="wkv_local", grid=(nh // hb, nc),
        in_specs=[blk] * 6, out_specs=[blk] * 6 + [sq],
        out_shape=[jax.ShapeDtypeStruct((T, RW), f32)] * 6 + [jax.ShapeDtypeStruct((nc, nh, head, head), f32)],
        compiler_params=_cparams(2),
    )(r, lw, k, v, a, b)


def _wkv_state_fwd(W, U, Rt, Pb, Bend, lw, Y0, Z, head, T, RW):
    C, nh, hb, bw, nc = _wkv_dims(head, T, RW, WKV_STATE_HEADS)

    def body(w_ref, u_ref, rt_ref, pb_ref, be_ref, lw_ref, y0_ref, z_ref, y_ref, st_ref, s_ref):
        @pl.when(pl.program_id(1) == 0)
        def _():
            s_ref[...] = jnp.zeros_like(s_ref)

        S0 = [s_ref[h] for h in range(hb)]
        for h in range(hb):
            st_ref[0, h] = S0[h]
        rows = [_heads(x, hb, head) for x in (w_ref, u_ref, rt_ref, pb_ref, be_ref, lw_ref, y0_ref)]
        y, S1 = _wkv_state(S0, *rows, [z_ref[0, h] for h in range(hb)])
        _put_heads(y_ref, y, head)
        for h in range(hb):
            s_ref[h] = S1[h]

    blk = pl.BlockSpec((C, bw), lambda g, c: (c, g))
    sq = pl.BlockSpec((1, hb, head, head), lambda g, c: (c, g, 0, 0))
    return pl.pallas_call(
        body, name="wkv_state", grid=(nh // hb, nc),
        in_specs=[blk] * 7 + [sq], out_specs=[blk, sq],
        out_shape=[jax.ShapeDtypeStruct((T, RW), f32), jax.ShapeDtypeStruct((nc, nh, head, head), f32)],
        scratch_shapes=[pltpu.VMEM((hb, head, head), f32)],
        compiler_params=_cparams(2),
    )(W, U, Rt, Pb, Bend, lw, Y0, Z)


def _wkv_state_bwd(W, U, Rt, Pb, Bend, lw, Y0, Z, states, dy, head, T, RW):
    C, nh, hb, bw, nc = _wkv_dims(head, T, RW, WKV_STATE_HEADS)

    def body(w_ref, u_ref, rt_ref, pb_ref, be_ref, lw_ref, y0_ref, z_ref, st_ref, dy_ref,
             dw_ref, du_ref, drt_ref, dpb_ref, dbe_ref, dlw_ref, dz_ref, ds_ref):
        @pl.when(pl.program_id(1) == 0)
        def _():
            ds_ref[...] = jnp.zeros_like(ds_ref)

        dS1 = [ds_ref[h] for h in range(hb)]
        for h in range(hb):
            dz_ref[0, h] = dS1[h]
        rows = [_heads(x, hb, head) for x in (w_ref, u_ref, rt_ref, pb_ref, be_ref, lw_ref)]
        Y0 = _heads(y0_ref, hb, head)
        Zs = [z_ref[0, h] for h in range(hb)]
        _, vjp = jax.vjp(lambda s0, *rw: _wkv_state(s0, *rw, Y0, Zs), [st_ref[0, h] for h in range(hb)], *rows)
        grads = vjp((_heads(dy_ref, hb, head), dS1))
        for o_ref, vals in zip((dw_ref, du_ref, drt_ref, dpb_ref, dbe_ref, dlw_ref), grads[1:]):
            _put_heads(o_ref, vals, head)
        for h in range(hb):
            ds_ref[h] = grads[0][h]

    blk = pl.BlockSpec((C, bw), lambda g, c: (nc - 1 - c, g))
    sq = pl.BlockSpec((1, hb, head, head), lambda g, c: (nc - 1 - c, g, 0, 0))
    return pl.pallas_call(
        body, name="wkv_state_bwd", grid=(nh // hb, nc),
        in_specs=[blk] * 7 + [sq, sq, blk], out_specs=[blk] * 6 + [sq],
        out_shape=[jax.ShapeDtypeStruct((T, RW), f32)] * 6 + [jax.ShapeDtypeStruct((nc, nh, head, head), f32)],
        scratch_shapes=[pltpu.VMEM((hb, head, head), f32)],
        compiler_params=_cparams(2),
    )(W, U, Rt, Pb, Bend, lw, Y0, Z, states, dy)


def _wkv_local_bwd(r, lw, k, v, a, b, cots, d_lw_x, dr_x, dk_x, dv_x, head, T, RW):
    C, nh, hb, bw, nc = _wkv_dims(head, T, RW, WKV_HEADS)

    def body(*refs):
        ins, cot_refs, add_refs, outs = refs[:6], refs[6:13], refs[13:17], refs[17:]
        _, vjp = jax.vjp(_wkv_local, *[_heads(x, hb, head) for x in ins])
        cts = [_heads(x, hb, head) for x in cot_refs[:6]] + [[cot_refs[6][0, h] for h in range(hb)]]
        dr, dlw, dk, dv, da, db = vjp(tuple(cts))
        dlw_x, drx, dkx, dvx = [_heads(x, hb, head) for x in add_refs]
        _put_heads(outs[0], _each(jnp.add, dr, drx), head)
        _put_heads(outs[1], _each(jnp.add, dlw, dlw_x), head)
        _put_heads(outs[2], _each(jnp.add, dk, dkx), head)
        _put_heads(outs[3], _each(jnp.add, dv, dvx), head)
        _put_heads(outs[4], da, head)
        _put_heads(outs[5], db, head)

    blk = pl.BlockSpec((C, bw), lambda g, c: (c, g))
    sq = pl.BlockSpec((1, hb, head, head), lambda g, c: (c, g, 0, 0))
    return pl.pallas_call(
        body, name="wkv_local_bwd", grid=(nh // hb, nc),
        in_specs=[blk] * 12 + [sq] + [blk] * 4, out_specs=[blk] * 6,
        out_shape=[jax.ShapeDtypeStruct((T, RW), f32)] * 6,
        compiler_params=_cparams(2),
    )(r, lw, k, v, a, b, *cots, d_lw_x, dr_x, dk_x, dv_x)


def _rows_tile(R, row_bytes, budget, mult=SUBLANES):
    best = None
    t = mult
    while t <= R:
        if R % t == 0 and t * row_bytes <= budget:
            best = t
        t += mult
    return best if best is not None else R


def _sum_slots(name, parts):
    S, R, W = parts.shape
    budget = 4 << 20
    tr = _rows_tile(R, S * W * 4, budget, 2 * SUBLANES)
    cw = W if tr * S * W * 4 <= 2 * budget else _tile(W, max(LANES, 2 * budget // (S * tr * 4)))

    def body(p_ref, o_ref):
        acc = p_ref[0].astype(f32)
        for d in range(1, S):
            acc = acc + p_ref[d].astype(f32)
        o_ref[...] = acc

    return pl.pallas_call(
        body, name=name, grid=(R // tr, W // cw),
        in_specs=[pl.BlockSpec((S, tr, cw), lambda i, j: (0, i, j))],
        out_specs=pl.BlockSpec((tr, cw), lambda i, j: (i, j)),
        out_shape=jax.ShapeDtypeStruct((R, W), f32),
        compiler_params=_cparams(2),
    )(parts)


def _tile2d(R, W, budget, mult):
    tr = _rows_tile(R, LANES * 4, budget, mult)
    cw = _tile(W, max(LANES, budget // (tr * 4))) if W % LANES == 0 else W
    return tr, cw


def _core_index():
    return lax.axis_index("c").astype(jnp.int32).reshape(1)


def _add_kept(name, shards, got):
    _, _, R, W = shards.shape
    tr, cw = _tile2d(R, W, 2 << 20, 2 * SUBLANES)

    def body(core_ref, a_ref, b_ref, o_ref):
        o_ref[...] = (a_ref[...].astype(f32) + b_ref[...].astype(f32)).astype(o_ref.dtype)

    return pl.pallas_call(
        body, name=name,
        grid_spec=pltpu.PrefetchScalarGridSpec(
            num_scalar_prefetch=1, grid=(4, R // tr, W // cw),
            in_specs=[pl.BlockSpec((None, None, tr, cw), lambda s, i, j, core: (s, core[0], i, j)),
                      pl.BlockSpec((None, tr, cw), lambda s, i, j, core: (s, i, j))],
            out_specs=pl.BlockSpec((None, tr, cw), lambda s, i, j, core: (s, i, j))),
        out_shape=jax.ShapeDtypeStruct((4, R, W), shards.dtype), compiler_params=_cparams(3),
    )(_core_index(), shards, got)


def _sum_slots_half(name, parts):
    S, R, W = parts.shape
    budget = 4 << 20
    tr = _rows_tile(R, S * W * 4, budget, 2 * SUBLANES)
    cw = W if tr * S * W * 4 <= 2 * budget else _tile(W, max(LANES, 2 * budget // (S * tr * 4)))
    nrow = R // tr

    def body(core_ref, p_ref, o_ref):
        acc = p_ref[0].astype(f32)
        for d in range(1, S):
            acc = acc + p_ref[d].astype(f32)
        o_ref[...] = acc

    return pl.pallas_call(
        body, name=name,
        grid_spec=pltpu.PrefetchScalarGridSpec(
            num_scalar_prefetch=1, grid=(nrow, W // cw),
            in_specs=[pl.BlockSpec((S, tr, cw), lambda i, j, core: (0, i, j))],
            out_specs=pl.BlockSpec((tr, cw), lambda i, j, core: (core[0] * nrow + i, j))),
        out_shape=jax.ShapeDtypeStruct((2 * R, W), f32), compiler_params=_cparams(2),
    )(_core_index(), parts)


def _adamw(name, w, g, m, v):
    R, W = w.shape
    tr, cw = _tile2d(R, W, 2 << 20, SUBLANES)

    def body(w_ref, g_ref, m_ref, v_ref, d_ref, nm_ref, nv_ref):
        g_v = g_ref[...]
        nm = ADAM_B1 * m_ref[...] + (1.0 - ADAM_B1) * g_v
        nv = ADAM_B2 * v_ref[...] + (1.0 - ADAM_B2) * (g_v * g_v)
        m_hat = nm / (1.0 - ADAM_B1 ** ADAM_STEP)
        v_hat = nv / (1.0 - ADAM_B2 ** ADAM_STEP)
        d_ref[...] = -ADAM_LR * (m_hat / (jnp.sqrt(v_hat) + ADAM_EPS) + ADAM_WD * w_ref[...])
        nm_ref[...] = nm
        nv_ref[...] = nv

    blk = pl.BlockSpec((tr, cw), lambda i, j: (i, j))
    return pl.pallas_call(
        body, name=name, grid=(R // tr, W // cw),
        in_specs=[blk] * 4, out_specs=[blk] * 3,
        out_shape=[jax.ShapeDtypeStruct((R, W), f32)] * 3,
        compiler_params=_cparams(2),
    )(w, g, m, v)


ANY = pl.BlockSpec(memory_space=pl.ANY)


def _place():
    return lax.axis_index("x"), lax.axis_index("y"), lax.axis_index("c")


class _Comm:
    def __init__(self, operands, out_shape, scratch, start, wait):
        self.operands, self.out_shape, self.scratch, self.start, self.wait = operands, out_shape, scratch, start, wait


def _run_comm(name, comm):
    n = len(comm.operands)

    def body(*refs):
        parts = (refs[:n], refs[n:2 * n], refs[2 * n:])
        comm.start(*parts)
        comm.wait(*parts)

    return pl.pallas_call(
        body, name=name, in_specs=[ANY] * n, out_specs=[ANY] * n, out_shape=comm.out_shape,
        scratch_shapes=comm.scratch,
    )(*comm.operands)


def _copy_chunks(rows, cols):
    k = SHARE_CHUNKS // 2
    if rows % (k * 2 * SUBLANES) == 0:
        return [(pl.ds(q * (rows // k), rows // k), pl.ds(0, cols)) for q in range(k)]
    if cols % (k * LANES) == 0:
        return [(pl.ds(0, rows), pl.ds(q * (cols // k), cols // k)) for q in range(k)]
    return [(pl.ds(0, rows), pl.ds(0, cols))]


def _gather_chips(arrays):
    n = len(arrays)
    parts = [(a, h, blk) for a, arr in enumerate(arrays) for h in range(2) for blk in _copy_chunks(*arr.shape[1:])]

    def copies(ins, outs, sems):
        send_sems, recv_sems, local_sems = sems
        x, y, c = _place()
        mine = 2 * x + y
        sib = (x, y, 1 - c)
        chips = [(1 - x, y), (x, 1 - y), (1 - x, 1 - y)]

        def local():
            return [pltpu.make_async_copy(ins[a].at[(h, *blk)], outs[a].at[(mine, h, *blk)], local_sems.at[p])
                    for p, (a, h, blk) in enumerate(parts)]

        def over_ici(a, j, slot):
            px, py = chips[j]
            return pltpu.make_async_remote_copy(
                src_ref=ins[a].at[c], dst_ref=outs[a].at[slot, c], send_sem=send_sems.at[3 * a + j],
                recv_sem=recv_sems.at[3 * a + j], device_id=(px, py, c), device_id_type=MESH)

        def over_d2d(a, j, half):
            px, py = chips[j]
            slot = 2 * px + py
            return pltpu.make_async_remote_copy(
                src_ref=outs[a].at[slot, half], dst_ref=outs[a].at[slot, half], send_sem=send_sems.at[3 * (n + a) + j],
                recv_sem=recv_sems.at[3 * (n + a) + j], device_id=sib, device_id_type=MESH)

        pairs = [(a, j) for a in range(n) for j in range(3)]
        return dict(local=local,
                    sends=lambda: [over_ici(a, j, mine) for a, j in pairs],
                    landing=lambda: [over_ici(a, j, 2 * chips[j][0] + chips[j][1]) for a, j in pairs],
                    passed=lambda: [over_d2d(a, j, c) for a, j in pairs],
                    from_sib=lambda: [over_d2d(a, j, 1 - c) for a, j in pairs])

    def start(ins, outs, sems):
        cps = copies(ins, outs, sems)
        for cp in cps["local"]() + cps["sends"]():
            cp.start()

    def wait(ins, outs, sems):
        cps = copies(ins, outs, sems)
        passed = cps["passed"]()
        for got, on in zip(cps["landing"](), passed):
            got.wait_recv()
            on.start()
        for cp in cps["from_sib"]():
            cp.wait_recv()
        for cp in cps["sends"]() + passed:
            cp.wait_send()
        for cp in cps["local"]():
            cp.wait()

    return _Comm(arrays, [jax.ShapeDtypeStruct((4,) + a.shape, a.dtype) for a in arrays],
                 [pltpu.SemaphoreType.DMA((6 * n,)), pltpu.SemaphoreType.DMA((6 * n,)),
                  pltpu.SemaphoreType.DMA((len(parts),))], start, wait)


def _exchange_chips(pieces, whole):
    n, m = len(pieces), len(whole)
    parts = [(a, blk) for a, arr in enumerate(pieces) for blk in _copy_chunks(*arr.shape[1:])]

    def copies(ins, outs, sems):
        send_sems, recv_sems, local_sems = sems
        x, y, c = _place()
        chip, dev = 2 * x + y, 4 * x + 2 * y + c
        chips = [(1 - x, y), (x, 1 - y), (1 - x, 1 - y)]
        peers = [(x ^ (k >> 2), y ^ ((k >> 1) & 1), c ^ (k & 1)) for k in range(1, 8)]
        def local():
            cps = [pltpu.make_async_copy(ins[a].at[(chip, *blk)], outs[a].at[(chip, *blk)], local_sems.at[p])
                   for p, (a, blk) in enumerate(parts)]
            return cps + [pltpu.make_async_copy(ins[n + b], outs[n + b].at[dev], local_sems.at[len(parts) + b])
                          for b in range(m)]

        def piece(a, j, slot_from):
            px, py = chips[j]
            return pltpu.make_async_remote_copy(
                src_ref=ins[a].at[2 * px + py], dst_ref=outs[a].at[slot_from], send_sem=send_sems.at[3 * a + j],
                recv_sem=recv_sems.at[3 * a + j], device_id=(px, py, c), device_id_type=MESH)

        def everyone(b, j, slot_from):
            px, py, pc = peers[j]
            return pltpu.make_async_remote_copy(
                src_ref=ins[n + b], dst_ref=outs[n + b].at[slot_from], send_sem=send_sems.at[3 * n + 7 * b + j],
                recv_sem=recv_sems.at[3 * n + 7 * b + j], device_id=(px, py, pc), device_id_type=MESH)

        def sends():
            return ([everyone(b, j, dev) for b in range(m) for j in range(7)]
                    + [piece(a, j, chip) for a in range(n) for j in range(3)])

        def landing():
            return ([everyone(b, j, 4 * px + 2 * py + pc) for b in range(m) for j, (px, py, pc) in enumerate(peers)]
                    + [piece(a, j, 2 * px + py) for a in range(n) for j, (px, py) in enumerate(chips)])

        return local, sends, landing

    def start(ins, outs, sems):
        local, sends, _ = copies(ins, outs, sems)
        for cp in local() + sends():
            cp.start()

    def wait(ins, outs, sems):
        local, sends, landing = copies(ins, outs, sems)
        for cp in landing():
            cp.wait_recv()
        for cp in sends():
            cp.wait_send()
        for cp in local():
            cp.wait()

    shapes = [jax.ShapeDtypeStruct(a.shape, a.dtype) for a in pieces]
    shapes += [jax.ShapeDtypeStruct((8,) + a.shape, a.dtype) for a in whole]
    nsem = 3 * n + 7 * m
    return _Comm(list(pieces) + list(whole), shapes,
                 [pltpu.SemaphoreType.DMA((nsem,)), pltpu.SemaphoreType.DMA((nsem,)),
                  pltpu.SemaphoreType.DMA((len(parts) + m,))], start, wait)


def _pair_exchange(name, shards):
    n = len(shards)

    def body(*refs):
        ins, outs = refs[:n], refs[n:2 * n]
        send_sems, recv_sems = refs[2 * n:]
        x, y, c = _place()
        copies = [pltpu.make_async_remote_copy(
            src_ref=ins[a].at[s, 1 - c], dst_ref=outs[a].at[s], send_sem=send_sems.at[4 * a + s],
            recv_sem=recv_sems.at[4 * a + s], device_id=(x, y, 1 - c), device_id_type=MESH)
            for a in range(n) for s in range(4)]
        for cp in copies:
            cp.start()
        for cp in copies:
            cp.wait_recv()
        for cp in copies:
            cp.wait_send()

    return pl.pallas_call(
        body, name=name, in_specs=[ANY] * n, out_specs=[ANY] * n,
        out_shape=[jax.ShapeDtypeStruct((4,) + a.shape[2:], a.dtype) for a in shards],
        scratch_shapes=[pltpu.SemaphoreType.DMA((4 * n,)), pltpu.SemaphoreType.DMA((4 * n,))],
    )(*shards)


def _share_halves(name, arrays):
    n = len(arrays)

    def body(*refs):
        bufs = refs[n:2 * n]
        send_sems, recv_sems = refs[2 * n:]
        x, y, c = _place()

        def half(a, h):
            rows = arrays[a].shape[0] // 2
            return bufs[a].at[pl.ds(pl.multiple_of(h * rows, SUBLANES), rows)]

        def copy(a, h):
            return pltpu.make_async_remote_copy(
                src_ref=half(a, h), dst_ref=half(a, h), send_sem=send_sems.at[a], recv_sem=recv_sems.at[a],
                device_id=(x, y, 1 - c), device_id_type=MESH)

        sends = [copy(a, c) for a in range(n)]
        for cp in sends:
            cp.start()
        for a in range(n):
            copy(a, 1 - c).wait_recv()
        for cp in sends:
            cp.wait_send()

    return pl.pallas_call(
        body, name=name, in_specs=[ANY] * n, out_specs=[ANY] * n,
        out_shape=[jax.ShapeDtypeStruct(a.shape, a.dtype) for a in arrays],
        input_output_aliases={a: a for a in range(n)},
        scratch_shapes=[pltpu.SemaphoreType.DMA((n,)), pltpu.SemaphoreType.DMA((n,))],
    )(*arrays)


def _place_blocks(blocks, axis):
    shape = list(blocks[0].shape)
    shape[axis] = sum(b.shape[axis] for b in blocks)
    buf = lax.empty(tuple(shape), blocks[0].dtype)
    at = 0
    for b in blocks:
        buf = lax.dynamic_update_slice_in_dim(buf, b, at, axis)
        at += b.shape[axis]
    return buf


def kernel(x, norm_pre_g, w_in, mu_shift, w0, w_lora_up, a0, a_lora_up, k_k, k_a, r_k, lnx_g, lnx_b, conv_w, conv_b, cln_g, cln_b, w_pw2, b_pw2, w_out, norm_post_g, loss_target, m_norm_pre_g, m_w_in, m_mu_shift, m_w0, m_w_lora_up, m_a0, m_a_lora_up, m_k_k, m_k_a, m_r_k, m_lnx_g, m_lnx_b, m_conv_w, m_conv_b, m_cln_g, m_cln_b, m_w_pw2, m_b_pw2, m_w_out, m_norm_post_g, v_norm_pre_g, v_w_in, v_mu_shift, v_w0, v_w_lora_up, v_a0, v_a_lora_up, v_k_k, v_k_a, v_r_k, v_lnx_g, v_lnx_b, v_conv_w, v_conv_b, v_cln_g, v_cln_b, v_w_pw2, v_b_pw2, v_w_out, v_norm_post_g):
    _, T, D = x.shape
    RW = w0.shape[0]
    CW = conv_b.shape[0]
    head = r_k.shape[1]
    lora = w_lora_up.shape[0]
    ktaps = conv_w.shape[0]
    assert RW == CW and 2 * lora <= LORA_PAD and ktaps - 1 <= CONV_HALO
    n_in = 3 * RW + 2 * lora + RW + 3 * CW
    shard = n_in // 4
    PW = 7 * RW + LORA_PAD
    off_l = 7 * RW
    tm = min(256, T // 2)
    tm_wide = min(128, T // 2)
    tm_halo = min(512, T // 2)
    row = lambda vec: vec.reshape(1, -1)
    x2, tgt2 = x[0], loss_target[0]

    halves = lambda a: a.reshape(2, a.shape[0] // 2, a.shape[1])
    conv_w_p = jnp.concatenate([conv_w, jnp.zeros((CONV_HALO - ktaps, CW // 4), f32)], axis=0)
    w_in_t, m_w_in_t, v_w_in_t = w_in.T, m_w_in.T, v_w_in.T
    (g_win,) = _run_comm("gather_w_in", _gather_chips([halves(w_in_t.astype(bf16))]))
    win_t = g_win.reshape(n_in, D)
    lo = 3 * RW
    wp_t = _place_blocks([win_t[:lo], win_t[lo + 2 * lora:], win_t[lo:lo + 2 * lora],
                          jnp.zeros((LORA_PAD - 2 * lora, D), bf16)], axis=0)
    npg = row(norm_pre_g)
    (h,) = _row_fwd("rms_pre", _fn_rms_pre, [(npg, False)], [(x2, 0, D, False)], [(D, bf16)], T, tm)
    others = [halves(a) for a in (w_lora_up, a_lora_up, conv_w_p, w_pw2.astype(bf16), w_out.astype(bf16))]
    proj, (g_wup, g_aup, g_cw, g_pw2, g_wout) = _matmul("proj", h, wp_t, "nt", f32, comm=_gather_chips(others))
    cat_cols = lambda g: jnp.concatenate([g[s].reshape(-1, g.shape[-1]) for s in range(4)], axis=1)
    wup_full, aup_full, cw_p = cat_cols(g_wup), cat_cols(g_aup), cat_cols(g_cw)
    zl = lambda n: jnp.zeros((n, RW), f32)
    wup_p = jnp.concatenate([wup_full, zl(LORA_PAD - lora)], axis=0)
    aup_p = jnp.concatenate([zl(lora), aup_full, zl(LORA_PAD - 2 * lora)], axis=0)
    pw2_full = g_pw2.reshape(CW, CW)
    wout_full = g_wout.reshape(RW + CW, D)
    mu_r, mu_k, mu_v = (row(mu_shift[s * RW:(s + 1) * RW]) for s in range(3))
    mu_l = row(jnp.concatenate([mu_shift[3 * RW:], jnp.zeros((LORA_PAD - 2 * lora,), f32)]))

    xs_r = _shift_fwd("shift_r", proj, 0, RW, mu_r, T, tm_halo)
    xs_k = _shift_fwd("shift_k", proj, RW, RW, mu_k, T, tm_halo)
    xs_v = _shift_fwd("shift_v", proj, 2 * RW, RW, mu_v, T, tm_halo)
    xs_l = _shift_fwd("shift_l", proj, off_l, LORA_PAD, mu_l, T, tm_halo)
    lora_params = [(row(w0), False), (wup_p, False), (row(a0), False), (aup_p, False)]
    qw, qa = _row_fwd("lora_up", _fn_lora, lora_params, [(xs_l, 0, LORA_PAD, False)], [(RW, f32), (RW, f32)], T, tm)
    ncol = RW // _tile(RW, 512)
    fn_pre = functools.partial(_fn_rwkv_pre, head)
    pre_params = [(row(k_k), True), (row(k_a), True)]
    pre_rows = [(xs_k, 0, RW, True), (qw, 0, RW, True), (qa, 0, RW, True)]
    lw, k_h, a_rec, b_rec = _row_fwd("rwkv_pre", fn_pre, pre_params, pre_rows, [(RW, f32)] * 4, T, tm, ncol)
    wkv_in = (xs_r, lw, k_h, xs_v, a_rec, b_rec)
    c_w, c_u, c_rt, c_pb, c_y0, c_bend, c_z = _wkv_local_fwd(*wkv_in, head, T, RW)
    wkv_loc = (c_w, c_u, c_rt, c_pb, c_bend, lw, c_y0, c_z)
    y_wkv, states = _wkv_state_fwd(*wkv_loc, head, T, RW)
    fn_post = functools.partial(_fn_rwkv_post, head)
    post_params = [(row(lnx_g), True), (row(lnx_b), True), (r_k.reshape(1, RW), True)]
    post_rows = [(y_wkv, 0, RW, True), (xs_r, 0, RW, True), (k_h, 0, RW, True), (xs_v, 0, RW, True),
                 (proj, 3 * RW, RW, True)]
    (y_rwkv,) = _row_fwd("rwkv_post", fn_post, post_params, post_rows, [(RW, bf16)], T, tm, ncol)

    c_pre = _conv_fwd(proj, 4 * RW, 5 * RW, CW, cw_p, row(conv_b), ktaps, T, tm_halo)
    ln_params = [(row(cln_g), False), (row(cln_b), False)]
    (c_act,) = _row_fwd("conv_ln", _fn_conv_ln, ln_params, [(c_pre, 0, CW, False)], [(CW, bf16)], T, tm)
    c2 = _matmul("pw2", c_act, pw2_full, "nn", f32)
    cpost_params = [(row(b_pw2), True)]
    cpost_rows = [(c2, 0, CW, True), (proj, 6 * RW, CW, True)]
    (y_conv,) = _row_fwd("conv_post", _fn_conv_post, cpost_params, cpost_rows, [(CW, bf16)], T, tm, ncol)

    mix = jnp.concatenate([y_rwkv, y_conv], axis=1)
    out = _matmul("out_proj", mix, wout_full, "nn", f32)
    d_out, gx_res, loss_part, g_npost = _post(out, x2, tgt2, row(norm_post_g), T, D, tm_wide)

    g_wout_full = _matmul("d_w_out", mix, d_out, "tn", bf16)
    d_mix = _matmul("d_mix", d_out, wout_full, "nt", f32)

    d_c2, d_gconv, g_bpw2 = _row_bwd("conv_post_bwd", _fn_conv_post, cpost_params, cpost_rows,
                                      [(d_mix, RW, CW, True)], [bf16, bf16], T, tm, ncol)
    g_pw2_full = _matmul("d_w_pw2", c_act, d_c2, "tn", bf16)
    d_cact = _matmul("d_c_act", d_c2, pw2_full, "nt", f32)
    d_cpre, g_clng, g_clnb = _row_bwd("conv_ln_bwd", _fn_conv_ln, ln_params, [(c_pre, 0, CW, False)],
                                      [(d_cact, 0, CW, False)], [f32], T, tm)
    d_gluv, d_glug, g_cw_p, g_cb = _conv_bwd(proj, 4 * RW, 5 * RW, CW, cw_p, d_cpre, ktaps, T, tm_halo)

    d_y, dr_x, dk_x, dv_x, d_grwkv, g_lnxg, g_lnxb, g_rk = _row_bwd(
        "rwkv_post_bwd", fn_post, post_params, post_rows, [(d_mix, 0, RW, True)], [f32, f32, f32, f32, bf16], T, tm, ncol)
    d_cw, d_cu, d_crt, d_cpb, d_cbend, d_lw_dec, d_cz = _wkv_state_bwd(*wkv_loc, states, d_y, head, T, RW)
    d_xr, d_lw, d_kh, d_xv, d_a, d_b = _wkv_local_bwd(
        *wkv_in, (d_cw, d_cu, d_crt, d_cpb, d_y, d_cbend, d_cz), d_lw_dec, dr_x, dk_x, dv_x, head, T, RW)
    pre_cots = [(d_lw, 0, RW, True), (d_kh, 0, RW, True), (d_a, 0, RW, True), (d_b, 0, RW, True)]
    d_xk, d_qw, d_qa, g_kk, g_ka = _row_bwd("rwkv_pre_bwd", fn_pre, pre_params, pre_rows, pre_cots, [f32, f32, f32],
                                            T, tm, ncol)
    d_xl, g_w0, g_wup_p, g_a0, g_aup_p = _row_bwd("lora_up_bwd", _fn_lora, lora_params, [(xs_l, 0, LORA_PAD, False)],
                                                  [(d_qw, 0, RW, False), (d_qa, 0, RW, False)], [f32], T, tm)
    dp_r, g_mur = _shift_bwd("shift_r_bwd", proj, 0, RW, mu_r, d_xr, T, tm_halo)
    dp_k, g_muk = _shift_bwd("shift_k_bwd", proj, RW, RW, mu_k, d_xk, T, tm_halo)
    dp_v, g_muv = _shift_bwd("shift_v_bwd", proj, 2 * RW, RW, mu_v, d_xv, T, tm_halo)
    dp_l, g_mul = _shift_bwd("shift_l_bwd", proj, off_l, LORA_PAD, mu_l, d_xl, T, tm_halo)
    d_proj = _place_blocks([dp_r, dp_k, dp_v, d_grwkv, d_gluv, d_glug, d_gconv, dp_l], axis=1)

    def chip_sums(tag, shards):
        halves4 = [a.reshape(4, 2, a.shape[1] // 2, a.shape[2]) for a in shards]
        got = _pair_exchange("pair_exchange_" + tag, halves4)
        return [_add_kept("chip_sum_%s_%d" % (tag, i), a, g) for i, (a, g) in enumerate(zip(halves4, got))]

    def all_chips(tag, slots):
        return _share_halves("share_" + tag, [_sum_slots_half("sum_%s_%d" % (tag, i), r) for i, r in enumerate(slots)])

    q_early = chip_sums("early", [g_wout_full.reshape(4, (RW + CW) // 4, D), g_pw2_full.reshape(4, CW // 4, CW)])
    g_wp_t, r_early = _matmul("d_w_in", d_proj, h, "tn", bf16, comm=_exchange_chips(q_early, []))
    g_win_t = _place_blocks([g_wp_t[:lo], g_wp_t[off_l:off_l + 2 * lora], g_wp_t[lo:off_l]], axis=0)
    col_shards = lambda a: a.reshape(a.shape[0], 4, a.shape[1] // 4).transpose(1, 0, 2)
    q_late = chip_sums("late", [g_win_t.reshape(4, shard, D), col_shards(g_wup_p[:lora]),
                                col_shards(g_aup_p[lora:2 * lora]), col_shards(g_cw_p)])
    g_mu = jnp.concatenate([g_mur[0], g_muk[0], g_muv[0], g_mul[0, :2 * lora]])
    pad_rows = lambda a, n: jnp.concatenate([a, jnp.zeros((n - a.shape[0], a.shape[1]), f32)], axis=0)
    n_mu = -(-mu_shift.shape[0] // RW)
    small_vecs = [pad_rows(jnp.pad(g_mu, (0, n_mu * RW - g_mu.shape[0])).reshape(n_mu, RW), n_mu),
                  g_w0, g_a0, g_kk, g_ka, g_rk, g_lnxg, g_lnxb, g_cb, g_clng, g_clnb, g_bpw2,
                  g_npost.reshape(D // RW, RW)]
    n_small = sum(a.shape[0] for a in small_vecs)
    n_small_pad = -(-n_small // (2 * SUBLANES)) * (2 * SUBLANES)
    small = pad_rows(jnp.concatenate(small_vecs, axis=0), n_small_pad)

    d_h, r_late = _matmul("d_h", d_proj, wp_t, "nn", bf16, tk_t=2560, comm=_exchange_chips(q_late, [small]))
    grad_x2, g_npre = _rms_pre_bwd(x2, npg, d_h, gx_res, T, D, tm_wide)
    (r_npre,) = _run_comm("exchange_norm_pre", _exchange_chips([], [g_npre.reshape(D // RW, RW)]))
    s_npre = _sum_slots("sum_norm_pre", r_npre)
    s_small = _sum_slots("sum_small", r_late[4])
    grad_w_out, grad_w_pw2 = all_chips("early", r_early)
    grad_w_in, grad_wup, grad_aup, grad_cw = all_chips("late", r_late[:4])

    pos = [0]

    def take(nrows):
        a = s_small[pos[0]:pos[0] + nrows]
        pos[0] += nrows
        return a

    grads = {}
    grads["norm_pre_g"] = s_npre.reshape(D)
    grads["mu_shift"] = take(n_mu).reshape(-1)[:mu_shift.shape[0]]
    for nm in ["w0", "a0", "k_k", "k_a"]:
        grads[nm] = take(1).reshape(RW)
    grads["r_k"] = take(1).reshape(r_k.shape)
    for nm in ["lnx_g", "lnx_b", "conv_b", "cln_g", "cln_b", "b_pw2"]:
        grads[nm] = take(1).reshape(RW)
    grads["norm_post_g"] = take(D // RW).reshape(D)
    grads["w_lora_up"], grads["a_lora_up"], grads["conv_w"] = grad_wup, grad_aup, grad_cw[:ktaps]
    grads["w_in"], grads["w_out"], grads["w_pw2"] = grad_w_in, grad_w_out, grad_w_pw2

    weights = dict(norm_pre_g=norm_pre_g, w_in=w_in, mu_shift=mu_shift, w0=w0, w_lora_up=w_lora_up, a0=a0,
                   a_lora_up=a_lora_up, k_k=k_k, k_a=k_a, r_k=r_k, lnx_g=lnx_g, lnx_b=lnx_b, conv_w=conv_w,
                   conv_b=conv_b, cln_g=cln_g, cln_b=cln_b, w_pw2=w_pw2, b_pw2=b_pw2, w_out=w_out,
                   norm_post_g=norm_post_g)
    ms = dict(norm_pre_g=m_norm_pre_g, w_in=m_w_in, mu_shift=m_mu_shift, w0=m_w0, w_lora_up=m_w_lora_up, a0=m_a0,
              a_lora_up=m_a_lora_up, k_k=m_k_k, k_a=m_k_a, r_k=m_r_k, lnx_g=m_lnx_g, lnx_b=m_lnx_b, conv_w=m_conv_w,
              conv_b=m_conv_b, cln_g=m_cln_g, cln_b=m_cln_b, w_pw2=m_w_pw2, b_pw2=m_b_pw2, w_out=m_w_out,
              norm_post_g=m_norm_post_g)
    vs = dict(norm_pre_g=v_norm_pre_g, w_in=v_w_in, mu_shift=v_mu_shift, w0=v_w0, w_lora_up=v_w_lora_up, a0=v_a0,
              a_lora_up=v_a_lora_up, k_k=v_k_k, k_a=v_k_a, r_k=v_r_k, lnx_g=v_lnx_g, lnx_b=v_lnx_b, conv_w=v_conv_w,
              conv_b=v_conv_b, cln_g=v_cln_g, cln_b=v_cln_b, w_pw2=v_w_pw2, b_pw2=v_b_pw2, w_out=v_w_out,
              norm_post_g=v_norm_post_g)
    names = list(weights)
    big = ["w_in", "w_out", "w_pw2"]
    deltas, new_m, new_v = {}, {}, {}
    d_t, m_t, v_t = _adamw("adamw_w_in", w_in_t, grad_w_in, m_w_in_t, v_w_in_t)
    grads["w_in"], deltas["w_in"], new_m["w_in"], new_v["w_in"] = grad_w_in.T, d_t.T, m_t.T, v_t.T
    for nm in big[1:]:
        deltas[nm], new_m[nm], new_v[nm] = _adamw("adamw_" + nm, weights[nm], grads[nm], ms[nm], vs[nm])
    rest = [nm for nm in names if nm not in big]
    sizes = [weights[nm].size for nm in rest]
    total = sum(sizes)
    width = 4 * LANES
    rows_p = -(-total // (width * SUBLANES)) * SUBLANES

    def pack(d):
        flat = jnp.concatenate([d[nm].reshape(-1) for nm in rest])
        return jnp.pad(flat, (0, rows_p * width - total)).reshape(rows_p, width)

    p_d, p_m, p_v = _adamw("adamw_small", pack(weights), pack(grads), pack(ms), pack(vs))
    o = 0
    for nm, sz in zip(rest, sizes):
        shp = weights[nm].shape
        deltas[nm] = p_d.reshape(-1)[o:o + sz].reshape(shp)
        new_m[nm] = p_m.reshape(-1)[o:o + sz].reshape(shp)
        new_v[nm] = p_v.reshape(-1)[o:o + sz].reshape(shp)
        o += sz

    loss = lax.psum(loss_part[0, 0], ("x", "y", "c"))
    grad_x = grad_x2[None]
    return (loss, grad_x, *[grads[nm] for nm in names], *[deltas[nm] for nm in names],
            *[new_m[nm] for nm in names], *[new_v[nm] for nm in names])
```

```python
import functools

import jax
import jax.numpy as jnp
from jax import lax
from jax.experimental import pallas as pl
from jax.experimental.pallas import tpu as pltpu

f32 = jnp.float32
bf16 = jnp.bfloat16
MESH = pl.DeviceIdType.MESH

NORM_EPS = 1e-6
LN_EPS = 1e-5
ADAM_LR, ADAM_B1, ADAM_B2, ADAM_EPS, ADAM_WD, ADAM_STEP = 0.001, 0.9, 0.999, 1e-08, 0.01, 10

LANES = 128
SUBLANES = 8
LORA_PAD = 256
CONV_HALO = 32
WKV_CHUNK = 64
WKV_HEADS = 16
WKV_STATE_HEADS = 32
WKV_PREC = lax.Precision.HIGH
SHARE_CHUNKS = 8
VMEM_LIMIT = 56 * 1024 * 1024


def _cparams(n_axes):
    return pltpu.CompilerParams(dimension_semantics=("arbitrary",) * n_axes, vmem_limit_bytes=VMEM_LIMIT)


def _tile(dim, target):
    best = None
    t = LANES
    while t <= min(dim, target):
        if dim % t == 0:
            best = t
        t += LANES
    return best if best is not None else dim


def _mm(a, b, prec=None):
    return lax.dot_general(a, b, (((1,), (0,)), ((), ())), precision=prec, preferred_element_type=f32)


def _mm_nt(a, b, prec=None):
    return lax.dot_general(a, b, (((1,), (1,)), ((), ())), precision=prec, preferred_element_type=f32)


def _mm_tn(a, b, prec=None):
    return lax.dot_general(a, b, (((0,), (0,)), ((), ())), precision=prec, preferred_element_type=f32)


@jax.custom_vjp
def _bmm(a, b):
    return _mm(a.astype(bf16), b.astype(bf16))


def _bmm_fwd(a, b):
    return _bmm(a, b), (a, b)


def _bmm_bwd(res, dc):
    a, b = res
    dcb = dc.astype(bf16)
    return _mm_nt(dcb, b.astype(bf16)), _mm_tn(a.astype(bf16), dcb)


_bmm.defvjp(_bmm_fwd, _bmm_bwd)


@jax.custom_vjp
def _bmm_nt(a, b):
    return _mm_nt(a.astype(bf16), b.astype(bf16))


def _bmm_nt_fwd(a, b):
    return _bmm_nt(a, b), (a, b)


def _bmm_nt_bwd(res, dc):
    a, b = res
    dcb = dc.astype(bf16)
    return _mm(dcb, b.astype(bf16)), _mm_tn(dcb, a.astype(bf16))


_bmm_nt.defvjp(_bmm_nt_fwd, _bmm_nt_bwd)


@jax.custom_vjp
def _bmm_tn(a, b):
    return _mm_tn(a.astype(bf16), b.astype(bf16))


def _bmm_tn_fwd(a, b):
    return _bmm_tn(a, b), (a, b)


def _bmm_tn_bwd(res, dc):
    a, b = res
    dcb = dc.astype(bf16)
    return _mm_nt(b.astype(bf16), dcb), _mm(a.astype(bf16), dcb)


_bmm_tn.defvjp(_bmm_tn_fwd, _bmm_tn_bwd)


def _matmul(name, a, b, mode, out_dtype, tm_t=1024, tn_t=1024, tk_t=4096, comm=None):
    if mode == "nn":
        (M, K), (_, N) = a.shape, b.shape
    elif mode == "nt":
        (M, K), (N, _) = a.shape, b.shape
    else:
        (K, M), (_, N) = a.shape, b.shape
    tm, tn, tk = _tile(M, tm_t), _tile(N, tn_t), _tile(K, tk_t)
    ni, nj, nk = M // tm, N // tn, K // tk
    dot = {"nn": _mm, "nt": _mm_nt, "tn": _mm_tn}[mode]
    nc = len(comm.operands) if comm else 0

    def body(*refs):
        a_ref, b_ref = refs[:2]
        o_ref = refs[2 + nc]
        scratch = refs[3 + 2 * nc:]
        i, j, k = pl.program_id(0), pl.program_id(1), pl.program_id(2)
        if comm:
            comm_refs = (refs[2:2 + nc], refs[3 + nc:3 + 2 * nc], scratch[:len(comm.scratch)])

            @pl.when(jnp.logical_and(jnp.logical_and(i == 0, j == 0), k == 0))
            def _():
                comm.start(*comm_refs)

        if nk == 1:
            o_ref[...] = dot(a_ref[...], b_ref[...]).astype(o_ref.dtype)
        else:
            acc_ref = scratch[-1]

            @pl.when(k == 0)
            def _():
                acc_ref[...] = jnp.zeros_like(acc_ref)

            acc_ref[...] += dot(a_ref[...], b_ref[...])

            @pl.when(k == nk - 1)
            def _():
                o_ref[...] = acc_ref[...].astype(o_ref.dtype)

        if comm:
            @pl.when(jnp.logical_and(jnp.logical_and(i == ni - 1, j == nj - 1), k == nk - 1))
            def _():
                comm.wait(*comm_refs)

    a_spec = {"nn": pl.BlockSpec((tm, tk), lambda i, j, k: (i, k)),
              "nt": pl.BlockSpec((tm, tk), lambda i, j, k: (i, k)),
              "tn": pl.BlockSpec((tk, tm), lambda i, j, k: (k, i))}[mode]
    b_spec = {"nn": pl.BlockSpec((tk, tn), lambda i, j, k: (k, j)),
              "nt": pl.BlockSpec((tn, tk), lambda i, j, k: (j, k)),
              "tn": pl.BlockSpec((tk, tn), lambda i, j, k: (k, j))}[mode]
    res = pl.pallas_call(
        body, name=name, grid=(ni, nj, nk),
        in_specs=[a_spec, b_spec] + [ANY] * nc,
        out_specs=[pl.BlockSpec((tm, tn), lambda i, j, k: (i, j))] + [ANY] * nc,
        out_shape=[jax.ShapeDtypeStruct((M, N), out_dtype)] + (list(comm.out_shape) if comm else []),
        scratch_shapes=(list(comm.scratch) if comm else []) + ([pltpu.VMEM((tm, tn), f32)] if nk > 1 else []),
        compiler_params=_cparams(3),
    )(a, b, *(comm.operands if comm else []))
    return (res[0], res[1:]) if comm else res[0]


def _row_spec(op, tm, ncol):
    arr, off, width, tiled = op
    if tiled:
        cw = width // ncol
        return pl.BlockSpec((tm, cw), lambda j, i: (i, off // cw + j))
    return pl.BlockSpec((tm, width), lambda j, i: (i, off // width))


def _param_spec(p, ncol):
    arr, tiled = p
    rows, width = arr.shape
    if tiled:
        return pl.BlockSpec((rows, width // ncol), lambda j, i: (0, j))
    return pl.BlockSpec((rows, width), lambda j, i: (0, 0))


def _row_fwd(name, fn, params, rows, outs, T, tm, ncol=1):
    npar, nrow = len(params), len(rows)

    def body(*refs):
        pv = [r[...] for r in refs[:npar]]
        rv = [r[...].astype(f32) for r in refs[npar:npar + nrow]]
        res = fn(*pv, *rv)
        for o_ref, val in zip(refs[npar + nrow:], res):
            o_ref[...] = val.astype(o_ref.dtype)

    return pl.pallas_call(
        body, name=name, grid=(ncol, T // tm),
        in_specs=[_param_spec(p, ncol) for p in params] + [_row_spec(r, tm, ncol) for r in rows],
        out_specs=[pl.BlockSpec((tm, w // ncol), lambda j, i: (i, j)) for w, _ in outs],
        out_shape=[jax.ShapeDtypeStruct((T, w), dt) for w, dt in outs],
        compiler_params=_cparams(2),
    )(*[p[0] for p in params], *[r[0] for r in rows])


def _row_bwd(name, fn, params, rows, cots, row_grads, T, tm, ncol=1):
    npar, nrow, ncot = len(params), len(rows), len(cots)
    want = [k for k, dt in enumerate(row_grads) if dt is not None]

    def body(*refs):
        pv = [r[...] for r in refs[:npar]]
        rv = [r[...].astype(f32) for r in refs[npar:npar + nrow]]
        cv = tuple(r[...].astype(f32) for r in refs[npar + nrow:npar + nrow + ncot])
        out_refs = refs[npar + nrow + ncot:]
        _, vjp = jax.vjp(fn, *pv, *rv)
        grads = vjp(cv)
        for o_ref, k in zip(out_refs[:len(want)], want):
            o_ref[...] = grads[npar + k].astype(o_ref.dtype)
        j, i = pl.program_id(0), pl.program_id(1)
        for o_ref, p, g in zip(out_refs[len(want):], params, grads[:npar]):
            first = (i == 0) if p[1] else jnp.logical_and(i == 0, j == 0)

            @pl.when(first)
            def _():
                o_ref[...] = jnp.zeros_like(o_ref)

            o_ref[...] += g

    def grad_spec(op):
        arr, off, width, tiled = op
        if tiled:
            return pl.BlockSpec((tm, width // ncol), lambda j, i: (i, j)), (T, width)
        return pl.BlockSpec((tm, width), lambda j, i: (i, j)), (T, width * ncol)

    gspecs = [grad_spec(rows[k]) for k in want]
    return pl.pallas_call(
        body, name=name, grid=(ncol, T // tm),
        in_specs=[_param_spec(p, ncol) for p in params] + [_row_spec(r, tm, ncol) for r in rows]
        + [_row_spec(c, tm, ncol) for c in cots],
        out_specs=[s for s, _ in gspecs] + [_param_spec(p, ncol) for p in params],
        out_shape=[jax.ShapeDtypeStruct(shp, row_grads[k]) for (_, shp), k in zip(gspecs, want)]
        + [jax.ShapeDtypeStruct(p[0].shape, f32) for p in params],
        compiler_params=_cparams(2),
    )(*[p[0] for p in params], *[r[0] for r in rows], *[c[0] for c in cots])


def _seg_sum(x, head):
    li = lax.broadcasted_iota(jnp.int32, (LANES, LANES), 0) // head
    lj = lax.broadcasted_iota(jnp.int32, (LANES, LANES), 1) // head
    q = (li == lj).astype(f32)
    parts = [_mm(x[:, s:s + LANES], q, lax.Precision.HIGH) for s in range(0, x.shape[1], LANES)]
    return parts[0] if len(parts) == 1 else jnp.concatenate(parts, axis=1)


def _sigmoid(z):
    return 1.0 / (1.0 + jnp.exp(-z))


def _silu(z):
    return z * _sigmoid(z)


def _rms(g, x):
    return x * lax.rsqrt(jnp.mean(x * x, axis=-1, keepdims=True) + NORM_EPS) * g


def _fn_rms_pre(g, x):
    return (_rms(g, x),)


def _fn_lora(w0, wup, a0, aup, xl):
    qw = w0 + _bmm(jnp.tanh(xl), wup)
    qa = a0 + _bmm(xl, aup)
    return qw, qa


def _fn_rwkv_pre(head, k_k, k_a, xk, qw, qa):
    w_log = -(jnp.maximum(-qw, 0.0) + jnp.log(1.0 + jnp.exp(-jnp.abs(qw)))) - 0.5
    lw = -jnp.exp(w_log)
    a_sig = _sigmoid(qa)
    kk = xk * k_k
    kk = kk / jnp.maximum(jnp.sqrt(_seg_sum(kk * kk, head)), 1e-12)
    k_h = xk * (1.0 + (a_sig - 1.0) * k_a)
    return lw, k_h, -kk, kk * a_sig


def _fn_rwkv_post(head, lnx_g, lnx_b, r_k, y, r, k_h, v, g):
    inv = 1.0 / head
    mu = _seg_sum(y, head) * inv
    d = y - mu
    var = _seg_sum(d * d, head) * inv
    yn = d * lax.rsqrt(var + 1e-5 * head) * lnx_g + lnx_b
    bonus = _seg_sum(r * k_h * r_k, head) * v
    return ((yn + bonus) * _silu(g),)


def _fn_conv_ln(cln_g, cln_b, c):
    mu = jnp.mean(c, axis=-1, keepdims=True)
    d = c - mu
    var = jnp.mean(d * d, axis=-1, keepdims=True)
    return (_silu(d * lax.rsqrt(var + LN_EPS) * cln_g + cln_b),)


def _fn_conv_post(b_pw2, c2, g):
    return ((c2 + b_pw2) * _silu(g),)


def _post(out, x, tgt, g, T, D, tm):
    def body(g_ref, o_ref, x_ref, t_ref, dout_ref, gx_ref, loss_ref, dg_ref):
        i = pl.program_id(0)
        o, vjp = jax.vjp(_rms, g_ref[...], o_ref[...])
        err = x_ref[...] + o - t_ref[...]
        d_y = err * (1.0 / D)
        dg, d_out = vjp(d_y)
        dout_ref[...] = d_out.astype(dout_ref.dtype)
        gx_ref[...] = d_y

        @pl.when(i == 0)
        def _():
            loss_ref[...] = jnp.zeros_like(loss_ref)
            dg_ref[...] = jnp.zeros_like(dg_ref)

        loss_ref[...] += jnp.sum(err * err, keepdims=True) * (0.5 / D)
        dg_ref[...] += dg

    row = pl.BlockSpec((tm, D), lambda i: (i, 0))
    vec = pl.BlockSpec((1, D), lambda i: (0, 0))
    return pl.pallas_call(
        body, name="post_loss", grid=(T // tm,),
        in_specs=[vec, row, row, row],
        out_specs=[row, row, pl.BlockSpec((1, 1), lambda i: (0, 0)), vec],
        out_shape=[jax.ShapeDtypeStruct((T, D), bf16), jax.ShapeDtypeStruct((T, D), f32),
                   jax.ShapeDtypeStruct((1, 1), f32), jax.ShapeDtypeStruct((1, D), f32)],
        compiler_params=_cparams(1),
    )(g, out, x, tgt)


def _rms_pre_bwd(x, g, dh, gx_res, T, D, tm):
    def body(g_ref, x_ref, dh_ref, res_ref, dx_ref, dg_ref):
        i = pl.program_id(0)
        _, vjp = jax.vjp(_rms, g_ref[...], x_ref[...])
        dg, dx = vjp(dh_ref[...].astype(f32))
        dx_ref[...] = dx + res_ref[...]

        @pl.when(i == 0)
        def _():
            dg_ref[...] = jnp.zeros_like(dg_ref)

        dg_ref[...] += dg

    row = pl.BlockSpec((tm, D), lambda i: (i, 0))
    vec = pl.BlockSpec((1, D), lambda i: (0, 0))
    return pl.pallas_call(
        body, name="rms_pre_bwd", grid=(T // tm,),
        in_specs=[vec, row, row, row], out_specs=[row, vec],
        out_shape=[jax.ShapeDtypeStruct((T, D), f32), jax.ShapeDtypeStruct((1, D), f32)],
        compiler_params=_cparams(1),
    )(g, x, dh, gx_res)


def _prev_rows(cur, halo_ref, first):
    top = jnp.where(first, 0.0, halo_ref[SUBLANES - 1:SUBLANES, :])
    rolled = pltpu.roll(cur, 1, 0)
    rid = lax.broadcasted_iota(jnp.int32, cur.shape, 0)
    return jnp.where(rid == 0, top, rolled)


def _shift_fwd(name, proj, off, width, mu, T, tm):
    cw = _tile(width, 512)
    ncol, cb = width // cw, off // cw
    hb = tm // SUBLANES

    def body(mu_ref, cur_ref, halo_ref, o_ref):
        i = pl.program_id(1)
        cur = cur_ref[...]
        prev = _prev_rows(cur, halo_ref, i == 0)
        o_ref[...] = cur + (prev - cur) * mu_ref[...]

    return pl.pallas_call(
        body, name=name, grid=(ncol, T // tm),
        in_specs=[pl.BlockSpec((1, cw), lambda j, i: (0, j)),
                  pl.BlockSpec((tm, cw), lambda j, i: (i, cb + j)),
                  pl.BlockSpec((SUBLANES, cw), lambda j, i: (jnp.maximum(i * hb - 1, 0), cb + j))],
        out_specs=pl.BlockSpec((tm, cw), lambda j, i: (i, j)),
        out_shape=jax.ShapeDtypeStruct((T, width), f32),
        compiler_params=_cparams(2),
    )(mu, proj, proj)


def _shift_bwd(name, proj, off, width, mu, dxs, T, tm):
    cw = _tile(width, 512)
    ncol, cb = width // cw, off // cw
    hb, nt = tm // SUBLANES, T // tm

    def body(mu_ref, cur_ref, halo_ref, d_ref, dnext_ref, o_ref, dmu_ref):
        i = pl.program_id(1)
        cur = cur_ref[...]
        prev = _prev_rows(cur, halo_ref, i == 0)
        d = d_ref[...]
        bottom = jnp.where(i == nt - 1, 0.0, dnext_ref[0:1, :])
        rid = lax.broadcasted_iota(jnp.int32, d.shape, 0)
        d_next = jnp.where(rid == tm - 1, bottom, pltpu.roll(d, tm - 1, 0))
        mu_v = mu_ref[...]
        o_ref[...] = (d * (1.0 - mu_v) + d_next * mu_v).astype(o_ref.dtype)

        @pl.when(i == 0)
        def _():
            dmu_ref[...] = jnp.zeros_like(dmu_ref)

        dmu_ref[...] += jnp.sum(d * (prev - cur), axis=0, keepdims=True)

    return pl.pallas_call(
        body, name=name, grid=(ncol, nt),
        in_specs=[pl.BlockSpec((1, cw), lambda j, i: (0, j)),
                  pl.BlockSpec((tm, cw), lambda j, i: (i, cb + j)),
                  pl.BlockSpec((SUBLANES, cw), lambda j, i: (jnp.maximum(i * hb - 1, 0), cb + j)),
                  pl.BlockSpec((tm, cw), lambda j, i: (i, j)),
                  pl.BlockSpec((SUBLANES, cw), lambda j, i: (jnp.minimum((i + 1) * hb, nt * hb - 1), j))],
        out_specs=[pl.BlockSpec((tm, cw), lambda j, i: (i, j)), pl.BlockSpec((1, cw), lambda j, i: (0, j))],
        out_shape=[jax.ShapeDtypeStruct((T, width), bf16), jax.ShapeDtypeStruct((1, width), f32)],
        compiler_params=_cparams(2),
    )(mu, proj, proj, dxs, dxs)


def _rolled_copies(dst_ref, ext):
    n = ext.shape[0]
    dst_ref[0] = ext
    for r in range(1, SUBLANES):
        dst_ref[r] = pltpu.roll(ext, n - r, 0)


def _window(rolled_ref, start, rows):
    q, r = divmod(start, SUBLANES)
    return rolled_ref[r, pl.ds(SUBLANES * q, rows), :]


def _conv_fwd(proj, off_v, off_g, width, conv_w, conv_b, ktaps, T, tm):
    cw = _tile(width, 512)
    ncol = width // cw
    hb = tm // CONV_HALO
    lead = CONV_HALO - (ktaps - 1)

    def body(w_ref, b_ref, v_ref, g_ref, hv_ref, hg_ref, o_ref, u_ref):
        i = pl.program_id(1)
        halo = hv_ref[...] * _sigmoid(hg_ref[...])
        _rolled_copies(u_ref, jnp.concatenate([jnp.where(i == 0, 0.0, halo), v_ref[...] * _sigmoid(g_ref[...])], axis=0))
        acc = jnp.zeros((tm, cw), f32) + b_ref[...]
        for j in range(ktaps):
            acc = acc + _window(u_ref, lead + j, tm) * w_ref[j:j + 1, :]
        o_ref[...] = acc

    def tile(off):
        return pl.BlockSpec((tm, cw), lambda j, i: (i, off // cw + j))

    def halo(off):
        return pl.BlockSpec((CONV_HALO, cw), lambda j, i: (jnp.maximum(i * hb - 1, 0), off // cw + j))

    return pl.pallas_call(
        body, name="conv_fwd", grid=(ncol, T // tm),
        in_specs=[pl.BlockSpec((CONV_HALO, cw), lambda j, i: (0, j)), pl.BlockSpec((1, cw), lambda j, i: (0, j)),
                  tile(off_v), tile(off_g), halo(off_v), halo(off_g)],
        out_specs=pl.BlockSpec((tm, cw), lambda j, i: (i, j)),
        out_shape=jax.ShapeDtypeStruct((T, width), f32),
        scratch_shapes=[pltpu.VMEM((SUBLANES, CONV_HALO + tm, cw), f32)],
        compiler_params=_cparams(2),
    )(conv_w, conv_b, proj, proj, proj, proj)


def _conv_bwd(proj, off_v, off_g, width, conv_w, dc, ktaps, T, tm):
    cw = _tile(width, 512)
    ncol = width // cw
    hb, nt = tm // CONV_HALO, T // tm
    lead = CONV_HALO - (ktaps - 1)

    def body(w_ref, v_ref, g_ref, hv_ref, hg_ref, dc_ref, dcn_ref, dv_ref, dg_ref, dw_ref, db_ref, u_ref, d_ref):
        i = pl.program_id(1)
        halo = hv_ref[...] * _sigmoid(hg_ref[...])
        sig = _sigmoid(g_ref[...])
        gv = v_ref[...]
        _rolled_copies(u_ref, jnp.concatenate([jnp.where(i == 0, 0.0, halo), gv * sig], axis=0))
        dcur = dc_ref[...]
        _rolled_copies(d_ref, jnp.concatenate([dcur, jnp.where(i == nt - 1, 0.0, dcn_ref[...])], axis=0))

        @pl.when(i == 0)
        def _():
            dw_ref[...] = jnp.zeros_like(dw_ref)
            db_ref[...] = jnp.zeros_like(db_ref)

        du = jnp.zeros((tm, cw), f32)
        for j in range(ktaps):
            du = du + _window(d_ref, ktaps - 1 - j, tm) * w_ref[j:j + 1, :]
            dw_ref[j:j + 1, :] += jnp.sum(_window(u_ref, lead + j, tm) * dcur, axis=0, keepdims=True)
        db_ref[...] += jnp.sum(dcur, axis=0, keepdims=True)
        dv_ref[...] = (du * sig).astype(dv_ref.dtype)
        dg_ref[...] = (du * gv * sig * (1.0 - sig)).astype(dg_ref.dtype)

    def tile(off):
        return pl.BlockSpec((tm, cw), lambda j, i: (i, off // cw + j))

    def halo(off):
        return pl.BlockSpec((CONV_HALO, cw), lambda j, i: (jnp.maximum(i * hb - 1, 0), off // cw + j))

    return pl.pallas_call(
        body, name="conv_bwd", grid=(ncol, nt),
        in_specs=[pl.BlockSpec((CONV_HALO, cw), lambda j, i: (0, j)),
                  tile(off_v), tile(off_g), halo(off_v), halo(off_g),
                  pl.BlockSpec((tm, cw), lambda j, i: (i, j)),
                  pl.BlockSpec((CONV_HALO, cw), lambda j, i: (jnp.minimum((i + 1) * hb, nt * hb - 1), j))],
        out_specs=[pl.BlockSpec((tm, cw), lambda j, i: (i, j)), pl.BlockSpec((tm, cw), lambda j, i: (i, j)),
                   pl.BlockSpec((CONV_HALO, cw), lambda j, i: (0, j)), pl.BlockSpec((1, cw), lambda j, i: (0, j))],
        out_shape=[jax.ShapeDtypeStruct((T, width), bf16), jax.ShapeDtypeStruct((T, width), bf16),
                   jax.ShapeDtypeStruct((CONV_HALO, width), f32), jax.ShapeDtypeStruct((1, width), f32)],
        scratch_shapes=[pltpu.VMEM((SUBLANES, CONV_HALO + tm, cw), f32), pltpu.VMEM((SUBLANES, tm + CONV_HALO, cw), f32)],
        compiler_params=_cparams(2),
    )(conv_w, proj, proj, proj, proj, dc, dc)


def _each(f, *lists):
    return [f(*xs) for xs in zip(*lists)]


def _wkv_local(r, lw, k, v, a, b):
    C = r[0].shape[0]
    P = WKV_PREC
    row = lax.broadcasted_iota(jnp.int32, (C, C), 0)
    col = lax.broadcasted_iota(jnp.int32, (C, C), 1)
    incl, strict = row >= col, row > col
    tri = incl.astype(f32)
    zero = jnp.zeros((C, C), f32)
    G = _each(lambda x: _mm(tri, x, P), lw)
    to_end = _each(lambda x, g: jnp.exp(jnp.sum(x, axis=0, keepdims=True) - g), lw, G)
    e_g = _each(jnp.exp, G)
    e_ng = _each(lambda g: jnp.exp(-g), G)
    At = _each(lambda x, g, w: x * jnp.exp(g - w), a, G, lw)
    Rt = _each(jnp.multiply, r, e_g)
    Kt = _each(jnp.multiply, k, e_ng)
    Bt = _each(jnp.multiply, b, e_ng)
    sc = _each(lambda at, rt, bt, kt: _mm_nt(jnp.concatenate([at, rt], axis=0), jnp.concatenate([bt, kt], axis=0), P),
               At, Rt, Bt, Kt)
    L = _each(lambda s: jnp.where(strict, s[:C, :C], zero), sc)
    M = _each(lambda s: jnp.where(strict, s[:C, C:], zero), sc)
    Pb = _each(lambda s: jnp.where(incl, s[C:, :C], zero), sc)
    Pk = _each(lambda s: jnp.where(incl, s[C:, C:], zero), sc)
    MPk = _each(lambda m, p, x: _bmm(jnp.concatenate([m, p], axis=0), x), M, Pk, v)
    WU = _each(lambda at, mp: jnp.concatenate([at, mp[:C]], axis=1), At, MPk)
    Lp = L
    n = 1
    while n < C:
        n *= 2
        if n < C:
            step = _each(lambda l, x: _bmm(l, jnp.concatenate([x, l], axis=1)), Lp, WU)
            WU = _each(lambda x, s: x + s[:, :x.shape[1]], WU, step)
            Lp = _each(lambda x, s: s[:, x.shape[1]:], WU, step)
        else:
            WU = _each(lambda x, l: x + _bmm(l, x), WU, Lp)
    N = r[0].shape[1]
    W = _each(lambda x: x[:, :N], WU)
    U = _each(lambda x: x[:, N:], WU)
    Y0 = _each(lambda mp: mp[C:], MPk)
    Bend = _each(jnp.multiply, b, to_end)
    Z = _each(lambda x, y, e: _bmm_tn(x, y * e), v, k, to_end)
    return W, U, Rt, Pb, Y0, Bend, Z


def _wkv_state(S0, W, U, Rt, Pb, Bend, lw, Y0, Z):
    C = W[0].shape[0]
    WR = _each(lambda w, rt, s: _bmm_nt(jnp.concatenate([w, rt], axis=0), s), W, Rt, S0)
    X = _each(lambda wr, u: wr[:C] + u, WR, U)
    y = _each(lambda p, x, wr, c: _bmm(p, x) + wr[C:] + c, Pb, X, WR, Y0)
    S1 = _each(lambda s, w, x, e, z: s * jnp.exp(jnp.sum(w, axis=0, keepdims=True)) + _bmm_tn(x, e) + z,
               S0, lw, X, Bend, Z)
    return y, S1


def _wkv_dims(head, T, RW, heads_per_step):
    C = min(WKV_CHUNK, T)
    nh = RW // head
    hb = min(heads_per_step, nh)
    return C, nh, hb, hb * head, T // C


def _heads(ref, hb, head):
    return [ref[:, h * head:(h + 1) * head] for h in range(hb)]


def _put_heads(ref, vals, head):
    for h, val in enumerate(vals):
        ref[:, h * head:(h + 1) * head] = val


def _wkv_local_fwd(r, lw, k, v, a, b, head, T, RW):
    C, nh, hb, bw, nc = _wkv_dims(head, T, RW, WKV_HEADS)

    def body(*refs):
        ins, outs = refs[:6], refs[6:]
        res = _wkv_local(*[_heads(x, hb, head) for x in ins])
        for o_ref, vals in zip(outs[:6], res[:6]):
            _put_heads(o_ref, vals, head)
        for h in range(hb):
            outs[6][0, h] = res[6][h]

    blk = pl.BlockSpec((C, bw), lambda g, c: (c, g))
    sq = pl.BlockSpec((1, hb, head, head), lambda g, c: (c, g, 0, 0))
    return pl.pallas_call(
        body, name="wkv_local", grid=(nh // hb, nc),
        in_specs=[blk] * 6, out_specs=[blk] * 6 + [sq],
        out_shape=[jax.ShapeDtypeStruct((T, RW), f32)] * 6 + [jax.ShapeDtypeStruct((nc, nh, head, head), f32)],
        compiler_params=_cparams(2),
    )(r, lw, k, v, a, b)


def _wkv_state_fwd(W, U, Rt, Pb, Bend, lw, Y0, Z, head, T, RW):
    C, nh, hb, bw, nc = _wkv_dims(head, T, RW, WKV_STATE_HEADS)

    def body(w_ref, u_ref, rt_ref, pb_ref, be_ref, lw_ref, y0_ref, z_ref, y_ref, st_ref, s_ref):
        @pl.when(pl.program_id(1) == 0)
        def _():
            s_ref[...] = jnp.zeros_like(s_ref)

        S0 = [s_ref[h] for h in range(hb)]
        for h in range(hb):
            st_ref[0, h] = S0[h]
        rows = [_heads(x, hb, head) for x in (w_ref, u_ref, rt_ref, pb_ref, be_ref, lw_ref, y0_ref)]
        y, S1 = _wkv_state(S0, *rows, [z_ref[0, h] for h in range(hb)])
        _put_heads(y_ref, y, head)
        for h in range(hb):
            s_ref[h] = S1[h]

    blk = pl.BlockSpec((C, bw), lambda g, c: (c, g))
    sq = pl.BlockSpec((1, hb, head, head), lambda g, c: (c, g, 0, 0))
    return pl.pallas_call(
        body, name="wkv_state", grid=(nh // hb, nc),
        in_specs=[blk] * 7 + [sq], out_specs=[blk, sq],
        out_shape=[jax.ShapeDtypeStruct((T, RW), f32), jax.ShapeDtypeStruct((nc, nh, head, head), f32)],
        scratch_shapes=[pltpu.VMEM((hb, head, head), f32)],
        compiler_params=_cparams(2),
    )(W, U, Rt, Pb, Bend, lw, Y0, Z)


def _wkv_state_bwd(W, U, Rt, Pb, Bend, lw, Y0, Z, states, dy, head, T, RW):
    C, nh, hb, bw, nc = _wkv_dims(head, T, RW, WKV_STATE_HEADS)

    def body(w_ref, u_ref, rt_ref, pb_ref, be_ref, lw_ref, y0_ref, z_ref, st_ref, dy_ref,
             dw_ref, du_ref, drt_ref, dpb_ref, dbe_ref, dlw_ref, dz_ref, ds_ref):
        @pl.when(pl.program_id(1) == 0)
        def _():
            ds_ref[...] = jnp.zeros_like(ds_ref)

        dS1 = [ds_ref[h] for h in range(hb)]
        for h in range(hb):
            dz_ref[0, h] = dS1[h]
        rows = [_heads(x, hb, head) for x in (w_ref, u_ref, rt_ref, pb_ref, be_ref, lw_ref)]
        Y0 = _heads(y0_ref, hb, head)
        Zs = [z_ref[0, h] for h in range(hb)]
        _, vjp = jax.vjp(lambda s0, *rw: _wkv_state(s0, *rw, Y0, Zs), [st_ref[0, h] for h in range(hb)], *rows)
        grads = vjp((_heads(dy_ref, hb, head), dS1))
        for o_ref, vals in zip((dw_ref, du_ref, drt_ref, dpb_ref, dbe_ref, dlw_ref), grads[1:]):
            _put_heads(o_ref, vals, head)
        for h in range(hb):
            ds_ref[h] = grads[0][h]

    blk = pl.BlockSpec((C, bw), lambda g, c: (nc - 1 - c, g))
    sq = pl.BlockSpec((1, hb, head, head), lambda g, c: (nc - 1 - c, g, 0, 0))
    return pl.pallas_call(
        body, name="wkv_state_bwd", grid=(nh // hb, nc),
        in_specs=[blk] * 7 + [sq, sq, blk], out_specs=[blk] * 6 + [sq],
        out_shape=[jax.ShapeDtypeStruct((T, RW), f32)] * 6 + [jax.ShapeDtypeStruct((nc, nh, head, head), f32)],
        scratch_shapes=[pltpu.VMEM((hb, head, head), f32)],
        compiler_params=_cparams(2),
    )(W, U, Rt, Pb, Bend, lw, Y0, Z, states, dy)


def _wkv_local_bwd(r, lw, k, v, a, b, cots, d_lw_x, dr_x, dk_x, dv_x, head, T, RW):
    C, nh, hb, bw, nc = _wkv_dims(head, T, RW, WKV_HEADS)

    def body(*refs):
        ins, cot_refs, add_refs, outs = refs[:6], refs[6:13], refs[13:17], refs[17:]
        _, vjp = jax.vjp(_wkv_local, *[_heads(x, hb, head) for x in ins])
        cts = [_heads(x, hb, head) for x in cot_refs[:6]] + [[cot_refs[6][0, h] for h in range(hb)]]
        dr, dlw, dk, dv, da, db = vjp(tuple(cts))
        dlw_x, drx, dkx, dvx = [_heads(x, hb, head) for x in add_refs]
        _put_heads(outs[0], _each(jnp.add, dr, drx), head)
        _put_heads(outs[1], _each(jnp.add, dlw, dlw_x), head)
        _put_heads(outs[2], _each(jnp.add, dk, dkx), head)
        _put_heads(outs[3], _each(jnp.add, dv, dvx), head)
        _put_heads(outs[4], da, head)
        _put_heads(outs[5], db, head)

    blk = pl.BlockSpec((C, bw), lambda g, c: (c, g))
    sq = pl.BlockSpec((1, hb, head, head), lambda g, c: (c, g, 0, 0))
    return pl.pallas_call(
        body, name="wkv_local_bwd", grid=(nh // hb, nc),
        in_specs=[blk] * 12 + [sq] + [blk] * 4, out_specs=[blk] * 6,
        out_shape=[jax.ShapeDtypeStruct((T, RW), f32)] * 6,
        compiler_params=_cparams(2),
    )(r, lw, k, v, a, b, *cots, d_lw_x, dr_x, dk_x, dv_x)


def _rows_tile(R, row_bytes, budget, mult=SUBLANES):
    best = None
    t = mult
    while t <= R:
        if R % t == 0 and t * row_bytes <= budget:
            best = t
        t += mult
    return best if best is not None else R


def _sum_slots(name, parts):
    S, R, W = parts.shape
    budget = 4 << 20
    tr = _rows_tile(R, S * W * 4, budget, 2 * SUBLANES)
    cw = W if tr * S * W * 4 <= 2 * budget else _tile(W, max(LANES, 2 * budget // (S * tr * 4)))

    def body(p_ref, o_ref):
        acc = p_ref[0].astype(f32)
        for d in range(1, S):
            acc = acc + p_ref[d].astype(f32)
        o_ref[...] = acc

    return pl.pallas_call(
        body, name=name, grid=(R // tr, W // cw),
        in_specs=[pl.BlockSpec((S, tr, cw), lambda i, j: (0, i, j))],
        out_specs=pl.BlockSpec((tr, cw), lambda i, j: (i, j)),
        out_shape=jax.ShapeDtypeStruct((R, W), f32),
        compiler_params=_cparams(2),
    )(parts)


def _tile2d(R, W, budget, mult):
    tr = _rows_tile(R, LANES * 4, budget, mult)
    cw = _tile(W, max(LANES, budget // (tr * 4))) if W % LANES == 0 else W
    return tr, cw


def _core_index():
    return lax.axis_index("c").astype(jnp.int32).reshape(1)


def _add_kept(name, shards, got):
    _, _, R, W = shards.shape
    tr, cw = _tile2d(R, W, 2 << 20, 2 * SUBLANES)

    def body(core_ref, a_ref, b_ref, o_ref):
        o_ref[...] = (a_ref[...].astype(f32) + b_ref[...].astype(f32)).astype(o_ref.dtype)

    return pl.pallas_call(
        body, name=name,
        grid_spec=pltpu.PrefetchScalarGridSpec(
            num_scalar_prefetch=1, grid=(4, R // tr, W // cw),
            in_specs=[pl.BlockSpec((None, None, tr, cw), lambda s, i, j, core: (s, core[0], i, j)),
                      pl.BlockSpec((None, tr, cw), lambda s, i, j, core: (s, i, j))],
            out_specs=pl.BlockSpec((None, tr, cw), lambda s, i, j, core: (s, i, j))),
        out_shape=jax.ShapeDtypeStruct((4, R, W), shards.dtype), compiler_params=_cparams(3),
    )(_core_index(), shards, got)


def _sum_slots_half(name, parts):
    S, R, W = parts.shape
    budget = 4 << 20
    tr = _rows_tile(R, S * W * 4, budget, 2 * SUBLANES)
    cw = W if tr * S * W * 4 <= 2 * budget else _tile(W, max(LANES, 2 * budget // (S * tr * 4)))
    nrow = R // tr

    def body(core_ref, p_ref, o_ref):
        acc = p_ref[0].astype(f32)
        for d in range(1, S):
            acc = acc + p_ref[d].astype(f32)
        o_ref[...] = acc

    return pl.pallas_call(
        body, name=name,
        grid_spec=pltpu.PrefetchScalarGridSpec(
            num_scalar_prefetch=1, grid=(nrow, W // cw),
            in_specs=[pl.BlockSpec((S, tr, cw), lambda i, j, core: (0, i, j))],
            out_specs=pl.BlockSpec((tr, cw), lambda i, j, core: (core[0] * nrow + i, j))),
        out_shape=jax.ShapeDtypeStruct((2 * R, W), f32), compiler_params=_cparams(2),
    )(_core_index(), parts)


def _adamw(name, w, g, m, v):
    R, W = w.shape
    tr, cw = _tile2d(R, W, 2 << 20, SUBLANES)

    def body(w_ref, g_ref, m_ref, v_ref, d_ref, nm_ref, nv_ref):
        g_v = g_ref[...]
        nm = ADAM_B1 * m_ref[...] + (1.0 - ADAM_B1) * g_v
        nv = ADAM_B2 * v_ref[...] + (1.0 - ADAM_B2) * (g_v * g_v)
        m_hat = nm / (1.0 - ADAM_B1 ** ADAM_STEP)
        v_hat = nv / (1.0 - ADAM_B2 ** ADAM_STEP)
        d_ref[...] = -ADAM_LR * (m_hat / (jnp.sqrt(v_hat) + ADAM_EPS) + ADAM_WD * w_ref[...])
        nm_ref[...] = nm
        nv_ref[...] = nv

    blk = pl.BlockSpec((tr, cw), lambda i, j: (i, j))
    return pl.pallas_call(
        body, name=name, grid=(R // tr, W // cw),
        in_specs=[blk] * 4, out_specs=[blk] * 3,
        out_shape=[jax.ShapeDtypeStruct((R, W), f32)] * 3,
        compiler_params=_cparams(2),
    )(w, g, m, v)


ANY = pl.BlockSpec(memory_space=pl.ANY)


def _place():
    return lax.axis_index("x"), lax.axis_index("y"), lax.axis_index("c")


class _Comm:
    def __init__(self, operands, out_shape, scratch, start, wait):
        self.operands, self.out_shape, self.scratch, self.start, self.wait = operands, out_shape, scratch, start, wait


def _run_comm(name, comm):
    n = len(comm.operands)

    def body(*refs):
        parts = (refs[:n], refs[n:2 * n], refs[2 * n:])
        comm.start(*parts)
        comm.wait(*parts)

    return pl.pallas_call(
        body, name=name, in_specs=[ANY] * n, out_specs=[ANY] * n, out_shape=comm.out_shape,
        scratch_shapes=comm.scratch,
    )(*comm.operands)


def _copy_chunks(rows, cols):
    k = SHARE_CHUNKS // 2
    if rows % (k * 2 * SUBLANES) == 0:
        return [(pl.ds(q * (rows // k), rows // k), pl.ds(0, cols)) for q in range(k)]
    if cols % (k * LANES) == 0:
        return [(pl.ds(0, rows), pl.ds(q * (cols // k), cols // k)) for q in range(k)]
    return [(pl.ds(0, rows), pl.ds(0, cols))]


def _gather_chips(arrays):
    n = len(arrays)
    parts = [(a, h, blk) for a, arr in enumerate(arrays) for h in range(2) for blk in _copy_chunks(*arr.shape[1:])]

    def copies(ins, outs, sems):
        send_sems, recv_sems, local_sems = sems
        x, y, c = _place()
        mine = 2 * x + y
        sib = (x, y, 1 - c)
        chips = [(1 - x, y), (x, 1 - y), (1 - x, 1 - y)]

        def local():
            return [pltpu.make_async_copy(ins[a].at[(h, *blk)], outs[a].at[(mine, h, *blk)], local_sems.at[p])
                    for p, (a, h, blk) in enumerate(parts)]

        def over_ici(a, j, slot):
            px, py = chips[j]
            return pltpu.make_async_remote_copy(
                src_ref=ins[a].at[c], dst_ref=outs[a].at[slot, c], send_sem=send_sems.at[3 * a + j],
                recv_sem=recv_sems.at[3 * a + j], device_id=(px, py, c), device_id_type=MESH)

        def over_d2d(a, j, half):
            px, py = chips[j]
            slot = 2 * px + py
            return pltpu.make_async_remote_copy(
                src_ref=outs[a].at[slot, half], dst_ref=outs[a].at[slot, half], send_sem=send_sems.at[3 * (n + a) + j],
                recv_sem=recv_sems.at[3 * (n + a) + j], device_id=sib, device_id_type=MESH)

        pairs = [(a, j) for a in range(n) for j in range(3)]
        return dict(local=local,
                    sends=lambda: [over_ici(a, j, mine) for a, j in pairs],
                    landing=lambda: [over_ici(a, j, 2 * chips[j][0] + chips[j][1]) for a, j in pairs],
                    passed=lambda: [over_d2d(a, j, c) for a, j in pairs],
                    from_sib=lambda: [over_d2d(a, j, 1 - c) for a, j in pairs])

    def start(ins, outs, sems):
        cps = copies(ins, outs, sems)
        for cp in cps["local"]() + cps["sends"]():
            cp.start()

    def wait(ins, outs, sems):
        cps = copies(ins, outs, sems)
        passed = cps["passed"]()
        for got, on in zip(cps["landing"](), passed):
            got.wait_recv()
            on.start()
        for cp in cps["from_sib"]():
            cp.wait_recv()
        for cp in cps["sends"]() + passed:
            cp.wait_send()
        for cp in cps["local"]():
            cp.wait()

    return _Comm(arrays, [jax.ShapeDtypeStruct((4,) + a.shape, a.dtype) for a in arrays],
                 [pltpu.SemaphoreType.DMA((6 * n,)), pltpu.SemaphoreType.DMA((6 * n,)),
                  pltpu.SemaphoreType.DMA((len(parts),))], start, wait)


def _exchange_chips(pieces, whole):
    n, m = len(pieces), len(whole)
    parts = [(a, blk) for a, arr in enumerate(pieces) for blk in _copy_chunks(*arr.shape[1:])]

    def copies(ins, outs, sems):
        send_sems, recv_sems, local_sems = sems
        x, y, c = _place()
        chip, dev = 2 * x + y, 4 * x + 2 * y + c
        chips = [(1 - x, y), (x, 1 - y), (1 - x, 1 - y)]
        peers = [(x ^ (k >> 2), y ^ ((k >> 1) & 1), c ^ (k & 1)) for k in range(1, 8)]
        def local():
            cps = [pltpu.make_async_copy(ins[a].at[(chip, *blk)], outs[a].at[(chip, *blk)], local_sems.at[p])
                   for p, (a, blk) in enumerate(parts)]
            return cps + [pltpu.make_async_copy(ins[n + b], outs[n + b].at[dev], local_sems.at[len(parts) + b])
                          for b in range(m)]

        def piece(a, j, slot_from):
            px, py = chips[j]
            return pltpu.make_async_remote_copy(
                src_ref=ins[a].at[2 * px + py], dst_ref=outs[a].at[slot_from], send_sem=send_sems.at[3 * a + j],
                recv_sem=recv_sems.at[3 * a + j], device_id=(px, py, c), device_id_type=MESH)

        def everyone(b, j, slot_from):
            px, py, pc = peers[j]
            return pltpu.make_async_remote_copy(
                src_ref=ins[n + b], dst_ref=outs[n + b].at[slot_from], send_sem=send_sems.at[3 * n + 7 * b + j],
                recv_sem=recv_sems.at[3 * n + 7 * b + j], device_id=(px, py, pc), device_id_type=MESH)

        def sends():
            return ([everyone(b, j, dev) for b in range(m) for j in range(7)]
                    + [piece(a, j, chip) for a in range(n) for j in range(3)])

        def landing():
            return ([everyone(b, j, 4 * px + 2 * py + pc) for b in range(m) for j, (px, py, pc) in enumerate(peers)]
                    + [piece(a, j, 2 * px + py) for a in range(n) for j, (px, py) in enumerate(chips)])

        return local, sends, landing

    def start(ins, outs, sems):
        local, sends, _ = copies(ins, outs, sems)
        for cp in local() + sends():
            cp.start()

    def wait(ins, outs, sems):
        local, sends, landing = copies(ins, outs, sems)
        for cp in landing():
            cp.wait_recv()
        for cp in sends():
            cp.wait_send()
        for cp in local():
            cp.wait()

    shapes = [jax.ShapeDtypeStruct(a.shape, a.dtype) for a in pieces]
    shapes += [jax.ShapeDtypeStruct((8,) + a.shape, a.dtype) for a in whole]
    nsem = 3 * n + 7 * m
    return _Comm(list(pieces) + list(whole), shapes,
                 [pltpu.SemaphoreType.DMA((nsem,)), pltpu.SemaphoreType.DMA((nsem,)),
                  pltpu.SemaphoreType.DMA((len(parts) + m,))], start, wait)


def _pair_exchange(name, shards):
    n = len(shards)

    def body(*refs):
        ins, outs = refs[:n], refs[n:2 * n]
        send_sems, recv_sems = refs[2 * n:]
        x, y, c = _place()
        copies = [pltpu.make_async_remote_copy(
            src_ref=ins[a].at[s, 1 - c], dst_ref=outs[a].at[s], send_sem=send_sems.at[4 * a + s],
            recv_sem=recv_sems.at[4 * a + s], device_id=(x, y, 1 - c), device_id_type=MESH)
            for a in range(n) for s in range(4)]
        for cp in copies:
            cp.start()
        for cp in copies:
            cp.wait_recv()
        for cp in copies:
            cp.wait_send()

    return pl.pallas_call(
        body, name=name, in_specs=[ANY] * n, out_specs=[ANY] * n,
        out_shape=[jax.ShapeDtypeStruct((4,) + a.shape[2:], a.dtype) for a in shards],
        scratch_shapes=[pltpu.SemaphoreType.DMA((4 * n,)), pltpu.SemaphoreType.DMA((4 * n,))],
    )(*shards)


def _share_halves(name, arrays):
    n = len(arrays)

    def body(*refs):
        bufs = refs[n:2 * n]
        send_sems, recv_sems = refs[2 * n:]
        x, y, c = _place()

        def half(a, h):
            rows = arrays[a].shape[0] // 2
            return bufs[a].at[pl.ds(pl.multiple_of(h * rows, SUBLANES), rows)]

        def copy(a, h):
            return pltpu.make_async_remote_copy(
                src_ref=half(a, h), dst_ref=half(a, h), send_sem=send_sems.at[a], recv_sem=recv_sems.at[a],
                device_id=(x, y, 1 - c), device_id_type=MESH)

        sends = [copy(a, c) for a in range(n)]
        for cp in sends:
            cp.start()
        for a in range(n):
            copy(a, 1 - c).wait_recv()
        for cp in sends:
            cp.wait_send()

    return pl.pallas_call(
        body, name=name, in_specs=[ANY] * n, out_specs=[ANY] * n,
        out_shape=[jax.ShapeDtypeStruct(a.shape, a.dtype) for a in arrays],
        input_output_aliases={a: a for a in range(n)},
        scratch_shapes=[pltpu.SemaphoreType.DMA((n,)), pltpu.SemaphoreType.DMA((n,))],
    )(*arrays)


def _place_blocks(blocks, axis):
    shape = list(blocks[0].shape)
    shape[axis] = sum(b.shape[axis] for b in blocks)
    buf = lax.empty(tuple(shape), blocks[0].dtype)
    at = 0
    for b in blocks:
        buf = lax.dynamic_update_slice_in_dim(buf, b, at, axis)
        at += b.shape[axis]
    return buf


def kernel(x, norm_pre_g, w_in, mu_shift, w0, w_lora_up, a0, a_lora_up, k_k, k_a, r_k, lnx_g, lnx_b, conv_w, conv_b, cln_g, cln_b, w_pw2, b_pw2, w_out, norm_post_g, loss_target, m_norm_pre_g, m_w_in, m_mu_shift, m_w0, m_w_lora_up, m_a0, m_a_lora_up, m_k_k, m_k_a, m_r_k, m_lnx_g, m_lnx_b, m_conv_w, m_conv_b, m_cln_g, m_cln_b, m_w_pw2, m_b_pw2, m_w_out, m_norm_post_g, v_norm_pre_g, v_w_in, v_mu_shift, v_w0, v_w_lora_up, v_a0, v_a_lora_up, v_k_k, v_k_a, v_r_k, v_lnx_g, v_lnx_b, v_conv_w, v_conv_b, v_cln_g, v_cln_b, v_w_pw2, v_b_pw2, v_w_out, v_norm_post_g):
    _, T, D = x.shape
    RW = w0.shape[0]
    CW = conv_b.shape[0]
    head = r_k.shape[1]
    lora = w_lora_up.shape[0]
    ktaps = conv_w.shape[0]
    assert RW == CW and 2 * lora <= LORA_PAD and ktaps - 1 <= CONV_HALO
    n_in = 3 * RW + 2 * lora + RW + 3 * CW
    shard = n_in // 4
    PW = 7 * RW + LORA_PAD
    off_l = 7 * RW
    tm = min(256, T // 2)
    tm_wide = min(128, T // 2)
    tm_halo = min(512, T // 2)
    row = lambda vec: vec.reshape(1, -1)
    x2, tgt2 = x[0], loss_target[0]

    halves = lambda a: a.reshape(2, a.shape[0] // 2, a.shape[1])
    conv_w_p = jnp.concatenate([conv_w, jnp.zeros((CONV_HALO - ktaps, CW // 4), f32)], axis=0)
    w_in_t, m_w_in_t, v_w_in_t = w_in.T, m_w_in.T, v_w_in.T
    (g_win,) = _run_comm("gather_w_in", _gather_chips([halves(w_in_t.astype(bf16))]))
    win_t = g_win.reshape(n_in, D)
    lo = 3 * RW
    wp_t = _place_blocks([win_t[:lo], win_t[lo + 2 * lora:], win_t[lo:lo + 2 * lora],
                          jnp.zeros((LORA_PAD - 2 * lora, D), bf16)], axis=0)
    npg = row(norm_pre_g)
    (h,) = _row_fwd("rms_pre", _fn_rms_pre, [(npg, False)], [(x2, 0, D, False)], [(D, bf16)], T, tm)
    others = [halves(a) for a in (w_lora_up, a_lora_up, conv_w_p, w_pw2.astype(bf16), w_out.astype(bf16))]
    proj, (g_wup, g_aup, g_cw, g_pw2, g_wout) = _matmul("proj", h, wp_t, "nt", f32, comm=_gather_chips(others))
    cat_cols = lambda g: jnp.concatenate([g[s].reshape(-1, g.shape[-1]) for s in range(4)], axis=1)
    wup_full, aup_full, cw_p = cat_cols(g_wup), cat_cols(g_aup), cat_cols(g_cw)
    zl = lambda n: jnp.zeros((n, RW), f32)
    wup_p = jnp.concatenate([wup_full, zl(LORA_PAD - lora)], axis=0)
    aup_p = jnp.concatenate([zl(lora), aup_full, zl(LORA_PAD - 2 * lora)], axis=0)
    pw2_full = g_pw2.reshape(CW, CW)
    wout_full = g_wout.reshape(RW + CW, D)
    mu_r, mu_k, mu_v = (row(mu_shift[s * RW:(s + 1) * RW]) for s in range(3))
    mu_l = row(jnp.concatenate([mu_shift[3 * RW:], jnp.zeros((LORA_PAD - 2 * lora,), f32)]))

    xs_r = _shift_fwd("shift_r", proj, 0, RW, mu_r, T, tm_halo)
    xs_k = _shift_fwd("shift_k", proj, RW, RW, mu_k, T, tm_halo)
    xs_v = _shift_fwd("shift_v", proj, 2 * RW, RW, mu_v, T, tm_halo)
    xs_l = _shift_fwd("shift_l", proj, off_l, LORA_PAD, mu_l, T, tm_halo)
    lora_params = [(row(w0), False), (wup_p, False), (row(a0), False), (aup_p, False)]
    qw, qa = _row_fwd("lora_up", _fn_lora, lora_params, [(xs_l, 0, LORA_PAD, False)], [(RW, f32), (RW, f32)], T, tm)
    ncol = RW // _tile(RW, 512)
    fn_pre = functools.partial(_fn_rwkv_pre, head)
    pre_params = [(row(k_k), True), (row(k_a), True)]
    pre_rows = [(xs_k, 0, RW, True), (qw, 0, RW, True), (qa, 0, RW, True)]
    lw, k_h, a_rec, b_rec = _row_fwd("rwkv_pre", fn_pre, pre_params, pre_rows, [(RW, f32)] * 4, T, tm, ncol)
    wkv_in = (xs_r, lw, k_h, xs_v, a_rec, b_rec)
    c_w, c_u, c_rt, c_pb, c_y0, c_bend, c_z = _wkv_local_fwd(*wkv_in, head, T, RW)
    wkv_loc = (c_w, c_u, c_rt, c_pb, c_bend, lw, c_y0, c_z)
    y_wkv, states = _wkv_state_fwd(*wkv_loc, head, T, RW)
    fn_post = functools.partial(_fn_rwkv_post, head)
    post_params = [(row(lnx_g), True), (row(lnx_b), True), (r_k.reshape(1, RW), True)]
    post_rows = [(y_wkv, 0, RW, True), (xs_r, 0, RW, True), (k_h, 0, RW, True), (xs_v, 0, RW, True),
                 (proj, 3 * RW, RW, True)]
    (y_rwkv,) = _row_fwd("rwkv_post", fn_post, post_params, post_rows, [(RW, bf16)], T, tm, ncol)

    c_pre = _conv_fwd(proj, 4 * RW, 5 * RW, CW, cw_p, row(conv_b), ktaps, T, tm_halo)
    ln_params = [(row(cln_g), False), (row(cln_b), False)]
    (c_act,) = _row_fwd("conv_ln", _fn_conv_ln, ln_params, [(c_pre, 0, CW, False)], [(CW, bf16)], T, tm)
    c2 = _matmul("pw2", c_act, pw2_full, "nn", f32)
    cpost_params = [(row(b_pw2), True)]
    cpost_rows = [(c2, 0, CW, True), (proj, 6 * RW, CW, True)]
    (y_conv,) = _row_fwd("conv_post", _fn_conv_post, cpost_params, cpost_rows, [(CW, bf16)], T, tm, ncol)

    mix = jnp.concatenate([y_rwkv, y_conv], axis=1)
    out = _matmul("out_proj", mix, wout_full, "nn", f32)
    d_out, gx_res, loss_part, g_npost = _post(out, x2, tgt2, row(norm_post_g), T, D, tm_wide)

    g_wout_full = _matmul("d_w_out", mix, d_out, "tn", bf16)
    d_mix = _matmul("d_mix", d_out, wout_full, "nt", f32)

    d_c2, d_gconv, g_bpw2 = _row_bwd("conv_post_bwd", _fn_conv_post, cpost_params, cpost_rows,
                                      [(d_mix, RW, CW, True)], [bf16, bf16], T, tm, ncol)
    g_pw2_full = _matmul("d_w_pw2", c_act, d_c2, "tn", bf16)
    d_cact = _matmul("d_c_act", d_c2, pw2_full, "nt", f32)
    d_cpre, g_clng, g_clnb = _row_bwd("conv_ln_bwd", _fn_conv_ln, ln_params, [(c_pre, 0, CW, False)],
                                      [(d_cact, 0, CW, False)], [f32], T, tm)
    d_gluv, d_glug, g_cw_p, g_cb = _conv_bwd(proj, 4 * RW, 5 * RW, CW, cw_p, d_cpre, ktaps, T, tm_halo)

    d_y, dr_x, dk_x, dv_x, d_grwkv, g_lnxg, g_lnxb, g_rk = _row_bwd(
        "rwkv_post_bwd", fn_post, post_params, post_rows, [(d_mix, 0, RW, True)], [f32, f32, f32, f32, bf16], T, tm, ncol)
    d_cw, d_cu, d_crt, d_cpb, d_cbend, d_lw_dec, d_cz = _wkv_state_bwd(*wkv_loc, states, d_y, head, T, RW)
    d_xr, d_lw, d_kh, d_xv, d_a, d_b = _wkv_local_bwd(
        *wkv_in, (d_cw, d_cu, d_crt, d_cpb, d_y, d_cbend, d_cz), d_lw_dec, dr_x, dk_x, dv_x, head, T, RW)
    pre_cots = [(d_lw, 0, RW, True), (d_kh, 0, RW, True), (d_a, 0, RW, True), (d_b, 0, RW, True)]
    d_xk, d_qw, d_qa, g_kk, g_ka = _row_bwd("rwkv_pre_bwd", fn_pre, pre_params, pre_rows, pre_cots, [f32, f32, f32],
                                            T, tm, ncol)
    d_xl, g_w0, g_wup_p, g_a0, g_aup_p = _row_bwd("lora_up_bwd", _fn_lora, lora_params, [(xs_l, 0, LORA_PAD, False)],
                                                  [(d_qw, 0, RW, False), (d_qa, 0, RW, False)], [f32], T, tm)
    dp_r, g_mur = _shift_bwd("shift_r_bwd", proj, 0, RW, mu_r, d_xr, T, tm_halo)
    dp_k, g_muk = _shift_bwd("shift_k_bwd", proj, RW, RW, mu_k, d_xk, T, tm_halo)
    dp_v, g_muv = _shift_bwd("shift_v_bwd", proj, 2 * RW, RW, mu_v, d_xv, T, tm_halo)
    dp_l, g_mul = _shift_bwd("shift_l_bwd", proj, off_l, LORA_PAD, mu_l, d_xl, T, tm_halo)
    d_proj = _place_blocks([dp_r, dp_k, dp_v, d_grwkv, d_gluv, d_glug, d_gconv, dp_l], axis=1)

    def chip_sums(tag, shards):
        halves4 = [a.reshape(4, 2, a.shape[1] // 2, a.shape[2]) for a in shards]
        got = _pair_exchange("pair_exchange_" + tag, halves4)
        return [_add_kept("chip_sum_%s_%d" % (tag, i), a, g) for i, (a, g) in enumerate(zip(halves4, got))]

    def all_chips(tag, slots):
        return _share_halves("share_" + tag, [_sum_slots_half("sum_%s_%d" % (tag, i), r) for i, r in enumerate(slots)])

    q_early = chip_sums("early", [g_wout_full.reshape(4, (RW + CW) // 4, D), g_pw2_full.reshape(4, CW // 4, CW)])
    g_wp_t, r_early = _matmul("d_w_in", d_proj, h, "tn", bf16, comm=_exchange_chips(q_early, []))
    g_win_t = _place_blocks([g_wp_t[:lo], g_wp_t[off_l:off_l + 2 * lora], g_wp_t[lo:off_l]], axis=0)
    col_shards = lambda a: a.reshape(a.shape[0], 4, a.shape[1] // 4).transpose(1, 0, 2)
    q_late = chip_sums("late", [g_win_t.reshape(4, shard, D), col_shards(g_wup_p[:lora]),
                                col_shards(g_aup_p[lora:2 * lora]), col_shards(g_cw_p)])
    g_mu = jnp.concatenate([g_mur[0], g_muk[0], g_muv[0], g_mul[0, :2 * lora]])
    pad_rows = lambda a, n: jnp.concatenate([a, jnp.zeros((n - a.shape[0], a.shape[1]), f32)], axis=0)
    n_mu = -(-mu_shift.shape[0] // RW)
    small_vecs = [pad_rows(jnp.pad(g_mu, (0, n_mu * RW - g_mu.shape[0])).reshape(n_mu, RW), n_mu),
                  g_w0, g_a0, g_kk, g_ka, g_rk, g_lnxg, g_lnxb, g_cb, g_clng, g_clnb, g_bpw2,
                  g_npost.reshape(D // RW, RW)]
    n_small = sum(a.shape[0] for a in small_vecs)
    n_small_pad = -(-n_small // (2 * SUBLANES)) * (2 * SUBLANES)
    small = pad_rows(jnp.concatenate(small_vecs, axis=0), n_small_pad)

    d_h, r_late = _matmul("d_h", d_proj, wp_t, "nn", bf16, tk_t=2560, comm=_exchange_chips(q_late, [small]))
    grad_x2, g_npre = _rms_pre_bwd(x2, npg, d_h, gx_res, T, D, tm_wide)
    (r_npre,) = _run_comm("exchange_norm_pre", _exchange_chips([], [g_npre.reshape(D // RW, RW)]))
    s_npre = _sum_slots("sum_norm_pre", r_npre)
    s_small = _sum_slots("sum_small", r_late[4])
    grad_w_out, grad_w_pw2 = all_chips("early", r_early)
    grad_w_in, grad_wup, grad_aup, grad_cw = all_chips("late", r_late[:4])

    pos = [0]

    def take(nrows):
        a = s_small[pos[0]:pos[0] + nrows]
        pos[0] += nrows
        return a

    grads = {}
    grads["norm_pre_g"] = s_npre.reshape(D)
    grads["mu_shift"] = take(n_mu).reshape(-1)[:mu_shift.shape[0]]
    for nm in ["w0", "a0", "k_k", "k_a"]:
        grads[nm] = take(1).reshape(RW)
    grads["r_k"] = take(1).reshape(r_k.shape)
    for nm in ["lnx_g", "lnx_b", "conv_b", "cln_g", "cln_b", "b_pw2"]:
        grads[nm] = take(1).reshape(RW)
    grads["norm_post_g"] = take(D // RW).reshape(D)
    grads["w_lora_up"], grads["a_lora_up"], grads["conv_w"] = grad_wup, grad_aup, grad_cw[:ktaps]
    grads["w_in"], grads["w_out"], grads["w_pw2"] = grad_w_in, grad_w_out, grad_w_pw2

    weights = dict(norm_pre_g=norm_pre_g, w_in=w_in, mu_shift=mu_shift, w0=w0, w_lora_up=w_lora_up, a0=a0,
                   a_lora_up=a_lora_up, k_k=k_k, k_a=k_a, r_k=r_k, lnx_g=lnx_g, lnx_b=lnx_b, conv_w=conv_w,
                   conv_b=conv_b, cln_g=cln_g, cln_b=cln_b, w_pw2=w_pw2, b_pw2=b_pw2, w_out=w_out,
                   norm_post_g=norm_post_g)
    ms = dict(norm_pre_g=m_norm_pre_g, w_in=m_w_in, mu_shift=m_mu_shift, w0=m_w0, w_lora_up=m_w_lora_up, a0=m_a0,
              a_lora_up=m_a_lora_up, k_k=m_k_k, k_a=m_k_a, r_k=m_r_k, lnx_g=m_lnx_g, lnx_b=m_lnx_b, conv_w=m_conv_w,
              conv_b=m_conv_b, cln_g=m_cln_g, cln_b=m_cln_b, w_pw2=m_w_pw2, b_pw2=m_b_pw2, w_out=m_w_out,
              norm_post_g=m_norm_post_g)
    vs = dict(norm_pre_g=v_norm_pre_g, w_in=v_w_in, mu_shift=v_mu_shift, w0=v_w0, w_lora_up=v_w_lora_up, a0=v_a0,
              a_lora_up=v_a_lora_up, k_k=v_k_k, k_a=v_k_a, r_k=v_r_k, lnx_g=v_lnx_g, lnx_b=v_lnx_b, conv_w=v_conv_w,
              conv_b=v_conv_b, cln_g=v_cln_g, cln_b=v_cln_b, w_pw2=v_w_pw2, b_pw2=v_b_pw2, w_out=v_w_out,
              norm_post_g=v_norm_post_g)
    names = list(weights)
    big = ["w_in", "w_out", "w_pw2"]
    deltas, new_m, new_v = {}, {}, {}
    d_t, m_t, v_t = _adamw("adamw_w_in", w_in_t, grad_w_in, m_w_in_t, v_w_in_t)
    grads["w_in"], deltas["w_in"], new_m["w_in"], new_v["w_in"] = grad_w_in.T, d_t.T, m_t.T, v_t.T
    for nm in big[1:]:
        deltas[nm], new_m[nm], new_v[nm] = _adamw("adamw_" + nm, weights[nm], grads[nm], ms[nm], vs[nm])
    rest = [nm for nm in names if nm not in big]
    sizes = [weights[nm].size for nm in rest]
    total = sum(sizes)
    width = 4 * LANES
    rows_p = -(-total // (width * SUBLANES)) * SUBLANES

    def pack(d):
        flat = jnp.concatenate([d[nm].reshape(-1) for nm in rest])
        return jnp.pad(flat, (0, rows_p * width - total)).reshape(rows_p, width)

    p_d, p_m, p_v = _adamw("adamw_small", pack(weights), pack(grads), pack(ms), pack(vs))
    o = 0
    for nm, sz in zip(rest, sizes):
        shp = weights[nm].shape
        deltas[nm] = p_d.reshape(-1)[o:o + sz].reshape(shp)
        new_m[nm] = p_m.reshape(-1)[o:o + sz].reshape(shp)
        new_v[nm] = p_v.reshape(-1)[o:o + sz].reshape(shp)
        o += sz

    loss = lax.psum(loss_part[0, 0], ("x", "y", "c"))
    grad_x = grad_x2[None]
    return (loss, grad_x, *[grads[nm] for nm in names], *[deltas[nm] for nm in names],
            *[new_m[nm] for nm in names], *[new_v[nm] for nm in names])
```

```python
import functools

import jax
import jax.numpy as jnp
from jax import lax
from jax.experimental import pallas as pl
from jax.experimental.pallas import tpu as pltpu

f32 = jnp.float32
bf16 = jnp.bfloat16
MESH = pl.DeviceIdType.MESH

NORM_EPS = 1e-6
LN_EPS = 1e-5
ADAM_LR, ADAM_B1, ADAM_B2, ADAM_EPS, ADAM_WD, ADAM_STEP = 0.001, 0.9, 0.999, 1e-08, 0.01, 10

LANES = 128
SUBLANES = 8
LORA_PAD = 256
CONV_HALO = 32
WKV_CHUNK = 64
WKV_HEADS = 16
WKV_STATE_HEADS = 32
WKV_PREC = lax.Precision.HIGH
SHARE_CHUNKS = 8
VMEM_LIMIT = 56 * 1024 * 1024


def _cparams(n_axes):
    return pltpu.CompilerParams(dimension_semantics=("arbitrary",) * n_axes, vmem_limit_bytes=VMEM_LIMIT)


def _tile(dim, target):
    best = None
    t = LANES
    while t <= min(dim, target):
        if dim % t == 0:
            best = t
        t += LANES
    return best if best is not None else dim


def _mm(a, b, prec=None):
    return lax.dot_general(a, b, (((1,), (0,)), ((), ())), precision=prec, preferred_element_type=f32)


def _mm_nt(a, b, prec=None):
    return lax.dot_general(a, b, (((1,), (1,)), ((), ())), precision=prec, preferred_element_type=f32)


def _mm_tn(a, b, prec=None):
    return lax.dot_general(a, b, (((0,), (0,)), ((), ())), precision=prec, preferred_element_type=f32)


@jax.custom_vjp
def _bmm(a, b):
    return _mm(a.astype(bf16), b.astype(bf16))


def _bmm_fwd(a, b):
    return _bmm(a, b), (a, b)


def _bmm_bwd(res, dc):
    a, b = res
    dcb = dc.astype(bf16)
    return _mm_nt(dcb, b.astype(bf16)), _mm_tn(a.astype(bf16), dcb)


_bmm.defvjp(_bmm_fwd, _bmm_bwd)


@jax.custom_vjp
def _bmm_nt(a, b):
    return _mm_nt(a.astype(bf16), b.astype(bf16))


def _bmm_nt_fwd(a, b):
    return _bmm_nt(a, b), (a, b)


def _bmm_nt_bwd(res, dc):
    a, b = res
    dcb = dc.astype(bf16)
    return _mm(dcb, b.astype(bf16)), _mm_tn(dcb, a.astype(bf16))


_bmm_nt.defvjp(_bmm_nt_fwd, _bmm_nt_bwd)


@jax.custom_vjp
def _bmm_tn(a, b):
    return _mm_tn(a.astype(bf16), b.astype(bf16))


def _bmm_tn_fwd(a, b):
    return _bmm_tn(a, b), (a, b)


def _bmm_tn_bwd(res, dc):
    a, b = res
    dcb = dc.astype(bf16)
    return _mm_nt(b.astype(bf16), dcb), _mm(a.astype(bf16), dcb)


_bmm_tn.defvjp(_bmm_tn_fwd, _bmm_tn_bwd)


def _matmul(name, a, b, mode, out_dtype, tm_t=1024, tn_t=1024, tk_t=4096, comm=None):
    if mode == "nn":
        (M, K), (_, N) = a.shape, b.shape
    elif mode == "nt":
        (M, K), (N, _) = a.shape, b.shape
    else:
        (K, M), (_, N) = a.shape, b.shape
    tm, tn, tk = _tile(M, tm_t), _tile(N, tn_t), _tile(K, tk_t)
    ni, nj, nk = M // tm, N // tn, K // tk
    dot = {"nn": _mm, "nt": _mm_nt, "tn": _mm_tn}[mode]
    nc = len(comm.operands) if comm else 0

    def body(*refs):
        a_ref, b_ref = refs[:2]
        o_ref = refs[2 + nc]
        scratch = refs[3 + 2 * nc:]
        i, j, k = pl.program_id(0), pl.program_id(1), pl.program_id(2)
        if comm:
            comm_refs = (refs[2:2 + nc], refs[3 + nc:3 + 2 * nc], scratch[:len(comm.scratch)])

            @pl.when(jnp.logical_and(jnp.logical_and(i == 0, j == 0), k == 0))
            def _():
                comm.start(*comm_refs)

        if nk == 1:
            o_ref[...] = dot(a_ref[...], b_ref[...]).astype(o_ref.dtype)
        else:
            acc_ref = scratch[-1]

            @pl.when(k == 0)
            def _():
                acc_ref[...] = jnp.zeros_like(acc_ref)

            acc_ref[...] += dot(a_ref[...], b_ref[...])

            @pl.when(k == nk - 1)
            def _():
                o_ref[...] = acc_ref[...].astype(o_ref.dtype)

        if comm:
            @pl.when(jnp.logical_and(jnp.logical_and(i == ni - 1, j == nj - 1), k == nk - 1))
            def _():
                comm.wait(*comm_refs)

    a_spec = {"nn": pl.BlockSpec((tm, tk), lambda i, j, k: (i, k)),
              "nt": pl.BlockSpec((tm, tk), lambda i, j, k: (i, k)),
              "tn": pl.BlockSpec((tk, tm), lambda i, j, k: (k, i))}[mode]
    b_spec = {"nn": pl.BlockSpec((tk, tn), lambda i, j, k: (k, j)),
              "nt": pl.BlockSpec((tn, tk), lambda i, j, k: (j, k)),
              "tn": pl.BlockSpec((tk, tn), lambda i, j, k: (k, j))}[mode]
    res = pl.pallas_call(
        body, name=name, grid=(ni, nj, nk),
        in_specs=[a_spec, b_spec] + [ANY] * nc,
        out_specs=[pl.BlockSpec((tm, tn), lambda i, j, k: (i, j))] + [ANY] * nc,
        out_shape=[jax.ShapeDtypeStruct((M, N), out_dtype)] + (list(comm.out_shape) if comm else []),
        scratch_shapes=(list(comm.scratch) if comm else []) + ([pltpu.VMEM((tm, tn), f32)] if nk > 1 else []),
        compiler_params=_cparams(3),
    )(a, b, *(comm.operands if comm else []))
    return (res[0], res[1:]) if comm else res[0]


def _row_spec(op, tm, ncol):
    arr, off, width, tiled = op
    if tiled:
        cw = width // ncol
        return pl.BlockSpec((tm, cw), lambda j, i: (i, off // cw + j))
    return pl.BlockSpec((tm, width), lambda j, i: (i, off // width))


def _param_spec(p, ncol):
    arr, tiled = p
    rows, width = arr.shape
    if tiled:
        return pl.BlockSpec((rows, width // ncol), lambda j, i: (0, j))
    return pl.BlockSpec((rows, width), lambda j, i: (0, 0))


def _row_fwd(name, fn, params, rows, outs, T, tm, ncol=1):
    npar, nrow = len(params), len(rows)

    def body(*refs):
        pv = [r[...] for r in refs[:npar]]
        rv = [r[...].astype(f32) for r in refs[npar:npar + nrow]]
        res = fn(*pv, *rv)
        for o_ref, val in zip(refs[npar + nrow:], res):
            o_ref[...] = val.astype(o_ref.dtype)

    return pl.pallas_call(
        body, name=name, grid=(ncol, T // tm),
        in_specs=[_param_spec(p, ncol) for p in params] + [_row_spec(r, tm, ncol) for r in rows],
        out_specs=[pl.BlockSpec((tm, w // ncol), lambda j, i: (i, j)) for w, _ in outs],
        out_shape=[jax.ShapeDtypeStruct((T, w), dt) for w, dt in outs],
        compiler_params=_cparams(2),
    )(*[p[0] for p in params], *[r[0] for r in rows])


def _row_bwd(name, fn, params, rows, cots, row_grads, T, tm, ncol=1):
    npar, nrow, ncot = len(params), len(rows), len(cots)
    want = [k for k, dt in enumerate(row_grads) if dt is not None]

    def body(*refs):
        pv = [r[...] for r in refs[:npar]]
        rv = [r[...].astype(f32) for r in refs[npar:npar + nrow]]
        cv = tuple(r[...].astype(f32) for r in refs[npar + nrow:npar + nrow + ncot])
        out_refs = refs[npar + nrow + ncot:]
        _, vjp = jax.vjp(fn, *pv, *rv)
        grads = vjp(cv)
        for o_ref, k in zip(out_refs[:len(want)], want):
            o_ref[...] = grads[npar + k].astype(o_ref.dtype)
        j, i = pl.program_id(0), pl.program_id(1)
        for o_ref, p, g in zip(out_refs[len(want):], params, grads[:npar]):
            first = (i == 0) if p[1] else jnp.logical_and(i == 0, j == 0)

            @pl.when(first)
            def _():
                o_ref[...] = jnp.zeros_like(o_ref)

            o_ref[...] += g

    def grad_spec(op):
        arr, off, width, tiled = op
        if tiled:
            return pl.BlockSpec((tm, width // ncol), lambda j, i: (i, j)), (T, width)
        return pl.BlockSpec((tm, width), lambda j, i: (i, j)), (T, width * ncol)

    gspecs = [grad_spec(rows[k]) for k in want]
    return pl.pallas_call(
        body, name=name, grid=(ncol, T // tm),
        in_specs=[_param_spec(p, ncol) for p in params] + [_row_spec(r, tm, ncol) for r in rows]
        + [_row_spec(c, tm, ncol) for c in cots],
        out_specs=[s for s, _ in gspecs] + [_param_spec(p, ncol) for p in params],
        out_shape=[jax.ShapeDtypeStruct(shp, row_grads[k]) for (_, shp), k in zip(gspecs, want)]
        + [jax.ShapeDtypeStruct(p[0].shape, f32) for p in params],
        compiler_params=_cparams(2),
    )(*[p[0] for p in params], *[r[0] for r in rows], *[c[0] for c in cots])


def _seg_sum(x, head):
    li = lax.broadcasted_iota(jnp.int32, (LANES, LANES), 0) // head
    lj = lax.broadcasted_iota(jnp.int32, (LANES, LANES), 1) // head
    q = (li == lj).astype(f32)
    parts = [_mm(x[:, s:s + LANES], q, lax.Precision.HIGH) for s in range(0, x.shape[1], LANES)]
    return parts[0] if len(parts) == 1 else jnp.concatenate(parts, axis=1)


def _sigmoid(z):
    return 1.0 / (1.0 + jnp.exp(-z))


def _silu(z):
    return z * _sigmoid(z)


def _rms(g, x):
    return x * lax.rsqrt(jnp.mean(x * x, axis=-1, keepdims=True) + NORM_EPS) * g


def _fn_rms_pre(g, x):
    return (_rms(g, x),)


def _fn_lora(w0, wup, a0, aup, xl):
    qw = w0 + _bmm(jnp.tanh(xl), wup)
    qa = a0 + _bmm(xl, aup)
    return qw, qa


def _fn_rwkv_pre(head, k_k, k_a, xk, qw, qa):
    w_log = -(jnp.maximum(-qw, 0.0) + jnp.log(1.0 + jnp.exp(-jnp.abs(qw)))) - 0.5
    lw = -jnp.exp(w_log)
    a_sig = _sigmoid(qa)
    kk = xk * k_k
    kk = kk / jnp.maximum(jnp.sqrt(_seg_sum(kk * kk, head)), 1e-12)
    k_h = xk * (1.0 + (a_sig - 1.0) * k_a)
    return lw, k_h, -kk, kk * a_sig


def _fn_rwkv_post(head, lnx_g, lnx_b, r_k, y, r, k_h, v, g):
    inv = 1.0 / head
    mu = _seg_sum(y, head) * inv
    d = y - mu
    var = _seg_sum(d * d, head) * inv
    yn = d * lax.rsqrt(var + 1e-5 * head) * lnx_g + lnx_b
    bonus = _seg_sum(r * k_h * r_k, head) * v
    return ((yn + bonus) * _silu(g),)


def _fn_conv_ln(cln_g, cln_b, c):
    mu = jnp.mean(c, axis=-1, keepdims=True)
    d = c - mu
    var = jnp.mean(d * d, axis=-1, keepdims=True)
    return (_silu(d * lax.rsqrt(var + LN_EPS) * cln_g + cln_b),)


def _fn_conv_post(b_pw2, c2, g):
    return ((c2 + b_pw2) * _silu(g),)


def _post(out, x, tgt, g, T, D, tm):
    def body(g_ref, o_ref, x_ref, t_ref, dout_ref, gx_ref, loss_ref, dg_ref):
        i = pl.program_id(0)
        o, vjp = jax.vjp(_rms, g_ref[...], o_ref[...])
        err = x_ref[...] + o - t_ref[...]
        d_y = err * (1.0 / D)
        dg, d_out = vjp(d_y)
        dout_ref[...] = d_out.astype(dout_ref.dtype)
        gx_ref[...] = d_y

        @pl.when(i == 0)
        def _():
            loss_ref[...] = jnp.zeros_like(loss_ref)
            dg_ref[...] = jnp.zeros_like(dg_ref)

        loss_ref[...] += jnp.sum(err * err, keepdims=True) * (0.5 / D)
        dg_ref[...] += dg

    row = pl.BlockSpec((tm, D), lambda i: (i, 0))
    vec = pl.BlockSpec((1, D), lambda i: (0, 0))
    return pl.pallas_call(
        body, name="post_loss", grid=(T // tm,),
        in_specs=[vec, row, row, row],
        out_specs=[row, row, pl.BlockSpec((1, 1), lambda i: (0, 0)), vec],
        out_shape=[jax.ShapeDtypeStruct((T, D), bf16), jax.ShapeDtypeStruct((T, D), f32),
                   jax.ShapeDtypeStruct((1, 1), f32), jax.ShapeDtypeStruct((1, D), f32)],
        compiler_params=_cparams(1),
    )(g, out, x, tgt)


def _rms_pre_bwd(x, g, dh, gx_res, T, D, tm):
    def body(g_ref, x_ref, dh_ref, res_ref, dx_ref, dg_ref):
        i = pl.program_id(0)
        _, vjp = jax.vjp(_rms, g_ref[...], x_ref[...])
        dg, dx = vjp(dh_ref[...].astype(f32))
        dx_ref[...] = dx + res_ref[...]

        @pl.when(i == 0)
        def _():
            dg_ref[...] = jnp.zeros_like(dg_ref)

        dg_ref[...] += dg

    row = pl.BlockSpec((tm, D), lambda i: (i, 0))
    vec = pl.BlockSpec((1, D), lambda i: (0, 0))
    return pl.pallas_call(
        body, name="rms_pre_bwd", grid=(T // tm,),
        in_specs=[vec, row, row, row], out_specs=[row, vec],
        out_shape=[jax.ShapeDtypeStruct((T, D), f32), jax.ShapeDtypeStruct((1, D), f32)],
        compiler_params=_cparams(1),
    )(g, x, dh, gx_res)


def _prev_rows(cur, halo_ref, first):
    top = jnp.where(first, 0.0, halo_ref[SUBLANES - 1:SUBLANES, :])
    rolled = pltpu.roll(cur, 1, 0)
    rid = lax.broadcasted_iota(jnp.int32, cur.shape, 0)
    return jnp.where(rid == 0, top, rolled)


def _shift_fwd(name, proj, off, width, mu, T, tm):
    cw = _tile(width, 512)
    ncol, cb = width // cw, off // cw
    hb = tm // SUBLANES

    def body(mu_ref, cur_ref, halo_ref, o_ref):
        i = pl.program_id(1)
        cur = cur_ref[...]
        prev = _prev_rows(cur, halo_ref, i == 0)
        o_ref[...] = cur + (prev - cur) * mu_ref[...]

    return pl.pallas_call(
        body, name=name, grid=(ncol, T // tm),
        in_specs=[pl.BlockSpec((1, cw), lambda j, i: (0, j)),
                  pl.BlockSpec((tm, cw), lambda j, i: (i, cb + j)),
                  pl.BlockSpec((SUBLANES, cw), lambda j, i: (jnp.maximum(i * hb - 1, 0), cb + j))],
        out_specs=pl.BlockSpec((tm, cw), lambda j, i: (i, j)),
        out_shape=jax.ShapeDtypeStruct((T, width), f32),
        compiler_params=_cparams(2),
    )(mu, proj, proj)


def _shift_bwd(name, proj, off, width, mu, dxs, T, tm):
    cw = _tile(width, 512)
    ncol, cb = width // cw, off // cw
    hb, nt = tm // SUBLANES, T // tm

    def body(mu_ref, cur_ref, halo_ref, d_ref, dnext_ref, o_ref, dmu_ref):
        i = pl.program_id(1)
        cur = cur_ref[...]
        prev = _prev_rows(cur, halo_ref, i == 0)
        d = d_ref[...]
        bottom = jnp.where(i == nt - 1, 0.0, dnext_ref[0:1, :])
        rid = lax.broadcasted_iota(jnp.int32, d.shape, 0)
        d_next = jnp.where(rid == tm - 1, bottom, pltpu.roll(d, tm - 1, 0))
        mu_v = mu_ref[...]
        o_ref[...] = (d * (1.0 - mu_v) + d_next * mu_v).astype(o_ref.dtype)

        @pl.when(i == 0)
        def _():
            dmu_ref[...] = jnp.zeros_like(dmu_ref)

        dmu_ref[...] += jnp.sum(d * (prev - cur), axis=0, keepdims=True)

    return pl.pallas_call(
        body, name=name, grid=(ncol, nt),
        in_specs=[pl.BlockSpec((1, cw), lambda j, i: (0, j)),
                  pl.BlockSpec((tm, cw), lambda j, i: (i, cb + j)),
                  pl.BlockSpec((SUBLANES, cw), lambda j, i: (jnp.maximum(i * hb - 1, 0), cb + j)),
                  pl.BlockSpec((tm, cw), lambda j, i: (i, j)),
                  pl.BlockSpec((SUBLANES, cw), lambda j, i: (jnp.minimum((i + 1) * hb, nt * hb - 1), j))],
        out_specs=[pl.BlockSpec((tm, cw), lambda j, i: (i, j)), pl.BlockSpec((1, cw), lambda j, i: (0, j))],
        out_shape=[jax.ShapeDtypeStruct((T, width), bf16), jax.ShapeDtypeStruct((1, width), f32)],
        compiler_params=_cparams(2),
    )(mu, proj, proj, dxs, dxs)


def _rolled_copies(dst_ref, ext):
    n = ext.shape[0]
    dst_ref[0] = ext
    for r in range(1, SUBLANES):
        dst_ref[r] = pltpu.roll(ext, n - r, 0)


def _window(rolled_ref, start, rows):
    q, r = divmod(start, SUBLANES)
    return rolled_ref[r, pl.ds(SUBLANES * q, rows), :]


def _conv_fwd(proj, off_v, off_g, width, conv_w, conv_b, ktaps, T, tm):
    cw = _tile(width, 512)
    ncol = width // cw
    hb = tm // CONV_HALO
    lead = CONV_HALO - (ktaps - 1)

    def body(w_ref, b_ref, v_ref, g_ref, hv_ref, hg_ref, o_ref, u_ref):
        i = pl.program_id(1)
        halo = hv_ref[...] * _sigmoid(hg_ref[...])
        _rolled_copies(u_ref, jnp.concatenate([jnp.where(i == 0, 0.0, halo), v_ref[...] * _sigmoid(g_ref[...])], axis=0))
        acc = jnp.zeros((tm, cw), f32) + b_ref[...]
        for j in range(ktaps):
            acc = acc + _window(u_ref, lead + j, tm) * w_ref[j:j + 1, :]
        o_ref[...] = acc

    def tile(off):
        return pl.BlockSpec((tm, cw), lambda j, i: (i, off // cw + j))

    def halo(off):
        return pl.BlockSpec((CONV_HALO, cw), lambda j, i: (jnp.maximum(i * hb - 1, 0), off // cw + j))

    return pl.pallas_call(
        body, name="conv_fwd", grid=(ncol, T // tm),
        in_specs=[pl.BlockSpec((CONV_HALO, cw), lambda j, i: (0, j)), pl.BlockSpec((1, cw), lambda j, i: (0, j)),
                  tile(off_v), tile(off_g), halo(off_v), halo(off_g)],
        out_specs=pl.BlockSpec((tm, cw), lambda j, i: (i, j)),
        out_shape=jax.ShapeDtypeStruct((T, width), f32),
        scratch_shapes=[pltpu.VMEM((SUBLANES, CONV_HALO + tm, cw), f32)],
        compiler_params=_cparams(2),
    )(conv_w, conv_b, proj, proj, proj, proj)


def _conv_bwd(proj, off_v, off_g, width, conv_w, dc, ktaps, T, tm):
    cw = _tile(width, 512)
    ncol = width // cw
    hb, nt = tm // CONV_HALO, T // tm
    lead = CONV_HALO - (ktaps - 1)

    def body(w_ref, v_ref, g_ref, hv_ref, hg_ref, dc_ref, dcn_ref, dv_ref, dg_ref, dw_ref, db_ref, u_ref, d_ref):
        i = pl.program_id(1)
        halo = hv_ref[...] * _sigmoid(hg_ref[...])
        sig = _sigmoid(g_ref[...])
        gv = v_ref[...]
        _rolled_copies(u_ref, jnp.concatenate([jnp.where(i == 0, 0.0, halo), gv * sig], axis=0))
        dcur = dc_ref[...]
        _rolled_copies(d_ref, jnp.concatenate([dcur, jnp.where(i == nt - 1, 0.0, dcn_ref[...])], axis=0))

        @pl.when(i == 0)
        def _():
            dw_ref[...] = jnp.zeros_like(dw_ref)
            db_ref[...] = jnp.zeros_like(db_ref)

        du = jnp.zeros((tm, cw), f32)
        for j in range(ktaps):
            du = du + _window(d_ref, ktaps - 1 - j, tm) * w_ref[j:j + 1, :]
            dw_ref[j:j + 1, :] += jnp.sum(_window(u_ref, lead + j, tm) * dcur, axis=0, keepdims=True)
        db_ref[...] += jnp.sum(dcur, axis=0, keepdims=True)
        dv_ref[...] = (du * sig).astype(dv_ref.dtype)
        dg_ref[...] = (du * gv * sig * (1.0 - sig)).astype(dg_ref.dtype)

    def tile(off):
        return pl.BlockSpec((tm, cw), lambda j, i: (i, off // cw + j))

    def halo(off):
        return pl.BlockSpec((CONV_HALO, cw), lambda j, i: (jnp.maximum(i * hb - 1, 0), off // cw + j))

    return pl.pallas_call(
        body, name="conv_bwd", grid=(ncol, nt),
        in_specs=[pl.BlockSpec((CONV_HALO, cw), lambda j, i: (0, j)),
                  tile(off_v), tile(off_g), halo(off_v), halo(off_g),
                  pl.BlockSpec((tm, cw), lambda j, i: (i, j)),
                  pl.BlockSpec((CONV_HALO, cw), lambda j, i: (jnp.minimum((i + 1) * hb, nt * hb - 1), j))],
        out_specs=[pl.BlockSpec((tm, cw), lambda j, i: (i, j)), pl.BlockSpec((tm, cw), lambda j, i: (i, j)),
                   pl.BlockSpec((CONV_HALO, cw), lambda j, i: (0, j)), pl.BlockSpec((1, cw), lambda j, i: (0, j))],
        out_shape=[jax.ShapeDtypeStruct((T, width), bf16), jax.ShapeDtypeStruct((T, width), bf16),
                   jax.ShapeDtypeStruct((CONV_HALO, width), f32), jax.ShapeDtypeStruct((1, width), f32)],
        scratch_shapes=[pltpu.VMEM((SUBLANES, CONV_HALO + tm, cw), f32), pltpu.VMEM((SUBLANES, tm + CONV_HALO, cw), f32)],
        compiler_params=_cparams(2),
    )(conv_w, proj, proj, proj, proj, dc, dc)


def _each(f, *lists):
    return [f(*xs) for xs in zip(*lists)]


def _wkv_local(r, lw, k, v, a, b):
    C = r[0].shape[0]
    P = WKV_PREC
    row = lax.broadcasted_iota(jnp.int32, (C, C), 0)
    col = lax.broadcasted_iota(jnp.int32, (C, C), 1)
    incl, strict = row >= col, row > col
    tri = incl.astype(f32)
    zero = jnp.zeros((C, C), f32)
    G = _each(lambda x: _mm(tri, x, P), lw)
    to_end = _each(lambda x, g: jnp.exp(jnp.sum(x, axis=0, keepdims=True) - g), lw, G)
    e_g = _each(jnp.exp, G)
    e_ng = _each(lambda g: jnp.exp(-g), G)
    At = _each(lambda x, g, w: x * jnp.exp(g - w), a, G, lw)
    Rt = _each(jnp.multiply, r, e_g)
    Kt = _each(jnp.multiply, k, e_ng)
    Bt = _each(jnp.multiply, b, e_ng)
    sc = _each(lambda at, rt, bt, kt: _mm_nt(jnp.concatenate([at, rt], axis=0), jnp.concatenate([bt, kt], axis=0), P),
               At, Rt, Bt, Kt)
    L = _each(lambda s: jnp.where(strict, s[:C, :C], zero), sc)
    M = _each(lambda s: jnp.where(strict, s[:C, C:], zero), sc)
    Pb = _each(lambda s: jnp.where(incl, s[C:, :C], zero), sc)
    Pk = _each(lambda s: jnp.where(incl, s[C:, C:], zero), sc)
    MPk = _each(lambda m, p, x: _bmm(jnp.concatenate([m, p], axis=0), x), M, Pk, v)
    WU = _each(lambda at, mp: jnp.concatenate([at, mp[:C]], axis=1), At, MPk)
    Lp = L
    n = 1
    while n < C:
        n *= 2
        if n < C:
            step = _each(lambda l, x: _bmm(l, jnp.concatenate([x, l], axis=1)), Lp, WU)
            WU = _each(lambda x, s: x + s[:, :x.shape[1]], WU, step)
            Lp = _each(lambda x, s: s[:, x.shape[1]:], WU, step)
        else:
            WU = _each(lambda x, l: x + _bmm(l, x), WU, Lp)
    N = r[0].shape[1]
    W = _each(lambda x: x[:, :N], WU)
    U = _each(lambda x: x[:, N:], WU)
    Y0 = _each(lambda mp: mp[C:], MPk)
    Bend = _each(jnp.multiply, b, to_end)
    Z = _each(lambda x, y, e: _bmm_tn(x, y * e), v, k, to_end)
    return W, U, Rt, Pb, Y0, Bend, Z


def _wkv_state(S0, W, U, Rt, Pb, Bend, lw, Y0, Z):
    C = W[0].shape[0]
    WR = _each(lambda w, rt, s: _bmm_nt(jnp.concatenate([w, rt], axis=0), s), W, Rt, S0)
    X = _each(lambda wr, u: wr[:C] + u, WR, U)
    y = _each(lambda p, x, wr, c: _bmm(p, x) + wr[C:] + c, Pb, X, WR, Y0)
    S1 = _each(lambda s, w, x, e, z: s * jnp.exp(jnp.sum(w, axis=0, keepdims=True)) + _bmm_tn(x, e) + z,
               S0, lw, X, Bend, Z)
    return y, S1


def _wkv_dims(head, T, RW, heads_per_step):
    C = min(WKV_CHUNK, T)
    nh = RW // head
    hb = min(heads_per_step, nh)
    return C, nh, hb, hb * head, T // C


def _heads(ref, hb, head):
    return [ref[:, h * head:(h + 1) * head] for h in range(hb)]


def _put_heads(ref, vals, head):
    for h, val in enumerate(vals):
        ref[:, h * head:(h + 1) * head] = val


def _wkv_local_fwd(r, lw, k, v, a, b, head, T, RW):
    C, nh, hb, bw, nc = _wkv_dims(head, T, RW, WKV_HEADS)

    def body(*refs):
        ins, outs = refs[:6], refs[6:]
        res = _wkv_local(*[_heads(x, hb, head) for x in ins])
        for o_ref, vals in zip(outs[:6], res[:6]):
            _put_heads(o_ref, vals, head)
        for h in range(hb):
            outs[6][0, h] = res[6][h]

    blk = pl.BlockSpec((C, bw), lambda g, c: (c, g))
    sq = pl.BlockSpec((1, hb, head, head), lambda g, c: (c, g, 0, 0))
    return pl.pallas_call(
        body, name="wkv_local", grid=(nh // hb, nc),
        in_specs=[blk] * 6, out_specs=[blk] * 6 + [sq],
        out_shape=[jax.ShapeDtypeStruct((T, RW), f32)] * 6 + [jax.ShapeDtypeStruct((nc, nh, head, head), f32)],
        compiler_params=_cparams(2),
    )(r, lw, k, v, a, b)


def _wkv_state_fwd(W, U, Rt, Pb, Bend, lw, Y0, Z, head, T, RW):
    C, nh, hb, bw, nc = _wkv_dims(head, T, RW, WKV_STATE_HEADS)

    def body(w_ref, u_ref, rt_ref, pb_ref, be_ref, lw_ref, y0_ref, z_ref, y_ref, st_ref, s_ref):
        @pl.when(pl.program_id(1) == 0)
        def _():
            s_ref[...] = jnp.zeros_like(s_ref)

        S0 = [s_ref[h] for h in range(hb)]
        for h in range(hb):
            st_ref[0, h] = S0[h]
        rows = [_heads(x, hb, head) for x in (w_ref, u_ref, rt_ref, pb_ref, be_ref, lw_ref, y0_ref)]
        y, S1 = _wkv_state(S0, *rows, [z_ref[0, h] for h in range(hb)])
        _put_heads(y_ref, y, head)
        for h in range(hb):
            s_ref[h] = S1[h]

    blk = pl.BlockSpec((C, bw), lambda g, c: (c, g))
    sq = pl.BlockSpec((1, hb, head, head), lambda g, c: (c, g, 0, 0))
    return pl.pallas_call(
        body, name="wkv_state", grid=(nh // hb, nc),
        in_specs=[blk] * 7 + [sq], out_specs=[blk, sq],
        out_shape=[jax.ShapeDtypeStruct((T, RW), f32), jax.ShapeDtypeStruct((nc, nh, head, head), f32)],
        scratch_shapes=[pltpu.VMEM((hb, head, head), f32)],
        compiler_params=_cparams(2),
    )(W, U, Rt, Pb, Bend, lw, Y0, Z)


def _wkv_state_bwd(W, U, Rt, Pb, Bend, lw, Y0, Z, states, dy, head, T, RW):
    C, nh, hb, bw, nc = _wkv_dims(head, T, RW, WKV_STATE_HEADS)

    def body(w_ref, u_ref, rt_ref, pb_ref, be_ref, lw_ref, y0_ref, z_ref, st_ref, dy_ref,
             dw_ref, du_ref, drt_ref, dpb_ref, dbe_ref, dlw_ref, dz_ref, ds_ref):
        @pl.when(pl.program_id(1) == 0)
        def _():
            ds_ref[...] = jnp.zeros_like(ds_ref)

        dS1 = [ds_ref[h] for h in range(hb)]
        for h in range(hb):
            dz_ref[0, h] = dS1[h]
        rows = [_heads(x, hb, head) for x in (w_ref, u_ref, rt_ref, pb_ref, be_ref, lw_ref)]
        Y0 = _heads(y0_ref, hb, head)
        Zs = [z_ref[0, h] for h in range(hb)]
        _, vjp = jax.vjp(lambda s0, *rw: _wkv_state(s0, *rw, Y0, Zs), [st_ref[0, h] for h in range(hb)], *rows)
        grads = vjp((_heads(dy_ref, hb, head), dS1))
        for o_ref, vals in zip((dw_ref, du_ref, drt_ref, dpb_ref, dbe_ref, dlw_ref), grads[1:]):
            _put_heads(o_ref, vals, head)
        for h in range(hb):
            ds_ref[h] = grads[0][h]

    blk = pl.BlockSpec((C, bw), lambda g, c: (nc - 1 - c, g))
    sq = pl.BlockSpec((1, hb, head, head), lambda g, c: (nc - 1 - c, g, 0, 0))
    return pl.pallas_call(
        body, name="wkv_state_bwd", grid=(nh // hb, nc),
        in_specs=[blk] * 7 + [sq, sq, blk], out_specs=[blk] * 6 + [sq],
        out_shape=[jax.ShapeDtypeStruct((T, RW), f32)] * 6 + [jax.ShapeDtypeStruct((nc, nh, head, head), f32)],
        scratch_shapes=[pltpu.VMEM((hb, head, head), f32)],
        compiler_params=_cparams(2),
    )(W, U, Rt, Pb, Bend, lw, Y0, Z, states, dy)


def _wkv_local_bwd(r, lw, k, v, a, b, cots, d_lw_x, dr_x, dk_x, dv_x, head, T, RW):
    C, nh, hb, bw, nc = _wkv_dims(head, T, RW, WKV_HEADS)

    def body(*refs):
        ins, cot_refs, add_refs, outs = refs[:6], refs[6:13], refs[13:17], refs[17:]
        _, vjp = jax.vjp(_wkv_local, *[_heads(x, hb, head) for x in ins])
        cts = [_heads(x, hb, head) for x in cot_refs[:6]] + [[cot_refs[6][0, h] for h in range(hb)]]
        dr, dlw, dk, dv, da, db = vjp(tuple(cts))
        dlw_x, drx, dkx, dvx = [_heads(x, hb, head) for x in add_refs]
        _put_heads(outs[0], _each(jnp.add, dr, drx), head)
        _put_heads(outs[1], _each(jnp.add, dlw, dlw_x), head)
        _put_heads(outs[2], _each(jnp.add, dk, dkx), head)
        _put_heads(outs[3], _each(jnp.add, dv, dvx), head)
        _put_heads(outs[4], da, head)
        _put_heads(outs[5], db, head)

    blk = pl.BlockSpec((C, bw), lambda g, c: (c, g))
    sq = pl.BlockSpec((1, hb, head, head), lambda g, c: (c, g, 0, 0))
    return pl.pallas_call(
        body, name="wkv_local_bwd", grid=(nh // hb, nc),
        in_specs=[blk] * 12 + [sq] + [blk] * 4, out_specs=[blk] * 6,
        out_shape=[jax.ShapeDtypeStruct((T, RW), f32)] * 6,
        compiler_params=_cparams(2),
    )(r, lw, k, v, a, b, *cots, d_lw_x, dr_x, dk_x, dv_x)


def _rows_tile(R, row_bytes, budget, mult=SUBLANES):
    best = None
    t = mult
    while t <= R:
        if R % t == 0 and t * row_bytes <= budget:
            best = t
        t += mult
    return best if best is not None else R


def _sum_slots(name, parts):
    S, R, W = parts.shape
    budget = 4 << 20
    tr = _rows_tile(R, S * W * 4, budget, 2 * SUBLANES)
    cw = W if tr * S * W * 4 <= 2 * budget else _tile(W, max(LANES, 2 * budget // (S * tr * 4)))

    def body(p_ref, o_ref):
        acc = p_ref[0].astype(f32)
        for d in range(1, S):
            acc = acc + p_ref[d].astype(f32)
        o_ref[...] = acc

    return pl.pallas_call(
        body, name=name, grid=(R // tr, W // cw),
        in_specs=[pl.BlockSpec((S, tr, cw), lambda i, j: (0, i, j))],
        out_specs=pl.BlockSpec((tr, cw), lambda i, j: (i, j)),
        out_shape=jax.ShapeDtypeStruct((R, W), f32),
        compiler_params=_cparams(2),
    )(parts)


def _tile2d(R, W, budget, mult):
    tr = _rows_tile(R, LANES * 4, budget, mult)
    cw = _tile(W, max(LANES, budget // (tr * 4))) if W % LANES == 0 else W
    return tr, cw


def _core_index():
    return lax.axis_index("c").astype(jnp.int32).reshape(1)


def _add_kept(name, shards, got):
    _, _, R, W = shards.shape
    tr, cw = _tile2d(R, W, 2 << 20, 2 * SUBLANES)

    def body(core_ref, a_ref, b_ref, o_ref):
        o_ref[...] = (a_ref[...].astype(f32) + b_ref[...].astype(f32)).astype(o_ref.dtype)

    return pl.pallas_call(
        body, name=name,
        grid_spec=pltpu.PrefetchScalarGridSpec(
            num_scalar_prefetch=1, grid=(4, R // tr, W // cw),
            in_specs=[pl.BlockSpec((None, None, tr, cw), lambda s, i, j, core: (s, core[0], i, j)),
                      pl.BlockSpec((None, tr, cw), lambda s, i, j, core: (s, i, j))],
            out_specs=pl.BlockSpec((None, tr, cw), lambda s, i, j, core: (s, i, j))),
        out_shape=jax.ShapeDtypeStruct((4, R, W), shards.dtype), compiler_params=_cparams(3),
    )(_core_index(), shards, got)


def _sum_slots_half(name, parts):
    S, R, W = parts.shape
    budget = 4 << 20
    tr = _rows_tile(R, S * W * 4, budget, 2 * SUBLANES)
    cw = W if tr * S * W * 4 <= 2 * budget else _tile(W, max(LANES, 2 * budget // (S * tr * 4)))
    nrow = R // tr

    def body(core_ref, p_ref, o_ref):
        acc = p_ref[0].astype(f32)
        for d in range(1, S):
            acc = acc + p_ref[d].astype(f32)
        o_ref[...] = acc

    return pl.pallas_call(
        body, name=name,
        grid_spec=pltpu.PrefetchScalarGridSpec(
            num_scalar_prefetch=1, grid=(nrow, W // cw),
            in_specs=[pl.BlockSpec((S, tr, cw), lambda i, j, core: (0, i, j))],
            out_specs=pl.BlockSpec((tr, cw), lambda i, j, core: (core[0] * nrow + i, j))),
        out_shape=jax.ShapeDtypeStruct((2 * R, W), f32), compiler_params=_cparams(2),
    )(_core_index(), parts)


def _adamw(name, w, g, m, v):
    R, W = w.shape
    tr, cw = _tile2d(R, W, 2 << 20, SUBLANES)

    def body(w_ref, g_ref, m_ref, v_ref, d_ref, nm_ref, nv_ref):
        g_v = g_ref[...]
        nm = ADAM_B1 * m_ref[...] + (1.0 - ADAM_B1) * g_v
        nv = ADAM_B2 * v_ref[...] + (1.0 - ADAM_B2) * (g_v * g_v)
        m_hat = nm / (1.0 - ADAM_B1 ** ADAM_STEP)
        v_hat = nv / (1.0 - ADAM_B2 ** ADAM_STEP)
        d_ref[...] = -ADAM_LR * (m_hat / (jnp.sqrt(v_hat) + ADAM_EPS) + ADAM_WD * w_ref[...])
        nm_ref[...] = nm
        nv_ref[...] = nv

    blk = pl.BlockSpec((tr, cw), lambda i, j: (i, j))
    return pl.pallas_call(
        body, name=name, grid=(R // tr, W // cw),
        in_specs=[blk] * 4, out_specs=[blk] * 3,
        out_shape=[jax.ShapeDtypeStruct((R, W), f32)] * 3,
        compiler_params=_cparams(2),
    )(w, g, m, v)


ANY = pl.BlockSpec(memory_space=pl.ANY)


def _place():
    return lax.axis_index("x"), lax.axis_index("y"), lax.axis_index("c")


class _Comm:
    def __init__(self, operands, out_shape, scratch, start, wait):
        self.operands, self.out_shape, self.scratch, self.start, self.wait = operands, out_shape, scratch, start, wait


def _run_comm(name, comm):
    n = len(comm.operands)

    def body(*refs):
        parts = (refs[:n], refs[n:2 * n], refs[2 * n:])
        comm.start(*parts)
        comm.wait(*parts)

    return pl.pallas_call(
        body, name=name, in_specs=[ANY] * n, out_specs=[ANY] * n, out_shape=comm.out_shape,
        scratch_shapes=comm.scratch,
    )(*comm.operands)


def _copy_chunks(rows, cols):
    k = SHARE_CHUNKS // 2
    if rows % (k * 2 * SUBLANES) == 0:
        return [(pl.ds(q * (rows // k), rows // k), pl.ds(0, cols)) for q in range(k)]
    if cols % (k * LANES) == 0:
        return [(pl.ds(0, rows), pl.ds(q * (cols // k), cols // k)) for q in range(k)]
    return [(pl.ds(0, rows), pl.ds(0, cols))]


def _gather_chips(arrays, relayed=False):
    n = len(arrays)
    assert not relayed or all(a.shape[2] % (2 * LANES) == 0 for a in arrays)
    parts = [(a, h, blk) for a, arr in enumerate(arrays) for h in range(2) for blk in _copy_chunks(*arr.shape[1:])]

    def copies(ins, outs, sems):
        send_sems, recv_sems, local_sems = sems
        x, y, c = _place()
        mine = 2 * x + y
        sib = (x, y, 1 - c)
        chips = [(1 - x, y), (x, 1 - y), (1 - x, 1 - y)]

        def local():
            return [pltpu.make_async_copy(ins[a].at[(h, *blk)], outs[a].at[(mine, h, *blk)], local_sems.at[p])
                    for p, (a, h, blk) in enumerate(parts)]

        def over_ici(a, j, slot):
            px, py = chips[j]
            return pltpu.make_async_remote_copy(
                src_ref=ins[a].at[c], dst_ref=outs[a].at[slot, c], send_sem=send_sems.at[3 * a + j],
                recv_sem=recv_sems.at[3 * a + j], device_id=(px, py, c), device_id_type=MESH)

        def over_d2d(a, j, half):
            px, py = chips[j]
            slot = 2 * px + py
            return pltpu.make_async_remote_copy(
                src_ref=outs[a].at[slot, half], dst_ref=outs[a].at[slot, half], send_sem=send_sems.at[3 * (n + a) + j],
                recv_sem=recv_sems.at[3 * (n + a) + j], device_id=sib, device_id_type=MESH)

        def relay(a, q, origin):
            ox, oy = origin
            px, py = chips[1 - q]
            rows, cols = arrays[a].shape[1], arrays[a].shape[2] // 2
            win = outs[a].at[2 * ox + oy, c, pl.ds(0, rows), pl.ds(q * cols, cols)]
            return pltpu.make_async_remote_copy(
                src_ref=win, dst_ref=win, send_sem=send_sems.at[6 * n + 2 * a + q],
                recv_sem=recv_sems.at[6 * n + 2 * a + q], device_id=(px, py, c), device_id_type=MESH)

        direct = 2 if relayed else 3
        pairs = [(a, j) for a in range(n) for j in range(direct)]
        return dict(local=local,
                    sends=lambda: [over_ici(a, j, mine) for a, j in pairs],
                    landing=lambda: [over_ici(a, j, 2 * chips[j][0] + chips[j][1]) for a, j in pairs],
                    passed=lambda: [over_d2d(a, j, c) for a, j in pairs],
                    relays=lambda: [relay(a, j, chips[j]) for a, j in pairs],
                    relayed_in=lambda: [relay(a, q, chips[2]) for a in range(n) for q in range(2)],
                    passed_diag=lambda: [over_d2d(a, 2, c) for a in range(n)],
                    from_sib=lambda: [over_d2d(a, j, 1 - c) for a in range(n) for j in range(3)])

    def start(ins, outs, sems):
        cps = copies(ins, outs, sems)
        for cp in cps["local"]() + cps["sends"]():
            cp.start()

    def wait(ins, outs, sems):
        cps = copies(ins, outs, sems)
        passed = cps["passed"]()
        relays = cps["relays"]() if relayed else [None] * len(passed)
        for got, on, via in zip(cps["landing"](), passed, relays):
            got.wait_recv()
            if relayed:
                via.start()
            on.start()
        if relayed:
            for cp in cps["relayed_in"]():
                cp.wait_recv()
            diag = cps["passed_diag"]()
            for cp in diag:
                cp.start()
            passed = passed + diag + relays
        for cp in cps["from_sib"]():
            cp.wait_recv()
        for cp in cps["sends"]() + passed:
            cp.wait_send()
        for cp in cps["local"]():
            cp.wait()

    return _Comm(arrays, [jax.ShapeDtypeStruct((4,) + a.shape, a.dtype) for a in arrays],
                 [pltpu.SemaphoreType.DMA((8 * n,)), pltpu.SemaphoreType.DMA((8 * n,)),
                  pltpu.SemaphoreType.DMA((len(parts),))], start, wait)


def _exchange_chips(pieces, whole):
    n, m = len(pieces), len(whole)
    parts = [(a, blk) for a, arr in enumerate(pieces) for blk in _copy_chunks(*arr.shape[1:])]

    def copies(ins, outs, sems):
        send_sems, recv_sems, local_sems = sems
        x, y, c = _place()
        chip, dev = 2 * x + y, 4 * x + 2 * y + c
        chips = [(1 - x, y), (x, 1 - y), (1 - x, 1 - y)]
        peers = [(x ^ (k >> 2), y ^ ((k >> 1) & 1), c ^ (k & 1)) for k in range(1, 8)]
        def local():
            cps = [pltpu.make_async_copy(ins[a].at[(chip, *blk)], outs[a].at[(chip, *blk)], local_sems.at[p])
                   for p, (a, blk) in enumerate(parts)]
            return cps + [pltpu.make_async_copy(ins[n + b], outs[n + b].at[dev], local_sems.at[len(parts) + b])
                          for b in range(m)]

        def piece(a, j, slot_from):
            px, py = chips[j]
            return pltpu.make_async_remote_copy(
                src_ref=ins[a].at[2 * px + py], dst_ref=outs[a].at[slot_from], send_sem=send_sems.at[3 * a + j],
                recv_sem=recv_sems.at[3 * a + j], device_id=(px, py, c), device_id_type=MESH)

        def everyone(b, j, slot_from):
            px, py, pc = peers[j]
            return pltpu.make_async_remote_copy(
                src_ref=ins[n + b], dst_ref=outs[n + b].at[slot_from], send_sem=send_sems.at[3 * n + 7 * b + j],
                recv_sem=recv_sems.at[3 * n + 7 * b + j], device_id=(px, py, pc), device_id_type=MESH)

        def sends():
            return ([everyone(b, j, dev) for b in range(m) for j in range(7)]
                    + [piece(a, j, chip) for a in range(n) for j in range(3)])

        def landing():
            return ([everyone(b, j, 4 * px + 2 * py + pc) for b in range(m) for j, (px, py, pc) in enumerate(peers)]
                    + [piece(a, j, 2 * px + py) for a in range(n) for j, (px, py) in enumerate(chips)])

        return local, sends, landing

    def start(ins, outs, sems):
        local, sends, _ = copies(ins, outs, sems)
        for cp in local() + sends():
            cp.start()

    def wait(ins, outs, sems):
        local, sends, landing = copies(ins, outs, sems)
        for cp in landing():
            cp.wait_recv()
        for cp in sends():
            cp.wait_send()
        for cp in local():
            cp.wait()

    shapes = [jax.ShapeDtypeStruct(a.shape, a.dtype) for a in pieces]
    shapes += [jax.ShapeDtypeStruct((8,) + a.shape, a.dtype) for a in whole]
    nsem = 3 * n + 7 * m
    return _Comm(list(pieces) + list(whole), shapes,
                 [pltpu.SemaphoreType.DMA((nsem,)), pltpu.SemaphoreType.DMA((nsem,)),
                  pltpu.SemaphoreType.DMA((len(parts) + m,))], start, wait)


def _pair_exchange(name, shards):
    n = len(shards)

    def body(*refs):
        ins, outs = refs[:n], refs[n:2 * n]
        send_sems, recv_sems = refs[2 * n:]
        x, y, c = _place()
        copies = [pltpu.make_async_remote_copy(
            src_ref=ins[a].at[s, 1 - c], dst_ref=outs[a].at[s], send_sem=send_sems.at[4 * a + s],
            recv_sem=recv_sems.at[4 * a + s], device_id=(x, y, 1 - c), device_id_type=MESH)
            for a in range(n) for s in range(4)]
        for cp in copies:
            cp.start()
        for cp in copies:
            cp.wait_recv()
        for cp in copies:
            cp.wait_send()

    return pl.pallas_call(
        body, name=name, in_specs=[ANY] * n, out_specs=[ANY] * n,
        out_shape=[jax.ShapeDtypeStruct((4,) + a.shape[2:], a.dtype) for a in shards],
        scratch_shapes=[pltpu.SemaphoreType.DMA((4 * n,)), pltpu.SemaphoreType.DMA((4 * n,))],
    )(*shards)


def _share_halves(name, arrays):
    n = len(arrays)

    def body(*refs):
        bufs = refs[n:2 * n]
        send_sems, recv_sems = refs[2 * n:]
        x, y, c = _place()

        def half(a, h):
            rows = arrays[a].shape[0] // 2
            return bufs[a].at[pl.ds(pl.multiple_of(h * rows, SUBLANES), rows)]

        def copy(a, h):
            return pltpu.make_async_remote_copy(
                src_ref=half(a, h), dst_ref=half(a, h), send_sem=send_sems.at[a], recv_sem=recv_sems.at[a],
                device_id=(x, y, 1 - c), device_id_type=MESH)

        sends = [copy(a, c) for a in range(n)]
        for cp in sends:
            cp.start()
        for a in range(n):
            copy(a, 1 - c).wait_recv()
        for cp in sends:
            cp.wait_send()

    return pl.pallas_call(
        body, name=name, in_specs=[ANY] * n, out_specs=[ANY] * n,
        out_shape=[jax.ShapeDtypeStruct(a.shape, a.dtype) for a in arrays],
        input_output_aliases={a: a for a in range(n)},
        scratch_shapes=[pltpu.SemaphoreType.DMA((n,)), pltpu.SemaphoreType.DMA((n,))],
    )(*arrays)


def _place_blocks(blocks, axis):
    shape = list(blocks[0].shape)
    shape[axis] = sum(b.shape[axis] for b in blocks)
    buf = lax.empty(tuple(shape), blocks[0].dtype)
    at = 0
    for b in blocks:
        buf = lax.dynamic_update_slice_in_dim(buf, b, at, axis)
        at += b.shape[axis]
    return buf


def kernel(x, norm_pre_g, w_in, mu_shift, w0, w_lora_up, a0, a_lora_up, k_k, k_a, r_k, lnx_g, lnx_b, conv_w, conv_b, cln_g, cln_b, w_pw2, b_pw2, w_out, norm_post_g, loss_target, m_norm_pre_g, m_w_in, m_mu_shift, m_w0, m_w_lora_up, m_a0, m_a_lora_up, m_k_k, m_k_a, m_r_k, m_lnx_g, m_lnx_b, m_conv_w, m_conv_b, m_cln_g, m_cln_b, m_w_pw2, m_b_pw2, m_w_out, m_norm_post_g, v_norm_pre_g, v_w_in, v_mu_shift, v_w0, v_w_lora_up, v_a0, v_a_lora_up, v_k_k, v_k_a, v_r_k, v_lnx_g, v_lnx_b, v_conv_w, v_conv_b, v_cln_g, v_cln_b, v_w_pw2, v_b_pw2, v_w_out, v_norm_post_g):
    _, T, D = x.shape
    RW = w0.shape[0]
    CW = conv_b.shape[0]
    head = r_k.shape[1]
    lora = w_lora_up.shape[0]
    ktaps = conv_w.shape[0]
    assert RW == CW and 2 * lora <= LORA_PAD and ktaps - 1 <= CONV_HALO
    n_in = 3 * RW + 2 * lora + RW + 3 * CW
    shard = n_in // 4
    PW = 7 * RW + LORA_PAD
    off_l = 7 * RW
    tm = min(256, T // 2)
    tm_wide = min(128, T // 2)
    tm_halo = min(512, T // 2)
    row = lambda vec: vec.reshape(1, -1)
    x2, tgt2 = x[0], loss_target[0]

    halves = lambda a: a.reshape(2, a.shape[0] // 2, a.shape[1])
    conv_w_p = jnp.concatenate([conv_w, jnp.zeros((CONV_HALO - ktaps, CW // 4), f32)], axis=0)
    w_in_t, m_w_in_t, v_w_in_t = w_in.T, m_w_in.T, v_w_in.T
    (g_win,) = _run_comm("gather_w_in", _gather_chips([halves(w_in_t.astype(bf16))], relayed=True))
    win_t = g_win.reshape(n_in, D)
    lo = 3 * RW
    wp_t = _place_blocks([win_t[:lo], win_t[lo + 2 * lora:], win_t[lo:lo + 2 * lora],
                          jnp.zeros((LORA_PAD - 2 * lora, D), bf16)], axis=0)
    npg = row(norm_pre_g)
    (h,) = _row_fwd("rms_pre", _fn_rms_pre, [(npg, False)], [(x2, 0, D, False)], [(D, bf16)], T, tm)
    others = [halves(a) for a in (w_lora_up, a_lora_up, conv_w_p, w_pw2.astype(bf16), w_out.astype(bf16))]
    proj, (g_wup, g_aup, g_cw, g_pw2, g_wout) = _matmul("proj", h, wp_t, "nt", f32, comm=_gather_chips(others))
    cat_cols = lambda g: jnp.concatenate([g[s].reshape(-1, g.shape[-1]) for s in range(4)], axis=1)
    wup_full, aup_full, cw_p = cat_cols(g_wup), cat_cols(g_aup), cat_cols(g_cw)
    zl = lambda n: jnp.zeros((n, RW), f32)
    wup_p = jnp.concatenate([wup_full, zl(LORA_PAD - lora)], axis=0)
    aup_p = jnp.concatenate([zl(lora), aup_full, zl(LORA_PAD - 2 * lora)], axis=0)
    pw2_full = g_pw2.reshape(CW, CW)
    wout_full = g_wout.reshape(RW + CW, D)
    mu_r, mu_k, mu_v = (row(mu_shift[s * RW:(s + 1) * RW]) for s in range(3))
    mu_l = row(jnp.concatenate([mu_shift[3 * RW:], jnp.zeros((LORA_PAD - 2 * lora,), f32)]))

    xs_r = _shift_fwd("shift_r", proj, 0, RW, mu_r, T, tm_halo)
    xs_k = _shift_fwd("shift_k", proj, RW, RW, mu_k, T, tm_halo)
    xs_v = _shift_fwd("shift_v", proj, 2 * RW, RW, mu_v, T, tm_halo)
    xs_l = _shift_fwd("shift_l", proj, off_l, LORA_PAD, mu_l, T, tm_halo)
    lora_params = [(row(w0), False), (wup_p, False), (row(a0), False), (aup_p, False)]
    qw, qa = _row_fwd("lora_up", _fn_lora, lora_params, [(xs_l, 0, LORA_PAD, False)], [(RW, f32), (RW, f32)], T, tm)
    ncol = RW // _tile(RW, 512)
    fn_pre = functools.partial(_fn_rwkv_pre, head)
    pre_params = [(row(k_k), True), (row(k_a), True)]
    pre_rows = [(xs_k, 0, RW, True), (qw, 0, RW, True), (qa, 0, RW, True)]
    lw, k_h, a_rec, b_rec = _row_fwd("rwkv_pre", fn_pre, pre_params, pre_rows, [(RW, f32)] * 4, T, tm, ncol)
    wkv_in = (xs_r, lw, k_h, xs_v, a_rec, b_rec)
    c_w, c_u, c_rt, c_pb, c_y0, c_bend, c_z = _wkv_local_fwd(*wkv_in, head, T, RW)
    wkv_loc = (c_w, c_u, c_rt, c_pb, c_bend, lw, c_y0, c_z)
    y_wkv, states = _wkv_state_fwd(*wkv_loc, head, T, RW)
    fn_post = functools.partial(_fn_rwkv_post, head)
    post_params = [(row(lnx_g), True), (row(lnx_b), True), (r_k.reshape(1, RW), True)]
    post_rows = [(y_wkv, 0, RW, True), (xs_r, 0, RW, True), (k_h, 0, RW, True), (xs_v, 0, RW, True),
                 (proj, 3 * RW, RW, True)]
    (y_rwkv,) = _row_fwd("rwkv_post", fn_post, post_params, post_rows, [(RW, bf16)], T, tm, ncol)

    c_pre = _conv_fwd(proj, 4 * RW, 5 * RW, CW, cw_p, row(conv_b), ktaps, T, tm_halo)
    ln_params = [(row(cln_g), False), (row(cln_b), False)]
    (c_act,) = _row_fwd("conv_ln", _fn_conv_ln, ln_params, [(c_pre, 0, CW, False)], [(CW, bf16)], T, tm)
    c2 = _matmul("pw2", c_act, pw2_full, "nn", f32)
    cpost_params = [(row(b_pw2), True)]
    cpost_rows = [(c2, 0, CW, True), (proj, 6 * RW, CW, True)]
    (y_conv,) = _row_fwd("conv_post", _fn_conv_post, cpost_params, cpost_rows, [(CW, bf16)], T, tm, ncol)

    mix = jnp.concatenate([y_rwkv, y_conv], axis=1)
    out = _matmul("out_proj", mix, wout_full, "nn", f32)
    d_out, gx_res, loss_part, g_npost = _post(out, x2, tgt2, row(norm_post_g), T, D, tm_wide)

    g_wout_full = _matmul("d_w_out", mix, d_out, "tn", bf16)
    d_mix = _matmul("d_mix", d_out, wout_full, "nt", f32)

    d_c2, d_gconv, g_bpw2 = _row_bwd("conv_post_bwd", _fn_conv_post, cpost_params, cpost_rows,
                                      [(d_mix, RW, CW, True)], [bf16, bf16], T, tm, ncol)
    g_pw2_full = _matmul("d_w_pw2", c_act, d_c2, "tn", bf16)
    d_cact = _matmul("d_c_act", d_c2, pw2_full, "nt", f32)
    d_cpre, g_clng, g_clnb = _row_bwd("conv_ln_bwd", _fn_conv_ln, ln_params, [(c_pre, 0, CW, False)],
                                      [(d_cact, 0, CW, False)], [f32], T, tm)
    d_gluv, d_glug, g_cw_p, g_cb = _conv_bwd(proj, 4 * RW, 5 * RW, CW, cw_p, d_cpre, ktaps, T, tm_halo)

    d_y, dr_x, dk_x, dv_x, d_grwkv, g_lnxg, g_lnxb, g_rk = _row_bwd(
        "rwkv_post_bwd", fn_post, post_params, post_rows, [(d_mix, 0, RW, True)], [f32, f32, f32, f32, bf16], T, tm, ncol)
    d_cw, d_cu, d_crt, d_cpb, d_cbend, d_lw_dec, d_cz = _wkv_state_bwd(*wkv_loc, states, d_y, head, T, RW)
    d_xr, d_lw, d_kh, d_xv, d_a, d_b = _wkv_local_bwd(
        *wkv_in, (d_cw, d_cu, d_crt, d_cpb, d_y, d_cbend, d_cz), d_lw_dec, dr_x, dk_x, dv_x, head, T, RW)
    pre_cots = [(d_lw, 0, RW, True), (d_kh, 0, RW, True), (d_a, 0, RW, True), (d_b, 0, RW, True)]
    d_xk, d_qw, d_qa, g_kk, g_ka = _row_bwd("rwkv_pre_bwd", fn_pre, pre_params, pre_rows, pre_cots, [f32, f32, f32],
                                            T, tm, ncol)
    d_xl, g_w0, g_wup_p, g_a0, g_aup_p = _row_bwd("lora_up_bwd", _fn_lora, lora_params, [(xs_l, 0, LORA_PAD, False)],
                                                  [(d_qw, 0, RW, False), (d_qa, 0, RW, False)], [f32], T, tm)
    dp_r, g_mur = _shift_bwd("shift_r_bwd", proj, 0, RW, mu_r, d_xr, T, tm_halo)
    dp_k, g_muk = _shift_bwd("shift_k_bwd", proj, RW, RW, mu_k, d_xk, T, tm_halo)
    dp_v, g_muv = _shift_bwd("shift_v_bwd", proj, 2 * RW, RW, mu_v, d_xv, T, tm_halo)
    dp_l, g_mul = _shift_bwd("shift_l_bwd", proj, off_l, LORA_PAD, mu_l, d_xl, T, tm_halo)
    d_proj = _place_blocks([dp_r, dp_k, dp_v, d_grwkv, d_gluv, d_glug, d_gconv, dp_l], axis=1)

    def chip_sums(tag, shards):
        halves4 = [a.reshape(4, 2, a.shape[1] // 2, a.shape[2]) for a in shards]
        got = _pair_exchange("pair_exchange_" + tag, halves4)
        return [_add_kept("chip_sum_%s_%d" % (tag, i), a, g) for i, (a, g) in enumerate(zip(halves4, got))]

    def all_chips(tag, slots):
        return _share_halves("share_" + tag, [_sum_slots_half("sum_%s_%d" % (tag, i), r) for i, r in enumerate(slots)])

    q_early = chip_sums("early", [g_wout_full.reshape(4, (RW + CW) // 4, D), g_pw2_full.reshape(4, CW // 4, CW)])
    g_wp_t, r_early = _matmul("d_w_in", d_proj, h, "tn", bf16, comm=_exchange_chips(q_early, []))
    g_win_t = _place_blocks([g_wp_t[:lo], g_wp_t[off_l:off_l + 2 * lora], g_wp_t[lo:off_l]], axis=0)
    col_shards = lambda a: a.reshape(a.shape[0], 4, a.shape[1] // 4).transpose(1, 0, 2)
    q_late = chip_sums("late", [g_win_t.reshape(4, shard, D), col_shards(g_wup_p[:lora]),
                                col_shards(g_aup_p[lora:2 * lora]), col_shards(g_cw_p)])
    g_mu = jnp.concatenate([g_mur[0], g_muk[0], g_muv[0], g_mul[0, :2 * lora]])
    pad_rows = lambda a, n: jnp.concatenate([a, jnp.zeros((n - a.shape[0], a.shape[1]), f32)], axis=0)
    n_mu = -(-mu_shift.shape[0] // RW)
    small_vecs = [pad_rows(jnp.pad(g_mu, (0, n_mu * RW - g_mu.shape[0])).reshape(n_mu, RW), n_mu),
                  g_w0, g_a0, g_kk, g_ka, g_rk, g_lnxg, g_lnxb, g_cb, g_clng, g_clnb, g_bpw2,
                  g_npost.reshape(D // RW, RW)]
    n_small = sum(a.shape[0] for a in small_vecs)
    n_small_pad = -(-n_small // (2 * SUBLANES)) * (2 * SUBLANES)
    small = pad_rows(jnp.concatenate(small_vecs, axis=0), n_small_pad)

    d_h, r_late = _matmul("d_h", d_proj, wp_t, "nn", bf16, tk_t=2560, comm=_exchange_chips(q_late, [small]))
    grad_x2, g_npre = _rms_pre_bwd(x2, npg, d_h, gx_res, T, D, tm_wide)
    (r_npre,) = _run_comm("exchange_norm_pre", _exchange_chips([], [g_npre.reshape(D // RW, RW)]))
    s_npre = _sum_slots("sum_norm_pre", r_npre)
    s_small = _sum_slots("sum_small", r_late[4])
    grad_w_out, grad_w_pw2 = all_chips("early", r_early)
    grad_w_in, grad_wup, grad_aup, grad_cw = all_chips("late", r_late[:4])

    pos = [0]

    def take(nrows):
        a = s_small[pos[0]:pos[0] + nrows]
        pos[0] += nrows
        return a

    grads = {}
    grads["norm_pre_g"] = s_npre.reshape(D)
    grads["mu_shift"] = take(n_mu).reshape(-1)[:mu_shift.shape[0]]
    for nm in ["w0", "a0", "k_k", "k_a"]:
        grads[nm] = take(1).reshape(RW)
    grads["r_k"] = take(1).reshape(r_k.shape)
    for nm in ["lnx_g", "lnx_b", "conv_b", "cln_g", "cln_b", "b_pw2"]:
        grads[nm] = take(1).reshape(RW)
    grads["norm_post_g"] = take(D // RW).reshape(D)
    grads["w_lora_up"], grads["a_lora_up"], grads["conv_w"] = grad_wup, grad_aup, grad_cw[:ktaps]
    grads["w_in"], grads["w_out"], grads["w_pw2"] = grad_w_in, grad_w_out, grad_w_pw2

    weights = dict(norm_pre_g=norm_pre_g, w_in=w_in, mu_shift=mu_shift, w0=w0, w_lora_up=w_lora_up, a0=a0,
                   a_lora_up=a_lora_up, k_k=k_k, k_a=k_a, r_k=r_k, lnx_g=lnx_g, lnx_b=lnx_b, conv_w=conv_w,
                   conv_b=conv_b, cln_g=cln_g, cln_b=cln_b, w_pw2=w_pw2, b_pw2=b_pw2, w_out=w_out,
                   norm_post_g=norm_post_g)
    ms = dict(norm_pre_g=m_norm_pre_g, w_in=m_w_in, mu_shift=m_mu_shift, w0=m_w0, w_lora_up=m_w_lora_up, a0=m_a0,
              a_lora_up=m_a_lora_up, k_k=m_k_k, k_a=m_k_a, r_k=m_r_k, lnx_g=m_lnx_g, lnx_b=m_lnx_b, conv_w=m_conv_w,
              conv_b=m_conv_b, cln_g=m_cln_g, cln_b=m_cln_b, w_pw2=m_w_pw2, b_pw2=m_b_pw2, w_out=m_w_out,
              norm_post_g=m_norm_post_g)
    vs = dict(norm_pre_g=v_norm_pre_g, w_in=v_w_in, mu_shift=v_mu_shift, w0=v_w0, w_lora_up=v_w_lora_up, a0=v_a0,
              a_lora_up=v_a_lora_up, k_k=v_k_k, k_a=v_k_a, r_k=v_r_k, lnx_g=v_lnx_g, lnx_b=v_lnx_b, conv_w=v_conv_w,
              conv_b=v_conv_b, cln_g=v_cln_g, cln_b=v_cln_b, w_pw2=v_w_pw2, b_pw2=v_b_pw2, w_out=v_w_out,
              norm_post_g=v_norm_post_g)
    names = list(weights)
    big = ["w_in", "w_out", "w_pw2"]
    deltas, new_m, new_v = {}, {}, {}
    d_t, m_t, v_t = _adamw("adamw_w_in", w_in_t, grad_w_in, m_w_in_t, v_w_in_t)
    grads["w_in"], deltas["w_in"], new_m["w_in"], new_v["w_in"] = grad_w_in.T, d_t.T, m_t.T, v_t.T
    for nm in big[1:]:
        deltas[nm], new_m[nm], new_v[nm] = _adamw("adamw_" + nm, weights[nm], grads[nm], ms[nm], vs[nm])
    rest = [nm for nm in names if nm not in big]
    sizes = [weights[nm].size for nm in rest]
    total = sum(sizes)
    width = 4 * LANES
    rows_p = -(-total // (width * SUBLANES)) * SUBLANES

    def pack(d):
        flat = jnp.concatenate([d[nm].reshape(-1) for nm in rest])
        return jnp.pad(flat, (0, rows_p * width - total)).reshape(rows_p, width)

    p_d, p_m, p_v = _adamw("adamw_small", pack(weights), pack(grads), pack(ms), pack(vs))
    o = 0
    for nm, sz in zip(rest, sizes):
        shp = weights[nm].shape
        deltas[nm] = p_d.reshape(-1)[o:o + sz].reshape(shp)
        new_m[nm] = p_m.reshape(-1)[o:o + sz].reshape(shp)
        new_v[nm] = p_v.reshape(-1)[o:o + sz].reshape(shp)
        o += sz

    loss = lax.psum(loss_part[0, 0], ("x", "y", "c"))
    grad_x = grad_x2[None]
    return (loss, grad_x, *[grads[nm] for nm in names], *[deltas[nm] for nm in names],
            *[new_m[nm] for nm in names], *[new_v[nm] for nm in names])
```

```python
import functools

import jax
import jax.numpy as jnp
from jax import lax
from jax.experimental import pallas as pl
from jax.experimental.pallas import tpu as pltpu

f32 = jnp.float32
bf16 = jnp.bfloat16
MESH = pl.DeviceIdType.MESH

NORM_EPS = 1e-6
LN_EPS = 1e-5
ADAM_LR, ADAM_B1, ADAM_B2, ADAM_EPS, ADAM_WD, ADAM_STEP = 0.001, 0.9, 0.999, 1e-08, 0.01, 10

LANES = 128
SUBLANES = 8
LORA_PAD = 256
CONV_HALO = 32
WKV_CHUNK = 64
WKV_HEADS = 16
WKV_STATE_HEADS = 32
WKV_PREC = lax.Precision.HIGH
SHARE_CHUNKS = 8
VMEM_LIMIT = 56 * 1024 * 1024


def _cparams(n_axes):
    return pltpu.CompilerParams(dimension_semantics=("arbitrary",) * n_axes, vmem_limit_bytes=VMEM_LIMIT)


def _tile(dim, target):
    best = None
    t = LANES
    while t <= min(dim, target):
        if dim % t == 0:
            best = t
        t += LANES
    return best if best is not None else dim


def _mm(a, b, prec=None):
    return lax.dot_general(a, b, (((1,), (0,)), ((), ())), precision=prec, preferred_element_type=f32)


def _mm_nt(a, b, prec=None):
    return lax.dot_general(a, b, (((1,), (1,)), ((), ())), precision=prec, preferred_element_type=f32)


def _mm_tn(a, b, prec=None):
    return lax.dot_general(a, b, (((0,), (0,)), ((), ())), precision=prec, preferred_element_type=f32)


@jax.custom_vjp
def _bmm(a, b):
    return _mm(a.astype(bf16), b.astype(bf16))


def _bmm_fwd(a, b):
    return _bmm(a, b), (a, b)


def _bmm_bwd(res, dc):
    a, b = res
    dcb = dc.astype(bf16)
    return _mm_nt(dcb, b.astype(bf16)), _mm_tn(a.astype(bf16), dcb)


_bmm.defvjp(_bmm_fwd, _bmm_bwd)


@jax.custom_vjp
def _bmm_nt(a, b):
    return _mm_nt(a.astype(bf16), b.astype(bf16))


def _bmm_nt_fwd(a, b):
    return _bmm_nt(a, b), (a, b)


def _bmm_nt_bwd(res, dc):
    a, b = res
    dcb = dc.astype(bf16)
    return _mm(dcb, b.astype(bf16)), _mm_tn(dcb, a.astype(bf16))


_bmm_nt.defvjp(_bmm_nt_fwd, _bmm_nt_bwd)


@jax.custom_vjp
def _bmm_tn(a, b):
    return _mm_tn(a.astype(bf16), b.astype(bf16))


def _bmm_tn_fwd(a, b):
    return _bmm_tn(a, b), (a, b)


def _bmm_tn_bwd(res, dc):
    a, b = res
    dcb = dc.astype(bf16)
    return _mm_nt(b.astype(bf16), dcb), _mm(a.astype(bf16), dcb)


_bmm_tn.defvjp(_bmm_tn_fwd, _bmm_tn_bwd)


def _matmul(name, a, b, mode, out_dtype, tm_t=1024, tn_t=1024, tk_t=4096, comm=None):
    if mode == "nn":
        (M, K), (_, N) = a.shape, b.shape
    elif mode == "nt":
        (M, K), (N, _) = a.shape, b.shape
    else:
        (K, M), (_, N) = a.shape, b.shape
    tm, tn, tk = _tile(M, tm_t), _tile(N, tn_t), _tile(K, tk_t)
    ni, nj, nk = M // tm, N // tn, K // tk
    dot = {"nn": _mm, "nt": _mm_nt, "tn": _mm_tn}[mode]
    nc = len(comm.operands) if comm else 0

    def body(*refs):
        a_ref, b_ref = refs[:2]
        o_ref = refs[2 + nc]
        scratch = refs[3 + 2 * nc:]
        i, j, k = pl.program_id(0), pl.program_id(1), pl.program_id(2)
        if comm:
            comm_refs = (refs[2:2 + nc], refs[3 + nc:3 + 2 * nc], scratch[:len(comm.scratch)])

            @pl.when(jnp.logical_and(jnp.logical_and(i == 0, j == 0), k == 0))
            def _():
                comm.start(*comm_refs)

        if nk == 1:
            o_ref[...] = dot(a_ref[...], b_ref[...]).astype(o_ref.dtype)
        else:
            acc_ref = scratch[-1]

            @pl.when(k == 0)
            def _():
                acc_ref[...] = jnp.zeros_like(acc_ref)

            acc_ref[...] += dot(a_ref[...], b_ref[...])

            @pl.when(k == nk - 1)
            def _():
                o_ref[...] = acc_ref[...].astype(o_ref.dtype)

        if comm:
            @pl.when(jnp.logical_and(jnp.logical_and(i == ni - 1, j == nj - 1), k == nk - 1))
            def _():
                comm.wait(*comm_refs)

    a_spec = {"nn": pl.BlockSpec((tm, tk), lambda i, j, k: (i, k)),
              "nt": pl.BlockSpec((tm, tk), lambda i, j, k: (i, k)),
              "tn": pl.BlockSpec((tk, tm), lambda i, j, k: (k, i))}[mode]
    b_spec = {"nn": pl.BlockSpec((tk, tn), lambda i, j, k: (k, j)),
              "nt": pl.BlockSpec((tn, tk), lambda i, j, k: (j, k)),
              "tn": pl.BlockSpec((tk, tn), lambda i, j, k: (k, j))}[mode]
    res = pl.pallas_call(
        body, name=name, grid=(ni, nj, nk),
        in_specs=[a_spec, b_spec] + [ANY] * nc,
        out_specs=[pl.BlockSpec((tm, tn), lambda i, j, k: (i, j))] + [ANY] * nc,
        out_shape=[jax.ShapeDtypeStruct((M, N), out_dtype)] + (list(comm.out_shape) if comm else []),
        scratch_shapes=(list(comm.scratch) if comm else []) + ([pltpu.VMEM((tm, tn), f32)] if nk > 1 else []),
        compiler_params=_cparams(3),
    )(a, b, *(comm.operands if comm else []))
    return (res[0], res[1:]) if comm else res[0]


def _row_spec(op, tm, ncol):
    arr, off, width, tiled = op
    if tiled:
        cw = width // ncol
        return pl.BlockSpec((tm, cw), lambda j, i: (i, off // cw + j))
    return pl.BlockSpec((tm, width), lambda j, i: (i, off // width))


def _param_spec(p, ncol):
    arr, tiled = p
    rows, width = arr.shape
    if tiled:
        return pl.BlockSpec((rows, width // ncol), lambda j, i: (0, j))
    return pl.BlockSpec((rows, width), lambda j, i: (0, 0))


def _row_fwd(name, fn, params, rows, outs, T, tm, ncol=1):
    npar, nrow = len(params), len(rows)

    def body(*refs):
        pv = [r[...] for r in refs[:npar]]
        rv = [r[...].astype(f32) for r in refs[npar:npar + nrow]]
        res = fn(*pv, *rv)
        for o_ref, val in zip(refs[npar + nrow:], res):
            o_ref[...] = val.astype(o_ref.dtype)

    return pl.pallas_call(
        body, name=name, grid=(ncol, T // tm),
        in_specs=[_param_spec(p, ncol) for p in params] + [_row_spec(r, tm, ncol) for r in rows],
        out_specs=[pl.BlockSpec((tm, w // ncol), lambda j, i: (i, j)) for w, _ in outs],
        out_shape=[jax.ShapeDtypeStruct((T, w), dt) for w, dt in outs],
        compiler_params=_cparams(2),
    )(*[p[0] for p in params], *[r[0] for r in rows])


def _row_bwd(name, fn, params, rows, cots, row_grads, T, tm, ncol=1):
    npar, nrow, ncot = len(params), len(rows), len(cots)
    want = [k for k, dt in enumerate(row_grads) if dt is not None]

    def body(*refs):
        pv = [r[...] for r in refs[:npar]]
        rv = [r[...].astype(f32) for r in refs[npar:npar + nrow]]
        cv = tuple(r[...].astype(f32) for r in refs[npar + nrow:npar + nrow + ncot])
        out_refs = refs[npar + nrow + ncot:]
        _, vjp = jax.vjp(fn, *pv, *rv)
        grads = vjp(cv)
        for o_ref, k in zip(out_refs[:len(want)], want):
            o_ref[...] = grads[npar + k].astype(o_ref.dtype)
        j, i = pl.program_id(0), pl.program_id(1)
        for o_ref, p, g in zip(out_refs[len(want):], params, grads[:npar]):
            first = (i == 0) if p[1] else jnp.logical_and(i == 0, j == 0)

            @pl.when(first)
            def _():
                o_ref[...] = jnp.zeros_like(o_ref)

            o_ref[...] += g

    def grad_spec(op):
        arr, off, width, tiled = op
        if tiled:
            return pl.BlockSpec((tm, width // ncol), lambda j, i: (i, j)), (T, width)
        return pl.BlockSpec((tm, width), lambda j, i: (i, j)), (T, width * ncol)

    gspecs = [grad_spec(rows[k]) for k in want]
    return pl.pallas_call(
        body, name=name, grid=(ncol, T // tm),
        in_specs=[_param_spec(p, ncol) for p in params] + [_row_spec(r, tm, ncol) for r in rows]
        + [_row_spec(c, tm, ncol) for c in cots],
        out_specs=[s for s, _ in gspecs] + [_param_spec(p, ncol) for p in params],
        out_shape=[jax.ShapeDtypeStruct(shp, row_grads[k]) for (_, shp), k in zip(gspecs, want)]
        + [jax.ShapeDtypeStruct(p[0].shape, f32) for p in params],
        compiler_params=_cparams(2),
    )(*[p[0] for p in params], *[r[0] for r in rows], *[c[0] for c in cots])


def _seg_sum(x, head):
    li = lax.broadcasted_iota(jnp.int32, (LANES, LANES), 0) // head
    lj = lax.broadcasted_iota(jnp.int32, (LANES, LANES), 1) // head
    q = (li == lj).astype(f32)
    parts = [_mm(x[:, s:s + LANES], q, lax.Precision.HIGH) for s in range(0, x.shape[1], LANES)]
    return parts[0] if len(parts) == 1 else jnp.concatenate(parts, axis=1)


def _sigmoid(z):
    return 1.0 / (1.0 + jnp.exp(-z))


def _silu(z):
    return z * _sigmoid(z)


def _rms(g, x):
    return x * lax.rsqrt(jnp.mean(x * x, axis=-1, keepdims=True) + NORM_EPS) * g


def _fn_rms_pre(g, x):
    return (_rms(g, x),)


def _fn_lora(w0, wup, a0, aup, xl):
    qw = w0 + _bmm(jnp.tanh(xl), wup)
    qa = a0 + _bmm(xl, aup)
    return qw, qa


def _fn_rwkv_pre(head, k_k, k_a, xk, qw, qa):
    w_log = -(jnp.maximum(-qw, 0.0) + jnp.log(1.0 + jnp.exp(-jnp.abs(qw)))) - 0.5
    lw = -jnp.exp(w_log)
    a_sig = _sigmoid(qa)
    kk = xk * k_k
    kk = kk / jnp.maximum(jnp.sqrt(_seg_sum(kk * kk, head)), 1e-12)
    k_h = xk * (1.0 + (a_sig - 1.0) * k_a)
    return lw, k_h, -kk, kk * a_sig


def _fn_rwkv_post(head, lnx_g, lnx_b, r_k, y, r, k_h, v, g):
    inv = 1.0 / head
    mu = _seg_sum(y, head) * inv
    d = y - mu
    var = _seg_sum(d * d, head) * inv
    yn = d * lax.rsqrt(var + 1e-5 * head) * lnx_g + lnx_b
    bonus = _seg_sum(r * k_h * r_k, head) * v
    return ((yn + bonus) * _silu(g),)


def _fn_conv_ln(cln_g, cln_b, c):
    mu = jnp.mean(c, axis=-1, keepdims=True)
    d = c - mu
    var = jnp.mean(d * d, axis=-1, keepdims=True)
    return (_silu(d * lax.rsqrt(var + LN_EPS) * cln_g + cln_b),)


def _fn_conv_post(b_pw2, c2, g):
    return ((c2 + b_pw2) * _silu(g),)


def _post(out, x, tgt, g, T, D, tm):
    def body(g_ref, o_ref, x_ref, t_ref, dout_ref, gx_ref, loss_ref, dg_ref):
        i = pl.program_id(0)
        o, vjp = jax.vjp(_rms, g_ref[...], o_ref[...])
        err = x_ref[...] + o - t_ref[...]
        d_y = err * (1.0 / D)
        dg, d_out = vjp(d_y)
        dout_ref[...] = d_out.astype(dout_ref.dtype)
        gx_ref[...] = d_y

        @pl.when(i == 0)
        def _():
            loss_ref[...] = jnp.zeros_like(loss_ref)
            dg_ref[...] = jnp.zeros_like(dg_ref)

        loss_ref[...] += jnp.sum(err * err, keepdims=True) * (0.5 / D)
        dg_ref[...] += dg

    row = pl.BlockSpec((tm, D), lambda i: (i, 0))
    vec = pl.BlockSpec((1, D), lambda i: (0, 0))
    return pl.pallas_call(
        body, name="post_loss", grid=(T // tm,),
        in_specs=[vec, row, row, row],
        out_specs=[row, row, pl.BlockSpec((1, 1), lambda i: (0, 0)), vec],
        out_shape=[jax.ShapeDtypeStruct((T, D), bf16), jax.ShapeDtypeStruct((T, D), f32),
                   jax.ShapeDtypeStruct((1, 1), f32), jax.ShapeDtypeStruct((1, D), f32)],
        compiler_params=_cparams(1),
    )(g, out, x, tgt)


def _rms_pre_bwd(x, g, dh, gx_res, T, D, tm):
    def body(g_ref, x_ref, dh_ref, res_ref, dx_ref, dg_ref):
        i = pl.program_id(0)
        _, vjp = jax.vjp(_rms, g_ref[...], x_ref[...])
        dg, dx = vjp(dh_ref[...].astype(f32))
        dx_ref[...] = dx + res_ref[...]

        @pl.when(i == 0)
        def _():
            dg_ref[...] = jnp.zeros_like(dg_ref)

        dg_ref[...] += dg

    row = pl.BlockSpec((tm, D), lambda i: (i, 0))
    vec = pl.BlockSpec((1, D), lambda i: (0, 0))
    return pl.pallas_call(
        body, name="rms_pre_bwd", grid=(T // tm,),
        in_specs=[vec, row, row, row], out_specs=[row, vec],
        out_shape=[jax.ShapeDtypeStruct((T, D), f32), jax.ShapeDtypeStruct((1, D), f32)],
        compiler_params=_cparams(1),
    )(g, x, dh, gx_res)


def _prev_rows(cur, halo_ref, first):
    top = jnp.where(first, 0.0, halo_ref[SUBLANES - 1:SUBLANES, :])
    rolled = pltpu.roll(cur, 1, 0)
    rid = lax.broadcasted_iota(jnp.int32, cur.shape, 0)
    return jnp.where(rid == 0, top, rolled)


def _shift_fwd(name, proj, off, width, mu, T, tm):
    cw = _tile(width, 512)
    ncol, cb = width // cw, off // cw
    hb = tm // SUBLANES

    def body(mu_ref, cur_ref, halo_ref, o_ref):
        i = pl.program_id(1)
        cur = cur_ref[...]
        prev = _prev_rows(cur, halo_ref, i == 0)
        o_ref[...] = cur + (prev - cur) * mu_ref[...]

    return pl.pallas_call(
        body, name=name, grid=(ncol, T // tm),
        in_specs=[pl.BlockSpec((1, cw), lambda j, i: (0, j)),
                  pl.BlockSpec((tm, cw), lambda j, i: (i, cb + j)),
                  pl.BlockSpec((SUBLANES, cw), lambda j, i: (jnp.maximum(i * hb - 1, 0), cb + j))],
        out_specs=pl.BlockSpec((tm, cw), lambda j, i: (i, j)),
        out_shape=jax.ShapeDtypeStruct((T, width), f32),
        compiler_params=_cparams(2),
    )(mu, proj, proj)


def _shift_bwd(name, proj, off, width, mu, dxs, T, tm):
    cw = _tile(width, 512)
    ncol, cb = width // cw, off // cw
    hb, nt = tm // SUBLANES, T // tm

    def body(mu_ref, cur_ref, halo_ref, d_ref, dnext_ref, o_ref, dmu_ref):
        i = pl.program_id(1)
        cur = cur_ref[...]
        prev = _prev_rows(cur, halo_ref, i == 0)
        d = d_ref[...]
        bottom = jnp.where(i == nt - 1, 0.0, dnext_ref[0:1, :])
        rid = lax.broadcasted_iota(jnp.int32, d.shape, 0)
        d_next = jnp.where(rid == tm - 1, bottom, pltpu.roll(d, tm - 1, 0))
        mu_v = mu_ref[...]
        o_ref[...] = (d * (1.0 - mu_v) + d_next * mu_v).astype(o_ref.dtype)

        @pl.when(i == 0)
        def _():
            dmu_ref[...] = jnp.zeros_like(dmu_ref)

        dmu_ref[...] += jnp.sum(d * (prev - cur), axis=0, keepdims=True)

    return pl.pallas_call(
        body, name=name, grid=(ncol, nt),
        in_specs=[pl.BlockSpec((1, cw), lambda j, i: (0, j)),
                  pl.BlockSpec((tm, cw), lambda j, i: (i, cb + j)),
                  pl.BlockSpec((SUBLANES, cw), lambda j, i: (jnp.maximum(i * hb - 1, 0), cb + j)),
                  pl.BlockSpec((tm, cw), lambda j, i: (i, j)),
                  pl.BlockSpec((SUBLANES, cw), lambda j, i: (jnp.minimum((i + 1) * hb, nt * hb - 1), j))],
        out_specs=[pl.BlockSpec((tm, cw), lambda j, i: (i, j)), pl.BlockSpec((1, cw), lambda j, i: (0, j))],
        out_shape=[jax.ShapeDtypeStruct((T, width), bf16), jax.ShapeDtypeStruct((1, width), f32)],
        compiler_params=_cparams(2),
    )(mu, proj, proj, dxs, dxs)


def _rolled_copies(dst_ref, ext):
    n = ext.shape[0]
    dst_ref[0] = ext
    for r in range(1, SUBLANES):
        dst_ref[r] = pltpu.roll(ext, n - r, 0)


def _window(rolled_ref, start, rows):
    q, r = divmod(start, SUBLANES)
    return rolled_ref[r, pl.ds(SUBLANES * q, rows), :]


def _conv_fwd(proj, off_v, off_g, width, conv_w, conv_b, ktaps, T, tm):
    cw = _tile(width, 512)
    ncol = width // cw
    hb = tm // CONV_HALO
    lead = CONV_HALO - (ktaps - 1)

    def body(w_ref, b_ref, v_ref, g_ref, hv_ref, hg_ref, o_ref, u_ref):
        i = pl.program_id(1)
        halo = hv_ref[...] * _sigmoid(hg_ref[...])
        _rolled_copies(u_ref, jnp.concatenate([jnp.where(i == 0, 0.0, halo), v_ref[...] * _sigmoid(g_ref[...])], axis=0))
        acc = jnp.zeros((tm, cw), f32) + b_ref[...]
        for j in range(ktaps):
            acc = acc + _window(u_ref, lead + j, tm) * w_ref[j:j + 1, :]
        o_ref[...] = acc

    def tile(off):
        return pl.BlockSpec((tm, cw), lambda j, i: (i, off // cw + j))

    def halo(off):
        return pl.BlockSpec((CONV_HALO, cw), lambda j, i: (jnp.maximum(i * hb - 1, 0), off // cw + j))

    return pl.pallas_call(
        body, name="conv_fwd", grid=(ncol, T // tm),
        in_specs=[pl.BlockSpec((CONV_HALO, cw), lambda j, i: (0, j)), pl.BlockSpec((1, cw), lambda j, i: (0, j)),
                  tile(off_v), tile(off_g), halo(off_v), halo(off_g)],
        out_specs=pl.BlockSpec((tm, cw), lambda j, i: (i, j)),
        out_shape=jax.ShapeDtypeStruct((T, width), f32),
        scratch_shapes=[pltpu.VMEM((SUBLANES, CONV_HALO + tm, cw), f32)],
        compiler_params=_cparams(2),
    )(conv_w, conv_b, proj, proj, proj, proj)


def _conv_bwd(proj, off_v, off_g, width, conv_w, dc, ktaps, T, tm):
    cw = _tile(width, 512)
    ncol = width // cw
    hb, nt = tm // CONV_HALO, T // tm
    lead = CONV_HALO - (ktaps - 1)

    def body(w_ref, v_ref, g_ref, hv_ref, hg_ref, dc_ref, dcn_ref, dv_ref, dg_ref, dw_ref, db_ref, u_ref, d_ref):
        i = pl.program_id(1)
        halo = hv_ref[...] * _sigmoid(hg_ref[...])
        sig = _sigmoid(g_ref[...])
        gv = v_ref[...]
        _rolled_copies(u_ref, jnp.concatenate([jnp.where(i == 0, 0.0, halo), gv * sig], axis=0))
        dcur = dc_ref[...]
        _rolled_copies(d_ref, jnp.concatenate([dcur, jnp.where(i == nt - 1, 0.0, dcn_ref[...])], axis=0))

        @pl.when(i == 0)
        def _():
            dw_ref[...] = jnp.zeros_like(dw_ref)
            db_ref[...] = jnp.zeros_like(db_ref)

        du = jnp.zeros((tm, cw), f32)
        for j in range(ktaps):
            du = du + _window(d_ref, ktaps - 1 - j, tm) * w_ref[j:j + 1, :]
            dw_ref[j:j + 1, :] += jnp.sum(_window(u_ref, lead + j, tm) * dcur, axis=0, keepdims=True)
        db_ref[...] += jnp.sum(dcur, axis=0, keepdims=True)
        dv_ref[...] = (du * sig).astype(dv_ref.dtype)
        dg_ref[...] = (du * gv * sig * (1.0 - sig)).astype(dg_ref.dtype)

    def tile(off):
        return pl.BlockSpec((tm, cw), lambda j, i: (i, off // cw + j))

    def halo(off):
        return pl.BlockSpec((CONV_HALO, cw), lambda j, i: (jnp.maximum(i * hb - 1, 0), off // cw + j))

    return pl.pallas_call(
        body, name="conv_bwd", grid=(ncol, nt),
        in_specs=[pl.BlockSpec((CONV_HALO, cw), lambda j, i: (0, j)),
                  tile(off_v), tile(off_g), halo(off_v), halo(off_g),
                  pl.BlockSpec((tm, cw), lambda j, i: (i, j)),
                  pl.BlockSpec((CONV_HALO, cw), lambda j, i: (jnp.minimum((i + 1) * hb, nt * hb - 1), j))],
        out_specs=[pl.BlockSpec((tm, cw), lambda j, i: (i, j)), pl.BlockSpec((tm, cw), lambda j, i: (i, j)),
                   pl.BlockSpec((CONV_HALO, cw), lambda j, i: (0, j)), pl.BlockSpec((1, cw), lambda j, i: (0, j))],
        out_shape=[jax.ShapeDtypeStruct((T, width), bf16), jax.ShapeDtypeStruct((T, width), bf16),
                   jax.ShapeDtypeStruct((CONV_HALO, width), f32), jax.ShapeDtypeStruct((1, width), f32)],
        scratch_shapes=[pltpu.VMEM((SUBLANES, CONV_HALO + tm, cw), f32), pltpu.VMEM((SUBLANES, tm + CONV_HALO, cw), f32)],
        compiler_params=_cparams(2),
    )(conv_w, proj, proj, proj, proj, dc, dc)


def _each(f, *lists):
    return [f(*xs) for xs in zip(*lists)]


def _wkv_local(r, lw, k, v, a, b):
    C = r[0].shape[0]
    P = WKV_PREC
    row = lax.broadcasted_iota(jnp.int32, (C, C), 0)
    col = lax.broadcasted_iota(jnp.int32, (C, C), 1)
    incl, strict = row >= col, row > col
    tri = incl.astype(f32)
    zero = jnp.zeros((C, C), f32)
    G = _each(lambda x: _mm(tri, x, P), lw)
    to_end = _each(lambda x, g: jnp.exp(jnp.sum(x, axis=0, keepdims=True) - g), lw, G)
    e_g = _each(jnp.exp, G)
    e_ng = _each(lambda g: jnp.exp(-g), G)
    At = _each(lambda x, g, w: x * jnp.exp(g - w), a, G, lw)
    Rt = _each(jnp.multiply, r, e_g)
    Kt = _each(jnp.multiply, k, e_ng)
    Bt = _each(jnp.multiply, b, e_ng)
    sc = _each(lambda at, rt, bt, kt: _mm_nt(jnp.concatenate([at, rt], axis=0), jnp.concatenate([bt, kt], axis=0), P),
               At, Rt, Bt, Kt)
    L = _each(lambda s: jnp.where(strict, s[:C, :C], zero), sc)
    M = _each(lambda s: jnp.where(strict, s[:C, C:], zero), sc)
    Pb = _each(lambda s: jnp.where(incl, s[C:, :C], zero), sc)
    Pk = _each(lambda s: jnp.where(incl, s[C:, C:], zero), sc)
    MPk = _each(lambda m, p, x: _bmm(jnp.concatenate([m, p], axis=0), x), M, Pk, v)
    WU = _each(lambda at, mp: jnp.concatenate([at, mp[:C]], axis=1), At, MPk)
    Lp = L
    n = 1
    while n < C:
        n *= 2
        if n < C:
            step = _each(lambda l, x: _bmm(l, jnp.concatenate([x, l], axis=1)), Lp, WU)
            WU = _each(lambda x, s: x + s[:, :x.shape[1]], WU, step)
            Lp = _each(lambda x, s: s[:, x.shape[1]:], WU, step)
        else:
            WU = _each(lambda x, l: x + _bmm(l, x), WU, Lp)
    N = r[0].shape[1]
    W = _each(lambda x: x[:, :N], WU)
    U = _each(lambda x: x[:, N:], WU)
    Y0 = _each(lambda mp: mp[C:], MPk)
    Bend = _each(jnp.multiply, b, to_end)
    Z = _each(lambda x, y, e: _bmm_tn(x, y * e), v, k, to_end)
    return W, U, Rt, Pb, Y0, Bend, Z


def _wkv_state(S0, W, U, Rt, Pb, Bend, lw, Y0, Z):
    C = W[0].shape[0]
    WR = _each(lambda w, rt, s: _bmm_nt(jnp.concatenate([w, rt], axis=0), s), W, Rt, S0)
    X = _each(lambda wr, u: wr[:C] + u, WR, U)
    y = _each(lambda p, x, wr, c: _bmm(p, x) + wr[C:] + c, Pb, X, WR, Y0)
    S1 = _each(lambda s, w, x, e, z: s * jnp.exp(jnp.sum(w, axis=0, keepdims=True)) + _bmm_tn(x, e) + z,
               S0, lw, X, Bend, Z)
    return y, S1


def _wkv_dims(head, T, RW, heads_per_step):
    C = min(WKV_CHUNK, T)
    nh = RW // head
    hb = min(heads_per_step, nh)
    return C, nh, hb, hb * head, T // C


def _heads(ref, hb, head):
    return [ref[:, h * head:(h + 1) * head] for h in range(hb)]


def _put_heads(ref, vals, head):
    for h, val in enumerate(vals):
        ref[:, h * head:(h + 1) * head] = val


def _wkv_local_fwd(r, lw, k, v, a, b, head, T, RW):
    C, nh, hb, bw, nc = _wkv_dims(head, T, RW, WKV_HEADS)

    def body(*refs):
        ins, outs = refs[:6], refs[6:]
        res = _wkv_local(*[_heads(x, hb, head) for x in ins])
        for o_ref, vals in zip(outs[:6], res[:6]):
            _put_heads(o_ref, vals, head)
        for h in range(hb):
            outs[6][0, h] = res[6][h]

    blk = pl.BlockSpec((C, bw), lambda g, c: (c, g))
    sq = pl.BlockSpec((1, hb, head, head), lambda g, c: (c, g, 0, 0))
    return pl.pallas_call(
        body, name="wkv_local", grid=(nh // hb, nc),
        in_specs=[blk] * 6, out_specs=[blk] * 6 + [sq],
        out_shape=[jax.ShapeDtypeStruct((T, RW), f32)] * 6 + [jax.ShapeDtypeStruct((nc, nh, head, head), f32)],
        compiler_params=_cparams(2),
    )(r, lw, k, v, a, b)


def _wkv_state_fwd(W, U, Rt, Pb, Bend, lw, Y0, Z, head, T, RW):
    C, nh, hb, bw, nc = _wkv_dims(head, T, RW, WKV_STATE_HEADS)

    def body(w_ref, u_ref, rt_ref, pb_ref, be_ref, lw_ref, y0_ref, z_ref, y_ref, st_ref, s_ref):
        @pl.when(pl.program_id(1) == 0)
        def _():
            s_ref[...] = jnp.zeros_like(s_ref)

        S0 = [s_ref[h] for h in range(hb)]
        for h in range(hb):
            st_ref[0, h] = S0[h]
        rows = [_heads(x, hb, head) for x in (w_ref, u_ref, rt_ref, pb_ref, be_ref, lw_ref, y0_ref)]
        y, S1 = _wkv_state(S0, *rows, [z_ref[0, h] for h in range(hb)])
        _put_heads(y_ref, y, head)
        for h in range(hb):
            s_ref[h] = S1[h]

    blk = pl.BlockSpec((C, bw), lambda g, c: (c, g))
    sq = pl.BlockSpec((1, hb, head, head), lambda g, c: (c, g, 0, 0))
    return pl.pallas_call(
        body, name="wkv_state", grid=(nh // hb, nc),
        in_specs=[blk] * 7 + [sq], out_specs=[blk, sq],
        out_shape=[jax.ShapeDtypeStruct((T, RW), f32), jax.ShapeDtypeStruct((nc, nh, head, head), f32)],
        scratch_shapes=[pltpu.VMEM((hb, head, head), f32)],
        compiler_params=_cparams(2),
    )(W, U, Rt, Pb, Bend, lw, Y0, Z)


def _wkv_state_bwd(W, U, Rt, Pb, Bend, lw, Y0, Z, states, dy, head, T, RW):
    C, nh, hb, bw, nc = _wkv_dims(head, T, RW, WKV_STATE_HEADS)

    def body(w_ref, u_ref, rt_ref, pb_ref, be_ref, lw_ref, y0_ref, z_ref, st_ref, dy_ref,
             dw_ref, du_ref, drt_ref, dpb_ref, dbe_ref, dlw_ref, dz_ref, ds_ref):
        @pl.when(pl.program_id(1) == 0)
        def _():
            ds_ref[...] = jnp.zeros_like(ds_ref)

        dS1 = [ds_ref[h] for h in range(hb)]
        for h in range(hb):
            dz_ref[0, h] = dS1[h]
        rows = [_heads(x, hb, head) for x in (w_ref, u_ref, rt_ref, pb_ref, be_ref, lw_ref)]
        Y0 = _heads(y0_ref, hb, head)
        Zs = [z_ref[0, h] for h in range(hb)]
        _, vjp = jax.vjp(lambda s0, *rw: _wkv_state(s0, *rw, Y0, Zs), [st_ref[0, h] for h in range(hb)], *rows)
        grads = vjp((_heads(dy_ref, hb, head), dS1))
        for o_ref, vals in zip((dw_ref, du_ref, drt_ref, dpb_ref, dbe_ref, dlw_ref), grads[1:]):
            _put_heads(o_ref, vals, head)
        for h in range(hb):
            ds_ref[h] = grads[0][h]

    blk = pl.BlockSpec((C, bw), lambda g, c: (nc - 1 - c, g))
    sq = pl.BlockSpec((1, hb, head, head), lambda g, c: (nc - 1 - c, g, 0, 0))
    return pl.pallas_call(
        body, name="wkv_state_bwd", grid=(nh // hb, nc),
        in_specs=[blk] * 7 + [sq, sq, blk], out_specs=[blk] * 6 + [sq],
        out_shape=[jax.ShapeDtypeStruct((T, RW), f32)] * 6 + [jax.ShapeDtypeStruct((nc, nh, head, head), f32)],
        scratch_shapes=[pltpu.VMEM((hb, head, head), f32)],
        compiler_params=_cparams(2),
    )(W, U, Rt, Pb, Bend, lw, Y0, Z, states, dy)


def _wkv_local_bwd(r, lw, k, v, a, b, cots, d_lw_x, dr_x, dk_x, dv_x, head, T, RW):
    C, nh, hb, bw, nc = _wkv_dims(head, T, RW, WKV_HEADS)

    def body(*refs):
        ins, cot_refs, add_refs, outs = refs[:6], refs[6:13], refs[13:17], refs[17:]
        _, vjp = jax.vjp(_wkv_local, *[_heads(x, hb, head) for x in ins])
        cts = [_heads(x, hb, head) for x in cot_refs[:6]] + [[cot_refs[6][0, h] for h in range(hb)]]
        dr, dlw, dk, dv, da, db = vjp(tuple(cts))
        dlw_x, drx, dkx, dvx = [_heads(x, hb, head) for x in add_refs]
        _put_heads(outs[0], _each(jnp.add, dr, drx), head)
        _put_heads(outs[1], _each(jnp.add, dlw, dlw_x), head)
        _put_heads(outs[2], _each(jnp.add, dk, dkx), head)
        _put_heads(outs[3], _each(jnp.add, dv, dvx), head)
        _put_heads(outs[4], da, head)
        _put_heads(outs[5], db, head)

    blk = pl.BlockSpec((C, bw), lambda g, c: (c, g))
    sq = pl.BlockSpec((1, hb, head, head), lambda g, c: (c, g, 0, 0))
    return pl.pallas_call(
        body, name="wkv_local_bwd", grid=(nh // hb, nc),
        in_specs=[blk] * 12 + [sq] + [blk] * 4, out_specs=[blk] * 6,
        out_shape=[jax.ShapeDtypeStruct((T, RW), f32)] * 6,
        compiler_params=_cparams(2),
    )(r, lw, k, v, a, b, *cots, d_lw_x, dr_x, dk_x, dv_x)


def _rows_tile(R, row_bytes, budget, mult=SUBLANES):
    best = None
    t = mult
    while t <= R:
        if R % t == 0 and t * row_bytes <= budget:
            best = t
        t += mult
    return best if best is not None else R


def _sum_slots(name, parts):
    S, R, W = parts.shape
    budget = 4 << 20
    tr = _rows_tile(R, S * W * 4, budget, 2 * SUBLANES)
    cw = W if tr * S * W * 4 <= 2 * budget else _tile(W, max(LANES, 2 * budget // (S * tr * 4)))

    def body(p_ref, o_ref):
        acc = p_ref[0].astype(f32)
        for d in range(1, S):
            acc = acc + p_ref[d].astype(f32)
        o_ref[...] = acc

    return pl.pallas_call(
        body, name=name, grid=(R // tr, W // cw),
        in_specs=[pl.BlockSpec((S, tr, cw), lambda i, j: (0, i, j))],
        out_specs=pl.BlockSpec((tr, cw), lambda i, j: (i, j)),
        out_shape=jax.ShapeDtypeStruct((R, W), f32),
        compiler_params=_cparams(2),
    )(parts)


def _tile2d(R, W, budget, mult):
    tr = _rows_tile(R, LANES * 4, budget, mult)
    cw = _tile(W, max(LANES, budget // (tr * 4))) if W % LANES == 0 else W
    return tr, cw


def _core_index():
    return lax.axis_index("c").astype(jnp.int32).reshape(1)


def _add_kept(name, shards, got):
    _, _, R, W = shards.shape
    tr, cw = _tile2d(R, W, 2 << 20, 2 * SUBLANES)

    def body(core_ref, a_ref, b_ref, o_ref):
        o_ref[...] = (a_ref[...].astype(f32) + b_ref[...].astype(f32)).astype(o_ref.dtype)

    return pl.pallas_call(
        body, name=name,
        grid_spec=pltpu.PrefetchScalarGridSpec(
            num_scalar_prefetch=1, grid=(4, R // tr, W // cw),
            in_specs=[pl.BlockSpec((None, None, tr, cw), lambda s, i, j, core: (s, core[0], i, j)),
                      pl.BlockSpec((None, tr, cw), lambda s, i, j, core: (s, i, j))],
            out_specs=pl.BlockSpec((None, tr, cw), lambda s, i, j, core: (s, i, j))),
        out_shape=jax.ShapeDtypeStruct((4, R, W), shards.dtype), compiler_params=_cparams(3),
    )(_core_index(), shards, got)


def _sum_slots_half(name, parts):
    S, R, W = parts.shape
    budget = 4 << 20
    tr = _rows_tile(R, S * W * 4, budget, 2 * SUBLANES)
    cw = W if tr * S * W * 4 <= 2 * budget else _tile(W, max(LANES, 2 * budget // (S * tr * 4)))
    nrow = R // tr

    def body(core_ref, p_ref, o_ref):
        acc = p_ref[0].astype(f32)
        for d in range(1, S):
            acc = acc + p_ref[d].astype(f32)
        o_ref[...] = acc

    return pl.pallas_call(
        body, name=name,
        grid_spec=pltpu.PrefetchScalarGridSpec(
            num_scalar_prefetch=1, grid=(nrow, W // cw),
            in_specs=[pl.BlockSpec((S, tr, cw), lambda i, j, core: (0, i, j))],
            out_specs=pl.BlockSpec((tr, cw), lambda i, j, core: (core[0] * nrow + i, j))),
        out_shape=jax.ShapeDtypeStruct((2 * R, W), f32), compiler_params=_cparams(2),
    )(_core_index(), parts)


def _adamw(name, w, g, m, v):
    R, W = w.shape
    tr, cw = _tile2d(R, W, 2 << 20, SUBLANES)

    def body(w_ref, g_ref, m_ref, v_ref, d_ref, nm_ref, nv_ref):
        g_v = g_ref[...]
        nm = ADAM_B1 * m_ref[...] + (1.0 - ADAM_B1) * g_v
        nv = ADAM_B2 * v_ref[...] + (1.0 - ADAM_B2) * (g_v * g_v)
        m_hat = nm / (1.0 - ADAM_B1 ** ADAM_STEP)
        v_hat = nv / (1.0 - ADAM_B2 ** ADAM_STEP)
        d_ref[...] = -ADAM_LR * (m_hat / (jnp.sqrt(v_hat) + ADAM_EPS) + ADAM_WD * w_ref[...])
        nm_ref[...] = nm
        nv_ref[...] = nv

    blk = pl.BlockSpec((tr, cw), lambda i, j: (i, j))
    return pl.pallas_call(
        body, name=name, grid=(R // tr, W // cw),
        in_specs=[blk] * 4, out_specs=[blk] * 3,
        out_shape=[jax.ShapeDtypeStruct((R, W), f32)] * 3,
        compiler_params=_cparams(2),
    )(w, g, m, v)


ANY = pl.BlockSpec(memory_space=pl.ANY)


def _place():
    return lax.axis_index("x"), lax.axis_index("y"), lax.axis_index("c")


class _Comm:
    def __init__(self, operands, out_shape, scratch, start, wait):
        self.operands, self.out_shape, self.scratch, self.start, self.wait = operands, out_shape, scratch, start, wait


def _run_comm(name, comm):
    n = len(comm.operands)

    def body(*refs):
        parts = (refs[:n], refs[n:2 * n], refs[2 * n:])
        comm.start(*parts)
        comm.wait(*parts)

    return pl.pallas_call(
        body, name=name, in_specs=[ANY] * n, out_specs=[ANY] * n, out_shape=comm.out_shape,
        scratch_shapes=comm.scratch,
    )(*comm.operands)


def _copy_chunks(rows, cols):
    k = SHARE_CHUNKS // 2
    if rows % (k * 2 * SUBLANES) == 0:
        return [(pl.ds(q * (rows // k), rows // k), pl.ds(0, cols)) for q in range(k)]
    if cols % (k * LANES) == 0:
        return [(pl.ds(0, rows), pl.ds(q * (cols // k), cols // k)) for q in range(k)]
    return [(pl.ds(0, rows), pl.ds(0, cols))]


def _gather_chips(arrays, relayed=False, copy_own=True):
    n = len(arrays)
    assert not relayed or all(a.shape[2] % (2 * LANES) == 0 for a in arrays)
    parts = [(a, h, blk) for a, arr in enumerate(arrays) for h in range(2) for blk in _copy_chunks(*arr.shape[1:])]
    parts = parts if copy_own else []

    def copies(ins, outs, sems):
        send_sems, recv_sems, local_sems = sems
        x, y, c = _place()
        mine = 2 * x + y
        sib = (x, y, 1 - c)
        chips = [(1 - x, y), (x, 1 - y), (1 - x, 1 - y)]

        def local():
            return [pltpu.make_async_copy(ins[a].at[(h, *blk)], outs[a].at[(mine, h, *blk)], local_sems.at[p])
                    for p, (a, h, blk) in enumerate(parts)]

        def over_ici(a, j, slot):
            px, py = chips[j]
            return pltpu.make_async_remote_copy(
                src_ref=ins[a].at[c], dst_ref=outs[a].at[slot, c], send_sem=send_sems.at[3 * a + j],
                recv_sem=recv_sems.at[3 * a + j], device_id=(px, py, c), device_id_type=MESH)

        def over_d2d(a, j, half):
            px, py = chips[j]
            slot = 2 * px + py
            return pltpu.make_async_remote_copy(
                src_ref=outs[a].at[slot, half], dst_ref=outs[a].at[slot, half], send_sem=send_sems.at[3 * (n + a) + j],
                recv_sem=recv_sems.at[3 * (n + a) + j], device_id=sib, device_id_type=MESH)

        def relay(a, q, origin):
            ox, oy = origin
            px, py = chips[1 - q]
            rows, cols = arrays[a].shape[1], arrays[a].shape[2] // 2
            win = outs[a].at[2 * ox + oy, c, pl.ds(0, rows), pl.ds(q * cols, cols)]
            return pltpu.make_async_remote_copy(
                src_ref=win, dst_ref=win, send_sem=send_sems.at[6 * n + 2 * a + q],
                recv_sem=recv_sems.at[6 * n + 2 * a + q], device_id=(px, py, c), device_id_type=MESH)

        direct = 2 if relayed else 3
        pairs = [(a, j) for a in range(n) for j in range(direct)]
        return dict(local=local,
                    sends=lambda: [over_ici(a, j, mine) for a, j in pairs],
                    landing=lambda: [over_ici(a, j, 2 * chips[j][0] + chips[j][1]) for a, j in pairs],
                    passed=lambda: [over_d2d(a, j, c) for a, j in pairs],
                    relays=lambda: [relay(a, j, chips[j]) for a, j in pairs],
                    relayed_in=lambda: [relay(a, q, chips[2]) for a in range(n) for q in range(2)],
                    passed_diag=lambda: [over_d2d(a, 2, c) for a in range(n)],
                    from_sib=lambda: [over_d2d(a, j, 1 - c) for a in range(n) for j in range(3)])

    def start(ins, outs, sems):
        cps = copies(ins, outs, sems)
        for cp in cps["local"]() + cps["sends"]():
            cp.start()

    def wait(ins, outs, sems):
        cps = copies(ins, outs, sems)
        passed = cps["passed"]()
        relays = cps["relays"]() if relayed else [None] * len(passed)
        for got, on, via in zip(cps["landing"](), passed, relays):
            got.wait_recv()
            if relayed:
                via.start()
            on.start()
        if relayed:
            for cp in cps["relayed_in"]():
                cp.wait_recv()
            diag = cps["passed_diag"]()
            for cp in diag:
                cp.start()
            passed = passed + diag + relays
        for cp in cps["from_sib"]():
            cp.wait_recv()
        for cp in cps["sends"]() + passed:
            cp.wait_send()
        for cp in cps["local"]():
            cp.wait()

    return _Comm(arrays, [jax.ShapeDtypeStruct((4,) + a.shape, a.dtype) for a in arrays],
                 [pltpu.SemaphoreType.DMA((8 * n,)), pltpu.SemaphoreType.DMA((8 * n,)),
                  pltpu.SemaphoreType.DMA((max(len(parts), 1),))], start, wait)


def _exchange_chips(pieces, whole):
    n, m = len(pieces), len(whole)
    parts = [(a, blk) for a, arr in enumerate(pieces) for blk in _copy_chunks(*arr.shape[1:])]

    def copies(ins, outs, sems):
        send_sems, recv_sems, local_sems = sems
        x, y, c = _place()
        chip, dev = 2 * x + y, 4 * x + 2 * y + c
        chips = [(1 - x, y), (x, 1 - y), (1 - x, 1 - y)]
        peers = [(x ^ (k >> 2), y ^ ((k >> 1) & 1), c ^ (k & 1)) for k in range(1, 8)]
        def local():
            cps = [pltpu.make_async_copy(ins[a].at[(chip, *blk)], outs[a].at[(chip, *blk)], local_sems.at[p])
                   for p, (a, blk) in enumerate(parts)]
            return cps + [pltpu.make_async_copy(ins[n + b], outs[n + b].at[dev], local_sems.at[len(parts) + b])
                          for b in range(m)]

        def piece(a, j, slot_from):
            px, py = chips[j]
            return pltpu.make_async_remote_copy(
                src_ref=ins[a].at[2 * px + py], dst_ref=outs[a].at[slot_from], send_sem=send_sems.at[3 * a + j],
                recv_sem=recv_sems.at[3 * a + j], device_id=(px, py, c), device_id_type=MESH)

        def everyone(b, j, slot_from):
            px, py, pc = peers[j]
            return pltpu.make_async_remote_copy(
                src_ref=ins[n + b], dst_ref=outs[n + b].at[slot_from], send_sem=send_sems.at[3 * n + 7 * b + j],
                recv_sem=recv_sems.at[3 * n + 7 * b + j], device_id=(px, py, pc), device_id_type=MESH)

        def sends():
            return ([everyone(b, j, dev) for b in range(m) for j in range(7)]
                    + [piece(a, j, chip) for a in range(n) for j in range(3)])

        def landing():
            return ([everyone(b, j, 4 * px + 2 * py + pc) for b in range(m) for j, (px, py, pc) in enumerate(peers)]
                    + [piece(a, j, 2 * px + py) for a in range(n) for j, (px, py) in enumerate(chips)])

        return local, sends, landing

    def start(ins, outs, sems):
        local, sends, _ = copies(ins, outs, sems)
        for cp in local() + sends():
            cp.start()

    def wait(ins, outs, sems):
        local, sends, landing = copies(ins, outs, sems)
        for cp in landing():
            cp.wait_recv()
        for cp in sends():
            cp.wait_send()
        for cp in local():
            cp.wait()

    shapes = [jax.ShapeDtypeStruct(a.shape, a.dtype) for a in pieces]
    shapes += [jax.ShapeDtypeStruct((8,) + a.shape, a.dtype) for a in whole]
    nsem = 3 * n + 7 * m
    return _Comm(list(pieces) + list(whole), shapes,
                 [pltpu.SemaphoreType.DMA((nsem,)), pltpu.SemaphoreType.DMA((nsem,)),
                  pltpu.SemaphoreType.DMA((len(parts) + m,))], start, wait)


def _pair_exchange(name, shards):
    n = len(shards)

    def body(*refs):
        ins, outs = refs[:n], refs[n:2 * n]
        send_sems, recv_sems = refs[2 * n:]
        x, y, c = _place()
        copies = [pltpu.make_async_remote_copy(
            src_ref=ins[a].at[s, 1 - c], dst_ref=outs[a].at[s], send_sem=send_sems.at[4 * a + s],
            recv_sem=recv_sems.at[4 * a + s], device_id=(x, y, 1 - c), device_id_type=MESH)
            for a in range(n) for s in range(4)]
        for cp in copies:
            cp.start()
        for cp in copies:
            cp.wait_recv()
        for cp in copies:
            cp.wait_send()

    return pl.pallas_call(
        body, name=name, in_specs=[ANY] * n, out_specs=[ANY] * n,
        out_shape=[jax.ShapeDtypeStruct((4,) + a.shape[2:], a.dtype) for a in shards],
        scratch_shapes=[pltpu.SemaphoreType.DMA((4 * n,)), pltpu.SemaphoreType.DMA((4 * n,))],
    )(*shards)


def _share_halves(name, arrays):
    n = len(arrays)

    def body(*refs):
        bufs = refs[n:2 * n]
        send_sems, recv_sems = refs[2 * n:]
        x, y, c = _place()

        def half(a, h):
            rows = arrays[a].shape[0] // 2
            return bufs[a].at[pl.ds(pl.multiple_of(h * rows, SUBLANES), rows)]

        def copy(a, h):
            return pltpu.make_async_remote_copy(
                src_ref=half(a, h), dst_ref=half(a, h), send_sem=send_sems.at[a], recv_sem=recv_sems.at[a],
                device_id=(x, y, 1 - c), device_id_type=MESH)

        sends = [copy(a, c) for a in range(n)]
        for cp in sends:
            cp.start()
        for a in range(n):
            copy(a, 1 - c).wait_recv()
        for cp in sends:
            cp.wait_send()

    return pl.pallas_call(
        body, name=name, in_specs=[ANY] * n, out_specs=[ANY] * n,
        out_shape=[jax.ShapeDtypeStruct(a.shape, a.dtype) for a in arrays],
        input_output_aliases={a: a for a in range(n)},
        scratch_shapes=[pltpu.SemaphoreType.DMA((n,)), pltpu.SemaphoreType.DMA((n,))],
    )(*arrays)


def _place_blocks(blocks, axis):
    shape = list(blocks[0].shape)
    shape[axis] = sum(b.shape[axis] for b in blocks)
    buf = lax.empty(tuple(shape), blocks[0].dtype)
    at = 0
    for b in blocks:
        buf = lax.dynamic_update_slice_in_dim(buf, b, at, axis)
        at += b.shape[axis]
    return buf


def kernel(x, norm_pre_g, w_in, mu_shift, w0, w_lora_up, a0, a_lora_up, k_k, k_a, r_k, lnx_g, lnx_b, conv_w, conv_b, cln_g, cln_b, w_pw2, b_pw2, w_out, norm_post_g, loss_target, m_norm_pre_g, m_w_in, m_mu_shift, m_w0, m_w_lora_up, m_a0, m_a_lora_up, m_k_k, m_k_a, m_r_k, m_lnx_g, m_lnx_b, m_conv_w, m_conv_b, m_cln_g, m_cln_b, m_w_pw2, m_b_pw2, m_w_out, m_norm_post_g, v_norm_pre_g, v_w_in, v_mu_shift, v_w0, v_w_lora_up, v_a0, v_a_lora_up, v_k_k, v_k_a, v_r_k, v_lnx_g, v_lnx_b, v_conv_w, v_conv_b, v_cln_g, v_cln_b, v_w_pw2, v_b_pw2, v_w_out, v_norm_post_g):
    _, T, D = x.shape
    RW = w0.shape[0]
    CW = conv_b.shape[0]
    head = r_k.shape[1]
    lora = w_lora_up.shape[0]
    ktaps = conv_w.shape[0]
    assert RW == CW and 2 * lora <= LORA_PAD and ktaps - 1 <= CONV_HALO
    n_in = 3 * RW + 2 * lora + RW + 3 * CW
    shard = n_in // 4
    PW = 7 * RW + LORA_PAD
    off_l = 7 * RW
    tm = min(256, T // 2)
    tm_wide = min(128, T // 2)
    tm_halo = min(512, T // 2)
    row = lambda vec: vec.reshape(1, -1)
    x2, tgt2 = x[0], loss_target[0]

    halves = lambda a: a.reshape(2, a.shape[0] // 2, a.shape[1])
    conv_w_p = jnp.concatenate([conv_w, jnp.zeros((CONV_HALO - ktaps, CW // 4), f32)], axis=0)
    w_in_t, m_w_in_t, v_w_in_t = w_in.T, m_w_in.T, v_w_in.T
    own_win = halves(w_in_t.astype(bf16))
    (g_win,) = _run_comm("gather_w_in", _gather_chips([own_win], relayed=True, copy_own=False))
    g_win = lax.dynamic_update_slice(g_win, own_win[None], (2 * lax.axis_index("x") + lax.axis_index("y"), 0, 0, 0))
    win_t = g_win.reshape(n_in, D)
    lo = 3 * RW
    wp_t = _place_blocks([win_t[:lo], win_t[lo + 2 * lora:], win_t[lo:lo + 2 * lora],
                          jnp.zeros((LORA_PAD - 2 * lora, D), bf16)], axis=0)
    npg = row(norm_pre_g)
    (h,) = _row_fwd("rms_pre", _fn_rms_pre, [(npg, False)], [(x2, 0, D, False)], [(D, bf16)], T, tm)
    others = [halves(a) for a in (w_lora_up, a_lora_up, conv_w_p, w_pw2.astype(bf16), w_out.astype(bf16))]
    proj, (g_wup, g_aup, g_cw, g_pw2, g_wout) = _matmul("proj", h, wp_t, "nt", f32, comm=_gather_chips(others))
    cat_cols = lambda g: jnp.concatenate([g[s].reshape(-1, g.shape[-1]) for s in range(4)], axis=1)
    wup_full, aup_full, cw_p = cat_cols(g_wup), cat_cols(g_aup), cat_cols(g_cw)
    zl = lambda n: jnp.zeros((n, RW), f32)
    wup_p = jnp.concatenate([wup_full, zl(LORA_PAD - lora)], axis=0)
    aup_p = jnp.concatenate([zl(lora), aup_full, zl(LORA_PAD - 2 * lora)], axis=0)
    pw2_full = g_pw2.reshape(CW, CW)
    wout_full = g_wout.reshape(RW + CW, D)
    mu_r, mu_k, mu_v = (row(mu_shift[s * RW:(s + 1) * RW]) for s in range(3))
    mu_l = row(jnp.concatenate([mu_shift[3 * RW:], jnp.zeros((LORA_PAD - 2 * lora,), f32)]))

    xs_r = _shift_fwd("shift_r", proj, 0, RW, mu_r, T, tm_halo)
    xs_k = _shift_fwd("shift_k", proj, RW, RW, mu_k, T, tm_halo)
    xs_v = _shift_fwd("shift_v", proj, 2 * RW, RW, mu_v, T, tm_halo)
    xs_l = _shift_fwd("shift_l", proj, off_l, LORA_PAD, mu_l, T, tm_halo)
    lora_params = [(row(w0), False), (wup_p, False), (row(a0), False), (aup_p, False)]
    qw, qa = _row_fwd("lora_up", _fn_lora, lora_params, [(xs_l, 0, LORA_PAD, False)], [(RW, f32), (RW, f32)], T, tm)
    ncol = RW // _tile(RW, 512)
    fn_pre = functools.partial(_fn_rwkv_pre, head)
    pre_params = [(row(k_k), True), (row(k_a), True)]
    pre_rows = [(xs_k, 0, RW, True), (qw, 0, RW, True), (qa, 0, RW, True)]
    lw, k_h, a_rec, b_rec = _row_fwd("rwkv_pre", fn_pre, pre_params, pre_rows, [(RW, f32)] * 4, T, tm, ncol)
    wkv_in = (xs_r, lw, k_h, xs_v, a_rec, b_rec)
    c_w, c_u, c_rt, c_pb, c_y0, c_bend, c_z = _wkv_local_fwd(*wkv_in, head, T, RW)
    wkv_loc = (c_w, c_u, c_rt, c_pb, c_bend, lw, c_y0, c_z)
    y_wkv, states = _wkv_state_fwd(*wkv_loc, head, T, RW)
    fn_post = functools.partial(_fn_rwkv_post, head)
    post_params = [(row(lnx_g), True), (row(lnx_b), True), (r_k.reshape(1, RW), True)]
    post_rows = [(y_wkv, 0, RW, True), (xs_r, 0, RW, True), (k_h, 0, RW, True), (xs_v, 0, RW, True),
                 (proj, 3 * RW, RW, True)]
    (y_rwkv,) = _row_fwd("rwkv_post", fn_post, post_params, post_rows, [(RW, bf16)], T, tm, ncol)

    c_pre = _conv_fwd(proj, 4 * RW, 5 * RW, CW, cw_p, row(conv_b), ktaps, T, tm_halo)
    ln_params = [(row(cln_g), False), (row(cln_b), False)]
    (c_act,) = _row_fwd("conv_ln", _fn_conv_ln, ln_params, [(c_pre, 0, CW, False)], [(CW, bf16)], T, tm)
    c2 = _matmul("pw2", c_act, pw2_full, "nn", f32)
    cpost_params = [(row(b_pw2), True)]
    cpost_rows = [(c2, 0, CW, True), (proj, 6 * RW, CW, True)]
    (y_conv,) = _row_fwd("conv_post", _fn_conv_post, cpost_params, cpost_rows, [(CW, bf16)], T, tm, ncol)

    mix = jnp.concatenate([y_rwkv, y_conv], axis=1)
    out = _matmul("out_proj", mix, wout_full, "nn", f32)
    d_out, gx_res, loss_part, g_npost = _post(out, x2, tgt2, row(norm_post_g), T, D, tm_wide)

    g_wout_full = _matmul("d_w_out", mix, d_out, "tn", bf16)
    d_mix = _matmul("d_mix", d_out, wout_full, "nt", f32)

    d_c2, d_gconv, g_bpw2 = _row_bwd("conv_post_bwd", _fn_conv_post, cpost_params, cpost_rows,
                                      [(d_mix, RW, CW, True)], [bf16, bf16], T, tm, ncol)
    g_pw2_full = _matmul("d_w_pw2", c_act, d_c2, "tn", bf16)
    d_cact = _matmul("d_c_act", d_c2, pw2_full, "nt", f32)
    d_cpre, g_clng, g_clnb = _row_bwd("conv_ln_bwd", _fn_conv_ln, ln_params, [(c_pre, 0, CW, False)],
                                      [(d_cact, 0, CW, False)], [f32], T, tm)
    d_gluv, d_glug, g_cw_p, g_cb = _conv_bwd(proj, 4 * RW, 5 * RW, CW, cw_p, d_cpre, ktaps, T, tm_halo)

    d_y, dr_x, dk_x, dv_x, d_grwkv, g_lnxg, g_lnxb, g_rk = _row_bwd(
        "rwkv_post_bwd", fn_post, post_params, post_rows, [(d_mix, 0, RW, True)], [f32, f32, f32, f32, bf16], T, tm, ncol)
    d_cw, d_cu, d_crt, d_cpb, d_cbend, d_lw_dec, d_cz = _wkv_state_bwd(*wkv_loc, states, d_y, head, T, RW)
    d_xr, d_lw, d_kh, d_xv, d_a, d_b = _wkv_local_bwd(
        *wkv_in, (d_cw, d_cu, d_crt, d_cpb, d_y, d_cbend, d_cz), d_lw_dec, dr_x, dk_x, dv_x, head, T, RW)
    pre_cots = [(d_lw, 0, RW, True), (d_kh, 0, RW, True), (d_a, 0, RW, True), (d_b, 0, RW, True)]
    d_xk, d_qw, d_qa, g_kk, g_ka = _row_bwd("rwkv_pre_bwd", fn_pre, pre_params, pre_rows, pre_cots, [f32, f32, f32],
                                            T, tm, ncol)
    d_xl, g_w0, g_wup_p, g_a0, g_aup_p = _row_bwd("lora_up_bwd", _fn_lora, lora_params, [(xs_l, 0, LORA_PAD, False)],
                                                  [(d_qw, 0, RW, False), (d_qa, 0, RW, False)], [f32], T, tm)
    dp_r, g_mur = _shift_bwd("shift_r_bwd", proj, 0, RW, mu_r, d_xr, T, tm_halo)
    dp_k, g_muk = _shift_bwd("shift_k_bwd", proj, RW, RW, mu_k, d_xk, T, tm_halo)
    dp_v, g_muv = _shift_bwd("shift_v_bwd", proj, 2 * RW, RW, mu_v, d_xv, T, tm_halo)
    dp_l, g_mul = _shift_bwd("shift_l_bwd", proj, off_l, LORA_PAD, mu_l, d_xl, T, tm_halo)
    d_proj = _place_blocks([dp_r, dp_k, dp_v, d_grwkv, d_gluv, d_glug, d_gconv, dp_l], axis=1)

    def chip_sums(tag, shards):
        halves4 = [a.reshape(4, 2, a.shape[1] // 2, a.shape[2]) for a in shards]
        got = _pair_exchange("pair_exchange_" + tag, halves4)
        return [_add_kept("chip_sum_%s_%d" % (tag, i), a, g) for i, (a, g) in enumerate(zip(halves4, got))]

    def all_chips(tag, slots):
        return _share_halves("share_" + tag, [_sum_slots_half("sum_%s_%d" % (tag, i), r) for i, r in enumerate(slots)])

    q_early = chip_sums("early", [g_wout_full.reshape(4, (RW + CW) // 4, D), g_pw2_full.reshape(4, CW // 4, CW)])
    g_wp_t, r_early = _matmul("d_w_in", d_proj, h, "tn", bf16, comm=_exchange_chips(q_early, []))
    g_win_t = _place_blocks([g_wp_t[:lo], g_wp_t[off_l:off_l + 2 * lora], g_wp_t[lo:off_l]], axis=0)
    col_shards = lambda a: a.reshape(a.shape[0], 4, a.shape[1] // 4).transpose(1, 0, 2)
    q_late = chip_sums("late", [g_win_t.reshape(4, shard, D), col_shards(g_wup_p[:lora]),
                                col_shards(g_aup_p[lora:2 * lora]), col_shards(g_cw_p)])
    g_mu = jnp.concatenate([g_mur[0], g_muk[0], g_muv[0], g_mul[0, :2 * lora]])
    pad_rows = lambda a, n: jnp.concatenate([a, jnp.zeros((n - a.shape[0], a.shape[1]), f32)], axis=0)
    n_mu = -(-mu_shift.shape[0] // RW)
    small_vecs = [pad_rows(jnp.pad(g_mu, (0, n_mu * RW - g_mu.shape[0])).reshape(n_mu, RW), n_mu),
                  g_w0, g_a0, g_kk, g_ka, g_rk, g_lnxg, g_lnxb, g_cb, g_clng, g_clnb, g_bpw2,
                  g_npost.reshape(D // RW, RW)]
    n_small = sum(a.shape[0] for a in small_vecs)
    n_small_pad = -(-n_small // (2 * SUBLANES)) * (2 * SUBLANES)
    small = pad_rows(jnp.concatenate(small_vecs, axis=0), n_small_pad)

    d_h, r_late = _matmul("d_h", d_proj, wp_t, "nn", bf16, tk_t=2560, comm=_exchange_chips(q_late, [small]))
    grad_x2, g_npre = _rms_pre_bwd(x2, npg, d_h, gx_res, T, D, tm_wide)
    (r_npre,) = _run_comm("exchange_norm_pre", _exchange_chips([], [g_npre.reshape(D // RW, RW)]))
    s_npre = _sum_slots("sum_norm_pre", r_npre)
    s_small = _sum_slots("sum_small", r_late[4])
    grad_w_out, grad_w_pw2 = all_chips("early", r_early)
    grad_w_in, grad_wup, grad_aup, grad_cw = all_chips("late", r_late[:4])

    pos = [0]

    def take(nrows):
        a = s_small[pos[0]:pos[0] + nrows]
        pos[0] += nrows
        return a

    grads = {}
    grads["norm_pre_g"] = s_npre.reshape(D)
    grads["mu_shift"] = take(n_mu).reshape(-1)[:mu_shift.shape[0]]
    for nm in ["w0", "a0", "k_k", "k_a"]:
        grads[nm] = take(1).reshape(RW)
    grads["r_k"] = take(1).reshape(r_k.shape)
    for nm in ["lnx_g", "lnx_b", "conv_b", "cln_g", "cln_b", "b_pw2"]:
        grads[nm] = take(1).reshape(RW)
    grads["norm_post_g"] = take(D // RW).reshape(D)
    grads["w_lora_up"], grads["a_lora_up"], grads["conv_w"] = grad_wup, grad_aup, grad_cw[:ktaps]
    grads["w_in"], grads["w_out"], grads["w_pw2"] = grad_w_in, grad_w_out, grad_w_pw2

    weights = dict(norm_pre_g=norm_pre_g, w_in=w_in, mu_shift=mu_shift, w0=w0, w_lora_up=w_lora_up, a0=a0,
                   a_lora_up=a_lora_up, k_k=k_k, k_a=k_a, r_k=r_k, lnx_g=lnx_g, lnx_b=lnx_b, conv_w=conv_w,
                   conv_b=conv_b, cln_g=cln_g, cln_b=cln_b, w_pw2=w_pw2, b_pw2=b_pw2, w_out=w_out,
                   norm_post_g=norm_post_g)
    ms = dict(norm_pre_g=m_norm_pre_g, w_in=m_w_in, mu_shift=m_mu_shift, w0=m_w0, w_lora_up=m_w_lora_up, a0=m_a0,
              a_lora_up=m_a_lora_up, k_k=m_k_k, k_a=m_k_a, r_k=m_r_k, lnx_g=m_lnx_g, lnx_b=m_lnx_b, conv_w=m_conv_w,
              conv_b=m_conv_b, cln_g=m_cln_g, cln_b=m_cln_b, w_pw2=m_w_pw2, b_pw2=m_b_pw2, w_out=m_w_out,
              norm_post_g=m_norm_post_g)
    vs = dict(norm_pre_g=v_norm_pre_g, w_in=v_w_in, mu_shift=v_mu_shift, w0=v_w0, w_lora_up=v_w_lora_up, a0=v_a0,
              a_lora_up=v_a_lora_up, k_k=v_k_k, k_a=v_k_a, r_k=v_r_k, lnx_g=v_lnx_g, lnx_b=v_lnx_b, conv_w=v_conv_w,
              conv_b=v_conv_b, cln_g=v_cln_g, cln_b=v_cln_b, w_pw2=v_w_pw2, b_pw2=v_b_pw2, w_out=v_w_out,
              norm_post_g=v_norm_post_g)
    names = list(weights)
    big = ["w_in", "w_out", "w_pw2"]
    deltas, new_m, new_v = {}, {}, {}
    d_t, m_t, v_t = _adamw("adamw_w_in", w_in_t, grad_w_in, m_w_in_t, v_w_in_t)
    grads["w_in"], deltas["w_in"], new_m["w_in"], new_v["w_in"] = grad_w_in.T, d_t.T, m_t.T, v_t.T
    for nm in big[1:]:
        deltas[nm], new_m[nm], new_v[nm] = _adamw("adamw_" + nm, weights[nm], grads[nm], ms[nm], vs[nm])
    rest = [nm for nm in names if nm not in big]
    sizes = [weights[nm].size for nm in rest]
    total = sum(sizes)
    width = 4 * LANES
    rows_p = -(-total // (width * SUBLANES)) * SUBLANES

    def pack(d):
        flat = jnp.concatenate([d[nm].reshape(-1) for nm in rest])
        return jnp.pad(flat, (0, rows_p * width - total)).reshape(rows_p, width)

    p_d, p_m, p_v = _adamw("adamw_small", pack(weights), pack(grads), pack(ms), pack(vs))
    o = 0
    for nm, sz in zip(rest, sizes):
        shp = weights[nm].shape
        deltas[nm] = p_d.reshape(-1)[o:o + sz].reshape(shp)
        new_m[nm] = p_m.reshape(-1)[o:o + sz].reshape(shp)
        new_v[nm] = p_v.reshape(-1)[o:o + sz].reshape(shp)
        o += sz

    loss = lax.psum(loss_part[0, 0], ("x", "y", "c"))
    grad_x = grad_x2[None]
    return (loss, grad_x, *[grads[nm] for nm in names], *[deltas[nm] for nm in names],
            *[new_m[nm] for nm in names], *[new_v[nm] for nm in names])
```

```python
import functools

import jax
import jax.numpy as jnp
from jax import lax
from jax.experimental import pallas as pl
from jax.experimental.pallas import tpu as pltpu

f32 = jnp.float32
bf16 = jnp.bfloat16
MESH = pl.DeviceIdType.MESH

NORM_EPS = 1e-6
LN_EPS = 1e-5
ADAM_LR, ADAM_B1, ADAM_B2, ADAM_EPS, ADAM_WD, ADAM_STEP = 0.001, 0.9, 0.999, 1e-08, 0.01, 10

LANES = 128
SUBLANES = 8
LORA_PAD = 256
CONV_HALO = 32
WKV_CHUNK = 64
WKV_HEADS = 16
WKV_FWD_HEADS = 32
WKV_STATE_HEADS = 32
WKV_PREC = lax.Precision.HIGH
SHARE_CHUNKS = 8
VMEM_LIMIT = 56 * 1024 * 1024


def _cparams(n_axes):
    return pltpu.CompilerParams(dimension_semantics=("arbitrary",) * n_axes, vmem_limit_bytes=VMEM_LIMIT)


def _tile(dim, target):
    best = None
    t = LANES
    while t <= min(dim, target):
        if dim % t == 0:
            best = t
        t += LANES
    return best if best is not None else dim


def _mm(a, b, prec=None):
    return lax.dot_general(a, b, (((1,), (0,)), ((), ())), precision=prec, preferred_element_type=f32)


def _mm_nt(a, b, prec=None):
    return lax.dot_general(a, b, (((1,), (1,)), ((), ())), precision=prec, preferred_element_type=f32)


def _mm_tn(a, b, prec=None):
    return lax.dot_general(a, b, (((0,), (0,)), ((), ())), precision=prec, preferred_element_type=f32)


@jax.custom_vjp
def _bmm(a, b):
    return _mm(a.astype(bf16), b.astype(bf16))


def _bmm_fwd(a, b):
    return _bmm(a, b), (a, b)


def _bmm_bwd(res, dc):
    a, b = res
    dcb = dc.astype(bf16)
    return _mm_nt(dcb, b.astype(bf16)), _mm_tn(a.astype(bf16), dcb)


_bmm.defvjp(_bmm_fwd, _bmm_bwd)


@jax.custom_vjp
def _bmm_nt(a, b):
    return _mm_nt(a.astype(bf16), b.astype(bf16))


def _bmm_nt_fwd(a, b):
    return _bmm_nt(a, b), (a, b)


def _bmm_nt_bwd(res, dc):
    a, b = res
    dcb = dc.astype(bf16)
    return _mm(dcb, b.astype(bf16)), _mm_tn(dcb, a.astype(bf16))


_bmm_nt.defvjp(_bmm_nt_fwd, _bmm_nt_bwd)


@jax.custom_vjp
def _bmm_tn(a, b):
    return _mm_tn(a.astype(bf16), b.astype(bf16))


def _bmm_tn_fwd(a, b):
    return _bmm_tn(a, b), (a, b)


def _bmm_tn_bwd(res, dc):
    a, b = res
    dcb = dc.astype(bf16)
    return _mm_nt(b.astype(bf16), dcb), _mm(a.astype(bf16), dcb)


_bmm_tn.defvjp(_bmm_tn_fwd, _bmm_tn_bwd)


def _matmul(name, a, b, mode, out_dtype, tm_t=1024, tn_t=1024, tk_t=4096, comm=None):
    if mode == "nn":
        (M, K), (_, N) = a.shape, b.shape
    elif mode == "nt":
        (M, K), (N, _) = a.shape, b.shape
    else:
        (K, M), (_, N) = a.shape, b.shape
    tm, tn, tk = _tile(M, tm_t), _tile(N, tn_t), _tile(K, tk_t)
    ni, nj, nk = M // tm, N // tn, K // tk
    dot = {"nn": _mm, "nt": _mm_nt, "tn": _mm_tn}[mode]
    nc = len(comm.operands) if comm else 0

    def body(*refs):
        a_ref, b_ref = refs[:2]
        o_ref = refs[2 + nc]
        scratch = refs[3 + 2 * nc:]
        i, j, k = pl.program_id(0), pl.program_id(1), pl.program_id(2)
        if comm:
            comm_refs = (refs[2:2 + nc], refs[3 + nc:3 + 2 * nc], scratch[:len(comm.scratch)])

            @pl.when(jnp.logical_and(jnp.logical_and(i == 0, j == 0), k == 0))
            def _():
                comm.start(*comm_refs)

        if nk == 1:
            o_ref[...] = dot(a_ref[...], b_ref[...]).astype(o_ref.dtype)
        else:
            acc_ref = scratch[-1]

            @pl.when(k == 0)
            def _():
                acc_ref[...] = jnp.zeros_like(acc_ref)

            acc_ref[...] += dot(a_ref[...], b_ref[...])

            @pl.when(k == nk - 1)
            def _():
                o_ref[...] = acc_ref[...].astype(o_ref.dtype)

        if comm:
            @pl.when(jnp.logical_and(jnp.logical_and(i == ni - 1, j == nj - 1), k == nk - 1))
            def _():
                comm.wait(*comm_refs)

    a_spec = {"nn": pl.BlockSpec((tm, tk), lambda i, j, k: (i, k)),
              "nt": pl.BlockSpec((tm, tk), lambda i, j, k: (i, k)),
              "tn": pl.BlockSpec((tk, tm), lambda i, j, k: (k, i))}[mode]
    b_spec = {"nn": pl.BlockSpec((tk, tn), lambda i, j, k: (k, j)),
              "nt": pl.BlockSpec((tn, tk), lambda i, j, k: (j, k)),
              "tn": pl.BlockSpec((tk, tn), lambda i, j, k: (k, j))}[mode]
    res = pl.pallas_call(
        body, name=name, grid=(ni, nj, nk),
        in_specs=[a_spec, b_spec] + [ANY] * nc,
        out_specs=[pl.BlockSpec((tm, tn), lambda i, j, k: (i, j))] + [ANY] * nc,
        out_shape=[jax.ShapeDtypeStruct((M, N), out_dtype)] + (list(comm.out_shape) if comm else []),
        scratch_shapes=(list(comm.scratch) if comm else []) + ([pltpu.VMEM((tm, tn), f32)] if nk > 1 else []),
        compiler_params=_cparams(3),
    )(a, b, *(comm.operands if comm else []))
    return (res[0], res[1:]) if comm else res[0]


def _row_spec(op, tm, ncol):
    arr, off, width, tiled = op
    if tiled:
        cw = width // ncol
        return pl.BlockSpec((tm, cw), lambda j, i: (i, off // cw + j))
    return pl.BlockSpec((tm, width), lambda j, i: (i, off // width))


def _param_spec(p, ncol):
    arr, tiled = p
    rows, width = arr.shape
    if tiled:
        return pl.BlockSpec((rows, width // ncol), lambda j, i: (0, j))
    return pl.BlockSpec((rows, width), lambda j, i: (0, 0))


def _row_fwd(name, fn, params, rows, outs, T, tm, ncol=1):
    npar, nrow = len(params), len(rows)

    def body(*refs):
        pv = [r[...] for r in refs[:npar]]
        rv = [r[...].astype(f32) for r in refs[npar:npar + nrow]]
        res = fn(*pv, *rv)
        for o_ref, val in zip(refs[npar + nrow:], res):
            o_ref[...] = val.astype(o_ref.dtype)

    return pl.pallas_call(
        body, name=name, grid=(ncol, T // tm),
        in_specs=[_param_spec(p, ncol) for p in params] + [_row_spec(r, tm, ncol) for r in rows],
        out_specs=[pl.BlockSpec((tm, w // ncol), lambda j, i: (i, j)) for w, _ in outs],
        out_shape=[jax.ShapeDtypeStruct((T, w), dt) for w, dt in outs],
        compiler_params=_cparams(2),
    )(*[p[0] for p in params], *[r[0] for r in rows])


def _row_bwd(name, fn, params, rows, cots, row_grads, T, tm, ncol=1):
    npar, nrow, ncot = len(params), len(rows), len(cots)
    want = [k for k, dt in enumerate(row_grads) if dt is not None]

    def body(*refs):
        pv = [r[...] for r in refs[:npar]]
        rv = [r[...].astype(f32) for r in refs[npar:npar + nrow]]
        cv = tuple(r[...].astype(f32) for r in refs[npar + nrow:npar + nrow + ncot])
        out_refs = refs[npar + nrow + ncot:]
        _, vjp = jax.vjp(fn, *pv, *rv)
        grads = vjp(cv)
        for o_ref, k in zip(out_refs[:len(want)], want):
            o_ref[...] = grads[npar + k].astype(o_ref.dtype)
        j, i = pl.program_id(0), pl.program_id(1)
        for o_ref, p, g in zip(out_refs[len(want):], params, grads[:npar]):
            first = (i == 0) if p[1] else jnp.logical_and(i == 0, j == 0)

            @pl.when(first)
            def _():
                o_ref[...] = jnp.zeros_like(o_ref)

            o_ref[...] += g

    def grad_spec(op):
        arr, off, width, tiled = op
        if tiled:
            return pl.BlockSpec((tm, width // ncol), lambda j, i: (i, j)), (T, width)
        return pl.BlockSpec((tm, width), lambda j, i: (i, j)), (T, width * ncol)

    gspecs = [grad_spec(rows[k]) for k in want]
    return pl.pallas_call(
        body, name=name, grid=(ncol, T // tm),
        in_specs=[_param_spec(p, ncol) for p in params] + [_row_spec(r, tm, ncol) for r in rows]
        + [_row_spec(c, tm, ncol) for c in cots],
        out_specs=[s for s, _ in gspecs] + [_param_spec(p, ncol) for p in params],
        out_shape=[jax.ShapeDtypeStruct(shp, row_grads[k]) for (_, shp), k in zip(gspecs, want)]
        + [jax.ShapeDtypeStruct(p[0].shape, f32) for p in params],
        compiler_params=_cparams(2),
    )(*[p[0] for p in params], *[r[0] for r in rows], *[c[0] for c in cots])


def _seg_sum(x, head):
    li = lax.broadcasted_iota(jnp.int32, (LANES, LANES), 0) // head
    lj = lax.broadcasted_iota(jnp.int32, (LANES, LANES), 1) // head
    q = (li == lj).astype(f32)
    parts = [_mm(x[:, s:s + LANES], q, lax.Precision.HIGH) for s in range(0, x.shape[1], LANES)]
    return parts[0] if len(parts) == 1 else jnp.concatenate(parts, axis=1)


def _sigmoid(z):
    return 1.0 / (1.0 + jnp.exp(-z))


def _silu(z):
    return z * _sigmoid(z)


def _rms(g, x):
    return x * lax.rsqrt(jnp.mean(x * x, axis=-1, keepdims=True) + NORM_EPS) * g


def _fn_rms_pre(g, x):
    return (_rms(g, x),)


def _fn_lora(w0, wup, a0, aup, xl):
    qw = w0 + _bmm(jnp.tanh(xl), wup)
    qa = a0 + _bmm(xl, aup)
    return qw, qa


def _fn_rwkv_pre(head, k_k, k_a, xk, qw, qa):
    w_log = -(jnp.maximum(-qw, 0.0) + jnp.log(1.0 + jnp.exp(-jnp.abs(qw)))) - 0.5
    lw = -jnp.exp(w_log)
    a_sig = _sigmoid(qa)
    kk = xk * k_k
    kk = kk / jnp.maximum(jnp.sqrt(_seg_sum(kk * kk, head)), 1e-12)
    k_h = xk * (1.0 + (a_sig - 1.0) * k_a)
    return lw, k_h, -kk, kk * a_sig


def _fn_rwkv_post(head, lnx_g, lnx_b, r_k, y, r, k_h, v, g):
    inv = 1.0 / head
    mu = _seg_sum(y, head) * inv
    d = y - mu
    var = _seg_sum(d * d, head) * inv
    yn = d * lax.rsqrt(var + 1e-5 * head) * lnx_g + lnx_b
    bonus = _seg_sum(r * k_h * r_k, head) * v
    return ((yn + bonus) * _silu(g),)


def _fn_conv_ln(cln_g, cln_b, c):
    mu = jnp.mean(c, axis=-1, keepdims=True)
    d = c - mu
    var = jnp.mean(d * d, axis=-1, keepdims=True)
    return (_silu(d * lax.rsqrt(var + LN_EPS) * cln_g + cln_b),)


def _fn_conv_post(b_pw2, c2, g):
    return ((c2 + b_pw2) * _silu(g),)


def _post(out, x, tgt, g, T, D, tm):
    def body(g_ref, o_ref, x_ref, t_ref, dout_ref, gx_ref, loss_ref, dg_ref):
        i = pl.program_id(0)
        o, vjp = jax.vjp(_rms, g_ref[...], o_ref[...])
        err = x_ref[...] + o - t_ref[...]
        d_y = err * (1.0 / D)
        dg, d_out = vjp(d_y)
        dout_ref[...] = d_out.astype(dout_ref.dtype)
        gx_ref[...] = d_y

        @pl.when(i == 0)
        def _():
            loss_ref[...] = jnp.zeros_like(loss_ref)
            dg_ref[...] = jnp.zeros_like(dg_ref)

        loss_ref[...] += jnp.sum(err * err, keepdims=True) * (0.5 / D)
        dg_ref[...] += dg

    row = pl.BlockSpec((tm, D), lambda i: (i, 0))
    vec = pl.BlockSpec((1, D), lambda i: (0, 0))
    return pl.pallas_call(
        body, name="post_loss", grid=(T // tm,),
        in_specs=[vec, row, row, row],
        out_specs=[row, row, pl.BlockSpec((1, 1), lambda i: (0, 0)), vec],
        out_shape=[jax.ShapeDtypeStruct((T, D), bf16), jax.ShapeDtypeStruct((T, D), f32),
                   jax.ShapeDtypeStruct((1, 1), f32), jax.ShapeDtypeStruct((1, D), f32)],
        compiler_params=_cparams(1),
    )(g, out, x, tgt)


def _rms_pre_bwd(x, g, dh, gx_res, T, D, tm):
    def body(g_ref, x_ref, dh_ref, res_ref, dx_ref, dg_ref):
        i = pl.program_id(0)
        _, vjp = jax.vjp(_rms, g_ref[...], x_ref[...])
        dg, dx = vjp(dh_ref[...].astype(f32))
        dx_ref[...] = dx + res_ref[...]

        @pl.when(i == 0)
        def _():
            dg_ref[...] = jnp.zeros_like(dg_ref)

        dg_ref[...] += dg

    row = pl.BlockSpec((tm, D), lambda i: (i, 0))
    vec = pl.BlockSpec((1, D), lambda i: (0, 0))
    return pl.pallas_call(
        body, name="rms_pre_bwd", grid=(T // tm,),
        in_specs=[vec, row, row, row], out_specs=[row, vec],
        out_shape=[jax.ShapeDtypeStruct((T, D), f32), jax.ShapeDtypeStruct((1, D), f32)],
        compiler_params=_cparams(1),
    )(g, x, dh, gx_res)


def _prev_rows(cur, halo_ref, first):
    top = jnp.where(first, 0.0, halo_ref[SUBLANES - 1:SUBLANES, :])
    rolled = pltpu.roll(cur, 1, 0)
    rid = lax.broadcasted_iota(jnp.int32, cur.shape, 0)
    return jnp.where(rid == 0, top, rolled)


def _shift_fwd(name, proj, off, width, mu, T, tm):
    cw = _tile(width, 512)
    ncol, cb = width // cw, off // cw
    hb = tm // SUBLANES

    def body(mu_ref, cur_ref, halo_ref, o_ref):
        i = pl.program_id(1)
        cur = cur_ref[...]
        prev = _prev_rows(cur, halo_ref, i == 0)
        o_ref[...] = cur + (prev - cur) * mu_ref[...]

    return pl.pallas_call(
        body, name=name, grid=(ncol, T // tm),
        in_specs=[pl.BlockSpec((1, cw), lambda j, i: (0, j)),
                  pl.BlockSpec((tm, cw), lambda j, i: (i, cb + j)),
                  pl.BlockSpec((SUBLANES, cw), lambda j, i: (jnp.maximum(i * hb - 1, 0), cb + j))],
        out_specs=pl.BlockSpec((tm, cw), lambda j, i: (i, j)),
        out_shape=jax.ShapeDtypeStruct((T, width), f32),
        compiler_params=_cparams(2),
    )(mu, proj, proj)


def _shift_bwd(name, proj, off, width, mu, dxs, T, tm):
    cw = _tile(width, 512)
    ncol, cb = width // cw, off // cw
    hb, nt = tm // SUBLANES, T // tm

    def body(mu_ref, cur_ref, halo_ref, d_ref, dnext_ref, o_ref, dmu_ref):
        i = pl.program_id(1)
        cur = cur_ref[...]
        prev = _prev_rows(cur, halo_ref, i == 0)
        d = d_ref[...]
        bottom = jnp.where(i == nt - 1, 0.0, dnext_ref[0:1, :])
        rid = lax.broadcasted_iota(jnp.int32, d.shape, 0)
        d_next = jnp.where(rid == tm - 1, bottom, pltpu.roll(d, tm - 1, 0))
        mu_v = mu_ref[...]
        o_ref[...] = (d * (1.0 - mu_v) + d_next * mu_v).astype(o_ref.dtype)

        @pl.when(i == 0)
        def _():
            dmu_ref[...] = jnp.zeros_like(dmu_ref)

        dmu_ref[...] += jnp.sum(d * (prev - cur), axis=0, keepdims=True)

    return pl.pallas_call(
        body, name=name, grid=(ncol, nt),
        in_specs=[pl.BlockSpec((1, cw), lambda j, i: (0, j)),
                  pl.BlockSpec((tm, cw), lambda j, i: (i, cb + j)),
                  pl.BlockSpec((SUBLANES, cw), lambda j, i: (jnp.maximum(i * hb - 1, 0), cb + j)),
                  pl.BlockSpec((tm, cw), lambda j, i: (i, j)),
                  pl.BlockSpec((SUBLANES, cw), lambda j, i: (jnp.minimum((i + 1) * hb, nt * hb - 1), j))],
        out_specs=[pl.BlockSpec((tm, cw), lambda j, i: (i, j)), pl.BlockSpec((1, cw), lambda j, i: (0, j))],
        out_shape=[jax.ShapeDtypeStruct((T, width), bf16), jax.ShapeDtypeStruct((1, width), f32)],
        compiler_params=_cparams(2),
    )(mu, proj, proj, dxs, dxs)


def _rolled_copies(dst_ref, ext):
    n = ext.shape[0]
    dst_ref[0] = ext
    for r in range(1, SUBLANES):
        dst_ref[r] = pltpu.roll(ext, n - r, 0)


def _window(rolled_ref, start, rows):
    q, r = divmod(start, SUBLANES)
    return rolled_ref[r, pl.ds(SUBLANES * q, rows), :]


def _conv_fwd(proj, off_v, off_g, width, conv_w, conv_b, ktaps, T, tm):
    cw = _tile(width, 512)
    ncol = width // cw
    hb = tm // CONV_HALO
    lead = CONV_HALO - (ktaps - 1)

    def body(w_ref, b_ref, v_ref, g_ref, hv_ref, hg_ref, o_ref, u_ref):
        i = pl.program_id(1)
        halo = hv_ref[...] * _sigmoid(hg_ref[...])
        _rolled_copies(u_ref, jnp.concatenate([jnp.where(i == 0, 0.0, halo), v_ref[...] * _sigmoid(g_ref[...])], axis=0))
        acc = jnp.zeros((tm, cw), f32) + b_ref[...]
        for j in range(ktaps):
            acc = acc + _window(u_ref, lead + j, tm) * w_ref[j:j + 1, :]
        o_ref[...] = acc

    def tile(off):
        return pl.BlockSpec((tm, cw), lambda j, i: (i, off // cw + j))

    def halo(off):
        return pl.BlockSpec((CONV_HALO, cw), lambda j, i: (jnp.maximum(i * hb - 1, 0), off // cw + j))

    return pl.pallas_call(
        body, name="conv_fwd", grid=(ncol, T // tm),
        in_specs=[pl.BlockSpec((CONV_HALO, cw), lambda j, i: (0, j)), pl.BlockSpec((1, cw), lambda j, i: (0, j)),
                  tile(off_v), tile(off_g), halo(off_v), halo(off_g)],
        out_specs=pl.BlockSpec((tm, cw), lambda j, i: (i, j)),
        out_shape=jax.ShapeDtypeStruct((T, width), f32),
        scratch_shapes=[pltpu.VMEM((SUBLANES, CONV_HALO + tm, cw), f32)],
        compiler_params=_cparams(2),
    )(conv_w, conv_b, proj, proj, proj, proj)


def _conv_bwd(proj, off_v, off_g, width, conv_w, dc, ktaps, T, tm):
    cw = _tile(width, 512)
    ncol = width // cw
    hb, nt = tm // CONV_HALO, T // tm
    lead = CONV_HALO - (ktaps - 1)

    def body(w_ref, v_ref, g_ref, hv_ref, hg_ref, dc_ref, dcn_ref, dv_ref, dg_ref, dw_ref, db_ref, u_ref, d_ref):
        i = pl.program_id(1)
        halo = hv_ref[...] * _sigmoid(hg_ref[...])
        sig = _sigmoid(g_ref[...])
        gv = v_ref[...]
        _rolled_copies(u_ref, jnp.concatenate([jnp.where(i == 0, 0.0, halo), gv * sig], axis=0))
        dcur = dc_ref[...]
        _rolled_copies(d_ref, jnp.concatenate([dcur, jnp.where(i == nt - 1, 0.0, dcn_ref[...])], axis=0))

        @pl.when(i == 0)
        def _():
            dw_ref[...] = jnp.zeros_like(dw_ref)
            db_ref[...] = jnp.zeros_like(db_ref)

        du = jnp.zeros((tm, cw), f32)
        for j in range(ktaps):
            du = du + _window(d_ref, ktaps - 1 - j, tm) * w_ref[j:j + 1, :]
            dw_ref[j:j + 1, :] += jnp.sum(_window(u_ref, lead + j, tm) * dcur, axis=0, keepdims=True)
        db_ref[...] += jnp.sum(dcur, axis=0, keepdims=True)
        dv_ref[...] = (du * sig).astype(dv_ref.dtype)
        dg_ref[...] = (du * gv * sig * (1.0 - sig)).astype(dg_ref.dtype)

    def tile(off):
        return pl.BlockSpec((tm, cw), lambda j, i: (i, off // cw + j))

    def halo(off):
        return pl.BlockSpec((CONV_HALO, cw), lambda j, i: (jnp.maximum(i * hb - 1, 0), off // cw + j))

    return pl.pallas_call(
        body, name="conv_bwd", grid=(ncol, nt),
        in_specs=[pl.BlockSpec((CONV_HALO, cw), lambda j, i: (0, j)),
                  tile(off_v), tile(off_g), halo(off_v), halo(off_g),
                  pl.BlockSpec((tm, cw), lambda j, i: (i, j)),
                  pl.BlockSpec((CONV_HALO, cw), lambda j, i: (jnp.minimum((i + 1) * hb, nt * hb - 1), j))],
        out_specs=[pl.BlockSpec((tm, cw), lambda j, i: (i, j)), pl.BlockSpec((tm, cw), lambda j, i: (i, j)),
                   pl.BlockSpec((CONV_HALO, cw), lambda j, i: (0, j)), pl.BlockSpec((1, cw), lambda j, i: (0, j))],
        out_shape=[jax.ShapeDtypeStruct((T, width), bf16), jax.ShapeDtypeStruct((T, width), bf16),
                   jax.ShapeDtypeStruct((CONV_HALO, width), f32), jax.ShapeDtypeStruct((1, width), f32)],
        scratch_shapes=[pltpu.VMEM((SUBLANES, CONV_HALO + tm, cw), f32), pltpu.VMEM((SUBLANES, tm + CONV_HALO, cw), f32)],
        compiler_params=_cparams(2),
    )(conv_w, proj, proj, proj, proj, dc, dc)


def _each(f, *lists):
    return [f(*xs) for xs in zip(*lists)]


def _wkv_local(r, lw, k, v, a, b):
    C = r[0].shape[0]
    P = WKV_PREC
    row = lax.broadcasted_iota(jnp.int32, (C, C), 0)
    col = lax.broadcasted_iota(jnp.int32, (C, C), 1)
    incl, strict = row >= col, row > col
    tri = incl.astype(f32)
    zero = jnp.zeros((C, C), f32)
    G = _each(lambda x: _mm(tri, x, P), lw)
    to_end = _each(lambda x, g: jnp.exp(jnp.sum(x, axis=0, keepdims=True) - g), lw, G)
    e_g = _each(jnp.exp, G)
    e_ng = _each(lambda g: jnp.exp(-g), G)
    At = _each(lambda x, g, w: x * jnp.exp(g - w), a, G, lw)
    Rt = _each(jnp.multiply, r, e_g)
    Kt = _each(jnp.multiply, k, e_ng)
    Bt = _each(jnp.multiply, b, e_ng)
    sc = _each(lambda at, rt, bt, kt: _mm_nt(jnp.concatenate([at, rt], axis=0), jnp.concatenate([bt, kt], axis=0), P),
               At, Rt, Bt, Kt)
    L = _each(lambda s: jnp.where(strict, s[:C, :C], zero), sc)
    M = _each(lambda s: jnp.where(strict, s[:C, C:], zero), sc)
    Pb = _each(lambda s: jnp.where(incl, s[C:, :C], zero), sc)
    Pk = _each(lambda s: jnp.where(incl, s[C:, C:], zero), sc)
    MPk = _each(lambda m, p, x: _bmm(jnp.concatenate([m, p], axis=0), x), M, Pk, v)
    WU = _each(lambda at, mp: jnp.concatenate([at, mp[:C]], axis=1), At, MPk)
    Lp = L
    n = 1
    while n < C:
        n *= 2
        if n < C:
            step = _each(lambda l, x: _bmm(l, jnp.concatenate([x, l], axis=1)), Lp, WU)
            WU = _each(lambda x, s: x + s[:, :x.shape[1]], WU, step)
            Lp = _each(lambda x, s: s[:, x.shape[1]:], WU, step)
        else:
            WU = _each(lambda x, l: x + _bmm(l, x), WU, Lp)
    N = r[0].shape[1]
    W = _each(lambda x: x[:, :N], WU)
    U = _each(lambda x: x[:, N:], WU)
    Y0 = _each(lambda mp: mp[C:], MPk)
    Bend = _each(jnp.multiply, b, to_end)
    Z = _each(lambda x, y, e: _bmm_tn(x, y * e), v, k, to_end)
    return W, U, Rt, Pb, Y0, Bend, Z


def _wkv_state(S0, W, U, Rt, Pb, Bend, lw, Y0, Z):
    C = W[0].shape[0]
    WR = _each(lambda w, rt, s: _bmm_nt(jnp.concatenate([w, rt], axis=0), s), W, Rt, S0)
    X = _each(lambda wr, u: wr[:C] + u, WR, U)
    y = _each(lambda p, x, wr, c: _bmm(p, x) + wr[C:] + c, Pb, X, WR, Y0)
    S1 = _each(lambda s, w, x, e, z: s * jnp.exp(jnp.sum(w, axis=0, keepdims=True)) + _bmm_tn(x, e) + z,
               S0, lw, X, Bend, Z)
    return y, S1


def _wkv_dims(head, T, RW, heads_per_step):
    C = min(WKV_CHUNK, T)
    nh = RW // head
    hb = min(heads_per_step, nh)
    return C, nh, hb, hb * head, T // C


def _heads(ref, hb, head):
    return [ref[:, h * head:(h + 1) * head] for h in range(hb)]


def _put_heads(ref, vals, head):
    for h, val in enumerate(vals):
        ref[:, h * head:(h + 1) * head] = val


def _wkv_local_fwd(r, lw, k, v, a, b, head, T, RW):
    C, nh, hb, bw, nc = _wkv_dims(head, T, RW, WKV_FWD_HEADS)

    def body(*refs):
        ins, outs = refs[:6], refs[6:]
        res = _wkv_local(*[_heads(x, hb, head) for x in ins])
        for o_ref, vals in zip(outs[:6], res[:6]):
            _put_heads(o_ref, vals, head)
        for h in range(hb):
            outs[6][0, h] = res[6][h]

    blk = pl.BlockSpec((C, bw), lambda g, c: (c, g))
    sq = pl.BlockSpec((1, hb, head, head), lambda g, c: (c, g, 0, 0))
    return pl.pallas_call(
        body, name="wkv_local", grid=(nh // hb, nc),
        in_specs=[blk] * 6, out_specs=[blk] * 6 + [sq],
        out_shape=[jax.ShapeDtypeStruct((T, RW), f32)] * 6 + [jax.ShapeDtypeStruct((nc, nh, head, head), f32)],
        compiler_params=_cparams(2),
    )(r, lw, k, v, a, b)


def _wkv_state_fwd(W, U, Rt, Pb, Bend, lw, Y0, Z, head, T, RW):
    C, nh, hb, bw, nc = _wkv_dims(head, T, RW, WKV_STATE_HEADS)

    def body(w_ref, u_ref, rt_ref, pb_ref, be_ref, lw_ref, y0_ref, z_ref, y_ref, st_ref, s_ref):
        @pl.when(pl.program_id(1) == 0)
        def _():
            s_ref[...] = jnp.zeros_like(s_ref)

        S0 = [s_ref[h] for h in range(hb)]
        for h in range(hb):
            st_ref[0, h] = S0[h]
        rows = [_heads(x, hb, head) for x in (w_ref, u_ref, rt_ref, pb_ref, be_ref, lw_ref, y0_ref)]
        y, S1 = _wkv_state(S0, *rows, [z_ref[0, h] for h in range(hb)])
        _put_heads(y_ref, y, head)
        for h in range(hb):
            s_ref[h] = S1[h]

    blk = pl.BlockSpec((C, bw), lambda g, c: (c, g))
    sq = pl.BlockSpec((1, hb, head, head), lambda g, c: (c, g, 0, 0))
    return pl.pallas_call(
        body, name="wkv_state", grid=(nh // hb, nc),
        in_specs=[blk] * 7 + [sq], out_specs=[blk, sq],
        out_shape=[jax.ShapeDtypeStruct((T, RW), f32), jax.ShapeDtypeStruct((nc, nh, head, head), f32)],
        scratch_shapes=[pltpu.VMEM((hb, head, head), f32)],
        compiler_params=_cparams(2),
    )(W, U, Rt, Pb, Bend, lw, Y0, Z)


def _wkv_state_bwd(W, U, Rt, Pb, Bend, lw, Y0, Z, states, dy, head, T, RW):
    C, nh, hb, bw, nc = _wkv_dims(head, T, RW, WKV_STATE_HEADS)

    def body(w_ref, u_ref, rt_ref, pb_ref, be_ref, lw_ref, y0_ref, z_ref, st_ref, dy_ref,
             dw_ref, du_ref, drt_ref, dpb_ref, dbe_ref, dlw_ref, dz_ref, ds_ref):
        @pl.when(pl.program_id(1) == 0)
        def _():
            ds_ref[...] = jnp.zeros_like(ds_ref)

        dS1 = [ds_ref[h] for h in range(hb)]
        for h in range(hb):
            dz_ref[0, h] = dS1[h]
        rows = [_heads(x, hb, head) for x in (w_ref, u_ref, rt_ref, pb_ref, be_ref, lw_ref)]
        Y0 = _heads(y0_ref, hb, head)
        Zs = [z_ref[0, h] for h in range(hb)]
        _, vjp = jax.vjp(lambda s0, *rw: _wkv_state(s0, *rw, Y0, Zs), [st_ref[0, h] for h in range(hb)], *rows)
        grads = vjp((_heads(dy_ref, hb, head), dS1))
        for o_ref, vals in zip((dw_ref, du_ref, drt_ref, dpb_ref, dbe_ref, dlw_ref), grads[1:]):
            _put_heads(o_ref, vals, head)
        for h in range(hb):
            ds_ref[h] = grads[0][h]

    blk = pl.BlockSpec((C, bw), lambda g, c: (nc - 1 - c, g))
    sq = pl.BlockSpec((1, hb, head, head), lambda g, c: (nc - 1 - c, g, 0, 0))
    return pl.pallas_call(
        body, name="wkv_state_bwd", grid=(nh // hb, nc),
        in_specs=[blk] * 7 + [sq, sq, blk], out_specs=[blk] * 6 + [sq],
        out_shape=[jax.ShapeDtypeStruct((T, RW), f32)] * 6 + [jax.ShapeDtypeStruct((nc, nh, head, head), f32)],
        scratch_shapes=[pltpu.VMEM((hb, head, head), f32)],
        compiler_params=_cparams(2),
    )(W, U, Rt, Pb, Bend, lw, Y0, Z, states, dy)


def _wkv_local_bwd(r, lw, k, v, a, b, cots, d_lw_x, dr_x, dk_x, dv_x, head, T, RW):
    C, nh, hb, bw, nc = _wkv_dims(head, T, RW, WKV_HEADS)

    def body(*refs):
        ins, cot_refs, add_refs, outs = refs[:6], refs[6:13], refs[13:17], refs[17:]
        _, vjp = jax.vjp(_wkv_local, *[_heads(x, hb, head) for x in ins])
        cts = [_heads(x, hb, head) for x in cot_refs[:6]] + [[cot_refs[6][0, h] for h in range(hb)]]
        dr, dlw, dk, dv, da, db = vjp(tuple(cts))
        dlw_x, drx, dkx, dvx = [_heads(x, hb, head) for x in add_refs]
        _put_heads(outs[0], _each(jnp.add, dr, drx), head)
        _put_heads(outs[1], _each(jnp.add, dlw, dlw_x), head)
        _put_heads(outs[2], _each(jnp.add, dk, dkx), head)
        _put_heads(outs[3], _each(jnp.add, dv, dvx), head)
        _put_heads(outs[4], da, head)
        _put_heads(outs[5], db, head)

    blk = pl.BlockSpec((C, bw), lambda g, c: (c, g))
    sq = pl.BlockSpec((1, hb, head, head), lambda g, c: (c, g, 0, 0))
    return pl.pallas_call(
        body, name="wkv_local_bwd", grid=(nh // hb, nc),
        in_specs=[blk] * 12 + [sq] + [blk] * 4, out_specs=[blk] * 6,
        out_shape=[jax.ShapeDtypeStruct((T, RW), f32)] * 6,
        compiler_params=_cparams(2),
    )(r, lw, k, v, a, b, *cots, d_lw_x, dr_x, dk_x, dv_x)


def _rows_tile(R, row_bytes, budget, mult=SUBLANES):
    best = None
    t = mult
    while t <= R:
        if R % t == 0 and t * row_bytes <= budget:
            best = t
        t += mult
    return best if best is not None else R


def _sum_slots(name, parts):
    S, R, W = parts.shape
    budget = 4 << 20
    tr = _rows_tile(R, S * W * 4, budget, 2 * SUBLANES)
    cw = W if tr * S * W * 4 <= 2 * budget else _tile(W, max(LANES, 2 * budget // (S * tr * 4)))

    def body(p_ref, o_ref):
        acc = p_ref[0].astype(f32)
        for d in range(1, S):
            acc = acc + p_ref[d].astype(f32)
        o_ref[...] = acc

    return pl.pallas_call(
        body, name=name, grid=(R // tr, W // cw),
        in_specs=[pl.BlockSpec((S, tr, cw), lambda i, j: (0, i, j))],
        out_specs=pl.BlockSpec((tr, cw), lambda i, j: (i, j)),
        out_shape=jax.ShapeDtypeStruct((R, W), f32),
        compiler_params=_cparams(2),
    )(parts)


def _tile2d(R, W, budget, mult):
    tr = _rows_tile(R, LANES * 4, budget, mult)
    cw = _tile(W, max(LANES, budget // (tr * 4))) if W % LANES == 0 else W
    return tr, cw


def _core_index():
    return lax.axis_index("c").astype(jnp.int32).reshape(1)


def _add_kept(name, shards, got):
    _, _, R, W = shards.shape
    tr, cw = _tile2d(R, W, 2 << 20, 2 * SUBLANES)

    def body(core_ref, a_ref, b_ref, o_ref):
        o_ref[...] = (a_ref[...].astype(f32) + b_ref[...].astype(f32)).astype(o_ref.dtype)

    return pl.pallas_call(
        body, name=name,
        grid_spec=pltpu.PrefetchScalarGridSpec(
            num_scalar_prefetch=1, grid=(4, R // tr, W // cw),
            in_specs=[pl.BlockSpec((None, None, tr, cw), lambda s, i, j, core: (s, core[0], i, j)),
                      pl.BlockSpec((None, tr, cw), lambda s, i, j, core: (s, i, j))],
            out_specs=pl.BlockSpec((None, tr, cw), lambda s, i, j, core: (s, i, j))),
        out_shape=jax.ShapeDtypeStruct((4, R, W), shards.dtype), compiler_params=_cparams(3),
    )(_core_index(), shards, got)


def _sum_slots_half(name, parts):
    S, R, W = parts.shape
    budget = 4 << 20
    tr = _rows_tile(R, S * W * 4, budget, 2 * SUBLANES)
    cw = W if tr * S * W * 4 <= 2 * budget else _tile(W, max(LANES, 2 * budget // (S * tr * 4)))
    nrow = R // tr

    def body(core_ref, p_ref, o_ref):
        acc = p_ref[0].astype(f32)
        for d in range(1, S):
            acc = acc + p_ref[d].astype(f32)
        o_ref[...] = acc

    return pl.pallas_call(
        body, name=name,
        grid_spec=pltpu.PrefetchScalarGridSpec(
            num_scalar_prefetch=1, grid=(nrow, W // cw),
            in_specs=[pl.BlockSpec((S, tr, cw), lambda i, j, core: (0, i, j))],
            out_specs=pl.BlockSpec((tr, cw), lambda i, j, core: (core[0] * nrow + i, j))),
        out_shape=jax.ShapeDtypeStruct((2 * R, W), f32), compiler_params=_cparams(2),
    )(_core_index(), parts)


def _adamw(name, w, g, m, v):
    R, W = w.shape
    tr, cw = _tile2d(R, W, 2 << 20, SUBLANES)

    def body(w_ref, g_ref, m_ref, v_ref, d_ref, nm_ref, nv_ref):
        g_v = g_ref[...]
        nm = ADAM_B1 * m_ref[...] + (1.0 - ADAM_B1) * g_v
        nv = ADAM_B2 * v_ref[...] + (1.0 - ADAM_B2) * (g_v * g_v)
        m_hat = nm / (1.0 - ADAM_B1 ** ADAM_STEP)
        v_hat = nv / (1.0 - ADAM_B2 ** ADAM_STEP)
        d_ref[...] = -ADAM_LR * (m_hat / (jnp.sqrt(v_hat) + ADAM_EPS) + ADAM_WD * w_ref[...])
        nm_ref[...] = nm
        nv_ref[...] = nv

    blk = pl.BlockSpec((tr, cw), lambda i, j: (i, j))
    return pl.pallas_call(
        body, name=name, grid=(R // tr, W // cw),
        in_specs=[blk] * 4, out_specs=[blk] * 3,
        out_shape=[jax.ShapeDtypeStruct((R, W), f32)] * 3,
        compiler_params=_cparams(2),
    )(w, g, m, v)


ANY = pl.BlockSpec(memory_space=pl.ANY)


def _place():
    return lax.axis_index("x"), lax.axis_index("y"), lax.axis_index("c")


class _Comm:
    def __init__(self, operands, out_shape, scratch, start, wait):
        self.operands, self.out_shape, self.scratch, self.start, self.wait = operands, out_shape, scratch, start, wait


def _run_comm(name, comm):
    n = len(comm.operands)

    def body(*refs):
        parts = (refs[:n], refs[n:2 * n], refs[2 * n:])
        comm.start(*parts)
        comm.wait(*parts)

    return pl.pallas_call(
        body, name=name, in_specs=[ANY] * n, out_specs=[ANY] * n, out_shape=comm.out_shape,
        scratch_shapes=comm.scratch,
    )(*comm.operands)


def _copy_chunks(rows, cols):
    k = SHARE_CHUNKS // 2
    if rows % (k * 2 * SUBLANES) == 0:
        return [(pl.ds(q * (rows // k), rows // k), pl.ds(0, cols)) for q in range(k)]
    if cols % (k * LANES) == 0:
        return [(pl.ds(0, rows), pl.ds(q * (cols // k), cols // k)) for q in range(k)]
    return [(pl.ds(0, rows), pl.ds(0, cols))]


def _gather_chips(arrays, relayed=False, copy_own=True):
    n = len(arrays)
    assert not relayed or all(a.shape[2] % (2 * LANES) == 0 for a in arrays)
    parts = [(a, h, blk) for a, arr in enumerate(arrays) for h in range(2) for blk in _copy_chunks(*arr.shape[1:])]
    parts = parts if copy_own else []

    def copies(ins, outs, sems):
        send_sems, recv_sems, local_sems = sems
        x, y, c = _place()
        mine = 2 * x + y
        sib = (x, y, 1 - c)
        chips = [(1 - x, y), (x, 1 - y), (1 - x, 1 - y)]

        def local():
            return [pltpu.make_async_copy(ins[a].at[(h, *blk)], outs[a].at[(mine, h, *blk)], local_sems.at[p])
                    for p, (a, h, blk) in enumerate(parts)]

        def over_ici(a, j, slot):
            px, py = chips[j]
            return pltpu.make_async_remote_copy(
                src_ref=ins[a].at[c], dst_ref=outs[a].at[slot, c], send_sem=send_sems.at[3 * a + j],
                recv_sem=recv_sems.at[3 * a + j], device_id=(px, py, c), device_id_type=MESH)

        def over_d2d(a, j, half):
            px, py = chips[j]
            slot = 2 * px + py
            return pltpu.make_async_remote_copy(
                src_ref=outs[a].at[slot, half], dst_ref=outs[a].at[slot, half], send_sem=send_sems.at[3 * (n + a) + j],
                recv_sem=recv_sems.at[3 * (n + a) + j], device_id=sib, device_id_type=MESH)

        def relay(a, q, origin):
            ox, oy = origin
            px, py = chips[1 - q]
            rows, cols = arrays[a].shape[1], arrays[a].shape[2] // 2
            win = outs[a].at[2 * ox + oy, c, pl.ds(0, rows), pl.ds(q * cols, cols)]
            return pltpu.make_async_remote_copy(
                src_ref=win, dst_ref=win, send_sem=send_sems.at[6 * n + 2 * a + q],
                recv_sem=recv_sems.at[6 * n + 2 * a + q], device_id=(px, py, c), device_id_type=MESH)

        direct = 2 if relayed else 3
        pairs = [(a, j) for a in range(n) for j in range(direct)]
        return dict(local=local,
                    sends=lambda: [over_ici(a, j, mine) for a, j in pairs],
                    landing=lambda: [over_ici(a, j, 2 * chips[j][0] + chips[j][1]) for a, j in pairs],
                    passed=lambda: [over_d2d(a, j, c) for a, j in pairs],
                    relays=lambda: [relay(a, j, chips[j]) for a, j in pairs],
                    relayed_in=lambda: [relay(a, q, chips[2]) for a in range(n) for q in range(2)],
                    passed_diag=lambda: [over_d2d(a, 2, c) for a in range(n)],
                    from_sib=lambda: [over_d2d(a, j, 1 - c) for a in range(n) for j in range(3)])

    def start(ins, outs, sems):
        cps = copies(ins, outs, sems)
        for cp in cps["local"]() + cps["sends"]():
            cp.start()

    def wait(ins, outs, sems):
        cps = copies(ins, outs, sems)
        passed = cps["passed"]()
        relays = cps["relays"]() if relayed else [None] * len(passed)
        for got, on, via in zip(cps["landing"](), passed, relays):
            got.wait_recv()
            if relayed:
                via.start()
            on.start()
        if relayed:
            for cp in cps["relayed_in"]():
                cp.wait_recv()
            diag = cps["passed_diag"]()
            for cp in diag:
                cp.start()
            passed = passed + diag + relays
        for cp in cps["from_sib"]():
            cp.wait_recv()
        for cp in cps["sends"]() + passed:
            cp.wait_send()
        for cp in cps["local"]():
            cp.wait()

    return _Comm(arrays, [jax.ShapeDtypeStruct((4,) + a.shape, a.dtype) for a in arrays],
                 [pltpu.SemaphoreType.DMA((8 * n,)), pltpu.SemaphoreType.DMA((8 * n,)),
                  pltpu.SemaphoreType.DMA((max(len(parts), 1),))], start, wait)


def _exchange_chips(pieces, whole):
    n, m = len(pieces), len(whole)
    parts = [(a, blk) for a, arr in enumerate(pieces) for blk in _copy_chunks(*arr.shape[1:])]

    def copies(ins, outs, sems):
        send_sems, recv_sems, local_sems = sems
        x, y, c = _place()
        chip, dev = 2 * x + y, 4 * x + 2 * y + c
        chips = [(1 - x, y), (x, 1 - y), (1 - x, 1 - y)]
        peers = [(x ^ (k >> 2), y ^ ((k >> 1) & 1), c ^ (k & 1)) for k in range(1, 8)]
        def local():
            cps = [pltpu.make_async_copy(ins[a].at[(chip, *blk)], outs[a].at[(chip, *blk)], local_sems.at[p])
                   for p, (a, blk) in enumerate(parts)]
            return cps + [pltpu.make_async_copy(ins[n + b], outs[n + b].at[dev], local_sems.at[len(parts) + b])
                          for b in range(m)]

        def piece(a, j, slot_from):
            px, py = chips[j]
            return pltpu.make_async_remote_copy(
                src_ref=ins[a].at[2 * px + py], dst_ref=outs[a].at[slot_from], send_sem=send_sems.at[3 * a + j],
                recv_sem=recv_sems.at[3 * a + j], device_id=(px, py, c), device_id_type=MESH)

        def everyone(b, j, slot_from):
            px, py, pc = peers[j]
            return pltpu.make_async_remote_copy(
                src_ref=ins[n + b], dst_ref=outs[n + b].at[slot_from], send_sem=send_sems.at[3 * n + 7 * b + j],
                recv_sem=recv_sems.at[3 * n + 7 * b + j], device_id=(px, py, pc), device_id_type=MESH)

        def sends():
            return ([everyone(b, j, dev) for b in range(m) for j in range(7)]
                    + [piece(a, j, chip) for a in range(n) for j in range(3)])

        def landing():
            return ([everyone(b, j, 4 * px + 2 * py + pc) for b in range(m) for j, (px, py, pc) in enumerate(peers)]
                    + [piece(a, j, 2 * px + py) for a in range(n) for j, (px, py) in enumerate(chips)])

        return local, sends, landing

    def start(ins, outs, sems):
        local, sends, _ = copies(ins, outs, sems)
        for cp in local() + sends():
            cp.start()

    def wait(ins, outs, sems):
        local, sends, landing = copies(ins, outs, sems)
        for cp in landing():
            cp.wait_recv()
        for cp in sends():
            cp.wait_send()
        for cp in local():
            cp.wait()

    shapes = [jax.ShapeDtypeStruct(a.shape, a.dtype) for a in pieces]
    shapes += [jax.ShapeDtypeStruct((8,) + a.shape, a.dtype) for a in whole]
    nsem = 3 * n + 7 * m
    return _Comm(list(pieces) + list(whole), shapes,
                 [pltpu.SemaphoreType.DMA((nsem,)), pltpu.SemaphoreType.DMA((nsem,)),
                  pltpu.SemaphoreType.DMA((len(parts) + m,))], start, wait)


def _pair_exchange(name, shards):
    n = len(shards)

    def body(*refs):
        ins, outs = refs[:n], refs[n:2 * n]
        send_sems, recv_sems = refs[2 * n:]
        x, y, c = _place()
        copies = [pltpu.make_async_remote_copy(
            src_ref=ins[a].at[s, 1 - c], dst_ref=outs[a].at[s], send_sem=send_sems.at[4 * a + s],
            recv_sem=recv_sems.at[4 * a + s], device_id=(x, y, 1 - c), device_id_type=MESH)
            for a in range(n) for s in range(4)]
        for cp in copies:
            cp.start()
        for cp in copies:
            cp.wait_recv()
        for cp in copies:
            cp.wait_send()

    return pl.pallas_call(
        body, name=name, in_specs=[ANY] * n, out_specs=[ANY] * n,
        out_shape=[jax.ShapeDtypeStruct((4,) + a.shape[2:], a.dtype) for a in shards],
        scratch_shapes=[pltpu.SemaphoreType.DMA((4 * n,)), pltpu.SemaphoreType.DMA((4 * n,))],
    )(*shards)


def _share_halves(name, arrays):
    n = len(arrays)

    def body(*refs):
        bufs = refs[n:2 * n]
        send_sems, recv_sems = refs[2 * n:]
        x, y, c = _place()

        def half(a, h):
            rows = arrays[a].shape[0] // 2
            return bufs[a].at[pl.ds(pl.multiple_of(h * rows, SUBLANES), rows)]

        def copy(a, h):
            return pltpu.make_async_remote_copy(
                src_ref=half(a, h), dst_ref=half(a, h), send_sem=send_sems.at[a], recv_sem=recv_sems.at[a],
                device_id=(x, y, 1 - c), device_id_type=MESH)

        sends = [copy(a, c) for a in range(n)]
        for cp in sends:
            cp.start()
        for a in range(n):
            copy(a, 1 - c).wait_recv()
        for cp in sends:
            cp.wait_send()

    return pl.pallas_call(
        body, name=name, in_specs=[ANY] * n, out_specs=[ANY] * n,
        out_shape=[jax.ShapeDtypeStruct(a.shape, a.dtype) for a in arrays],
        input_output_aliases={a: a for a in range(n)},
        scratch_shapes=[pltpu.SemaphoreType.DMA((n,)), pltpu.SemaphoreType.DMA((n,))],
    )(*arrays)


def _place_blocks(blocks, axis):
    shape = list(blocks[0].shape)
    shape[axis] = sum(b.shape[axis] for b in blocks)
    buf = lax.empty(tuple(shape), blocks[0].dtype)
    at = 0
    for b in blocks:
        buf = lax.dynamic_update_slice_in_dim(buf, b, at, axis)
        at += b.shape[axis]
    return buf


def kernel(x, norm_pre_g, w_in, mu_shift, w0, w_lora_up, a0, a_lora_up, k_k, k_a, r_k, lnx_g, lnx_b, conv_w, conv_b, cln_g, cln_b, w_pw2, b_pw2, w_out, norm_post_g, loss_target, m_norm_pre_g, m_w_in, m_mu_shift, m_w0, m_w_lora_up, m_a0, m_a_lora_up, m_k_k, m_k_a, m_r_k, m_lnx_g, m_lnx_b, m_conv_w, m_conv_b, m_cln_g, m_cln_b, m_w_pw2, m_b_pw2, m_w_out, m_norm_post_g, v_norm_pre_g, v_w_in, v_mu_shift, v_w0, v_w_lora_up, v_a0, v_a_lora_up, v_k_k, v_k_a, v_r_k, v_lnx_g, v_lnx_b, v_conv_w, v_conv_b, v_cln_g, v_cln_b, v_w_pw2, v_b_pw2, v_w_out, v_norm_post_g):
    _, T, D = x.shape
    RW = w0.shape[0]
    CW = conv_b.shape[0]
    head = r_k.shape[1]
    lora = w_lora_up.shape[0]
    ktaps = conv_w.shape[0]
    assert RW == CW and 2 * lora <= LORA_PAD and ktaps - 1 <= CONV_HALO
    n_in = 3 * RW + 2 * lora + RW + 3 * CW
    shard = n_in // 4
    PW = 7 * RW + LORA_PAD
    off_l = 7 * RW
    tm = min(256, T // 2)
    tm_wide = min(128, T // 2)
    tm_halo = min(512, T // 2)
    row = lambda vec: vec.reshape(1, -1)
    x2, tgt2 = x[0], loss_target[0]

    halves = lambda a: a.reshape(2, a.shape[0] // 2, a.shape[1])
    conv_w_p = jnp.concatenate([conv_w, jnp.zeros((CONV_HALO - ktaps, CW // 4), f32)], axis=0)
    w_in_t, m_w_in_t, v_w_in_t = w_in.T, m_w_in.T, v_w_in.T
    own_win = halves(w_in_t.astype(bf16))
    (g_win,) = _run_comm("gather_w_in", _gather_chips([own_win], relayed=True, copy_own=False))
    g_win = lax.dynamic_update_slice(g_win, own_win[None], (2 * lax.axis_index("x") + lax.axis_index("y"), 0, 0, 0))
    win_t = g_win.reshape(n_in, D)
    lo = 3 * RW
    wp_t = _place_blocks([win_t[:lo], win_t[lo + 2 * lora:], win_t[lo:lo + 2 * lora],
                          jnp.zeros((LORA_PAD - 2 * lora, D), bf16)], axis=0)
    npg = row(norm_pre_g)
    (h,) = _row_fwd("rms_pre", _fn_rms_pre, [(npg, False)], [(x2, 0, D, False)], [(D, bf16)], T, tm)
    others = [halves(a) for a in (w_lora_up, a_lora_up, conv_w_p, w_pw2.astype(bf16), w_out.astype(bf16))]
    proj, (g_wup, g_aup, g_cw, g_pw2, g_wout) = _matmul("proj", h, wp_t, "nt", f32, comm=_gather_chips(others))
    cat_cols = lambda g: jnp.concatenate([g[s].reshape(-1, g.shape[-1]) for s in range(4)], axis=1)
    wup_full, aup_full, cw_p = cat_cols(g_wup), cat_cols(g_aup), cat_cols(g_cw)
    zl = lambda n: jnp.zeros((n, RW), f32)
    wup_p = jnp.concatenate([wup_full, zl(LORA_PAD - lora)], axis=0)
    aup_p = jnp.concatenate([zl(lora), aup_full, zl(LORA_PAD - 2 * lora)], axis=0)
    pw2_full = g_pw2.reshape(CW, CW)
    wout_full = g_wout.reshape(RW + CW, D)
    mu_r, mu_k, mu_v = (row(mu_shift[s * RW:(s + 1) * RW]) for s in range(3))
    mu_l = row(jnp.concatenate([mu_shift[3 * RW:], jnp.zeros((LORA_PAD - 2 * lora,), f32)]))

    xs_r = _shift_fwd("shift_r", proj, 0, RW, mu_r, T, tm_halo)
    xs_k = _shift_fwd("shift_k", proj, RW, RW, mu_k, T, tm_halo)
    xs_v = _shift_fwd("shift_v", proj, 2 * RW, RW, mu_v, T, tm_halo)
    xs_l = _shift_fwd("shift_l", proj, off_l, LORA_PAD, mu_l, T, tm_halo)
    lora_params = [(row(w0), False), (wup_p, False), (row(a0), False), (aup_p, False)]
    qw, qa = _row_fwd("lora_up", _fn_lora, lora_params, [(xs_l, 0, LORA_PAD, False)], [(RW, f32), (RW, f32)], T, tm)
    ncol = RW // _tile(RW, 512)
    fn_pre = functools.partial(_fn_rwkv_pre, head)
    pre_params = [(row(k_k), True), (row(k_a), True)]
    pre_rows = [(xs_k, 0, RW, True), (qw, 0, RW, True), (qa, 0, RW, True)]
    lw, k_h, a_rec, b_rec = _row_fwd("rwkv_pre", fn_pre, pre_params, pre_rows, [(RW, f32)] * 4, T, tm_halo, ncol)
    wkv_in = (xs_r, lw, k_h, xs_v, a_rec, b_rec)
    c_w, c_u, c_rt, c_pb, c_y0, c_bend, c_z = _wkv_local_fwd(*wkv_in, head, T, RW)
    wkv_loc = (c_w, c_u, c_rt, c_pb, c_bend, lw, c_y0, c_z)
    y_wkv, states = _wkv_state_fwd(*wkv_loc, head, T, RW)
    fn_post = functools.partial(_fn_rwkv_post, head)
    post_params = [(row(lnx_g), True), (row(lnx_b), True), (r_k.reshape(1, RW), True)]
    post_rows = [(y_wkv, 0, RW, True), (xs_r, 0, RW, True), (k_h, 0, RW, True), (xs_v, 0, RW, True),
                 (proj, 3 * RW, RW, True)]
    (y_rwkv,) = _row_fwd("rwkv_post", fn_post, post_params, post_rows, [(RW, bf16)], T, tm_halo, ncol)

    c_pre = _conv_fwd(proj, 4 * RW, 5 * RW, CW, cw_p, row(conv_b), ktaps, T, tm_halo)
    ln_params = [(row(cln_g), False), (row(cln_b), False)]
    (c_act,) = _row_fwd("conv_ln", _fn_conv_ln, ln_params, [(c_pre, 0, CW, False)], [(CW, bf16)], T, tm)
    c2 = _matmul("pw2", c_act, pw2_full, "nn", f32)
    cpost_params = [(row(b_pw2), True)]
    cpost_rows = [(c2, 0, CW, True), (proj, 6 * RW, CW, True)]
    (y_conv,) = _row_fwd("conv_post", _fn_conv_post, cpost_params, cpost_rows, [(CW, bf16)], T, tm, ncol)

    mix = jnp.concatenate([y_rwkv, y_conv], axis=1)
    out = _matmul("out_proj", mix, wout_full, "nn", f32)
    d_out, gx_res, loss_part, g_npost = _post(out, x2, tgt2, row(norm_post_g), T, D, tm_wide)

    g_wout_full = _matmul("d_w_out", mix, d_out, "tn", bf16)
    d_mix = _matmul("d_mix", d_out, wout_full, "nt", f32)

    d_c2, d_gconv, g_bpw2 = _row_bwd("conv_post_bwd", _fn_conv_post, cpost_params, cpost_rows,
                                      [(d_mix, RW, CW, True)], [bf16, bf16], T, tm, ncol)
    g_pw2_full = _matmul("d_w_pw2", c_act, d_c2, "tn", bf16)
    d_cact = _matmul("d_c_act", d_c2, pw2_full, "nt", f32)
    d_cpre, g_clng, g_clnb = _row_bwd("conv_ln_bwd", _fn_conv_ln, ln_params, [(c_pre, 0, CW, False)],
                                      [(d_cact, 0, CW, False)], [f32], T, tm)
    d_gluv, d_glug, g_cw_p, g_cb = _conv_bwd(proj, 4 * RW, 5 * RW, CW, cw_p, d_cpre, ktaps, T, tm_halo)

    d_y, dr_x, dk_x, dv_x, d_grwkv, g_lnxg, g_lnxb, g_rk = _row_bwd(
        "rwkv_post_bwd", fn_post, post_params, post_rows, [(d_mix, 0, RW, True)], [f32, f32, f32, f32, bf16], T, tm, ncol)
    d_cw, d_cu, d_crt, d_cpb, d_cbend, d_lw_dec, d_cz = _wkv_state_bwd(*wkv_loc, states, d_y, head, T, RW)
    d_xr, d_lw, d_kh, d_xv, d_a, d_b = _wkv_local_bwd(
        *wkv_in, (d_cw, d_cu, d_crt, d_cpb, d_y, d_cbend, d_cz), d_lw_dec, dr_x, dk_x, dv_x, head, T, RW)
    pre_cots = [(d_lw, 0, RW, True), (d_kh, 0, RW, True), (d_a, 0, RW, True), (d_b, 0, RW, True)]
    d_xk, d_qw, d_qa, g_kk, g_ka = _row_bwd("rwkv_pre_bwd", fn_pre, pre_params, pre_rows, pre_cots, [f32, f32, f32],
                                            T, tm, ncol)
    d_xl, g_w0, g_wup_p, g_a0, g_aup_p = _row_bwd("lora_up_bwd", _fn_lora, lora_params, [(xs_l, 0, LORA_PAD, False)],
                                                  [(d_qw, 0, RW, False), (d_qa, 0, RW, False)], [f32], T, tm)
    dp_r, g_mur = _shift_bwd("shift_r_bwd", proj, 0, RW, mu_r, d_xr, T, tm_halo)
    dp_k, g_muk = _shift_bwd("shift_k_bwd", proj, RW, RW, mu_k, d_xk, T, tm_halo)
    dp_v, g_muv = _shift_bwd("shift_v_bwd", proj, 2 * RW, RW, mu_v, d_xv, T, tm_halo)
    dp_l, g_mul = _shift_bwd("shift_l_bwd", proj, off_l, LORA_PAD, mu_l, d_xl, T, tm_halo)
    d_proj = _place_blocks([dp_r, dp_k, dp_v, d_grwkv, d_gluv, d_glug, d_gconv, dp_l], axis=1)

    def chip_sums(tag, shards):
        halves4 = [a.reshape(4, 2, a.shape[1] // 2, a.shape[2]) for a in shards]
        got = _pair_exchange("pair_exchange_" + tag, halves4)
        return [_add_kept("chip_sum_%s_%d" % (tag, i), a, g) for i, (a, g) in enumerate(zip(halves4, got))]

    def all_chips(tag, slots):
        return _share_halves("share_" + tag, [_sum_slots_half("sum_%s_%d" % (tag, i), r) for i, r in enumerate(slots)])

    q_early = chip_sums("early", [g_wout_full.reshape(4, (RW + CW) // 4, D), g_pw2_full.reshape(4, CW // 4, CW)])
    g_wp_t, r_early = _matmul("d_w_in", d_proj, h, "tn", bf16, comm=_exchange_chips(q_early, []))
    g_win_t = _place_blocks([g_wp_t[:lo], g_wp_t[off_l:off_l + 2 * lora], g_wp_t[lo:off_l]], axis=0)
    col_shards = lambda a: a.reshape(a.shape[0], 4, a.shape[1] // 4).transpose(1, 0, 2)
    q_late = chip_sums("late", [g_win_t.reshape(4, shard, D), col_shards(g_wup_p[:lora]),
                                col_shards(g_aup_p[lora:2 * lora]), col_shards(g_cw_p)])
    g_mu = jnp.concatenate([g_mur[0], g_muk[0], g_muv[0], g_mul[0, :2 * lora]])
    pad_rows = lambda a, n: jnp.concatenate([a, jnp.zeros((n - a.shape[0], a.shape[1]), f32)], axis=0)
    n_mu = -(-mu_shift.shape[0] // RW)
    small_vecs = [pad_rows(jnp.pad(g_mu, (0, n_mu * RW - g_mu.shape[0])).reshape(n_mu, RW), n_mu),
                  g_w0, g_a0, g_kk, g_ka, g_rk, g_lnxg, g_lnxb, g_cb, g_clng, g_clnb, g_bpw2,
                  g_npost.reshape(D // RW, RW)]
    n_small = sum(a.shape[0] for a in small_vecs)
    n_small_pad = -(-n_small // (2 * SUBLANES)) * (2 * SUBLANES)
    small = pad_rows(jnp.concatenate(small_vecs, axis=0), n_small_pad)

    d_h, r_late = _matmul("d_h", d_proj, wp_t, "nn", bf16, tk_t=2560, comm=_exchange_chips(q_late, [small]))
    grad_x2, g_npre = _rms_pre_bwd(x2, npg, d_h, gx_res, T, D, tm_wide)
    (r_npre,) = _run_comm("exchange_norm_pre", _exchange_chips([], [g_npre.reshape(D // RW, RW)]))
    s_npre = _sum_slots("sum_norm_pre", r_npre)
    s_small = _sum_slots("sum_small", r_late[4])
    grad_w_out, grad_w_pw2 = all_chips("early", r_early)
    grad_w_in, grad_wup, grad_aup, grad_cw = all_chips("late", r_late[:4])

    pos = [0]

    def take(nrows):
        a = s_small[pos[0]:pos[0] + nrows]
        pos[0] += nrows
        return a

    grads = {}
    grads["norm_pre_g"] = s_npre.reshape(D)
    grads["mu_shift"] = take(n_mu).reshape(-1)[:mu_shift.shape[0]]
    for nm in ["w0", "a0", "k_k", "k_a"]:
        grads[nm] = take(1).reshape(RW)
    grads["r_k"] = take(1).reshape(r_k.shape)
    for nm in ["lnx_g", "lnx_b", "conv_b", "cln_g", "cln_b", "b_pw2"]:
        grads[nm] = take(1).reshape(RW)
    grads["norm_post_g"] = take(D // RW).reshape(D)
    grads["w_lora_up"], grads["a_lora_up"], grads["conv_w"] = grad_wup, grad_aup, grad_cw[:ktaps]
    grads["w_in"], grads["w_out"], grads["w_pw2"] = grad_w_in, grad_w_out, grad_w_pw2

    weights = dict(norm_pre_g=norm_pre_g, w_in=w_in, mu_shift=mu_shift, w0=w0, w_lora_up=w_lora_up, a0=a0,
                   a_lora_up=a_lora_up, k_k=k_k, k_a=k_a, r_k=r_k, lnx_g=lnx_g, lnx_b=lnx_b, conv_w=conv_w,
                   conv_b=conv_b, cln_g=cln_g, cln_b=cln_b, w_pw2=w_pw2, b_pw2=b_pw2, w_out=w_out,
                   norm_post_g=norm_post_g)
    ms = dict(norm_pre_g=m_norm_pre_g, w_in=m_w_in, mu_shift=m_mu_shift, w0=m_w0, w_lora_up=m_w_lora_up, a0=m_a0,
              a_lora_up=m_a_lora_up, k_k=m_k_k, k_a=m_k_a, r_k=m_r_k, lnx_g=m_lnx_g, lnx_b=m_lnx_b, conv_w=m_conv_w,
              conv_b=m_conv_b, cln_g=m_cln_g, cln_b=m_cln_b, w_pw2=m_w_pw2, b_pw2=m_b_pw2, w_out=m_w_out,
              norm_post_g=m_norm_post_g)
    vs = dict(norm_pre_g=v_norm_pre_g, w_in=v_w_in, mu_shift=v_mu_shift, w0=v_w0, w_lora_up=v_w_lora_up, a0=v_a0,
              a_lora_up=v_a_lora_up, k_k=v_k_k, k_a=v_k_a, r_k=v_r_k, lnx_g=v_lnx_g, lnx_b=v_lnx_b, conv_w=v_conv_w,
              conv_b=v_conv_b, cln_g=v_cln_g, cln_b=v_cln_b, w_pw2=v_w_pw2, b_pw2=v_b_pw2, w_out=v_w_out,
              norm_post_g=v_norm_post_g)
    names = list(weights)
    big = ["w_in", "w_out", "w_pw2"]
    deltas, new_m, new_v = {}, {}, {}
    d_t, m_t, v_t = _adamw("adamw_w_in", w_in_t, grad_w_in, m_w_in_t, v_w_in_t)
    grads["w_in"], deltas["w_in"], new_m["w_in"], new_v["w_in"] = grad_w_in.T, d_t.T, m_t.T, v_t.T
    for nm in big[1:]:
        deltas[nm], new_m[nm], new_v[nm] = _adamw("adamw_" + nm, weights[nm], grads[nm], ms[nm], vs[nm])
    rest = [nm for nm in names if nm not in big]
    sizes = [weights[nm].size for nm in rest]
    total = sum(sizes)
    width = 4 * LANES
    rows_p = -(-total // (width * SUBLANES)) * SUBLANES

    def pack(d):
        flat = jnp.concatenate([d[nm].reshape(-1) for nm in rest])
        return jnp.pad(flat, (0, rows_p * width - total)).reshape(rows_p, width)

    p_d, p_m, p_v = _adamw("adamw_small", pack(weights), pack(grads), pack(ms), pack(vs))
    o = 0
    for nm, sz in zip(rest, sizes):
        shp = weights[nm].shape
        deltas[nm] = p_d.reshape(-1)[o:o + sz].reshape(shp)
        new_m[nm] = p_m.reshape(-1)[o:o + sz].reshape(shp)
        new_v[nm] = p_v.reshape(-1)[o:o + sz].reshape(shp)
        o += sz

    loss = lax.psum(loss_part[0, 0], ("x", "y", "c"))
    grad_x = grad_x2[None]
    return (loss, grad_x, *[grads[nm] for nm in names], *[deltas[nm] for nm in names],
            *[new_m[nm] for nm in names], *[new_v[nm] for nm in names])
```

```python
import functools

import jax
import jax.numpy as jnp
from jax import lax
from jax.experimental import pallas as pl
from jax.experimental.pallas import tpu as pltpu

f32 = jnp.float32
bf16 = jnp.bfloat16
MESH = pl.DeviceIdType.MESH

NORM_EPS = 1e-6
LN_EPS = 1e-5
ADAM_LR, ADAM_B1, ADAM_B2, ADAM_EPS, ADAM_WD, ADAM_STEP = 0.001, 0.9, 0.999, 1e-08, 0.01, 10

LANES = 128
SUBLANES = 8
LORA_PAD = 256
CONV_HALO = 32
WKV_CHUNK = 64
WKV_HEADS = 16
WKV_FWD_HEADS = 32
WKV_STATE_HEADS = 32
WKV_PREC = lax.Precision.HIGH
SHARE_CHUNKS = 8
VMEM_LIMIT = 56 * 1024 * 1024


def _cparams(n_axes):
    return pltpu.CompilerParams(dimension_semantics=("arbitrary",) * n_axes, vmem_limit_bytes=VMEM_LIMIT)


def _tile(dim, target):
    best = None
    t = LANES
    while t <= min(dim, target):
        if dim % t == 0:
            best = t
        t += LANES
    return best if best is not None else dim


def _mm(a, b, prec=None):
    return lax.dot_general(a, b, (((1,), (0,)), ((), ())), precision=prec, preferred_element_type=f32)


def _mm_nt(a, b, prec=None):
    return lax.dot_general(a, b, (((1,), (1,)), ((), ())), precision=prec, preferred_element_type=f32)


def _mm_tn(a, b, prec=None):
    return lax.dot_general(a, b, (((0,), (0,)), ((), ())), precision=prec, preferred_element_type=f32)


@jax.custom_vjp
def _bmm(a, b):
    return _mm(a.astype(bf16), b.astype(bf16))


def _bmm_fwd(a, b):
    return _bmm(a, b), (a, b)


def _bmm_bwd(res, dc):
    a, b = res
    dcb = dc.astype(bf16)
    return _mm_nt(dcb, b.astype(bf16)), _mm_tn(a.astype(bf16), dcb)


_bmm.defvjp(_bmm_fwd, _bmm_bwd)


@jax.custom_vjp
def _bmm_nt(a, b):
    return _mm_nt(a.astype(bf16), b.astype(bf16))


def _bmm_nt_fwd(a, b):
    return _bmm_nt(a, b), (a, b)


def _bmm_nt_bwd(res, dc):
    a, b = res
    dcb = dc.astype(bf16)
    return _mm(dcb, b.astype(bf16)), _mm_tn(dcb, a.astype(bf16))


_bmm_nt.defvjp(_bmm_nt_fwd, _bmm_nt_bwd)


@jax.custom_vjp
def _bmm_tn(a, b):
    return _mm_tn(a.astype(bf16), b.astype(bf16))


def _bmm_tn_fwd(a, b):
    return _bmm_tn(a, b), (a, b)


def _bmm_tn_bwd(res, dc):
    a, b = res
    dcb = dc.astype(bf16)
    return _mm_nt(b.astype(bf16), dcb), _mm(a.astype(bf16), dcb)


_bmm_tn.defvjp(_bmm_tn_fwd, _bmm_tn_bwd)


def _matmul(name, a, b, mode, out_dtype, tm_t=1024, tn_t=1024, tk_t=4096, comm=None):
    if mode == "nn":
        (M, K), (_, N) = a.shape, b.shape
    elif mode == "nt":
        (M, K), (N, _) = a.shape, b.shape
    else:
        (K, M), (_, N) = a.shape, b.shape
    tm, tn, tk = _tile(M, tm_t), _tile(N, tn_t), _tile(K, tk_t)
    ni, nj, nk = M // tm, N // tn, K // tk
    dot = {"nn": _mm, "nt": _mm_nt, "tn": _mm_tn}[mode]
    nc = len(comm.operands) if comm else 0

    def body(*refs):
        a_ref, b_ref = refs[:2]
        o_ref = refs[2 + nc]
        scratch = refs[3 + 2 * nc:]
        i, j, k = pl.program_id(0), pl.program_id(1), pl.program_id(2)
        if comm:
            comm_refs = (refs[2:2 + nc], refs[3 + nc:3 + 2 * nc], scratch[:len(comm.scratch)])

            @pl.when(jnp.logical_and(jnp.logical_and(i == 0, j == 0), k == 0))
            def _():
                comm.start(*comm_refs)

        if nk == 1:
            o_ref[...] = dot(a_ref[...], b_ref[...]).astype(o_ref.dtype)
        else:
            acc_ref = scratch[-1]

            @pl.when(k == 0)
            def _():
                acc_ref[...] = jnp.zeros_like(acc_ref)

            acc_ref[...] += dot(a_ref[...], b_ref[...])

            @pl.when(k == nk - 1)
            def _():
                o_ref[...] = acc_ref[...].astype(o_ref.dtype)

        if comm:
            @pl.when(jnp.logical_and(jnp.logical_and(i == ni - 1, j == nj - 1), k == nk - 1))
            def _():
                comm.wait(*comm_refs)

    a_spec = {"nn": pl.BlockSpec((tm, tk), lambda i, j, k: (i, k)),
              "nt": pl.BlockSpec((tm, tk), lambda i, j, k: (i, k)),
              "tn": pl.BlockSpec((tk, tm), lambda i, j, k: (k, i))}[mode]
    b_spec = {"nn": pl.BlockSpec((tk, tn), lambda i, j, k: (k, j)),
              "nt": pl.BlockSpec((tn, tk), lambda i, j, k: (j, k)),
              "tn": pl.BlockSpec((tk, tn), lambda i, j, k: (k, j))}[mode]
    res = pl.pallas_call(
        body, name=name, grid=(ni, nj, nk),
        in_specs=[a_spec, b_spec] + [ANY] * nc,
        out_specs=[pl.BlockSpec((tm, tn), lambda i, j, k: (i, j))] + [ANY] * nc,
        out_shape=[jax.ShapeDtypeStruct((M, N), out_dtype)] + (list(comm.out_shape) if comm else []),
        scratch_shapes=(list(comm.scratch) if comm else []) + ([pltpu.VMEM((tm, tn), f32)] if nk > 1 else []),
        compiler_params=_cparams(3),
    )(a, b, *(comm.operands if comm else []))
    return (res[0], res[1:]) if comm else res[0]


def _row_spec(op, tm, ncol):
    arr, off, width, tiled = op
    if tiled:
        cw = width // ncol
        return pl.BlockSpec((tm, cw), lambda j, i: (i, off // cw + j))
    return pl.BlockSpec((tm, width), lambda j, i: (i, off // width))


def _param_spec(p, ncol):
    arr, tiled = p
    rows, width = arr.shape
    if tiled:
        return pl.BlockSpec((rows, width // ncol), lambda j, i: (0, j))
    return pl.BlockSpec((rows, width), lambda j, i: (0, 0))


def _row_fwd(name, fn, params, rows, outs, T, tm, ncol=1):
    npar, nrow = len(params), len(rows)

    def body(*refs):
        pv = [r[...] for r in refs[:npar]]
        rv = [r[...].astype(f32) for r in refs[npar:npar + nrow]]
        res = fn(*pv, *rv)
        for o_ref, val in zip(refs[npar + nrow:], res):
            o_ref[...] = val.astype(o_ref.dtype)

    return pl.pallas_call(
        body, name=name, grid=(ncol, T // tm),
        in_specs=[_param_spec(p, ncol) for p in params] + [_row_spec(r, tm, ncol) for r in rows],
        out_specs=[pl.BlockSpec((tm, w // ncol), lambda j, i: (i, j)) for w, _ in outs],
        out_shape=[jax.ShapeDtypeStruct((T, w), dt) for w, dt in outs],
        compiler_params=_cparams(2),
    )(*[p[0] for p in params], *[r[0] for r in rows])


def _row_bwd(name, fn, params, rows, cots, row_grads, T, tm, ncol=1):
    npar, nrow, ncot = len(params), len(rows), len(cots)
    want = [k for k, dt in enumerate(row_grads) if dt is not None]

    def body(*refs):
        pv = [r[...] for r in refs[:npar]]
        rv = [r[...].astype(f32) for r in refs[npar:npar + nrow]]
        cv = tuple(r[...].astype(f32) for r in refs[npar + nrow:npar + nrow + ncot])
        out_refs = refs[npar + nrow + ncot:]
        _, vjp = jax.vjp(fn, *pv, *rv)
        grads = vjp(cv)
        for o_ref, k in zip(out_refs[:len(want)], want):
            o_ref[...] = grads[npar + k].astype(o_ref.dtype)
        j, i = pl.program_id(0), pl.program_id(1)
        for o_ref, p, g in zip(out_refs[len(want):], params, grads[:npar]):
            first = (i == 0) if p[1] else jnp.logical_and(i == 0, j == 0)

            @pl.when(first)
            def _():
                o_ref[...] = jnp.zeros_like(o_ref)

            o_ref[...] += g

    def grad_spec(op):
        arr, off, width, tiled = op
        if tiled:
            return pl.BlockSpec((tm, width // ncol), lambda j, i: (i, j)), (T, width)
        return pl.BlockSpec((tm, width), lambda j, i: (i, j)), (T, width * ncol)

    gspecs = [grad_spec(rows[k]) for k in want]
    return pl.pallas_call(
        body, name=name, grid=(ncol, T // tm),
        in_specs=[_param_spec(p, ncol) for p in params] + [_row_spec(r, tm, ncol) for r in rows]
        + [_row_spec(c, tm, ncol) for c in cots],
        out_specs=[s for s, _ in gspecs] + [_param_spec(p, ncol) for p in params],
        out_shape=[jax.ShapeDtypeStruct(shp, row_grads[k]) for (_, shp), k in zip(gspecs, want)]
        + [jax.ShapeDtypeStruct(p[0].shape, f32) for p in params],
        compiler_params=_cparams(2),
    )(*[p[0] for p in params], *[r[0] for r in rows], *[c[0] for c in cots])


def _seg_sum(x, head):
    li = lax.broadcasted_iota(jnp.int32, (LANES, LANES), 0) // head
    lj = lax.broadcasted_iota(jnp.int32, (LANES, LANES), 1) // head
    q = (li == lj).astype(f32)
    parts = [_mm(x[:, s:s + LANES], q, lax.Precision.HIGH) for s in range(0, x.shape[1], LANES)]
    return parts[0] if len(parts) == 1 else jnp.concatenate(parts, axis=1)


def _sigmoid(z):
    return 1.0 / (1.0 + jnp.exp(-z))


def _silu(z):
    return z * _sigmoid(z)


def _rms(g, x):
    return x * lax.rsqrt(jnp.mean(x * x, axis=-1, keepdims=True) + NORM_EPS) * g


def _fn_rms_pre(g, x):
    return (_rms(g, x),)


def _fn_lora(w0, wup, a0, aup, xl):
    qw = w0 + _bmm(jnp.tanh(xl), wup)
    qa = a0 + _bmm(xl, aup)
    return qw, qa


def _fn_rwkv_pre(head, k_k, k_a, xk, qw, qa):
    w_log = -(jnp.maximum(-qw, 0.0) + jnp.log(1.0 + jnp.exp(-jnp.abs(qw)))) - 0.5
    lw = -jnp.exp(w_log)
    a_sig = _sigmoid(qa)
    kk = xk * k_k
    kk = kk / jnp.maximum(jnp.sqrt(_seg_sum(kk * kk, head)), 1e-12)
    k_h = xk * (1.0 + (a_sig - 1.0) * k_a)
    return lw, k_h, -kk, kk * a_sig


def _fn_rwkv_post(head, lnx_g, lnx_b, r_k, y, r, k_h, v, g):
    inv = 1.0 / head
    mu = _seg_sum(y, head) * inv
    d = y - mu
    var = _seg_sum(d * d, head) * inv
    yn = d * lax.rsqrt(var + 1e-5 * head) * lnx_g + lnx_b
    bonus = _seg_sum(r * k_h * r_k, head) * v
    return ((yn + bonus) * _silu(g),)


def _fn_conv_ln(cln_g, cln_b, c):
    mu = jnp.mean(c, axis=-1, keepdims=True)
    d = c - mu
    var = jnp.mean(d * d, axis=-1, keepdims=True)
    return (_silu(d * lax.rsqrt(var + LN_EPS) * cln_g + cln_b),)


def _fn_conv_post(b_pw2, c2, g):
    return ((c2 + b_pw2) * _silu(g),)


def _post(out, x, tgt, g, T, D, tm):
    def body(g_ref, o_ref, x_ref, t_ref, dout_ref, gx_ref, loss_ref, dg_ref):
        i = pl.program_id(0)
        o, vjp = jax.vjp(_rms, g_ref[...], o_ref[...])
        err = x_ref[...] + o - t_ref[...]
        d_y = err * (1.0 / D)
        dg, d_out = vjp(d_y)
        dout_ref[...] = d_out.astype(dout_ref.dtype)
        gx_ref[...] = d_y

        @pl.when(i == 0)
        def _():
            loss_ref[...] = jnp.zeros_like(loss_ref)
            dg_ref[...] = jnp.zeros_like(dg_ref)

        loss_ref[...] += jnp.sum(err * err, keepdims=True) * (0.5 / D)
        dg_ref[...] += dg

    row = pl.BlockSpec((tm, D), lambda i: (i, 0))
    vec = pl.BlockSpec((1, D), lambda i: (0, 0))
    return pl.pallas_call(
        body, name="post_loss", grid=(T // tm,),
        in_specs=[vec, row, row, row],
        out_specs=[row, row, pl.BlockSpec((1, 1), lambda i: (0, 0)), vec],
        out_shape=[jax.ShapeDtypeStruct((T, D), bf16), jax.ShapeDtypeStruct((T, D), f32),
                   jax.ShapeDtypeStruct((1, 1), f32), jax.ShapeDtypeStruct((1, D), f32)],
        compiler_params=_cparams(1),
    )(g, out, x, tgt)


def _rms_pre_bwd(x, g, dh, gx_res, T, D, tm):
    def body(g_ref, x_ref, dh_ref, res_ref, dx_ref, dg_ref):
        i = pl.program_id(0)
        _, vjp = jax.vjp(_rms, g_ref[...], x_ref[...])
        dg, dx = vjp(dh_ref[...].astype(f32))
        dx_ref[...] = dx + res_ref[...]

        @pl.when(i == 0)
        def _():
            dg_ref[...] = jnp.zeros_like(dg_ref)

        dg_ref[...] += dg

    row = pl.BlockSpec((tm, D), lambda i: (i, 0))
    vec = pl.BlockSpec((1, D), lambda i: (0, 0))
    return pl.pallas_call(
        body, name="rms_pre_bwd", grid=(T // tm,),
        in_specs=[vec, row, row, row], out_specs=[row, vec],
        out_shape=[jax.ShapeDtypeStruct((T, D), f32), jax.ShapeDtypeStruct((1, D), f32)],
        compiler_params=_cparams(1),
    )(g, x, dh, gx_res)


def _prev_rows(cur, halo_ref, first):
    top = jnp.where(first, 0.0, halo_ref[SUBLANES - 1:SUBLANES, :])
    rolled = pltpu.roll(cur, 1, 0)
    rid = lax.broadcasted_iota(jnp.int32, cur.shape, 0)
    return jnp.where(rid == 0, top, rolled)


def _shift_fwd(name, proj, off, width, mu, T, tm):
    cw = _tile(width, 512)
    ncol, cb = width // cw, off // cw
    hb = tm // SUBLANES

    def body(mu_ref, cur_ref, halo_ref, o_ref):
        i = pl.program_id(1)
        cur = cur_ref[...]
        prev = _prev_rows(cur, halo_ref, i == 0)
        o_ref[...] = cur + (prev - cur) * mu_ref[...]

    return pl.pallas_call(
        body, name=name, grid=(ncol, T // tm),
        in_specs=[pl.BlockSpec((1, cw), lambda j, i: (0, j)),
                  pl.BlockSpec((tm, cw), lambda j, i: (i, cb + j)),
                  pl.BlockSpec((SUBLANES, cw), lambda j, i: (jnp.maximum(i * hb - 1, 0), cb + j))],
        out_specs=pl.BlockSpec((tm, cw), lambda j, i: (i, j)),
        out_shape=jax.ShapeDtypeStruct((T, width), f32),
        compiler_params=_cparams(2),
    )(mu, proj, proj)


def _shift_bwd(name, proj, off, width, mu, dxs, T, tm, into):
    cw = _tile(width, 512)
    ncol, cb = width // cw, off // cw
    hb, nt = tm // SUBLANES, T // tm

    def body(mu_ref, cur_ref, halo_ref, d_ref, dnext_ref, into_ref, o_ref, dmu_ref):
        i = pl.program_id(1)
        cur = cur_ref[...]
        prev = _prev_rows(cur, halo_ref, i == 0)
        d = d_ref[...]
        bottom = jnp.where(i == nt - 1, 0.0, dnext_ref[0:1, :])
        rid = lax.broadcasted_iota(jnp.int32, d.shape, 0)
        d_next = jnp.where(rid == tm - 1, bottom, pltpu.roll(d, tm - 1, 0))
        mu_v = mu_ref[...]
        o_ref[...] = (d * (1.0 - mu_v) + d_next * mu_v).astype(o_ref.dtype)

        @pl.when(i == 0)
        def _():
            dmu_ref[...] = jnp.zeros_like(dmu_ref)

        dmu_ref[...] += jnp.sum(d * (prev - cur), axis=0, keepdims=True)

    return pl.pallas_call(
        body, name=name, grid=(ncol, nt),
        in_specs=[pl.BlockSpec((1, cw), lambda j, i: (0, j)),
                  pl.BlockSpec((tm, cw), lambda j, i: (i, cb + j)),
                  pl.BlockSpec((SUBLANES, cw), lambda j, i: (jnp.maximum(i * hb - 1, 0), cb + j)),
                  pl.BlockSpec((tm, cw), lambda j, i: (i, j)),
                  pl.BlockSpec((SUBLANES, cw), lambda j, i: (jnp.minimum((i + 1) * hb, nt * hb - 1), j)),
                  ANY],
        out_specs=[pl.BlockSpec((tm, cw), lambda j, i: (i, cb + j)), pl.BlockSpec((1, cw), lambda j, i: (0, j))],
        out_shape=[jax.ShapeDtypeStruct(into.shape, into.dtype), jax.ShapeDtypeStruct((1, width), f32)],
        input_output_aliases={5: 0},
        compiler_params=_cparams(2),
    )(mu, proj, proj, dxs, dxs, into)


def _rolled_copies(dst_ref, ext):
    n = ext.shape[0]
    dst_ref[0] = ext
    for r in range(1, SUBLANES):
        dst_ref[r] = pltpu.roll(ext, n - r, 0)


def _window(rolled_ref, start, rows):
    q, r = divmod(start, SUBLANES)
    return rolled_ref[r, pl.ds(SUBLANES * q, rows), :]


def _conv_fwd(proj, off_v, off_g, width, conv_w, conv_b, ktaps, T, tm):
    cw = _tile(width, 512)
    ncol = width // cw
    hb = tm // CONV_HALO
    lead = CONV_HALO - (ktaps - 1)

    def body(w_ref, b_ref, v_ref, g_ref, hv_ref, hg_ref, o_ref, u_ref):
        i = pl.program_id(1)
        halo = hv_ref[...] * _sigmoid(hg_ref[...])
        _rolled_copies(u_ref, jnp.concatenate([jnp.where(i == 0, 0.0, halo), v_ref[...] * _sigmoid(g_ref[...])], axis=0))
        acc = jnp.zeros((tm, cw), f32) + b_ref[...]
        for j in range(ktaps):
            acc = acc + _window(u_ref, lead + j, tm) * w_ref[j:j + 1, :]
        o_ref[...] = acc

    def tile(off):
        return pl.BlockSpec((tm, cw), lambda j, i: (i, off // cw + j))

    def halo(off):
        return pl.BlockSpec((CONV_HALO, cw), lambda j, i: (jnp.maximum(i * hb - 1, 0), off // cw + j))

    return pl.pallas_call(
        body, name="conv_fwd", grid=(ncol, T // tm),
        in_specs=[pl.BlockSpec((CONV_HALO, cw), lambda j, i: (0, j)), pl.BlockSpec((1, cw), lambda j, i: (0, j)),
                  tile(off_v), tile(off_g), halo(off_v), halo(off_g)],
        out_specs=pl.BlockSpec((tm, cw), lambda j, i: (i, j)),
        out_shape=jax.ShapeDtypeStruct((T, width), f32),
        scratch_shapes=[pltpu.VMEM((SUBLANES, CONV_HALO + tm, cw), f32)],
        compiler_params=_cparams(2),
    )(conv_w, conv_b, proj, proj, proj, proj)


def _conv_bwd(proj, off_v, off_g, width, conv_w, dc, ktaps, T, tm):
    cw = _tile(width, 512)
    ncol = width // cw
    hb, nt = tm // CONV_HALO, T // tm
    lead = CONV_HALO - (ktaps - 1)

    def body(w_ref, v_ref, g_ref, hv_ref, hg_ref, dc_ref, dcn_ref, dv_ref, dg_ref, dw_ref, db_ref, u_ref, d_ref):
        i = pl.program_id(1)
        halo = hv_ref[...] * _sigmoid(hg_ref[...])
        sig = _sigmoid(g_ref[...])
        gv = v_ref[...]
        _rolled_copies(u_ref, jnp.concatenate([jnp.where(i == 0, 0.0, halo), gv * sig], axis=0))
        dcur = dc_ref[...]
        _rolled_copies(d_ref, jnp.concatenate([dcur, jnp.where(i == nt - 1, 0.0, dcn_ref[...])], axis=0))

        @pl.when(i == 0)
        def _():
            dw_ref[...] = jnp.zeros_like(dw_ref)
            db_ref[...] = jnp.zeros_like(db_ref)

        du = jnp.zeros((tm, cw), f32)
        for j in range(ktaps):
            du = du + _window(d_ref, ktaps - 1 - j, tm) * w_ref[j:j + 1, :]
            dw_ref[j:j + 1, :] += jnp.sum(_window(u_ref, lead + j, tm) * dcur, axis=0, keepdims=True)
        db_ref[...] += jnp.sum(dcur, axis=0, keepdims=True)
        dv_ref[...] = (du * sig).astype(dv_ref.dtype)
        dg_ref[...] = (du * gv * sig * (1.0 - sig)).astype(dg_ref.dtype)

    def tile(off):
        return pl.BlockSpec((tm, cw), lambda j, i: (i, off // cw + j))

    def halo(off):
        return pl.BlockSpec((CONV_HALO, cw), lambda j, i: (jnp.maximum(i * hb - 1, 0), off // cw + j))

    return pl.pallas_call(
        body, name="conv_bwd", grid=(ncol, nt),
        in_specs=[pl.BlockSpec((CONV_HALO, cw), lambda j, i: (0, j)),
                  tile(off_v), tile(off_g), halo(off_v), halo(off_g),
                  pl.BlockSpec((tm, cw), lambda j, i: (i, j)),
                  pl.BlockSpec((CONV_HALO, cw), lambda j, i: (jnp.minimum((i + 1) * hb, nt * hb - 1), j))],
        out_specs=[pl.BlockSpec((tm, cw), lambda j, i: (i, j)), pl.BlockSpec((tm, cw), lambda j, i: (i, j)),
                   pl.BlockSpec((CONV_HALO, cw), lambda j, i: (0, j)), pl.BlockSpec((1, cw), lambda j, i: (0, j))],
        out_shape=[jax.ShapeDtypeStruct((T, width), bf16), jax.ShapeDtypeStruct((T, width), bf16),
                   jax.ShapeDtypeStruct((CONV_HALO, width), f32), jax.ShapeDtypeStruct((1, width), f32)],
        scratch_shapes=[pltpu.VMEM((SUBLANES, CONV_HALO + tm, cw), f32), pltpu.VMEM((SUBLANES, tm + CONV_HALO, cw), f32)],
        compiler_params=_cparams(2),
    )(conv_w, proj, proj, proj, proj, dc, dc)


def _each(f, *lists):
    return [f(*xs) for xs in zip(*lists)]


def _wkv_local(r, lw, k, v, a, b):
    C = r[0].shape[0]
    P = WKV_PREC
    row = lax.broadcasted_iota(jnp.int32, (C, C), 0)
    col = lax.broadcasted_iota(jnp.int32, (C, C), 1)
    incl, strict = row >= col, row > col
    tri = incl.astype(f32)
    zero = jnp.zeros((C, C), f32)
    G = _each(lambda x: _mm(tri, x, P), lw)
    to_end = _each(lambda x, g: jnp.exp(jnp.sum(x, axis=0, keepdims=True) - g), lw, G)
    e_g = _each(jnp.exp, G)
    e_ng = _each(lambda g: jnp.exp(-g), G)
    At = _each(lambda x, g, w: x * jnp.exp(g - w), a, G, lw)
    Rt = _each(jnp.multiply, r, e_g)
    Kt = _each(jnp.multiply, k, e_ng)
    Bt = _each(jnp.multiply, b, e_ng)
    sc = _each(lambda at, rt, bt, kt: _mm_nt(jnp.concatenate([at, rt], axis=0), jnp.concatenate([bt, kt], axis=0), P),
               At, Rt, Bt, Kt)
    L = _each(lambda s: jnp.where(strict, s[:C, :C], zero), sc)
    M = _each(lambda s: jnp.where(strict, s[:C, C:], zero), sc)
    Pb = _each(lambda s: jnp.where(incl, s[C:, :C], zero), sc)
    Pk = _each(lambda s: jnp.where(incl, s[C:, C:], zero), sc)
    MPk = _each(lambda m, p, x: _bmm(jnp.concatenate([m, p], axis=0), x), M, Pk, v)
    WU = _each(lambda at, mp: jnp.concatenate([at, mp[:C]], axis=1), At, MPk)
    Lp = L
    n = 1
    while n < C:
        n *= 2
        if n < C:
            step = _each(lambda l, x: _bmm(l, jnp.concatenate([x, l], axis=1)), Lp, WU)
            WU = _each(lambda x, s: x + s[:, :x.shape[1]], WU, step)
            Lp = _each(lambda x, s: s[:, x.shape[1]:], WU, step)
        else:
            WU = _each(lambda x, l: x + _bmm(l, x), WU, Lp)
    N = r[0].shape[1]
    W = _each(lambda x: x[:, :N], WU)
    U = _each(lambda x: x[:, N:], WU)
    Y0 = _each(lambda mp: mp[C:], MPk)
    Bend = _each(jnp.multiply, b, to_end)
    Z = _each(lambda x, y, e: _bmm_tn(x, y * e), v, k, to_end)
    return W, U, Rt, Pb, Y0, Bend, Z


def _wkv_state(S0, W, U, Rt, Pb, Bend, lw, Y0, Z):
    C = W[0].shape[0]
    WR = _each(lambda w, rt, s: _bmm_nt(jnp.concatenate([w, rt], axis=0), s), W, Rt, S0)
    X = _each(lambda wr, u: wr[:C] + u, WR, U)
    y = _each(lambda p, x, wr, c: _bmm(p, x) + wr[C:] + c, Pb, X, WR, Y0)
    S1 = _each(lambda s, w, x, e, z: s * jnp.exp(jnp.sum(w, axis=0, keepdims=True)) + _bmm_tn(x, e) + z,
               S0, lw, X, Bend, Z)
    return y, S1


def _wkv_dims(head, T, RW, heads_per_step):
    C = min(WKV_CHUNK, T)
    nh = RW // head
    hb = min(heads_per_step, nh)
    return C, nh, hb, hb * head, T // C


def _heads(ref, hb, head):
    return [ref[:, h * head:(h + 1) * head] for h in range(hb)]


def _put_heads(ref, vals, head):
    for h, val in enumerate(vals):
        ref[:, h * head:(h + 1) * head] = val


def _wkv_local_fwd(r, lw, k, v, a, b, head, T, RW):
    C, nh, hb, bw, nc = _wkv_dims(head, T, RW, WKV_FWD_HEADS)

    def body(*refs):
        ins, outs = refs[:6], refs[6:]
        res = _wkv_local(*[_heads(x, hb, head) for x in ins])
        for o_ref, vals in zip(outs[:6], res[:6]):
            _put_heads(o_ref, vals, head)
        for h in range(hb):
            outs[6][0, h] = res[6][h]

    blk = pl.BlockSpec((C, bw), lambda g, c: (c, g))
    sq = pl.BlockSpec((1, hb, head, head), lambda g, c: (c, g, 0, 0))
    return pl.pallas_call(
        body, name="wkv_local", grid=(nh // hb, nc),
        in_specs=[blk] * 6, out_specs=[blk] * 6 + [sq],
        out_shape=[jax.ShapeDtypeStruct((T, RW), f32)] * 6 + [jax.ShapeDtypeStruct((nc, nh, head, head), f32)],
        compiler_params=_cparams(2),
    )(r, lw, k, v, a, b)


def _wkv_state_fwd(W, U, Rt, Pb, Bend, lw, Y0, Z, head, T, RW):
    C, nh, hb, bw, nc = _wkv_dims(head, T, RW, WKV_STATE_HEADS)

    def body(w_ref, u_ref, rt_ref, pb_ref, be_ref, lw_ref, y0_ref, z_ref, y_ref, st_ref, s_ref):
        @pl.when(pl.program_id(1) == 0)
        def _():
            s_ref[...] = jnp.zeros_like(s_ref)

        S0 = [s_ref[h] for h in range(hb)]
        for h in range(hb):
            st_ref[0, h] = S0[h]
        rows = [_heads(x, hb, head) for x in (w_ref, u_ref, rt_ref, pb_ref, be_ref, lw_ref, y0_ref)]
        y, S1 = _wkv_state(S0, *rows, [z_ref[0, h] for h in range(hb)])
        _put_heads(y_ref, y, head)
        for h in range(hb):
            s_ref[h] = S1[h]

    blk = pl.BlockSpec((C, bw), lambda g, c: (c, g))
    sq = pl.BlockSpec((1, hb, head, head), lambda g, c: (c, g, 0, 0))
    return pl.pallas_call(
        body, name="wkv_state", grid=(nh // hb, nc),
        in_specs=[blk] * 7 + [sq], out_specs=[blk, sq],
        out_shape=[jax.ShapeDtypeStruct((T, RW), f32), jax.ShapeDtypeStruct((nc, nh, head, head), f32)],
        scratch_shapes=[pltpu.VMEM((hb, head, head), f32)],
        compiler_params=_cparams(2),
    )(W, U, Rt, Pb, Bend, lw, Y0, Z)


def _wkv_state_bwd(W, U, Rt, Pb, Bend, lw, Y0, Z, states, dy, head, T, RW):
    C, nh, hb, bw, nc = _wkv_dims(head, T, RW, WKV_STATE_HEADS)

    def body(w_ref, u_ref, rt_ref, pb_ref, be_ref, lw_ref, y0_ref, z_ref, st_ref, dy_ref,
             dw_ref, du_ref, drt_ref, dpb_ref, dbe_ref, dlw_ref, dz_ref, ds_ref):
        @pl.when(pl.program_id(1) == 0)
        def _():
            ds_ref[...] = jnp.zeros_like(ds_ref)

        dS1 = [ds_ref[h] for h in range(hb)]
        for h in range(hb):
            dz_ref[0, h] = dS1[h]
        rows = [_heads(x, hb, head) for x in (w_ref, u_ref, rt_ref, pb_ref, be_ref, lw_ref)]
        Y0 = _heads(y0_ref, hb, head)
        Zs = [z_ref[0, h] for h in range(hb)]
        _, vjp = jax.vjp(lambda s0, *rw: _wkv_state(s0, *rw, Y0, Zs), [st_ref[0, h] for h in range(hb)], *rows)
        grads = vjp((_heads(dy_ref, hb, head), dS1))
        for o_ref, vals in zip((dw_ref, du_ref, drt_ref, dpb_ref, dbe_ref, dlw_ref), grads[1:]):
            _put_heads(o_ref, vals, head)
        for h in range(hb):
            ds_ref[h] = grads[0][h]

    blk = pl.BlockSpec((C, bw), lambda g, c: (nc - 1 - c, g))
    sq = pl.BlockSpec((1, hb, head, head), lambda g, c: (nc - 1 - c, g, 0, 0))
    return pl.pallas_call(
        body, name="wkv_state_bwd", grid=(nh // hb, nc),
        in_specs=[blk] * 7 + [sq, sq, blk], out_specs=[blk] * 6 + [sq],
        out_shape=[jax.ShapeDtypeStruct((T, RW), f32)] * 6 + [jax.ShapeDtypeStruct((nc, nh, head, head), f32)],
        scratch_shapes=[pltpu.VMEM((hb, head, head), f32)],
        compiler_params=_cparams(2),
    )(W, U, Rt, Pb, Bend, lw, Y0, Z, states, dy)


def _wkv_local_bwd(r, lw, k, v, a, b, cots, d_lw_x, dr_x, dk_x, dv_x, head, T, RW):
    C, nh, hb, bw, nc = _wkv_dims(head, T, RW, WKV_HEADS)

    def body(*refs):
        ins, cot_refs, add_refs, outs = refs[:6], refs[6:13], refs[13:17], refs[17:]
        _, vjp = jax.vjp(_wkv_local, *[_heads(x, hb, head) for x in ins])
        cts = [_heads(x, hb, head) for x in cot_refs[:6]] + [[cot_refs[6][0, h] for h in range(hb)]]
        dr, dlw, dk, dv, da, db = vjp(tuple(cts))
        dlw_x, drx, dkx, dvx = [_heads(x, hb, head) for x in add_refs]
        _put_heads(outs[0], _each(jnp.add, dr, drx), head)
        _put_heads(outs[1], _each(jnp.add, dlw, dlw_x), head)
        _put_heads(outs[2], _each(jnp.add, dk, dkx), head)
        _put_heads(outs[3], _each(jnp.add, dv, dvx), head)
        _put_heads(outs[4], da, head)
        _put_heads(outs[5], db, head)

    blk = pl.BlockSpec((C, bw), lambda g, c: (c, g))
    sq = pl.BlockSpec((1, hb, head, head), lambda g, c: (c, g, 0, 0))
    return pl.pallas_call(
        body, name="wkv_local_bwd", grid=(nh // hb, nc),
        in_specs=[blk] * 12 + [sq] + [blk] * 4, out_specs=[blk] * 6,
        out_shape=[jax.ShapeDtypeStruct((T, RW), f32)] * 6,
        compiler_params=_cparams(2),
    )(r, lw, k, v, a, b, *cots, d_lw_x, dr_x, dk_x, dv_x)


def _rows_tile(R, row_bytes, budget, mult=SUBLANES):
    best = None
    t = mult
    while t <= R:
        if R % t == 0 and t * row_bytes <= budget:
            best = t
        t += mult
    return best if best is not None else R


def _sum_slots(name, parts):
    S, R, W = parts.shape
    budget = 4 << 20
    tr = _rows_tile(R, S * W * 4, budget, 2 * SUBLANES)
    cw = W if tr * S * W * 4 <= 2 * budget else _tile(W, max(LANES, 2 * budget // (S * tr * 4)))

    def body(p_ref, o_ref):
        acc = p_ref[0].astype(f32)
        for d in range(1, S):
            acc = acc + p_ref[d].astype(f32)
        o_ref[...] = acc

    return pl.pallas_call(
        body, name=name, grid=(R // tr, W // cw),
        in_specs=[pl.BlockSpec((S, tr, cw), lambda i, j: (0, i, j))],
        out_specs=pl.BlockSpec((tr, cw), lambda i, j: (i, j)),
        out_shape=jax.ShapeDtypeStruct((R, W), f32),
        compiler_params=_cparams(2),
    )(parts)


def _tile2d(R, W, budget, mult):
    tr = _rows_tile(R, LANES * 4, budget, mult)
    cw = _tile(W, max(LANES, budget // (tr * 4))) if W % LANES == 0 else W
    return tr, cw


def _core_index():
    return lax.axis_index("c").astype(jnp.int32).reshape(1)


def _add_kept(name, shards, got):
    _, _, R, W = shards.shape
    tr, cw = _tile2d(R, W, 2 << 20, 2 * SUBLANES)

    def body(core_ref, a_ref, b_ref, o_ref):
        o_ref[...] = (a_ref[...].astype(f32) + b_ref[...].astype(f32)).astype(o_ref.dtype)

    return pl.pallas_call(
        body, name=name,
        grid_spec=pltpu.PrefetchScalarGridSpec(
            num_scalar_prefetch=1, grid=(4, R // tr, W // cw),
            in_specs=[pl.BlockSpec((None, None, tr, cw), lambda s, i, j, core: (s, core[0], i, j)),
                      pl.BlockSpec((None, tr, cw), lambda s, i, j, core: (s, i, j))],
            out_specs=pl.BlockSpec((None, tr, cw), lambda s, i, j, core: (s, i, j))),
        out_shape=jax.ShapeDtypeStruct((4, R, W), shards.dtype), compiler_params=_cparams(3),
    )(_core_index(), shards, got)


def _sum_slots_half(name, parts):
    S, R, W = parts.shape
    budget = 4 << 20
    tr = _rows_tile(R, S * W * 4, budget, 2 * SUBLANES)
    cw = W if tr * S * W * 4 <= 2 * budget else _tile(W, max(LANES, 2 * budget // (S * tr * 4)))
    nrow = R // tr

    def body(core_ref, p_ref, o_ref):
        acc = p_ref[0].astype(f32)
        for d in range(1, S):
            acc = acc + p_ref[d].astype(f32)
        o_ref[...] = acc

    return pl.pallas_call(
        body, name=name,
        grid_spec=pltpu.PrefetchScalarGridSpec(
            num_scalar_prefetch=1, grid=(nrow, W // cw),
            in_specs=[pl.BlockSpec((S, tr, cw), lambda i, j, core: (0, i, j))],
            out_specs=pl.BlockSpec((tr, cw), lambda i, j, core: (core[0] * nrow + i, j))),
        out_shape=jax.ShapeDtypeStruct((2 * R, W), f32), compiler_params=_cparams(2),
    )(_core_index(), parts)


def _adamw(name, w, g, m, v):
    R, W = w.shape
    tr, cw = _tile2d(R, W, 2 << 20, SUBLANES)

    def body(w_ref, g_ref, m_ref, v_ref, d_ref, nm_ref, nv_ref):
        g_v = g_ref[...]
        nm = ADAM_B1 * m_ref[...] + (1.0 - ADAM_B1) * g_v
        nv = ADAM_B2 * v_ref[...] + (1.0 - ADAM_B2) * (g_v * g_v)
        m_hat = nm / (1.0 - ADAM_B1 ** ADAM_STEP)
        v_hat = nv / (1.0 - ADAM_B2 ** ADAM_STEP)
        d_ref[...] = -ADAM_LR * (m_hat / (jnp.sqrt(v_hat) + ADAM_EPS) + ADAM_WD * w_ref[...])
        nm_ref[...] = nm
        nv_ref[...] = nv

    blk = pl.BlockSpec((tr, cw), lambda i, j: (i, j))
    return pl.pallas_call(
        body, name=name, grid=(R // tr, W // cw),
        in_specs=[blk] * 4, out_specs=[blk] * 3,
        out_shape=[jax.ShapeDtypeStruct((R, W), f32)] * 3,
        compiler_params=_cparams(2),
    )(w, g, m, v)


ANY = pl.BlockSpec(memory_space=pl.ANY)


def _place():
    return lax.axis_index("x"), lax.axis_index("y"), lax.axis_index("c")


class _Comm:
    def __init__(self, operands, out_shape, scratch, start, wait):
        self.operands, self.out_shape, self.scratch, self.start, self.wait = operands, out_shape, scratch, start, wait


def _run_comm(name, comm):
    n = len(comm.operands)

    def body(*refs):
        parts = (refs[:n], refs[n:2 * n], refs[2 * n:])
        comm.start(*parts)
        comm.wait(*parts)

    return pl.pallas_call(
        body, name=name, in_specs=[ANY] * n, out_specs=[ANY] * n, out_shape=comm.out_shape,
        scratch_shapes=comm.scratch,
    )(*comm.operands)


def _copy_chunks(rows, cols):
    k = SHARE_CHUNKS // 2
    if rows % (k * 2 * SUBLANES) == 0:
        return [(pl.ds(q * (rows // k), rows // k), pl.ds(0, cols)) for q in range(k)]
    if cols % (k * LANES) == 0:
        return [(pl.ds(0, rows), pl.ds(q * (cols // k), cols // k)) for q in range(k)]
    return [(pl.ds(0, rows), pl.ds(0, cols))]


def _gather_chips(arrays, relayed=False, copy_own=True):
    n = len(arrays)
    assert not relayed or all(a.shape[2] % (2 * LANES) == 0 for a in arrays)
    parts = [(a, h, blk) for a, arr in enumerate(arrays) for h in range(2) for blk in _copy_chunks(*arr.shape[1:])]
    parts = parts if copy_own else []

    def copies(ins, outs, sems):
        send_sems, recv_sems, local_sems = sems
        x, y, c = _place()
        mine = 2 * x + y
        sib = (x, y, 1 - c)
        chips = [(1 - x, y), (x, 1 - y), (1 - x, 1 - y)]

        def local():
            return [pltpu.make_async_copy(ins[a].at[(h, *blk)], outs[a].at[(mine, h, *blk)], local_sems.at[p])
                    for p, (a, h, blk) in enumerate(parts)]

        def over_ici(a, j, slot):
            px, py = chips[j]
            return pltpu.make_async_remote_copy(
                src_ref=ins[a].at[c], dst_ref=outs[a].at[slot, c], send_sem=send_sems.at[3 * a + j],
                recv_sem=recv_sems.at[3 * a + j], device_id=(px, py, c), device_id_type=MESH)

        def over_d2d(a, j, half):
            px, py = chips[j]
            slot = 2 * px + py
            return pltpu.make_async_remote_copy(
                src_ref=outs[a].at[slot, half], dst_ref=outs[a].at[slot, half], send_sem=send_sems.at[3 * (n + a) + j],
                recv_sem=recv_sems.at[3 * (n + a) + j], device_id=sib, device_id_type=MESH)

        def relay(a, q, origin):
            ox, oy = origin
            px, py = chips[1 - q]
            rows, cols = arrays[a].shape[1], arrays[a].shape[2] // 2
            win = outs[a].at[2 * ox + oy, c, pl.ds(0, rows), pl.ds(q * cols, cols)]
            return pltpu.make_async_remote_copy(
                src_ref=win, dst_ref=win, send_sem=send_sems.at[6 * n + 2 * a + q],
                recv_sem=recv_sems.at[6 * n + 2 * a + q], device_id=(px, py, c), device_id_type=MESH)

        direct = 2 if relayed else 3
        pairs = [(a, j) for a in range(n) for j in range(direct)]
        return dict(local=local,
                    sends=lambda: [over_ici(a, j, mine) for a, j in pairs],
                    landing=lambda: [over_ici(a, j, 2 * chips[j][0] + chips[j][1]) for a, j in pairs],
                    passed=lambda: [over_d2d(a, j, c) for a, j in pairs],
                    relays=lambda: [relay(a, j, chips[j]) for a, j in pairs],
                    relayed_in=lambda: [relay(a, q, chips[2]) for a in range(n) for q in range(2)],
                    passed_diag=lambda: [over_d2d(a, 2, c) for a in range(n)],
                    from_sib=lambda: [over_d2d(a, j, 1 - c) for a in range(n) for j in range(3)])

    def start(ins, outs, sems):
        cps = copies(ins, outs, sems)
        for cp in cps["local"]() + cps["sends"]():
            cp.start()

    def wait(ins, outs, sems):
        cps = copies(ins, outs, sems)
        passed = cps["passed"]()
        relays = cps["relays"]() if relayed else [None] * len(passed)
        for got, on, via in zip(cps["landing"](), passed, relays):
            got.wait_recv()
            if relayed:
                via.start()
            on.start()
        if relayed:
            for cp in cps["relayed_in"]():
                cp.wait_recv()
            diag = cps["passed_diag"]()
            for cp in diag:
                cp.start()
            passed = passed + diag + relays
        for cp in cps["from_sib"]():
            cp.wait_recv()
        for cp in cps["sends"]() + passed:
            cp.wait_send()
        for cp in cps["local"]():
            cp.wait()

    return _Comm(arrays, [jax.ShapeDtypeStruct((4,) + a.shape, a.dtype) for a in arrays],
                 [pltpu.SemaphoreType.DMA((8 * n,)), pltpu.SemaphoreType.DMA((8 * n,)),
                  pltpu.SemaphoreType.DMA((max(len(parts), 1),))], start, wait)


def _exchange_chips(pieces, whole):
    n, m = len(pieces), len(whole)
    parts = [(a, blk) for a, arr in enumerate(pieces) for blk in _copy_chunks(*arr.shape[1:])]

    def copies(ins, outs, sems):
        send_sems, recv_sems, local_sems = sems
        x, y, c = _place()
        chip, dev = 2 * x + y, 4 * x + 2 * y + c
        chips = [(1 - x, y), (x, 1 - y), (1 - x, 1 - y)]
        peers = [(x ^ (k >> 2), y ^ ((k >> 1) & 1), c ^ (k & 1)) for k in range(1, 8)]
        def local():
            cps = [pltpu.make_async_copy(ins[a].at[(chip, *blk)], outs[a].at[(chip, *blk)], local_sems.at[p])
                   for p, (a, blk) in enumerate(parts)]
            return cps + [pltpu.make_async_copy(ins[n + b], outs[n + b].at[dev], local_sems.at[len(parts) + b])
                          for b in range(m)]

        def piece(a, j, slot_from):
            px, py = chips[j]
            return pltpu.make_async_remote_copy(
                src_ref=ins[a].at[2 * px + py], dst_ref=outs[a].at[slot_from], send_sem=send_sems.at[3 * a + j],
                recv_sem=recv_sems.at[3 * a + j], device_id=(px, py, c), device_id_type=MESH)

        def everyone(b, j, slot_from):
            px, py, pc = peers[j]
            return pltpu.make_async_remote_copy(
                src_ref=ins[n + b], dst_ref=outs[n + b].at[slot_from], send_sem=send_sems.at[3 * n + 7 * b + j],
                recv_sem=recv_sems.at[3 * n + 7 * b + j], device_id=(px, py, pc), device_id_type=MESH)

        def sends():
            return ([everyone(b, j, dev) for b in range(m) for j in range(7)]
                    + [piece(a, j, chip) for a in range(n) for j in range(3)])

        def landing():
            return ([everyone(b, j, 4 * px + 2 * py + pc) for b in range(m) for j, (px, py, pc) in enumerate(peers)]
                    + [piece(a, j, 2 * px + py) for a in range(n) for j, (px, py) in enumerate(chips)])

        return local, sends, landing

    def start(ins, outs, sems):
        local, sends, _ = copies(ins, outs, sems)
        for cp in local() + sends():
            cp.start()

    def wait(ins, outs, sems):
        local, sends, landing = copies(ins, outs, sems)
        for cp in landing():
            cp.wait_recv()
        for cp in sends():
            cp.wait_send()
        for cp in local():
            cp.wait()

    shapes = [jax.ShapeDtypeStruct(a.shape, a.dtype) for a in pieces]
    shapes += [jax.ShapeDtypeStruct((8,) + a.shape, a.dtype) for a in whole]
    nsem = 3 * n + 7 * m
    return _Comm(list(pieces) + list(whole), shapes,
                 [pltpu.SemaphoreType.DMA((nsem,)), pltpu.SemaphoreType.DMA((nsem,)),
                  pltpu.SemaphoreType.DMA((len(parts) + m,))], start, wait)


def _pair_exchange(name, shards):
    n = len(shards)

    def body(*refs):
        ins, outs = refs[:n], refs[n:2 * n]
        send_sems, recv_sems = refs[2 * n:]
        x, y, c = _place()
        copies = [pltpu.make_async_remote_copy(
            src_ref=ins[a].at[s, 1 - c], dst_ref=outs[a].at[s], send_sem=send_sems.at[4 * a + s],
            recv_sem=recv_sems.at[4 * a + s], device_id=(x, y, 1 - c), device_id_type=MESH)
            for a in range(n) for s in range(4)]
        for cp in copies:
            cp.start()
        for cp in copies:
            cp.wait_recv()
        for cp in copies:
            cp.wait_send()

    return pl.pallas_call(
        body, name=name, in_specs=[ANY] * n, out_specs=[ANY] * n,
        out_shape=[jax.ShapeDtypeStruct((4,) + a.shape[2:], a.dtype) for a in shards],
        scratch_shapes=[pltpu.SemaphoreType.DMA((4 * n,)), pltpu.SemaphoreType.DMA((4 * n,))],
    )(*shards)


def _share_halves(name, arrays):
    n = len(arrays)

    def body(*refs):
        bufs = refs[n:2 * n]
        send_sems, recv_sems = refs[2 * n:]
        x, y, c = _place()

        def half(a, h):
            rows = arrays[a].shape[0] // 2
            return bufs[a].at[pl.ds(pl.multiple_of(h * rows, SUBLANES), rows)]

        def copy(a, h):
            return pltpu.make_async_remote_copy(
                src_ref=half(a, h), dst_ref=half(a, h), send_sem=send_sems.at[a], recv_sem=recv_sems.at[a],
                device_id=(x, y, 1 - c), device_id_type=MESH)

        sends = [copy(a, c) for a in range(n)]
        for cp in sends:
            cp.start()
        for a in range(n):
            copy(a, 1 - c).wait_recv()
        for cp in sends:
            cp.wait_send()

    return pl.pallas_call(
        body, name=name, in_specs=[ANY] * n, out_specs=[ANY] * n,
        out_shape=[jax.ShapeDtypeStruct(a.shape, a.dtype) for a in arrays],
        input_output_aliases={a: a for a in range(n)},
        scratch_shapes=[pltpu.SemaphoreType.DMA((n,)), pltpu.SemaphoreType.DMA((n,))],
    )(*arrays)


def _place_blocks(blocks, axis):
    shape = list(blocks[0].shape)
    shape[axis] = sum(b.shape[axis] for b in blocks)
    buf = lax.empty(tuple(shape), blocks[0].dtype)
    at = 0
    for b in blocks:
        buf = lax.dynamic_update_slice_in_dim(buf, b, at, axis)
        at += b.shape[axis]
    return buf


def kernel(x, norm_pre_g, w_in, mu_shift, w0, w_lora_up, a0, a_lora_up, k_k, k_a, r_k, lnx_g, lnx_b, conv_w, conv_b, cln_g, cln_b, w_pw2, b_pw2, w_out, norm_post_g, loss_target, m_norm_pre_g, m_w_in, m_mu_shift, m_w0, m_w_lora_up, m_a0, m_a_lora_up, m_k_k, m_k_a, m_r_k, m_lnx_g, m_lnx_b, m_conv_w, m_conv_b, m_cln_g, m_cln_b, m_w_pw2, m_b_pw2, m_w_out, m_norm_post_g, v_norm_pre_g, v_w_in, v_mu_shift, v_w0, v_w_lora_up, v_a0, v_a_lora_up, v_k_k, v_k_a, v_r_k, v_lnx_g, v_lnx_b, v_conv_w, v_conv_b, v_cln_g, v_cln_b, v_w_pw2, v_b_pw2, v_w_out, v_norm_post_g):
    _, T, D = x.shape
    RW = w0.shape[0]
    CW = conv_b.shape[0]
    head = r_k.shape[1]
    lora = w_lora_up.shape[0]
    ktaps = conv_w.shape[0]
    assert RW == CW and 2 * lora <= LORA_PAD and ktaps - 1 <= CONV_HALO
    n_in = 3 * RW + 2 * lora + RW + 3 * CW
    shard = n_in // 4
    PW = 7 * RW + LORA_PAD
    off_l = 7 * RW
    tm = min(256, T // 2)
    tm_wide = min(128, T // 2)
    tm_halo = min(512, T // 2)
    row = lambda vec: vec.reshape(1, -1)
    x2, tgt2 = x[0], loss_target[0]

    halves = lambda a: a.reshape(2, a.shape[0] // 2, a.shape[1])
    conv_w_p = jnp.concatenate([conv_w, jnp.zeros((CONV_HALO - ktaps, CW // 4), f32)], axis=0)
    w_in_t, m_w_in_t, v_w_in_t = w_in.T, m_w_in.T, v_w_in.T
    own_win = halves(w_in_t.astype(bf16))
    (g_win,) = _run_comm("gather_w_in", _gather_chips([own_win], relayed=True, copy_own=False))
    g_win = lax.dynamic_update_slice(g_win, own_win[None], (2 * lax.axis_index("x") + lax.axis_index("y"), 0, 0, 0))
    win_t = g_win.reshape(n_in, D)
    lo = 3 * RW
    wp_t = _place_blocks([win_t[:lo], win_t[lo + 2 * lora:], win_t[lo:lo + 2 * lora],
                          jnp.zeros((LORA_PAD - 2 * lora, D), bf16)], axis=0)
    npg = row(norm_pre_g)
    (h,) = _row_fwd("rms_pre", _fn_rms_pre, [(npg, False)], [(x2, 0, D, False)], [(D, bf16)], T, tm)
    others = [halves(a) for a in (w_lora_up, a_lora_up, conv_w_p, w_pw2.astype(bf16), w_out.astype(bf16))]
    proj, (g_wup, g_aup, g_cw, g_pw2, g_wout) = _matmul("proj", h, wp_t, "nt", f32, comm=_gather_chips(others))
    cat_cols = lambda g: jnp.concatenate([g[s].reshape(-1, g.shape[-1]) for s in range(4)], axis=1)
    wup_full, aup_full, cw_p = cat_cols(g_wup), cat_cols(g_aup), cat_cols(g_cw)
    zl = lambda n: jnp.zeros((n, RW), f32)
    wup_p = jnp.concatenate([wup_full, zl(LORA_PAD - lora)], axis=0)
    aup_p = jnp.concatenate([zl(lora), aup_full, zl(LORA_PAD - 2 * lora)], axis=0)
    pw2_full = g_pw2.reshape(CW, CW)
    wout_full = g_wout.reshape(RW + CW, D)
    mu_r, mu_k, mu_v = (row(mu_shift[s * RW:(s + 1) * RW]) for s in range(3))
    mu_l = row(jnp.concatenate([mu_shift[3 * RW:], jnp.zeros((LORA_PAD - 2 * lora,), f32)]))

    xs_r = _shift_fwd("shift_r", proj, 0, RW, mu_r, T, tm_halo)
    xs_k = _shift_fwd("shift_k", proj, RW, RW, mu_k, T, tm_halo)
    xs_v = _shift_fwd("shift_v", proj, 2 * RW, RW, mu_v, T, tm_halo)
    xs_l = _shift_fwd("shift_l", proj, off_l, LORA_PAD, mu_l, T, tm_halo)
    lora_params = [(row(w0), False), (wup_p, False), (row(a0), False), (aup_p, False)]
    qw, qa = _row_fwd("lora_up", _fn_lora, lora_params, [(xs_l, 0, LORA_PAD, False)], [(RW, f32), (RW, f32)], T, tm)
    ncol = RW // _tile(RW, 512)
    fn_pre = functools.partial(_fn_rwkv_pre, head)
    pre_params = [(row(k_k), True), (row(k_a), True)]
    pre_rows = [(xs_k, 0, RW, True), (qw, 0, RW, True), (qa, 0, RW, True)]
    lw, k_h, a_rec, b_rec = _row_fwd("rwkv_pre", fn_pre, pre_params, pre_rows, [(RW, f32)] * 4, T, tm_halo, ncol)
    wkv_in = (xs_r, lw, k_h, xs_v, a_rec, b_rec)
    c_w, c_u, c_rt, c_pb, c_y0, c_bend, c_z = _wkv_local_fwd(*wkv_in, head, T, RW)
    wkv_loc = (c_w, c_u, c_rt, c_pb, c_bend, lw, c_y0, c_z)
    y_wkv, states = _wkv_state_fwd(*wkv_loc, head, T, RW)
    fn_post = functools.partial(_fn_rwkv_post, head)
    post_params = [(row(lnx_g), True), (row(lnx_b), True), (r_k.reshape(1, RW), True)]
    post_rows = [(y_wkv, 0, RW, True), (xs_r, 0, RW, True), (k_h, 0, RW, True), (xs_v, 0, RW, True),
                 (proj, 3 * RW, RW, True)]
    (y_rwkv,) = _row_fwd("rwkv_post", fn_post, post_params, post_rows, [(RW, bf16)], T, tm_halo, ncol)

    c_pre = _conv_fwd(proj, 4 * RW, 5 * RW, CW, cw_p, row(conv_b), ktaps, T, tm_halo)
    ln_params = [(row(cln_g), False), (row(cln_b), False)]
    (c_act,) = _row_fwd("conv_ln", _fn_conv_ln, ln_params, [(c_pre, 0, CW, False)], [(CW, bf16)], T, tm)
    c2 = _matmul("pw2", c_act, pw2_full, "nn", f32)
    cpost_params = [(row(b_pw2), True)]
    cpost_rows = [(c2, 0, CW, True), (proj, 6 * RW, CW, True)]
    (y_conv,) = _row_fwd("conv_post", _fn_conv_post, cpost_params, cpost_rows, [(CW, bf16)], T, tm, ncol)

    mix = jnp.concatenate([y_rwkv, y_conv], axis=1)
    out = _matmul("out_proj", mix, wout_full, "nn", f32)
    d_out, gx_res, loss_part, g_npost = _post(out, x2, tgt2, row(norm_post_g), T, D, tm_wide)

    g_wout_full = _matmul("d_w_out", mix, d_out, "tn", bf16)
    d_mix = _matmul("d_mix", d_out, wout_full, "nt", f32)

    d_c2, d_gconv, g_bpw2 = _row_bwd("conv_post_bwd", _fn_conv_post, cpost_params, cpost_rows,
                                      [(d_mix, RW, CW, True)], [bf16, bf16], T, tm, ncol)
    g_pw2_full = _matmul("d_w_pw2", c_act, d_c2, "tn", bf16)
    d_cact = _matmul("d_c_act", d_c2, pw2_full, "nt", f32)
    d_cpre, g_clng, g_clnb = _row_bwd("conv_ln_bwd", _fn_conv_ln, ln_params, [(c_pre, 0, CW, False)],
                                      [(d_cact, 0, CW, False)], [f32], T, tm)
    d_gluv, d_glug, g_cw_p, g_cb = _conv_bwd(proj, 4 * RW, 5 * RW, CW, cw_p, d_cpre, ktaps, T, tm_halo)

    d_y, dr_x, dk_x, dv_x, d_grwkv, g_lnxg, g_lnxb, g_rk = _row_bwd(
        "rwkv_post_bwd", fn_post, post_params, post_rows, [(d_mix, 0, RW, True)], [f32, f32, f32, f32, bf16], T, tm, ncol)
    d_cw, d_cu, d_crt, d_cpb, d_cbend, d_lw_dec, d_cz = _wkv_state_bwd(*wkv_loc, states, d_y, head, T, RW)
    d_xr, d_lw, d_kh, d_xv, d_a, d_b = _wkv_local_bwd(
        *wkv_in, (d_cw, d_cu, d_crt, d_cpb, d_y, d_cbend, d_cz), d_lw_dec, dr_x, dk_x, dv_x, head, T, RW)
    pre_cots = [(d_lw, 0, RW, True), (d_kh, 0, RW, True), (d_a, 0, RW, True), (d_b, 0, RW, True)]
    d_xk, d_qw, d_qa, g_kk, g_ka = _row_bwd("rwkv_pre_bwd", fn_pre, pre_params, pre_rows, pre_cots, [f32, f32, f32],
                                            T, tm, ncol)
    d_xl, g_w0, g_wup_p, g_a0, g_aup_p = _row_bwd("lora_up_bwd", _fn_lora, lora_params, [(xs_l, 0, LORA_PAD, False)],
                                                  [(d_qw, 0, RW, False), (d_qa, 0, RW, False)], [f32], T, tm)
    d_proj = lax.empty((T, PW), bf16)
    for blk, at in ((d_grwkv, 3 * RW), (d_gluv, 4 * RW), (d_glug, 5 * RW), (d_gconv, 6 * RW)):
        d_proj = lax.dynamic_update_slice_in_dim(d_proj, blk, at, 1)
    d_proj, g_mur = _shift_bwd("shift_r_bwd", proj, 0, RW, mu_r, d_xr, T, tm_halo, d_proj)
    d_proj, g_muk = _shift_bwd("shift_k_bwd", proj, RW, RW, mu_k, d_xk, T, tm_halo, d_proj)
    d_proj, g_muv = _shift_bwd("shift_v_bwd", proj, 2 * RW, RW, mu_v, d_xv, T, tm_halo, d_proj)
    d_proj, g_mul = _shift_bwd("shift_l_bwd", proj, off_l, LORA_PAD, mu_l, d_xl, T, tm_halo, d_proj)

    def chip_sums(tag, shards):
        halves4 = [a.reshape(4, 2, a.shape[1] // 2, a.shape[2]) for a in shards]
        got = _pair_exchange("pair_exchange_" + tag, halves4)
        return [_add_kept("chip_sum_%s_%d" % (tag, i), a, g) for i, (a, g) in enumerate(zip(halves4, got))]

    def all_chips(tag, slots):
        return _share_halves("share_" + tag, [_sum_slots_half("sum_%s_%d" % (tag, i), r) for i, r in enumerate(slots)])

    q_early = chip_sums("early", [g_wout_full.reshape(4, (RW + CW) // 4, D), g_pw2_full.reshape(4, CW // 4, CW)])
    g_wp_t, r_early = _matmul("d_w_in", d_proj, h, "tn", bf16, comm=_exchange_chips(q_early, []))
    g_win_t = _place_blocks([g_wp_t[:lo], g_wp_t[off_l:off_l + 2 * lora], g_wp_t[lo:off_l]], axis=0)
    col_shards = lambda a: a.reshape(a.shape[0], 4, a.shape[1] // 4).transpose(1, 0, 2)
    q_late = chip_sums("late", [g_win_t.reshape(4, shard, D), col_shards(g_wup_p[:lora]),
                                col_shards(g_aup_p[lora:2 * lora]), col_shards(g_cw_p)])
    g_mu = jnp.concatenate([g_mur[0], g_muk[0], g_muv[0], g_mul[0, :2 * lora]])
    pad_rows = lambda a, n: jnp.concatenate([a, jnp.zeros((n - a.shape[0], a.shape[1]), f32)], axis=0)
    n_mu = -(-mu_shift.shape[0] // RW)
    small_vecs = [pad_rows(jnp.pad(g_mu, (0, n_mu * RW - g_mu.shape[0])).reshape(n_mu, RW), n_mu),
                  g_w0, g_a0, g_kk, g_ka, g_rk, g_lnxg, g_lnxb, g_cb, g_clng, g_clnb, g_bpw2,
                  g_npost.reshape(D // RW, RW)]
    n_small = sum(a.shape[0] for a in small_vecs)
    n_small_pad = -(-n_small // (2 * SUBLANES)) * (2 * SUBLANES)
    small = pad_rows(jnp.concatenate(small_vecs, axis=0), n_small_pad)

    d_h, r_late = _matmul("d_h", d_proj, wp_t, "nn", bf16, tk_t=2560, comm=_exchange_chips(q_late, [small]))
    grad_x2, g_npre = _rms_pre_bwd(x2, npg, d_h, gx_res, T, D, tm_wide)
    (r_npre,) = _run_comm("exchange_norm_pre", _exchange_chips([], [g_npre.reshape(D // RW, RW)]))
    s_npre = _sum_slots("sum_norm_pre", r_npre)
    s_small = _sum_slots("sum_small", r_late[4])
    grad_w_out, grad_w_pw2 = all_chips("early", r_early)
    grad_w_in, grad_wup, grad_aup, grad_cw = all_chips("late", r_late[:4])

    pos = [0]

    def take(nrows):
        a = s_small[pos[0]:pos[0] + nrows]
        pos[0] += nrows
        return a

    grads = {}
    grads["norm_pre_g"] = s_npre.reshape(D)
    grads["mu_shift"] = take(n_mu).reshape(-1)[:mu_shift.shape[0]]
    for nm in ["w0", "a0", "k_k", "k_a"]:
        grads[nm] = take(1).reshape(RW)
    grads["r_k"] = take(1).reshape(r_k.shape)
    for nm in ["lnx_g", "lnx_b", "conv_b", "cln_g", "cln_b", "b_pw2"]:
        grads[nm] = take(1).reshape(RW)
    grads["norm_post_g"] = take(D // RW).reshape(D)
    grads["w_lora_up"], grads["a_lora_up"], grads["conv_w"] = grad_wup, grad_aup, grad_cw[:ktaps]
    grads["w_in"], grads["w_out"], grads["w_pw2"] = grad_w_in, grad_w_out, grad_w_pw2

    weights = dict(norm_pre_g=norm_pre_g, w_in=w_in, mu_shift=mu_shift, w0=w0, w_lora_up=w_lora_up, a0=a0,
                   a_lora_up=a_lora_up, k_k=k_k, k_a=k_a, r_k=r_k, lnx_g=lnx_g, lnx_b=lnx_b, conv_w=conv_w,
                   conv_b=conv_b, cln_g=cln_g, cln_b=cln_b, w_pw2=w_pw2, b_pw2=b_pw2, w_out=w_out,
                   norm_post_g=norm_post_g)
    ms = dict(norm_pre_g=m_norm_pre_g, w_in=m_w_in, mu_shift=m_mu_shift, w0=m_w0, w_lora_up=m_w_lora_up, a0=m_a0,
              a_lora_up=m_a_lora_up, k_k=m_k_k, k_a=m_k_a, r_k=m_r_k, lnx_g=m_lnx_g, lnx_b=m_lnx_b, conv_w=m_conv_w,
              conv_b=m_conv_b, cln_g=m_cln_g, cln_b=m_cln_b, w_pw2=m_w_pw2, b_pw2=m_b_pw2, w_out=m_w_out,
              norm_post_g=m_norm_post_g)
    vs = dict(norm_pre_g=v_norm_pre_g, w_in=v_w_in, mu_shift=v_mu_shift, w0=v_w0, w_lora_up=v_w_lora_up, a0=v_a0,
              a_lora_up=v_a_lora_up, k_k=v_k_k, k_a=v_k_a, r_k=v_r_k, lnx_g=v_lnx_g, lnx_b=v_lnx_b, conv_w=v_conv_w,
              conv_b=v_conv_b, cln_g=v_cln_g, cln_b=v_cln_b, w_pw2=v_w_pw2, b_pw2=v_b_pw2, w_out=v_w_out,
              norm_post_g=v_norm_post_g)
    names = list(weights)
    big = ["w_in", "w_out", "w_pw2"]
    deltas, new_m, new_v = {}, {}, {}
    d_t, m_t, v_t = _adamw("adamw_w_in", w_in_t, grad_w_in, m_w_in_t, v_w_in_t)
    grads["w_in"], deltas["w_in"], new_m["w_in"], new_v["w_in"] = grad_w_in.T, d_t.T, m_t.T, v_t.T
    for nm in big[1:]:
        deltas[nm], new_m[nm], new_v[nm] = _adamw("adamw_" + nm, weights[nm], grads[nm], ms[nm], vs[nm])
    rest = [nm for nm in names if nm not in big]
    sizes = [weights[nm].size for nm in rest]
    total = sum(sizes)
    width = 4 * LANES
    rows_p = -(-total // (width * SUBLANES)) * SUBLANES

    def pack(d):
        flat = jnp.concatenate([d[nm].reshape(-1) for nm in rest])
        return jnp.pad(flat, (0, rows_p * width - total)).reshape(rows_p, width)

    p_d, p_m, p_v = _adamw("adamw_small", pack(weights), pack(grads), pack(ms), pack(vs))
    o = 0
    for nm, sz in zip(rest, sizes):
        shp = weights[nm].shape
        deltas[nm] = p_d.reshape(-1)[o:o + sz].reshape(shp)
        new_m[nm] = p_m.reshape(-1)[o:o + sz].reshape(shp)
        new_v[nm] = p_v.reshape(-1)[o:o + sz].reshape(shp)
        o += sz

    loss = lax.psum(loss_part[0, 0], ("x", "y", "c"))
    grad_x = grad_x2[None]
    return (loss, grad_x, *[grads[nm] for nm in names], *[deltas[nm] for nm in names],
            *[new_m[nm] for nm in names], *[new_v[nm] for nm in names])
```
